```python
import jax, jax.numpy as jnp
from jax import lax
import numpy as np

D_MODEL = 1024
BATCH = 8
SEQ = 4096
DEPTH = 2

CHUNK = 64
Q_BLOCK = 128
D_FF = 2816
C_CONV = 512
CONV_WIDTH = 31
N_HEADS = 8
NOPE_DIM = 64
ROPE_DIM = 32
QK_DIM = NOPE_DIM + ROPE_DIM
V_DIM = 64
Q_LORA = 256
KV_LORA = 256
ROPE_THETA = 10000.0
EPS = 1e-6
D_IN = 2 * C_CONV + Q_LORA + KV_LORA + ROPE_DIM + 2 * D_MODEL

kernel_name = 'hybrid_conformer_mla_gated'


def _rms_norm(x, g):
    x32 = x.astype(jnp.float32)
    y = x32 * lax.rsqrt(jnp.mean(x32 * x32, axis=-1, keepdims=True) + EPS)
    return (y * g.astype(jnp.float32)).astype(x.dtype)


def _layer_norm(x, g, b):
    x32 = x.astype(jnp.float32)
    mu = jnp.mean(x32, axis=-1, keepdims=True)
    xc = x32 - mu
    y = xc * lax.rsqrt(jnp.mean(xc * xc, axis=-1, keepdims=True) + EPS)
    return (y * g.astype(jnp.float32) + b.astype(jnp.float32)).astype(x.dtype)


def _swiglu(h, w_gate, w_up, w_down):
    return (jax.nn.silu(h @ w_gate) * (h @ w_up)) @ w_down


def _rope(x, cos, sin):
    x1, x2 = jnp.split(x.astype(jnp.float32), 2, axis=-1)
    return jnp.concatenate([x1 * cos - x2 * sin, x2 * cos + x1 * sin], axis=-1).astype(x.dtype)


def _conv_module(u2, conv_w, conv_b, ln_g, ln_b, w_conv_out):
    a, gate = jnp.split(u2, 2, axis=-1)
    u = a * jax.nn.sigmoid(gate)
    u = lax.conv_general_dilated(
        u, conv_w[:, None, :], window_strides=(1,), padding=[(CONV_WIDTH - 1, 0)],
        dimension_numbers=('NWC', 'WIO', 'NWC'), feature_group_count=C_CONV) + conv_b
    u = jax.nn.silu(_layer_norm(u, ln_g, ln_b))
    return u @ w_conv_out


def _chunk_causal_attention(q, k, v):
    b, s, h, dq = q.shape
    nb = s // Q_BLOCK
    scale = dq ** -0.5
    key_chunk = jnp.arange(s) // CHUNK
    q_blocks = q.reshape(b, nb, Q_BLOCK, h, dq).transpose(1, 0, 2, 3, 4)

    def one_block(args):
        qi, i = args
        scores = jnp.einsum('bqhd,bkhd->bhqk', qi, k).astype(jnp.float32) * scale
        q_chunk = (i * Q_BLOCK + jnp.arange(Q_BLOCK)) // CHUNK
        mask = key_chunk[None, :] <= q_chunk[:, None]
        p = jax.nn.softmax(jnp.where(mask, scores, -jnp.inf), axis=-1)
        return jnp.einsum('bhqk,bkhd->bqhd', p.astype(v.dtype), v)

    out = lax.map(one_block, (q_blocks, jnp.arange(nb)))
    return out.transpose(1, 0, 2, 3, 4).reshape(b, s, h, v.shape[-1])


def _mla(c_q, c_kv, k_rope, cos, sin, cq_norm, ckv_norm, w_uq, w_ukv, q_norm, k_norm, w_mla_out):
    b, s, _ = c_q.shape
    q = (_rms_norm(c_q, cq_norm) @ w_uq).reshape(b, s, N_HEADS, QK_DIM)
    kv = (_rms_norm(c_kv, ckv_norm) @ w_ukv).reshape(b, s, N_HEADS, NOPE_DIM + V_DIM)
    k_nope, v = kv[..., :NOPE_DIM], kv[..., NOPE_DIM:]
    k_r = jnp.broadcast_to(k_rope[:, :, None, :], (b, s, N_HEADS, ROPE_DIM))
    k = jnp.concatenate([k_nope, k_r], axis=-1)
    q = _rms_norm(q, q_norm)
    k = _rms_norm(k, k_norm)
    q = jnp.concatenate([q[..., :NOPE_DIM], _rope(q[..., NOPE_DIM:], cos, sin)], axis=-1)
    k = jnp.concatenate([k[..., :NOPE_DIM], _rope(k[..., NOPE_DIM:], cos, sin)], axis=-1)
    o = _chunk_causal_attention(q, k, v)
    return o.reshape(b, s, N_HEADS * V_DIM) @ w_mla_out


def _fwd_setup_inputs(seed: int = 0) -> dict:
    key = jax.random.key(seed)
    ks = jax.random.split(key, 32)

    def nrm(k, shape, fan_in):
        return jax.random.normal(k, shape, jnp.float32) * fan_in ** -0.5

    def gain(k, shape):
        return 1.0 + 0.05 * jax.random.normal(k, shape, jnp.float32)

    def small(k, shape, s):
        return s * jax.random.normal(k, shape, jnp.float32)

    x = jax.random.normal(ks[0], (BATCH, SEQ, D_MODEL), jnp.float32)
    offset = jax.random.randint(ks[1], (BATCH, 1), 0, 1024, dtype=jnp.int32)
    positions = offset + jnp.arange(SEQ, dtype=jnp.int32)[None, :]
    return {
        'x': x,
        'positions': positions,
        'ffn1_norm': gain(ks[2], (DEPTH, D_MODEL)),
        'ffn1_w_gate': nrm(ks[3], (DEPTH, D_MODEL, D_FF), D_MODEL),
        'ffn1_w_up': nrm(ks[4], (DEPTH, D_MODEL, D_FF), D_MODEL),
        'ffn1_w_down': nrm(ks[5], (DEPTH, D_FF, D_MODEL), D_FF),
        'mix_norm': gain(ks[6], (DEPTH, D_MODEL)),
        'w_in': nrm(ks[7], (DEPTH, D_MODEL, D_IN), D_MODEL),
        'gate_bias': small(ks[8], (DEPTH, 2, D_MODEL), 0.1),
        'conv_w': nrm(ks[9], (DEPTH, CONV_WIDTH, C_CONV), CONV_WIDTH),
        'conv_b': small(ks[10], (DEPTH, C_CONV), 0.02),
        'conv_ln_g': gain(ks[11], (DEPTH, C_CONV)),
        'conv_ln_b': small(ks[12], (DEPTH, C_CONV), 0.02),
        'w_conv_out': nrm(ks[13], (DEPTH, C_CONV, D_MODEL), C_CONV),
        'cq_norm': gain(ks[14], (DEPTH, Q_LORA)),
        'ckv_norm': gain(ks[15], (DEPTH, KV_LORA)),
        'w_uq': nrm(ks[16], (DEPTH, Q_LORA, N_HEADS * QK_DIM), Q_LORA),
        'w_ukv': nrm(ks[17], (DEPTH, KV_LORA, N_HEADS * (NOPE_DIM + V_DIM)), KV_LORA),
        'q_norm': gain(ks[18], (DEPTH, QK_DIM)),
        'k_norm': gain(ks[19], (DEPTH, QK_DIM)),
        'w_mla_out': nrm(ks[20], (DEPTH, N_HEADS * V_DIM, D_MODEL), N_HEADS * V_DIM),
        'w_out': nrm(ks[21], (DEPTH, D_MODEL, D_MODEL), D_MODEL),
        'ffn2_norm': gain(ks[22], (DEPTH, D_MODEL)),
        'ffn2_w_gate': nrm(ks[23], (DEPTH, D_MODEL, D_FF), D_MODEL),
        'ffn2_w_up': nrm(ks[24], (DEPTH, D_MODEL, D_FF), D_MODEL),
        'ffn2_w_down': nrm(ks[25], (DEPTH, D_FF, D_MODEL), D_FF),
    }


def _fwd_reference(x, positions, ffn1_norm, ffn1_w_gate, ffn1_w_up, ffn1_w_down, mix_norm, w_in,
              gate_bias, conv_w, conv_b, conv_ln_g, conv_ln_b, w_conv_out, cq_norm, ckv_norm,
              w_uq, w_ukv, q_norm, k_norm, w_mla_out, w_out, ffn2_norm, ffn2_w_gate, ffn2_w_up,
              ffn2_w_down):
    b, s, _ = x.shape
    inv_freq = ROPE_THETA ** (-jnp.arange(0, ROPE_DIM, 2, dtype=jnp.float32) / ROPE_DIM)
    ang = positions.astype(jnp.float32)[..., None] * inv_freq
    cos = jnp.cos(ang)[:, :, None, :]
    sin = jnp.sin(ang)[:, :, None, :]
    o1 = 2 * C_CONV
    o2 = o1 + Q_LORA
    o3 = o2 + KV_LORA
    o4 = o3 + ROPE_DIM
    for l in range(DEPTH):
        x = x + 0.5 * _swiglu(_rms_norm(x, ffn1_norm[l]), ffn1_w_gate[l], ffn1_w_up[l], ffn1_w_down[l])
        h = _rms_norm(x, mix_norm[l])
        proj = h @ w_in[l]
        y_conv = _conv_module(proj[..., :o1], conv_w[l], conv_b[l], conv_ln_g[l], conv_ln_b[l],
                              w_conv_out[l])
        y_mla = _mla(proj[..., o1:o2], proj[..., o2:o3], proj[..., o3:o4], cos, sin,
                     cq_norm[l], ckv_norm[l], w_uq[l], w_ukv[l], q_norm[l], k_norm[l], w_mla_out[l])
        gates = jax.nn.sigmoid(proj[..., o4:].reshape(b, s, 2, D_MODEL) + gate_bias[l])
        y = gates[:, :, 0] * y_conv + gates[:, :, 1] * y_mla
        x = x + y @ w_out[l]
        x = x + 0.5 * _swiglu(_rms_norm(x, ffn2_norm[l]), ffn2_w_gate[l], ffn2_w_up[l], ffn2_w_down[l])
    return x


import jax as _jax
import jax.numpy as _jnp

TWIN_FORMAT = 'train_step'
FWD_PARAMS = ['x', 'positions', 'ffn1_norm', 'ffn1_w_gate', 'ffn1_w_up', 'ffn1_w_down', 'mix_norm', 'w_in', 'gate_bias', 'conv_w', 'conv_b', 'conv_ln_g', 'conv_ln_b', 'w_conv_out', 'cq_norm', 'ckv_norm', 'w_uq', 'w_ukv', 'q_norm', 'k_norm', 'w_mla_out', 'w_out', 'ffn2_norm', 'ffn2_w_gate', 'ffn2_w_up', 'ffn2_w_down']
TWIN_WEIGHTS = ['ffn1_norm', 'ffn1_w_gate', 'ffn1_w_up', 'ffn1_w_down', 'mix_norm', 'w_in', 'gate_bias', 'conv_w', 'conv_b', 'conv_ln_g', 'conv_ln_b', 'w_conv_out', 'cq_norm', 'ckv_norm', 'w_uq', 'w_ukv', 'q_norm', 'k_norm', 'w_mla_out', 'w_out', 'ffn2_norm', 'ffn2_w_gate', 'ffn2_w_up', 'ffn2_w_down']
TWIN_DIFF_INPUT = 'x'
TWIN_INPUTS = ['x', 'positions', 'ffn1_norm', 'ffn1_w_gate', 'ffn1_w_up', 'ffn1_w_down', 'mix_norm', 'w_in', 'gate_bias', 'conv_w', 'conv_b', 'conv_ln_g', 'conv_ln_b', 'w_conv_out', 'cq_norm', 'ckv_norm', 'w_uq', 'w_ukv', 'q_norm', 'k_norm', 'w_mla_out', 'w_out', 'ffn2_norm', 'ffn2_w_gate', 'ffn2_w_up', 'ffn2_w_down', 'loss_target', 'm_ffn1_norm', 'm_ffn1_w_gate', 'm_ffn1_w_up', 'm_ffn1_w_down', 'm_mix_norm', 'm_w_in', 'm_gate_bias', 'm_conv_w', 'm_conv_b', 'm_conv_ln_g', 'm_conv_ln_b', 'm_w_conv_out', 'm_cq_norm', 'm_ckv_norm', 'm_w_uq', 'm_w_ukv', 'm_q_norm', 'm_k_norm', 'm_w_mla_out', 'm_w_out', 'm_ffn2_norm', 'm_ffn2_w_gate', 'm_ffn2_w_up', 'm_ffn2_w_down', 'v_ffn1_norm', 'v_ffn1_w_gate', 'v_ffn1_w_up', 'v_ffn1_w_down', 'v_mix_norm', 'v_w_in', 'v_gate_bias', 'v_conv_w', 'v_conv_b', 'v_conv_ln_g', 'v_conv_ln_b', 'v_w_conv_out', 'v_cq_norm', 'v_ckv_norm', 'v_w_uq', 'v_w_ukv', 'v_q_norm', 'v_k_norm', 'v_w_mla_out', 'v_w_out', 'v_ffn2_norm', 'v_ffn2_w_gate', 'v_ffn2_w_up', 'v_ffn2_w_down']
TWIN_OUTPUTS = ['loss', 'grad_x', 'grad_ffn1_norm', 'grad_ffn1_w_gate', 'grad_ffn1_w_up', 'grad_ffn1_w_down', 'grad_mix_norm', 'grad_w_in', 'grad_gate_bias', 'grad_conv_w', 'grad_conv_b', 'grad_conv_ln_g', 'grad_conv_ln_b', 'grad_w_conv_out', 'grad_cq_norm', 'grad_ckv_norm', 'grad_w_uq', 'grad_w_ukv', 'grad_q_norm', 'grad_k_norm', 'grad_w_mla_out', 'grad_w_out', 'grad_ffn2_norm', 'grad_ffn2_w_gate', 'grad_ffn2_w_up', 'grad_ffn2_w_down', 'delta_ffn1_norm', 'delta_ffn1_w_gate', 'delta_ffn1_w_up', 'delta_ffn1_w_down', 'delta_mix_norm', 'delta_w_in', 'delta_gate_bias', 'delta_conv_w', 'delta_conv_b', 'delta_conv_ln_g', 'delta_conv_ln_b', 'delta_w_conv_out', 'delta_cq_norm', 'delta_ckv_norm', 'delta_w_uq', 'delta_w_ukv', 'delta_q_norm', 'delta_k_norm', 'delta_w_mla_out', 'delta_w_out', 'delta_ffn2_norm', 'delta_ffn2_w_gate', 'delta_ffn2_w_up', 'delta_ffn2_w_down', 'new_m_ffn1_norm', 'new_m_ffn1_w_gate', 'new_m_ffn1_w_up', 'new_m_ffn1_w_down', 'new_m_mix_norm', 'new_m_w_in', 'new_m_gate_bias', 'new_m_conv_w', 'new_m_conv_b', 'new_m_conv_ln_g', 'new_m_conv_ln_b', 'new_m_w_conv_out', 'new_m_cq_norm', 'new_m_ckv_norm', 'new_m_w_uq', 'new_m_w_ukv', 'new_m_q_norm', 'new_m_k_norm', 'new_m_w_mla_out', 'new_m_w_out', 'new_m_ffn2_norm', 'new_m_ffn2_w_gate', 'new_m_ffn2_w_up', 'new_m_ffn2_w_down', 'new_v_ffn1_norm', 'new_v_ffn1_w_gate', 'new_v_ffn1_w_up', 'new_v_ffn1_w_down', 'new_v_mix_norm', 'new_v_w_in', 'new_v_gate_bias', 'new_v_conv_w', 'new_v_conv_b', 'new_v_conv_ln_g', 'new_v_conv_ln_b', 'new_v_w_conv_out', 'new_v_cq_norm', 'new_v_ckv_norm', 'new_v_w_uq', 'new_v_w_ukv', 'new_v_q_norm', 'new_v_k_norm', 'new_v_w_mla_out', 'new_v_w_out', 'new_v_ffn2_norm', 'new_v_ffn2_w_gate', 'new_v_ffn2_w_up', 'new_v_ffn2_w_down']
TWIN_LEAF_KINDS = {'loss': 'loss', 'grad_x': 'grad_x', 'grad_ffn1_norm': 'grad_w', 'grad_ffn1_w_gate': 'grad_w', 'grad_ffn1_w_up': 'grad_w', 'grad_ffn1_w_down': 'grad_w', 'grad_mix_norm': 'grad_w', 'grad_w_in': 'grad_w', 'grad_gate_bias': 'grad_w', 'grad_conv_w': 'grad_w', 'grad_conv_b': 'grad_w', 'grad_conv_ln_g': 'grad_w', 'grad_conv_ln_b': 'grad_w', 'grad_w_conv_out': 'grad_w', 'grad_cq_norm': 'grad_w', 'grad_ckv_norm': 'grad_w', 'grad_w_uq': 'grad_w', 'grad_w_ukv': 'grad_w', 'grad_q_norm': 'grad_w', 'grad_k_norm': 'grad_w', 'grad_w_mla_out': 'grad_w', 'grad_w_out': 'grad_w', 'grad_ffn2_norm': 'grad_w', 'grad_ffn2_w_gate': 'grad_w', 'grad_ffn2_w_up': 'grad_w', 'grad_ffn2_w_down': 'grad_w', 'delta_ffn1_norm': 'delta_w', 'delta_ffn1_w_gate': 'delta_w', 'delta_ffn1_w_up': 'delta_w', 'delta_ffn1_w_down': 'delta_w', 'delta_mix_norm': 'delta_w', 'delta_w_in': 'delta_w', 'delta_gate_bias': 'delta_w', 'delta_conv_w': 'delta_w', 'delta_conv_b': 'delta_w', 'delta_conv_ln_g': 'delta_w', 'delta_conv_ln_b': 'delta_w', 'delta_w_conv_out': 'delta_w', 'delta_cq_norm': 'delta_w', 'delta_ckv_norm': 'delta_w', 'delta_w_uq': 'delta_w', 'delta_w_ukv': 'delta_w', 'delta_q_norm': 'delta_w', 'delta_k_norm': 'delta_w', 'delta_w_mla_out': 'delta_w', 'delta_w_out': 'delta_w', 'delta_ffn2_norm': 'delta_w', 'delta_ffn2_w_gate': 'delta_w', 'delta_ffn2_w_up': 'delta_w', 'delta_ffn2_w_down': 'delta_w', 'new_m_ffn1_norm': 'new_m', 'new_m_ffn1_w_gate': 'new_m', 'new_m_ffn1_w_up': 'new_m', 'new_m_ffn1_w_down': 'new_m', 'new_m_mix_norm': 'new_m', 'new_m_w_in': 'new_m', 'new_m_gate_bias': 'new_m', 'new_m_conv_w': 'new_m', 'new_m_conv_b': 'new_m', 'new_m_conv_ln_g': 'new_m', 'new_m_conv_ln_b': 'new_m', 'new_m_w_conv_out': 'new_m', 'new_m_cq_norm': 'new_m', 'new_m_ckv_norm': 'new_m', 'new_m_w_uq': 'new_m', 'new_m_w_ukv': 'new_m', 'new_m_q_norm': 'new_m', 'new_m_k_norm': 'new_m', 'new_m_w_mla_out': 'new_m', 'new_m_w_out': 'new_m', 'new_m_ffn2_norm': 'new_m', 'new_m_ffn2_w_gate': 'new_m', 'new_m_ffn2_w_up': 'new_m', 'new_m_ffn2_w_down': 'new_m', 'new_v_ffn1_norm': 'new_v', 'new_v_ffn1_w_gate': 'new_v', 'new_v_ffn1_w_up': 'new_v', 'new_v_ffn1_w_down': 'new_v', 'new_v_mix_norm': 'new_v', 'new_v_w_in': 'new_v', 'new_v_gate_bias': 'new_v', 'new_v_conv_w': 'new_v', 'new_v_conv_b': 'new_v', 'new_v_conv_ln_g': 'new_v', 'new_v_conv_ln_b': 'new_v', 'new_v_w_conv_out': 'new_v', 'new_v_cq_norm': 'new_v', 'new_v_ckv_norm': 'new_v', 'new_v_w_uq': 'new_v', 'new_v_w_ukv': 'new_v', 'new_v_q_norm': 'new_v', 'new_v_k_norm': 'new_v', 'new_v_w_mla_out': 'new_v', 'new_v_w_out': 'new_v', 'new_v_ffn2_norm': 'new_v', 'new_v_ffn2_w_gate': 'new_v', 'new_v_ffn2_w_up': 'new_v', 'new_v_ffn2_w_down': 'new_v'}


def _forward(args):
    return _fwd_reference(*[args[k] for k in FWD_PARAMS])


def _output_shape():
    out = _jax.eval_shape(lambda: _forward(_fwd_setup_inputs(0)))
    return out.shape, out.dtype

N_MICROBATCH = 1
ADAM_LR = 0.001
ADAM_B1 = 0.9
ADAM_B2 = 0.999
ADAM_EPS = 1e-08
ADAM_WD = 0.01
ADAM_STEP = 10
PER_EXAMPLE_BATCH_AXIS = {'x': 0, 'positions': 0, 'loss_target': 0}
SHARED_INPUTS = []
_WEIGHT_DTYPES = {'ffn1_norm': _jnp.float32, 'ffn1_w_gate': _jnp.float32, 'ffn1_w_up': _jnp.float32, 'ffn1_w_down': _jnp.float32, 'mix_norm': _jnp.float32, 'w_in': _jnp.float32, 'gate_bias': _jnp.float32, 'conv_w': _jnp.float32, 'conv_b': _jnp.float32, 'conv_ln_g': _jnp.float32, 'conv_ln_b': _jnp.float32, 'w_conv_out': _jnp.float32, 'cq_norm': _jnp.float32, 'ckv_norm': _jnp.float32, 'w_uq': _jnp.float32, 'w_ukv': _jnp.float32, 'q_norm': _jnp.float32, 'k_norm': _jnp.float32, 'w_mla_out': _jnp.float32, 'w_out': _jnp.float32, 'ffn2_norm': _jnp.float32, 'ffn2_w_gate': _jnp.float32, 'ffn2_w_up': _jnp.float32, 'ffn2_w_down': _jnp.float32}
MOMENT_SCALE = {'ffn1_norm': 6.177526e+00, 'ffn1_w_gate': 8.222684e-02, 'ffn1_w_up': 8.432580e-02, 'ffn1_w_down': 1.372764e-01, 'mix_norm': 3.540393e-01, 'w_in': 1.263997e-01, 'gate_bias': 9.014037e-01, 'conv_w': 3.272224e-01, 'conv_b': 5.267173e+00, 'conv_ln_g': 8.752453e+00, 'conv_ln_b': 6.259829e+00, 'w_conv_out': 7.453753e-01, 'cq_norm': 5.685648e-02, 'ckv_norm': 5.137794e-01, 'w_uq': 3.406346e-02, 'w_ukv': 1.719687e-01, 'q_norm': 3.889709e-01, 'k_norm': 3.963572e-01, 'w_mla_out': 1.813337e-01, 'w_out': 6.194587e-01, 'ffn2_norm': 6.061503e+00, 'ffn2_w_gate': 8.865124e-02, 'ffn2_w_up': 8.373110e-02, 'ffn2_w_down': 1.360868e-01}


def _to_microbatches(a, axis):
    t = _jnp.moveaxis(a, axis, 0)
    t = t.reshape((N_MICROBATCH, t.shape[0] // N_MICROBATCH) + t.shape[1:])
    return _jnp.moveaxis(t, 1, axis + 1)


def setup_inputs(seed: int = 0) -> dict:
    inp = _fwd_setup_inputs(seed)
    key = _jax.random.fold_in(_jax.random.key(seed), 7919)
    shape, _ = _output_shape()
    out = dict(inp)
    out["loss_target"] = _jax.random.normal(_jax.random.fold_in(key, 0), shape, _jnp.float32)
    for i, name in enumerate(TWIN_WEIGHTS):
        w = inp[name].astype(_jnp.float32)
        if MOMENT_SCALE is None:
            s = _jnp.sqrt(_jnp.mean(_jnp.square(w)) + 1e-30)
        else:
            s = MOMENT_SCALE[name]
        km, kv = _jax.random.split(_jax.random.fold_in(key, i + 1))
        out[name] = w
        out["m_" + name] = s * _jax.random.normal(km, w.shape, _jnp.float32)
        out["v_" + name] = (s * s) * _jax.random.uniform(kv, w.shape, _jnp.float32, 0.5, 1.5)
    if N_MICROBATCH > 1:
        for name, axis in PER_EXAMPLE_BATCH_AXIS.items():
            out[name] = _to_microbatches(out[name], axis)
    return {'x': out['x'], 'positions': out['positions'], 'ffn1_norm': out['ffn1_norm'], 'ffn1_w_gate': out['ffn1_w_gate'], 'ffn1_w_up': out['ffn1_w_up'], 'ffn1_w_down': out['ffn1_w_down'], 'mix_norm': out['mix_norm'], 'w_in': out['w_in'], 'gate_bias': out['gate_bias'], 'conv_w': out['conv_w'], 'conv_b': out['conv_b'], 'conv_ln_g': out['conv_ln_g'], 'conv_ln_b': out['conv_ln_b'], 'w_conv_out': out['w_conv_out'], 'cq_norm': out['cq_norm'], 'ckv_norm': out['ckv_norm'], 'w_uq': out['w_uq'], 'w_ukv': out['w_ukv'], 'q_norm': out['q_norm'], 'k_norm': out['k_norm'], 'w_mla_out': out['w_mla_out'], 'w_out': out['w_out'], 'ffn2_norm': out['ffn2_norm'], 'ffn2_w_gate': out['ffn2_w_gate'], 'ffn2_w_up': out['ffn2_w_up'], 'ffn2_w_down': out['ffn2_w_down'], 'loss_target': out['loss_target'], 'm_ffn1_norm': out['m_ffn1_norm'], 'm_ffn1_w_gate': out['m_ffn1_w_gate'], 'm_ffn1_w_up': out['m_ffn1_w_up'], 'm_ffn1_w_down': out['m_ffn1_w_down'], 'm_mix_norm': out['m_mix_norm'], 'm_w_in': out['m_w_in'], 'm_gate_bias': out['m_gate_bias'], 'm_conv_w': out['m_conv_w'], 'm_conv_b': out['m_conv_b'], 'm_conv_ln_g': out['m_conv_ln_g'], 'm_conv_ln_b': out['m_conv_ln_b'], 'm_w_conv_out': out['m_w_conv_out'], 'm_cq_norm': out['m_cq_norm'], 'm_ckv_norm': out['m_ckv_norm'], 'm_w_uq': out['m_w_uq'], 'm_w_ukv': out['m_w_ukv'], 'm_q_norm': out['m_q_norm'], 'm_k_norm': out['m_k_norm'], 'm_w_mla_out': out['m_w_mla_out'], 'm_w_out': out['m_w_out'], 'm_ffn2_norm': out['m_ffn2_norm'], 'm_ffn2_w_gate': out['m_ffn2_w_gate'], 'm_ffn2_w_up': out['m_ffn2_w_up'], 'm_ffn2_w_down': out['m_ffn2_w_down'], 'v_ffn1_norm': out['v_ffn1_norm'], 'v_ffn1_w_gate': out['v_ffn1_w_gate'], 'v_ffn1_w_up': out['v_ffn1_w_up'], 'v_ffn1_w_down': out['v_ffn1_w_down'], 'v_mix_norm': out['v_mix_norm'], 'v_w_in': out['v_w_in'], 'v_gate_bias': out['v_gate_bias'], 'v_conv_w': out['v_conv_w'], 'v_conv_b': out['v_conv_b'], 'v_conv_ln_g': out['v_conv_ln_g'], 'v_conv_ln_b': out['v_conv_ln_b'], 'v_w_conv_out': out['v_w_conv_out'], 'v_cq_norm': out['v_cq_norm'], 'v_ckv_norm': out['v_ckv_norm'], 'v_w_uq': out['v_w_uq'], 'v_w_ukv': out['v_w_ukv'], 'v_q_norm': out['v_q_norm'], 'v_k_norm': out['v_k_norm'], 'v_w_mla_out': out['v_w_mla_out'], 'v_w_out': out['v_w_out'], 'v_ffn2_norm': out['v_ffn2_norm'], 'v_ffn2_w_gate': out['v_ffn2_w_gate'], 'v_ffn2_w_up': out['v_ffn2_w_up'], 'v_ffn2_w_down': out['v_ffn2_w_down']}


def _loss(weights, diff, rest, loss_target):
    with _jax.named_scope("forward"):
        args = {**rest, TWIN_DIFF_INPUT: diff, **{k: w.astype(_WEIGHT_DTYPES[k]) for k, w in weights.items()}}
        y = _forward(args)
    with _jax.named_scope("loss_head"):
        err = _jnp.square(y.astype(_jnp.float32) - loss_target)
        return 0.5 * _jnp.sum(_jnp.mean(err, axis=-1)) if err.ndim else 0.5 * err


def _adamw(w, g, m, v):
    m = ADAM_B1 * m + (1.0 - ADAM_B1) * g
    v = ADAM_B2 * v + (1.0 - ADAM_B2) * _jnp.square(g)
    m_hat = m / (1.0 - ADAM_B1 ** ADAM_STEP)
    v_hat = v / (1.0 - ADAM_B2 ** ADAM_STEP)
    delta = -ADAM_LR * (m_hat / (_jnp.sqrt(v_hat) + ADAM_EPS) + ADAM_WD * w)
    return delta, m, v


def reference(x, positions, ffn1_norm, ffn1_w_gate, ffn1_w_up, ffn1_w_down, mix_norm, w_in, gate_bias, conv_w, conv_b, conv_ln_g, conv_ln_b, w_conv_out, cq_norm, ckv_norm, w_uq, w_ukv, q_norm, k_norm, w_mla_out, w_out, ffn2_norm, ffn2_w_gate, ffn2_w_up, ffn2_w_down, loss_target, m_ffn1_norm, m_ffn1_w_gate, m_ffn1_w_up, m_ffn1_w_down, m_mix_norm, m_w_in, m_gate_bias, m_conv_w, m_conv_b, m_conv_ln_g, m_conv_ln_b, m_w_conv_out, m_cq_norm, m_ckv_norm, m_w_uq, m_w_ukv, m_q_norm, m_k_norm, m_w_mla_out, m_w_out, m_ffn2_norm, m_ffn2_w_gate, m_ffn2_w_up, m_ffn2_w_down, v_ffn1_norm, v_ffn1_w_gate, v_ffn1_w_up, v_ffn1_w_down, v_mix_norm, v_w_in, v_gate_bias, v_conv_w, v_conv_b, v_conv_ln_g, v_conv_ln_b, v_w_conv_out, v_cq_norm, v_ckv_norm, v_w_uq, v_w_ukv, v_q_norm, v_k_norm, v_w_mla_out, v_w_out, v_ffn2_norm, v_ffn2_w_gate, v_ffn2_w_up, v_ffn2_w_down):
    given = dict(x=x, positions=positions, ffn1_norm=ffn1_norm, ffn1_w_gate=ffn1_w_gate, ffn1_w_up=ffn1_w_up, ffn1_w_down=ffn1_w_down, mix_norm=mix_norm, w_in=w_in, gate_bias=gate_bias, conv_w=conv_w, conv_b=conv_b, conv_ln_g=conv_ln_g, conv_ln_b=conv_ln_b, w_conv_out=w_conv_out, cq_norm=cq_norm, ckv_norm=ckv_norm, w_uq=w_uq, w_ukv=w_ukv, q_norm=q_norm, k_norm=k_norm, w_mla_out=w_mla_out, w_out=w_out, ffn2_norm=ffn2_norm, ffn2_w_gate=ffn2_w_gate, ffn2_w_up=ffn2_w_up, ffn2_w_down=ffn2_w_down, loss_target=loss_target, m_ffn1_norm=m_ffn1_norm, m_ffn1_w_gate=m_ffn1_w_gate, m_ffn1_w_up=m_ffn1_w_up, m_ffn1_w_down=m_ffn1_w_down, m_mix_norm=m_mix_norm, m_w_in=m_w_in, m_gate_bias=m_gate_bias, m_conv_w=m_conv_w, m_conv_b=m_conv_b, m_conv_ln_g=m_conv_ln_g, m_conv_ln_b=m_conv_ln_b, m_w_conv_out=m_w_conv_out, m_cq_norm=m_cq_norm, m_ckv_norm=m_ckv_norm, m_w_uq=m_w_uq, m_w_ukv=m_w_ukv, m_q_norm=m_q_norm, m_k_norm=m_k_norm, m_w_mla_out=m_w_mla_out, m_w_out=m_w_out, m_ffn2_norm=m_ffn2_norm, m_ffn2_w_gate=m_ffn2_w_gate, m_ffn2_w_up=m_ffn2_w_up, m_ffn2_w_down=m_ffn2_w_down, v_ffn1_norm=v_ffn1_norm, v_ffn1_w_gate=v_ffn1_w_gate, v_ffn1_w_up=v_ffn1_w_up, v_ffn1_w_down=v_ffn1_w_down, v_mix_norm=v_mix_norm, v_w_in=v_w_in, v_gate_bias=v_gate_bias, v_conv_w=v_conv_w, v_conv_b=v_conv_b, v_conv_ln_g=v_conv_ln_g, v_conv_ln_b=v_conv_ln_b, v_w_conv_out=v_w_conv_out, v_cq_norm=v_cq_norm, v_ckv_norm=v_ckv_norm, v_w_uq=v_w_uq, v_w_ukv=v_w_ukv, v_q_norm=v_q_norm, v_k_norm=v_k_norm, v_w_mla_out=v_w_mla_out, v_w_out=v_w_out, v_ffn2_norm=v_ffn2_norm, v_ffn2_w_gate=v_ffn2_w_gate, v_ffn2_w_up=v_ffn2_w_up, v_ffn2_w_down=v_ffn2_w_down)
    weights = {n: given[n] for n in TWIN_WEIGHTS}
    shared = {n: given[n] for n in SHARED_INPUTS}
    per_example = {n: given[n] for n in ['x', 'positions']}
    grad_fn = _jax.value_and_grad(_loss, argnums=(0, 1))

    def one_microbatch(ex, loss_target):
        ex = dict(ex)
        diff = ex.pop(TWIN_DIFF_INPUT)
        return grad_fn(weights, diff, {**shared, **ex}, loss_target)

    if N_MICROBATCH == 1:
        loss, (grad_w, grad_x) = one_microbatch(per_example, given["loss_target"])
    else:
        def body(carry, xs):
            loss_sum, grad_sum = carry
            l_k, (gw_k, gx_k) = one_microbatch(xs[0], xs[1])
            with _jax.named_scope("update"):
                return (loss_sum + l_k, _jax.tree.map(_jnp.add, grad_sum, gw_k)), gx_k

        init = (_jnp.zeros((), _jnp.float32), _jax.tree.map(_jnp.zeros_like, weights))
        (loss, grad_w), grad_x = _jax.lax.scan(body, init, (per_example, given["loss_target"]))
    with _jax.named_scope("update"):
        delta_w, new_m, new_v = {}, {}, {}
        for n in TWIN_WEIGHTS:
            delta_w[n], new_m[n], new_v[n] = _adamw(weights[n], grad_w[n], given["m_" + n], given["v_" + n])
    return (loss, grad_x, *[grad_w[n] for n in TWIN_WEIGHTS], *[delta_w[n] for n in TWIN_WEIGHTS],
            *[new_m[n] for n in TWIN_WEIGHTS], *[new_v[n] for n in TWIN_WEIGHTS])
```

```python
import functools

import jax
import jax.numpy as jnp
from jax import lax
from jax.experimental import pallas as pl
from jax.experimental.pallas import tpu as pltpu

F32 = jnp.float32
_MXU_DTYPE = jnp.bfloat16
_WIRE_DTYPE = jnp.bfloat16

_LANES = 128
_SUBLANES = 8
_PACKED_ROWS = 16
_V7X_VMEM_BYTES = 64 * 2 ** 20
_VMEM_HEADROOM = 8 * 2 ** 20

N_DEV = 8
EPS = 1e-6
CHUNK = 64
CONV_WIDTH = 31
N_HEADS = 8
NOPE_DIM = 64
ROPE_DIM = 32
QK_DIM = NOPE_DIM + ROPE_DIM
V_DIM = 64
HEAD_W = _LANES
ROPE_THETA = 10000.0
HALO = 32

ADAM_LR = 0.001
ADAM_B1 = 0.9
ADAM_B2 = 0.999
ADAM_EPS = 1e-08
ADAM_WD = 0.01
ADAM_STEP = 10

_ROW_TILE = 512
_MM_TILE = 512
_MM_TILE_N = 1408
_MM_TILE_K = 1024
_TN_TILE_M = 1408
_ATT_TILE = 256
_PACK_W = 1024
_PACK_ROW_MULT = 256

WEIGHTS = ['ffn1_norm', 'ffn1_w_gate', 'ffn1_w_up', 'ffn1_w_down', 'mix_norm', 'w_in', 'gate_bias',
           'conv_w', 'conv_b', 'conv_ln_g', 'conv_ln_b', 'w_conv_out', 'cq_norm', 'ckv_norm', 'w_uq',
           'w_ukv', 'q_norm', 'k_norm', 'w_mla_out', 'w_out', 'ffn2_norm', 'ffn2_w_gate', 'ffn2_w_up',
           'ffn2_w_down']
BIG = {'ffn1_w_gate': True, 'ffn1_w_up': True, 'ffn1_w_down': False, 'w_in': True, 'w_conv_out': True,
       'w_uq': True, 'w_ukv': True, 'w_mla_out': True, 'w_out': False, 'ffn2_w_gate': True,
       'ffn2_w_up': True, 'ffn2_w_down': False}
SMALL_SHARDED = ['conv_w', 'gate_bias']
REPLICATED = ['ffn1_norm', 'mix_norm', 'conv_b', 'conv_ln_g', 'conv_ln_b', 'cq_norm', 'ckv_norm',
              'q_norm', 'k_norm', 'ffn2_norm']


def _tile(n, target, mult=_LANES):
    if n <= target:
        return n
    for d in range(target - target % mult, 0, -mult):
        if n % d == 0:
            return d
    return n


def _nbytes(shape, dtype):
    n = 1
    for s in shape:
        n *= s
    return n * jnp.dtype(dtype).itemsize


def _params(dims, block_bytes, extra_bytes=0):
    need = 2 * block_bytes + extra_bytes + _VMEM_HEADROOM
    limit = int(min(max(need, 32 * 2 ** 20), _V7X_VMEM_BYTES - 4 * 2 ** 20))
    return pltpu.CompilerParams(dimension_semantics=dims, vmem_limit_bytes=limit)


def _mxu(x, scale=None):
    if scale is not None:
        x = x * scale
    return x if x.dtype == _MXU_DTYPE else x.astype(_MXU_DTYPE)


def _dot(a, b, mode):
    dims = {'nn': (((1,), (0,)), ((), ())), 'nt': (((1,), (1,)), ((), ())), 'tn': (((0,), (0,)), ((), ()))}[mode]
    return lax.dot_general(a, b, dims, preferred_element_type=F32)


def _colsum(x):
    return jnp.sum(x, axis=0, keepdims=True)


def _sig(x):
    return jax.nn.sigmoid(x)


def _mm(name, a, b, mode, *, out_dtype=F32, res=None, scale=None, a_scale=None, tm=None, tn=None, tk=None):
    if mode == 'nn':
        (M, K), (_, N) = a.shape, b.shape
    elif mode == 'nt':
        (M, K), (N, _) = a.shape, b.shape
    else:
        (K, M), (_, N) = a.shape, b.shape
    tm = _tile(M, tm or (_TN_TILE_M if mode == 'tn' else _MM_TILE))
    tn = _tile(N, tn or _MM_TILE_N)
    tk = _tile(K, tk or (_MM_TILE_K if mode == 'tn' else 4096))
    nk = K // tk
    grid = (M // tm, N // tn, nk)
    a_spec = pl.BlockSpec((tk, tm), lambda i, j, k: (k, i)) if mode == 'tn' else pl.BlockSpec((tm, tk), lambda i, j, k: (i, k))
    b_spec = pl.BlockSpec((tn, tk), lambda i, j, k: (j, k)) if mode == 'nt' else pl.BlockSpec((tk, tn), lambda i, j, k: (k, j))
    o_spec = pl.BlockSpec((tm, tn), lambda i, j, k: (i, j))
    has_res = res is not None

    def body(*refs):
        a_ref, b_ref = refs[0], refs[1]
        res_ref = refs[2] if has_res else None
        o_ref = refs[3] if has_res else refs[2]
        acc_ref = refs[-1] if nk > 1 else None

        def finish(p):
            if scale is not None:
                p = p * scale
            if has_res:
                p = res_ref[...] + p
            o_ref[...] = p.astype(o_ref.dtype)

        p = _dot(_mxu(a_ref[...], a_scale), _mxu(b_ref[...]), mode)
        if nk == 1:
            finish(p)
        else:
            k = pl.program_id(2)

            @pl.when(k == 0)
            def _():
                acc_ref[...] = p

            @pl.when(k > 0)
            def _():
                acc_ref[...] += p

            @pl.when(k == nk - 1)
            def _():
                finish(acc_ref[...])

    blocks = (_nbytes((tm, tk), a.dtype) + _nbytes((tk, tn), b.dtype) + _nbytes((tm, tn), out_dtype)
              + (_nbytes((tm, tn), F32) if has_res else 0))
    extra = 3 * _nbytes((tm, tn), F32) + _nbytes((tm, tk), _MXU_DTYPE) + _nbytes((tk, tn), _MXU_DTYPE)
    return pl.pallas_call(
        body, name=name, grid=grid,
        in_specs=[a_spec, b_spec] + ([o_spec] if has_res else []),
        out_specs=o_spec,
        out_shape=jax.ShapeDtypeStruct((M, N), out_dtype),
        scratch_shapes=[pltpu.VMEM((tm, tn), F32)] if nk > 1 else [],
        compiler_params=_params(("parallel", "parallel", "arbitrary"), blocks, extra),
    )(*([a, b] + ([res] if has_res else [])))


def _ffn_up(h, wg, wu):
    T, D = h.shape
    F = wg.shape[0]
    tm, tn = _tile(T, _MM_TILE), _tile(F, _MM_TILE_N)

    def body(h_ref, wg_ref, wu_ref, a_ref, b_ref, u_ref):
        hv = h_ref[...]
        a = _dot(hv, wg_ref[...], 'nt')
        b = _dot(hv, wu_ref[...], 'nt')
        a_ref[...] = a
        b_ref[...] = b
        u_ref[...] = (a * _sig(a) * b).astype(u_ref.dtype)

    w_spec = pl.BlockSpec((tn, D), lambda j, i: (j, 0))
    o_spec = pl.BlockSpec((tm, tn), lambda j, i: (i, j))
    blocks = _nbytes((tm, D), h.dtype) + 2 * _nbytes((tn, D), wg.dtype) + 2 * _nbytes((tm, tn), F32) + _nbytes((tm, tn), _MXU_DTYPE)
    return pl.pallas_call(
        body, name="ffn_up", grid=(F // tn, T // tm),
        in_specs=[pl.BlockSpec((tm, D), lambda j, i: (i, 0)), w_spec, w_spec],
        out_specs=[o_spec, o_spec, o_spec],
        out_shape=[jax.ShapeDtypeStruct((T, F), F32), jax.ShapeDtypeStruct((T, F), F32),
                   jax.ShapeDtypeStruct((T, F), _MXU_DTYPE)],
        compiler_params=_params(("parallel", "parallel"), blocks, 4 * _nbytes((tm, tn), F32)),
    )(h, wg, wu)


def _ffn_bwd_up(dx, wd, a, b):
    T, D = dx.shape
    F = wd.shape[0]
    tm, tn = _tile(T, _MM_TILE), _tile(F, _MM_TILE_N)

    def body(dx_ref, wd_ref, a_ref, b_ref, da_ref, db_ref):
        du = _dot(_mxu(dx_ref[...], 0.5), wd_ref[...], 'nt')
        av, bv = a_ref[...], b_ref[...]
        s = _sig(av)
        db_ref[...] = (du * av * s).astype(db_ref.dtype)
        da_ref[...] = (du * bv * (s * (1.0 + av * (1.0 - s)))).astype(da_ref.dtype)

    t_spec = pl.BlockSpec((tm, tn), lambda j, i: (i, j))
    blocks = _nbytes((tm, D), F32) + _nbytes((tn, D), wd.dtype) + 2 * _nbytes((tm, tn), F32) + 2 * _nbytes((tm, tn), _MXU_DTYPE)
    return pl.pallas_call(
        body, name="ffn_bwd_up", grid=(F // tn, T // tm),
        in_specs=[pl.BlockSpec((tm, D), lambda j, i: (i, 0)), pl.BlockSpec((tn, D), lambda j, i: (j, 0)), t_spec, t_spec],
        out_specs=[t_spec, t_spec],
        out_shape=[jax.ShapeDtypeStruct((T, F), _MXU_DTYPE)] * 2,
        compiler_params=_params(("parallel", "parallel"), blocks, 5 * _nbytes((tm, tn), F32)),
    )(dx, wd, a, b)


def _row_spec(tm, w, cb=0):
    return pl.BlockSpec((tm, w), lambda i: (i, cb))


def _full_spec(shape):
    return pl.BlockSpec(shape, lambda i: (0,) * len(shape))


def _init_acc(refs):
    @pl.when(pl.program_id(0) == 0)
    def _():
        for r in refs:
            r[...] = jnp.zeros_like(r)


def _rms_fwd(x, g):
    T, D = x.shape
    tm = _tile(T, _ROW_TILE, _PACKED_ROWS)

    def body(x_ref, g_ref, h_ref):
        xv = x_ref[...]
        r = lax.rsqrt(jnp.mean(xv * xv, axis=-1, keepdims=True) + EPS)
        h_ref[...] = (xv * r * g_ref[...]).astype(h_ref.dtype)

    return pl.pallas_call(
        body, name="rms_fwd", grid=(T // tm,),
        in_specs=[_row_spec(tm, D), _full_spec((1, D))], out_specs=_row_spec(tm, D),
        out_shape=jax.ShapeDtypeStruct((T, D), _MXU_DTYPE),
        compiler_params=_params(("parallel",), 2 * _nbytes((tm, D), F32), 4 * _nbytes((tm, D), F32)),
    )(x, g)


def _rms_bwd(x, g, dh, dres):
    T, D = x.shape
    tm = _tile(T, _ROW_TILE, _SUBLANES)

    def body(x_ref, g_ref, dh_ref, dres_ref, dx_ref, dg_ref):
        _init_acc([dg_ref])
        xv, dhv = x_ref[...], dh_ref[...]
        r = lax.rsqrt(jnp.mean(xv * xv, axis=-1, keepdims=True) + EPS)
        xh = xv * r
        dxh = dhv * g_ref[...]
        dx_ref[...] = dres_ref[...] + r * (dxh - xh * jnp.mean(dxh * xh, axis=-1, keepdims=True))
        dg_ref[...] += _colsum(dhv * xh)

    return pl.pallas_call(
        body, name="rms_bwd", grid=(T // tm,),
        in_specs=[_row_spec(tm, D), _full_spec((1, D)), _row_spec(tm, D), _row_spec(tm, D)],
        out_specs=[_row_spec(tm, D), _full_spec((1, D))],
        out_shape=[jax.ShapeDtypeStruct((T, D), F32), jax.ShapeDtypeStruct((1, D), F32)],
        compiler_params=_params(("arbitrary",), 4 * _nbytes((tm, D), F32), 6 * _nbytes((tm, D), F32)),
    )(x, g, dh, dres)


def _loss_grad(y, target):
    T, D = y.shape
    tm = _tile(T, _ROW_TILE, _SUBLANES)

    def body(y_ref, t_ref, dy_ref, l_ref):
        _init_acc([l_ref])
        err = y_ref[...] - t_ref[...]
        dy_ref[...] = err * (1.0 / D)
        per_row = jnp.sum(err * err, axis=-1, keepdims=True) * (0.5 / D)
        l_ref[...] += jnp.sum(per_row, axis=0, keepdims=True)

    return pl.pallas_call(
        body, name="loss_grad", grid=(T // tm,),
        in_specs=[_row_spec(tm, D), _row_spec(tm, D)],
        out_specs=[_row_spec(tm, D), _full_spec((1, _LANES))],
        out_shape=[jax.ShapeDtypeStruct((T, D), F32), jax.ShapeDtypeStruct((1, _LANES), F32)],
        compiler_params=_params(("arbitrary",), 3 * _nbytes((tm, D), F32), 3 * _nbytes((tm, D), F32)),
    )(y, target)


def _glu(pc, C):
    a, gate = pc[:, :C], pc[:, C:]
    s = _sig(gate)
    return a, s, a * s


def _conv_fwd(pc, w, cb, lg, lb):
    T, C2 = pc.shape
    C = C2 // 2
    tm = _tile(T, _ROW_TILE // 2, HALO)
    per = tm // HALO

    def body(pc_ref, ph_ref, w_ref, cb_ref, lg_ref, lb_ref, z_ref, c_ref, ubuf):
        i = pl.program_id(0)
        _, _, u_cur = _glu(pc_ref[...], C)
        _, _, u_prev = _glu(ph_ref[...], C)
        ubuf[0:HALO, :] = jnp.where(i > 0, u_prev, 0.0)
        ubuf[HALO:HALO + tm, :] = u_cur
        acc = jnp.zeros((tm, C), F32)
        for k in range(CONV_WIDTH):
            off = HALO - (CONV_WIDTH - 1) + k
            acc = acc + w_ref[k:k + 1, :] * ubuf[off:off + tm, :]
        z = acc + cb_ref[...]
        z_ref[...] = z
        zc = z - jnp.mean(z, axis=-1, keepdims=True)
        y = zc * lax.rsqrt(jnp.mean(zc * zc, axis=-1, keepdims=True) + EPS) * lg_ref[...] + lb_ref[...]
        c_ref[...] = (y * _sig(y)).astype(c_ref.dtype)

    return pl.pallas_call(
        body, name="conv_fwd", grid=(T // tm,),
        in_specs=[_row_spec(tm, C2), pl.BlockSpec((HALO, C2), lambda i: (jnp.maximum(i * per - 1, 0), 0)),
                  _full_spec((HALO, C)), _full_spec((1, C)), _full_spec((1, C)), _full_spec((1, C))],
        out_specs=[_row_spec(tm, C), _row_spec(tm, C)],
        out_shape=[jax.ShapeDtypeStruct((T, C), F32), jax.ShapeDtypeStruct((T, C), _MXU_DTYPE)],
        scratch_shapes=[pltpu.VMEM((tm + HALO, C), F32)],
        compiler_params=_params(("parallel",), 3 * _nbytes((tm, C2), F32), 8 * _nbytes((tm, C), F32)),
    )(pc, pc, w, cb, lg, lb)


def _conv_bwd(pc, z, dc, w, lg, lb):
    T, C2 = pc.shape
    C = C2 // 2
    tm = _tile(T, _ROW_TILE // 2, HALO)
    per = tm // HALO
    n = T // tm
    last_halo = T // HALO - 1

    def body(pc_ref, ph_ref, z_ref, zn_ref, dc_ref, dcn_ref, w_ref, lg_ref, lb_ref,
             dpc_ref, dw_ref, dcb_ref, dlg_ref, dlb_ref, ubuf, dzbuf):
        i = pl.program_id(0)
        _init_acc([dw_ref, dcb_ref, dlg_ref, dlb_ref])
        g, b = lg_ref[...], lb_ref[...]

        def ln_swish_bwd(zv, dcv):
            zc = zv - jnp.mean(zv, axis=-1, keepdims=True)
            r = lax.rsqrt(jnp.mean(zc * zc, axis=-1, keepdims=True) + EPS)
            zh = zc * r
            y = zh * g + b
            s = _sig(y)
            dy = dcv * (s * (1.0 + y * (1.0 - s)))
            dyg = dy * g
            dz = r * (dyg - jnp.mean(dyg, axis=-1, keepdims=True) - zh * jnp.mean(dyg * zh, axis=-1, keepdims=True))
            return dz, dy, zh

        dz_c, dy_c, zh_c = ln_swish_bwd(z_ref[...], dc_ref[...])
        dz_n, _, _ = ln_swish_bwd(zn_ref[...], dcn_ref[...])
        dlg_ref[...] += _colsum(dy_c * zh_c)
        dlb_ref[...] += _colsum(dy_c)
        dcb_ref[...] += _colsum(dz_c)
        dzbuf[0:tm, :] = dz_c
        dzbuf[tm:tm + HALO, :] = jnp.where(i < n - 1, dz_n, 0.0)
        a, s, u_cur = _glu(pc_ref[...], C)
        _, _, u_prev = _glu(ph_ref[...], C)
        ubuf[0:HALO, :] = jnp.where(i > 0, u_prev, 0.0)
        ubuf[HALO:HALO + tm, :] = u_cur
        du = jnp.zeros((tm, C), F32)
        for k in range(CONV_WIDTH):
            back = CONV_WIDTH - 1 - k
            du = du + w_ref[k:k + 1, :] * dzbuf[back:back + tm, :]
            off = HALO - (CONV_WIDTH - 1) + k
            dw_ref[k:k + 1, :] += _colsum(dz_c * ubuf[off:off + tm, :])
        dpc_ref[:, :C] = (du * s).astype(dpc_ref.dtype)
        dpc_ref[:, C:] = (du * a * s * (1.0 - s)).astype(dpc_ref.dtype)

    nxt = lambda i: (jnp.minimum((i + 1) * per, last_halo), 0)
    vec = _full_spec((1, C))
    return pl.pallas_call(
        body, name="conv_bwd", grid=(n,),
        in_specs=[_row_spec(tm, C2), pl.BlockSpec((HALO, C2), lambda i: (jnp.maximum(i * per - 1, 0), 0)),
                  _row_spec(tm, C), pl.BlockSpec((HALO, C), nxt), _row_spec(tm, C), pl.BlockSpec((HALO, C), nxt),
                  _full_spec((HALO, C)), vec, vec],
        out_specs=[_row_spec(tm, C2), _full_spec((HALO, C)), vec, vec, vec],
        out_shape=[jax.ShapeDtypeStruct((T, C2), _MXU_DTYPE), jax.ShapeDtypeStruct((HALO, C), F32)]
                  + [jax.ShapeDtypeStruct((1, C), F32)] * 3,
        scratch_shapes=[pltpu.VMEM((tm + HALO, C), F32), pltpu.VMEM((tm + HALO, C), F32)],
        compiler_params=_params(("arbitrary",), 5 * _nbytes((tm, C2), F32), 12 * _nbytes((tm, C), F32)),
    )(pc, pc, z, z, dc, dc, w, lg, lb)


def _lat_norm_fwd(pl_, gq, gkv, QL, KVL):
    T, W = pl_.shape
    tm = _tile(T, _ROW_TILE, _PACKED_ROWS)

    def body(p_ref, gq_ref, gkv_ref, cq_ref, ckv_ref):
        def norm(xv, gv):
            return xv * lax.rsqrt(jnp.mean(xv * xv, axis=-1, keepdims=True) + EPS) * gv
        cq_ref[...] = norm(p_ref[:, :QL], gq_ref[...]).astype(cq_ref.dtype)
        ckv_ref[...] = norm(p_ref[:, QL:QL + KVL], gkv_ref[...]).astype(ckv_ref.dtype)

    return pl.pallas_call(
        body, name="lat_norm_fwd", grid=(T // tm,),
        in_specs=[_row_spec(tm, W), _full_spec((1, QL)), _full_spec((1, KVL))],
        out_specs=[_row_spec(tm, QL), _row_spec(tm, KVL)],
        out_shape=[jax.ShapeDtypeStruct((T, QL), _MXU_DTYPE), jax.ShapeDtypeStruct((T, KVL), _MXU_DTYPE)],
        compiler_params=_params(("parallel",), 2 * _nbytes((tm, W), F32), 4 * _nbytes((tm, W), F32)),
    )(pl_, gq, gkv)


def _lat_norm_bwd(pl_, gq, gkv, dcq, dckv, drope, QL, KVL):
    T, W = pl_.shape
    tm = _tile(T, _ROW_TILE, _PACKED_ROWS)

    def body(p_ref, gq_ref, gkv_ref, dcq_ref, dckv_ref, dr_ref, dp_ref, dgq_ref, dgkv_ref):
        _init_acc([dgq_ref, dgkv_ref])

        def bwd(xv, gv, dv, dg_ref):
            r = lax.rsqrt(jnp.mean(xv * xv, axis=-1, keepdims=True) + EPS)
            xh = xv * r
            dxh = dv * gv
            dg_ref[...] += _colsum(dv * xh)
            return r * (dxh - xh * jnp.mean(dxh * xh, axis=-1, keepdims=True))

        dp_ref[:, :QL] = bwd(p_ref[:, :QL], gq_ref[...], dcq_ref[...], dgq_ref).astype(dp_ref.dtype)
        dp_ref[:, QL:QL + KVL] = bwd(p_ref[:, QL:QL + KVL], gkv_ref[...], dckv_ref[...], dgkv_ref).astype(dp_ref.dtype)
        dp_ref[:, QL + KVL:] = dr_ref[...].astype(dp_ref.dtype)

    return pl.pallas_call(
        body, name="lat_norm_bwd", grid=(T // tm,),
        in_specs=[_row_spec(tm, W), _full_spec((1, QL)), _full_spec((1, KVL)), _row_spec(tm, QL), _row_spec(tm, KVL),
                  _row_spec(tm, HEAD_W)],
        out_specs=[_row_spec(tm, W), _full_spec((1, QL)), _full_spec((1, KVL))],
        out_shape=[jax.ShapeDtypeStruct((T, W), _MXU_DTYPE), jax.ShapeDtypeStruct((1, QL), F32),
                   jax.ShapeDtypeStruct((1, KVL), F32)],
        compiler_params=_params(("arbitrary",), 4 * _nbytes((tm, W), F32), 6 * _nbytes((tm, W), F32)),
    )(pl_, gq, gkv, dcq, dckv, drope)


def _norm_rope(xv, gv, ct, sa, sb):
    r = lax.rsqrt(jnp.sum(xv * xv, axis=-1, keepdims=True) * (1.0 / QK_DIM) + EPS)
    xn = xv * r * gv
    half = ROPE_DIM // 2
    return xn * ct + pltpu.roll(xn, HEAD_W - half, 1) * sa + pltpu.roll(xn, half, 1) * sb


def _norm_rope_bwd(xv, gv, ct, sa, sb, dout):
    half = ROPE_DIM // 2
    dxn = dout * ct + pltpu.roll(dout * sa, half, 1) + pltpu.roll(dout * sb, HEAD_W - half, 1)
    r = lax.rsqrt(jnp.sum(xv * xv, axis=-1, keepdims=True) * (1.0 / QK_DIM) + EPS)
    xh = xv * r
    dxh = dxn * gv
    dx = r * (dxh - xh * (jnp.sum(dxh * xh, axis=-1, keepdims=True) * (1.0 / QK_DIM)))
    return dx, _colsum(dxn * xh)


def _qk_prep_fwd(q_raw, k_raw, pl_, rope_cb, ct, sa, sb, gq, gk):
    T, HW = q_raw.shape
    H = HW // HEAD_W
    tm = _tile(T, _ROW_TILE, _PACKED_ROWS)

    def body(q_ref, k_ref, r_ref, ct_ref, sa_ref, sb_ref, gq_ref, gk_ref, qn_ref, kn_ref):
        ctv, sav, sbv = ct_ref[...], sa_ref[...], sb_ref[...]
        qn_ref[...] = _norm_rope(q_ref[...], gq_ref[...], ctv, sav, sbv).astype(qn_ref.dtype)
        kn_ref[...] = _norm_rope(k_ref[...] + r_ref[...], gk_ref[...], ctv, sav, sbv).astype(kn_ref.dtype)

    hb = pl.BlockSpec((tm, HEAD_W), lambda i, h: (i, h))
    tb = pl.BlockSpec((tm, HEAD_W), lambda i, h: (i, 0))
    gb = pl.BlockSpec((1, HEAD_W), lambda i, h: (0, 0))
    return pl.pallas_call(
        body, name="qk_prep_fwd", grid=(T // tm, H),
        in_specs=[hb, hb, pl.BlockSpec((tm, HEAD_W), lambda i, h: (i, rope_cb)), tb, tb, tb, gb, gb],
        out_specs=[hb, hb],
        out_shape=[jax.ShapeDtypeStruct((T, HW), _MXU_DTYPE)] * 2,
        compiler_params=_params(("parallel", "parallel"), 8 * _nbytes((tm, HEAD_W), F32), 8 * _nbytes((tm, HEAD_W), F32)),
    )(q_raw, k_raw, pl_, ct, sa, sb, gq, gk)


def _qk_prep_bwd(q_raw, k_raw, pl_, rope_cb, ct, sa, sb, gq, gk, dqn, dkn):
    T, HW = q_raw.shape
    H = HW // HEAD_W
    tm = _tile(T, _ROW_TILE, _PACKED_ROWS)

    def body(q_ref, k_ref, r_ref, ct_ref, sa_ref, sb_ref, gq_ref, gk_ref, dqn_ref, dkn_ref,
             dq_ref, dk_ref, dr_ref, dgq_ref, dgk_ref):
        i, h = pl.program_id(0), pl.program_id(1)

        @pl.when((i == 0) & (h == 0))
        def _():
            dgq_ref[...] = jnp.zeros_like(dgq_ref)
            dgk_ref[...] = jnp.zeros_like(dgk_ref)

        ctv, sav, sbv = ct_ref[...], sa_ref[...], sb_ref[...]
        dq, dgq = _norm_rope_bwd(q_ref[...], gq_ref[...], ctv, sav, sbv, dqn_ref[...])
        dk, dgk = _norm_rope_bwd(k_ref[...] + r_ref[...], gk_ref[...], ctv, sav, sbv, dkn_ref[...])
        dq_ref[...] = dq.astype(dq_ref.dtype)
        dk_ref[...] = dk.astype(dk_ref.dtype)
        dgq_ref[...] += dgq
        dgk_ref[...] += dgk

        @pl.when(h == 0)
        def _():
            dr_ref[...] = dk

        @pl.when(h > 0)
        def _():
            dr_ref[...] += dk

    hb = pl.BlockSpec((tm, HEAD_W), lambda i, h: (i, h))
    tb = pl.BlockSpec((tm, HEAD_W), lambda i, h: (i, 0))
    gb = pl.BlockSpec((1, HEAD_W), lambda i, h: (0, 0))
    return pl.pallas_call(
        body, name="qk_prep_bwd", grid=(T // tm, H),
        in_specs=[hb, hb, pl.BlockSpec((tm, HEAD_W), lambda i, h: (i, rope_cb)), tb, tb, tb, gb, gb, hb, hb],
        out_specs=[hb, hb, tb, gb, gb],
        out_shape=[jax.ShapeDtypeStruct((T, HW), _MXU_DTYPE)] * 2 + [jax.ShapeDtypeStruct((T, HEAD_W), F32)]
                  + [jax.ShapeDtypeStruct((1, HEAD_W), F32)] * 2,
        compiler_params=_params(("arbitrary", "arbitrary"), 13 * _nbytes((tm, HEAD_W), F32), 12 * _nbytes((tm, HEAD_W), F32)),
    )(q_raw, k_raw, pl_, ct, sa, sb, gq, gk, dqn, dkn)


def _chunk_mask(t):
    rc = lax.broadcasted_iota(jnp.int32, (t, t), 0) // CHUNK
    cc = lax.broadcasted_iota(jnp.int32, (t, t), 1) // CHUNK
    return cc <= rc


_NEG = -1e30


def _attn_fwd(q, k, v):
    T, HW = q.shape
    H = HW // HEAD_W
    t = _tile(T, _ATT_TILE)
    reps = t // _LANES
    scale = QK_DIM ** -0.5

    def body(q_ref, k_ref, v_ref, o_ref, lse_ref, m_s, l_s, acc_s):
        i = pl.program_id(1)
        qv = q_ref[...]
        m_s[...] = jnp.full_like(m_s, _NEG)
        l_s[...] = jnp.zeros_like(l_s)
        acc_s[...] = jnp.zeros_like(acc_s)

        def step(j, masked):
            rows = pl.ds(pl.multiple_of(j * t, t), t)
            s = _dot(qv, k_ref[rows, :], 'nt') * scale
            if masked:
                s = jnp.where(_chunk_mask(t), s, _NEG)
            m_prev = m_s[...]
            m_new = jnp.maximum(m_prev, jnp.max(s, axis=1, keepdims=True))
            p = jnp.exp(s - jnp.tile(m_new, (1, reps)))
            alpha = jnp.exp(m_prev - m_new)
            l_s[...] = alpha * l_s[...] + jnp.sum(p, axis=1, keepdims=True)
            acc_s[...] = alpha * acc_s[...] + _dot(p.astype(v_ref.dtype), v_ref[rows, :], 'nn')
            m_s[...] = m_new

        def loop_body(j, carry):
            step(j, False)
            return carry

        lax.fori_loop(0, i, loop_body, 0)
        step(i, True)
        l = l_s[...]
        o_ref[...] = (acc_s[...] / l).astype(o_ref.dtype)
        lse_ref[...] = m_s[...] + jnp.log(l)

    qb = pl.BlockSpec((t, HEAD_W), lambda h, i: (i, h))
    kb = pl.BlockSpec((T, HEAD_W), lambda h, i: (0, h))
    sc = pltpu.VMEM((t, HEAD_W), F32)
    return pl.pallas_call(
        body, name="attn_fwd", grid=(H, T // t),
        in_specs=[qb, kb, kb], out_specs=[qb, qb],
        out_shape=[jax.ShapeDtypeStruct((T, HW), _MXU_DTYPE), jax.ShapeDtypeStruct((T, HW), F32)],
        scratch_shapes=[sc, sc, sc],
        compiler_params=_params(("parallel", "arbitrary"), 2 * _nbytes((T, HEAD_W), _MXU_DTYPE) + 3 * _nbytes((t, HEAD_W), F32),
                                6 * _nbytes((t, t), F32)),
    )(q, k, v)


def _attn_bwd_dq(q, k, v, o, do, lse):
    T, HW = q.shape
    H = HW // HEAD_W
    t = _tile(T, _ATT_TILE)
    reps = t // _LANES
    scale = QK_DIM ** -0.5

    def body(q_ref, k_ref, v_ref, o_ref, do_ref, lse_ref, dq_ref, acc_s):
        i = pl.program_id(1)
        qv, dov = q_ref[...], do_ref[...]
        delta = jnp.sum(dov.astype(F32) * o_ref[...].astype(F32), axis=1, keepdims=True)
        lse_t = jnp.tile(lse_ref[...], (1, reps))
        acc_s[...] = jnp.zeros_like(acc_s)

        def step(j, masked):
            rows = pl.ds(pl.multiple_of(j * t, t), t)
            kv = k_ref[rows, :]
            p = jnp.exp(_dot(qv, kv, 'nt') * scale - lse_t)
            if masked:
                p = jnp.where(_chunk_mask(t), p, 0.0)
            dp = _dot(dov, v_ref[rows, :], 'nt')
            ds = p * (dp - delta) * scale
            acc_s[...] += _dot(ds.astype(kv.dtype), kv, 'nn')

        def loop_body(j, carry):
            step(j, False)
            return carry

        lax.fori_loop(0, i, loop_body, 0)
        step(i, True)
        dq_ref[...] = acc_s[...]

    qb = pl.BlockSpec((t, HEAD_W), lambda h, i: (i, h))
    kb = pl.BlockSpec((T, HEAD_W), lambda h, i: (0, h))
    return pl.pallas_call(
        body, name="attn_bwd_dq", grid=(H, T // t),
        in_specs=[qb, kb, kb, qb, qb, qb], out_specs=qb,
        out_shape=jax.ShapeDtypeStruct((T, HW), F32),
        scratch_shapes=[pltpu.VMEM((t, HEAD_W), F32)],
        compiler_params=_params(("parallel", "arbitrary"), 2 * _nbytes((T, HEAD_W), _MXU_DTYPE) + 5 * _nbytes((t, HEAD_W), F32),
                                8 * _nbytes((t, t), F32)),
    )(q, k, v, o, do, lse)


def _attn_bwd_dkv(q, k, v, o, do, lse):
    T, HW = q.shape
    H = HW // HEAD_W
    t = _tile(T, _ATT_TILE)
    n = T // t
    reps = t // _LANES
    scale = QK_DIM ** -0.5

    def body(k_ref, v_ref, q_ref, o_ref, do_ref, lse_ref, dk_ref, dv_ref, dk_s, dv_s):
        j = pl.program_id(1)
        kv, vv = k_ref[...], v_ref[...]
        dk_s[...] = jnp.zeros_like(dk_s)
        dv_s[...] = jnp.zeros_like(dv_s)

        def step(i, masked):
            rows = pl.ds(pl.multiple_of(i * t, t), t)
            qv, dov = q_ref[rows, :], do_ref[rows, :]
            delta = jnp.sum(dov.astype(F32) * o_ref[rows, :].astype(F32), axis=1, keepdims=True)
            p = jnp.exp(_dot(qv, kv, 'nt') * scale - jnp.tile(lse_ref[rows, :], (1, reps)))
            if masked:
                p = jnp.where(_chunk_mask(t), p, 0.0)
            dv_s[...] += _dot(p.astype(dov.dtype), dov, 'tn')
            dp = _dot(dov, vv, 'nt')
            ds = p * (dp - delta) * scale
            dk_s[...] += _dot(ds.astype(qv.dtype), qv, 'tn')

        def loop_body(i, carry):
            step(i, False)
            return carry

        step(j, True)
        lax.fori_loop(j + 1, n, loop_body, 0)
        dk_ref[...] = dk_s[...]
        dv_ref[...] = dv_s[...].astype(dv_ref.dtype)

    jb = pl.BlockSpec((t, HEAD_W), lambda h, j: (j, h))
    fb = pl.BlockSpec((T, HEAD_W), lambda h, j: (0, h))
    sc = pltpu.VMEM((t, HEAD_W), F32)
    return pl.pallas_call(
        body, name="attn_bwd_dkv", grid=(H, n),
        in_specs=[jb, jb, fb, fb, fb, fb], out_specs=[jb, jb],
        out_shape=[jax.ShapeDtypeStruct((T, HW), F32), jax.ShapeDtypeStruct((T, HW), _MXU_DTYPE)],
        scratch_shapes=[sc, sc],
        compiler_params=_params(("parallel", "arbitrary"), 3 * _nbytes((T, HEAD_W), _MXU_DTYPE) + _nbytes((T, HEAD_W), F32)
                                + 4 * _nbytes((t, HEAD_W), F32), 8 * _nbytes((t, t), F32)),
    )(k, v, q, o, do, lse)


def _gate_fwd(pg, gb, yc, ym):
    T, D = yc.shape
    tm = _tile(T, _ROW_TILE, _PACKED_ROWS)

    def body(g0_ref, g1_ref, gb_ref, yc_ref, ym_ref, y_ref):
        s0 = _sig(g0_ref[...] + gb_ref[0:1, :])
        s1 = _sig(g1_ref[...] + gb_ref[1:2, :])
        y_ref[...] = (s0 * yc_ref[...] + s1 * ym_ref[...]).astype(y_ref.dtype)

    return pl.pallas_call(
        body, name="gate_fwd", grid=(T // tm,),
        in_specs=[_row_spec(tm, D, 0), _row_spec(tm, D, 1), _full_spec((2, D)), _row_spec(tm, D), _row_spec(tm, D)],
        out_specs=_row_spec(tm, D),
        out_shape=jax.ShapeDtypeStruct((T, D), _MXU_DTYPE),
        compiler_params=_params(("parallel",), 5 * _nbytes((tm, D), F32), 4 * _nbytes((tm, D), F32)),
    )(pg, pg, gb, yc, ym)


def _gate_bwd(pg, gb, yc, ym, dy):
    T, D = yc.shape
    tm = _tile(T, _ROW_TILE, _PACKED_ROWS)

    def body(g0_ref, g1_ref, gb_ref, yc_ref, ym_ref, dy_ref, dyc_ref, dym_ref, dpg_ref, dgb_ref):
        _init_acc([dgb_ref])
        dyv = dy_ref[...]
        s0 = _sig(g0_ref[...] + gb_ref[0:1, :])
        s1 = _sig(g1_ref[...] + gb_ref[1:2, :])
        dyc_ref[...] = (dyv * s0).astype(dyc_ref.dtype)
        dym_ref[...] = (dyv * s1).astype(dym_ref.dtype)
        d0 = dyv * yc_ref[...] * s0 * (1.0 - s0)
        d1 = dyv * ym_ref[...] * s1 * (1.0 - s1)
        dpg_ref[:, :D] = d0.astype(dpg_ref.dtype)
        dpg_ref[:, D:] = d1.astype(dpg_ref.dtype)
        dgb_ref[0:1, :] += _colsum(d0)
        dgb_ref[1:2, :] += _colsum(d1)

    return pl.pallas_call(
        body, name="gate_bwd", grid=(T // tm,),
        in_specs=[_row_spec(tm, D, 0), _row_spec(tm, D, 1), _full_spec((2, D)), _row_spec(tm, D), _row_spec(tm, D),
                  _row_spec(tm, D)],
        out_specs=[_row_spec(tm, D), _row_spec(tm, D), _row_spec(tm, 2 * D), _full_spec((2, D))],
        out_shape=[jax.ShapeDtypeStruct((T, D), _MXU_DTYPE)] * 2 + [jax.ShapeDtypeStruct((T, 2 * D), _MXU_DTYPE),
                                                                   jax.ShapeDtypeStruct((2, D), F32)],
        compiler_params=_params(("arbitrary",), 7 * _nbytes((tm, D), F32), 8 * _nbytes((tm, D), F32)),
    )(pg, pg, gb, yc, ym, dy)


def _place():
    return lax.axis_index("x"), lax.axis_index("y"), lax.axis_index("c")


def _all_gather(name, xs):
    R, C = xs.shape
    MESH = pl.DeviceIdType.MESH

    def body(x_ref, out_ref, send_sems, recv_sems, local_sem):
        x, y, c = _place()
        me, sibling = (x, y, c), (x, y, 1 - c)
        chips = [(1 - x, y), (x, 1 - y), (1 - x, 1 - y)]

        def rows(px, py, pc):
            return out_ref.at[4 * px + 2 * py + pc]

        def copy(k, block, to, src=None):
            return pltpu.make_async_remote_copy(
                src_ref=rows(*block) if src is None else src, dst_ref=rows(*block),
                send_sem=send_sems.at[k], recv_sem=recv_sems.at[k], device_id=to, device_id_type=MESH)

        mine = pltpu.make_async_copy(x_ref, rows(*me), local_sem)
        mine.start()
        first = [copy(0, me, sibling, src=x_ref)]
        first += [copy(1 + j, me, (*chip, c), src=x_ref) for j, chip in enumerate(chips)]
        for cp in first:
            cp.start()
        passed = [copy(4 + j, (*chip, c), sibling) for j, chip in enumerate(chips)]
        for j, chip in enumerate(chips):
            copy(1 + j, (*chip, c), me).wait_recv()
            passed[j].start()
        copy(0, sibling, me).wait_recv()
        for j, chip in enumerate(chips):
            copy(4 + j, (*chip, 1 - c), me).wait_recv()
        for cp in first + passed:
            cp.wait_send()
        mine.wait()

    return pl.pallas_call(
        body, name=name,
        out_shape=jax.ShapeDtypeStruct((N_DEV, R, C), xs.dtype),
        in_specs=[pl.BlockSpec(memory_space=pl.ANY)], out_specs=pl.BlockSpec(memory_space=pl.ANY),
        scratch_shapes=[pltpu.SemaphoreType.DMA((7,)), pltpu.SemaphoreType.DMA((7,)), pltpu.SemaphoreType.DMA],
    )(xs)


def _all_to_all(name, xs):
    _, R, C = xs.shape
    MESH = pl.DeviceIdType.MESH

    def body(x_ref, out_ref, send_sems, recv_sems, local_sem):
        x, y, c = _place()
        me = 4 * x + 2 * y + c
        mine = pltpu.make_async_copy(x_ref.at[me], out_ref.at[me], local_sem)
        mine.start()
        copies = []
        for k in range(1, N_DEV):
            px = 1 - x if k & 4 else x
            py = 1 - y if k & 2 else y
            pc = 1 - c if k & 1 else c
            peer = 4 * px + 2 * py + pc
            copies.append(pltpu.make_async_remote_copy(
                src_ref=x_ref.at[peer], dst_ref=out_ref.at[me],
                send_sem=send_sems.at[k - 1], recv_sem=recv_sems.at[k - 1],
                device_id=(px, py, pc), device_id_type=MESH))
        for cp in copies:
            cp.start()
        for cp in copies:
            cp.wait()
        mine.wait()

    return pl.pallas_call(
        body, name=name,
        out_shape=jax.ShapeDtypeStruct((N_DEV, R, C), xs.dtype),
        in_specs=[pl.BlockSpec(memory_space=pl.ANY)], out_specs=pl.BlockSpec(memory_space=pl.ANY),
        scratch_shapes=[pltpu.SemaphoreType.DMA((7,)), pltpu.SemaphoreType.DMA((7,)), pltpu.SemaphoreType.DMA],
    )(xs)


def _sum_blocks(name, parts, scale=None):
    n, R, C = parts.shape
    tr = _tile(R, _PACK_ROW_MULT, _PACKED_ROWS)

    def body(p_ref, o_ref):
        acc = p_ref[0].astype(F32)
        for s in range(1, n):
            acc = acc + p_ref[s].astype(F32)
        if scale is not None:
            acc = acc * scale
        o_ref[...] = acc

    return pl.pallas_call(
        body, name=name, grid=(R // tr,),
        in_specs=[pl.BlockSpec((n, tr, C), lambda i: (0, i, 0))], out_specs=pl.BlockSpec((tr, C), lambda i: (i, 0)),
        out_shape=jax.ShapeDtypeStruct((R, C), F32),
        compiler_params=_params(("parallel",), _nbytes((n, tr, C), parts.dtype) + _nbytes((tr, C), F32), 2 * _nbytes((tr, C), F32)),
    )(parts)


def _adamw(w, g, m, v):
    shape = w.shape
    w2, g2, m2, v2 = (t.reshape(-1, shape[-1]) for t in (w, g, m, v))
    R, C = w2.shape
    tr = _tile(R, _ROW_TILE, _SUBLANES)

    def body(w_ref, g_ref, m_ref, v_ref, d_ref, nm_ref, nv_ref):
        gv = g_ref[...]
        mv = ADAM_B1 * m_ref[...] + (1.0 - ADAM_B1) * gv
        vv = ADAM_B2 * v_ref[...] + (1.0 - ADAM_B2) * (gv * gv)
        m_hat = mv / (1.0 - ADAM_B1 ** ADAM_STEP)
        v_hat = vv / (1.0 - ADAM_B2 ** ADAM_STEP)
        d_ref[...] = -ADAM_LR * (m_hat / (jnp.sqrt(v_hat) + ADAM_EPS) + ADAM_WD * w_ref[...])
        nm_ref[...] = mv
        nv_ref[...] = vv

    spec = pl.BlockSpec((tr, C), lambda i: (i, 0))
    outs = pl.pallas_call(
        body, name="adamw", grid=(R // tr,),
        in_specs=[spec] * 4, out_specs=[spec] * 3,
        out_shape=[jax.ShapeDtypeStruct((R, C), F32)] * 3,
        compiler_params=_params(("parallel",), 7 * _nbytes((tr, C), F32), 4 * _nbytes((tr, C), F32)),
    )(w2, g2, m2, v2)
    return tuple(o.reshape(shape) for o in outs)


def _pack_rows(parts, lead=()):
    out, spans, r0 = [], [], 0
    unit = _PACKED_ROWS * _PACK_W
    for p in parts:
        flat = p.reshape(lead + (-1,))
        size = flat.shape[-1]
        pad = (-size) % unit
        if pad:
            flat = jnp.pad(flat, [(0, 0)] * len(lead) + [(0, pad)])
        rows = (size + pad) // _PACK_W
        out.append(flat.reshape(lead + (rows, _PACK_W)))
        spans.append((r0, rows, p.shape[len(lead):]))
        r0 += rows
    pad = (-r0) % _PACK_ROW_MULT
    if pad:
        out.append(jnp.zeros(lead + (pad, _PACK_W), out[0].dtype))
    return jnp.concatenate(out, axis=len(lead)), spans


def _unpack_rows(packed, span, lead=()):
    r0, rows, shape = span
    size = 1
    for s in shape:
        size *= s
    sl = lax.slice_in_dim(packed, r0, r0 + rows, axis=len(lead))
    return sl.reshape(lead + (-1,))[..., :size].reshape(lead + tuple(shape))


def _rope_tables(positions, T):
    inv_freq = ROPE_THETA ** (-jnp.arange(0, ROPE_DIM, 2, dtype=F32) / ROPE_DIM)
    ang = positions.reshape(T, 1).astype(F32) * inv_freq
    cos, sin = jnp.cos(ang), jnp.sin(ang)
    half = ROPE_DIM // 2
    z = lambda n: jnp.zeros((T, n), F32)
    tail = HEAD_W - QK_DIM
    ct = jnp.concatenate([jnp.ones((T, NOPE_DIM), F32), cos, cos, z(tail)], axis=1)
    sa = jnp.concatenate([z(NOPE_DIM), -sin, z(half), z(tail)], axis=1)
    sb = jnp.concatenate([z(NOPE_DIM), z(half), sin, z(tail)], axis=1)
    return ct, sa, sb


def _pad_heads(wt, per_head, keep_from=0, keep=None):
    K = wt.shape[1]
    keep = per_head if keep is None else keep
    w3 = wt.reshape(N_HEADS, per_head, K)[:, keep_from:keep_from + keep]
    return jnp.pad(w3, ((0, 0), (0, HEAD_W - keep), (0, 0))).reshape(N_HEADS * HEAD_W, K)


def _unpad_heads(g, keep):
    return g.reshape(N_HEADS, HEAD_W, g.shape[1])[:, :keep]


def _layer_weights(W):
    D = W['w_out'].shape[1]
    C = W['w_conv_out'].shape[1]
    QL, KVL = W['w_uq'].shape[1], W['w_ukv'].shape[1]
    o1 = 2 * C
    o2, o3 = o1 + QL, o1 + QL + KVL
    o4 = o3 + ROPE_DIM
    win = W['w_in']
    zr = lambda n: jnp.zeros((n, D), win.dtype)
    L = dict(W)
    L['w_c'] = win[:o1]
    L['w_l'] = jnp.concatenate([win[o1:o3], zr(NOPE_DIM), win[o3:o4], zr(HEAD_W - QK_DIM)], axis=0)
    L['w_g'] = win[o4:]
    L['w_q'] = _pad_heads(W['w_uq'], QK_DIM)
    L['w_k'] = _pad_heads(W['w_ukv'], NOPE_DIM + V_DIM, 0, NOPE_DIM)
    L['w_v'] = _pad_heads(W['w_ukv'], NOPE_DIM + V_DIM, NOPE_DIM, V_DIM)
    wmo = W['w_mla_out'].reshape(D, N_HEADS, V_DIM)
    L['w_mo'] = jnp.pad(wmo, ((0, 0), (0, 0), (0, HEAD_W - V_DIM))).reshape(D, N_HEADS * HEAD_W)
    L['dims'] = (D, C, QL, KVL)
    return L


def _row(v, width=None):
    v = v.reshape(1, -1)
    if width is not None and v.shape[1] < width:
        v = jnp.pad(v, ((0, 0), (0, width - v.shape[1])))
    return v


def _ffn_fwd(x, g, wg, wu, wd):
    h = _rms_fwd(x, _row(g))
    a, b, u = _ffn_up(h, wg, wu)
    x_out = _mm("ffn_down", u, wd, 'nn', res=x, scale=0.5)
    return x_out, (x, h, a, b, u)


def _ffn_bwd(dx, saved, g, wg, wu, wd):
    x, h, a, b, u = saved
    d_wd = _mm("ffn_dwd", u, dx, 'tn', scale=0.5)
    da, db = _ffn_bwd_up(dx, wd, a, b)
    dh = _mm("ffn_dh_g", da, wg, 'nn')
    dh = _mm("ffn_dh_u", db, wu, 'nn', res=dh)
    d_wg = _mm("ffn_dwg", da, h, 'tn')
    d_wu = _mm("ffn_dwg", db, h, 'tn')
    dx_in, dg = _rms_bwd(x, _row(g), dh, dx)
    return dx_in, dg, d_wg, d_wu, d_wd


def _layer_fwd(x, L, S, tabs):
    D, C, QL, KVL = L['dims']
    rope_cb = (QL + KVL) // HEAD_W
    x1, ffn1 = _ffn_fwd(x, S['ffn1_norm'], L['ffn1_w_gate'], L['ffn1_w_up'], L['ffn1_w_down'])
    h = _rms_fwd(x1, _row(S['mix_norm']))
    pc = _mm("proj_c", h, L['w_c'], 'nt')
    pl_ = _mm("proj_l", h, L['w_l'], 'nt')
    pg = _mm("proj_g", h, L['w_g'], 'nt')
    conv_w = jnp.pad(S['conv_w'], ((0, HALO - CONV_WIDTH), (0, 0)))
    z, c = _conv_fwd(pc, conv_w, _row(S['conv_b']), _row(S['conv_ln_g']), _row(S['conv_ln_b']))
    yc = _mm("conv_out", c, L['w_conv_out'], 'nt')
    cqn, ckvn = _lat_norm_fwd(pl_, _row(S['cq_norm']), _row(S['ckv_norm']), QL, KVL)
    q_raw = _mm("up_q", cqn, L['w_q'], 'nt')
    k_raw = _mm("up_kv", ckvn, L['w_k'], 'nt')
    v = _mm("up_kv", ckvn, L['w_v'], 'nt', out_dtype=_MXU_DTYPE)
    gq, gk = _row(S['q_norm'], HEAD_W), _row(S['k_norm'], HEAD_W)
    qn, kn = _qk_prep_fwd(q_raw, k_raw, pl_, rope_cb, *tabs, gq, gk)
    o, lse = _attn_fwd(qn, kn, v)
    ym = _mm("mla_out", o, L['w_mo'], 'nt')
    y = _gate_fwd(pg, S['gate_bias'], yc, ym)
    x2 = _mm("mix_out", y, L['w_out'], 'nn', res=x1)
    x3, ffn2 = _ffn_fwd(x2, S['ffn2_norm'], L['ffn2_w_gate'], L['ffn2_w_up'], L['ffn2_w_down'])
    saved = dict(ffn1=ffn1, ffn2=ffn2, x1=x1, h=h, pc=pc, pl=pl_, pg=pg, z=z, c=c, yc=yc, cqn=cqn, ckvn=ckvn,
                 q_raw=q_raw, k_raw=k_raw, v=v, qn=qn, kn=kn, o=o, lse=lse, ym=ym, y=y, conv_w=conv_w, gq=gq, gk=gk)
    return x3, saved


def _layer_bwd(dx, L, S, tabs, A):
    D, C, QL, KVL = L['dims']
    rope_cb = (QL + KVL) // HEAD_W
    G, g = {}, {}
    dx, g['ffn2_norm'], G['ffn2_w_gate'], G['ffn2_w_up'], G['ffn2_w_down'] = _ffn_bwd(
        dx, A['ffn2'], S['ffn2_norm'], L['ffn2_w_gate'], L['ffn2_w_up'], L['ffn2_w_down'])
    G['w_out'] = _mm("mix_dwout", A['y'], dx, 'tn')
    dy = _mm("mix_dy", dx, L['w_out'], 'nt')
    dyc, dym, dpg, g['gate_bias'] = _gate_bwd(A['pg'], S['gate_bias'], A['yc'], A['ym'], dy)
    G['w_conv_out'] = _mm("conv_dwout", dyc, A['c'], 'tn')
    dc = _mm("conv_dc", dyc, L['w_conv_out'], 'nn')
    dpc, dcw, g['conv_b'], g['conv_ln_g'], g['conv_ln_b'] = _conv_bwd(
        A['pc'], A['z'], dc, A['conv_w'], _row(S['conv_ln_g']), _row(S['conv_ln_b']))
    g['conv_w'] = dcw[:CONV_WIDTH]
    d_wmo = _mm("mla_dwout", dym, A['o'], 'tn')
    G['w_mla_out'] = d_wmo.reshape(D, N_HEADS, HEAD_W)[:, :, :V_DIM].reshape(D, N_HEADS * V_DIM)
    do = _mm("mla_do", dym, L['w_mo'], 'nn', out_dtype=_MXU_DTYPE)
    dqn = _attn_bwd_dq(A['qn'], A['kn'], A['v'], A['o'], do, A['lse'])
    dkn, dv = _attn_bwd_dkv(A['qn'], A['kn'], A['v'], A['o'], do, A['lse'])
    dq_raw, dk_raw, drope, dgq, dgk = _qk_prep_bwd(A['q_raw'], A['k_raw'], A['pl'], rope_cb, *tabs, A['gq'], A['gk'], dqn, dkn)
    g['q_norm'], g['k_norm'] = dgq[:, :QK_DIM], dgk[:, :QK_DIM]
    G['w_uq'] = _unpad_heads(_mm("up_dwq", dq_raw, A['cqn'], 'tn'), QK_DIM).reshape(N_HEADS * QK_DIM, QL)
    d_wk = _unpad_heads(_mm("up_dwkv", dk_raw, A['ckvn'], 'tn'), NOPE_DIM)
    d_wv = _unpad_heads(_mm("up_dwkv", dv, A['ckvn'], 'tn'), V_DIM)
    G['w_ukv'] = jnp.concatenate([d_wk, d_wv], axis=1).reshape(N_HEADS * (NOPE_DIM + V_DIM), KVL)
    dcq = _mm("up_dcq", dq_raw, L['w_q'], 'nn')
    dckv = _mm("up_dckv_k", dk_raw, L['w_k'], 'nn')
    dckv = _mm("up_dckv_v", dv, L['w_v'], 'nn', res=dckv)
    dpl, g['cq_norm'], g['ckv_norm'] = _lat_norm_bwd(A['pl'], _row(S['cq_norm']), _row(S['ckv_norm']), dcq, dckv, drope, QL, KVL)
    d_wc = _mm("proj_dwc", dpc, A['h'], 'tn')
    d_wl = _mm("proj_dwl", dpl, A['h'], 'tn')
    d_wg = _mm("proj_dwg", dpg, A['h'], 'tn')
    ql = QL + KVL
    G['w_in'] = jnp.concatenate([d_wc, d_wl[:ql], d_wl[ql + NOPE_DIM:ql + NOPE_DIM + ROPE_DIM], d_wg], axis=0)
    dh = _mm("proj_dh_c", dpc, L['w_c'], 'nn')
    dh = _mm("proj_dh_l", dpl, L['w_l'], 'nn', res=dh)
    dh = _mm("proj_dh_g", dpg, L['w_g'], 'nn', res=dh)
    dx, g['mix_norm'] = _rms_bwd(A['x1'], _row(S['mix_norm']), dh, dx)
    dx, g['ffn1_norm'], G['ffn1_w_gate'], G['ffn1_w_up'], G['ffn1_w_down'] = _ffn_bwd(
        dx, A['ffn1'], S['ffn1_norm'], L['ffn1_w_gate'], L['ffn1_w_up'], L['ffn1_w_down'])
    return dx, G, g


def kernel(x, positions, ffn1_norm, ffn1_w_gate, ffn1_w_up, ffn1_w_down, mix_norm, w_in, gate_bias, conv_w, conv_b, conv_ln_g, conv_ln_b, w_conv_out, cq_norm, ckv_norm, w_uq, w_ukv, q_norm, k_norm, w_mla_out, w_out, ffn2_norm, ffn2_w_gate, ffn2_w_up, ffn2_w_down, loss_target, m_ffn1_norm, m_ffn1_w_gate, m_ffn1_w_up, m_ffn1_w_down, m_mix_norm, m_w_in, m_gate_bias, m_conv_w, m_conv_b, m_conv_ln_g, m_conv_ln_b, m_w_conv_out, m_cq_norm, m_ckv_norm, m_w_uq, m_w_ukv, m_q_norm, m_k_norm, m_w_mla_out, m_w_out, m_ffn2_norm, m_ffn2_w_gate, m_ffn2_w_up, m_ffn2_w_down, v_ffn1_norm, v_ffn1_w_gate, v_ffn1_w_up, v_ffn1_w_down, v_mix_norm, v_w_in, v_gate_bias, v_conv_w, v_conv_b, v_conv_ln_g, v_conv_ln_b, v_w_conv_out, v_cq_norm, v_ckv_norm, v_w_uq, v_w_ukv, v_q_norm, v_k_norm, v_w_mla_out, v_w_out, v_ffn2_norm, v_ffn2_w_gate, v_ffn2_w_up, v_ffn2_w_down):
    w = dict(zip(WEIGHTS, (ffn1_norm, ffn1_w_gate, ffn1_w_up, ffn1_w_down, mix_norm, w_in, gate_bias, conv_w, conv_b, conv_ln_g, conv_ln_b, w_conv_out, cq_norm, ckv_norm, w_uq, w_ukv, q_norm, k_norm, w_mla_out, w_out, ffn2_norm, ffn2_w_gate, ffn2_w_up, ffn2_w_down)))
    m = dict(zip(WEIGHTS, (m_ffn1_norm, m_ffn1_w_gate, m_ffn1_w_up, m_ffn1_w_down, m_mix_norm, m_w_in, m_gate_bias, m_conv_w, m_conv_b, m_conv_ln_g, m_conv_ln_b, m_w_conv_out, m_cq_norm, m_ckv_norm, m_w_uq, m_w_ukv, m_q_norm, m_k_norm, m_w_mla_out, m_w_out, m_ffn2_norm, m_ffn2_w_gate, m_ffn2_w_up, m_ffn2_w_down)))
    v = dict(zip(WEIGHTS, (v_ffn1_norm, v_ffn1_w_gate, v_ffn1_w_up, v_ffn1_w_down, v_mix_norm, v_w_in, v_gate_bias, v_conv_w, v_conv_b, v_conv_ln_g, v_conv_ln_b, v_w_conv_out, v_cq_norm, v_ckv_norm, v_w_uq, v_w_ukv, v_q_norm, v_k_norm, v_w_mla_out, v_w_out, v_ffn2_norm, v_ffn2_w_gate, v_ffn2_w_up, v_ffn2_w_down)))
    depth = ffn1_norm.shape[0]
    T, D = x.shape[1], x.shape[2]
    xs = x.reshape(T, D)
    target = loss_target.reshape(T, D)
    tabs = _rope_tables(positions, T)
    me = 4 * lax.axis_index("x") + 2 * lax.axis_index("y") + lax.axis_index("c")

    big_names = [(l, n) for l in range(depth) for n in BIG]
    shards = []
    for l, n in big_names:
        s = w[n][l]
        shards.append((s.T if BIG[n] else s).astype(_WIRE_DTYPE))
    packed, spans = _pack_rows(shards)
    gathered = _all_gather("gather_weights", packed)
    small_names = [(l, n) for l in range(depth) for n in SMALL_SHARDED]
    small_packed, small_spans = _pack_rows([w[n][l].T for l, n in small_names])
    small_gathered = _all_gather("gather_small", small_packed)

    layers = []
    for l in range(depth):
        W = {}
        for (ll, n), span in zip(big_names, spans):
            if ll == l:
                sh = _unpack_rows(gathered, span, (N_DEV,))
                W[n] = sh.reshape(N_DEV * sh.shape[1], sh.shape[2])
        S = {n: w[n][l] for n in REPLICATED}
        for (ll, n), span in zip(small_names, small_spans):
            if ll == l:
                sh = _unpack_rows(small_gathered, span, (N_DEV,))
                S[n] = sh.reshape(N_DEV * sh.shape[1], sh.shape[2]).T
        layers.append((_layer_weights(W), S))

    acts = []
    hcur = xs
    for L, S in layers:
        hcur, saved = _layer_fwd(hcur, L, S, tabs)
        acts.append(saved)
    dx, loss_part = _loss_grad(hcur, target)
    big_grads, small_grads = [None] * depth, [None] * depth
    for l in reversed(range(depth)):
        L, S = layers[l]
        dx, big_grads[l], small_grads[l] = _layer_bwd(dx, L, S, tabs, acts[l])

    send, gspans = _pack_rows([big_grads[l][n].reshape(N_DEV, -1, big_grads[l][n].shape[1]).astype(_WIRE_DTYPE)
                               for l, n in big_names], (N_DEV,))
    received = _all_to_all("scatter_grads", send)
    reduced = _sum_blocks("sum_grads", received)
    small_list = [(l, n) for l in range(depth) for n in REPLICATED + SMALL_SHARDED]
    small_send, sspans = _pack_rows([small_grads[l][n] for l, n in small_list] + [loss_part])
    small_all = _sum_blocks("sum_small", _all_gather("gather_small_grads", small_send))

    loss = _unpack_rows(small_all, sspans[-1])[0, 0]
    grads = {}
    for n in WEIGHTS:
        per_layer = []
        for l in range(depth):
            if n in BIG:
                gt = _unpack_rows(reduced, gspans[big_names.index((l, n))])
                per_layer.append(gt.T if BIG[n] else gt)
            else:
                gs = _unpack_rows(small_all, sspans[small_list.index((l, n))])
                if n in SMALL_SHARDED:
                    cols = w[n].shape[-1]
                    gs = lax.dynamic_slice_in_dim(gs, me * cols, cols, axis=1)
                per_layer.append(gs.reshape(w[n].shape[1:]))
        grads[n] = jnp.stack(per_layer, axis=0)

    delta, new_m, new_v = {}, {}, {}
    for n in WEIGHTS:
        delta[n], new_m[n], new_v[n] = _adamw(w[n], grads[n], m[n], v[n])
    return (loss, dx.reshape(1, T, D), *[grads[n] for n in WEIGHTS], *[delta[n] for n in WEIGHTS],
            *[new_m[n] for n in WEIGHTS], *[new_v[n] for n in WEIGHTS])
```

```python
import jax
import jax.numpy as jnp
from jax import lax
from jax.experimental import pallas as pl
from jax.experimental.pallas import tpu as pltpu

F32 = jnp.float32
_MXU_DTYPE = jnp.bfloat16
_WIRE_DTYPE = jnp.bfloat16

_LANES = 128
_SUBLANES = 8
_PACKED_ROWS = 16
_V7X_VMEM_BYTES = 64 * 2 ** 20
_VMEM_HEADROOM = 8 * 2 ** 20

N_DEV = 8
EPS = 1e-6
CHUNK = 64
CONV_WIDTH = 31
N_HEADS = 8
NOPE_DIM = 64
ROPE_DIM = 32
QK_DIM = NOPE_DIM + ROPE_DIM
V_DIM = 64
HEAD_W = _LANES
ROPE_THETA = 10000.0
HALO = 32

ADAM_LR = 0.001
ADAM_B1 = 0.9
ADAM_B2 = 0.999
ADAM_EPS = 1e-08
ADAM_WD = 0.01
ADAM_STEP = 10

_ROW_TILE = 512
_MM_TILE = 512
_MM_TILE_N = 1408
_MM_TILE_K = 1024
_TN_TILE_M = 1408
_ATT_TILE = 512
_ATT_ROWS = 32
_PACK_W = 1024
_PACK_ROW_MULT = 256

WEIGHTS = ['ffn1_norm', 'ffn1_w_gate', 'ffn1_w_up', 'ffn1_w_down', 'mix_norm', 'w_in', 'gate_bias',
           'conv_w', 'conv_b', 'conv_ln_g', 'conv_ln_b', 'w_conv_out', 'cq_norm', 'ckv_norm', 'w_uq',
           'w_ukv', 'q_norm', 'k_norm', 'w_mla_out', 'w_out', 'ffn2_norm', 'ffn2_w_gate', 'ffn2_w_up',
           'ffn2_w_down']
BIG = {'ffn1_w_gate': True, 'ffn1_w_up': True, 'ffn1_w_down': False, 'w_in': True, 'w_conv_out': True,
       'w_uq': True, 'w_ukv': True, 'w_mla_out': True, 'w_out': False, 'ffn2_w_gate': True,
       'ffn2_w_up': True, 'ffn2_w_down': False}
SMALL_SHARDED = ['conv_w', 'gate_bias']
REPLICATED = ['ffn1_norm', 'mix_norm', 'conv_b', 'conv_ln_g', 'conv_ln_b', 'cq_norm', 'ckv_norm',
              'q_norm', 'k_norm', 'ffn2_norm']


def _tile(n, target, mult=_LANES):
    if n <= target:
        return n
    for d in range(target - target % mult, 0, -mult):
        if n % d == 0:
            return d
    return n


def _nbytes(shape, dtype):
    n = 1
    for s in shape:
        n *= s
    return n * jnp.dtype(dtype).itemsize


def _params(dims, block_bytes, extra_bytes=0):
    need = 2 * block_bytes + extra_bytes + _VMEM_HEADROOM
    limit = int(min(max(need, 32 * 2 ** 20), _V7X_VMEM_BYTES - 4 * 2 ** 20))
    return pltpu.CompilerParams(dimension_semantics=dims, vmem_limit_bytes=limit)


def _mxu(x, scale=None):
    if scale is not None:
        x = x * scale
    return x if x.dtype == _MXU_DTYPE else x.astype(_MXU_DTYPE)


def _dot(a, b, mode):
    dims = {'nn': (((1,), (0,)), ((), ())), 'nt': (((1,), (1,)), ((), ())), 'tn': (((0,), (0,)), ((), ()))}[mode]
    return lax.dot_general(a, b, dims, preferred_element_type=F32)


def _colsum(x):
    return jnp.sum(x, axis=0, keepdims=True)


def _sig(x):
    return jax.nn.sigmoid(x)


def _mm(name, a, b, mode, *, out_dtype=F32, res=None, scale=None, a_scale=None, tm=None, tn=None, tk=None):
    if mode == 'nn':
        (M, K), (_, N) = a.shape, b.shape
    elif mode == 'nt':
        (M, K), (N, _) = a.shape, b.shape
    else:
        (K, M), (_, N) = a.shape, b.shape
    tm = _tile(M, tm or (_TN_TILE_M if mode == 'tn' else _MM_TILE))
    tn = _tile(N, tn or _MM_TILE_N)
    tk = _tile(K, tk or (_MM_TILE_K if mode == 'tn' else 4096))
    nk = K // tk
    grid = (M // tm, N // tn, nk)
    a_spec = pl.BlockSpec((tk, tm), lambda i, j, k: (k, i)) if mode == 'tn' else pl.BlockSpec((tm, tk), lambda i, j, k: (i, k))
    b_spec = pl.BlockSpec((tn, tk), lambda i, j, k: (j, k)) if mode == 'nt' else pl.BlockSpec((tk, tn), lambda i, j, k: (k, j))
    o_spec = pl.BlockSpec((tm, tn), lambda i, j, k: (i, j))
    has_res = res is not None

    def body(*refs):
        a_ref, b_ref = refs[0], refs[1]
        res_ref = refs[2] if has_res else None
        o_ref = refs[3] if has_res else refs[2]
        acc_ref = refs[-1] if nk > 1 else None

        def finish(p):
            if scale is not None:
                p = p * scale
            if has_res:
                p = res_ref[...] + p
            o_ref[...] = p.astype(o_ref.dtype)

        p = _dot(_mxu(a_ref[...], a_scale), _mxu(b_ref[...]), mode)
        if nk == 1:
            finish(p)
        else:
            k = pl.program_id(2)

            @pl.when(k == 0)
            def _():
                acc_ref[...] = p

            @pl.when(k > 0)
            def _():
                acc_ref[...] += p

            @pl.when(k == nk - 1)
            def _():
                finish(acc_ref[...])

    blocks = (_nbytes((tm, tk), a.dtype) + _nbytes((tk, tn), b.dtype) + _nbytes((tm, tn), out_dtype)
              + (_nbytes((tm, tn), F32) if has_res else 0))
    extra = 3 * _nbytes((tm, tn), F32) + _nbytes((tm, tk), _MXU_DTYPE) + _nbytes((tk, tn), _MXU_DTYPE)
    return pl.pallas_call(
        body, name=name, grid=grid,
        in_specs=[a_spec, b_spec] + ([o_spec] if has_res else []),
        out_specs=o_spec,
        out_shape=jax.ShapeDtypeStruct((M, N), out_dtype),
        scratch_shapes=[pltpu.VMEM((tm, tn), F32)] if nk > 1 else [],
        compiler_params=_params(("parallel", "parallel", "arbitrary"), blocks, extra),
    )(*([a, b] + ([res] if has_res else [])))


def _ffn_up(h, wg, wu):
    T, D = h.shape
    F = wg.shape[0]
    tm, tn = _tile(T, _MM_TILE), _tile(F, _MM_TILE_N)

    def body(h_ref, wg_ref, wu_ref, a_ref, b_ref, u_ref):
        hv = h_ref[...]
        a = _dot(hv, wg_ref[...], 'nt')
        b = _dot(hv, wu_ref[...], 'nt')
        a_ref[...] = a
        b_ref[...] = b
        u_ref[...] = (a * _sig(a) * b).astype(u_ref.dtype)

    w_spec = pl.BlockSpec((tn, D), lambda j, i: (j, 0))
    o_spec = pl.BlockSpec((tm, tn), lambda j, i: (i, j))
    blocks = _nbytes((tm, D), h.dtype) + 2 * _nbytes((tn, D), wg.dtype) + 2 * _nbytes((tm, tn), F32) + _nbytes((tm, tn), _MXU_DTYPE)
    return pl.pallas_call(
        body, name="ffn_up", grid=(F // tn, T // tm),
        in_specs=[pl.BlockSpec((tm, D), lambda j, i: (i, 0)), w_spec, w_spec],
        out_specs=[o_spec, o_spec, o_spec],
        out_shape=[jax.ShapeDtypeStruct((T, F), F32), jax.ShapeDtypeStruct((T, F), F32),
                   jax.ShapeDtypeStruct((T, F), _MXU_DTYPE)],
        compiler_params=_params(("parallel", "parallel"), blocks, 4 * _nbytes((tm, tn), F32)),
    )(h, wg, wu)


def _ffn_bwd_up(dx, wd, a, b):
    T, D = dx.shape
    F = wd.shape[0]
    tm, tn = _tile(T, _MM_TILE), _tile(F, _MM_TILE_N)

    def body(dx_ref, wd_ref, a_ref, b_ref, da_ref, db_ref):
        du = _dot(_mxu(dx_ref[...], 0.5), wd_ref[...], 'nt')
        av, bv = a_ref[...], b_ref[...]
        s = _sig(av)
        db_ref[...] = (du * av * s).astype(db_ref.dtype)
        da_ref[...] = (du * bv * (s * (1.0 + av * (1.0 - s)))).astype(da_ref.dtype)

    t_spec = pl.BlockSpec((tm, tn), lambda j, i: (i, j))
    blocks = _nbytes((tm, D), F32) + _nbytes((tn, D), wd.dtype) + 2 * _nbytes((tm, tn), F32) + 2 * _nbytes((tm, tn), _MXU_DTYPE)
    return pl.pallas_call(
        body, name="ffn_bwd_up", grid=(F // tn, T // tm),
        in_specs=[pl.BlockSpec((tm, D), lambda j, i: (i, 0)), pl.BlockSpec((tn, D), lambda j, i: (j, 0)), t_spec, t_spec],
        out_specs=[t_spec, t_spec],
        out_shape=[jax.ShapeDtypeStruct((T, F), _MXU_DTYPE)] * 2,
        compiler_params=_params(("parallel", "parallel"), blocks, 5 * _nbytes((tm, tn), F32)),
    )(dx, wd, a, b)


def _row_spec(tm, w, cb=0):
    return pl.BlockSpec((tm, w), lambda i: (i, cb))


def _full_spec(shape):
    return pl.BlockSpec(shape, lambda i: (0,) * len(shape))


def _init_acc(refs):
    @pl.when(pl.program_id(0) == 0)
    def _():
        for r in refs:
            r[...] = jnp.zeros_like(r)


def _rms_fwd(x, g):
    T, D = x.shape
    tm = _tile(T, _ROW_TILE, _PACKED_ROWS)

    def body(x_ref, g_ref, h_ref):
        xv = x_ref[...]
        r = lax.rsqrt(jnp.mean(xv * xv, axis=-1, keepdims=True) + EPS)
        h_ref[...] = (xv * r * g_ref[...]).astype(h_ref.dtype)

    return pl.pallas_call(
        body, name="rms_fwd", grid=(T // tm,),
        in_specs=[_row_spec(tm, D), _full_spec((1, D))], out_specs=_row_spec(tm, D),
        out_shape=jax.ShapeDtypeStruct((T, D), _MXU_DTYPE),
        compiler_params=_params(("parallel",), 2 * _nbytes((tm, D), F32), 4 * _nbytes((tm, D), F32)),
    )(x, g)


def _rms_bwd(x, g, dh, dres):
    T, D = x.shape
    tm = _tile(T, _ROW_TILE, _SUBLANES)

    def body(x_ref, g_ref, dh_ref, dres_ref, dx_ref, dg_ref):
        _init_acc([dg_ref])
        xv, dhv = x_ref[...], dh_ref[...]
        r = lax.rsqrt(jnp.mean(xv * xv, axis=-1, keepdims=True) + EPS)
        xh = xv * r
        dxh = dhv * g_ref[...]
        dx_ref[...] = dres_ref[...] + r * (dxh - xh * jnp.mean(dxh * xh, axis=-1, keepdims=True))
        dg_ref[...] += _colsum(dhv * xh)

    return pl.pallas_call(
        body, name="rms_bwd", grid=(T // tm,),
        in_specs=[_row_spec(tm, D), _full_spec((1, D)), _row_spec(tm, D), _row_spec(tm, D)],
        out_specs=[_row_spec(tm, D), _full_spec((1, D))],
        out_shape=[jax.ShapeDtypeStruct((T, D), F32), jax.ShapeDtypeStruct((1, D), F32)],
        compiler_params=_params(("arbitrary",), 4 * _nbytes((tm, D), F32), 6 * _nbytes((tm, D), F32)),
    )(x, g, dh, dres)


def _loss_grad(y, target):
    T, D = y.shape
    tm = _tile(T, _ROW_TILE, _SUBLANES)

    def body(y_ref, t_ref, dy_ref, l_ref):
        _init_acc([l_ref])
        err = y_ref[...] - t_ref[...]
        dy_ref[...] = err * (1.0 / D)
        per_row = jnp.sum(err * err, axis=-1, keepdims=True) * (0.5 / D)
        l_ref[...] += jnp.sum(per_row, axis=0, keepdims=True)

    return pl.pallas_call(
        body, name="loss_grad", grid=(T // tm,),
        in_specs=[_row_spec(tm, D), _row_spec(tm, D)],
        out_specs=[_row_spec(tm, D), _full_spec((1, _LANES))],
        out_shape=[jax.ShapeDtypeStruct((T, D), F32), jax.ShapeDtypeStruct((1, _LANES), F32)],
        compiler_params=_params(("arbitrary",), 3 * _nbytes((tm, D), F32), 3 * _nbytes((tm, D), F32)),
    )(y, target)


def _glu(pc, C):
    a, gate = pc[:, :C], pc[:, C:]
    s = _sig(gate)
    return a, s, a * s


def _conv_fwd(pc, w, cb, lg, lb):
    T, C2 = pc.shape
    C = C2 // 2
    tm = _tile(T, _ROW_TILE // 2, HALO)
    per = tm // HALO

    def body(pc_ref, ph_ref, w_ref, cb_ref, lg_ref, lb_ref, z_ref, c_ref, ubuf):
        i = pl.program_id(0)
        _, _, u_cur = _glu(pc_ref[...], C)
        _, _, u_prev = _glu(ph_ref[...], C)
        ubuf[0:HALO, :] = jnp.where(i > 0, u_prev, 0.0)
        ubuf[HALO:HALO + tm, :] = u_cur
        acc = jnp.zeros((tm, C), F32)
        for k in range(CONV_WIDTH):
            off = HALO - (CONV_WIDTH - 1) + k
            acc = acc + w_ref[k:k + 1, :] * ubuf[off:off + tm, :]
        z = acc + cb_ref[...]
        z_ref[...] = z
        zc = z - jnp.mean(z, axis=-1, keepdims=True)
        y = zc * lax.rsqrt(jnp.mean(zc * zc, axis=-1, keepdims=True) + EPS) * lg_ref[...] + lb_ref[...]
        c_ref[...] = (y * _sig(y)).astype(c_ref.dtype)

    return pl.pallas_call(
        body, name="conv_fwd", grid=(T // tm,),
        in_specs=[_row_spec(tm, C2), pl.BlockSpec((HALO, C2), lambda i: (jnp.maximum(i * per - 1, 0), 0)),
                  _full_spec((HALO, C)), _full_spec((1, C)), _full_spec((1, C)), _full_spec((1, C))],
        out_specs=[_row_spec(tm, C), _row_spec(tm, C)],
        out_shape=[jax.ShapeDtypeStruct((T, C), F32), jax.ShapeDtypeStruct((T, C), _MXU_DTYPE)],
        scratch_shapes=[pltpu.VMEM((tm + HALO, C), F32)],
        compiler_params=_params(("parallel",), 3 * _nbytes((tm, C2), F32), 8 * _nbytes((tm, C), F32)),
    )(pc, pc, w, cb, lg, lb)


def _conv_bwd(pc, z, dc, w, lg, lb):
    T, C2 = pc.shape
    C = C2 // 2
    tm = _tile(T, _ROW_TILE // 2, HALO)
    per = tm // HALO
    n = T // tm
    last_halo = T // HALO - 1

    def body(pc_ref, ph_ref, z_ref, zn_ref, dc_ref, dcn_ref, w_ref, lg_ref, lb_ref,
             dpc_ref, dw_ref, dcb_ref, dlg_ref, dlb_ref, ubuf, dzbuf):
        i = pl.program_id(0)
        _init_acc([dw_ref, dcb_ref, dlg_ref, dlb_ref])
        g, b = lg_ref[...], lb_ref[...]

        def ln_swish_bwd(zv, dcv):
            zc = zv - jnp.mean(zv, axis=-1, keepdims=True)
            r = lax.rsqrt(jnp.mean(zc * zc, axis=-1, keepdims=True) + EPS)
            zh = zc * r
            y = zh * g + b
            s = _sig(y)
            dy = dcv * (s * (1.0 + y * (1.0 - s)))
            dyg = dy * g
            dz = r * (dyg - jnp.mean(dyg, axis=-1, keepdims=True) - zh * jnp.mean(dyg * zh, axis=-1, keepdims=True))
            return dz, dy, zh

        dz_c, dy_c, zh_c = ln_swish_bwd(z_ref[...], dc_ref[...])
        dz_n, _, _ = ln_swish_bwd(zn_ref[...], dcn_ref[...])
        dlg_ref[...] += _colsum(dy_c * zh_c)
        dlb_ref[...] += _colsum(dy_c)
        dcb_ref[...] += _colsum(dz_c)
        dzbuf[0:tm, :] = dz_c
        dzbuf[tm:tm + HALO, :] = jnp.where(i < n - 1, dz_n, 0.0)
        a, s, u_cur = _glu(pc_ref[...], C)
        _, _, u_prev = _glu(ph_ref[...], C)
        ubuf[0:HALO, :] = jnp.where(i > 0, u_prev, 0.0)
        ubuf[HALO:HALO + tm, :] = u_cur
        du = jnp.zeros((tm, C), F32)
        for k in range(CONV_WIDTH):
            back = CONV_WIDTH - 1 - k
            du = du + w_ref[k:k + 1, :] * dzbuf[back:back + tm, :]
            off = HALO - (CONV_WIDTH - 1) + k
            dw_ref[k:k + 1, :] += _colsum(dz_c * ubuf[off:off + tm, :])
        dpc_ref[:, :C] = (du * s).astype(dpc_ref.dtype)
        dpc_ref[:, C:] = (du * a * s * (1.0 - s)).astype(dpc_ref.dtype)

    nxt = lambda i: (jnp.minimum((i + 1) * per, last_halo), 0)
    vec = _full_spec((1, C))
    return pl.pallas_call(
        body, name="conv_bwd", grid=(n,),
        in_specs=[_row_spec(tm, C2), pl.BlockSpec((HALO, C2), lambda i: (jnp.maximum(i * per - 1, 0), 0)),
                  _row_spec(tm, C), pl.BlockSpec((HALO, C), nxt), _row_spec(tm, C), pl.BlockSpec((HALO, C), nxt),
                  _full_spec((HALO, C)), vec, vec],
        out_specs=[_row_spec(tm, C2), _full_spec((HALO, C)), vec, vec, vec],
        out_shape=[jax.ShapeDtypeStruct((T, C2), _MXU_DTYPE), jax.ShapeDtypeStruct((HALO, C), F32)]
                  + [jax.ShapeDtypeStruct((1, C), F32)] * 3,
        scratch_shapes=[pltpu.VMEM((tm + HALO, C), F32), pltpu.VMEM((tm + HALO, C), F32)],
        compiler_params=_params(("arbitrary",), 5 * _nbytes((tm, C2), F32), 12 * _nbytes((tm, C), F32)),
    )(pc, pc, z, z, dc, dc, w, lg, lb)


def _lat_norm_fwd(pl_, gq, gkv, QL, KVL):
    T, W = pl_.shape
    tm = _tile(T, _ROW_TILE, _PACKED_ROWS)

    def body(p_ref, gq_ref, gkv_ref, cq_ref, ckv_ref):
        def norm(xv, gv):
            return xv * lax.rsqrt(jnp.mean(xv * xv, axis=-1, keepdims=True) + EPS) * gv
        cq_ref[...] = norm(p_ref[:, :QL], gq_ref[...]).astype(cq_ref.dtype)
        ckv_ref[...] = norm(p_ref[:, QL:QL + KVL], gkv_ref[...]).astype(ckv_ref.dtype)

    return pl.pallas_call(
        body, name="lat_norm_fwd", grid=(T // tm,),
        in_specs=[_row_spec(tm, W), _full_spec((1, QL)), _full_spec((1, KVL))],
        out_specs=[_row_spec(tm, QL), _row_spec(tm, KVL)],
        out_shape=[jax.ShapeDtypeStruct((T, QL), _MXU_DTYPE), jax.ShapeDtypeStruct((T, KVL), _MXU_DTYPE)],
        compiler_params=_params(("parallel",), 2 * _nbytes((tm, W), F32), 4 * _nbytes((tm, W), F32)),
    )(pl_, gq, gkv)


def _lat_norm_bwd(pl_, gq, gkv, dcq, dckv, drope, QL, KVL):
    T, W = pl_.shape
    tm = _tile(T, _ROW_TILE, _PACKED_ROWS)

    def body(p_ref, gq_ref, gkv_ref, dcq_ref, dckv_ref, dr_ref, dp_ref, dgq_ref, dgkv_ref):
        _init_acc([dgq_ref, dgkv_ref])

        def bwd(xv, gv, dv, dg_ref):
            r = lax.rsqrt(jnp.mean(xv * xv, axis=-1, keepdims=True) + EPS)
            xh = xv * r
            dxh = dv * gv
            dg_ref[...] += _colsum(dv * xh)
            return r * (dxh - xh * jnp.mean(dxh * xh, axis=-1, keepdims=True))

        dp_ref[:, :QL] = bwd(p_ref[:, :QL], gq_ref[...], dcq_ref[...], dgq_ref).astype(dp_ref.dtype)
        dp_ref[:, QL:QL + KVL] = bwd(p_ref[:, QL:QL + KVL], gkv_ref[...], dckv_ref[...], dgkv_ref).astype(dp_ref.dtype)
        dp_ref[:, QL + KVL:] = dr_ref[...].astype(dp_ref.dtype)

    return pl.pallas_call(
        body, name="lat_norm_bwd", grid=(T // tm,),
        in_specs=[_row_spec(tm, W), _full_spec((1, QL)), _full_spec((1, KVL)), _row_spec(tm, QL), _row_spec(tm, KVL),
                  _row_spec(tm, HEAD_W)],
        out_specs=[_row_spec(tm, W), _full_spec((1, QL)), _full_spec((1, KVL))],
        out_shape=[jax.ShapeDtypeStruct((T, W), _MXU_DTYPE), jax.ShapeDtypeStruct((1, QL), F32),
                   jax.ShapeDtypeStruct((1, KVL), F32)],
        compiler_params=_params(("arbitrary",), 4 * _nbytes((tm, W), F32), 6 * _nbytes((tm, W), F32)),
    )(pl_, gq, gkv, dcq, dckv, drope)


def _norm_rope(xv, gv, ct, sa, sb):
    r = lax.rsqrt(jnp.sum(xv * xv, axis=-1, keepdims=True) * (1.0 / QK_DIM) + EPS)
    xn = xv * r * gv
    half = ROPE_DIM // 2
    return xn * ct + pltpu.roll(xn, HEAD_W - half, 1) * sa + pltpu.roll(xn, half, 1) * sb


def _norm_rope_bwd(xv, gv, ct, sa, sb, dout):
    half = ROPE_DIM // 2
    dxn = dout * ct + pltpu.roll(dout * sa, half, 1) + pltpu.roll(dout * sb, HEAD_W - half, 1)
    r = lax.rsqrt(jnp.sum(xv * xv, axis=-1, keepdims=True) * (1.0 / QK_DIM) + EPS)
    xh = xv * r
    dxh = dxn * gv
    dx = r * (dxh - xh * (jnp.sum(dxh * xh, axis=-1, keepdims=True) * (1.0 / QK_DIM)))
    return dx, _colsum(dxn * xh)


def _qk_prep_fwd(q_raw, k_raw, pl_, rope_cb, ct, sa, sb, gq, gk):
    T, HW = q_raw.shape
    H = HW // HEAD_W
    tm = _tile(T, _ROW_TILE, _PACKED_ROWS)

    def body(q_ref, k_ref, r_ref, ct_ref, sa_ref, sb_ref, gq_ref, gk_ref, qn_ref, kn_ref):
        ctv, sav, sbv = ct_ref[...], sa_ref[...], sb_ref[...]
        qn_ref[...] = _norm_rope(q_ref[...], gq_ref[...], ctv, sav, sbv).astype(qn_ref.dtype)
        kn_ref[...] = _norm_rope(k_ref[...] + r_ref[...], gk_ref[...], ctv, sav, sbv).astype(kn_ref.dtype)

    hb = pl.BlockSpec((tm, HEAD_W), lambda i, h: (i, h))
    tb = pl.BlockSpec((tm, HEAD_W), lambda i, h: (i, 0))
    gb = pl.BlockSpec((1, HEAD_W), lambda i, h: (0, 0))
    return pl.pallas_call(
        body, name="qk_prep_fwd", grid=(T // tm, H),
        in_specs=[hb, hb, pl.BlockSpec((tm, HEAD_W), lambda i, h: (i, rope_cb)), tb, tb, tb, gb, gb],
        out_specs=[hb, hb],
        out_shape=[jax.ShapeDtypeStruct((T, HW), _MXU_DTYPE)] * 2,
        compiler_params=_params(("parallel", "parallel"), 8 * _nbytes((tm, HEAD_W), F32), 8 * _nbytes((tm, HEAD_W), F32)),
    )(q_raw, k_raw, pl_, ct, sa, sb, gq, gk)


def _qk_prep_bwd(q_raw, k_raw, pl_, rope_cb, ct, sa, sb, gq, gk, dqn, dkn):
    T, HW = q_raw.shape
    H = HW // HEAD_W
    tm = _tile(T, _ROW_TILE, _PACKED_ROWS)

    def body(q_ref, k_ref, r_ref, ct_ref, sa_ref, sb_ref, gq_ref, gk_ref, dqn_ref, dkn_ref,
             dq_ref, dk_ref, dr_ref, dgq_ref, dgk_ref):
        i, h = pl.program_id(0), pl.program_id(1)

        @pl.when((i == 0) & (h == 0))
        def _():
            dgq_ref[...] = jnp.zeros_like(dgq_ref)
            dgk_ref[...] = jnp.zeros_like(dgk_ref)

        ctv, sav, sbv = ct_ref[...], sa_ref[...], sb_ref[...]
        dq, dgq = _norm_rope_bwd(q_ref[...], gq_ref[...], ctv, sav, sbv, dqn_ref[...])
        dk, dgk = _norm_rope_bwd(k_ref[...] + r_ref[...], gk_ref[...], ctv, sav, sbv, dkn_ref[...])
        dq_ref[...] = dq.astype(dq_ref.dtype)
        dk_ref[...] = dk.astype(dk_ref.dtype)
        dgq_ref[...] += dgq
        dgk_ref[...] += dgk

        @pl.when(h == 0)
        def _():
            dr_ref[...] = dk

        @pl.when(h > 0)
        def _():
            dr_ref[...] += dk

    hb = pl.BlockSpec((tm, HEAD_W), lambda i, h: (i, h))
    tb = pl.BlockSpec((tm, HEAD_W), lambda i, h: (i, 0))
    gb = pl.BlockSpec((1, HEAD_W), lambda i, h: (0, 0))
    return pl.pallas_call(
        body, name="qk_prep_bwd", grid=(T // tm, H),
        in_specs=[hb, hb, pl.BlockSpec((tm, HEAD_W), lambda i, h: (i, rope_cb)), tb, tb, tb, gb, gb, hb, hb],
        out_specs=[hb, hb, tb, gb, gb],
        out_shape=[jax.ShapeDtypeStruct((T, HW), _MXU_DTYPE)] * 2 + [jax.ShapeDtypeStruct((T, HEAD_W), F32)]
                  + [jax.ShapeDtypeStruct((1, HEAD_W), F32)] * 2,
        compiler_params=_params(("arbitrary", "arbitrary"), 13 * _nbytes((tm, HEAD_W), F32), 12 * _nbytes((tm, HEAD_W), F32)),
    )(q_raw, k_raw, pl_, ct, sa, sb, gq, gk, dqn, dkn)


_NEG = -1e30
_LOG2E = 1.4426950408889634
_SCORE_C = QK_DIM ** -0.5 * _LOG2E


def _pieces(t, rb, diag):
    assert CHUNK % rb == 0 and rb % _SUBLANES == 0
    out = []
    for r in range(t // rb):
        lo = (r * rb // CHUNK) * CHUNK if diag else 0
        for c in range(t // _LANES):
            cut = None if (c + 1) * _LANES <= lo else max(lo - c * _LANES, 0)
            out.append((slice(r * rb, (r + 1) * rb), slice(c * _LANES, (c + 1) * _LANES), c, cut))
    return out


def _groups(x):
    return x.reshape(x.shape[0] // _SUBLANES, _SUBLANES, x.shape[1])


def _all_sublanes(x8, op):
    return jnp.broadcast_to(op(x8, axis=0, keepdims=True), x8.shape)


def _lane_ge(rb, cut):
    return lax.broadcasted_iota(jnp.int32, (rb, _LANES), 1) >= cut


def _attn_fwd(q, k, v):
    T, HW = q.shape
    H = HW // HEAD_W
    t = _tile(T, _ATT_TILE)
    rb = min(_ATT_ROWS, t)
    nc = t // _LANES

    def body(q_ref, k_ref, v_ref, o_ref, lse_ref, s_s, p_s, m_s, l_s, acc_s):
        i = pl.program_id(1)
        qv = q_ref[...]
        m_s[...] = jnp.full_like(m_s, _NEG)
        l_s[...] = jnp.zeros_like(l_s)
        acc_s[...] = jnp.zeros_like(acc_s)

        def step(j, diag):
            rows = pl.ds(pl.multiple_of(j * t, t), t)
            s_s[...] = _dot(k_ref[rows, :], qv, 'nt')
            pieces = _pieces(t, rb, diag)

            def scores(rs, cs, cut):
                sb = s_s[rs, cs]
                return jnp.where(_lane_ge(rb, cut), sb, _NEG) if cut else sb

            mx = [None] * nc
            for rs, cs, c, cut in pieces:
                if cut is not None:
                    g = jnp.max(_groups(scores(rs, cs, cut)), axis=0)
                    mx[c] = g if mx[c] is None else jnp.maximum(mx[c], g)
            m_new, alpha = [], []
            for c in range(nc):
                cs = slice(c * _LANES, (c + 1) * _LANES)
                m_prev = m_s[:, cs]
                m_new.append(jnp.maximum(m_prev, _all_sublanes(mx[c], jnp.max)))
                alpha.append(jnp.exp2((m_prev - m_new[c]) * _SCORE_C))
                m_s[:, cs] = m_new[c]
            lsum = [jnp.zeros((_SUBLANES, _LANES), F32)] * nc
            for rs, cs, c, cut in pieces:
                if cut is None:
                    p_s[rs, cs] = jnp.zeros((rb, _LANES), p_s.dtype)
                    continue
                p = jnp.exp2((scores(rs, cs, cut) - jnp.tile(m_new[c], (rb // _SUBLANES, 1))) * _SCORE_C)
                lsum[c] = lsum[c] + jnp.sum(_groups(p), axis=0)
                p_s[rs, cs] = p.astype(p_s.dtype)
            for c in range(nc):
                cs = slice(c * _LANES, (c + 1) * _LANES)
                l_s[:, cs] = alpha[c] * l_s[:, cs] + _all_sublanes(lsum[c], jnp.sum)
            a = jnp.tile(jnp.concatenate(alpha, axis=1), (HEAD_W // _SUBLANES, 1))
            acc_s[...] = a * acc_s[...] + _dot(v_ref[rows, :], p_s[...], 'tn')

        def loop_body(j, carry):
            step(j, False)
            return carry

        lax.fori_loop(0, i, loop_body, 0)
        step(i, True)
        l = l_s[...]
        o_t = acc_s[...] / jnp.tile(l, (HEAD_W // _SUBLANES, 1))
        o_ref[...] = o_t.T.astype(o_ref.dtype)
        lse_ref[...] = m_s[...] * _SCORE_C + jnp.log(l) * _LOG2E

    qb = pl.BlockSpec((t, HEAD_W), lambda h, i: (i, h))
    kb = pl.BlockSpec((T, HEAD_W), lambda h, i: (0, h))
    st = pltpu.VMEM((_SUBLANES, t), F32)
    return pl.pallas_call(
        body, name="attn_fwd", grid=(H, T // t),
        in_specs=[qb, kb, kb], out_specs=[qb, pl.BlockSpec((_SUBLANES, t), lambda h, i: (h, i))],
        out_shape=[jax.ShapeDtypeStruct((T, HW), _MXU_DTYPE), jax.ShapeDtypeStruct((H * _SUBLANES, T), F32)],
        scratch_shapes=[pltpu.VMEM((t, t), F32), pltpu.VMEM((t, t), _MXU_DTYPE), st, st, pltpu.VMEM((HEAD_W, t), F32)],
        compiler_params=_params(("parallel", "arbitrary"), 2 * _nbytes((T, HEAD_W), _MXU_DTYPE) + 3 * _nbytes((t, HEAD_W), F32),
                                4 * _nbytes((t, t), F32)),
    )(q, k, v)


def _attn_bwd(q, k, v, o, do, lse2):
    T, HW = q.shape
    H = HW // HEAD_W
    t = _tile(T, _ATT_TILE)
    n = T // t
    rb = min(_ATT_ROWS, t)
    scale = QK_DIM ** -0.5

    def body(k_ref, v_ref, q_ref, o_ref, do_ref, lse_ref, dq_ref, dk_ref, dv_ref,
             s_s, dp_s, p_s, ds_s, dk_s, dv_s, delta_s, dqt_s):
        j = pl.program_id(1)

        @pl.when(j == 0)
        def _():
            dqt_s[...] = jnp.zeros_like(dqt_s)
            for b in range(n):
                rs = slice(b * t, (b + 1) * t)
                d = jnp.sum(do_ref[rs, :].astype(F32) * o_ref[rs, :].astype(F32), axis=1, keepdims=True)
                delta_s[:, rs] = jnp.broadcast_to(d, (t, HEAD_W)).T[0:_SUBLANES, :]

        kv, vv = k_ref[...], v_ref[...]
        dk_s[...] = jnp.zeros_like(dk_s)
        dv_s[...] = jnp.zeros_like(dv_s)

        def step(i, diag):
            base = pl.multiple_of(i * t, t)
            qv, dov = q_ref[pl.ds(base, t), :], do_ref[pl.ds(base, t), :]
            s_s[...] = _dot(kv, qv, 'nt')
            dp_s[...] = _dot(vv, dov, 'nt')
            for rs, cs, c, cut in _pieces(t, rb, diag):
                if cut is None:
                    p_s[rs, cs] = jnp.zeros((rb, _LANES), p_s.dtype)
                    ds_s[rs, cs] = jnp.zeros((rb, _LANES), ds_s.dtype)
                    continue
                lanes = pl.ds(pl.multiple_of(base + c * _LANES, _LANES), _LANES)
                lse = jnp.tile(lse_ref[:, lanes], (rb // _SUBLANES, 1))
                dl = jnp.tile(delta_s[:, lanes], (rb // _SUBLANES, 1))
                p = jnp.exp2(s_s[rs, cs] * _SCORE_C - lse)
                if cut:
                    p = jnp.where(_lane_ge(rb, cut), p, 0.0)
                p_s[rs, cs] = p.astype(p_s.dtype)
                ds_s[rs, cs] = (p * (dp_s[rs, cs] - dl)).astype(ds_s.dtype)
            dsv = ds_s[...]
            dv_s[...] += _dot(p_s[...], dov, 'nn')
            dk_s[...] += _dot(dsv, qv, 'nn')
            dqt_s[:, pl.ds(base, t)] += _dot(kv, dsv, 'tn')

        def loop_body(i, carry):
            step(i, False)
            return carry

        step(j, True)
        lax.fori_loop(j + 1, n, loop_body, 0)
        dk_ref[...] = dk_s[...] * scale
        dv_ref[...] = dv_s[...].astype(dv_ref.dtype)

        @pl.when(j == n - 1)
        def _():
            for b in range(n):
                rs = slice(b * t, (b + 1) * t)
                dq_ref[rs, :] = dqt_s[:, rs].T * scale

    jb = pl.BlockSpec((t, HEAD_W), lambda h, j: (j, h))
    fb = pl.BlockSpec((T, HEAD_W), lambda h, j: (0, h))
    sc = pltpu.VMEM((t, HEAD_W), F32)
    return pl.pallas_call(
        body, name="attn_bwd", grid=(H, n),
        in_specs=[jb, jb, fb, fb, fb, pl.BlockSpec((_SUBLANES, T), lambda h, j: (h, 0))], out_specs=[fb, jb, jb],
        out_shape=[jax.ShapeDtypeStruct((T, HW), F32), jax.ShapeDtypeStruct((T, HW), F32),
                   jax.ShapeDtypeStruct((T, HW), _MXU_DTYPE)],
        scratch_shapes=[pltpu.VMEM((t, t), F32), pltpu.VMEM((t, t), F32), pltpu.VMEM((t, t), _MXU_DTYPE),
                        pltpu.VMEM((t, t), _MXU_DTYPE), sc, sc, pltpu.VMEM((_SUBLANES, T), F32),
                        pltpu.VMEM((HEAD_W, T), F32)],
        compiler_params=_params(("parallel", "arbitrary"), 3 * _nbytes((T, HEAD_W), _MXU_DTYPE) + _nbytes((T, HEAD_W), F32)
                                + 4 * _nbytes((t, HEAD_W), F32), 4 * _nbytes((t, t), F32) + _nbytes((T, HEAD_W), F32)),
    )(k, v, q, o, do, lse2)


def _gate_fwd(pg, gb, yc, ym):
    T, D = yc.shape
    tm = _tile(T, _ROW_TILE, _PACKED_ROWS)

    def body(g0_ref, g1_ref, gb_ref, yc_ref, ym_ref, y_ref):
        s0 = _sig(g0_ref[...] + gb_ref[0:1, :])
        s1 = _sig(g1_ref[...] + gb_ref[1:2, :])
        y_ref[...] = (s0 * yc_ref[...] + s1 * ym_ref[...]).astype(y_ref.dtype)

    return pl.pallas_call(
        body, name="gate_fwd", grid=(T // tm,),
        in_specs=[_row_spec(tm, D, 0), _row_spec(tm, D, 1), _full_spec((2, D)), _row_spec(tm, D), _row_spec(tm, D)],
        out_specs=_row_spec(tm, D),
        out_shape=jax.ShapeDtypeStruct((T, D), _MXU_DTYPE),
        compiler_params=_params(("parallel",), 5 * _nbytes((tm, D), F32), 4 * _nbytes((tm, D), F32)),
    )(pg, pg, gb, yc, ym)


def _gate_bwd(pg, gb, yc, ym, dy):
    T, D = yc.shape
    tm = _tile(T, _ROW_TILE, _PACKED_ROWS)

    def body(g0_ref, g1_ref, gb_ref, yc_ref, ym_ref, dy_ref, dyc_ref, dym_ref, dpg_ref, dgb_ref):
        _init_acc([dgb_ref])
        dyv = dy_ref[...]
        s0 = _sig(g0_ref[...] + gb_ref[0:1, :])
        s1 = _sig(g1_ref[...] + gb_ref[1:2, :])
        dyc_ref[...] = (dyv * s0).astype(dyc_ref.dtype)
        dym_ref[...] = (dyv * s1).astype(dym_ref.dtype)
        d0 = dyv * yc_ref[...] * s0 * (1.0 - s0)
        d1 = dyv * ym_ref[...] * s1 * (1.0 - s1)
        dpg_ref[:, :D] = d0.astype(dpg_ref.dtype)
        dpg_ref[:, D:] = d1.astype(dpg_ref.dtype)
        dgb_ref[0:1, :] += _colsum(d0)
        dgb_ref[1:2, :] += _colsum(d1)

    return pl.pallas_call(
        body, name="gate_bwd", grid=(T // tm,),
        in_specs=[_row_spec(tm, D, 0), _row_spec(tm, D, 1), _full_spec((2, D)), _row_spec(tm, D), _row_spec(tm, D),
                  _row_spec(tm, D)],
        out_specs=[_row_spec(tm, D), _row_spec(tm, D), _row_spec(tm, 2 * D), _full_spec((2, D))],
        out_shape=[jax.ShapeDtypeStruct((T, D), _MXU_DTYPE)] * 2 + [jax.ShapeDtypeStruct((T, 2 * D), _MXU_DTYPE),
                                                                   jax.ShapeDtypeStruct((2, D), F32)],
        compiler_params=_params(("arbitrary",), 7 * _nbytes((tm, D), F32), 8 * _nbytes((tm, D), F32)),
    )(pg, pg, gb, yc, ym, dy)


def _place():
    return lax.axis_index("x"), lax.axis_index("y"), lax.axis_index("c")


def _all_gather(name, xs):
    R, C = xs.shape
    MESH = pl.DeviceIdType.MESH

    def body(x_ref, out_ref, send_sems, recv_sems, local_sem):
        x, y, c = _place()
        me, sibling = (x, y, c), (x, y, 1 - c)
        chips = [(1 - x, y), (x, 1 - y), (1 - x, 1 - y)]

        def rows(px, py, pc):
            return out_ref.at[4 * px + 2 * py + pc]

        def copy(k, block, to, src=None):
            return pltpu.make_async_remote_copy(
                src_ref=rows(*block) if src is None else src, dst_ref=rows(*block),
                send_sem=send_sems.at[k], recv_sem=recv_sems.at[k], device_id=to, device_id_type=MESH)

        mine = pltpu.make_async_copy(x_ref, rows(*me), local_sem)
        mine.start()
        first = [copy(0, me, sibling, src=x_ref)]
        first += [copy(1 + j, me, (*chip, c), src=x_ref) for j, chip in enumerate(chips)]
        for cp in first:
            cp.start()
        passed = [copy(4 + j, (*chip, c), sibling) for j, chip in enumerate(chips)]
        for j, chip in enumerate(chips):
            copy(1 + j, (*chip, c), me).wait_recv()
            passed[j].start()
        copy(0, sibling, me).wait_recv()
        for j, chip in enumerate(chips):
            copy(4 + j, (*chip, 1 - c), me).wait_recv()
        for cp in first + passed:
            cp.wait_send()
        mine.wait()

    return pl.pallas_call(
        body, name=name,
        out_shape=jax.ShapeDtypeStruct((N_DEV, R, C), xs.dtype),
        in_specs=[pl.BlockSpec(memory_space=pl.ANY)], out_specs=pl.BlockSpec(memory_space=pl.ANY),
        scratch_shapes=[pltpu.SemaphoreType.DMA((7,)), pltpu.SemaphoreType.DMA((7,)), pltpu.SemaphoreType.DMA],
    )(xs)


def _pair_exchange(name, xs):
    _, G, R, C = xs.shape

    def body(x_ref, out_ref, send_sem, recv_sem):
        x, y, c = _place()
        cp = pltpu.make_async_remote_copy(
            src_ref=x_ref.at[1 - c], dst_ref=out_ref, send_sem=send_sem, recv_sem=recv_sem,
            device_id=(x, y, 1 - c), device_id_type=pl.DeviceIdType.MESH)
        cp.start()
        cp.wait()

    return pl.pallas_call(
        body, name=name,
        out_shape=jax.ShapeDtypeStruct((G, R, C), xs.dtype),
        in_specs=[pl.BlockSpec(memory_space=pl.ANY)], out_specs=pl.BlockSpec(memory_space=pl.ANY),
        scratch_shapes=[pltpu.SemaphoreType.DMA, pltpu.SemaphoreType.DMA],
    )(xs)


def _pair_sum(name, xs, got):
    _, G, R, C = xs.shape
    tr = _tile(R, _PACK_ROW_MULT, _PACKED_ROWS)

    def body(c_ref, a_ref, b_ref, o_ref):
        o_ref[...] = (a_ref[...].astype(F32) + b_ref[...].astype(F32)).astype(o_ref.dtype)

    blk = pl.BlockSpec((None, tr, C), lambda g, i, c_ref: (g, i, 0))
    return pl.pallas_call(
        body, name=name,
        grid_spec=pltpu.PrefetchScalarGridSpec(
            num_scalar_prefetch=1, grid=(G, R // tr),
            in_specs=[pl.BlockSpec((None, None, tr, C), lambda g, i, c_ref: (c_ref[0], g, i, 0)), blk],
            out_specs=blk),
        out_shape=jax.ShapeDtypeStruct((G, R, C), xs.dtype),
        compiler_params=_params(("parallel", "parallel"), 3 * _nbytes((tr, C), F32), 2 * _nbytes((tr, C), F32)),
    )(lax.axis_index("c").reshape(1).astype(jnp.int32), xs, got)


def _chip_all_to_all(name, xs):
    G, R, C = xs.shape

    def body(x_ref, out_ref, send_sems, recv_sems, local_sem):
        x, y, c = _place()
        me = 2 * x + y
        mine = pltpu.make_async_copy(x_ref.at[me], out_ref.at[me], local_sem)
        mine.start()
        copies = []
        for k in range(1, G):
            px = 1 - x if k & 2 else x
            py = 1 - y if k & 1 else y
            copies.append(pltpu.make_async_remote_copy(
                src_ref=x_ref.at[2 * px + py], dst_ref=out_ref.at[me],
                send_sem=send_sems.at[k - 1], recv_sem=recv_sems.at[k - 1],
                device_id=(px, py, c), device_id_type=pl.DeviceIdType.MESH))
        for cp in copies:
            cp.start()
        for cp in copies:
            cp.wait()
        mine.wait()

    return pl.pallas_call(
        body, name=name,
        out_shape=jax.ShapeDtypeStruct((G, R, C), xs.dtype),
        in_specs=[pl.BlockSpec(memory_space=pl.ANY)], out_specs=pl.BlockSpec(memory_space=pl.ANY),
        scratch_shapes=[pltpu.SemaphoreType.DMA((G - 1,)), pltpu.SemaphoreType.DMA((G - 1,)), pltpu.SemaphoreType.DMA],
    )(xs)


def _sum_blocks(name, parts, scale=None):
    n, R, C = parts.shape
    tr = _tile(R, _PACK_ROW_MULT, _PACKED_ROWS)

    def body(p_ref, o_ref):
        acc = p_ref[0].astype(F32)
        for s in range(1, n):
            acc = acc + p_ref[s].astype(F32)
        if scale is not None:
            acc = acc * scale
        o_ref[...] = acc

    return pl.pallas_call(
        body, name=name, grid=(R // tr,),
        in_specs=[pl.BlockSpec((n, tr, C), lambda i: (0, i, 0))], out_specs=pl.BlockSpec((tr, C), lambda i: (i, 0)),
        out_shape=jax.ShapeDtypeStruct((R, C), F32),
        compiler_params=_params(("parallel",), _nbytes((n, tr, C), parts.dtype) + _nbytes((tr, C), F32), 2 * _nbytes((tr, C), F32)),
    )(parts)


def _adamw(w, g, m, v):
    shape = w.shape
    w2, g2, m2, v2 = (t.reshape(-1, shape[-1]) for t in (w, g, m, v))
    R, C = w2.shape
    tr = _tile(R, _ROW_TILE, _SUBLANES)

    def body(w_ref, g_ref, m_ref, v_ref, d_ref, nm_ref, nv_ref):
        gv = g_ref[...]
        mv = ADAM_B1 * m_ref[...] + (1.0 - ADAM_B1) * gv
        vv = ADAM_B2 * v_ref[...] + (1.0 - ADAM_B2) * (gv * gv)
        m_hat = mv / (1.0 - ADAM_B1 ** ADAM_STEP)
        v_hat = vv / (1.0 - ADAM_B2 ** ADAM_STEP)
        d_ref[...] = -ADAM_LR * (m_hat / (jnp.sqrt(v_hat) + ADAM_EPS) + ADAM_WD * w_ref[...])
        nm_ref[...] = mv
        nv_ref[...] = vv

    spec = pl.BlockSpec((tr, C), lambda i: (i, 0))
    outs = pl.pallas_call(
        body, name="adamw", grid=(R // tr,),
        in_specs=[spec] * 4, out_specs=[spec] * 3,
        out_shape=[jax.ShapeDtypeStruct((R, C), F32)] * 3,
        compiler_params=_params(("parallel",), 7 * _nbytes((tr, C), F32), 4 * _nbytes((tr, C), F32)),
    )(w2, g2, m2, v2)
    return tuple(o.reshape(shape) for o in outs)


def _pack_rows(parts, lead=()):
    out, spans, r0 = [], [], 0
    unit = _PACKED_ROWS * _PACK_W
    for p in parts:
        flat = p.reshape(lead + (-1,))
        size = flat.shape[-1]
        pad = (-size) % unit
        if pad:
            flat = jnp.pad(flat, [(0, 0)] * len(lead) + [(0, pad)])
        rows = (size + pad) // _PACK_W
        out.append(flat.reshape(lead + (rows, _PACK_W)))
        spans.append((r0, rows, p.shape[len(lead):]))
        r0 += rows
    pad = (-r0) % _PACK_ROW_MULT
    if pad:
        out.append(jnp.zeros(lead + (pad, _PACK_W), out[0].dtype))
    return jnp.concatenate(out, axis=len(lead)), spans


def _unpack_rows(packed, span, lead=()):
    r0, rows, shape = span
    size = 1
    for s in shape:
        size *= s
    sl = lax.slice_in_dim(packed, r0, r0 + rows, axis=len(lead))
    return sl.reshape(lead + (-1,))[..., :size].reshape(lead + tuple(shape))


def _rope_tables(positions, T):
    inv_freq = ROPE_THETA ** (-jnp.arange(0, ROPE_DIM, 2, dtype=F32) / ROPE_DIM)
    ang = positions.reshape(T, 1).astype(F32) * inv_freq
    cos, sin = jnp.cos(ang), jnp.sin(ang)
    half = ROPE_DIM // 2
    z = lambda n: jnp.zeros((T, n), F32)
    tail = HEAD_W - QK_DIM
    ct = jnp.concatenate([jnp.ones((T, NOPE_DIM), F32), cos, cos, z(tail)], axis=1)
    sa = jnp.concatenate([z(NOPE_DIM), -sin, z(half), z(tail)], axis=1)
    sb = jnp.concatenate([z(NOPE_DIM), z(half), sin, z(tail)], axis=1)
    return ct, sa, sb


def _pad_heads(wt, per_head, keep_from=0, keep=None):
    K = wt.shape[1]
    keep = per_head if keep is None else keep
    w3 = wt.reshape(N_HEADS, per_head, K)[:, keep_from:keep_from + keep]
    return jnp.pad(w3, ((0, 0), (0, HEAD_W - keep), (0, 0))).reshape(N_HEADS * HEAD_W, K)


def _unpad_heads(g, keep):
    return g.reshape(N_HEADS, HEAD_W, g.shape[1])[:, :keep]


def _layer_weights(W):
    D = W['w_out'].shape[1]
    C = W['w_conv_out'].shape[1]
    QL, KVL = W['w_uq'].shape[1], W['w_ukv'].shape[1]
    o1 = 2 * C
    o2, o3 = o1 + QL, o1 + QL + KVL
    o4 = o3 + ROPE_DIM
    win = W['w_in']
    zr = lambda n: jnp.zeros((n, D), win.dtype)
    L = dict(W)
    L['w_c'] = win[:o1]
    L['w_l'] = jnp.concatenate([win[o1:o3], zr(NOPE_DIM), win[o3:o4], zr(HEAD_W - QK_DIM)], axis=0)
    L['w_g'] = win[o4:]
    L['w_q'] = _pad_heads(W['w_uq'], QK_DIM)
    L['w_k'] = _pad_heads(W['w_ukv'], NOPE_DIM + V_DIM, 0, NOPE_DIM)
    L['w_v'] = _pad_heads(W['w_ukv'], NOPE_DIM + V_DIM, NOPE_DIM, V_DIM)
    wmo = W['w_mla_out'].reshape(D, N_HEADS, V_DIM)
    L['w_mo'] = jnp.pad(wmo, ((0, 0), (0, 0), (0, HEAD_W - V_DIM))).reshape(D, N_HEADS * HEAD_W)
    L['dims'] = (D, C, QL, KVL)
    return L


def _row(v, width=None):
    v = v.reshape(1, -1)
    if width is not None and v.shape[1] < width:
        v = jnp.pad(v, ((0, 0), (0, width - v.shape[1])))
    return v


def _ffn_fwd(x, g, wg, wu, wd):
    h = _rms_fwd(x, _row(g))
    a, b, u = _ffn_up(h, wg, wu)
    x_out = _mm("ffn_down", u, wd, 'nn', res=x, scale=0.5)
    return x_out, (x, h, a, b, u)


def _ffn_bwd(dx, saved, g, wg, wu, wd):
    x, h, a, b, u = saved
    d_wd = _mm("ffn_dwd", u, dx, 'tn', scale=0.5)
    da, db = _ffn_bwd_up(dx, wd, a, b)
    dh = _mm("ffn_dh_g", da, wg, 'nn')
    dh = _mm("ffn_dh_u", db, wu, 'nn', res=dh)
    d_wg = _mm("ffn_dwg", da, h, 'tn')
    d_wu = _mm("ffn_dwg", db, h, 'tn')
    dx_in, dg = _rms_bwd(x, _row(g), dh, dx)
    return dx_in, dg, d_wg, d_wu, d_wd


def _layer_fwd(x, L, S, tabs):
    D, C, QL, KVL = L['dims']
    rope_cb = (QL + KVL) // HEAD_W
    x1, ffn1 = _ffn_fwd(x, S['ffn1_norm'], L['ffn1_w_gate'], L['ffn1_w_up'], L['ffn1_w_down'])
    h = _rms_fwd(x1, _row(S['mix_norm']))
    pc = _mm("proj_c", h, L['w_c'], 'nt')
    pl_ = _mm("proj_l", h, L['w_l'], 'nt')
    pg = _mm("proj_g", h, L['w_g'], 'nt')
    conv_w = jnp.pad(S['conv_w'], ((0, HALO - CONV_WIDTH), (0, 0)))
    z, c = _conv_fwd(pc, conv_w, _row(S['conv_b']), _row(S['conv_ln_g']), _row(S['conv_ln_b']))
    yc = _mm("conv_out", c, L['w_conv_out'], 'nt')
    cqn, ckvn = _lat_norm_fwd(pl_, _row(S['cq_norm']), _row(S['ckv_norm']), QL, KVL)
    q_raw = _mm("up_q", cqn, L['w_q'], 'nt')
    k_raw = _mm("up_kv", ckvn, L['w_k'], 'nt')
    v = _mm("up_kv", ckvn, L['w_v'], 'nt', out_dtype=_MXU_DTYPE)
    gq, gk = _row(S['q_norm'], HEAD_W), _row(S['k_norm'], HEAD_W)
    qn, kn = _qk_prep_fwd(q_raw, k_raw, pl_, rope_cb, *tabs, gq, gk)
    o, lse = _attn_fwd(qn, kn, v)
    ym = _mm("mla_out", o, L['w_mo'], 'nt')
    y = _gate_fwd(pg, S['gate_bias'], yc, ym)
    x2 = _mm("mix_out", y, L['w_out'], 'nn', res=x1)
    x3, ffn2 = _ffn_fwd(x2, S['ffn2_norm'], L['ffn2_w_gate'], L['ffn2_w_up'], L['ffn2_w_down'])
    saved = dict(ffn1=ffn1, ffn2=ffn2, x1=x1, h=h, pc=pc, pl=pl_, pg=pg, z=z, c=c, yc=yc, cqn=cqn, ckvn=ckvn,
                 q_raw=q_raw, k_raw=k_raw, v=v, qn=qn, kn=kn, o=o, lse=lse, ym=ym, y=y, conv_w=conv_w, gq=gq, gk=gk)
    return x3, saved


def _layer_bwd(dx, L, S, tabs, A):
    D, C, QL, KVL = L['dims']
    rope_cb = (QL + KVL) // HEAD_W
    G, g = {}, {}
    dx, g['ffn2_norm'], G['ffn2_w_gate'], G['ffn2_w_up'], G['ffn2_w_down'] = _ffn_bwd(
        dx, A['ffn2'], S['ffn2_norm'], L['ffn2_w_gate'], L['ffn2_w_up'], L['ffn2_w_down'])
    G['w_out'] = _mm("mix_dwout", A['y'], dx, 'tn')
    dy = _mm("mix_dy", dx, L['w_out'], 'nt')
    dyc, dym, dpg, g['gate_bias'] = _gate_bwd(A['pg'], S['gate_bias'], A['yc'], A['ym'], dy)
    G['w_conv_out'] = _mm("conv_dwout", dyc, A['c'], 'tn')
    dc = _mm("conv_dc", dyc, L['w_conv_out'], 'nn')
    dpc, dcw, g['conv_b'], g['conv_ln_g'], g['conv_ln_b'] = _conv_bwd(
        A['pc'], A['z'], dc, A['conv_w'], _row(S['conv_ln_g']), _row(S['conv_ln_b']))
    g['conv_w'] = dcw[:CONV_WIDTH]
    d_wmo = _mm("mla_dwout", dym, A['o'], 'tn')
    G['w_mla_out'] = d_wmo.reshape(D, N_HEADS, HEAD_W)[:, :, :V_DIM].reshape(D, N_HEADS * V_DIM)
    do = _mm("mla_do", dym, L['w_mo'], 'nn', out_dtype=_MXU_DTYPE)
    dqn, dkn, dv = _attn_bwd(A['qn'], A['kn'], A['v'], A['o'], do, A['lse'])
    dq_raw, dk_raw, drope, dgq, dgk = _qk_prep_bwd(A['q_raw'], A['k_raw'], A['pl'], rope_cb, *tabs, A['gq'], A['gk'], dqn, dkn)
    g['q_norm'], g['k_norm'] = dgq[:, :QK_DIM], dgk[:, :QK_DIM]
    G['w_uq'] = _unpad_heads(_mm("up_dwq", dq_raw, A['cqn'], 'tn'), QK_DIM).reshape(N_HEADS * QK_DIM, QL)
    d_wk = _unpad_heads(_mm("up_dwkv", dk_raw, A['ckvn'], 'tn'), NOPE_DIM)
    d_wv = _unpad_heads(_mm("up_dwkv", dv, A['ckvn'], 'tn'), V_DIM)
    G['w_ukv'] = jnp.concatenate([d_wk, d_wv], axis=1).reshape(N_HEADS * (NOPE_DIM + V_DIM), KVL)
    dcq = _mm("up_dcq", dq_raw, L['w_q'], 'nn')
    dckv = _mm("up_dckv_k", dk_raw, L['w_k'], 'nn')
    dckv = _mm("up_dckv_v", dv, L['w_v'], 'nn', res=dckv)
    dpl, g['cq_norm'], g['ckv_norm'] = _lat_norm_bwd(A['pl'], _row(S['cq_norm']), _row(S['ckv_norm']), dcq, dckv, drope, QL, KVL)
    d_wc = _mm("proj_dwc", dpc, A['h'], 'tn')
    d_wl = _mm("proj_dwl", dpl, A['h'], 'tn')
    d_wg = _mm("proj_dwg", dpg, A['h'], 'tn')
    ql = QL + KVL
    G['w_in'] = jnp.concatenate([d_wc, d_wl[:ql], d_wl[ql + NOPE_DIM:ql + NOPE_DIM + ROPE_DIM], d_wg], axis=0)
    dh = _mm("proj_dh_c", dpc, L['w_c'], 'nn')
    dh = _mm("proj_dh_l", dpl, L['w_l'], 'nn', res=dh)
    dh = _mm("proj_dh_g", dpg, L['w_g'], 'nn', res=dh)
    dx, g['mix_norm'] = _rms_bwd(A['x1'], _row(S['mix_norm']), dh, dx)
    dx, g['ffn1_norm'], G['ffn1_w_gate'], G['ffn1_w_up'], G['ffn1_w_down'] = _ffn_bwd(
        dx, A['ffn1'], S['ffn1_norm'], L['ffn1_w_gate'], L['ffn1_w_up'], L['ffn1_w_down'])
    return dx, G, g


def kernel(x, positions, ffn1_norm, ffn1_w_gate, ffn1_w_up, ffn1_w_down, mix_norm, w_in, gate_bias, conv_w, conv_b, conv_ln_g, conv_ln_b, w_conv_out, cq_norm, ckv_norm, w_uq, w_ukv, q_norm, k_norm, w_mla_out, w_out, ffn2_norm, ffn2_w_gate, ffn2_w_up, ffn2_w_down, loss_target, m_ffn1_norm, m_ffn1_w_gate, m_ffn1_w_up, m_ffn1_w_down, m_mix_norm, m_w_in, m_gate_bias, m_conv_w, m_conv_b, m_conv_ln_g, m_conv_ln_b, m_w_conv_out, m_cq_norm, m_ckv_norm, m_w_uq, m_w_ukv, m_q_norm, m_k_norm, m_w_mla_out, m_w_out, m_ffn2_norm, m_ffn2_w_gate, m_ffn2_w_up, m_ffn2_w_down, v_ffn1_norm, v_ffn1_w_gate, v_ffn1_w_up, v_ffn1_w_down, v_mix_norm, v_w_in, v_gate_bias, v_conv_w, v_conv_b, v_conv_ln_g, v_conv_ln_b, v_w_conv_out, v_cq_norm, v_ckv_norm, v_w_uq, v_w_ukv, v_q_norm, v_k_norm, v_w_mla_out, v_w_out, v_ffn2_norm, v_ffn2_w_gate, v_ffn2_w_up, v_ffn2_w_down):
    w = dict(zip(WEIGHTS, (ffn1_norm, ffn1_w_gate, ffn1_w_up, ffn1_w_down, mix_norm, w_in, gate_bias, conv_w, conv_b, conv_ln_g, conv_ln_b, w_conv_out, cq_norm, ckv_norm, w_uq, w_ukv, q_norm, k_norm, w_mla_out, w_out, ffn2_norm, ffn2_w_gate, ffn2_w_up, ffn2_w_down)))
    m = dict(zip(WEIGHTS, (m_ffn1_norm, m_ffn1_w_gate, m_ffn1_w_up, m_ffn1_w_down, m_mix_norm, m_w_in, m_gate_bias, m_conv_w, m_conv_b, m_conv_ln_g, m_conv_ln_b, m_w_conv_out, m_cq_norm, m_ckv_norm, m_w_uq, m_w_ukv, m_q_norm, m_k_norm, m_w_mla_out, m_w_out, m_ffn2_norm, m_ffn2_w_gate, m_ffn2_w_up, m_ffn2_w_down)))
    v = dict(zip(WEIGHTS, (v_ffn1_norm, v_ffn1_w_gate, v_ffn1_w_up, v_ffn1_w_down, v_mix_norm, v_w_in, v_gate_bias, v_conv_w, v_conv_b, v_conv_ln_g, v_conv_ln_b, v_w_conv_out, v_cq_norm, v_ckv_norm, v_w_uq, v_w_ukv, v_q_norm, v_k_norm, v_w_mla_out, v_w_out, v_ffn2_norm, v_ffn2_w_gate, v_ffn2_w_up, v_ffn2_w_down)))
    depth = ffn1_norm.shape[0]
    T, D = x.shape[1], x.shape[2]
    xs = x.reshape(T, D)
    target = loss_target.reshape(T, D)
    tabs = _rope_tables(positions, T)
    me = 4 * lax.axis_index("x") + 2 * lax.axis_index("y") + lax.axis_index("c")

    big_names = [(l, n) for l in range(depth) for n in BIG]
    shards = []
    for l, n in big_names:
        s = w[n][l]
        shards.append((s.T if BIG[n] else s).astype(_WIRE_DTYPE))
    packed, spans = _pack_rows(shards)
    gathered = _all_gather("gather_weights", packed)
    small_names = [(l, n) for l in range(depth) for n in SMALL_SHARDED]
    small_packed, small_spans = _pack_rows([w[n][l].T for l, n in small_names])
    small_gathered = _all_gather("gather_small", small_packed)

    layers = []
    for l in range(depth):
        W = {}
        for (ll, n), span in zip(big_names, spans):
            if ll == l:
                sh = _unpack_rows(gathered, span, (N_DEV,))
                W[n] = sh.reshape(N_DEV * sh.shape[1], sh.shape[2])
        S = {n: w[n][l] for n in REPLICATED}
        for (ll, n), span in zip(small_names, small_spans):
            if ll == l:
                sh = _unpack_rows(small_gathered, span, (N_DEV,))
                S[n] = sh.reshape(N_DEV * sh.shape[1], sh.shape[2]).T
        layers.append((_layer_weights(W), S))

    acts = []
    hcur = xs
    for L, S in layers:
        hcur, saved = _layer_fwd(hcur, L, S, tabs)
        acts.append(saved)
    dx, loss_part = _loss_grad(hcur, target)
    big_grads, small_grads = [None] * depth, [None] * depth
    for l in reversed(range(depth)):
        L, S = layers[l]
        dx, big_grads[l], small_grads[l] = _layer_bwd(dx, L, S, tabs, acts[l])

    send, gspans = _pack_rows([jnp.swapaxes(big_grads[l][n].reshape(N_DEV // 2, 2, -1, big_grads[l][n].shape[1]), 0, 1)
                               .astype(_WIRE_DTYPE) for l, n in big_names], (2, N_DEV // 2))
    pair = _pair_sum("pair_sum_grads", send, _pair_exchange("pair_exchange_grads", send))
    reduced = _sum_blocks("sum_grads", _chip_all_to_all("scatter_grads", pair))
    small_list = [(l, n) for l in range(depth) for n in REPLICATED + SMALL_SHARDED]
    small_send, sspans = _pack_rows([small_grads[l][n] for l, n in small_list] + [loss_part])
    small_all = _sum_blocks("sum_small", _all_gather("gather_small_grads", small_send))

    loss = _unpack_rows(small_all, sspans[-1])[0, 0]
    grads = {}
    for n in WEIGHTS:
        per_layer = []
        for l in range(depth):
            if n in BIG:
                gt = _unpack_rows(reduced, gspans[big_names.index((l, n))])
                per_layer.append(gt.T if BIG[n] else gt)
            else:
                gs = _unpack_rows(small_all, sspans[small_list.index((l, n))])
                if n in SMALL_SHARDED:
                    cols = w[n].shape[-1]
                    gs = lax.dynamic_slice_in_dim(gs, me * cols, cols, axis=1)
                per_layer.append(gs.reshape(w[n].shape[1:]))
        grads[n] = jnp.stack(per_layer, axis=0)

    delta, new_m, new_v = {}, {}, {}
    for n in WEIGHTS:
        delta[n], new_m[n], new_v[n] = _adamw(w[n], grads[n], m[n], v[n])
    return (loss, dx.reshape(1, T, D), *[grads[n] for n in WEIGHTS], *[delta[n] for n in WEIGHTS],
            *[new_m[n] for n in WEIGHTS], *[new_v[n] for n in WEIGHTS])
```

```python
import jax
import jax.numpy as jnp
from jax import lax
from jax.experimental import pallas as pl
from jax.experimental.pallas import tpu as pltpu

F32 = jnp.float32
_MXU_DTYPE = jnp.bfloat16
_WIRE_DTYPE = jnp.bfloat16

_LANES = 128
_SUBLANES = 8
_PACKED_ROWS = 16
_V7X_VMEM_BYTES = 64 * 2 ** 20
_VMEM_HEADROOM = 8 * 2 ** 20

N_DEV = 8
EPS = 1e-6
CHUNK = 64
CONV_WIDTH = 31
N_HEADS = 8
NOPE_DIM = 64
ROPE_DIM = 32
QK_DIM = NOPE_DIM + ROPE_DIM
V_DIM = 64
HEAD_W = _LANES
ROPE_THETA = 10000.0
HALO = 32

ADAM_LR = 0.001
ADAM_B1 = 0.9
ADAM_B2 = 0.999
ADAM_EPS = 1e-08
ADAM_WD = 0.01
ADAM_STEP = 10

_ROW_TILE = 512
_MM_TILE = 512
_MM_TILE_N = 1408
_MM_TILE_K = 1024
_TN_TILE_M = 1408
_ATT_TILE = 512
_ATT_ROWS = 32
_PACK_W = 1024
_PACK_ROW_MULT = 256

WEIGHTS = ['ffn1_norm', 'ffn1_w_gate', 'ffn1_w_up', 'ffn1_w_down', 'mix_norm', 'w_in', 'gate_bias',
           'conv_w', 'conv_b', 'conv_ln_g', 'conv_ln_b', 'w_conv_out', 'cq_norm', 'ckv_norm', 'w_uq',
           'w_ukv', 'q_norm', 'k_norm', 'w_mla_out', 'w_out', 'ffn2_norm', 'ffn2_w_gate', 'ffn2_w_up',
           'ffn2_w_down']
BIG = {'ffn1_w_gate': True, 'ffn1_w_up': True, 'ffn1_w_down': False, 'w_in': True, 'w_conv_out': True,
       'w_uq': True, 'w_ukv': True, 'w_mla_out': True, 'w_out': False, 'ffn2_w_gate': True,
       'ffn2_w_up': True, 'ffn2_w_down': False}
SMALL_SHARDED = ['conv_w', 'gate_bias']
REPLICATED = ['ffn1_norm', 'mix_norm', 'conv_b', 'conv_ln_g', 'conv_ln_b', 'cq_norm', 'ckv_norm',
              'q_norm', 'k_norm', 'ffn2_norm']


def _tile(n, target, mult=_LANES):
    if n <= target:
        return n
    for d in range(target - target % mult, 0, -mult):
        if n % d == 0:
            return d
    return n


def _nbytes(shape, dtype):
    n = 1
    for s in shape:
        n *= s
    return n * jnp.dtype(dtype).itemsize


def _params(dims, block_bytes, extra_bytes=0):
    need = 2 * block_bytes + extra_bytes + _VMEM_HEADROOM
    assert need <= _V7X_VMEM_BYTES, (need, dims)
    return pltpu.CompilerParams(dimension_semantics=dims, vmem_limit_bytes=_V7X_VMEM_BYTES - _VMEM_HEADROOM)


def _mxu(x, scale=None):
    if scale is not None:
        x = x * scale
    return x if x.dtype == _MXU_DTYPE else x.astype(_MXU_DTYPE)


def _dot(a, b, mode):
    dims = {'nn': (((1,), (0,)), ((), ())), 'nt': (((1,), (1,)), ((), ())), 'tn': (((0,), (0,)), ((), ()))}[mode]
    return lax.dot_general(a, b, dims, preferred_element_type=F32)


def _colsum(x):
    return jnp.sum(x, axis=0, keepdims=True)


def _sig(x):
    return jax.nn.sigmoid(x)


def _mm(name, a, b, mode, *, out_dtype=F32, res=None, scale=None, a_scale=None, tm=None, tn=None, tk=None):
    if mode == 'nn':
        (M, K), (_, N) = a.shape, b.shape
    elif mode == 'nt':
        (M, K), (N, _) = a.shape, b.shape
    else:
        (K, M), (_, N) = a.shape, b.shape
    tm = _tile(M, tm or (_TN_TILE_M if mode == 'tn' else _MM_TILE))
    tn = _tile(N, tn or _MM_TILE_N)
    tk = _tile(K, tk or (_MM_TILE_K if mode == 'tn' else 4096))
    nk = K // tk
    grid = (M // tm, N // tn, nk)
    a_spec = pl.BlockSpec((tk, tm), lambda i, j, k: (k, i)) if mode == 'tn' else pl.BlockSpec((tm, tk), lambda i, j, k: (i, k))
    b_spec = pl.BlockSpec((tn, tk), lambda i, j, k: (j, k)) if mode == 'nt' else pl.BlockSpec((tk, tn), lambda i, j, k: (k, j))
    o_spec = pl.BlockSpec((tm, tn), lambda i, j, k: (i, j))
    has_res = res is not None

    def body(*refs):
        a_ref, b_ref = refs[0], refs[1]
        res_ref = refs[2] if has_res else None
        o_ref = refs[3] if has_res else refs[2]
        acc_ref = refs[-1] if nk > 1 else None

        def finish(p):
            if scale is not None:
                p = p * scale
            if has_res:
                p = res_ref[...] + p
            o_ref[...] = p.astype(o_ref.dtype)

        p = _dot(_mxu(a_ref[...], a_scale), _mxu(b_ref[...]), mode)
        if nk == 1:
            finish(p)
        else:
            k = pl.program_id(2)

            @pl.when(k == 0)
            def _():
                acc_ref[...] = p

            @pl.when(k > 0)
            def _():
                acc_ref[...] += p

            @pl.when(k == nk - 1)
            def _():
                finish(acc_ref[...])

    blocks = (_nbytes((tm, tk), a.dtype) + _nbytes((tk, tn), b.dtype) + _nbytes((tm, tn), out_dtype)
              + (_nbytes((tm, tn), F32) if has_res else 0))
    extra = 3 * _nbytes((tm, tn), F32) + _nbytes((tm, tk), _MXU_DTYPE) + _nbytes((tk, tn), _MXU_DTYPE)
    return pl.pallas_call(
        body, name=name, grid=grid,
        in_specs=[a_spec, b_spec] + ([o_spec] if has_res else []),
        out_specs=o_spec,
        out_shape=jax.ShapeDtypeStruct((M, N), out_dtype),
        scratch_shapes=[pltpu.VMEM((tm, tn), F32)] if nk > 1 else [],
        compiler_params=_params(("parallel", "parallel", "arbitrary"), blocks, extra),
    )(*([a, b] + ([res] if has_res else [])))


def _ffn_up(h, wg, wu):
    T, D = h.shape
    F = wg.shape[0]
    tm, tn = _tile(T, _MM_TILE), _tile(F, _MM_TILE_N)

    def body(h_ref, wg_ref, wu_ref, a_ref, b_ref, u_ref):
        hv = h_ref[...]
        a = _dot(hv, wg_ref[...], 'nt')
        b = _dot(hv, wu_ref[...], 'nt')
        a_ref[...] = a
        b_ref[...] = b
        u_ref[...] = (a * _sig(a) * b).astype(u_ref.dtype)

    w_spec = pl.BlockSpec((tn, D), lambda j, i: (j, 0))
    o_spec = pl.BlockSpec((tm, tn), lambda j, i: (i, j))
    blocks = _nbytes((tm, D), h.dtype) + 2 * _nbytes((tn, D), wg.dtype) + 2 * _nbytes((tm, tn), F32) + _nbytes((tm, tn), _MXU_DTYPE)
    return pl.pallas_call(
        body, name="ffn_up", grid=(F // tn, T // tm),
        in_specs=[pl.BlockSpec((tm, D), lambda j, i: (i, 0)), w_spec, w_spec],
        out_specs=[o_spec, o_spec, o_spec],
        out_shape=[jax.ShapeDtypeStruct((T, F), F32), jax.ShapeDtypeStruct((T, F), F32),
                   jax.ShapeDtypeStruct((T, F), _MXU_DTYPE)],
        compiler_params=_params(("parallel", "parallel"), blocks, 4 * _nbytes((tm, tn), F32)),
    )(h, wg, wu)


def _ffn_bwd_up(dx, wd, a, b):
    T, D = dx.shape
    F = wd.shape[0]
    tm, tn = _tile(T, _MM_TILE), _tile(F, _MM_TILE_N)

    def body(dx_ref, wd_ref, a_ref, b_ref, da_ref, db_ref):
        du = _dot(_mxu(dx_ref[...], 0.5), wd_ref[...], 'nt')
        av, bv = a_ref[...], b_ref[...]
        s = _sig(av)
        db_ref[...] = (du * av * s).astype(db_ref.dtype)
        da_ref[...] = (du * bv * (s * (1.0 + av * (1.0 - s)))).astype(da_ref.dtype)

    t_spec = pl.BlockSpec((tm, tn), lambda j, i: (i, j))
    blocks = _nbytes((tm, D), F32) + _nbytes((tn, D), wd.dtype) + 2 * _nbytes((tm, tn), F32) + 2 * _nbytes((tm, tn), _MXU_DTYPE)
    return pl.pallas_call(
        body, name="ffn_bwd_up", grid=(F // tn, T // tm),
        in_specs=[pl.BlockSpec((tm, D), lambda j, i: (i, 0)), pl.BlockSpec((tn, D), lambda j, i: (j, 0)), t_spec, t_spec],
        out_specs=[t_spec, t_spec],
        out_shape=[jax.ShapeDtypeStruct((T, F), _MXU_DTYPE)] * 2,
        compiler_params=_params(("parallel", "parallel"), blocks, 5 * _nbytes((tm, tn), F32)),
    )(dx, wd, a, b)


def _row_spec(tm, w, cb=0):
    return pl.BlockSpec((tm, w), lambda i: (i, cb))


def _full_spec(shape):
    return pl.BlockSpec(shape, lambda i: (0,) * len(shape))


def _init_acc(refs):
    @pl.when(pl.program_id(0) == 0)
    def _():
        for r in refs:
            r[...] = jnp.zeros_like(r)


def _rms_fwd(x, g):
    T, D = x.shape
    tm = _tile(T, _ROW_TILE, _PACKED_ROWS)

    def body(x_ref, g_ref, h_ref):
        xv = x_ref[...]
        r = lax.rsqrt(jnp.mean(xv * xv, axis=-1, keepdims=True) + EPS)
        h_ref[...] = (xv * r * g_ref[...]).astype(h_ref.dtype)

    return pl.pallas_call(
        body, name="rms_fwd", grid=(T // tm,),
        in_specs=[_row_spec(tm, D), _full_spec((1, D))], out_specs=_row_spec(tm, D),
        out_shape=jax.ShapeDtypeStruct((T, D), _MXU_DTYPE),
        compiler_params=_params(("parallel",), 2 * _nbytes((tm, D), F32), 4 * _nbytes((tm, D), F32)),
    )(x, g)


def _rms_bwd(x, g, dh, dres):
    T, D = x.shape
    tm = _tile(T, _ROW_TILE, _SUBLANES)

    def body(x_ref, g_ref, dh_ref, dres_ref, dx_ref, dg_ref):
        _init_acc([dg_ref])
        xv, dhv = x_ref[...], dh_ref[...]
        r = lax.rsqrt(jnp.mean(xv * xv, axis=-1, keepdims=True) + EPS)
        xh = xv * r
        dxh = dhv * g_ref[...]
        dx_ref[...] = dres_ref[...] + r * (dxh - xh * jnp.mean(dxh * xh, axis=-1, keepdims=True))
        dg_ref[...] += _colsum(dhv * xh)

    return pl.pallas_call(
        body, name="rms_bwd", grid=(T // tm,),
        in_specs=[_row_spec(tm, D), _full_spec((1, D)), _row_spec(tm, D), _row_spec(tm, D)],
        out_specs=[_row_spec(tm, D), _full_spec((1, D))],
        out_shape=[jax.ShapeDtypeStruct((T, D), F32), jax.ShapeDtypeStruct((1, D), F32)],
        compiler_params=_params(("arbitrary",), 4 * _nbytes((tm, D), F32), 6 * _nbytes((tm, D), F32)),
    )(x, g, dh, dres)


def _loss_grad(y, target):
    T, D = y.shape
    tm = _tile(T, _ROW_TILE, _SUBLANES)

    def body(y_ref, t_ref, dy_ref, l_ref):
        _init_acc([l_ref])
        err = y_ref[...] - t_ref[...]
        dy_ref[...] = err * (1.0 / D)
        per_row = jnp.sum(err * err, axis=-1, keepdims=True) * (0.5 / D)
        l_ref[...] += jnp.sum(per_row, axis=0, keepdims=True)

    return pl.pallas_call(
        body, name="loss_grad", grid=(T // tm,),
        in_specs=[_row_spec(tm, D), _row_spec(tm, D)],
        out_specs=[_row_spec(tm, D), _full_spec((1, _LANES))],
        out_shape=[jax.ShapeDtypeStruct((T, D), F32), jax.ShapeDtypeStruct((1, _LANES), F32)],
        compiler_params=_params(("arbitrary",), 3 * _nbytes((tm, D), F32), 3 * _nbytes((tm, D), F32)),
    )(y, target)


def _glu(pc, C):
    a, gate = pc[:, :C], pc[:, C:]
    s = _sig(gate)
    return a, s, a * s


def _conv_fwd(pc, w, cb, lg, lb):
    T, C2 = pc.shape
    C = C2 // 2
    tm = _tile(T, _ROW_TILE // 2, HALO)
    per = tm // HALO

    def body(pc_ref, ph_ref, w_ref, cb_ref, lg_ref, lb_ref, z_ref, c_ref, ubuf):
        i = pl.program_id(0)
        _, _, u_cur = _glu(pc_ref[...], C)
        _, _, u_prev = _glu(ph_ref[...], C)
        ubuf[0:HALO, :] = jnp.where(i > 0, u_prev, 0.0)
        ubuf[HALO:HALO + tm, :] = u_cur
        acc = jnp.zeros((tm, C), F32)
        for k in range(CONV_WIDTH):
            off = HALO - (CONV_WIDTH - 1) + k
            acc = acc + w_ref[k:k + 1, :] * ubuf[off:off + tm, :]
        z = acc + cb_ref[...]
        z_ref[...] = z
        zc = z - jnp.mean(z, axis=-1, keepdims=True)
        y = zc * lax.rsqrt(jnp.mean(zc * zc, axis=-1, keepdims=True) + EPS) * lg_ref[...] + lb_ref[...]
        c_ref[...] = (y * _sig(y)).astype(c_ref.dtype)

    return pl.pallas_call(
        body, name="conv_fwd", grid=(T // tm,),
        in_specs=[_row_spec(tm, C2), pl.BlockSpec((HALO, C2), lambda i: (jnp.maximum(i * per - 1, 0), 0)),
                  _full_spec((HALO, C)), _full_spec((1, C)), _full_spec((1, C)), _full_spec((1, C))],
        out_specs=[_row_spec(tm, C), _row_spec(tm, C)],
        out_shape=[jax.ShapeDtypeStruct((T, C), F32), jax.ShapeDtypeStruct((T, C), _MXU_DTYPE)],
        scratch_shapes=[pltpu.VMEM((tm + HALO, C), F32)],
        compiler_params=_params(("parallel",), 3 * _nbytes((tm, C2), F32), 8 * _nbytes((tm, C), F32)),
    )(pc, pc, w, cb, lg, lb)


def _conv_bwd(pc, z, dc, w, lg, lb):
    T, C2 = pc.shape
    C = C2 // 2
    tm = _tile(T, _ROW_TILE // 2, HALO)
    per = tm // HALO
    n = T // tm
    last_halo = T // HALO - 1

    def body(pc_ref, ph_ref, z_ref, zn_ref, dc_ref, dcn_ref, w_ref, lg_ref, lb_ref,
             dpc_ref, dw_ref, dcb_ref, dlg_ref, dlb_ref, ubuf, dzbuf):
        i = pl.program_id(0)
        _init_acc([dw_ref, dcb_ref, dlg_ref, dlb_ref])
        g, b = lg_ref[...], lb_ref[...]

        def ln_swish_bwd(zv, dcv):
            zc = zv - jnp.mean(zv, axis=-1, keepdims=True)
            r = lax.rsqrt(jnp.mean(zc * zc, axis=-1, keepdims=True) + EPS)
            zh = zc * r
            y = zh * g + b
            s = _sig(y)
            dy = dcv * (s * (1.0 + y * (1.0 - s)))
            dyg = dy * g
            dz = r * (dyg - jnp.mean(dyg, axis=-1, keepdims=True) - zh * jnp.mean(dyg * zh, axis=-1, keepdims=True))
            return dz, dy, zh

        dz_c, dy_c, zh_c = ln_swish_bwd(z_ref[...], dc_ref[...])
        dz_n, _, _ = ln_swish_bwd(zn_ref[...], dcn_ref[...])
        dlg_ref[...] += _colsum(dy_c * zh_c)
        dlb_ref[...] += _colsum(dy_c)
        dcb_ref[...] += _colsum(dz_c)
        dzbuf[0:tm, :] = dz_c
        dzbuf[tm:tm + HALO, :] = jnp.where(i < n - 1, dz_n, 0.0)
        a, s, u_cur = _glu(pc_ref[...], C)
        _, _, u_prev = _glu(ph_ref[...], C)
        ubuf[0:HALO, :] = jnp.where(i > 0, u_prev, 0.0)
        ubuf[HALO:HALO + tm, :] = u_cur
        du = jnp.zeros((tm, C), F32)
        for k in range(CONV_WIDTH):
            back = CONV_WIDTH - 1 - k
            du = du + w_ref[k:k + 1, :] * dzbuf[back:back + tm, :]
            off = HALO - (CONV_WIDTH - 1) + k
            dw_ref[k:k + 1, :] += _colsum(dz_c * ubuf[off:off + tm, :])
        dpc_ref[:, :C] = (du * s).astype(dpc_ref.dtype)
        dpc_ref[:, C:] = (du * a * s * (1.0 - s)).astype(dpc_ref.dtype)

    nxt = lambda i: (jnp.minimum((i + 1) * per, last_halo), 0)
    vec = _full_spec((1, C))
    return pl.pallas_call(
        body, name="conv_bwd", grid=(n,),
        in_specs=[_row_spec(tm, C2), pl.BlockSpec((HALO, C2), lambda i: (jnp.maximum(i * per - 1, 0), 0)),
                  _row_spec(tm, C), pl.BlockSpec((HALO, C), nxt), _row_spec(tm, C), pl.BlockSpec((HALO, C), nxt),
                  _full_spec((HALO, C)), vec, vec],
        out_specs=[_row_spec(tm, C2), _full_spec((HALO, C)), vec, vec, vec],
        out_shape=[jax.ShapeDtypeStruct((T, C2), _MXU_DTYPE), jax.ShapeDtypeStruct((HALO, C), F32)]
                  + [jax.ShapeDtypeStruct((1, C), F32)] * 3,
        scratch_shapes=[pltpu.VMEM((tm + HALO, C), F32), pltpu.VMEM((tm + HALO, C), F32)],
        compiler_params=_params(("arbitrary",), 5 * _nbytes((tm, C2), F32), 12 * _nbytes((tm, C), F32)),
    )(pc, pc, z, z, dc, dc, w, lg, lb)


def _lat_norm_fwd(pl_, gq, gkv, QL, KVL):
    T, W = pl_.shape
    tm = _tile(T, _ROW_TILE, _PACKED_ROWS)

    def body(p_ref, gq_ref, gkv_ref, cq_ref, ckv_ref):
        def norm(xv, gv):
            return xv * lax.rsqrt(jnp.mean(xv * xv, axis=-1, keepdims=True) + EPS) * gv
        cq_ref[...] = norm(p_ref[:, :QL], gq_ref[...]).astype(cq_ref.dtype)
        ckv_ref[...] = norm(p_ref[:, QL:QL + KVL], gkv_ref[...]).astype(ckv_ref.dtype)

    return pl.pallas_call(
        body, name="lat_norm_fwd", grid=(T // tm,),
        in_specs=[_row_spec(tm, W), _full_spec((1, QL)), _full_spec((1, KVL))],
        out_specs=[_row_spec(tm, QL), _row_spec(tm, KVL)],
        out_shape=[jax.ShapeDtypeStruct((T, QL), _MXU_DTYPE), jax.ShapeDtypeStruct((T, KVL), _MXU_DTYPE)],
        compiler_params=_params(("parallel",), 2 * _nbytes((tm, W), F32), 4 * _nbytes((tm, W), F32)),
    )(pl_, gq, gkv)


def _lat_norm_bwd(pl_, gq, gkv, dcq, dckv, drope, QL, KVL):
    T, W = pl_.shape
    tm = _tile(T, _ROW_TILE, _PACKED_ROWS)

    def body(p_ref, gq_ref, gkv_ref, dcq_ref, dckv_ref, dr_ref, dp_ref, dgq_ref, dgkv_ref):
        _init_acc([dgq_ref, dgkv_ref])

        def bwd(xv, gv, dv, dg_ref):
            r = lax.rsqrt(jnp.mean(xv * xv, axis=-1, keepdims=True) + EPS)
            xh = xv * r
            dxh = dv * gv
            dg_ref[...] += _colsum(dv * xh)
            return r * (dxh - xh * jnp.mean(dxh * xh, axis=-1, keepdims=True))

        dp_ref[:, :QL] = bwd(p_ref[:, :QL], gq_ref[...], dcq_ref[...], dgq_ref).astype(dp_ref.dtype)
        dp_ref[:, QL:QL + KVL] = bwd(p_ref[:, QL:QL + KVL], gkv_ref[...], dckv_ref[...], dgkv_ref).astype(dp_ref.dtype)
        dp_ref[:, QL + KVL:] = dr_ref[...].astype(dp_ref.dtype)

    return pl.pallas_call(
        body, name="lat_norm_bwd", grid=(T // tm,),
        in_specs=[_row_spec(tm, W), _full_spec((1, QL)), _full_spec((1, KVL)), _row_spec(tm, QL), _row_spec(tm, KVL),
                  _row_spec(tm, HEAD_W)],
        out_specs=[_row_spec(tm, W), _full_spec((1, QL)), _full_spec((1, KVL))],
        out_shape=[jax.ShapeDtypeStruct((T, W), _MXU_DTYPE), jax.ShapeDtypeStruct((1, QL), F32),
                   jax.ShapeDtypeStruct((1, KVL), F32)],
        compiler_params=_params(("arbitrary",), 4 * _nbytes((tm, W), F32), 6 * _nbytes((tm, W), F32)),
    )(pl_, gq, gkv, dcq, dckv, drope)


def _norm_rope(xv, gv, ct, sa, sb):
    r = lax.rsqrt(jnp.sum(xv * xv, axis=-1, keepdims=True) * (1.0 / QK_DIM) + EPS)
    xn = xv * r * gv
    half = ROPE_DIM // 2
    return xn * ct + pltpu.roll(xn, HEAD_W - half, 1) * sa + pltpu.roll(xn, half, 1) * sb


def _norm_rope_bwd(xv, gv, ct, sa, sb, dout):
    half = ROPE_DIM // 2
    dxn = dout * ct + pltpu.roll(dout * sa, half, 1) + pltpu.roll(dout * sb, HEAD_W - half, 1)
    r = lax.rsqrt(jnp.sum(xv * xv, axis=-1, keepdims=True) * (1.0 / QK_DIM) + EPS)
    xh = xv * r
    dxh = dxn * gv
    dx = r * (dxh - xh * (jnp.sum(dxh * xh, axis=-1, keepdims=True) * (1.0 / QK_DIM)))
    return dx, _colsum(dxn * xh)


def _qk_prep_fwd(q_raw, k_raw, pl_, rope_cb, ct, sa, sb, gq, gk):
    T, HW = q_raw.shape
    H = HW // HEAD_W
    tm = _tile(T, _ROW_TILE, _PACKED_ROWS)

    def body(q_ref, k_ref, r_ref, ct_ref, sa_ref, sb_ref, gq_ref, gk_ref, qn_ref, kn_ref):
        ctv, sav, sbv, rv = ct_ref[...], sa_ref[...], sb_ref[...], r_ref[...]
        for h in range(H):
            hs = slice(h * HEAD_W, (h + 1) * HEAD_W)
            qn_ref[:, hs] = _norm_rope(q_ref[:, hs], gq_ref[...], ctv, sav, sbv).astype(qn_ref.dtype)
            kn_ref[:, hs] = _norm_rope(k_ref[:, hs] + rv, gk_ref[...], ctv, sav, sbv).astype(kn_ref.dtype)

    tb = _row_spec(tm, HEAD_W)
    gb = _full_spec((1, HEAD_W))
    return pl.pallas_call(
        body, name="qk_prep_fwd", grid=(T // tm,),
        in_specs=[_row_spec(tm, HW), _row_spec(tm, HW), _row_spec(tm, HEAD_W, rope_cb), tb, tb, tb, gb, gb],
        out_specs=[_row_spec(tm, HW), _row_spec(tm, HW)],
        out_shape=[jax.ShapeDtypeStruct((T, HW), _MXU_DTYPE)] * 2,
        compiler_params=_params(("parallel",), 3 * _nbytes((tm, HW), F32) + 4 * _nbytes((tm, HEAD_W), F32), 8 * _nbytes((tm, HEAD_W), F32)),
    )(q_raw, k_raw, pl_, ct, sa, sb, gq, gk)


def _qk_prep_bwd(q_raw, k_raw, pl_, rope_cb, ct, sa, sb, gq, gk, dqn, dkn):
    T, HW = q_raw.shape
    H = HW // HEAD_W
    tm = _tile(T, _ROW_TILE, _PACKED_ROWS)

    def body(q_ref, k_ref, r_ref, ct_ref, sa_ref, sb_ref, gq_ref, gk_ref, dqn_ref, dkn_ref,
             dq_ref, dk_ref, dr_ref, dgq_ref, dgk_ref):
        _init_acc([dgq_ref, dgk_ref])
        ctv, sav, sbv, rv = ct_ref[...], sa_ref[...], sb_ref[...], r_ref[...]
        dr = dgq_sum = dgk_sum = None
        for h in range(H):
            hs = slice(h * HEAD_W, (h + 1) * HEAD_W)
            dq, dgq = _norm_rope_bwd(q_ref[:, hs], gq_ref[...], ctv, sav, sbv, dqn_ref[:, hs])
            dk, dgk = _norm_rope_bwd(k_ref[:, hs] + rv, gk_ref[...], ctv, sav, sbv, dkn_ref[:, hs])
            dq_ref[:, hs] = dq.astype(dq_ref.dtype)
            dk_ref[:, hs] = dk.astype(dk_ref.dtype)
            dr = dk if h == 0 else dr + dk
            dgq_sum = dgq if h == 0 else dgq_sum + dgq
            dgk_sum = dgk if h == 0 else dgk_sum + dgk
        dr_ref[...] = dr
        dgq_ref[...] += dgq_sum
        dgk_ref[...] += dgk_sum

    tb = _row_spec(tm, HEAD_W)
    gb = _full_spec((1, HEAD_W))
    hb = _row_spec(tm, HW)
    return pl.pallas_call(
        body, name="qk_prep_bwd", grid=(T // tm,),
        in_specs=[hb, hb, _row_spec(tm, HEAD_W, rope_cb), tb, tb, tb, gb, gb, hb, hb],
        out_specs=[hb, hb, tb, gb, gb],
        out_shape=[jax.ShapeDtypeStruct((T, HW), _MXU_DTYPE)] * 2 + [jax.ShapeDtypeStruct((T, HEAD_W), F32)]
                  + [jax.ShapeDtypeStruct((1, HEAD_W), F32)] * 2,
        compiler_params=_params(("arbitrary",), 5 * _nbytes((tm, HW), F32) + 5 * _nbytes((tm, HEAD_W), F32), 12 * _nbytes((tm, HEAD_W), F32)),
    )(q_raw, k_raw, pl_, ct, sa, sb, gq, gk, dqn, dkn)


_NEG = -1e30
_LOG2E = 1.4426950408889634
_SCORE_C = QK_DIM ** -0.5 * _LOG2E


def _pieces(t, rb, diag):
    assert CHUNK % rb == 0 and rb % _SUBLANES == 0
    out = []
    for r in range(t // rb):
        lo = (r * rb // CHUNK) * CHUNK if diag else 0
        for c in range(t // _LANES):
            cut = None if (c + 1) * _LANES <= lo else max(lo - c * _LANES, 0)
            out.append((slice(r * rb, (r + 1) * rb), slice(c * _LANES, (c + 1) * _LANES), c, cut))
    return out


def _groups(x):
    return x.reshape(x.shape[0] // _SUBLANES, _SUBLANES, x.shape[1])


def _all_sublanes(x8, op):
    return jnp.broadcast_to(op(x8, axis=0, keepdims=True), x8.shape)


def _lane_ge(rb, cut):
    return lax.broadcasted_iota(jnp.int32, (rb, _LANES), 1) >= cut


def _attn_fwd(q, k, v):
    T, HW = q.shape
    H = HW // HEAD_W
    t = _tile(T, _ATT_TILE)
    rb = min(_ATT_ROWS, t)
    nc = t // _LANES

    def body(q_ref, k_ref, v_ref, o_ref, lse_ref, s_s, p_s, m_s, l_s, acc_s):
        i = pl.program_id(1)
        qv = q_ref[...]
        m_s[...] = jnp.full_like(m_s, _NEG)
        l_s[...] = jnp.zeros_like(l_s)
        acc_s[...] = jnp.zeros_like(acc_s)

        def step(j, diag):
            rows = pl.ds(pl.multiple_of(j * t, t), t)
            s_s[...] = _dot(k_ref[rows, :], qv, 'nt')
            pieces = _pieces(t, rb, diag)

            def scores(rs, cs, cut):
                sb = s_s[rs, cs]
                return jnp.where(_lane_ge(rb, cut), sb, _NEG) if cut else sb

            mx = [None] * nc
            for rs, cs, c, cut in pieces:
                if cut is not None:
                    g = jnp.max(_groups(scores(rs, cs, cut)), axis=0)
                    mx[c] = g if mx[c] is None else jnp.maximum(mx[c], g)
            m_new, alpha = [], []
            for c in range(nc):
                cs = slice(c * _LANES, (c + 1) * _LANES)
                m_prev = m_s[:, cs]
                m_new.append(jnp.maximum(m_prev, _all_sublanes(mx[c], jnp.max)))
                alpha.append(jnp.exp2((m_prev - m_new[c]) * _SCORE_C))
                m_s[:, cs] = m_new[c]
            lsum = [jnp.zeros((_SUBLANES, _LANES), F32)] * nc
            for rs, cs, c, cut in pieces:
                if cut is None:
                    p_s[rs, cs] = jnp.zeros((rb, _LANES), p_s.dtype)
                    continue
                p = jnp.exp2((scores(rs, cs, cut) - jnp.tile(m_new[c], (rb // _SUBLANES, 1))) * _SCORE_C)
                lsum[c] = lsum[c] + jnp.sum(_groups(p), axis=0)
                p_s[rs, cs] = p.astype(p_s.dtype)
            for c in range(nc):
                cs = slice(c * _LANES, (c + 1) * _LANES)
                l_s[:, cs] = alpha[c] * l_s[:, cs] + _all_sublanes(lsum[c], jnp.sum)
            a = jnp.tile(jnp.concatenate(alpha, axis=1), (HEAD_W // _SUBLANES, 1))
            acc_s[...] = a * acc_s[...] + _dot(v_ref[rows, :], p_s[...], 'tn')

        def loop_body(j, carry):
            step(j, False)
            return carry

        lax.fori_loop(0, i, loop_body, 0)
        step(i, True)
        l = l_s[...]
        o_t = acc_s[...] / jnp.tile(l, (HEAD_W // _SUBLANES, 1))
        o_ref[...] = o_t.T.astype(o_ref.dtype)
        lse_ref[...] = m_s[...] * _SCORE_C + jnp.log(l) * _LOG2E

    qb = pl.BlockSpec((t, HEAD_W), lambda h, i: (i, h))
    kb = pl.BlockSpec((T, HEAD_W), lambda h, i: (0, h))
    st = pltpu.VMEM((_SUBLANES, t), F32)
    return pl.pallas_call(
        body, name="attn_fwd", grid=(H, T // t),
        in_specs=[qb, kb, kb], out_specs=[qb, pl.BlockSpec((_SUBLANES, t), lambda h, i: (h, i))],
        out_shape=[jax.ShapeDtypeStruct((T, HW), _MXU_DTYPE), jax.ShapeDtypeStruct((H * _SUBLANES, T), F32)],
        scratch_shapes=[pltpu.VMEM((t, t), F32), pltpu.VMEM((t, t), _MXU_DTYPE), st, st, pltpu.VMEM((HEAD_W, t), F32)],
        compiler_params=_params(("parallel", "arbitrary"), 2 * _nbytes((T, HEAD_W), _MXU_DTYPE) + 3 * _nbytes((t, HEAD_W), F32),
                                4 * _nbytes((t, t), F32)),
    )(q, k, v)


def _attn_bwd(q, k, v, o, do, lse2):
    T, HW = q.shape
    H = HW // HEAD_W
    t = _tile(T, _ATT_TILE)
    n = T // t
    rb = min(_ATT_ROWS, t)
    scale = QK_DIM ** -0.5

    def body(k_ref, v_ref, q_ref, o_ref, do_ref, lse_ref, dq_ref, dk_ref, dv_ref,
             s_s, dp_s, p_s, ds_s, dk_s, dv_s, delta_s, dqt_s):
        j = pl.program_id(1)

        @pl.when(j == 0)
        def _():
            dqt_s[...] = jnp.zeros_like(dqt_s)
            for b in range(n):
                rs = slice(b * t, (b + 1) * t)
                d = jnp.sum(do_ref[rs, :].astype(F32) * o_ref[rs, :].astype(F32), axis=1, keepdims=True)
                delta_s[:, rs] = jnp.broadcast_to(d, (t, HEAD_W)).T[0:_SUBLANES, :]

        kv, vv = k_ref[...], v_ref[...]
        dk_s[...] = jnp.zeros_like(dk_s)
        dv_s[...] = jnp.zeros_like(dv_s)

        def step(i, diag):
            base = pl.multiple_of(i * t, t)
            qv, dov = q_ref[pl.ds(base, t), :], do_ref[pl.ds(base, t), :]
            s_s[...] = _dot(kv, qv, 'nt')
            dp_s[...] = _dot(vv, dov, 'nt')
            for rs, cs, c, cut in _pieces(t, rb, diag):
                if cut is None:
                    p_s[rs, cs] = jnp.zeros((rb, _LANES), p_s.dtype)
                    ds_s[rs, cs] = jnp.zeros((rb, _LANES), ds_s.dtype)
                    continue
                lanes = pl.ds(pl.multiple_of(base + c * _LANES, _LANES), _LANES)
                lse = jnp.tile(lse_ref[:, lanes], (rb // _SUBLANES, 1))
                dl = jnp.tile(delta_s[:, lanes], (rb // _SUBLANES, 1))
                p = jnp.exp2(s_s[rs, cs] * _SCORE_C - lse)
                if cut:
                    p = jnp.where(_lane_ge(rb, cut), p, 0.0)
                p_s[rs, cs] = p.astype(p_s.dtype)
                ds_s[rs, cs] = (p * (dp_s[rs, cs] - dl)).astype(ds_s.dtype)
            dsv = ds_s[...]
            dv_s[...] += _dot(p_s[...], dov, 'nn')
            dk_s[...] += _dot(dsv, qv, 'nn')
            dqt_s[:, pl.ds(base, t)] += _dot(kv, dsv, 'tn')

        def loop_body(i, carry):
            step(i, False)
            return carry

        step(j, True)
        lax.fori_loop(j + 1, n, loop_body, 0)
        dk_ref[...] = dk_s[...] * scale
        dv_ref[...] = dv_s[...].astype(dv_ref.dtype)

        @pl.when(j == n - 1)
        def _():
            for b in range(n):
                rs = slice(b * t, (b + 1) * t)
                dq_ref[rs, :] = dqt_s[:, rs].T * scale

    jb = pl.BlockSpec((t, HEAD_W), lambda h, j: (j, h))
    fb = pl.BlockSpec((T, HEAD_W), lambda h, j: (0, h))
    sc = pltpu.VMEM((t, HEAD_W), F32)
    return pl.pallas_call(
        body, name="attn_bwd", grid=(H, n),
        in_specs=[jb, jb, fb, fb, fb, pl.BlockSpec((_SUBLANES, T), lambda h, j: (h, 0))], out_specs=[fb, jb, jb],
        out_shape=[jax.ShapeDtypeStruct((T, HW), F32), jax.ShapeDtypeStruct((T, HW), F32),
                   jax.ShapeDtypeStruct((T, HW), _MXU_DTYPE)],
        scratch_shapes=[pltpu.VMEM((t, t), F32), pltpu.VMEM((t, t), F32), pltpu.VMEM((t, t), _MXU_DTYPE),
                        pltpu.VMEM((t, t), _MXU_DTYPE), sc, sc, pltpu.VMEM((_SUBLANES, T), F32),
                        pltpu.VMEM((HEAD_W, T), F32)],
        compiler_params=_params(("parallel", "arbitrary"), 3 * _nbytes((T, HEAD_W), _MXU_DTYPE) + _nbytes((T, HEAD_W), F32)
                                + 4 * _nbytes((t, HEAD_W), F32), 4 * _nbytes((t, t), F32) + _nbytes((T, HEAD_W), F32)),
    )(k, v, q, o, do, lse2)


def _gate_fwd(pg, gb, yc, ym):
    T, D = yc.shape
    tm = _tile(T, _ROW_TILE, _PACKED_ROWS)

    def body(g0_ref, g1_ref, gb_ref, yc_ref, ym_ref, y_ref):
        s0 = _sig(g0_ref[...] + gb_ref[0:1, :])
        s1 = _sig(g1_ref[...] + gb_ref[1:2, :])
        y_ref[...] = (s0 * yc_ref[...] + s1 * ym_ref[...]).astype(y_ref.dtype)

    return pl.pallas_call(
        body, name="gate_fwd", grid=(T // tm,),
        in_specs=[_row_spec(tm, D, 0), _row_spec(tm, D, 1), _full_spec((2, D)), _row_spec(tm, D), _row_spec(tm, D)],
        out_specs=_row_spec(tm, D),
        out_shape=jax.ShapeDtypeStruct((T, D), _MXU_DTYPE),
        compiler_params=_params(("parallel",), 5 * _nbytes((tm, D), F32), 4 * _nbytes((tm, D), F32)),
    )(pg, pg, gb, yc, ym)


def _gate_bwd(pg, gb, yc, ym, dy):
    T, D = yc.shape
    tm = _tile(T, _ROW_TILE, _PACKED_ROWS)

    def body(g0_ref, g1_ref, gb_ref, yc_ref, ym_ref, dy_ref, dyc_ref, dym_ref, dpg_ref, dgb_ref):
        _init_acc([dgb_ref])
        dyv = dy_ref[...]
        s0 = _sig(g0_ref[...] + gb_ref[0:1, :])
        s1 = _sig(g1_ref[...] + gb_ref[1:2, :])
        dyc_ref[...] = (dyv * s0).astype(dyc_ref.dtype)
        dym_ref[...] = (dyv * s1).astype(dym_ref.dtype)
        d0 = dyv * yc_ref[...] * s0 * (1.0 - s0)
        d1 = dyv * ym_ref[...] * s1 * (1.0 - s1)
        dpg_ref[:, :D] = d0.astype(dpg_ref.dtype)
        dpg_ref[:, D:] = d1.astype(dpg_ref.dtype)
        dgb_ref[0:1, :] += _colsum(d0)
        dgb_ref[1:2, :] += _colsum(d1)

    return pl.pallas_call(
        body, name="gate_bwd", grid=(T // tm,),
        in_specs=[_row_spec(tm, D, 0), _row_spec(tm, D, 1), _full_spec((2, D)), _row_spec(tm, D), _row_spec(tm, D),
                  _row_spec(tm, D)],
        out_specs=[_row_spec(tm, D), _row_spec(tm, D), _row_spec(tm, 2 * D), _full_spec((2, D))],
        out_shape=[jax.ShapeDtypeStruct((T, D), _MXU_DTYPE)] * 2 + [jax.ShapeDtypeStruct((T, 2 * D), _MXU_DTYPE),
                                                                   jax.ShapeDtypeStruct((2, D), F32)],
        compiler_params=_params(("arbitrary",), 7 * _nbytes((tm, D), F32), 8 * _nbytes((tm, D), F32)),
    )(pg, pg, gb, yc, ym, dy)


def _place():
    return lax.axis_index("x"), lax.axis_index("y"), lax.axis_index("c")


def _all_gather(name, xs):
    R, C = xs.shape
    MESH = pl.DeviceIdType.MESH

    def body(x_ref, out_ref, send_sems, recv_sems, local_sem):
        x, y, c = _place()
        me, sibling = (x, y, c), (x, y, 1 - c)
        chips = [(1 - x, y), (x, 1 - y), (1 - x, 1 - y)]

        def rows(px, py, pc):
            return out_ref.at[4 * px + 2 * py + pc]

        def copy(k, block, to, src=None):
            return pltpu.make_async_remote_copy(
                src_ref=rows(*block) if src is None else src, dst_ref=rows(*block),
                send_sem=send_sems.at[k], recv_sem=recv_sems.at[k], device_id=to, device_id_type=MESH)

        mine = pltpu.make_async_copy(x_ref, rows(*me), local_sem)
        mine.start()
        first = [copy(0, me, sibling, src=x_ref)]
        first += [copy(1 + j, me, (*chip, c), src=x_ref) for j, chip in enumerate(chips)]
        for cp in first:
            cp.start()
        passed = [copy(4 + j, (*chip, c), sibling) for j, chip in enumerate(chips)]
        for j, chip in enumerate(chips):
            copy(1 + j, (*chip, c), me).wait_recv()
            passed[j].start()
        copy(0, sibling, me).wait_recv()
        for j, chip in enumerate(chips):
            copy(4 + j, (*chip, 1 - c), me).wait_recv()
        for cp in first + passed:
            cp.wait_send()
        mine.wait()

    return pl.pallas_call(
        body, name=name,
        out_shape=jax.ShapeDtypeStruct((N_DEV, R, C), xs.dtype),
        in_specs=[pl.BlockSpec(memory_space=pl.ANY)], out_specs=pl.BlockSpec(memory_space=pl.ANY),
        scratch_shapes=[pltpu.SemaphoreType.DMA((7,)), pltpu.SemaphoreType.DMA((7,)), pltpu.SemaphoreType.DMA],
    )(xs)


def _pair_exchange(name, xs):
    _, G, R, C = xs.shape

    def body(x_ref, out_ref, send_sem, recv_sem):
        x, y, c = _place()
        cp = pltpu.make_async_remote_copy(
            src_ref=x_ref.at[1 - c], dst_ref=out_ref, send_sem=send_sem, recv_sem=recv_sem,
            device_id=(x, y, 1 - c), device_id_type=pl.DeviceIdType.MESH)
        cp.start()
        cp.wait()

    return pl.pallas_call(
        body, name=name,
        out_shape=jax.ShapeDtypeStruct((G, R, C), xs.dtype),
        in_specs=[pl.BlockSpec(memory_space=pl.ANY)], out_specs=pl.BlockSpec(memory_space=pl.ANY),
        scratch_shapes=[pltpu.SemaphoreType.DMA, pltpu.SemaphoreType.DMA],
    )(xs)


def _pair_sum(name, xs, got):
    _, G, R, C = xs.shape
    tr = _tile(R, _PACK_ROW_MULT, _PACKED_ROWS)

    def body(c_ref, a_ref, b_ref, o_ref):
        o_ref[...] = (a_ref[...].astype(F32) + b_ref[...].astype(F32)).astype(o_ref.dtype)

    blk = pl.BlockSpec((None, tr, C), lambda g, i, c_ref: (g, i, 0))
    return pl.pallas_call(
        body, name=name,
        grid_spec=pltpu.PrefetchScalarGridSpec(
            num_scalar_prefetch=1, grid=(G, R // tr),
            in_specs=[pl.BlockSpec((None, None, tr, C), lambda g, i, c_ref: (c_ref[0], g, i, 0)), blk],
            out_specs=blk),
        out_shape=jax.ShapeDtypeStruct((G, R, C), xs.dtype),
        compiler_params=_params(("parallel", "parallel"), 3 * _nbytes((tr, C), F32), 2 * _nbytes((tr, C), F32)),
    )(lax.axis_index("c").reshape(1).astype(jnp.int32), xs, got)


def _chip_all_to_all(name, xs):
    G, R, C = xs.shape

    def body(x_ref, out_ref, send_sems, recv_sems, local_sem):
        x, y, c = _place()
        me = 2 * x + y
        mine = pltpu.make_async_copy(x_ref.at[me], out_ref.at[me], local_sem)
        mine.start()
        copies = []
        for k in range(1, G):
            px = 1 - x if k & 2 else x
            py = 1 - y if k & 1 else y
            copies.append(pltpu.make_async_remote_copy(
                src_ref=x_ref.at[2 * px + py], dst_ref=out_ref.at[me],
                send_sem=send_sems.at[k - 1], recv_sem=recv_sems.at[k - 1],
                device_id=(px, py, c), device_id_type=pl.DeviceIdType.MESH))
        for cp in copies:
            cp.start()
        for cp in copies:
            cp.wait()
        mine.wait()

    return pl.pallas_call(
        body, name=name,
        out_shape=jax.ShapeDtypeStruct((G, R, C), xs.dtype),
        in_specs=[pl.BlockSpec(memory_space=pl.ANY)], out_specs=pl.BlockSpec(memory_space=pl.ANY),
        scratch_shapes=[pltpu.SemaphoreType.DMA((G - 1,)), pltpu.SemaphoreType.DMA((G - 1,)), pltpu.SemaphoreType.DMA],
    )(xs)


def _sum_blocks(name, parts, scale=None):
    n, R, C = parts.shape
    tr = _tile(R, _PACK_ROW_MULT, _PACKED_ROWS)

    def body(p_ref, o_ref):
        acc = p_ref[0].astype(F32)
        for s in range(1, n):
            acc = acc + p_ref[s].astype(F32)
        if scale is not None:
            acc = acc * scale
        o_ref[...] = acc

    return pl.pallas_call(
        body, name=name, grid=(R // tr,),
        in_specs=[pl.BlockSpec((n, tr, C), lambda i: (0, i, 0))], out_specs=pl.BlockSpec((tr, C), lambda i: (i, 0)),
        out_shape=jax.ShapeDtypeStruct((R, C), F32),
        compiler_params=_params(("parallel",), _nbytes((n, tr, C), parts.dtype) + _nbytes((tr, C), F32), 2 * _nbytes((tr, C), F32)),
    )(parts)


def _adamw(w, g, m, v):
    shape = w.shape
    w2, g2, m2, v2 = (t.reshape(-1, shape[-1]) for t in (w, g, m, v))
    R, C = w2.shape
    tr = _tile(R, _ROW_TILE, _SUBLANES)

    def body(w_ref, g_ref, m_ref, v_ref, d_ref, nm_ref, nv_ref):
        gv = g_ref[...]
        mv = ADAM_B1 * m_ref[...] + (1.0 - ADAM_B1) * gv
        vv = ADAM_B2 * v_ref[...] + (1.0 - ADAM_B2) * (gv * gv)
        m_hat = mv / (1.0 - ADAM_B1 ** ADAM_STEP)
        v_hat = vv / (1.0 - ADAM_B2 ** ADAM_STEP)
        d_ref[...] = -ADAM_LR * (m_hat / (jnp.sqrt(v_hat) + ADAM_EPS) + ADAM_WD * w_ref[...])
        nm_ref[...] = mv
        nv_ref[...] = vv

    spec = pl.BlockSpec((tr, C), lambda i: (i, 0))
    outs = pl.pallas_call(
        body, name="adamw", grid=(R // tr,),
        in_specs=[spec] * 4, out_specs=[spec] * 3,
        out_shape=[jax.ShapeDtypeStruct((R, C), F32)] * 3,
        compiler_params=_params(("parallel",), 7 * _nbytes((tr, C), F32), 4 * _nbytes((tr, C), F32)),
    )(w2, g2, m2, v2)
    return tuple(o.reshape(shape) for o in outs)


def _pack_rows(parts, lead=()):
    out, spans, r0 = [], [], 0
    unit = _PACKED_ROWS * _PACK_W
    for p in parts:
        flat = p.reshape(lead + (-1,))
        size = flat.shape[-1]
        pad = (-size) % unit
        if pad:
            flat = jnp.pad(flat, [(0, 0)] * len(lead) + [(0, pad)])
        rows = (size + pad) // _PACK_W
        out.append(flat.reshape(lead + (rows, _PACK_W)))
        spans.append((r0, rows, p.shape[len(lead):]))
        r0 += rows
    pad = (-r0) % _PACK_ROW_MULT
    if pad:
        out.append(jnp.zeros(lead + (pad, _PACK_W), out[0].dtype))
    return jnp.concatenate(out, axis=len(lead)), spans


def _unpack_rows(packed, span, lead=()):
    r0, rows, shape = span
    size = 1
    for s in shape:
        size *= s
    sl = lax.slice_in_dim(packed, r0, r0 + rows, axis=len(lead))
    return sl.reshape(lead + (-1,))[..., :size].reshape(lead + tuple(shape))


def _pack_flat(parts):
    flat, spans, at = [], [], 0
    for p in parts:
        size = p.size
        padded = size + (-size) % _LANES
        flat.append(jnp.pad(p.reshape(-1), (0, padded - size)))
        spans.append((at, size, p.shape))
        at += padded
    tail = (-at) % (_PACKED_ROWS * _PACK_W)
    if tail:
        flat.append(jnp.zeros((tail,), flat[0].dtype))
    return jnp.concatenate(flat).reshape(-1, _PACK_W), spans


def _unpack_flat(packed, span, lead=()):
    at, size, shape = span
    return lax.slice_in_dim(packed.reshape(lead + (-1,)), at, at + size, axis=len(lead)).reshape(lead + tuple(shape))


def _rope_tables(positions, T):
    inv_freq = ROPE_THETA ** (-jnp.arange(0, ROPE_DIM, 2, dtype=F32) / ROPE_DIM)
    ang = positions.reshape(T, 1).astype(F32) * inv_freq
    cos, sin = jnp.cos(ang), jnp.sin(ang)
    half = ROPE_DIM // 2
    z = lambda n: jnp.zeros((T, n), F32)
    tail = HEAD_W - QK_DIM
    ct = jnp.concatenate([jnp.ones((T, NOPE_DIM), F32), cos, cos, z(tail)], axis=1)
    sa = jnp.concatenate([z(NOPE_DIM), -sin, z(half), z(tail)], axis=1)
    sb = jnp.concatenate([z(NOPE_DIM), z(half), sin, z(tail)], axis=1)
    return ct, sa, sb


def _pad_heads(wt, per_head, keep_from=0, keep=None):
    K = wt.shape[1]
    keep = per_head if keep is None else keep
    w3 = wt.reshape(N_HEADS, per_head, K)[:, keep_from:keep_from + keep]
    return jnp.pad(w3, ((0, 0), (0, HEAD_W - keep), (0, 0))).reshape(N_HEADS * HEAD_W, K)


def _unpad_heads(g, keep):
    return g.reshape(N_HEADS, HEAD_W, g.shape[1])[:, :keep]


def _layer_weights(W):
    D = W['w_out'].shape[1]
    C = W['w_conv_out'].shape[1]
    QL, KVL = W['w_uq'].shape[1], W['w_ukv'].shape[1]
    o1 = 2 * C
    o2, o3 = o1 + QL, o1 + QL + KVL
    o4 = o3 + ROPE_DIM
    win = W['w_in']
    zr = lambda n: jnp.zeros((n, D), win.dtype)
    L = dict(W)
    L['w_c'] = win[:o1]
    L['w_l'] = jnp.concatenate([win[o1:o3], zr(NOPE_DIM), win[o3:o4], zr(HEAD_W - QK_DIM)], axis=0)
    L['w_g'] = win[o4:]
    L['w_q'] = _pad_heads(W['w_uq'], QK_DIM)
    L['w_k'] = _pad_heads(W['w_ukv'], NOPE_DIM + V_DIM, 0, NOPE_DIM)
    L['w_v'] = _pad_heads(W['w_ukv'], NOPE_DIM + V_DIM, NOPE_DIM, V_DIM)
    wmo = W['w_mla_out'].reshape(D, N_HEADS, V_DIM)
    L['w_mo'] = jnp.pad(wmo, ((0, 0), (0, 0), (0, HEAD_W - V_DIM))).reshape(D, N_HEADS * HEAD_W)
    L['dims'] = (D, C, QL, KVL)
    return L


def _row(v, width=None):
    v = v.reshape(1, -1)
    if width is not None and v.shape[1] < width:
        v = jnp.pad(v, ((0, 0), (0, width - v.shape[1])))
    return v


def _ffn_fwd(x, g, wg, wu, wd):
    h = _rms_fwd(x, _row(g))
    a, b, u = _ffn_up(h, wg, wu)
    x_out = _mm("ffn_down", u, wd, 'nn', res=x, scale=0.5)
    return x_out, (x, h, a, b, u)


def _ffn_bwd(dx, saved, g, wg, wu, wd):
    x, h, a, b, u = saved
    d_wd = _mm("ffn_dwd", u, dx, 'tn', scale=0.5)
    da, db = _ffn_bwd_up(dx, wd, a, b)
    dh = _mm("ffn_dh_g", da, wg, 'nn')
    dh = _mm("ffn_dh_u", db, wu, 'nn', res=dh)
    d_wg = _mm("ffn_dwg", da, h, 'tn')
    d_wu = _mm("ffn_dwg", db, h, 'tn')
    dx_in, dg = _rms_bwd(x, _row(g), dh, dx)
    return dx_in, dg, d_wg, d_wu, d_wd


def _layer_fwd(x, L, S, tabs):
    D, C, QL, KVL = L['dims']
    rope_cb = (QL + KVL) // HEAD_W
    x1, ffn1 = _ffn_fwd(x, S['ffn1_norm'], L['ffn1_w_gate'], L['ffn1_w_up'], L['ffn1_w_down'])
    h = _rms_fwd(x1, _row(S['mix_norm']))
    pc = _mm("proj_c", h, L['w_c'], 'nt')
    pl_ = _mm("proj_l", h, L['w_l'], 'nt')
    pg = _mm("proj_g", h, L['w_g'], 'nt')
    conv_w = jnp.pad(S['conv_w'], ((0, HALO - CONV_WIDTH), (0, 0)))
    z, c = _conv_fwd(pc, conv_w, _row(S['conv_b']), _row(S['conv_ln_g']), _row(S['conv_ln_b']))
    yc = _mm("conv_out", c, L['w_conv_out'], 'nt')
    cqn, ckvn = _lat_norm_fwd(pl_, _row(S['cq_norm']), _row(S['ckv_norm']), QL, KVL)
    q_raw = _mm("up_q", cqn, L['w_q'], 'nt')
    k_raw = _mm("up_kv", ckvn, L['w_k'], 'nt')
    v = _mm("up_kv", ckvn, L['w_v'], 'nt', out_dtype=_MXU_DTYPE)
    gq, gk = _row(S['q_norm'], HEAD_W), _row(S['k_norm'], HEAD_W)
    qn, kn = _qk_prep_fwd(q_raw, k_raw, pl_, rope_cb, *tabs, gq, gk)
    o, lse = _attn_fwd(qn, kn, v)
    ym = _mm("mla_out", o, L['w_mo'], 'nt')
    y = _gate_fwd(pg, S['gate_bias'], yc, ym)
    x2 = _mm("mix_out", y, L['w_out'], 'nn', res=x1)
    x3, ffn2 = _ffn_fwd(x2, S['ffn2_norm'], L['ffn2_w_gate'], L['ffn2_w_up'], L['ffn2_w_down'])
    saved = dict(ffn1=ffn1, ffn2=ffn2, x1=x1, h=h, pc=pc, pl=pl_, pg=pg, z=z, c=c, yc=yc, cqn=cqn, ckvn=ckvn,
                 q_raw=q_raw, k_raw=k_raw, v=v, qn=qn, kn=kn, o=o, lse=lse, ym=ym, y=y, conv_w=conv_w, gq=gq, gk=gk)
    return x3, saved


def _layer_bwd(dx, L, S, tabs, A):
    D, C, QL, KVL = L['dims']
    rope_cb = (QL + KVL) // HEAD_W
    G, g = {}, {}
    dx, g['ffn2_norm'], G['ffn2_w_gate'], G['ffn2_w_up'], G['ffn2_w_down'] = _ffn_bwd(
        dx, A['ffn2'], S['ffn2_norm'], L['ffn2_w_gate'], L['ffn2_w_up'], L['ffn2_w_down'])
    G['w_out'] = _mm("mix_dwout", A['y'], dx, 'tn')
    dy = _mm("mix_dy", dx, L['w_out'], 'nt')
    dyc, dym, dpg, g['gate_bias'] = _gate_bwd(A['pg'], S['gate_bias'], A['yc'], A['ym'], dy)
    G['w_conv_out'] = _mm("conv_dwout", dyc, A['c'], 'tn')
    dc = _mm("conv_dc", dyc, L['w_conv_out'], 'nn')
    dpc, dcw, g['conv_b'], g['conv_ln_g'], g['conv_ln_b'] = _conv_bwd(
        A['pc'], A['z'], dc, A['conv_w'], _row(S['conv_ln_g']), _row(S['conv_ln_b']))
    g['conv_w'] = dcw[:CONV_WIDTH]
    d_wmo = _mm("mla_dwout", dym, A['o'], 'tn')
    G['w_mla_out'] = d_wmo.reshape(D, N_HEADS, HEAD_W)[:, :, :V_DIM].reshape(D, N_HEADS * V_DIM)
    do = _mm("mla_do", dym, L['w_mo'], 'nn', out_dtype=_MXU_DTYPE)
    dqn, dkn, dv = _attn_bwd(A['qn'], A['kn'], A['v'], A['o'], do, A['lse'])
    dq_raw, dk_raw, drope, dgq, dgk = _qk_prep_bwd(A['q_raw'], A['k_raw'], A['pl'], rope_cb, *tabs, A['gq'], A['gk'], dqn, dkn)
    g['q_norm'], g['k_norm'] = dgq[:, :QK_DIM], dgk[:, :QK_DIM]
    G['w_uq'] = _unpad_heads(_mm("up_dwq", dq_raw, A['cqn'], 'tn'), QK_DIM).reshape(N_HEADS * QK_DIM, QL)
    d_wk = _unpad_heads(_mm("up_dwkv", dk_raw, A['ckvn'], 'tn'), NOPE_DIM)
    d_wv = _unpad_heads(_mm("up_dwkv", dv, A['ckvn'], 'tn'), V_DIM)
    G['w_ukv'] = jnp.concatenate([d_wk, d_wv], axis=1).reshape(N_HEADS * (NOPE_DIM + V_DIM), KVL)
    dcq = _mm("up_dcq", dq_raw, L['w_q'], 'nn')
    dckv = _mm("up_dckv_k", dk_raw, L['w_k'], 'nn')
    dckv = _mm("up_dckv_v", dv, L['w_v'], 'nn', res=dckv)
    dpl, g['cq_norm'], g['ckv_norm'] = _lat_norm_bwd(A['pl'], _row(S['cq_norm']), _row(S['ckv_norm']), dcq, dckv, drope, QL, KVL)
    d_wc = _mm("proj_dwc", dpc, A['h'], 'tn')
    d_wl = _mm("proj_dwl", dpl, A['h'], 'tn')
    d_wg = _mm("proj_dwg", dpg, A['h'], 'tn')
    ql = QL + KVL
    G['w_in'] = jnp.concatenate([d_wc, d_wl[:ql], d_wl[ql + NOPE_DIM:ql + NOPE_DIM + ROPE_DIM], d_wg], axis=0)
    dh = _mm("proj_dh_c", dpc, L['w_c'], 'nn')
    dh = _mm("proj_dh_l", dpl, L['w_l'], 'nn', res=dh)
    dh = _mm("proj_dh_g", dpg, L['w_g'], 'nn', res=dh)
    dx, g['mix_norm'] = _rms_bwd(A['x1'], _row(S['mix_norm']), dh, dx)
    dx, g['ffn1_norm'], G['ffn1_w_gate'], G['ffn1_w_up'], G['ffn1_w_down'] = _ffn_bwd(
        dx, A['ffn1'], S['ffn1_norm'], L['ffn1_w_gate'], L['ffn1_w_up'], L['ffn1_w_down'])
    return dx, G, g


def kernel(x, positions, ffn1_norm, ffn1_w_gate, ffn1_w_up, ffn1_w_down, mix_norm, w_in, gate_bias, conv_w, conv_b, conv_ln_g, conv_ln_b, w_conv_out, cq_norm, ckv_norm, w_uq, w_ukv, q_norm, k_norm, w_mla_out, w_out, ffn2_norm, ffn2_w_gate, ffn2_w_up, ffn2_w_down, loss_target, m_ffn1_norm, m_ffn1_w_gate, m_ffn1_w_up, m_ffn1_w_down, m_mix_norm, m_w_in, m_gate_bias, m_conv_w, m_conv_b, m_conv_ln_g, m_conv_ln_b, m_w_conv_out, m_cq_norm, m_ckv_norm, m_w_uq, m_w_ukv, m_q_norm, m_k_norm, m_w_mla_out, m_w_out, m_ffn2_norm, m_ffn2_w_gate, m_ffn2_w_up, m_ffn2_w_down, v_ffn1_norm, v_ffn1_w_gate, v_ffn1_w_up, v_ffn1_w_down, v_mix_norm, v_w_in, v_gate_bias, v_conv_w, v_conv_b, v_conv_ln_g, v_conv_ln_b, v_w_conv_out, v_cq_norm, v_ckv_norm, v_w_uq, v_w_ukv, v_q_norm, v_k_norm, v_w_mla_out, v_w_out, v_ffn2_norm, v_ffn2_w_gate, v_ffn2_w_up, v_ffn2_w_down):
    w = dict(zip(WEIGHTS, (ffn1_norm, ffn1_w_gate, ffn1_w_up, ffn1_w_down, mix_norm, w_in, gate_bias, conv_w, conv_b, conv_ln_g, conv_ln_b, w_conv_out, cq_norm, ckv_norm, w_uq, w_ukv, q_norm, k_norm, w_mla_out, w_out, ffn2_norm, ffn2_w_gate, ffn2_w_up, ffn2_w_down)))
    m = dict(zip(WEIGHTS, (m_ffn1_norm, m_ffn1_w_gate, m_ffn1_w_up, m_ffn1_w_down, m_mix_norm, m_w_in, m_gate_bias, m_conv_w, m_conv_b, m_conv_ln_g, m_conv_ln_b, m_w_conv_out, m_cq_norm, m_ckv_norm, m_w_uq, m_w_ukv, m_q_norm, m_k_norm, m_w_mla_out, m_w_out, m_ffn2_norm, m_ffn2_w_gate, m_ffn2_w_up, m_ffn2_w_down)))
    v = dict(zip(WEIGHTS, (v_ffn1_norm, v_ffn1_w_gate, v_ffn1_w_up, v_ffn1_w_down, v_mix_norm, v_w_in, v_gate_bias, v_conv_w, v_conv_b, v_conv_ln_g, v_conv_ln_b, v_w_conv_out, v_cq_norm, v_ckv_norm, v_w_uq, v_w_ukv, v_q_norm, v_k_norm, v_w_mla_out, v_w_out, v_ffn2_norm, v_ffn2_w_gate, v_ffn2_w_up, v_ffn2_w_down)))
    depth = ffn1_norm.shape[0]
    T, D = x.shape[1], x.shape[2]
    xs = x.reshape(T, D)
    target = loss_target.reshape(T, D)
    tabs = _rope_tables(positions, T)
    me = 4 * lax.axis_index("x") + 2 * lax.axis_index("y") + lax.axis_index("c")

    big_names = [(l, n) for l in range(depth) for n in BIG]
    shards = []
    for l, n in big_names:
        s = w[n][l]
        shards.append((s.T if BIG[n] else s).astype(_WIRE_DTYPE))
    packed, spans = _pack_rows(shards)
    gathered = _all_gather("gather_weights", packed)
    small_names = [(l, n) for l in range(depth) for n in SMALL_SHARDED]
    small_packed, small_spans = _pack_flat([w[n][l].T for l, n in small_names])
    small_gathered = _all_gather("gather_small", small_packed)

    layers = []
    for l in range(depth):
        W = {}
        for (ll, n), span in zip(big_names, spans):
            if ll == l:
                sh = _unpack_rows(gathered, span, (N_DEV,))
                W[n] = sh.reshape(N_DEV * sh.shape[1], sh.shape[2])
        S = {n: w[n][l] for n in REPLICATED}
        for (ll, n), span in zip(small_names, small_spans):
            if ll == l:
                sh = _unpack_flat(small_gathered, span, (N_DEV,))
                S[n] = sh.reshape(N_DEV * sh.shape[1], sh.shape[2]).T
        layers.append((_layer_weights(W), S))

    acts = []
    hcur = xs
    for L, S in layers:
        hcur, saved = _layer_fwd(hcur, L, S, tabs)
        acts.append(saved)
    dx, loss_part = _loss_grad(hcur, target)
    big_grads, small_grads = [None] * depth, [None] * depth
    for l in reversed(range(depth)):
        L, S = layers[l]
        dx, big_grads[l], small_grads[l] = _layer_bwd(dx, L, S, tabs, acts[l])

    send, gspans = _pack_rows([jnp.swapaxes(big_grads[l][n].reshape(N_DEV // 2, 2, -1, big_grads[l][n].shape[1]), 0, 1)
                               .astype(_WIRE_DTYPE) for l, n in big_names], (2, N_DEV // 2))
    pair = _pair_sum("pair_sum_grads", send, _pair_exchange("pair_exchange_grads", send))
    reduced = _sum_blocks("sum_grads", _chip_all_to_all("scatter_grads", pair))
    small_list = [(l, n) for l in range(depth) for n in REPLICATED + SMALL_SHARDED]
    small_send, sspans = _pack_flat([small_grads[l][n] for l, n in small_list] + [loss_part])
    small_all = _sum_blocks("sum_small", _all_gather("gather_small_grads", small_send))

    loss = _unpack_flat(small_all, sspans[-1])[0, 0]
    grads = {}
    for n in WEIGHTS:
        per_layer = []
        for l in range(depth):
            if n in BIG:
                gt = _unpack_rows(reduced, gspans[big_names.index((l, n))])
                per_layer.append(gt.T if BIG[n] else gt)
            else:
                gs = _unpack_flat(small_all, sspans[small_list.index((l, n))])
                if n in SMALL_SHARDED:
                    cols = w[n].shape[-1]
                    gs = lax.dynamic_slice_in_dim(gs, me * cols, cols, axis=1)
                per_layer.append(gs.reshape(w[n].shape[1:]))
        grads[n] = jnp.stack(per_layer, axis=0)

    delta, new_m, new_v = {}, {}, {}
    for n in WEIGHTS:
        delta[n], new_m[n], new_v[n] = _adamw(w[n], grads[n], m[n], v[n])
    return (loss, dx.reshape(1, T, D), *[grads[n] for n in WEIGHTS], *[delta[n] for n in WEIGHTS],
            *[new_m[n] for n in WEIGHTS], *[new_v[n] for n in WEIGHTS])
```

```python
import jax
import jax.numpy as jnp
from jax import lax
from jax.experimental import pallas as pl
from jax.experimental.pallas import tpu as pltpu

F32 = jnp.float32
_MXU_DTYPE = jnp.bfloat16
_WIRE_DTYPE = jnp.bfloat16

_LANES = 128
_SUBLANES = 8
_PACKED_ROWS = 16
_V7X_VMEM_BYTES = 64 * 2 ** 20
_VMEM_HEADROOM = 8 * 2 ** 20

N_DEV = 8
EPS = 1e-6
CHUNK = 64
CONV_WIDTH = 31
N_HEADS = 8
NOPE_DIM = 64
ROPE_DIM = 32
QK_DIM = NOPE_DIM + ROPE_DIM
V_DIM = 64
HEAD_W = _LANES
ROPE_THETA = 10000.0
HALO = 32

ADAM_LR = 0.001
ADAM_B1 = 0.9
ADAM_B2 = 0.999
ADAM_EPS = 1e-08
ADAM_WD = 0.01
ADAM_STEP = 10

_ROW_TILE = 512
_MM_TILE = 512
_MM_TILE_N = 1408
_MM_TILE_K = 1024
_TN_TILE_M = 1408
_ATT_TILE = 512
_ATT_ROWS = 32
_PACK_W = 1024
_PACK_ROW_MULT = 128

WEIGHTS = ['ffn1_norm', 'ffn1_w_gate', 'ffn1_w_up', 'ffn1_w_down', 'mix_norm', 'w_in', 'gate_bias',
           'conv_w', 'conv_b', 'conv_ln_g', 'conv_ln_b', 'w_conv_out', 'cq_norm', 'ckv_norm', 'w_uq',
           'w_ukv', 'q_norm', 'k_norm', 'w_mla_out', 'w_out', 'ffn2_norm', 'ffn2_w_gate', 'ffn2_w_up',
           'ffn2_w_down']
BIG = {'ffn1_w_gate': True, 'ffn1_w_up': True, 'ffn1_w_down': False, 'w_in': True, 'w_conv_out': True,
       'w_uq': True, 'w_ukv': True, 'w_mla_out': True, 'w_out': False, 'ffn2_w_gate': True,
       'ffn2_w_up': True, 'ffn2_w_down': False}
SMALL_SHARDED = ['conv_w', 'gate_bias']
REPLICATED = ['ffn1_norm', 'mix_norm', 'conv_b', 'conv_ln_g', 'conv_ln_b', 'cq_norm', 'ckv_norm',
              'q_norm', 'k_norm', 'ffn2_norm']


def _tile(n, target, mult=_LANES):
    if n <= target:
        return n
    for d in range(target - target % mult, 0, -mult):
        if n % d == 0:
            return d
    return n


def _nbytes(shape, dtype):
    n = 1
    for s in shape:
        n *= s
    return n * jnp.dtype(dtype).itemsize


def _params(dims, block_bytes, extra_bytes=0):
    need = 2 * block_bytes + extra_bytes + _VMEM_HEADROOM
    assert need <= _V7X_VMEM_BYTES, (need, dims)
    return pltpu.CompilerParams(dimension_semantics=dims, vmem_limit_bytes=_V7X_VMEM_BYTES - _VMEM_HEADROOM)


def _mxu(x, scale=None):
    if scale is not None:
        x = x * scale
    return x if x.dtype == _MXU_DTYPE else x.astype(_MXU_DTYPE)


def _dot(a, b, mode):
    dims = {'nn': (((1,), (0,)), ((), ())), 'nt': (((1,), (1,)), ((), ())), 'tn': (((0,), (0,)), ((), ()))}[mode]
    return lax.dot_general(a, b, dims, preferred_element_type=F32)


def _colsum(x):
    return jnp.sum(x, axis=0, keepdims=True)


def _sig(x):
    return jax.nn.sigmoid(x)


def _mm(name, a, b, mode, *, out_dtype=F32, res=None, scale=None, a_scale=None, tm=None, tn=None, tk=None):
    if mode == 'nn':
        (M, K), (_, N) = a.shape, b.shape
    elif mode == 'nt':
        (M, K), (N, _) = a.shape, b.shape
    else:
        (K, M), (_, N) = a.shape, b.shape
    tm = _tile(M, tm or (_TN_TILE_M if mode == 'tn' else _MM_TILE))
    tn = _tile(N, tn or _MM_TILE_N)
    tk = _tile(K, tk or (_MM_TILE_K if mode == 'tn' else 4096))
    nk = K // tk
    grid = (M // tm, N // tn, nk)
    a_spec = pl.BlockSpec((tk, tm), lambda i, j, k: (k, i)) if mode == 'tn' else pl.BlockSpec((tm, tk), lambda i, j, k: (i, k))
    b_spec = pl.BlockSpec((tn, tk), lambda i, j, k: (j, k)) if mode == 'nt' else pl.BlockSpec((tk, tn), lambda i, j, k: (k, j))
    o_spec = pl.BlockSpec((tm, tn), lambda i, j, k: (i, j))
    has_res = res is not None

    def body(*refs):
        a_ref, b_ref = refs[0], refs[1]
        res_ref = refs[2] if has_res else None
        o_ref = refs[3] if has_res else refs[2]
        acc_ref = refs[-1] if nk > 1 else None

        def finish(p):
            if scale is not None:
                p = p * scale
            if has_res:
                p = res_ref[...] + p
            o_ref[...] = p.astype(o_ref.dtype)

        p = _dot(_mxu(a_ref[...], a_scale), _mxu(b_ref[...]), mode)
        if nk == 1:
            finish(p)
        else:
            k = pl.program_id(2)

            @pl.when(k == 0)
            def _():
                acc_ref[...] = p

            @pl.when(k > 0)
            def _():
                acc_ref[...] += p

            @pl.when(k == nk - 1)
            def _():
                finish(acc_ref[...])

    blocks = (_nbytes((tm, tk), a.dtype) + _nbytes((tk, tn), b.dtype) + _nbytes((tm, tn), out_dtype)
              + (_nbytes((tm, tn), F32) if has_res else 0))
    extra = 3 * _nbytes((tm, tn), F32) + _nbytes((tm, tk), _MXU_DTYPE) + _nbytes((tk, tn), _MXU_DTYPE)
    return pl.pallas_call(
        body, name=name, grid=grid,
        in_specs=[a_spec, b_spec] + ([o_spec] if has_res else []),
        out_specs=o_spec,
        out_shape=jax.ShapeDtypeStruct((M, N), out_dtype),
        scratch_shapes=[pltpu.VMEM((tm, tn), F32)] if nk > 1 else [],
        compiler_params=_params(("parallel", "parallel", "arbitrary"), blocks, extra),
    )(*([a, b] + ([res] if has_res else [])))


def _ffn_up(h, wg, wu):
    T, D = h.shape
    F = wg.shape[0]
    tm, tn = _tile(T, _MM_TILE), _tile(F, _MM_TILE_N)

    def body(h_ref, wg_ref, wu_ref, a_ref, b_ref, u_ref):
        hv = h_ref[...]
        a = _dot(hv, wg_ref[...], 'nt')
        b = _dot(hv, wu_ref[...], 'nt')
        a_ref[...] = a
        b_ref[...] = b
        u_ref[...] = (a * _sig(a) * b).astype(u_ref.dtype)

    w_spec = pl.BlockSpec((tn, D), lambda j, i: (j, 0))
    o_spec = pl.BlockSpec((tm, tn), lambda j, i: (i, j))
    blocks = _nbytes((tm, D), h.dtype) + 2 * _nbytes((tn, D), wg.dtype) + 2 * _nbytes((tm, tn), F32) + _nbytes((tm, tn), _MXU_DTYPE)
    return pl.pallas_call(
        body, name="ffn_up", grid=(F // tn, T // tm),
        in_specs=[pl.BlockSpec((tm, D), lambda j, i: (i, 0)), w_spec, w_spec],
        out_specs=[o_spec, o_spec, o_spec],
        out_shape=[jax.ShapeDtypeStruct((T, F), F32), jax.ShapeDtypeStruct((T, F), F32),
                   jax.ShapeDtypeStruct((T, F), _MXU_DTYPE)],
        compiler_params=_params(("parallel", "parallel"), blocks, 4 * _nbytes((tm, tn), F32)),
    )(h, wg, wu)


def _ffn_bwd_up(dx, wd, a, b):
    T, D = dx.shape
    F = wd.shape[0]
    tm, tn = _tile(T, _MM_TILE), _tile(F, _MM_TILE_N)

    def body(dx_ref, wd_ref, a_ref, b_ref, da_ref, db_ref):
        du = _dot(_mxu(dx_ref[...], 0.5), wd_ref[...], 'nt')
        av, bv = a_ref[...], b_ref[...]
        s = _sig(av)
        db_ref[...] = (du * av * s).astype(db_ref.dtype)
        da_ref[...] = (du * bv * (s * (1.0 + av * (1.0 - s)))).astype(da_ref.dtype)

    t_spec = pl.BlockSpec((tm, tn), lambda j, i: (i, j))
    blocks = _nbytes((tm, D), F32) + _nbytes((tn, D), wd.dtype) + 2 * _nbytes((tm, tn), F32) + 2 * _nbytes((tm, tn), _MXU_DTYPE)
    return pl.pallas_call(
        body, name="ffn_bwd_up", grid=(F // tn, T // tm),
        in_specs=[pl.BlockSpec((tm, D), lambda j, i: (i, 0)), pl.BlockSpec((tn, D), lambda j, i: (j, 0)), t_spec, t_spec],
        out_specs=[t_spec, t_spec],
        out_shape=[jax.ShapeDtypeStruct((T, F), _MXU_DTYPE)] * 2,
        compiler_params=_params(("parallel", "parallel"), blocks, 5 * _nbytes((tm, tn), F32)),
    )(dx, wd, a, b)


def _row_spec(tm, w, cb=0):
    return pl.BlockSpec((tm, w), lambda i: (i, cb))


def _full_spec(shape):
    return pl.BlockSpec(shape, lambda i: (0,) * len(shape))


def _init_acc(refs):
    @pl.when(pl.program_id(0) == 0)
    def _():
        for r in refs:
            r[...] = jnp.zeros_like(r)


def _rms_fwd(x, g):
    T, D = x.shape
    tm = _tile(T, _ROW_TILE, _PACKED_ROWS)

    def body(x_ref, g_ref, h_ref):
        xv = x_ref[...]
        r = lax.rsqrt(jnp.mean(xv * xv, axis=-1, keepdims=True) + EPS)
        h_ref[...] = (xv * r * g_ref[...]).astype(h_ref.dtype)

    return pl.pallas_call(
        body, name="rms_fwd", grid=(T // tm,),
        in_specs=[_row_spec(tm, D), _full_spec((1, D))], out_specs=_row_spec(tm, D),
        out_shape=jax.ShapeDtypeStruct((T, D), _MXU_DTYPE),
        compiler_params=_params(("parallel",), 2 * _nbytes((tm, D), F32), 4 * _nbytes((tm, D), F32)),
    )(x, g)


def _rms_bwd(x, g, dh, dres):
    T, D = x.shape
    tm = _tile(T, _ROW_TILE, _SUBLANES)

    def body(x_ref, g_ref, dh_ref, dres_ref, dx_ref, dg_ref):
        _init_acc([dg_ref])
        xv, dhv = x_ref[...], dh_ref[...]
        r = lax.rsqrt(jnp.mean(xv * xv, axis=-1, keepdims=True) + EPS)
        xh = xv * r
        dxh = dhv * g_ref[...]
        dx_ref[...] = dres_ref[...] + r * (dxh - xh * jnp.mean(dxh * xh, axis=-1, keepdims=True))
        dg_ref[...] += _colsum(dhv * xh)

    return pl.pallas_call(
        body, name="rms_bwd", grid=(T // tm,),
        in_specs=[_row_spec(tm, D), _full_spec((1, D)), _row_spec(tm, D), _row_spec(tm, D)],
        out_specs=[_row_spec(tm, D), _full_spec((1, D))],
        out_shape=[jax.ShapeDtypeStruct((T, D), F32), jax.ShapeDtypeStruct((1, D), F32)],
        compiler_params=_params(("arbitrary",), 4 * _nbytes((tm, D), F32), 6 * _nbytes((tm, D), F32)),
    )(x, g, dh, dres)


def _loss_grad(y, target):
    T, D = y.shape
    tm = _tile(T, _ROW_TILE, _SUBLANES)

    def body(y_ref, t_ref, dy_ref, l_ref):
        _init_acc([l_ref])
        err = y_ref[...] - t_ref[...]
        dy_ref[...] = err * (1.0 / D)
        per_row = jnp.sum(err * err, axis=-1, keepdims=True) * (0.5 / D)
        l_ref[...] += jnp.sum(per_row, axis=0, keepdims=True)

    return pl.pallas_call(
        body, name="loss_grad", grid=(T // tm,),
        in_specs=[_row_spec(tm, D), _row_spec(tm, D)],
        out_specs=[_row_spec(tm, D), _full_spec((1, _LANES))],
        out_shape=[jax.ShapeDtypeStruct((T, D), F32), jax.ShapeDtypeStruct((1, _LANES), F32)],
        compiler_params=_params(("arbitrary",), 3 * _nbytes((tm, D), F32), 3 * _nbytes((tm, D), F32)),
    )(y, target)


def _glu(pc, C):
    a, gate = pc[:, :C], pc[:, C:]
    s = _sig(gate)
    return a, s, a * s


def _conv_fwd(pc, w, cb, lg, lb):
    T, C2 = pc.shape
    C = C2 // 2
    tm = _tile(T, _ROW_TILE // 2, HALO)
    per = tm // HALO

    def body(pc_ref, ph_ref, w_ref, cb_ref, lg_ref, lb_ref, z_ref, c_ref, ubuf):
        i = pl.program_id(0)
        _, _, u_cur = _glu(pc_ref[...], C)
        _, _, u_prev = _glu(ph_ref[...], C)
        ubuf[0:HALO, :] = jnp.where(i > 0, u_prev, 0.0)
        ubuf[HALO:HALO + tm, :] = u_cur
        acc = jnp.zeros((tm, C), F32)
        for k in range(CONV_WIDTH):
            off = HALO - (CONV_WIDTH - 1) + k
            acc = acc + w_ref[k:k + 1, :] * ubuf[off:off + tm, :]
        z = acc + cb_ref[...]
        z_ref[...] = z
        zc = z - jnp.mean(z, axis=-1, keepdims=True)
        y = zc * lax.rsqrt(jnp.mean(zc * zc, axis=-1, keepdims=True) + EPS) * lg_ref[...] + lb_ref[...]
        c_ref[...] = (y * _sig(y)).astype(c_ref.dtype)

    return pl.pallas_call(
        body, name="conv_fwd", grid=(T // tm,),
        in_specs=[_row_spec(tm, C2), pl.BlockSpec((HALO, C2), lambda i: (jnp.maximum(i * per - 1, 0), 0)),
                  _full_spec((HALO, C)), _full_spec((1, C)), _full_spec((1, C)), _full_spec((1, C))],
        out_specs=[_row_spec(tm, C), _row_spec(tm, C)],
        out_shape=[jax.ShapeDtypeStruct((T, C), F32), jax.ShapeDtypeStruct((T, C), _MXU_DTYPE)],
        scratch_shapes=[pltpu.VMEM((tm + HALO, C), F32)],
        compiler_params=_params(("parallel",), 3 * _nbytes((tm, C2), F32), 8 * _nbytes((tm, C), F32)),
    )(pc, pc, w, cb, lg, lb)


def _conv_bwd(pc, z, dc, w, lg, lb):
    T, C2 = pc.shape
    C = C2 // 2
    tm = _tile(T, _ROW_TILE // 2, HALO)
    per = tm // HALO
    n = T // tm
    last_halo = T // HALO - 1

    def body(pc_ref, ph_ref, z_ref, zn_ref, dc_ref, dcn_ref, w_ref, lg_ref, lb_ref,
             dpc_ref, dw_ref, dcb_ref, dlg_ref, dlb_ref, ubuf, dzbuf):
        i = pl.program_id(0)
        _init_acc([dw_ref, dcb_ref, dlg_ref, dlb_ref])
        g, b = lg_ref[...], lb_ref[...]

        def ln_swish_bwd(zv, dcv):
            zc = zv - jnp.mean(zv, axis=-1, keepdims=True)
            r = lax.rsqrt(jnp.mean(zc * zc, axis=-1, keepdims=True) + EPS)
            zh = zc * r
            y = zh * g + b
            s = _sig(y)
            dy = dcv * (s * (1.0 + y * (1.0 - s)))
            dyg = dy * g
            dz = r * (dyg - jnp.mean(dyg, axis=-1, keepdims=True) - zh * jnp.mean(dyg * zh, axis=-1, keepdims=True))
            return dz, dy, zh

        dz_c, dy_c, zh_c = ln_swish_bwd(z_ref[...], dc_ref[...])
        dz_n, _, _ = ln_swish_bwd(zn_ref[...], dcn_ref[...])
        dlg_ref[...] += _colsum(dy_c * zh_c)
        dlb_ref[...] += _colsum(dy_c)
        dcb_ref[...] += _colsum(dz_c)
        dzbuf[0:tm, :] = dz_c
        dzbuf[tm:tm + HALO, :] = jnp.where(i < n - 1, dz_n, 0.0)
        a, s, u_cur = _glu(pc_ref[...], C)
        _, _, u_prev = _glu(ph_ref[...], C)
        ubuf[0:HALO, :] = jnp.where(i > 0, u_prev, 0.0)
        ubuf[HALO:HALO + tm, :] = u_cur
        du = jnp.zeros((tm, C), F32)
        for k in range(CONV_WIDTH):
            back = CONV_WIDTH - 1 - k
            du = du + w_ref[k:k + 1, :] * dzbuf[back:back + tm, :]
            off = HALO - (CONV_WIDTH - 1) + k
            dw_ref[k:k + 1, :] += _colsum(dz_c * ubuf[off:off + tm, :])
        dpc_ref[:, :C] = (du * s).astype(dpc_ref.dtype)
        dpc_ref[:, C:] = (du * a * s * (1.0 - s)).astype(dpc_ref.dtype)

    nxt = lambda i: (jnp.minimum((i + 1) * per, last_halo), 0)
    vec = _full_spec((1, C))
    return pl.pallas_call(
        body, name="conv_bwd", grid=(n,),
        in_specs=[_row_spec(tm, C2), pl.BlockSpec((HALO, C2), lambda i: (jnp.maximum(i * per - 1, 0), 0)),
                  _row_spec(tm, C), pl.BlockSpec((HALO, C), nxt), _row_spec(tm, C), pl.BlockSpec((HALO, C), nxt),
                  _full_spec((HALO, C)), vec, vec],
        out_specs=[_row_spec(tm, C2), _full_spec((HALO, C)), vec, vec, vec],
        out_shape=[jax.ShapeDtypeStruct((T, C2), _MXU_DTYPE), jax.ShapeDtypeStruct((HALO, C), F32)]
                  + [jax.ShapeDtypeStruct((1, C), F32)] * 3,
        scratch_shapes=[pltpu.VMEM((tm + HALO, C), F32), pltpu.VMEM((tm + HALO, C), F32)],
        compiler_params=_params(("arbitrary",), 5 * _nbytes((tm, C2), F32), 12 * _nbytes((tm, C), F32)),
    )(pc, pc, z, z, dc, dc, w, lg, lb)


def _lat_norm_fwd(pl_, gq, gkv, QL, KVL):
    T, W = pl_.shape
    tm = _tile(T, _ROW_TILE, _PACKED_ROWS)

    def body(p_ref, gq_ref, gkv_ref, cq_ref, ckv_ref):
        def norm(xv, gv):
            return xv * lax.rsqrt(jnp.mean(xv * xv, axis=-1, keepdims=True) + EPS) * gv
        cq_ref[...] = norm(p_ref[:, :QL], gq_ref[...]).astype(cq_ref.dtype)
        ckv_ref[...] = norm(p_ref[:, QL:QL + KVL], gkv_ref[...]).astype(ckv_ref.dtype)

    return pl.pallas_call(
        body, name="lat_norm_fwd", grid=(T // tm,),
        in_specs=[_row_spec(tm, W), _full_spec((1, QL)), _full_spec((1, KVL))],
        out_specs=[_row_spec(tm, QL), _row_spec(tm, KVL)],
        out_shape=[jax.ShapeDtypeStruct((T, QL), _MXU_DTYPE), jax.ShapeDtypeStruct((T, KVL), _MXU_DTYPE)],
        compiler_params=_params(("parallel",), 2 * _nbytes((tm, W), F32), 4 * _nbytes((tm, W), F32)),
    )(pl_, gq, gkv)


def _lat_norm_bwd(pl_, gq, gkv, dcq, dckv, drope, QL, KVL):
    T, W = pl_.shape
    tm = _tile(T, _ROW_TILE, _PACKED_ROWS)

    def body(p_ref, gq_ref, gkv_ref, dcq_ref, dckv_ref, dr_ref, dp_ref, dgq_ref, dgkv_ref):
        _init_acc([dgq_ref, dgkv_ref])

        def bwd(xv, gv, dv, dg_ref):
            r = lax.rsqrt(jnp.mean(xv * xv, axis=-1, keepdims=True) + EPS)
            xh = xv * r
            dxh = dv * gv
            dg_ref[...] += _colsum(dv * xh)
            return r * (dxh - xh * jnp.mean(dxh * xh, axis=-1, keepdims=True))

        dp_ref[:, :QL] = bwd(p_ref[:, :QL], gq_ref[...], dcq_ref[...], dgq_ref).astype(dp_ref.dtype)
        dp_ref[:, QL:QL + KVL] = bwd(p_ref[:, QL:QL + KVL], gkv_ref[...], dckv_ref[...], dgkv_ref).astype(dp_ref.dtype)
        dp_ref[:, QL + KVL:] = dr_ref[...].astype(dp_ref.dtype)

    return pl.pallas_call(
        body, name="lat_norm_bwd", grid=(T // tm,),
        in_specs=[_row_spec(tm, W), _full_spec((1, QL)), _full_spec((1, KVL)), _row_spec(tm, QL), _row_spec(tm, KVL),
                  _row_spec(tm, HEAD_W)],
        out_specs=[_row_spec(tm, W), _full_spec((1, QL)), _full_spec((1, KVL))],
        out_shape=[jax.ShapeDtypeStruct((T, W), _MXU_DTYPE), jax.ShapeDtypeStruct((1, QL), F32),
                   jax.ShapeDtypeStruct((1, KVL), F32)],
        compiler_params=_params(("arbitrary",), 4 * _nbytes((tm, W), F32), 6 * _nbytes((tm, W), F32)),
    )(pl_, gq, gkv, dcq, dckv, drope)


def _norm_rope(xv, gv, ct, sa, sb):
    r = lax.rsqrt(jnp.sum(xv * xv, axis=-1, keepdims=True) * (1.0 / QK_DIM) + EPS)
    xn = xv * r * gv
    half = ROPE_DIM // 2
    return xn * ct + pltpu.roll(xn, HEAD_W - half, 1) * sa + pltpu.roll(xn, half, 1) * sb


def _norm_rope_bwd(xv, gv, ct, sa, sb, dout):
    half = ROPE_DIM // 2
    dxn = dout * ct + pltpu.roll(dout * sa, half, 1) + pltpu.roll(dout * sb, HEAD_W - half, 1)
    r = lax.rsqrt(jnp.sum(xv * xv, axis=-1, keepdims=True) * (1.0 / QK_DIM) + EPS)
    xh = xv * r
    dxh = dxn * gv
    dx = r * (dxh - xh * (jnp.sum(dxh * xh, axis=-1, keepdims=True) * (1.0 / QK_DIM)))
    return dx, _colsum(dxn * xh)


def _qk_prep_fwd(q_raw, k_raw, pl_, rope_cb, ct, sa, sb, gq, gk):
    T, HW = q_raw.shape
    H = HW // HEAD_W
    tm = _tile(T, _ROW_TILE, _PACKED_ROWS)

    def body(q_ref, k_ref, r_ref, ct_ref, sa_ref, sb_ref, gq_ref, gk_ref, qn_ref, kn_ref):
        ctv, sav, sbv, rv = ct_ref[...], sa_ref[...], sb_ref[...], r_ref[...]
        for h in range(H):
            hs = slice(h * HEAD_W, (h + 1) * HEAD_W)
            qn_ref[:, hs] = _norm_rope(q_ref[:, hs], gq_ref[...], ctv, sav, sbv).astype(qn_ref.dtype)
            kn_ref[:, hs] = _norm_rope(k_ref[:, hs] + rv, gk_ref[...], ctv, sav, sbv).astype(kn_ref.dtype)

    tb = _row_spec(tm, HEAD_W)
    gb = _full_spec((1, HEAD_W))
    return pl.pallas_call(
        body, name="qk_prep_fwd", grid=(T // tm,),
        in_specs=[_row_spec(tm, HW), _row_spec(tm, HW), _row_spec(tm, HEAD_W, rope_cb), tb, tb, tb, gb, gb],
        out_specs=[_row_spec(tm, HW), _row_spec(tm, HW)],
        out_shape=[jax.ShapeDtypeStruct((T, HW), _MXU_DTYPE)] * 2,
        compiler_params=_params(("parallel",), 3 * _nbytes((tm, HW), F32) + 4 * _nbytes((tm, HEAD_W), F32), 8 * _nbytes((tm, HEAD_W), F32)),
    )(q_raw, k_raw, pl_, ct, sa, sb, gq, gk)


def _qk_prep_bwd(q_raw, k_raw, pl_, rope_cb, ct, sa, sb, gq, gk, dqn, dkn):
    T, HW = q_raw.shape
    H = HW // HEAD_W
    tm = _tile(T, _ROW_TILE, _PACKED_ROWS)

    def body(q_ref, k_ref, r_ref, ct_ref, sa_ref, sb_ref, gq_ref, gk_ref, dqn_ref, dkn_ref,
             dq_ref, dk_ref, dr_ref, dgq_ref, dgk_ref):
        _init_acc([dgq_ref, dgk_ref])
        ctv, sav, sbv, rv = ct_ref[...], sa_ref[...], sb_ref[...], r_ref[...]
        dr = dgq_sum = dgk_sum = None
        for h in range(H):
            hs = slice(h * HEAD_W, (h + 1) * HEAD_W)
            dq, dgq = _norm_rope_bwd(q_ref[:, hs], gq_ref[...], ctv, sav, sbv, dqn_ref[:, hs])
            dk, dgk = _norm_rope_bwd(k_ref[:, hs] + rv, gk_ref[...], ctv, sav, sbv, dkn_ref[:, hs])
            dq_ref[:, hs] = dq.astype(dq_ref.dtype)
            dk_ref[:, hs] = dk.astype(dk_ref.dtype)
            dr = dk if h == 0 else dr + dk
            dgq_sum = dgq if h == 0 else dgq_sum + dgq
            dgk_sum = dgk if h == 0 else dgk_sum + dgk
        dr_ref[...] = dr
        dgq_ref[...] += dgq_sum
        dgk_ref[...] += dgk_sum

    tb = _row_spec(tm, HEAD_W)
    gb = _full_spec((1, HEAD_W))
    hb = _row_spec(tm, HW)
    return pl.pallas_call(
        body, name="qk_prep_bwd", grid=(T // tm,),
        in_specs=[hb, hb, _row_spec(tm, HEAD_W, rope_cb), tb, tb, tb, gb, gb, hb, hb],
        out_specs=[hb, hb, tb, gb, gb],
        out_shape=[jax.ShapeDtypeStruct((T, HW), _MXU_DTYPE)] * 2 + [jax.ShapeDtypeStruct((T, HEAD_W), F32)]
                  + [jax.ShapeDtypeStruct((1, HEAD_W), F32)] * 2,
        compiler_params=_params(("arbitrary",), 5 * _nbytes((tm, HW), F32) + 5 * _nbytes((tm, HEAD_W), F32), 12 * _nbytes((tm, HEAD_W), F32)),
    )(q_raw, k_raw, pl_, ct, sa, sb, gq, gk, dqn, dkn)


_NEG = -1e30
_LOG2E = 1.4426950408889634
_SCORE_C = QK_DIM ** -0.5 * _LOG2E


def _pieces(t, rb, diag):
    assert CHUNK % rb == 0 and rb % _SUBLANES == 0
    out = []
    for r in range(t // rb):
        lo = (r * rb // CHUNK) * CHUNK if diag else 0
        for c in range(t // _LANES):
            cut = None if (c + 1) * _LANES <= lo else max(lo - c * _LANES, 0)
            out.append((slice(r * rb, (r + 1) * rb), slice(c * _LANES, (c + 1) * _LANES), c, cut))
    return out


def _groups(x):
    return x.reshape(x.shape[0] // _SUBLANES, _SUBLANES, x.shape[1])


def _all_sublanes(x8, op):
    return jnp.broadcast_to(op(x8, axis=0, keepdims=True), x8.shape)


def _lane_ge(rb, cut):
    return lax.broadcasted_iota(jnp.int32, (rb, _LANES), 1) >= cut


def _attn_fwd(q, k, v):
    T, HW = q.shape
    H = HW // HEAD_W
    t = _tile(T, _ATT_TILE)
    rb = min(_ATT_ROWS, t)
    nc = t // _LANES

    def body(q_ref, k_ref, v_ref, o_ref, lse_ref, s_s, p_s, m_s, l_s, acc_s):
        i = pl.program_id(1)
        qv = q_ref[...]
        m_s[...] = jnp.full_like(m_s, _NEG)
        l_s[...] = jnp.zeros_like(l_s)
        acc_s[...] = jnp.zeros_like(acc_s)

        def step(j, diag):
            rows = pl.ds(pl.multiple_of(j * t, t), t)
            s_s[...] = _dot(k_ref[rows, :], qv, 'nt')
            pieces = _pieces(t, rb, diag)

            def scores(rs, cs, cut):
                sb = s_s[rs, cs]
                return jnp.where(_lane_ge(rb, cut), sb, _NEG) if cut else sb

            mx = [None] * nc
            for rs, cs, c, cut in pieces:
                if cut is not None:
                    g = jnp.max(_groups(scores(rs, cs, cut)), axis=0)
                    mx[c] = g if mx[c] is None else jnp.maximum(mx[c], g)
            m_new, alpha = [], []
            for c in range(nc):
                cs = slice(c * _LANES, (c + 1) * _LANES)
                m_prev = m_s[:, cs]
                m_new.append(jnp.maximum(m_prev, _all_sublanes(mx[c], jnp.max)))
                alpha.append(jnp.exp2((m_prev - m_new[c]) * _SCORE_C))
                m_s[:, cs] = m_new[c]
            lsum = [jnp.zeros((_SUBLANES, _LANES), F32)] * nc
            for rs, cs, c, cut in pieces:
                if cut is None:
                    p_s[rs, cs] = jnp.zeros((rb, _LANES), p_s.dtype)
                    continue
                p = jnp.exp2((scores(rs, cs, cut) - jnp.tile(m_new[c], (rb // _SUBLANES, 1))) * _SCORE_C)
                lsum[c] = lsum[c] + jnp.sum(_groups(p), axis=0)
                p_s[rs, cs] = p.astype(p_s.dtype)
            for c in range(nc):
                cs = slice(c * _LANES, (c + 1) * _LANES)
                l_s[:, cs] = alpha[c] * l_s[:, cs] + _all_sublanes(lsum[c], jnp.sum)
            a = jnp.tile(jnp.concatenate(alpha, axis=1), (HEAD_W // _SUBLANES, 1))
            acc_s[...] = a * acc_s[...] + _dot(v_ref[rows, :], p_s[...], 'tn')

        def loop_body(j, carry):
            step(j, False)
            return carry

        lax.fori_loop(0, i, loop_body, 0)
        step(i, True)
        l = l_s[...]
        o_t = acc_s[...] / jnp.tile(l, (HEAD_W // _SUBLANES, 1))
        o_ref[...] = o_t.T.astype(o_ref.dtype)
        lse_ref[...] = m_s[...] * _SCORE_C + jnp.log(l) * _LOG2E

    qb = pl.BlockSpec((t, HEAD_W), lambda h, i: (i, h))
    kb = pl.BlockSpec((T, HEAD_W), lambda h, i: (0, h))
    st = pltpu.VMEM((_SUBLANES, t), F32)
    return pl.pallas_call(
        body, name="attn_fwd", grid=(H, T // t),
        in_specs=[qb, kb, kb], out_specs=[qb, pl.BlockSpec((_SUBLANES, t), lambda h, i: (h, i))],
        out_shape=[jax.ShapeDtypeStruct((T, HW), _MXU_DTYPE), jax.ShapeDtypeStruct((H * _SUBLANES, T), F32)],
        scratch_shapes=[pltpu.VMEM((t, t), F32), pltpu.VMEM((t, t), _MXU_DTYPE), st, st, pltpu.VMEM((HEAD_W, t), F32)],
        compiler_params=_params(("parallel", "arbitrary"), 2 * _nbytes((T, HEAD_W), _MXU_DTYPE) + 3 * _nbytes((t, HEAD_W), F32),
                                4 * _nbytes((t, t), F32)),
    )(q, k, v)


def _attn_bwd(q, k, v, o, do, lse2):
    T, HW = q.shape
    H = HW // HEAD_W
    t = _tile(T, _ATT_TILE)
    n = T // t
    rb = min(_ATT_ROWS, t)
    scale = QK_DIM ** -0.5

    def body(k_ref, v_ref, q_ref, o_ref, do_ref, lse_ref, dq_ref, dk_ref, dv_ref,
             s_s, dp_s, p_s, ds_s, dk_s, dv_s, delta_s, dqt_s):
        j = pl.program_id(1)

        @pl.when(j == 0)
        def _():
            dqt_s[...] = jnp.zeros_like(dqt_s)
            for b in range(n):
                rs = slice(b * t, (b + 1) * t)
                d = jnp.sum(do_ref[rs, :].astype(F32) * o_ref[rs, :].astype(F32), axis=1, keepdims=True)
                delta_s[:, rs] = jnp.broadcast_to(d, (t, HEAD_W)).T[0:_SUBLANES, :]

        kv, vv = k_ref[...], v_ref[...]
        dk_s[...] = jnp.zeros_like(dk_s)
        dv_s[...] = jnp.zeros_like(dv_s)

        def step(i, diag):
            base = pl.multiple_of(i * t, t)
            qv, dov = q_ref[pl.ds(base, t), :], do_ref[pl.ds(base, t), :]
            s_s[...] = _dot(kv, qv, 'nt')
            dp_s[...] = _dot(vv, dov, 'nt')
            for rs, cs, c, cut in _pieces(t, rb, diag):
                if cut is None:
                    p_s[rs, cs] = jnp.zeros((rb, _LANES), p_s.dtype)
                    ds_s[rs, cs] = jnp.zeros((rb, _LANES), ds_s.dtype)
                    continue
                lanes = pl.ds(pl.multiple_of(base + c * _LANES, _LANES), _LANES)
                lse = jnp.tile(lse_ref[:, lanes], (rb // _SUBLANES, 1))
                dl = jnp.tile(delta_s[:, lanes], (rb // _SUBLANES, 1))
                p = jnp.exp2(s_s[rs, cs] * _SCORE_C - lse)
                if cut:
                    p = jnp.where(_lane_ge(rb, cut), p, 0.0)
                p_s[rs, cs] = p.astype(p_s.dtype)
                ds_s[rs, cs] = (p * (dp_s[rs, cs] - dl)).astype(ds_s.dtype)
            dsv = ds_s[...]
            dv_s[...] += _dot(p_s[...], dov, 'nn')
            dk_s[...] += _dot(dsv, qv, 'nn')
            dqt_s[:, pl.ds(base, t)] += _dot(kv, dsv, 'tn')

        def loop_body(i, carry):
            step(i, False)
            return carry

        step(j, True)
        lax.fori_loop(j + 1, n, loop_body, 0)
        dk_ref[...] = dk_s[...] * scale
        dv_ref[...] = dv_s[...].astype(dv_ref.dtype)

        @pl.when(j == n - 1)
        def _():
            for b in range(n):
                rs = slice(b * t, (b + 1) * t)
                dq_ref[rs, :] = dqt_s[:, rs].T * scale

    jb = pl.BlockSpec((t, HEAD_W), lambda h, j: (j, h))
    fb = pl.BlockSpec((T, HEAD_W), lambda h, j: (0, h))
    sc = pltpu.VMEM((t, HEAD_W), F32)
    return pl.pallas_call(
        body, name="attn_bwd", grid=(H, n),
        in_specs=[jb, jb, fb, fb, fb, pl.BlockSpec((_SUBLANES, T), lambda h, j: (h, 0))], out_specs=[fb, jb, jb],
        out_shape=[jax.ShapeDtypeStruct((T, HW), F32), jax.ShapeDtypeStruct((T, HW), F32),
                   jax.ShapeDtypeStruct((T, HW), _MXU_DTYPE)],
        scratch_shapes=[pltpu.VMEM((t, t), F32), pltpu.VMEM((t, t), F32), pltpu.VMEM((t, t), _MXU_DTYPE),
                        pltpu.VMEM((t, t), _MXU_DTYPE), sc, sc, pltpu.VMEM((_SUBLANES, T), F32),
                        pltpu.VMEM((HEAD_W, T), F32)],
        compiler_params=_params(("parallel", "arbitrary"), 3 * _nbytes((T, HEAD_W), _MXU_DTYPE) + _nbytes((T, HEAD_W), F32)
                                + 4 * _nbytes((t, HEAD_W), F32), 4 * _nbytes((t, t), F32) + _nbytes((T, HEAD_W), F32)),
    )(k, v, q, o, do, lse2)


def _gate_fwd(pg, gb, yc, ym):
    T, D = yc.shape
    tm = _tile(T, _ROW_TILE, _PACKED_ROWS)

    def body(g0_ref, g1_ref, gb_ref, yc_ref, ym_ref, y_ref):
        s0 = _sig(g0_ref[...] + gb_ref[0:1, :])
        s1 = _sig(g1_ref[...] + gb_ref[1:2, :])
        y_ref[...] = (s0 * yc_ref[...] + s1 * ym_ref[...]).astype(y_ref.dtype)

    return pl.pallas_call(
        body, name="gate_fwd", grid=(T // tm,),
        in_specs=[_row_spec(tm, D, 0), _row_spec(tm, D, 1), _full_spec((2, D)), _row_spec(tm, D), _row_spec(tm, D)],
        out_specs=_row_spec(tm, D),
        out_shape=jax.ShapeDtypeStruct((T, D), _MXU_DTYPE),
        compiler_params=_params(("parallel",), 5 * _nbytes((tm, D), F32), 4 * _nbytes((tm, D), F32)),
    )(pg, pg, gb, yc, ym)


def _gate_bwd(pg, gb, yc, ym, dy):
    T, D = yc.shape
    tm = _tile(T, _ROW_TILE, _PACKED_ROWS)

    def body(g0_ref, g1_ref, gb_ref, yc_ref, ym_ref, dy_ref, dyc_ref, dym_ref, dpg_ref, dgb_ref):
        _init_acc([dgb_ref])
        dyv = dy_ref[...]
        s0 = _sig(g0_ref[...] + gb_ref[0:1, :])
        s1 = _sig(g1_ref[...] + gb_ref[1:2, :])
        dyc_ref[...] = (dyv * s0).astype(dyc_ref.dtype)
        dym_ref[...] = (dyv * s1).astype(dym_ref.dtype)
        d0 = dyv * yc_ref[...] * s0 * (1.0 - s0)
        d1 = dyv * ym_ref[...] * s1 * (1.0 - s1)
        dpg_ref[:, :D] = d0.astype(dpg_ref.dtype)
        dpg_ref[:, D:] = d1.astype(dpg_ref.dtype)
        dgb_ref[0:1, :] += _colsum(d0)
        dgb_ref[1:2, :] += _colsum(d1)

    return pl.pallas_call(
        body, name="gate_bwd", grid=(T // tm,),
        in_specs=[_row_spec(tm, D, 0), _row_spec(tm, D, 1), _full_spec((2, D)), _row_spec(tm, D), _row_spec(tm, D),
                  _row_spec(tm, D)],
        out_specs=[_row_spec(tm, D), _row_spec(tm, D), _row_spec(tm, 2 * D), _full_spec((2, D))],
        out_shape=[jax.ShapeDtypeStruct((T, D), _MXU_DTYPE)] * 2 + [jax.ShapeDtypeStruct((T, 2 * D), _MXU_DTYPE),
                                                                   jax.ShapeDtypeStruct((2, D), F32)],
        compiler_params=_params(("arbitrary",), 7 * _nbytes((tm, D), F32), 8 * _nbytes((tm, D), F32)),
    )(pg, pg, gb, yc, ym, dy)


def _place():
    return lax.axis_index("x"), lax.axis_index("y"), lax.axis_index("c")


def _all_gather(name, xs):
    R, C = xs.shape
    MESH = pl.DeviceIdType.MESH

    def body(x_ref, out_ref, send_sems, recv_sems, local_sem):
        x, y, c = _place()
        me, sibling = (x, y, c), (x, y, 1 - c)
        chips = [(1 - x, y), (x, 1 - y), (1 - x, 1 - y)]

        def rows(px, py, pc):
            return out_ref.at[4 * px + 2 * py + pc]

        def copy(k, block, to, src=None):
            return pltpu.make_async_remote_copy(
                src_ref=rows(*block) if src is None else src, dst_ref=rows(*block),
                send_sem=send_sems.at[k], recv_sem=recv_sems.at[k], device_id=to, device_id_type=MESH)

        mine = pltpu.make_async_copy(x_ref, rows(*me), local_sem)
        mine.start()
        first = [copy(0, me, sibling, src=x_ref)]
        first += [copy(1 + j, me, (*chip, c), src=x_ref) for j, chip in enumerate(chips)]
        for cp in first:
            cp.start()
        passed = [copy(4 + j, (*chip, c), sibling) for j, chip in enumerate(chips)]
        for j, chip in enumerate(chips):
            copy(1 + j, (*chip, c), me).wait_recv()
            passed[j].start()
        copy(0, sibling, me).wait_recv()
        for j, chip in enumerate(chips):
            copy(4 + j, (*chip, 1 - c), me).wait_recv()
        for cp in first + passed:
            cp.wait_send()
        mine.wait()

    return pl.pallas_call(
        body, name=name,
        out_shape=jax.ShapeDtypeStruct((N_DEV, R, C), xs.dtype),
        in_specs=[pl.BlockSpec(memory_space=pl.ANY)], out_specs=pl.BlockSpec(memory_space=pl.ANY),
        scratch_shapes=[pltpu.SemaphoreType.DMA((7,)), pltpu.SemaphoreType.DMA((7,)), pltpu.SemaphoreType.DMA],
    )(xs)


def _pair_exchange(name, xs):
    _, G, R, C = xs.shape

    def body(x_ref, out_ref, send_sem, recv_sem):
        x, y, c = _place()
        cp = pltpu.make_async_remote_copy(
            src_ref=x_ref.at[1 - c], dst_ref=out_ref, send_sem=send_sem, recv_sem=recv_sem,
            device_id=(x, y, 1 - c), device_id_type=pl.DeviceIdType.MESH)
        cp.start()
        cp.wait()

    return pl.pallas_call(
        body, name=name,
        out_shape=jax.ShapeDtypeStruct((G, R, C), xs.dtype),
        in_specs=[pl.BlockSpec(memory_space=pl.ANY)], out_specs=pl.BlockSpec(memory_space=pl.ANY),
        scratch_shapes=[pltpu.SemaphoreType.DMA, pltpu.SemaphoreType.DMA],
    )(xs)


def _pair_sum(name, xs, got):
    _, G, R, C = xs.shape
    tr = _tile(R, _PACK_ROW_MULT, _PACKED_ROWS)

    def body(c_ref, a_ref, b_ref, o_ref):
        o_ref[...] = (a_ref[...].astype(F32) + b_ref[...].astype(F32)).astype(o_ref.dtype)

    blk = pl.BlockSpec((None, tr, C), lambda g, i, c_ref: (g, i, 0))
    return pl.pallas_call(
        body, name=name,
        grid_spec=pltpu.PrefetchScalarGridSpec(
            num_scalar_prefetch=1, grid=(G, R // tr),
            in_specs=[pl.BlockSpec((None, None, tr, C), lambda g, i, c_ref: (c_ref[0], g, i, 0)), blk],
            out_specs=blk),
        out_shape=jax.ShapeDtypeStruct((G, R, C), xs.dtype),
        compiler_params=_params(("parallel", "parallel"), 3 * _nbytes((tr, C), F32), 2 * _nbytes((tr, C), F32)),
    )(lax.axis_index("c").reshape(1).astype(jnp.int32), xs, got)


def _chip_all_to_all(name, xs):
    G, R, C = xs.shape

    def body(x_ref, out_ref, send_sems, recv_sems, local_sem):
        x, y, c = _place()
        me = 2 * x + y
        mine = pltpu.make_async_copy(x_ref.at[me], out_ref.at[me], local_sem)
        mine.start()
        copies = []
        for k in range(1, G):
            px = 1 - x if k & 2 else x
            py = 1 - y if k & 1 else y
            copies.append(pltpu.make_async_remote_copy(
                src_ref=x_ref.at[2 * px + py], dst_ref=out_ref.at[me],
                send_sem=send_sems.at[k - 1], recv_sem=recv_sems.at[k - 1],
                device_id=(px, py, c), device_id_type=pl.DeviceIdType.MESH))
        for cp in copies:
            cp.start()
        for cp in copies:
            cp.wait()
        mine.wait()

    return pl.pallas_call(
        body, name=name,
        out_shape=jax.ShapeDtypeStruct((G, R, C), xs.dtype),
        in_specs=[pl.BlockSpec(memory_space=pl.ANY)], out_specs=pl.BlockSpec(memory_space=pl.ANY),
        scratch_shapes=[pltpu.SemaphoreType.DMA((G - 1,)), pltpu.SemaphoreType.DMA((G - 1,)), pltpu.SemaphoreType.DMA],
    )(xs)


_HBM_SPEC = pl.BlockSpec(memory_space=pltpu.HBM)
_SEM_SPEC = pl.BlockSpec(memory_space=pltpu.SEMAPHORE)
_DATAFLOW = pltpu.SideEffectType.DATAFLOW_SIDE_EFFECTING


def _peers():
    x, y, c = _place()
    out = []
    for k in range(1, N_DEV):
        px = 1 - x if k & 4 else x
        py = 1 - y if k & 2 else y
        pc = 1 - c if k & 1 else c
        out.append((k - 1, (px, py, pc), 4 * px + 2 * py + pc))
    return 4 * x + 2 * y + c, out


def _exchange_copies(x_ref, land_ref, send_sems, recv_sems, scatter):
    me, peers = _peers()
    return [pltpu.make_async_remote_copy(
        src_ref=x_ref.at[idx] if scatter else x_ref, dst_ref=land_ref.at[me],
        send_sem=send_sems.at[k], recv_sem=recv_sems.at[k], device_id=dev, device_id_type=pl.DeviceIdType.MESH)
        for k, dev, idx in peers]


def _exchange_start(name, xs, land, scatter):
    def body(x_ref, land_ref, send_sems, recv_sems, x_thru, land_thru, token):
        for cp in _exchange_copies(x_ref, land_ref, send_sems, recv_sems, scatter):
            cp.start()
        token[...] = jnp.zeros_like(token)

    sems = pltpu.SemaphoreType.DMA((N_DEV - 1,))
    return pl.pallas_call(
        body, name=name,
        out_shape=(sems, sems, pltpu.HBM(xs.shape, xs.dtype), pltpu.HBM(land.shape, land.dtype),
                   jax.ShapeDtypeStruct((_SUBLANES, _LANES), F32)),
        in_specs=(_HBM_SPEC, _HBM_SPEC),
        out_specs=(_SEM_SPEC, _SEM_SPEC, _HBM_SPEC, _HBM_SPEC, pl.BlockSpec(memory_space=pltpu.VMEM)),
        input_output_aliases={0: 2, 1: 3},
        compiler_params=pltpu.CompilerParams(has_side_effects=_DATAFLOW),
    )(pltpu.with_memory_space_constraint(xs, pltpu.HBM), pltpu.with_memory_space_constraint(land, pltpu.HBM))


def _exchange_wait(name, send_sems, recv_sems, xs, land, after, scatter):
    def body(x_ref, land_ref, send_sems, recv_sems, after_ref, x_dead, got_ref):
        for cp in _exchange_copies(x_ref, land_ref, send_sems, recv_sems, scatter):
            cp.wait_send()
            cp.wait_recv()

    return pl.pallas_call(
        body, name=name,
        out_shape=(pltpu.HBM(xs.shape, xs.dtype), pltpu.HBM(land.shape, land.dtype)),
        in_specs=(_HBM_SPEC, _HBM_SPEC, _SEM_SPEC, _SEM_SPEC, pl.BlockSpec(memory_space=pl.ANY)),
        out_specs=(_HBM_SPEC, _HBM_SPEC), input_output_aliases={0: 0, 1: 1},
        compiler_params=pltpu.CompilerParams(has_side_effects=_DATAFLOW),
    )(xs, land, send_sems, recv_sems, after)[1]


def _own_slot(block, me):
    land = lax.empty((N_DEV,) + block.shape, block.dtype)
    return lax.dynamic_update_slice(land, block[None], (me,) + (0,) * block.ndim)


def _sum_blocks(name, parts, scale=None):
    n, R, C = parts.shape
    tr = _tile(R, _PACK_ROW_MULT, _PACKED_ROWS)

    def body(p_ref, o_ref):
        acc = p_ref[0].astype(F32)
        for s in range(1, n):
            acc = acc + p_ref[s].astype(F32)
        if scale is not None:
            acc = acc * scale
        o_ref[...] = acc

    return pl.pallas_call(
        body, name=name, grid=(R // tr,),
        in_specs=[pl.BlockSpec((n, tr, C), lambda i: (0, i, 0))], out_specs=pl.BlockSpec((tr, C), lambda i: (i, 0)),
        out_shape=jax.ShapeDtypeStruct((R, C), F32),
        compiler_params=_params(("parallel",), _nbytes((n, tr, C), parts.dtype) + _nbytes((tr, C), F32), 2 * _nbytes((tr, C), F32)),
    )(parts)


def _adamw(w, g, m, v):
    shape = w.shape
    w2, g2, m2, v2 = (t.reshape(-1, shape[-1]) for t in (w, g, m, v))
    R, C = w2.shape
    tr = _tile(R, _ROW_TILE, _SUBLANES)

    def body(w_ref, g_ref, m_ref, v_ref, d_ref, nm_ref, nv_ref):
        gv = g_ref[...]
        mv = ADAM_B1 * m_ref[...] + (1.0 - ADAM_B1) * gv
        vv = ADAM_B2 * v_ref[...] + (1.0 - ADAM_B2) * (gv * gv)
        m_hat = mv / (1.0 - ADAM_B1 ** ADAM_STEP)
        v_hat = vv / (1.0 - ADAM_B2 ** ADAM_STEP)
        d_ref[...] = -ADAM_LR * (m_hat / (jnp.sqrt(v_hat) + ADAM_EPS) + ADAM_WD * w_ref[...])
        nm_ref[...] = mv
        nv_ref[...] = vv

    spec = pl.BlockSpec((tr, C), lambda i: (i, 0))
    outs = pl.pallas_call(
        body, name="adamw", grid=(R // tr,),
        in_specs=[spec] * 4, out_specs=[spec] * 3,
        out_shape=[jax.ShapeDtypeStruct((R, C), F32)] * 3,
        compiler_params=_params(("parallel",), 7 * _nbytes((tr, C), F32), 4 * _nbytes((tr, C), F32)),
    )(w2, g2, m2, v2)
    return tuple(o.reshape(shape) for o in outs)


def _pack_rows(parts, lead=()):
    out, spans, r0 = [], [], 0
    unit = _PACKED_ROWS * _PACK_W
    for p in parts:
        flat = p.reshape(lead + (-1,))
        size = flat.shape[-1]
        pad = (-size) % unit
        if pad:
            flat = jnp.pad(flat, [(0, 0)] * len(lead) + [(0, pad)])
        rows = (size + pad) // _PACK_W
        out.append(flat.reshape(lead + (rows, _PACK_W)))
        spans.append((r0, rows, p.shape[len(lead):]))
        r0 += rows
    pad = (-r0) % _PACK_ROW_MULT
    if pad:
        out.append(jnp.zeros(lead + (pad, _PACK_W), out[0].dtype))
    return jnp.concatenate(out, axis=len(lead)), spans


def _unpack_rows(packed, span, lead=()):
    r0, rows, shape = span
    size = 1
    for s in shape:
        size *= s
    sl = lax.slice_in_dim(packed, r0, r0 + rows, axis=len(lead))
    return sl.reshape(lead + (-1,))[..., :size].reshape(lead + tuple(shape))


def _pack_flat(parts):
    flat, spans, at = [], [], 0
    for p in parts:
        size = p.size
        padded = size + (-size) % _LANES
        flat.append(jnp.pad(p.reshape(-1), (0, padded - size)))
        spans.append((at, size, p.shape))
        at += padded
    tail = (-at) % (_PACKED_ROWS * _PACK_W)
    if tail:
        flat.append(jnp.zeros((tail,), flat[0].dtype))
    return jnp.concatenate(flat).reshape(-1, _PACK_W), spans


def _unpack_flat(packed, span, lead=()):
    at, size, shape = span
    return lax.slice_in_dim(packed.reshape(lead + (-1,)), at, at + size, axis=len(lead)).reshape(lead + tuple(shape))


def _rope_tables(positions, T):
    inv_freq = ROPE_THETA ** (-jnp.arange(0, ROPE_DIM, 2, dtype=F32) / ROPE_DIM)
    ang = positions.reshape(T, 1).astype(F32) * inv_freq
    cos, sin = jnp.cos(ang), jnp.sin(ang)
    half = ROPE_DIM // 2
    z = lambda n: jnp.zeros((T, n), F32)
    tail = HEAD_W - QK_DIM
    ct = jnp.concatenate([jnp.ones((T, NOPE_DIM), F32), cos, cos, z(tail)], axis=1)
    sa = jnp.concatenate([z(NOPE_DIM), -sin, z(half), z(tail)], axis=1)
    sb = jnp.concatenate([z(NOPE_DIM), z(half), sin, z(tail)], axis=1)
    return ct, sa, sb


def _pad_heads(wt, per_head, keep_from=0, keep=None):
    K = wt.shape[1]
    keep = per_head if keep is None else keep
    w3 = wt.reshape(N_HEADS, per_head, K)[:, keep_from:keep_from + keep]
    return jnp.pad(w3, ((0, 0), (0, HEAD_W - keep), (0, 0))).reshape(N_HEADS * HEAD_W, K)


def _unpad_heads(g, keep):
    return g.reshape(N_HEADS, HEAD_W, g.shape[1])[:, :keep]


def _layer_weights(W):
    D = W['w_out'].shape[1]
    C = W['w_conv_out'].shape[1]
    QL, KVL = W['w_uq'].shape[1], W['w_ukv'].shape[1]
    o1 = 2 * C
    o2, o3 = o1 + QL, o1 + QL + KVL
    o4 = o3 + ROPE_DIM
    win = W['w_in']
    zr = lambda n: jnp.zeros((n, D), win.dtype)
    L = dict(W)
    L['w_c'] = win[:o1]
    L['w_l'] = jnp.concatenate([win[o1:o3], zr(NOPE_DIM), win[o3:o4], zr(HEAD_W - QK_DIM)], axis=0)
    L['w_g'] = win[o4:]
    L['w_q'] = _pad_heads(W['w_uq'], QK_DIM)
    L['w_k'] = _pad_heads(W['w_ukv'], NOPE_DIM + V_DIM, 0, NOPE_DIM)
    L['w_v'] = _pad_heads(W['w_ukv'], NOPE_DIM + V_DIM, NOPE_DIM, V_DIM)
    wmo = W['w_mla_out'].reshape(D, N_HEADS, V_DIM)
    L['w_mo'] = jnp.pad(wmo, ((0, 0), (0, 0), (0, HEAD_W - V_DIM))).reshape(D, N_HEADS * HEAD_W)
    L['dims'] = (D, C, QL, KVL)
    return L


def _row(v, width=None):
    v = v.reshape(1, -1)
    if width is not None and v.shape[1] < width:
        v = jnp.pad(v, ((0, 0), (0, width - v.shape[1])))
    return v


def _ffn_fwd(x, g, wg, wu, wd):
    h = _rms_fwd(x, _row(g))
    a, b, u = _ffn_up(h, wg, wu)
    x_out = _mm("ffn_down", u, wd, 'nn', res=x, scale=0.5)
    return x_out, (x, h, a, b, u)


def _ffn_bwd(dx, saved, g, wg, wu, wd):
    x, h, a, b, u = saved
    d_wd = _mm("ffn_dwd", u, dx, 'tn', scale=0.5)
    da, db = _ffn_bwd_up(dx, wd, a, b)
    dh = _mm("ffn_dh_g", da, wg, 'nn')
    dh = _mm("ffn_dh_u", db, wu, 'nn', res=dh)
    d_wg = _mm("ffn_dwg", da, h, 'tn')
    d_wu = _mm("ffn_dwg", db, h, 'tn')
    dx_in, dg = _rms_bwd(x, _row(g), dh, dx)
    return dx_in, dg, d_wg, d_wu, d_wd


def _layer_fwd(x, L, S, tabs):
    D, C, QL, KVL = L['dims']
    rope_cb = (QL + KVL) // HEAD_W
    x1, ffn1 = _ffn_fwd(x, S['ffn1_norm'], L['ffn1_w_gate'], L['ffn1_w_up'], L['ffn1_w_down'])
    h = _rms_fwd(x1, _row(S['mix_norm']))
    pc = _mm("proj_c", h, L['w_c'], 'nt')
    pl_ = _mm("proj_l", h, L['w_l'], 'nt')
    pg = _mm("proj_g", h, L['w_g'], 'nt')
    conv_w = jnp.pad(S['conv_w'], ((0, HALO - CONV_WIDTH), (0, 0)))
    z, c = _conv_fwd(pc, conv_w, _row(S['conv_b']), _row(S['conv_ln_g']), _row(S['conv_ln_b']))
    yc = _mm("conv_out", c, L['w_conv_out'], 'nt')
    cqn, ckvn = _lat_norm_fwd(pl_, _row(S['cq_norm']), _row(S['ckv_norm']), QL, KVL)
    q_raw = _mm("up_q", cqn, L['w_q'], 'nt')
    k_raw = _mm("up_kv", ckvn, L['w_k'], 'nt')
    v = _mm("up_kv", ckvn, L['w_v'], 'nt', out_dtype=_MXU_DTYPE)
    gq, gk = _row(S['q_norm'], HEAD_W), _row(S['k_norm'], HEAD_W)
    qn, kn = _qk_prep_fwd(q_raw, k_raw, pl_, rope_cb, *tabs, gq, gk)
    o, lse = _attn_fwd(qn, kn, v)
    ym = _mm("mla_out", o, L['w_mo'], 'nt')
    y = _gate_fwd(pg, S['gate_bias'], yc, ym)
    x2 = _mm("mix_out", y, L['w_out'], 'nn', res=x1)
    x3, ffn2 = _ffn_fwd(x2, S['ffn2_norm'], L['ffn2_w_gate'], L['ffn2_w_up'], L['ffn2_w_down'])
    saved = dict(ffn1=ffn1, ffn2=ffn2, x1=x1, h=h, pc=pc, pl=pl_, pg=pg, z=z, c=c, yc=yc, cqn=cqn, ckvn=ckvn,
                 q_raw=q_raw, k_raw=k_raw, v=v, qn=qn, kn=kn, o=o, lse=lse, ym=ym, y=y, conv_w=conv_w, gq=gq, gk=gk)
    return x3, saved


def _layer_bwd(dx, L, S, tabs, A):
    D, C, QL, KVL = L['dims']
    rope_cb = (QL + KVL) // HEAD_W
    G, g = {}, {}
    dx, g['ffn2_norm'], G['ffn2_w_gate'], G['ffn2_w_up'], G['ffn2_w_down'] = _ffn_bwd(
        dx, A['ffn2'], S['ffn2_norm'], L['ffn2_w_gate'], L['ffn2_w_up'], L['ffn2_w_down'])
    G['w_out'] = _mm("mix_dwout", A['y'], dx, 'tn')
    dy = _mm("mix_dy", dx, L['w_out'], 'nt')
    dyc, dym, dpg, g['gate_bias'] = _gate_bwd(A['pg'], S['gate_bias'], A['yc'], A['ym'], dy)
    G['w_conv_out'] = _mm("conv_dwout", dyc, A['c'], 'tn')
    dc = _mm("conv_dc", dyc, L['w_conv_out'], 'nn')
    dpc, dcw, g['conv_b'], g['conv_ln_g'], g['conv_ln_b'] = _conv_bwd(
        A['pc'], A['z'], dc, A['conv_w'], _row(S['conv_ln_g']), _row(S['conv_ln_b']))
    g['conv_w'] = dcw[:CONV_WIDTH]
    d_wmo = _mm("mla_dwout", dym, A['o'], 'tn')
    G['w_mla_out'] = d_wmo.reshape(D, N_HEADS, HEAD_W)[:, :, :V_DIM].reshape(D, N_HEADS * V_DIM)
    do = _mm("mla_do", dym, L['w_mo'], 'nn', out_dtype=_MXU_DTYPE)
    dqn, dkn, dv = _attn_bwd(A['qn'], A['kn'], A['v'], A['o'], do, A['lse'])
    dq_raw, dk_raw, drope, dgq, dgk = _qk_prep_bwd(A['q_raw'], A['k_raw'], A['pl'], rope_cb, *tabs, A['gq'], A['gk'], dqn, dkn)
    g['q_norm'], g['k_norm'] = dgq[:, :QK_DIM], dgk[:, :QK_DIM]
    G['w_uq'] = _unpad_heads(_mm("up_dwq", dq_raw, A['cqn'], 'tn'), QK_DIM).reshape(N_HEADS * QK_DIM, QL)
    d_wk = _unpad_heads(_mm("up_dwkv", dk_raw, A['ckvn'], 'tn'), NOPE_DIM)
    d_wv = _unpad_heads(_mm("up_dwkv", dv, A['ckvn'], 'tn'), V_DIM)
    G['w_ukv'] = jnp.concatenate([d_wk, d_wv], axis=1).reshape(N_HEADS * (NOPE_DIM + V_DIM), KVL)
    dcq = _mm("up_dcq", dq_raw, L['w_q'], 'nn')
    dckv = _mm("up_dckv_k", dk_raw, L['w_k'], 'nn')
    dckv = _mm("up_dckv_v", dv, L['w_v'], 'nn', res=dckv)
    dpl, g['cq_norm'], g['ckv_norm'] = _lat_norm_bwd(A['pl'], _row(S['cq_norm']), _row(S['ckv_norm']), dcq, dckv, drope, QL, KVL)
    d_wc = _mm("proj_dwc", dpc, A['h'], 'tn')
    d_wl = _mm("proj_dwl", dpl, A['h'], 'tn')
    d_wg = _mm("proj_dwg", dpg, A['h'], 'tn')
    ql = QL + KVL
    G['w_in'] = jnp.concatenate([d_wc, d_wl[:ql], d_wl[ql + NOPE_DIM:ql + NOPE_DIM + ROPE_DIM], d_wg], axis=0)
    dh = _mm("proj_dh_c", dpc, L['w_c'], 'nn')
    dh = _mm("proj_dh_l", dpl, L['w_l'], 'nn', res=dh)
    dh = _mm("proj_dh_g", dpg, L['w_g'], 'nn', res=dh)
    dx, g['mix_norm'] = _rms_bwd(A['x1'], _row(S['mix_norm']), dh, dx)
    dx, g['ffn1_norm'], G['ffn1_w_gate'], G['ffn1_w_up'], G['ffn1_w_down'] = _ffn_bwd(
        dx, A['ffn1'], S['ffn1_norm'], L['ffn1_w_gate'], L['ffn1_w_up'], L['ffn1_w_down'])
    return dx, G, g


def kernel(x, positions, ffn1_norm, ffn1_w_gate, ffn1_w_up, ffn1_w_down, mix_norm, w_in, gate_bias, conv_w, conv_b, conv_ln_g, conv_ln_b, w_conv_out, cq_norm, ckv_norm, w_uq, w_ukv, q_norm, k_norm, w_mla_out, w_out, ffn2_norm, ffn2_w_gate, ffn2_w_up, ffn2_w_down, loss_target, m_ffn1_norm, m_ffn1_w_gate, m_ffn1_w_up, m_ffn1_w_down, m_mix_norm, m_w_in, m_gate_bias, m_conv_w, m_conv_b, m_conv_ln_g, m_conv_ln_b, m_w_conv_out, m_cq_norm, m_ckv_norm, m_w_uq, m_w_ukv, m_q_norm, m_k_norm, m_w_mla_out, m_w_out, m_ffn2_norm, m_ffn2_w_gate, m_ffn2_w_up, m_ffn2_w_down, v_ffn1_norm, v_ffn1_w_gate, v_ffn1_w_up, v_ffn1_w_down, v_mix_norm, v_w_in, v_gate_bias, v_conv_w, v_conv_b, v_conv_ln_g, v_conv_ln_b, v_w_conv_out, v_cq_norm, v_ckv_norm, v_w_uq, v_w_ukv, v_q_norm, v_k_norm, v_w_mla_out, v_w_out, v_ffn2_norm, v_ffn2_w_gate, v_ffn2_w_up, v_ffn2_w_down):
    w = dict(zip(WEIGHTS, (ffn1_norm, ffn1_w_gate, ffn1_w_up, ffn1_w_down, mix_norm, w_in, gate_bias, conv_w, conv_b, conv_ln_g, conv_ln_b, w_conv_out, cq_norm, ckv_norm, w_uq, w_ukv, q_norm, k_norm, w_mla_out, w_out, ffn2_norm, ffn2_w_gate, ffn2_w_up, ffn2_w_down)))
    m = dict(zip(WEIGHTS, (m_ffn1_norm, m_ffn1_w_gate, m_ffn1_w_up, m_ffn1_w_down, m_mix_norm, m_w_in, m_gate_bias, m_conv_w, m_conv_b, m_conv_ln_g, m_conv_ln_b, m_w_conv_out, m_cq_norm, m_ckv_norm, m_w_uq, m_w_ukv, m_q_norm, m_k_norm, m_w_mla_out, m_w_out, m_ffn2_norm, m_ffn2_w_gate, m_ffn2_w_up, m_ffn2_w_down)))
    v = dict(zip(WEIGHTS, (v_ffn1_norm, v_ffn1_w_gate, v_ffn1_w_up, v_ffn1_w_down, v_mix_norm, v_w_in, v_gate_bias, v_conv_w, v_conv_b, v_conv_ln_g, v_conv_ln_b, v_w_conv_out, v_cq_norm, v_ckv_norm, v_w_uq, v_w_ukv, v_q_norm, v_k_norm, v_w_mla_out, v_w_out, v_ffn2_norm, v_ffn2_w_gate, v_ffn2_w_up, v_ffn2_w_down)))
    depth = ffn1_norm.shape[0]
    T, D = x.shape[1], x.shape[2]
    xs = x.reshape(T, D)
    target = loss_target.reshape(T, D)
    tabs = _rope_tables(positions, T)
    me = 4 * lax.axis_index("x") + 2 * lax.axis_index("y") + lax.axis_index("c")

    big = list(BIG)
    packed = []
    for l in range(depth):
        p, spans = _pack_rows([(w[n][l].T if BIG[n] else w[n][l]).astype(_WIRE_DTYPE) for n in big])
        packed.append(p)
    gathered = {0: _all_gather("gather_weights", packed[0])}
    in_flight = {l: _exchange_start(f"gather_start_{l}", packed[l], _own_slot(packed[l], me), False)
                 for l in range(1, depth)}
    small_names = [(l, n) for l in range(depth) for n in SMALL_SHARDED]
    small_packed, small_spans = _pack_flat([w[n][l].T for l, n in small_names])
    small_gathered = _all_gather("gather_small", small_packed)
    for st in in_flight.values():
        xs, _ = lax.optimization_barrier((xs, st[4]))

    def layer_weights(l):
        W = {}
        for n, span in zip(big, spans):
            sh = _unpack_rows(gathered[l], span, (N_DEV,))
            W[n] = sh.reshape(N_DEV * sh.shape[1], sh.shape[2])
        S = {n: w[n][l] for n in REPLICATED}
        for (ll, n), span in zip(small_names, small_spans):
            if ll == l:
                sh = _unpack_flat(small_gathered, span, (N_DEV,))
                S[n] = sh.reshape(N_DEV * sh.shape[1], sh.shape[2]).T
        return _layer_weights(W), S

    layers, acts = [], []
    hcur = xs
    for l in range(depth):
        if l > 0:
            send_sems, recv_sems, src, land, _ = in_flight[l]
            gathered[l] = _exchange_wait(f"gather_wait_{l}", send_sems, recv_sems, src, land, hcur, False)
        layers.append(layer_weights(l))
        hcur, saved = _layer_fwd(hcur, *layers[l], tabs)
        acts.append(saved)
    dx, loss_part = _loss_grad(hcur, target)

    small_grads, reduced, in_flight = [None] * depth, [None] * depth, {}
    for l in reversed(range(depth)):
        dx, G, small_grads[l] = _layer_bwd(dx, *layers[l], tabs, acts[l])
        blocks = [G[n].reshape(N_DEV // 2, 2, -1, G[n].shape[1]).astype(_WIRE_DTYPE) for n in big]
        if l > 0:
            send, gspans = _pack_rows([b.reshape((N_DEV,) + b.shape[2:]) for b in blocks], (N_DEV,))
            own = lax.dynamic_index_in_dim(send, me, 0, keepdims=False)
            in_flight[l] = _exchange_start(f"scatter_start_{l}", send, _own_slot(own, me), True)
            dx, _ = lax.optimization_barrier((dx, in_flight[l][4]))
        else:
            send, gspans = _pack_rows([jnp.swapaxes(b, 0, 1) for b in blocks], (2, N_DEV // 2))
            pair = _pair_sum("pair_sum_grads", send, _pair_exchange("pair_exchange_grads", send))
            reduced[l] = _sum_blocks("sum_grads", _chip_all_to_all("scatter_grads", pair))
    for l, (send_sems, recv_sems, src, land, _) in in_flight.items():
        reduced[l] = _sum_blocks("sum_grads_direct", _exchange_wait(f"scatter_wait_{l}", send_sems, recv_sems, src, land, dx, True))
    small_list = [(l, n) for l in range(depth) for n in REPLICATED + SMALL_SHARDED]
    small_send, sspans = _pack_flat([small_grads[l][n] for l, n in small_list] + [loss_part])
    small_all = _sum_blocks("sum_small", _all_gather("gather_small_grads", small_send))

    loss = _unpack_flat(small_all, sspans[-1])[0, 0]
    grads = {}
    for n in WEIGHTS:
        per_layer = []
        for l in range(depth):
            if n in BIG:
                gt = _unpack_rows(reduced[l], gspans[big.index(n)])
                per_layer.append(gt.T if BIG[n] else gt)
            else:
                gs = _unpack_flat(small_all, sspans[small_list.index((l, n))])
                if n in SMALL_SHARDED:
                    cols = w[n].shape[-1]
                    gs = lax.dynamic_slice_in_dim(gs, me * cols, cols, axis=1)
                per_layer.append(gs.reshape(w[n].shape[1:]))
        grads[n] = jnp.stack(per_layer, axis=0)

    delta, new_m, new_v = {}, {}, {}
    for n in WEIGHTS:
        delta[n], new_m[n], new_v[n] = _adamw(w[n], grads[n], m[n], v[n])
    return (loss, dx.reshape(1, T, D), *[grads[n] for n in WEIGHTS], *[delta[n] for n in WEIGHTS],
            *[new_m[n] for n in WEIGHTS], *[new_v[n] for n in WEIGHTS])
```

```python
import jax
import jax.numpy as jnp
from jax import lax
from jax.experimental import pallas as pl
from jax.experimental.pallas import tpu as pltpu

F32 = jnp.float32
_MXU_DTYPE = jnp.bfloat16
_WIRE_DTYPE = jnp.bfloat16

_LANES = 128
_SUBLANES = 8
_PACKED_ROWS = 16
_V7X_VMEM_BYTES = 64 * 2 ** 20
_VMEM_HEADROOM = 8 * 2 ** 20

N_DEV = 8
EPS = 1e-6
CHUNK = 64
CONV_WIDTH = 31
N_HEADS = 8
NOPE_DIM = 64
ROPE_DIM = 32
QK_DIM = NOPE_DIM + ROPE_DIM
V_DIM = 64
HEAD_W = _LANES
ROPE_THETA = 10000.0
HALO = 32

ADAM_LR = 0.001
ADAM_B1 = 0.9
ADAM_B2 = 0.999
ADAM_EPS = 1e-08
ADAM_WD = 0.01
ADAM_STEP = 10

_ROW_TILE = 512
_MM_TILE = 512
_MM_TILE_N = 1408
_MM_TILE_K = 1024
_TN_TILE_M = 1408
_ATT_TILE = 512
_ATT_ROWS = 32
_PACK_W = 1024
_PACK_ROW_MULT = 128

WEIGHTS = ['ffn1_norm', 'ffn1_w_gate', 'ffn1_w_up', 'ffn1_w_down', 'mix_norm', 'w_in', 'gate_bias',
           'conv_w', 'conv_b', 'conv_ln_g', 'conv_ln_b', 'w_conv_out', 'cq_norm', 'ckv_norm', 'w_uq',
           'w_ukv', 'q_norm', 'k_norm', 'w_mla_out', 'w_out', 'ffn2_norm', 'ffn2_w_gate', 'ffn2_w_up',
           'ffn2_w_down']
BIG = {'ffn1_w_gate': True, 'ffn1_w_up': True, 'ffn1_w_down': False, 'w_in': True, 'w_conv_out': True,
       'w_uq': True, 'w_ukv': True, 'w_mla_out': True, 'w_out': False, 'ffn2_w_gate': True,
       'ffn2_w_up': True, 'ffn2_w_down': False}
SMALL_SHARDED = ['conv_w', 'gate_bias']
REPLICATED = ['ffn1_norm', 'mix_norm', 'conv_b', 'conv_ln_g', 'conv_ln_b', 'cq_norm', 'ckv_norm',
              'q_norm', 'k_norm', 'ffn2_norm']


def _tile(n, target, mult=_LANES):
    if n <= target:
        return n
    for d in range(target - target % mult, 0, -mult):
        if n % d == 0:
            return d
    return n


def _nbytes(shape, dtype):
    n = 1
    for s in shape:
        n *= s
    return n * jnp.dtype(dtype).itemsize


def _params(dims, block_bytes, extra_bytes=0):
    need = 2 * block_bytes + extra_bytes + _VMEM_HEADROOM
    assert need <= _V7X_VMEM_BYTES, (need, dims)
    return pltpu.CompilerParams(dimension_semantics=dims, vmem_limit_bytes=_V7X_VMEM_BYTES - _VMEM_HEADROOM)


def _mxu(x, scale=None):
    if scale is not None:
        x = x * scale
    return x if x.dtype == _MXU_DTYPE else x.astype(_MXU_DTYPE)


def _dot(a, b, mode):
    dims = {'nn': (((1,), (0,)), ((), ())), 'nt': (((1,), (1,)), ((), ())), 'tn': (((0,), (0,)), ((), ()))}[mode]
    return lax.dot_general(a, b, dims, preferred_element_type=F32)


def _colsum(x):
    return jnp.sum(x, axis=0, keepdims=True)


def _sig(x):
    return jax.nn.sigmoid(x)


def _mm(name, a, b, mode, *, out_dtype=F32, res=None, scale=None, a_scale=None, tm=None, tn=None, tk=None):
    if mode == 'nn':
        (M, K), (_, N) = a.shape, b.shape
    elif mode == 'nt':
        (M, K), (N, _) = a.shape, b.shape
    else:
        (K, M), (_, N) = a.shape, b.shape
    tm = _tile(M, tm or (_TN_TILE_M if mode == 'tn' else _MM_TILE))
    tn = _tile(N, tn or _MM_TILE_N)
    tk = _tile(K, tk or (_MM_TILE_K if mode == 'tn' else 4096))
    nk = K // tk
    grid = (M // tm, N // tn, nk)
    a_spec = pl.BlockSpec((tk, tm), lambda i, j, k: (k, i)) if mode == 'tn' else pl.BlockSpec((tm, tk), lambda i, j, k: (i, k))
    b_spec = pl.BlockSpec((tn, tk), lambda i, j, k: (j, k)) if mode == 'nt' else pl.BlockSpec((tk, tn), lambda i, j, k: (k, j))
    o_spec = pl.BlockSpec((tm, tn), lambda i, j, k: (i, j))
    has_res = res is not None

    def body(*refs):
        a_ref, b_ref = refs[0], refs[1]
        res_ref = refs[2] if has_res else None
        o_ref = refs[3] if has_res else refs[2]
        acc_ref = refs[-1] if nk > 1 else None

        def finish(p):
            if scale is not None:
                p = p * scale
            if has_res:
                p = res_ref[...] + p
            o_ref[...] = p.astype(o_ref.dtype)

        p = _dot(_mxu(a_ref[...], a_scale), _mxu(b_ref[...]), mode)
        if nk == 1:
            finish(p)
        else:
            k = pl.program_id(2)

            @pl.when(k == 0)
            def _():
                acc_ref[...] = p

            @pl.when(k > 0)
            def _():
                acc_ref[...] += p

            @pl.when(k == nk - 1)
            def _():
                finish(acc_ref[...])

    blocks = (_nbytes((tm, tk), a.dtype) + _nbytes((tk, tn), b.dtype) + _nbytes((tm, tn), out_dtype)
              + (_nbytes((tm, tn), F32) if has_res else 0))
    extra = 3 * _nbytes((tm, tn), F32) + _nbytes((tm, tk), _MXU_DTYPE) + _nbytes((tk, tn), _MXU_DTYPE)
    return pl.pallas_call(
        body, name=name, grid=grid,
        in_specs=[a_spec, b_spec] + ([o_spec] if has_res else []),
        out_specs=o_spec,
        out_shape=jax.ShapeDtypeStruct((M, N), out_dtype),
        scratch_shapes=[pltpu.VMEM((tm, tn), F32)] if nk > 1 else [],
        compiler_params=_params(("parallel", "parallel", "arbitrary"), blocks, extra),
    )(*([a, b] + ([res] if has_res else [])))


def _ffn_up(h, wg, wu):
    T, D = h.shape
    F = wg.shape[0]
    tm, tn = _tile(T, _MM_TILE), _tile(F, _MM_TILE_N)

    def body(h_ref, wg_ref, wu_ref, a_ref, b_ref, u_ref):
        hv = h_ref[...]
        a = _dot(hv, wg_ref[...], 'nt')
        b = _dot(hv, wu_ref[...], 'nt')
        a_ref[...] = a
        b_ref[...] = b
        u_ref[...] = (a * _sig(a) * b).astype(u_ref.dtype)

    w_spec = pl.BlockSpec((tn, D), lambda j, i: (j, 0))
    o_spec = pl.BlockSpec((tm, tn), lambda j, i: (i, j))
    blocks = _nbytes((tm, D), h.dtype) + 2 * _nbytes((tn, D), wg.dtype) + 2 * _nbytes((tm, tn), F32) + _nbytes((tm, tn), _MXU_DTYPE)
    return pl.pallas_call(
        body, name="ffn_up", grid=(F // tn, T // tm),
        in_specs=[pl.BlockSpec((tm, D), lambda j, i: (i, 0)), w_spec, w_spec],
        out_specs=[o_spec, o_spec, o_spec],
        out_shape=[jax.ShapeDtypeStruct((T, F), F32), jax.ShapeDtypeStruct((T, F), F32),
                   jax.ShapeDtypeStruct((T, F), _MXU_DTYPE)],
        compiler_params=_params(("parallel", "parallel"), blocks, 4 * _nbytes((tm, tn), F32)),
    )(h, wg, wu)


def _ffn_bwd_up(dx, wd, a, b, after):
    T, D = dx.shape
    F = wd.shape[0]
    tm, tn = _tile(T, _MM_TILE), _tile(F, _MM_TILE_N)

    def body(dx_ref, wd_ref, a_ref, b_ref, after_ref, da_ref, db_ref):
        du = _dot(_mxu(dx_ref[...], 0.5), wd_ref[...], 'nt')
        av, bv = a_ref[...], b_ref[...]
        s = _sig(av)
        db_ref[...] = (du * av * s).astype(db_ref.dtype)
        da_ref[...] = (du * bv * (s * (1.0 + av * (1.0 - s)))).astype(da_ref.dtype)

    t_spec = pl.BlockSpec((tm, tn), lambda j, i: (i, j))
    blocks = _nbytes((tm, D), F32) + _nbytes((tn, D), wd.dtype) + 2 * _nbytes((tm, tn), F32) + 2 * _nbytes((tm, tn), _MXU_DTYPE)
    return pl.pallas_call(
        body, name="ffn_bwd_up", grid=(F // tn, T // tm),
        in_specs=[pl.BlockSpec((tm, D), lambda j, i: (i, 0)), pl.BlockSpec((tn, D), lambda j, i: (j, 0)), t_spec, t_spec,
                  pl.BlockSpec(memory_space=pl.ANY)],
        out_specs=[t_spec, t_spec],
        out_shape=[jax.ShapeDtypeStruct((T, F), _MXU_DTYPE)] * 2,
        compiler_params=_params(("parallel", "parallel"), blocks, 5 * _nbytes((tm, tn), F32)),
    )(dx, wd, a, b, after)


def _row_spec(tm, w, cb=0):
    return pl.BlockSpec((tm, w), lambda i: (i, cb))


def _full_spec(shape):
    return pl.BlockSpec(shape, lambda i: (0,) * len(shape))


def _init_acc(refs):
    @pl.when(pl.program_id(0) == 0)
    def _():
        for r in refs:
            r[...] = jnp.zeros_like(r)


def _rms_fwd(x, g):
    T, D = x.shape
    tm = _tile(T, _ROW_TILE, _PACKED_ROWS)

    def body(x_ref, g_ref, h_ref):
        xv = x_ref[...]
        r = lax.rsqrt(jnp.mean(xv * xv, axis=-1, keepdims=True) + EPS)
        h_ref[...] = (xv * r * g_ref[...]).astype(h_ref.dtype)

    return pl.pallas_call(
        body, name="rms_fwd", grid=(T // tm,),
        in_specs=[_row_spec(tm, D), _full_spec((1, D))], out_specs=_row_spec(tm, D),
        out_shape=jax.ShapeDtypeStruct((T, D), _MXU_DTYPE),
        compiler_params=_params(("parallel",), 2 * _nbytes((tm, D), F32), 4 * _nbytes((tm, D), F32)),
    )(x, g)


def _rms_bwd(x, g, dh, dres):
    T, D = x.shape
    tm = _tile(T, _ROW_TILE, _SUBLANES)

    def body(x_ref, g_ref, dh_ref, dres_ref, dx_ref, dg_ref):
        _init_acc([dg_ref])
        xv, dhv = x_ref[...], dh_ref[...]
        r = lax.rsqrt(jnp.mean(xv * xv, axis=-1, keepdims=True) + EPS)
        xh = xv * r
        dxh = dhv * g_ref[...]
        dx_ref[...] = dres_ref[...] + r * (dxh - xh * jnp.mean(dxh * xh, axis=-1, keepdims=True))
        dg_ref[...] += _colsum(dhv * xh)

    return pl.pallas_call(
        body, name="rms_bwd", grid=(T // tm,),
        in_specs=[_row_spec(tm, D), _full_spec((1, D)), _row_spec(tm, D), _row_spec(tm, D)],
        out_specs=[_row_spec(tm, D), _full_spec((1, D))],
        out_shape=[jax.ShapeDtypeStruct((T, D), F32), jax.ShapeDtypeStruct((1, D), F32)],
        compiler_params=_params(("arbitrary",), 4 * _nbytes((tm, D), F32), 6 * _nbytes((tm, D), F32)),
    )(x, g, dh, dres)


def _loss_grad(y, target):
    T, D = y.shape
    tm = _tile(T, _ROW_TILE, _SUBLANES)

    def body(y_ref, t_ref, dy_ref, l_ref):
        _init_acc([l_ref])
        err = y_ref[...] - t_ref[...]
        dy_ref[...] = err * (1.0 / D)
        per_row = jnp.sum(err * err, axis=-1, keepdims=True) * (0.5 / D)
        l_ref[...] += jnp.sum(per_row, axis=0, keepdims=True)

    return pl.pallas_call(
        body, name="loss_grad", grid=(T // tm,),
        in_specs=[_row_spec(tm, D), _row_spec(tm, D)],
        out_specs=[_row_spec(tm, D), _full_spec((1, _LANES))],
        out_shape=[jax.ShapeDtypeStruct((T, D), F32), jax.ShapeDtypeStruct((1, _LANES), F32)],
        compiler_params=_params(("arbitrary",), 3 * _nbytes((tm, D), F32), 3 * _nbytes((tm, D), F32)),
    )(y, target)


def _glu(pc, C):
    a, gate = pc[:, :C], pc[:, C:]
    s = _sig(gate)
    return a, s, a * s


def _conv_fwd(pc, w, cb, lg, lb):
    T, C2 = pc.shape
    C = C2 // 2
    tm = _tile(T, _ROW_TILE // 2, HALO)
    per = tm // HALO

    def body(pc_ref, ph_ref, w_ref, cb_ref, lg_ref, lb_ref, z_ref, c_ref, ubuf):
        i = pl.program_id(0)
        _, _, u_cur = _glu(pc_ref[...], C)
        _, _, u_prev = _glu(ph_ref[...], C)
        ubuf[0:HALO, :] = jnp.where(i > 0, u_prev, 0.0)
        ubuf[HALO:HALO + tm, :] = u_cur
        acc = jnp.zeros((tm, C), F32)
        for k in range(CONV_WIDTH):
            off = HALO - (CONV_WIDTH - 1) + k
            acc = acc + w_ref[k:k + 1, :] * ubuf[off:off + tm, :]
        z = acc + cb_ref[...]
        z_ref[...] = z
        zc = z - jnp.mean(z, axis=-1, keepdims=True)
        y = zc * lax.rsqrt(jnp.mean(zc * zc, axis=-1, keepdims=True) + EPS) * lg_ref[...] + lb_ref[...]
        c_ref[...] = (y * _sig(y)).astype(c_ref.dtype)

    return pl.pallas_call(
        body, name="conv_fwd", grid=(T // tm,),
        in_specs=[_row_spec(tm, C2), pl.BlockSpec((HALO, C2), lambda i: (jnp.maximum(i * per - 1, 0), 0)),
                  _full_spec((HALO, C)), _full_spec((1, C)), _full_spec((1, C)), _full_spec((1, C))],
        out_specs=[_row_spec(tm, C), _row_spec(tm, C)],
        out_shape=[jax.ShapeDtypeStruct((T, C), F32), jax.ShapeDtypeStruct((T, C), _MXU_DTYPE)],
        scratch_shapes=[pltpu.VMEM((tm + HALO, C), F32)],
        compiler_params=_params(("parallel",), 3 * _nbytes((tm, C2), F32), 8 * _nbytes((tm, C), F32)),
    )(pc, pc, w, cb, lg, lb)


def _conv_bwd(pc, z, dc, w, lg, lb):
    T, C2 = pc.shape
    C = C2 // 2
    tm = _tile(T, _ROW_TILE // 2, HALO)
    per = tm // HALO
    n = T // tm
    last_halo = T // HALO - 1

    def body(pc_ref, ph_ref, z_ref, zn_ref, dc_ref, dcn_ref, w_ref, lg_ref, lb_ref,
             dpc_ref, dw_ref, dcb_ref, dlg_ref, dlb_ref, ubuf, dzbuf):
        i = pl.program_id(0)
        _init_acc([dw_ref, dcb_ref, dlg_ref, dlb_ref])
        g, b = lg_ref[...], lb_ref[...]

        def ln_swish_bwd(zv, dcv):
            zc = zv - jnp.mean(zv, axis=-1, keepdims=True)
            r = lax.rsqrt(jnp.mean(zc * zc, axis=-1, keepdims=True) + EPS)
            zh = zc * r
            y = zh * g + b
            s = _sig(y)
            dy = dcv * (s * (1.0 + y * (1.0 - s)))
            dyg = dy * g
            dz = r * (dyg - jnp.mean(dyg, axis=-1, keepdims=True) - zh * jnp.mean(dyg * zh, axis=-1, keepdims=True))
            return dz, dy, zh

        dz_c, dy_c, zh_c = ln_swish_bwd(z_ref[...], dc_ref[...])
        dz_n, _, _ = ln_swish_bwd(zn_ref[...], dcn_ref[...])
        dlg_ref[...] += _colsum(dy_c * zh_c)
        dlb_ref[...] += _colsum(dy_c)
        dcb_ref[...] += _colsum(dz_c)
        dzbuf[0:tm, :] = dz_c
        dzbuf[tm:tm + HALO, :] = jnp.where(i < n - 1, dz_n, 0.0)
        a, s, u_cur = _glu(pc_ref[...], C)
        _, _, u_prev = _glu(ph_ref[...], C)
        ubuf[0:HALO, :] = jnp.where(i > 0, u_prev, 0.0)
        ubuf[HALO:HALO + tm, :] = u_cur
        du = jnp.zeros((tm, C), F32)
        for k in range(CONV_WIDTH):
            back = CONV_WIDTH - 1 - k
            du = du + w_ref[k:k + 1, :] * dzbuf[back:back + tm, :]
            off = HALO - (CONV_WIDTH - 1) + k
            dw_ref[k:k + 1, :] += _colsum(dz_c * ubuf[off:off + tm, :])
        dpc_ref[:, :C] = (du * s).astype(dpc_ref.dtype)
        dpc_ref[:, C:] = (du * a * s * (1.0 - s)).astype(dpc_ref.dtype)

    nxt = lambda i: (jnp.minimum((i + 1) * per, last_halo), 0)
    vec = _full_spec((1, C))
    return pl.pallas_call(
        body, name="conv_bwd", grid=(n,),
        in_specs=[_row_spec(tm, C2), pl.BlockSpec((HALO, C2), lambda i: (jnp.maximum(i * per - 1, 0), 0)),
                  _row_spec(tm, C), pl.BlockSpec((HALO, C), nxt), _row_spec(tm, C), pl.BlockSpec((HALO, C), nxt),
                  _full_spec((HALO, C)), vec, vec],
        out_specs=[_row_spec(tm, C2), _full_spec((HALO, C)), vec, vec, vec],
        out_shape=[jax.ShapeDtypeStruct((T, C2), _MXU_DTYPE), jax.ShapeDtypeStruct((HALO, C), F32)]
                  + [jax.ShapeDtypeStruct((1, C), F32)] * 3,
        scratch_shapes=[pltpu.VMEM((tm + HALO, C), F32), pltpu.VMEM((tm + HALO, C), F32)],
        compiler_params=_params(("arbitrary",), 5 * _nbytes((tm, C2), F32), 12 * _nbytes((tm, C), F32)),
    )(pc, pc, z, z, dc, dc, w, lg, lb)


def _lat_norm_fwd(pl_, gq, gkv, QL, KVL):
    T, W = pl_.shape
    tm = _tile(T, _ROW_TILE, _PACKED_ROWS)

    def body(p_ref, gq_ref, gkv_ref, cq_ref, ckv_ref):
        def norm(xv, gv):
            return xv * lax.rsqrt(jnp.mean(xv * xv, axis=-1, keepdims=True) + EPS) * gv
        cq_ref[...] = norm(p_ref[:, :QL], gq_ref[...]).astype(cq_ref.dtype)
        ckv_ref[...] = norm(p_ref[:, QL:QL + KVL], gkv_ref[...]).astype(ckv_ref.dtype)

    return pl.pallas_call(
        body, name="lat_norm_fwd", grid=(T // tm,),
        in_specs=[_row_spec(tm, W), _full_spec((1, QL)), _full_spec((1, KVL))],
        out_specs=[_row_spec(tm, QL), _row_spec(tm, KVL)],
        out_shape=[jax.ShapeDtypeStruct((T, QL), _MXU_DTYPE), jax.ShapeDtypeStruct((T, KVL), _MXU_DTYPE)],
        compiler_params=_params(("parallel",), 2 * _nbytes((tm, W), F32), 4 * _nbytes((tm, W), F32)),
    )(pl_, gq, gkv)


def _lat_norm_bwd(pl_, gq, gkv, dcq, dckv, drope, QL, KVL):
    T, W = pl_.shape
    tm = _tile(T, _ROW_TILE, _PACKED_ROWS)

    def body(p_ref, gq_ref, gkv_ref, dcq_ref, dckv_ref, dr_ref, dp_ref, dgq_ref, dgkv_ref):
        _init_acc([dgq_ref, dgkv_ref])

        def bwd(xv, gv, dv, dg_ref):
            r = lax.rsqrt(jnp.mean(xv * xv, axis=-1, keepdims=True) + EPS)
            xh = xv * r
            dxh = dv * gv
            dg_ref[...] += _colsum(dv * xh)
            return r * (dxh - xh * jnp.mean(dxh * xh, axis=-1, keepdims=True))

        dp_ref[:, :QL] = bwd(p_ref[:, :QL], gq_ref[...], dcq_ref[...], dgq_ref).astype(dp_ref.dtype)
        dp_ref[:, QL:QL + KVL] = bwd(p_ref[:, QL:QL + KVL], gkv_ref[...], dckv_ref[...], dgkv_ref).astype(dp_ref.dtype)
        dp_ref[:, QL + KVL:] = dr_ref[...].astype(dp_ref.dtype)

    return pl.pallas_call(
        body, name="lat_norm_bwd", grid=(T // tm,),
        in_specs=[_row_spec(tm, W), _full_spec((1, QL)), _full_spec((1, KVL)), _row_spec(tm, QL), _row_spec(tm, KVL),
                  _row_spec(tm, HEAD_W)],
        out_specs=[_row_spec(tm, W), _full_spec((1, QL)), _full_spec((1, KVL))],
        out_shape=[jax.ShapeDtypeStruct((T, W), _MXU_DTYPE), jax.ShapeDtypeStruct((1, QL), F32),
                   jax.ShapeDtypeStruct((1, KVL), F32)],
        compiler_params=_params(("arbitrary",), 4 * _nbytes((tm, W), F32), 6 * _nbytes((tm, W), F32)),
    )(pl_, gq, gkv, dcq, dckv, drope)


def _norm_rope(xv, gv, ct, sa, sb):
    r = lax.rsqrt(jnp.sum(xv * xv, axis=-1, keepdims=True) * (1.0 / QK_DIM) + EPS)
    xn = xv * r * gv
    half = ROPE_DIM // 2
    return xn * ct + pltpu.roll(xn, HEAD_W - half, 1) * sa + pltpu.roll(xn, half, 1) * sb


def _norm_rope_bwd(xv, gv, ct, sa, sb, dout):
    half = ROPE_DIM // 2
    dxn = dout * ct + pltpu.roll(dout * sa, half, 1) + pltpu.roll(dout * sb, HEAD_W - half, 1)
    r = lax.rsqrt(jnp.sum(xv * xv, axis=-1, keepdims=True) * (1.0 / QK_DIM) + EPS)
    xh = xv * r
    dxh = dxn * gv
    dx = r * (dxh - xh * (jnp.sum(dxh * xh, axis=-1, keepdims=True) * (1.0 / QK_DIM)))
    return dx, _colsum(dxn * xh)


def _qk_prep_fwd(q_raw, k_raw, pl_, rope_cb, ct, sa, sb, gq, gk):
    T, HW = q_raw.shape
    H = HW // HEAD_W
    tm = _tile(T, _ROW_TILE, _PACKED_ROWS)

    def body(q_ref, k_ref, r_ref, ct_ref, sa_ref, sb_ref, gq_ref, gk_ref, qn_ref, kn_ref):
        ctv, sav, sbv, rv = ct_ref[...], sa_ref[...], sb_ref[...], r_ref[...]
        for h in range(H):
            hs = slice(h * HEAD_W, (h + 1) * HEAD_W)
            qn_ref[:, hs] = _norm_rope(q_ref[:, hs], gq_ref[...], ctv, sav, sbv).astype(qn_ref.dtype)
            kn_ref[:, hs] = _norm_rope(k_ref[:, hs] + rv, gk_ref[...], ctv, sav, sbv).astype(kn_ref.dtype)

    tb = _row_spec(tm, HEAD_W)
    gb = _full_spec((1, HEAD_W))
    return pl.pallas_call(
        body, name="qk_prep_fwd", grid=(T // tm,),
        in_specs=[_row_spec(tm, HW), _row_spec(tm, HW), _row_spec(tm, HEAD_W, rope_cb), tb, tb, tb, gb, gb],
        out_specs=[_row_spec(tm, HW), _row_spec(tm, HW)],
        out_shape=[jax.ShapeDtypeStruct((T, HW), _MXU_DTYPE)] * 2,
        compiler_params=_params(("parallel",), 3 * _nbytes((tm, HW), F32) + 4 * _nbytes((tm, HEAD_W), F32), 8 * _nbytes((tm, HEAD_W), F32)),
    )(q_raw, k_raw, pl_, ct, sa, sb, gq, gk)


def _qk_prep_bwd(q_raw, k_raw, pl_, rope_cb, ct, sa, sb, gq, gk, dqn, dkn):
    T, HW = q_raw.shape
    H = HW // HEAD_W
    tm = _tile(T, _ROW_TILE, _PACKED_ROWS)

    def body(q_ref, k_ref, r_ref, ct_ref, sa_ref, sb_ref, gq_ref, gk_ref, dqn_ref, dkn_ref,
             dq_ref, dk_ref, dr_ref, dgq_ref, dgk_ref):
        _init_acc([dgq_ref, dgk_ref])
        ctv, sav, sbv, rv = ct_ref[...], sa_ref[...], sb_ref[...], r_ref[...]
        dr = dgq_sum = dgk_sum = None
        for h in range(H):
            hs = slice(h * HEAD_W, (h + 1) * HEAD_W)
            dq, dgq = _norm_rope_bwd(q_ref[:, hs], gq_ref[...], ctv, sav, sbv, dqn_ref[:, hs])
            dk, dgk = _norm_rope_bwd(k_ref[:, hs] + rv, gk_ref[...], ctv, sav, sbv, dkn_ref[:, hs])
            dq_ref[:, hs] = dq.astype(dq_ref.dtype)
            dk_ref[:, hs] = dk.astype(dk_ref.dtype)
            dr = dk if h == 0 else dr + dk
            dgq_sum = dgq if h == 0 else dgq_sum + dgq
            dgk_sum = dgk if h == 0 else dgk_sum + dgk
        dr_ref[...] = dr
        dgq_ref[...] += dgq_sum
        dgk_ref[...] += dgk_sum

    tb = _row_spec(tm, HEAD_W)
    gb = _full_spec((1, HEAD_W))
    hb = _row_spec(tm, HW)
    return pl.pallas_call(
        body, name="qk_prep_bwd", grid=(T // tm,),
        in_specs=[hb, hb, _row_spec(tm, HEAD_W, rope_cb), tb, tb, tb, gb, gb, hb, hb],
        out_specs=[hb, hb, tb, gb, gb],
        out_shape=[jax.ShapeDtypeStruct((T, HW), _MXU_DTYPE)] * 2 + [jax.ShapeDtypeStruct((T, HEAD_W), F32)]
                  + [jax.ShapeDtypeStruct((1, HEAD_W), F32)] * 2,
        compiler_params=_params(("arbitrary",), 5 * _nbytes((tm, HW), F32) + 5 * _nbytes((tm, HEAD_W), F32), 12 * _nbytes((tm, HEAD_W), F32)),
    )(q_raw, k_raw, pl_, ct, sa, sb, gq, gk, dqn, dkn)


_NEG = -1e30
_LOG2E = 1.4426950408889634
_SCORE_C = QK_DIM ** -0.5 * _LOG2E


def _pieces(t, rb, diag):
    assert CHUNK % rb == 0 and rb % _SUBLANES == 0
    out = []
    for r in range(t // rb):
        lo = (r * rb // CHUNK) * CHUNK if diag else 0
        for c in range(t // _LANES):
            cut = None if (c + 1) * _LANES <= lo else max(lo - c * _LANES, 0)
            out.append((slice(r * rb, (r + 1) * rb), slice(c * _LANES, (c + 1) * _LANES), c, cut))
    return out


def _groups(x):
    return x.reshape(x.shape[0] // _SUBLANES, _SUBLANES, x.shape[1])


def _all_sublanes(x8, op):
    return jnp.broadcast_to(op(x8, axis=0, keepdims=True), x8.shape)


def _lane_ge(rb, cut):
    return lax.broadcasted_iota(jnp.int32, (rb, _LANES), 1) >= cut


def _attn_fwd(q, k, v):
    T, HW = q.shape
    H = HW // HEAD_W
    t = _tile(T, _ATT_TILE)
    rb = min(_ATT_ROWS, t)
    nc = t // _LANES

    def body(q_ref, k_ref, v_ref, o_ref, lse_ref, s_s, p_s, m_s, l_s, acc_s):
        i = pl.program_id(1)
        qv = q_ref[...]
        m_s[...] = jnp.full_like(m_s, _NEG)
        l_s[...] = jnp.zeros_like(l_s)
        acc_s[...] = jnp.zeros_like(acc_s)

        def step(j, diag):
            rows = pl.ds(pl.multiple_of(j * t, t), t)
            s_s[...] = _dot(k_ref[rows, :], qv, 'nt')
            pieces = _pieces(t, rb, diag)

            def scores(rs, cs, cut):
                sb = s_s[rs, cs]
                return jnp.where(_lane_ge(rb, cut), sb, _NEG) if cut else sb

            mx = [None] * nc
            for rs, cs, c, cut in pieces:
                if cut is not None:
                    g = jnp.max(_groups(scores(rs, cs, cut)), axis=0)
                    mx[c] = g if mx[c] is None else jnp.maximum(mx[c], g)
            m_new, alpha = [], []
            for c in range(nc):
                cs = slice(c * _LANES, (c + 1) * _LANES)
                m_prev = m_s[:, cs]
                m_new.append(jnp.maximum(m_prev, _all_sublanes(mx[c], jnp.max)))
                alpha.append(jnp.exp2((m_prev - m_new[c]) * _SCORE_C))
                m_s[:, cs] = m_new[c]
            lsum = [jnp.zeros((_SUBLANES, _LANES), F32)] * nc
            for rs, cs, c, cut in pieces:
                if cut is None:
                    p_s[rs, cs] = jnp.zeros((rb, _LANES), p_s.dtype)
                    continue
                p = jnp.exp2((scores(rs, cs, cut) - jnp.tile(m_new[c], (rb // _SUBLANES, 1))) * _SCORE_C)
                lsum[c] = lsum[c] + jnp.sum(_groups(p), axis=0)
                p_s[rs, cs] = p.astype(p_s.dtype)
            for c in range(nc):
                cs = slice(c * _LANES, (c + 1) * _LANES)
                l_s[:, cs] = alpha[c] * l_s[:, cs] + _all_sublanes(lsum[c], jnp.sum)
            a = jnp.tile(jnp.concatenate(alpha, axis=1), (HEAD_W // _SUBLANES, 1))
            acc_s[...] = a * acc_s[...] + _dot(v_ref[rows, :], p_s[...], 'tn')

        def loop_body(j, carry):
            step(j, False)
            return carry

        lax.fori_loop(0, i, loop_body, 0)
        step(i, True)
        l = l_s[...]
        o_t = acc_s[...] / jnp.tile(l, (HEAD_W // _SUBLANES, 1))
        o_ref[...] = o_t.T.astype(o_ref.dtype)
        lse_ref[...] = m_s[...] * _SCORE_C + jnp.log(l) * _LOG2E

    qb = pl.BlockSpec((t, HEAD_W), lambda h, i: (i, h))
    kb = pl.BlockSpec((T, HEAD_W), lambda h, i: (0, h))
    st = pltpu.VMEM((_SUBLANES, t), F32)
    return pl.pallas_call(
        body, name="attn_fwd", grid=(H, T // t),
        in_specs=[qb, kb, kb], out_specs=[qb, pl.BlockSpec((_SUBLANES, t), lambda h, i: (h, i))],
        out_shape=[jax.ShapeDtypeStruct((T, HW), _MXU_DTYPE), jax.ShapeDtypeStruct((H * _SUBLANES, T), F32)],
        scratch_shapes=[pltpu.VMEM((t, t), F32), pltpu.VMEM((t, t), _MXU_DTYPE), st, st, pltpu.VMEM((HEAD_W, t), F32)],
        compiler_params=_params(("parallel", "arbitrary"), 2 * _nbytes((T, HEAD_W), _MXU_DTYPE) + 3 * _nbytes((t, HEAD_W), F32),
                                4 * _nbytes((t, t), F32)),
    )(q, k, v)


def _attn_bwd(q, k, v, o, do, lse2):
    T, HW = q.shape
    H = HW // HEAD_W
    t = _tile(T, _ATT_TILE)
    n = T // t
    rb = min(_ATT_ROWS, t)
    scale = QK_DIM ** -0.5

    def body(k_ref, v_ref, q_ref, o_ref, do_ref, lse_ref, dq_ref, dk_ref, dv_ref,
             s_s, dp_s, p_s, ds_s, dk_s, dv_s, delta_s, dqt_s):
        j = pl.program_id(1)

        @pl.when(j == 0)
        def _():
            dqt_s[...] = jnp.zeros_like(dqt_s)
            for b in range(n):
                rs = slice(b * t, (b + 1) * t)
                d = jnp.sum(do_ref[rs, :].astype(F32) * o_ref[rs, :].astype(F32), axis=1, keepdims=True)
                delta_s[:, rs] = jnp.broadcast_to(d, (t, HEAD_W)).T[0:_SUBLANES, :]

        kv, vv = k_ref[...], v_ref[...]
        dk_s[...] = jnp.zeros_like(dk_s)
        dv_s[...] = jnp.zeros_like(dv_s)

        def step(i, diag):
            base = pl.multiple_of(i * t, t)
            qv, dov = q_ref[pl.ds(base, t), :], do_ref[pl.ds(base, t), :]
            s_s[...] = _dot(kv, qv, 'nt')
            dp_s[...] = _dot(vv, dov, 'nt')
            for rs, cs, c, cut in _pieces(t, rb, diag):
                if cut is None:
                    p_s[rs, cs] = jnp.zeros((rb, _LANES), p_s.dtype)
                    ds_s[rs, cs] = jnp.zeros((rb, _LANES), ds_s.dtype)
                    continue
                lanes = pl.ds(pl.multiple_of(base + c * _LANES, _LANES), _LANES)
                lse = jnp.tile(lse_ref[:, lanes], (rb // _SUBLANES, 1))
                dl = jnp.tile(delta_s[:, lanes], (rb // _SUBLANES, 1))
                p = jnp.exp2(s_s[rs, cs] * _SCORE_C - lse)
                if cut:
                    p = jnp.where(_lane_ge(rb, cut), p, 0.0)
                p_s[rs, cs] = p.astype(p_s.dtype)
                ds_s[rs, cs] = (p * (dp_s[rs, cs] - dl)).astype(ds_s.dtype)
            dsv = ds_s[...]
            dv_s[...] += _dot(p_s[...], dov, 'nn')
            dk_s[...] += _dot(dsv, qv, 'nn')
            dqt_s[:, pl.ds(base, t)] += _dot(kv, dsv, 'tn')

        def loop_body(i, carry):
            step(i, False)
            return carry

        step(j, True)
        lax.fori_loop(j + 1, n, loop_body, 0)
        dk_ref[...] = dk_s[...] * scale
        dv_ref[...] = dv_s[...].astype(dv_ref.dtype)

        @pl.when(j == n - 1)
        def _():
            for b in range(n):
                rs = slice(b * t, (b + 1) * t)
                dq_ref[rs, :] = dqt_s[:, rs].T * scale

    jb = pl.BlockSpec((t, HEAD_W), lambda h, j: (j, h))
    fb = pl.BlockSpec((T, HEAD_W), lambda h, j: (0, h))
    sc = pltpu.VMEM((t, HEAD_W), F32)
    return pl.pallas_call(
        body, name="attn_bwd", grid=(H, n),
        in_specs=[jb, jb, fb, fb, fb, pl.BlockSpec((_SUBLANES, T), lambda h, j: (h, 0))], out_specs=[fb, jb, jb],
        out_shape=[jax.ShapeDtypeStruct((T, HW), F32), jax.ShapeDtypeStruct((T, HW), F32),
                   jax.ShapeDtypeStruct((T, HW), _MXU_DTYPE)],
        scratch_shapes=[pltpu.VMEM((t, t), F32), pltpu.VMEM((t, t), F32), pltpu.VMEM((t, t), _MXU_DTYPE),
                        pltpu.VMEM((t, t), _MXU_DTYPE), sc, sc, pltpu.VMEM((_SUBLANES, T), F32),
                        pltpu.VMEM((HEAD_W, T), F32)],
        compiler_params=_params(("parallel", "arbitrary"), 3 * _nbytes((T, HEAD_W), _MXU_DTYPE) + _nbytes((T, HEAD_W), F32)
                                + 4 * _nbytes((t, HEAD_W), F32), 4 * _nbytes((t, t), F32) + _nbytes((T, HEAD_W), F32)),
    )(k, v, q, o, do, lse2)


def _gate_fwd(pg, gb, yc, ym):
    T, D = yc.shape
    tm = _tile(T, _ROW_TILE, _PACKED_ROWS)

    def body(g0_ref, g1_ref, gb_ref, yc_ref, ym_ref, y_ref):
        s0 = _sig(g0_ref[...] + gb_ref[0:1, :])
        s1 = _sig(g1_ref[...] + gb_ref[1:2, :])
        y_ref[...] = (s0 * yc_ref[...] + s1 * ym_ref[...]).astype(y_ref.dtype)

    return pl.pallas_call(
        body, name="gate_fwd", grid=(T // tm,),
        in_specs=[_row_spec(tm, D, 0), _row_spec(tm, D, 1), _full_spec((2, D)), _row_spec(tm, D), _row_spec(tm, D)],
        out_specs=_row_spec(tm, D),
        out_shape=jax.ShapeDtypeStruct((T, D), _MXU_DTYPE),
        compiler_params=_params(("parallel",), 5 * _nbytes((tm, D), F32), 4 * _nbytes((tm, D), F32)),
    )(pg, pg, gb, yc, ym)


def _gate_bwd(pg, gb, yc, ym, dy):
    T, D = yc.shape
    tm = _tile(T, _ROW_TILE, _PACKED_ROWS)

    def body(g0_ref, g1_ref, gb_ref, yc_ref, ym_ref, dy_ref, dyc_ref, dym_ref, dpg_ref, dgb_ref):
        _init_acc([dgb_ref])
        dyv = dy_ref[...]
        s0 = _sig(g0_ref[...] + gb_ref[0:1, :])
        s1 = _sig(g1_ref[...] + gb_ref[1:2, :])
        dyc_ref[...] = (dyv * s0).astype(dyc_ref.dtype)
        dym_ref[...] = (dyv * s1).astype(dym_ref.dtype)
        d0 = dyv * yc_ref[...] * s0 * (1.0 - s0)
        d1 = dyv * ym_ref[...] * s1 * (1.0 - s1)
        dpg_ref[:, :D] = d0.astype(dpg_ref.dtype)
        dpg_ref[:, D:] = d1.astype(dpg_ref.dtype)
        dgb_ref[0:1, :] += _colsum(d0)
        dgb_ref[1:2, :] += _colsum(d1)

    return pl.pallas_call(
        body, name="gate_bwd", grid=(T // tm,),
        in_specs=[_row_spec(tm, D, 0), _row_spec(tm, D, 1), _full_spec((2, D)), _row_spec(tm, D), _row_spec(tm, D),
                  _row_spec(tm, D)],
        out_specs=[_row_spec(tm, D), _row_spec(tm, D), _row_spec(tm, 2 * D), _full_spec((2, D))],
        out_shape=[jax.ShapeDtypeStruct((T, D), _MXU_DTYPE)] * 2 + [jax.ShapeDtypeStruct((T, 2 * D), _MXU_DTYPE),
                                                                   jax.ShapeDtypeStruct((2, D), F32)],
        compiler_params=_params(("arbitrary",), 7 * _nbytes((tm, D), F32), 8 * _nbytes((tm, D), F32)),
    )(pg, pg, gb, yc, ym, dy)


def _place():
    return lax.axis_index("x"), lax.axis_index("y"), lax.axis_index("c")


def _all_gather(name, xs):
    R, C = xs.shape
    MESH = pl.DeviceIdType.MESH

    def body(x_ref, out_ref, send_sems, recv_sems, local_sem):
        x, y, c = _place()
        me, sibling = (x, y, c), (x, y, 1 - c)
        chips = [(1 - x, y), (x, 1 - y), (1 - x, 1 - y)]

        def rows(px, py, pc):
            return out_ref.at[4 * px + 2 * py + pc]

        def copy(k, block, to, src=None):
            return pltpu.make_async_remote_copy(
                src_ref=rows(*block) if src is None else src, dst_ref=rows(*block),
                send_sem=send_sems.at[k], recv_sem=recv_sems.at[k], device_id=to, device_id_type=MESH)

        mine = pltpu.make_async_copy(x_ref, rows(*me), local_sem)
        mine.start()
        first = [copy(0, me, sibling, src=x_ref)]
        first += [copy(1 + j, me, (*chip, c), src=x_ref) for j, chip in enumerate(chips)]
        for cp in first:
            cp.start()
        passed = [copy(4 + j, (*chip, c), sibling) for j, chip in enumerate(chips)]
        for j, chip in enumerate(chips):
            copy(1 + j, (*chip, c), me).wait_recv()
            passed[j].start()
        copy(0, sibling, me).wait_recv()
        for j, chip in enumerate(chips):
            copy(4 + j, (*chip, 1 - c), me).wait_recv()
        for cp in first + passed:
            cp.wait_send()
        mine.wait()

    return pl.pallas_call(
        body, name=name,
        out_shape=jax.ShapeDtypeStruct((N_DEV, R, C), xs.dtype),
        in_specs=[pl.BlockSpec(memory_space=pl.ANY)], out_specs=pl.BlockSpec(memory_space=pl.ANY),
        scratch_shapes=[pltpu.SemaphoreType.DMA((7,)), pltpu.SemaphoreType.DMA((7,)), pltpu.SemaphoreType.DMA],
    )(xs)


def _pair_exchange(name, xs):
    _, G, R, C = xs.shape

    def body(x_ref, out_ref, send_sem, recv_sem):
        x, y, c = _place()
        cp = pltpu.make_async_remote_copy(
            src_ref=x_ref.at[1 - c], dst_ref=out_ref, send_sem=send_sem, recv_sem=recv_sem,
            device_id=(x, y, 1 - c), device_id_type=pl.DeviceIdType.MESH)
        cp.start()
        cp.wait()

    return pl.pallas_call(
        body, name=name,
        out_shape=jax.ShapeDtypeStruct((G, R, C), xs.dtype),
        in_specs=[pl.BlockSpec(memory_space=pl.ANY)], out_specs=pl.BlockSpec(memory_space=pl.ANY),
        scratch_shapes=[pltpu.SemaphoreType.DMA, pltpu.SemaphoreType.DMA],
    )(xs)


def _pair_sum(name, xs, got):
    _, G, R, C = xs.shape
    tr = _tile(R, _PACK_ROW_MULT, _PACKED_ROWS)

    def body(c_ref, a_ref, b_ref, o_ref):
        o_ref[...] = (a_ref[...].astype(F32) + b_ref[...].astype(F32)).astype(o_ref.dtype)

    blk = pl.BlockSpec((None, tr, C), lambda g, i, c_ref: (g, i, 0))
    return pl.pallas_call(
        body, name=name,
        grid_spec=pltpu.PrefetchScalarGridSpec(
            num_scalar_prefetch=1, grid=(G, R // tr),
            in_specs=[pl.BlockSpec((None, None, tr, C), lambda g, i, c_ref: (c_ref[0], g, i, 0)), blk],
            out_specs=blk),
        out_shape=jax.ShapeDtypeStruct((G, R, C), xs.dtype),
        compiler_params=_params(("parallel", "parallel"), 3 * _nbytes((tr, C), F32), 2 * _nbytes((tr, C), F32)),
    )(lax.axis_index("c").reshape(1).astype(jnp.int32), xs, got)


def _chip_all_to_all(name, xs):
    G, R, C = xs.shape

    def body(x_ref, out_ref, send_sems, recv_sems, local_sem):
        x, y, c = _place()
        me = 2 * x + y
        mine = pltpu.make_async_copy(x_ref.at[me], out_ref.at[me], local_sem)
        mine.start()
        copies = []
        for k in range(1, G):
            px = 1 - x if k & 2 else x
            py = 1 - y if k & 1 else y
            copies.append(pltpu.make_async_remote_copy(
                src_ref=x_ref.at[2 * px + py], dst_ref=out_ref.at[me],
                send_sem=send_sems.at[k - 1], recv_sem=recv_sems.at[k - 1],
                device_id=(px, py, c), device_id_type=pl.DeviceIdType.MESH))
        for cp in copies:
            cp.start()
        for cp in copies:
            cp.wait()
        mine.wait()

    return pl.pallas_call(
        body, name=name,
        out_shape=jax.ShapeDtypeStruct((G, R, C), xs.dtype),
        in_specs=[pl.BlockSpec(memory_space=pl.ANY)], out_specs=pl.BlockSpec(memory_space=pl.ANY),
        scratch_shapes=[pltpu.SemaphoreType.DMA((G - 1,)), pltpu.SemaphoreType.DMA((G - 1,)), pltpu.SemaphoreType.DMA],
    )(xs)


_HBM_SPEC = pl.BlockSpec(memory_space=pltpu.HBM)
_SEM_SPEC = pl.BlockSpec(memory_space=pltpu.SEMAPHORE)
_DATAFLOW = pltpu.SideEffectType.DATAFLOW_SIDE_EFFECTING


def _peers():
    x, y, c = _place()
    out = []
    for k in range(1, N_DEV):
        px = 1 - x if k & 4 else x
        py = 1 - y if k & 2 else y
        pc = 1 - c if k & 1 else c
        out.append((k - 1, (px, py, pc), 4 * px + 2 * py + pc))
    return 4 * x + 2 * y + c, out


def _exchange_copies(x_ref, land_ref, send_sems, recv_sems, scatter):
    me, peers = _peers()
    return [pltpu.make_async_remote_copy(
        src_ref=x_ref.at[idx] if scatter else x_ref, dst_ref=land_ref.at[me],
        send_sem=send_sems.at[k], recv_sem=recv_sems.at[k], device_id=dev, device_id_type=pl.DeviceIdType.MESH)
        for k, dev, idx in peers]


def _exchange_start(name, xs, land, scatter):
    def body(x_ref, land_ref, send_sems, recv_sems, x_thru, land_thru, token):
        for cp in _exchange_copies(x_ref, land_ref, send_sems, recv_sems, scatter):
            cp.start()
        token[...] = jnp.zeros_like(token)

    sems = pltpu.SemaphoreType.DMA((N_DEV - 1,))
    return pl.pallas_call(
        body, name=name,
        out_shape=(sems, sems, pltpu.HBM(xs.shape, xs.dtype), pltpu.HBM(land.shape, land.dtype),
                   jax.ShapeDtypeStruct((_SUBLANES, _LANES), F32)),
        in_specs=(_HBM_SPEC, _HBM_SPEC),
        out_specs=(_SEM_SPEC, _SEM_SPEC, _HBM_SPEC, _HBM_SPEC, pl.BlockSpec(memory_space=pltpu.VMEM)),
        input_output_aliases={0: 2, 1: 3},
        compiler_params=pltpu.CompilerParams(has_side_effects=_DATAFLOW),
    )(pltpu.with_memory_space_constraint(xs, pltpu.HBM), pltpu.with_memory_space_constraint(land, pltpu.HBM))


def _exchange_wait(name, send_sems, recv_sems, xs, land, after, scatter):
    def body(x_ref, land_ref, send_sems, recv_sems, after_ref, x_dead, got_ref):
        for cp in _exchange_copies(x_ref, land_ref, send_sems, recv_sems, scatter):
            cp.wait_send()
            cp.wait_recv()

    return pl.pallas_call(
        body, name=name,
        out_shape=(pltpu.HBM(xs.shape, xs.dtype), pltpu.HBM(land.shape, land.dtype)),
        in_specs=(_HBM_SPEC, _HBM_SPEC, _SEM_SPEC, _SEM_SPEC, pl.BlockSpec(memory_space=pl.ANY)),
        out_specs=(_HBM_SPEC, _HBM_SPEC), input_output_aliases={0: 0, 1: 1},
        compiler_params=pltpu.CompilerParams(has_side_effects=_DATAFLOW),
    )(xs, land, send_sems, recv_sems, after)[1]


def _own_slot(block, me):
    land = lax.empty((N_DEV,) + block.shape, block.dtype)
    return lax.dynamic_update_slice(land, block[None], (me,) + (0,) * block.ndim)


def _sum_blocks(name, parts, scale=None):
    n, R, C = parts.shape
    tr = _tile(R, _PACK_ROW_MULT, _PACKED_ROWS)

    def body(p_ref, o_ref):
        acc = p_ref[0].astype(F32)
        for s in range(1, n):
            acc = acc + p_ref[s].astype(F32)
        if scale is not None:
            acc = acc * scale
        o_ref[...] = acc

    return pl.pallas_call(
        body, name=name, grid=(R // tr,),
        in_specs=[pl.BlockSpec((n, tr, C), lambda i: (0, i, 0))], out_specs=pl.BlockSpec((tr, C), lambda i: (i, 0)),
        out_shape=jax.ShapeDtypeStruct((R, C), F32),
        compiler_params=_params(("parallel",), _nbytes((n, tr, C), parts.dtype) + _nbytes((tr, C), F32), 2 * _nbytes((tr, C), F32)),
    )(parts)


def _adamw(w, g, m, v):
    shape = w.shape
    w2, g2, m2, v2 = (t.reshape(-1, shape[-1]) for t in (w, g, m, v))
    R, C = w2.shape
    tr = _tile(R, _ROW_TILE, _SUBLANES)

    def body(w_ref, g_ref, m_ref, v_ref, d_ref, nm_ref, nv_ref):
        gv = g_ref[...]
        mv = ADAM_B1 * m_ref[...] + (1.0 - ADAM_B1) * gv
        vv = ADAM_B2 * v_ref[...] + (1.0 - ADAM_B2) * (gv * gv)
        m_hat = mv / (1.0 - ADAM_B1 ** ADAM_STEP)
        v_hat = vv / (1.0 - ADAM_B2 ** ADAM_STEP)
        d_ref[...] = -ADAM_LR * (m_hat / (jnp.sqrt(v_hat) + ADAM_EPS) + ADAM_WD * w_ref[...])
        nm_ref[...] = mv
        nv_ref[...] = vv

    spec = pl.BlockSpec((tr, C), lambda i: (i, 0))
    outs = pl.pallas_call(
        body, name="adamw", grid=(R // tr,),
        in_specs=[spec] * 4, out_specs=[spec] * 3,
        out_shape=[jax.ShapeDtypeStruct((R, C), F32)] * 3,
        compiler_params=_params(("parallel",), 7 * _nbytes((tr, C), F32), 4 * _nbytes((tr, C), F32)),
    )(w2, g2, m2, v2)
    return tuple(o.reshape(shape) for o in outs)


def _pack_rows(parts, lead=()):
    out, spans, r0 = [], [], 0
    unit = _PACKED_ROWS * _PACK_W
    for p in parts:
        flat = p.reshape(lead + (-1,))
        size = flat.shape[-1]
        pad = (-size) % unit
        if pad:
            flat = jnp.pad(flat, [(0, 0)] * len(lead) + [(0, pad)])
        rows = (size + pad) // _PACK_W
        out.append(flat.reshape(lead + (rows, _PACK_W)))
        spans.append((r0, rows, p.shape[len(lead):]))
        r0 += rows
    pad = (-r0) % _PACK_ROW_MULT
    if pad:
        out.append(jnp.zeros(lead + (pad, _PACK_W), out[0].dtype))
    return jnp.concatenate(out, axis=len(lead)), spans


def _unpack_rows(packed, span, lead=()):
    r0, rows, shape = span
    size = 1
    for s in shape:
        size *= s
    sl = lax.slice_in_dim(packed, r0, r0 + rows, axis=len(lead))
    return sl.reshape(lead + (-1,))[..., :size].reshape(lead + tuple(shape))


def _pack_flat(parts):
    flat, spans, at = [], [], 0
    for p in parts:
        size = p.size
        padded = size + (-size) % _LANES
        flat.append(jnp.pad(p.reshape(-1), (0, padded - size)))
        spans.append((at, size, p.shape))
        at += padded
    tail = (-at) % (_PACKED_ROWS * _PACK_W)
    if tail:
        flat.append(jnp.zeros((tail,), flat[0].dtype))
    return jnp.concatenate(flat).reshape(-1, _PACK_W), spans


def _unpack_flat(packed, span, lead=()):
    at, size, shape = span
    return lax.slice_in_dim(packed.reshape(lead + (-1,)), at, at + size, axis=len(lead)).reshape(lead + tuple(shape))


def _rope_tables(positions, T):
    inv_freq = ROPE_THETA ** (-jnp.arange(0, ROPE_DIM, 2, dtype=F32) / ROPE_DIM)
    ang = positions.reshape(T, 1).astype(F32) * inv_freq
    cos, sin = jnp.cos(ang), jnp.sin(ang)
    half = ROPE_DIM // 2
    z = lambda n: jnp.zeros((T, n), F32)
    tail = HEAD_W - QK_DIM
    ct = jnp.concatenate([jnp.ones((T, NOPE_DIM), F32), cos, cos, z(tail)], axis=1)
    sa = jnp.concatenate([z(NOPE_DIM), -sin, z(half), z(tail)], axis=1)
    sb = jnp.concatenate([z(NOPE_DIM), z(half), sin, z(tail)], axis=1)
    return ct, sa, sb


def _pad_heads(wt, per_head, keep_from=0, keep=None):
    K = wt.shape[1]
    keep = per_head if keep is None else keep
    w3 = wt.reshape(N_HEADS, per_head, K)[:, keep_from:keep_from + keep]
    return jnp.pad(w3, ((0, 0), (0, HEAD_W - keep), (0, 0))).reshape(N_HEADS * HEAD_W, K)


def _unpad_heads(g, keep):
    return g.reshape(N_HEADS, HEAD_W, g.shape[1])[:, :keep]


def _layer_weights(W):
    D = W['w_out'].shape[1]
    C = W['w_conv_out'].shape[1]
    QL, KVL = W['w_uq'].shape[1], W['w_ukv'].shape[1]
    o1 = 2 * C
    o2, o3 = o1 + QL, o1 + QL + KVL
    o4 = o3 + ROPE_DIM
    win = W['w_in']
    zr = lambda n: jnp.zeros((n, D), win.dtype)
    L = dict(W)
    L['w_c'] = win[:o1]
    L['w_l'] = jnp.concatenate([win[o1:o3], zr(NOPE_DIM), win[o3:o4], zr(HEAD_W - QK_DIM)], axis=0)
    L['w_g'] = win[o4:]
    L['w_q'] = _pad_heads(W['w_uq'], QK_DIM)
    L['w_k'] = _pad_heads(W['w_ukv'], NOPE_DIM + V_DIM, 0, NOPE_DIM)
    L['w_v'] = _pad_heads(W['w_ukv'], NOPE_DIM + V_DIM, NOPE_DIM, V_DIM)
    wmo = W['w_mla_out'].reshape(D, N_HEADS, V_DIM)
    L['w_mo'] = jnp.pad(wmo, ((0, 0), (0, 0), (0, HEAD_W - V_DIM))).reshape(D, N_HEADS * HEAD_W)
    L['dims'] = (D, C, QL, KVL)
    return L


def _row(v, width=None):
    v = v.reshape(1, -1)
    if width is not None and v.shape[1] < width:
        v = jnp.pad(v, ((0, 0), (0, width - v.shape[1])))
    return v


def _ffn_fwd(x, g, wg, wu, wd):
    h = _rms_fwd(x, _row(g))
    a, b, u = _ffn_up(h, wg, wu)
    x_out = _mm("ffn_down", u, wd, 'nn', res=x, scale=0.5)
    return x_out, (x, h, a, b, u)


def _ffn_bwd(dx, saved, g, wg, wu, wd, after):
    x, h, a, b, u = saved
    d_wd = _mm("ffn_dwd", u, dx, 'tn', scale=0.5)
    da, db = _ffn_bwd_up(dx, wd, a, b, after)
    dh = _mm("ffn_dh_g", da, wg, 'nn')
    dh = _mm("ffn_dh_u", db, wu, 'nn', res=dh)
    d_wg = _mm("ffn_dwg", da, h, 'tn')
    d_wu = _mm("ffn_dwg", db, h, 'tn')
    dx_in, dg = _rms_bwd(x, _row(g), dh, dx)
    return dx_in, dg, d_wg, d_wu, d_wd


def _layer_fwd(x, L, S, tabs):
    D, C, QL, KVL = L['dims']
    rope_cb = (QL + KVL) // HEAD_W
    x1, ffn1 = _ffn_fwd(x, S['ffn1_norm'], L['ffn1_w_gate'], L['ffn1_w_up'], L['ffn1_w_down'])
    h = _rms_fwd(x1, _row(S['mix_norm']))
    pc = _mm("proj_c", h, L['w_c'], 'nt')
    pl_ = _mm("proj_l", h, L['w_l'], 'nt')
    pg = _mm("proj_g", h, L['w_g'], 'nt')
    conv_w = jnp.pad(S['conv_w'], ((0, HALO - CONV_WIDTH), (0, 0)))
    z, c = _conv_fwd(pc, conv_w, _row(S['conv_b']), _row(S['conv_ln_g']), _row(S['conv_ln_b']))
    yc = _mm("conv_out", c, L['w_conv_out'], 'nt')
    cqn, ckvn = _lat_norm_fwd(pl_, _row(S['cq_norm']), _row(S['ckv_norm']), QL, KVL)
    q_raw = _mm("up_q", cqn, L['w_q'], 'nt')
    k_raw = _mm("up_kv", ckvn, L['w_k'], 'nt')
    v = _mm("up_kv", ckvn, L['w_v'], 'nt', out_dtype=_MXU_DTYPE)
    gq, gk = _row(S['q_norm'], HEAD_W), _row(S['k_norm'], HEAD_W)
    qn, kn = _qk_prep_fwd(q_raw, k_raw, pl_, rope_cb, *tabs, gq, gk)
    o, lse = _attn_fwd(qn, kn, v)
    ym = _mm("mla_out", o, L['w_mo'], 'nt')
    y = _gate_fwd(pg, S['gate_bias'], yc, ym)
    x2 = _mm("mix_out", y, L['w_out'], 'nn', res=x1)
    x3, ffn2 = _ffn_fwd(x2, S['ffn2_norm'], L['ffn2_w_gate'], L['ffn2_w_up'], L['ffn2_w_down'])
    saved = dict(ffn1=ffn1, ffn2=ffn2, x1=x1, h=h, pc=pc, pl=pl_, pg=pg, z=z, c=c, yc=yc, cqn=cqn, ckvn=ckvn,
                 q_raw=q_raw, k_raw=k_raw, v=v, qn=qn, kn=kn, o=o, lse=lse, ym=ym, y=y, conv_w=conv_w, gq=gq, gk=gk)
    return x3, saved


def _layer_bwd_ffn1(dx, L, S, A, after):
    G, g = {}, {}
    dx, g['ffn1_norm'], G['ffn1_w_gate'], G['ffn1_w_up'], G['ffn1_w_down'] = _ffn_bwd(
        dx, A['ffn1'], S['ffn1_norm'], L['ffn1_w_gate'], L['ffn1_w_up'], L['ffn1_w_down'], after)
    return dx, G, g


def _layer_bwd_mix(dx, L, S, tabs, A, after):
    D, C, QL, KVL = L['dims']
    rope_cb = (QL + KVL) // HEAD_W
    G, g = {}, {}
    dx, g['ffn2_norm'], G['ffn2_w_gate'], G['ffn2_w_up'], G['ffn2_w_down'] = _ffn_bwd(
        dx, A['ffn2'], S['ffn2_norm'], L['ffn2_w_gate'], L['ffn2_w_up'], L['ffn2_w_down'], after)
    G['w_out'] = _mm("mix_dwout", A['y'], dx, 'tn')
    dy = _mm("mix_dy", dx, L['w_out'], 'nt')
    dyc, dym, dpg, g['gate_bias'] = _gate_bwd(A['pg'], S['gate_bias'], A['yc'], A['ym'], dy)
    G['w_conv_out'] = _mm("conv_dwout", dyc, A['c'], 'tn')
    dc = _mm("conv_dc", dyc, L['w_conv_out'], 'nn')
    dpc, dcw, g['conv_b'], g['conv_ln_g'], g['conv_ln_b'] = _conv_bwd(
        A['pc'], A['z'], dc, A['conv_w'], _row(S['conv_ln_g']), _row(S['conv_ln_b']))
    g['conv_w'] = dcw[:CONV_WIDTH]
    d_wmo = _mm("mla_dwout", dym, A['o'], 'tn')
    G['w_mla_out'] = d_wmo.reshape(D, N_HEADS, HEAD_W)[:, :, :V_DIM].reshape(D, N_HEADS * V_DIM)
    do = _mm("mla_do", dym, L['w_mo'], 'nn', out_dtype=_MXU_DTYPE)
    dqn, dkn, dv = _attn_bwd(A['qn'], A['kn'], A['v'], A['o'], do, A['lse'])
    dq_raw, dk_raw, drope, dgq, dgk = _qk_prep_bwd(A['q_raw'], A['k_raw'], A['pl'], rope_cb, *tabs, A['gq'], A['gk'], dqn, dkn)
    g['q_norm'], g['k_norm'] = dgq[:, :QK_DIM], dgk[:, :QK_DIM]
    G['w_uq'] = _unpad_heads(_mm("up_dwq", dq_raw, A['cqn'], 'tn'), QK_DIM).reshape(N_HEADS * QK_DIM, QL)
    d_wk = _unpad_heads(_mm("up_dwkv", dk_raw, A['ckvn'], 'tn'), NOPE_DIM)
    d_wv = _unpad_heads(_mm("up_dwkv", dv, A['ckvn'], 'tn'), V_DIM)
    G['w_ukv'] = jnp.concatenate([d_wk, d_wv], axis=1).reshape(N_HEADS * (NOPE_DIM + V_DIM), KVL)
    dcq = _mm("up_dcq", dq_raw, L['w_q'], 'nn')
    dckv = _mm("up_dckv_k", dk_raw, L['w_k'], 'nn')
    dckv = _mm("up_dckv_v", dv, L['w_v'], 'nn', res=dckv)
    dpl, g['cq_norm'], g['ckv_norm'] = _lat_norm_bwd(A['pl'], _row(S['cq_norm']), _row(S['ckv_norm']), dcq, dckv, drope, QL, KVL)
    d_wc = _mm("proj_dwc", dpc, A['h'], 'tn')
    d_wl = _mm("proj_dwl", dpl, A['h'], 'tn')
    d_wg = _mm("proj_dwg", dpg, A['h'], 'tn')
    ql = QL + KVL
    G['w_in'] = jnp.concatenate([d_wc, d_wl[:ql], d_wl[ql + NOPE_DIM:ql + NOPE_DIM + ROPE_DIM], d_wg], axis=0)
    dh = _mm("proj_dh_c", dpc, L['w_c'], 'nn')
    dh = _mm("proj_dh_l", dpl, L['w_l'], 'nn', res=dh)
    dh = _mm("proj_dh_g", dpg, L['w_g'], 'nn', res=dh)
    dx, g['mix_norm'] = _rms_bwd(A['x1'], _row(S['mix_norm']), dh, dx)
    return dx, G, g


def kernel(x, positions, ffn1_norm, ffn1_w_gate, ffn1_w_up, ffn1_w_down, mix_norm, w_in, gate_bias, conv_w, conv_b, conv_ln_g, conv_ln_b, w_conv_out, cq_norm, ckv_norm, w_uq, w_ukv, q_norm, k_norm, w_mla_out, w_out, ffn2_norm, ffn2_w_gate, ffn2_w_up, ffn2_w_down, loss_target, m_ffn1_norm, m_ffn1_w_gate, m_ffn1_w_up, m_ffn1_w_down, m_mix_norm, m_w_in, m_gate_bias, m_conv_w, m_conv_b, m_conv_ln_g, m_conv_ln_b, m_w_conv_out, m_cq_norm, m_ckv_norm, m_w_uq, m_w_ukv, m_q_norm, m_k_norm, m_w_mla_out, m_w_out, m_ffn2_norm, m_ffn2_w_gate, m_ffn2_w_up, m_ffn2_w_down, v_ffn1_norm, v_ffn1_w_gate, v_ffn1_w_up, v_ffn1_w_down, v_mix_norm, v_w_in, v_gate_bias, v_conv_w, v_conv_b, v_conv_ln_g, v_conv_ln_b, v_w_conv_out, v_cq_norm, v_ckv_norm, v_w_uq, v_w_ukv, v_q_norm, v_k_norm, v_w_mla_out, v_w_out, v_ffn2_norm, v_ffn2_w_gate, v_ffn2_w_up, v_ffn2_w_down):
    w = dict(zip(WEIGHTS, (ffn1_norm, ffn1_w_gate, ffn1_w_up, ffn1_w_down, mix_norm, w_in, gate_bias, conv_w, conv_b, conv_ln_g, conv_ln_b, w_conv_out, cq_norm, ckv_norm, w_uq, w_ukv, q_norm, k_norm, w_mla_out, w_out, ffn2_norm, ffn2_w_gate, ffn2_w_up, ffn2_w_down)))
    m = dict(zip(WEIGHTS, (m_ffn1_norm, m_ffn1_w_gate, m_ffn1_w_up, m_ffn1_w_down, m_mix_norm, m_w_in, m_gate_bias, m_conv_w, m_conv_b, m_conv_ln_g, m_conv_ln_b, m_w_conv_out, m_cq_norm, m_ckv_norm, m_w_uq, m_w_ukv, m_q_norm, m_k_norm, m_w_mla_out, m_w_out, m_ffn2_norm, m_ffn2_w_gate, m_ffn2_w_up, m_ffn2_w_down)))
    v = dict(zip(WEIGHTS, (v_ffn1_norm, v_ffn1_w_gate, v_ffn1_w_up, v_ffn1_w_down, v_mix_norm, v_w_in, v_gate_bias, v_conv_w, v_conv_b, v_conv_ln_g, v_conv_ln_b, v_w_conv_out, v_cq_norm, v_ckv_norm, v_w_uq, v_w_ukv, v_q_norm, v_k_norm, v_w_mla_out, v_w_out, v_ffn2_norm, v_ffn2_w_gate, v_ffn2_w_up, v_ffn2_w_down)))
    depth = ffn1_norm.shape[0]
    T, D = x.shape[1], x.shape[2]
    xs = x.reshape(T, D)
    target = loss_target.reshape(T, D)
    tabs = _rope_tables(positions, T)
    me = 4 * lax.axis_index("x") + 2 * lax.axis_index("y") + lax.axis_index("c")

    big = list(BIG)
    packed = []
    for l in range(depth):
        p, spans = _pack_rows([(w[n][l].T if BIG[n] else w[n][l]).astype(_WIRE_DTYPE) for n in big])
        packed.append(p)
    gathered = {0: _all_gather("gather_weights", packed[0])}
    in_flight = {l: _exchange_start(f"gather_start_{l}", packed[l], _own_slot(packed[l], me), False)
                 for l in range(1, depth)}
    small_names = [(l, n) for l in range(depth) for n in SMALL_SHARDED]
    small_packed, small_spans = _pack_flat([w[n][l].T for l, n in small_names])
    small_gathered = _all_gather("gather_small", small_packed)
    for st in in_flight.values():
        xs, _ = lax.optimization_barrier((xs, st[4]))

    def layer_weights(l):
        W = {}
        for n, span in zip(big, spans):
            sh = _unpack_rows(gathered[l], span, (N_DEV,))
            W[n] = sh.reshape(N_DEV * sh.shape[1], sh.shape[2])
        S = {n: w[n][l] for n in REPLICATED}
        for (ll, n), span in zip(small_names, small_spans):
            if ll == l:
                sh = _unpack_flat(small_gathered, span, (N_DEV,))
                S[n] = sh.reshape(N_DEV * sh.shape[1], sh.shape[2]).T
        return _layer_weights(W), S

    layers, acts = [], []
    hcur = xs
    for l in range(depth):
        if l > 0:
            send_sems, recv_sems, src, land, _ = in_flight[l]
            gathered[l] = _exchange_wait(f"gather_wait_{l}", send_sems, recv_sems, src, land, hcur, False)
        layers.append(layer_weights(l))
        hcur, saved = _layer_fwd(hcur, *layers[l], tabs)
        acts.append(saved)
    dx, loss_part = _loss_grad(hcur, target)

    stage_names = [[n for n in big if n.startswith('ffn1_')], [n for n in big if not n.startswith('ffn1_')]]
    small_grads, reduced, gspans, in_flight = [{} for _ in range(depth)], {}, {}, {}
    after = jnp.zeros((_SUBLANES, _LANES), F32)
    for l in reversed(range(depth)):
        for stage in (1, 0):
            if stage:
                dx, G, g = _layer_bwd_mix(dx, *layers[l], tabs, acts[l], after)
            else:
                dx, G, g = _layer_bwd_ffn1(dx, *layers[l], acts[l], after)
            small_grads[l].update(g)
            blocks = [G[n].reshape(N_DEV // 2, 2, -1, G[n].shape[1]).astype(_WIRE_DTYPE) for n in stage_names[stage]]
            if (l, stage) != (0, 0):
                send, gspans[stage] = _pack_rows([b.reshape((N_DEV,) + b.shape[2:]) for b in blocks], (N_DEV,))
                own = lax.dynamic_index_in_dim(send, me, 0, keepdims=False)
                in_flight[l, stage] = _exchange_start(f"scatter_start_{l}_{stage}", send, _own_slot(own, me), True)
                after = in_flight[l, stage][4]
            else:
                send, gspans[stage] = _pack_rows([jnp.swapaxes(b, 0, 1) for b in blocks], (2, N_DEV // 2))
                pair = _pair_sum("pair_sum_grads", send, _pair_exchange("pair_exchange_grads", send))
                reduced[l, stage] = _sum_blocks("sum_grads", _chip_all_to_all("scatter_grads", pair))
    for (l, stage), (send_sems, recv_sems, src, land, _) in in_flight.items():
        got = _exchange_wait(f"scatter_wait_{l}_{stage}", send_sems, recv_sems, src, land, dx, True)
        reduced[l, stage] = _sum_blocks("sum_grads_direct", got)
    small_list = [(l, n) for l in range(depth) for n in REPLICATED + SMALL_SHARDED]
    small_send, sspans = _pack_flat([small_grads[l][n] for l, n in small_list] + [loss_part])
    small_all = _sum_blocks("sum_small", _all_gather("gather_small_grads", small_send))

    loss = _unpack_flat(small_all, sspans[-1])[0, 0]
    grads = {}
    for n in WEIGHTS:
        per_layer = []
        for l in range(depth):
            if n in BIG:
                stage = 0 if n in stage_names[0] else 1
                gt = _unpack_rows(reduced[l, stage], gspans[stage][stage_names[stage].index(n)])
                per_layer.append(gt.T if BIG[n] else gt)
            else:
                gs = _unpack_flat(small_all, sspans[small_list.index((l, n))])
                if n in SMALL_SHARDED:
                    cols = w[n].shape[-1]
                    gs = lax.dynamic_slice_in_dim(gs, me * cols, cols, axis=1)
                per_layer.append(gs.reshape(w[n].shape[1:]))
        grads[n] = jnp.stack(per_layer, axis=0)

    delta, new_m, new_v = {}, {}, {}
    for n in WEIGHTS:
        delta[n], new_m[n], new_v[n] = _adamw(w[n], grads[n], m[n], v[n])
    return (loss, dx.reshape(1, T, D), *[grads[n] for n in WEIGHTS], *[delta[n] for n in WEIGHTS],
            *[new_m[n] for n in WEIGHTS], *[new_v[n] for n in WEIGHTS])
```

```python
import jax
import jax.numpy as jnp
from jax import lax
from jax.experimental import pallas as pl
from jax.experimental.pallas import tpu as pltpu

F32 = jnp.float32
_MXU_DTYPE = jnp.bfloat16
_WIRE_DTYPE = jnp.bfloat16

_LANES = 128
_SUBLANES = 8
_PACKED_ROWS = 16
_V7X_VMEM_BYTES = 64 * 2 ** 20
_VMEM_HEADROOM = 8 * 2 ** 20

N_DEV = 8
EPS = 1e-6
CHUNK = 64
CONV_WIDTH = 31
N_HEADS = 8
NOPE_DIM = 64
ROPE_DIM = 32
QK_DIM = NOPE_DIM + ROPE_DIM
V_DIM = 64
HEAD_W = _LANES
ROPE_THETA = 10000.0
HALO = 32

ADAM_LR = 0.001
ADAM_B1 = 0.9
ADAM_B2 = 0.999
ADAM_EPS = 1e-08
ADAM_WD = 0.01
ADAM_STEP = 10

_ROW_TILE = 512
_MM_TILE = 512
_FFN_TILE = 512
_MM_TILE_N = 1408
_MM_TILE_K = 1024
_TN_TILE_M = 1408
_ATT_TILE = 512
_ATT_ROWS = 32
_ATT_HEADS = 2
_PACK_W = 1024
_PACK_ROW_MULT = 128

WEIGHTS = ['ffn1_norm', 'ffn1_w_gate', 'ffn1_w_up', 'ffn1_w_down', 'mix_norm', 'w_in', 'gate_bias',
           'conv_w', 'conv_b', 'conv_ln_g', 'conv_ln_b', 'w_conv_out', 'cq_norm', 'ckv_norm', 'w_uq',
           'w_ukv', 'q_norm', 'k_norm', 'w_mla_out', 'w_out', 'ffn2_norm', 'ffn2_w_gate', 'ffn2_w_up',
           'ffn2_w_down']
BIG = {'ffn1_w_gate': True, 'ffn1_w_up': True, 'ffn1_w_down': False, 'w_in': True, 'w_conv_out': True,
       'w_uq': True, 'w_ukv': True, 'w_mla_out': True, 'w_out': False, 'ffn2_w_gate': True,
       'ffn2_w_up': True, 'ffn2_w_down': False}
SMALL_SHARDED = ['conv_w', 'gate_bias']
REPLICATED = ['ffn1_norm', 'mix_norm', 'conv_b', 'conv_ln_g', 'conv_ln_b', 'cq_norm', 'ckv_norm',
              'q_norm', 'k_norm', 'ffn2_norm']


def _tile(n, target, mult=_LANES):
    if n <= target:
        return n
    for d in range(target - target % mult, 0, -mult):
        if n % d == 0:
            return d
    return n


def _nbytes(shape, dtype):
    n = 1
    for s in shape:
        n *= s
    return n * jnp.dtype(dtype).itemsize


def _params(dims, block_bytes, extra_bytes=0):
    need = 2 * block_bytes + extra_bytes + _VMEM_HEADROOM
    assert need <= _V7X_VMEM_BYTES, (need, dims)
    return pltpu.CompilerParams(dimension_semantics=dims, vmem_limit_bytes=_V7X_VMEM_BYTES - _VMEM_HEADROOM)


def _mxu(x, scale=None):
    if scale is not None:
        x = x * scale
    return x if x.dtype == _MXU_DTYPE else x.astype(_MXU_DTYPE)


def _dot(a, b, mode):
    dims = {'nn': (((1,), (0,)), ((), ())), 'nt': (((1,), (1,)), ((), ())), 'tn': (((0,), (0,)), ((), ()))}[mode]
    return lax.dot_general(a, b, dims, preferred_element_type=F32)


def _colsum(x):
    return jnp.sum(x, axis=0, keepdims=True)


def _sig(x):
    return jax.nn.sigmoid(x)


def _mm(name, a, b, mode, *, out_dtype=F32, res=None, scale=None, a_scale=None, tm=None, tn=None, tk=None):
    if mode == 'nn':
        (M, K), (_, N) = a.shape, b.shape
    elif mode == 'nt':
        (M, K), (N, _) = a.shape, b.shape
    else:
        (K, M), (_, N) = a.shape, b.shape
    tm = _tile(M, tm or (_TN_TILE_M if mode == 'tn' else _MM_TILE))
    tn = _tile(N, tn or _MM_TILE_N)
    tk = _tile(K, tk or (_MM_TILE_K if mode == 'tn' else 4096))
    nk = K // tk
    grid = (M // tm, N // tn, nk)
    a_spec = pl.BlockSpec((tk, tm), lambda i, j, k: (k, i)) if mode == 'tn' else pl.BlockSpec((tm, tk), lambda i, j, k: (i, k))
    b_spec = pl.BlockSpec((tn, tk), lambda i, j, k: (j, k)) if mode == 'nt' else pl.BlockSpec((tk, tn), lambda i, j, k: (k, j))
    o_spec = pl.BlockSpec((tm, tn), lambda i, j, k: (i, j))
    has_res = res is not None

    def body(*refs):
        a_ref, b_ref = refs[0], refs[1]
        res_ref = refs[2] if has_res else None
        o_ref = refs[3] if has_res else refs[2]
        acc_ref = refs[-1] if nk > 1 else None

        def finish(p):
            if scale is not None:
                p = p * scale
            if has_res:
                p = res_ref[...] + p
            o_ref[...] = p.astype(o_ref.dtype)

        p = _dot(_mxu(a_ref[...], a_scale), _mxu(b_ref[...]), mode)
        if nk == 1:
            finish(p)
        else:
            k = pl.program_id(2)

            @pl.when(k == 0)
            def _():
                acc_ref[...] = p

            @pl.when(k > 0)
            def _():
                acc_ref[...] += p

            @pl.when(k == nk - 1)
            def _():
                finish(acc_ref[...])

    blocks = (_nbytes((tm, tk), a.dtype) + _nbytes((tk, tn), b.dtype) + _nbytes((tm, tn), out_dtype)
              + (_nbytes((tm, tn), F32) if has_res else 0))
    extra = ((2 + (nk > 1)) * _nbytes((tm, tn), F32) + (a.dtype != _MXU_DTYPE) * _nbytes((tm, tk), _MXU_DTYPE)
             + (b.dtype != _MXU_DTYPE) * _nbytes((tk, tn), _MXU_DTYPE))
    return pl.pallas_call(
        body, name=name, grid=grid,
        in_specs=[a_spec, b_spec] + ([o_spec] if has_res else []),
        out_specs=o_spec,
        out_shape=jax.ShapeDtypeStruct((M, N), out_dtype),
        scratch_shapes=[pltpu.VMEM((tm, tn), F32)] if nk > 1 else [],
        compiler_params=_params(("parallel", "parallel", "arbitrary"), blocks, extra),
    )(*([a, b] + ([res] if has_res else [])))


def _wgrad(name, a, b, scale=None):
    return _mm(name, a, b, 'tn', out_dtype=_WIRE_DTYPE, scale=scale)


def _ffn_up(h, wg, wu):
    T, D = h.shape
    F = wg.shape[0]
    tm, tn = _tile(T, _FFN_TILE), _tile(F, _MM_TILE_N)

    def body(h_ref, wg_ref, wu_ref, a_ref, b_ref, u_ref):
        hv = h_ref[...]
        a = _dot(hv, wg_ref[...], 'nt')
        b = _dot(hv, wu_ref[...], 'nt')
        a_ref[...] = a
        b_ref[...] = b
        u_ref[...] = (a * _sig(a) * b).astype(u_ref.dtype)

    w_spec = pl.BlockSpec((tn, D), lambda j, i: (j, 0))
    o_spec = pl.BlockSpec((tm, tn), lambda j, i: (i, j))
    blocks = _nbytes((tm, D), h.dtype) + 2 * _nbytes((tn, D), wg.dtype) + 2 * _nbytes((tm, tn), F32) + _nbytes((tm, tn), _MXU_DTYPE)
    return pl.pallas_call(
        body, name="ffn_up", grid=(F // tn, T // tm),
        in_specs=[pl.BlockSpec((tm, D), lambda j, i: (i, 0)), w_spec, w_spec],
        out_specs=[o_spec, o_spec, o_spec],
        out_shape=[jax.ShapeDtypeStruct((T, F), F32), jax.ShapeDtypeStruct((T, F), F32),
                   jax.ShapeDtypeStruct((T, F), _MXU_DTYPE)],
        compiler_params=_params(("parallel", "parallel"), blocks, 4 * _nbytes((tm, tn), F32)),
    )(h, wg, wu)


def _ffn_bwd_up(dx, wd, a, b, after):
    T, D = dx.shape
    F = wd.shape[0]
    tm, tn = _tile(T, _FFN_TILE), _tile(F, _MM_TILE_N)

    def body(dx_ref, wd_ref, a_ref, b_ref, after_ref, da_ref, db_ref):
        du = _dot(_mxu(dx_ref[...], 0.5), wd_ref[...], 'nt')
        av, bv = a_ref[...], b_ref[...]
        s = _sig(av)
        db_ref[...] = (du * av * s).astype(db_ref.dtype)
        da_ref[...] = (du * bv * (s * (1.0 + av * (1.0 - s)))).astype(da_ref.dtype)

    t_spec = pl.BlockSpec((tm, tn), lambda j, i: (i, j))
    blocks = _nbytes((tm, D), F32) + _nbytes((tn, D), wd.dtype) + 2 * _nbytes((tm, tn), F32) + 2 * _nbytes((tm, tn), _MXU_DTYPE)
    return pl.pallas_call(
        body, name="ffn_bwd_up", grid=(F // tn, T // tm),
        in_specs=[pl.BlockSpec((tm, D), lambda j, i: (i, 0)), pl.BlockSpec((tn, D), lambda j, i: (j, 0)), t_spec, t_spec,
                  pl.BlockSpec(memory_space=pl.ANY)],
        out_specs=[t_spec, t_spec],
        out_shape=[jax.ShapeDtypeStruct((T, F), _MXU_DTYPE)] * 2,
        compiler_params=_params(("parallel", "parallel"), blocks, 5 * _nbytes((tm, tn), F32)),
    )(dx, wd, a, b, after)


def _row_spec(tm, w, cb=0):
    return pl.BlockSpec((tm, w), lambda i: (i, cb))


def _full_spec(shape):
    return pl.BlockSpec(shape, lambda i: (0,) * len(shape))


def _init_acc(refs):
    @pl.when(pl.program_id(0) == 0)
    def _():
        for r in refs:
            r[...] = jnp.zeros_like(r)


def _rms_fwd(x, g):
    T, D = x.shape
    tm = _tile(T, _ROW_TILE, _PACKED_ROWS)

    def body(x_ref, g_ref, h_ref):
        xv = x_ref[...]
        r = lax.rsqrt(jnp.mean(xv * xv, axis=-1, keepdims=True) + EPS)
        h_ref[...] = (xv * r * g_ref[...]).astype(h_ref.dtype)

    return pl.pallas_call(
        body, name="rms_fwd", grid=(T // tm,),
        in_specs=[_row_spec(tm, D), _full_spec((1, D))], out_specs=_row_spec(tm, D),
        out_shape=jax.ShapeDtypeStruct((T, D), _MXU_DTYPE),
        compiler_params=_params(("parallel",), 2 * _nbytes((tm, D), F32), 4 * _nbytes((tm, D), F32)),
    )(x, g)


def _rms_bwd(x, g, dh, dres):
    T, D = x.shape
    tm = _tile(T, _ROW_TILE, _SUBLANES)

    def body(x_ref, g_ref, dh_ref, dres_ref, dx_ref, dg_ref):
        _init_acc([dg_ref])
        xv, dhv = x_ref[...], dh_ref[...]
        r = lax.rsqrt(jnp.mean(xv * xv, axis=-1, keepdims=True) + EPS)
        xh = xv * r
        dxh = dhv * g_ref[...]
        dx_ref[...] = dres_ref[...] + r * (dxh - xh * jnp.mean(dxh * xh, axis=-1, keepdims=True))
        dg_ref[...] += _colsum(dhv * xh)

    return pl.pallas_call(
        body, name="rms_bwd", grid=(T // tm,),
        in_specs=[_row_spec(tm, D), _full_spec((1, D)), _row_spec(tm, D), _row_spec(tm, D)],
        out_specs=[_row_spec(tm, D), _full_spec((1, D))],
        out_shape=[jax.ShapeDtypeStruct((T, D), F32), jax.ShapeDtypeStruct((1, D), F32)],
        compiler_params=_params(("arbitrary",), 4 * _nbytes((tm, D), F32), 6 * _nbytes((tm, D), F32)),
    )(x, g, dh, dres)


def _loss_grad(y, target):
    T, D = y.shape
    tm = _tile(T, _ROW_TILE, _SUBLANES)

    def body(y_ref, t_ref, dy_ref, l_ref):
        _init_acc([l_ref])
        err = y_ref[...] - t_ref[...]
        dy_ref[...] = err * (1.0 / D)
        per_row = jnp.sum(err * err, axis=-1, keepdims=True) * (0.5 / D)
        l_ref[...] += jnp.sum(per_row, axis=0, keepdims=True)

    return pl.pallas_call(
        body, name="loss_grad", grid=(T // tm,),
        in_specs=[_row_spec(tm, D), _row_spec(tm, D)],
        out_specs=[_row_spec(tm, D), _full_spec((1, _LANES))],
        out_shape=[jax.ShapeDtypeStruct((T, D), F32), jax.ShapeDtypeStruct((1, _LANES), F32)],
        compiler_params=_params(("arbitrary",), 3 * _nbytes((tm, D), F32), 3 * _nbytes((tm, D), F32)),
    )(y, target)


def _glu(pc, C):
    a, gate = pc[:, :C], pc[:, C:]
    s = _sig(gate)
    return a, s, a * s


def _conv_fwd(pc, w, cb, lg, lb):
    T, C2 = pc.shape
    C = C2 // 2
    tm = _tile(T, _ROW_TILE // 2, HALO)
    per = tm // HALO

    def body(pc_ref, ph_ref, w_ref, cb_ref, lg_ref, lb_ref, z_ref, c_ref, ubuf):
        i = pl.program_id(0)
        _, _, u_cur = _glu(pc_ref[...], C)
        _, _, u_prev = _glu(ph_ref[...], C)
        ubuf[0:HALO, :] = jnp.where(i > 0, u_prev, 0.0)
        ubuf[HALO:HALO + tm, :] = u_cur
        acc = jnp.zeros((tm, C), F32)
        for k in range(CONV_WIDTH):
            off = HALO - (CONV_WIDTH - 1) + k
            acc = acc + w_ref[k:k + 1, :] * ubuf[off:off + tm, :]
        z = acc + cb_ref[...]
        z_ref[...] = z
        zc = z - jnp.mean(z, axis=-1, keepdims=True)
        y = zc * lax.rsqrt(jnp.mean(zc * zc, axis=-1, keepdims=True) + EPS) * lg_ref[...] + lb_ref[...]
        c_ref[...] = (y * _sig(y)).astype(c_ref.dtype)

    return pl.pallas_call(
        body, name="conv_fwd", grid=(T // tm,),
        in_specs=[_row_spec(tm, C2), pl.BlockSpec((HALO, C2), lambda i: (jnp.maximum(i * per - 1, 0), 0)),
                  _full_spec((HALO, C)), _full_spec((1, C)), _full_spec((1, C)), _full_spec((1, C))],
        out_specs=[_row_spec(tm, C), _row_spec(tm, C)],
        out_shape=[jax.ShapeDtypeStruct((T, C), F32), jax.ShapeDtypeStruct((T, C), _MXU_DTYPE)],
        scratch_shapes=[pltpu.VMEM((tm + HALO, C), F32)],
        compiler_params=_params(("parallel",), 3 * _nbytes((tm, C2), F32), 8 * _nbytes((tm, C), F32)),
    )(pc, pc, w, cb, lg, lb)


def _conv_bwd(pc, z, dc, w, lg, lb):
    T, C2 = pc.shape
    C = C2 // 2
    tm = _tile(T, _ROW_TILE // 2, HALO)
    per = tm // HALO
    n = T // tm
    last_halo = T // HALO - 1

    def body(pc_ref, ph_ref, z_ref, zn_ref, dc_ref, dcn_ref, w_ref, lg_ref, lb_ref,
             dpc_ref, dw_ref, dcb_ref, dlg_ref, dlb_ref, ubuf, dzbuf):
        i = pl.program_id(0)
        _init_acc([dw_ref, dcb_ref, dlg_ref, dlb_ref])
        g, b = lg_ref[...], lb_ref[...]

        def ln_swish_bwd(zv, dcv):
            zc = zv - jnp.mean(zv, axis=-1, keepdims=True)
            r = lax.rsqrt(jnp.mean(zc * zc, axis=-1, keepdims=True) + EPS)
            zh = zc * r
            y = zh * g + b
            s = _sig(y)
            dy = dcv * (s * (1.0 + y * (1.0 - s)))
            dyg = dy * g
            dz = r * (dyg - jnp.mean(dyg, axis=-1, keepdims=True) - zh * jnp.mean(dyg * zh, axis=-1, keepdims=True))
            return dz, dy, zh

        dz_c, dy_c, zh_c = ln_swish_bwd(z_ref[...], dc_ref[...])
        dz_n, _, _ = ln_swish_bwd(zn_ref[...], dcn_ref[...])
        dlg_ref[...] += _colsum(dy_c * zh_c)
        dlb_ref[...] += _colsum(dy_c)
        dcb_ref[...] += _colsum(dz_c)
        dzbuf[0:tm, :] = dz_c
        dzbuf[tm:tm + HALO, :] = jnp.where(i < n - 1, dz_n, 0.0)
        a, s, u_cur = _glu(pc_ref[...], C)
        _, _, u_prev = _glu(ph_ref[...], C)
        ubuf[0:HALO, :] = jnp.where(i > 0, u_prev, 0.0)
        ubuf[HALO:HALO + tm, :] = u_cur
        du = jnp.zeros((tm, C), F32)
        for k in range(CONV_WIDTH):
            back = CONV_WIDTH - 1 - k
            du = du + w_ref[k:k + 1, :] * dzbuf[back:back + tm, :]
            off = HALO - (CONV_WIDTH - 1) + k
            dw_ref[k:k + 1, :] += _colsum(dz_c * ubuf[off:off + tm, :])
        dpc_ref[:, :C] = (du * s).astype(dpc_ref.dtype)
        dpc_ref[:, C:] = (du * a * s * (1.0 - s)).astype(dpc_ref.dtype)

    nxt = lambda i: (jnp.minimum((i + 1) * per, last_halo), 0)
    vec = _full_spec((1, C))
    return pl.pallas_call(
        body, name="conv_bwd", grid=(n,),
        in_specs=[_row_spec(tm, C2), pl.BlockSpec((HALO, C2), lambda i: (jnp.maximum(i * per - 1, 0), 0)),
                  _row_spec(tm, C), pl.BlockSpec((HALO, C), nxt), _row_spec(tm, C), pl.BlockSpec((HALO, C), nxt),
                  _full_spec((HALO, C)), vec, vec],
        out_specs=[_row_spec(tm, C2), _full_spec((HALO, C)), vec, vec, vec],
        out_shape=[jax.ShapeDtypeStruct((T, C2), _MXU_DTYPE), jax.ShapeDtypeStruct((HALO, C), F32)]
                  + [jax.ShapeDtypeStruct((1, C), F32)] * 3,
        scratch_shapes=[pltpu.VMEM((tm + HALO, C), F32), pltpu.VMEM((tm + HALO, C), F32)],
        compiler_params=_params(("arbitrary",), 5 * _nbytes((tm, C2), F32), 12 * _nbytes((tm, C), F32)),
    )(pc, pc, z, z, dc, dc, w, lg, lb)


def _lat_norm_fwd(pl_, gq, gkv, QL, KVL):
    T, W = pl_.shape
    tm = _tile(T, _ROW_TILE, _PACKED_ROWS)

    def body(p_ref, gq_ref, gkv_ref, cq_ref, ckv_ref):
        def norm(xv, gv):
            return xv * lax.rsqrt(jnp.mean(xv * xv, axis=-1, keepdims=True) + EPS) * gv
        cq_ref[...] = norm(p_ref[:, :QL], gq_ref[...]).astype(cq_ref.dtype)
        ckv_ref[...] = norm(p_ref[:, QL:QL + KVL], gkv_ref[...]).astype(ckv_ref.dtype)

    return pl.pallas_call(
        body, name="lat_norm_fwd", grid=(T // tm,),
        in_specs=[_row_spec(tm, W), _full_spec((1, QL)), _full_spec((1, KVL))],
        out_specs=[_row_spec(tm, QL), _row_spec(tm, KVL)],
        out_shape=[jax.ShapeDtypeStruct((T, QL), _MXU_DTYPE), jax.ShapeDtypeStruct((T, KVL), _MXU_DTYPE)],
        compiler_params=_params(("parallel",), 2 * _nbytes((tm, W), F32), 4 * _nbytes((tm, W), F32)),
    )(pl_, gq, gkv)


def _lat_norm_bwd(pl_, gq, gkv, dcq, dckv, drope, QL, KVL):
    T, W = pl_.shape
    tm = _tile(T, _ROW_TILE, _PACKED_ROWS)

    def body(p_ref, gq_ref, gkv_ref, dcq_ref, dckv_ref, dr_ref, dp_ref, dgq_ref, dgkv_ref):
        _init_acc([dgq_ref, dgkv_ref])

        def bwd(xv, gv, dv, dg_ref):
            r = lax.rsqrt(jnp.mean(xv * xv, axis=-1, keepdims=True) + EPS)
            xh = xv * r
            dxh = dv * gv
            dg_ref[...] += _colsum(dv * xh)
            return r * (dxh - xh * jnp.mean(dxh * xh, axis=-1, keepdims=True))

        dp_ref[:, :QL] = bwd(p_ref[:, :QL], gq_ref[...], dcq_ref[...], dgq_ref).astype(dp_ref.dtype)
        dp_ref[:, QL:QL + KVL] = bwd(p_ref[:, QL:QL + KVL], gkv_ref[...], dckv_ref[...], dgkv_ref).astype(dp_ref.dtype)
        dp_ref[:, QL + KVL:] = dr_ref[...].astype(dp_ref.dtype)

    return pl.pallas_call(
        body, name="lat_norm_bwd", grid=(T // tm,),
        in_specs=[_row_spec(tm, W), _full_spec((1, QL)), _full_spec((1, KVL)), _row_spec(tm, QL), _row_spec(tm, KVL),
                  _row_spec(tm, HEAD_W)],
        out_specs=[_row_spec(tm, W), _full_spec((1, QL)), _full_spec((1, KVL))],
        out_shape=[jax.ShapeDtypeStruct((T, W), _MXU_DTYPE), jax.ShapeDtypeStruct((1, QL), F32),
                   jax.ShapeDtypeStruct((1, KVL), F32)],
        compiler_params=_params(("arbitrary",), 4 * _nbytes((tm, W), F32), 6 * _nbytes((tm, W), F32)),
    )(pl_, gq, gkv, dcq, dckv, drope)


def _norm_rope(xv, gv, ct, sa, sb):
    r = lax.rsqrt(jnp.sum(xv * xv, axis=-1, keepdims=True) * (1.0 / QK_DIM) + EPS)
    xn = xv * r * gv
    half = ROPE_DIM // 2
    return xn * ct + pltpu.roll(xn, HEAD_W - half, 1) * sa + pltpu.roll(xn, half, 1) * sb


def _norm_rope_bwd(xv, gv, ct, sa, sb, dout):
    half = ROPE_DIM // 2
    dxn = dout * ct + pltpu.roll(dout * sa, half, 1) + pltpu.roll(dout * sb, HEAD_W - half, 1)
    r = lax.rsqrt(jnp.sum(xv * xv, axis=-1, keepdims=True) * (1.0 / QK_DIM) + EPS)
    xh = xv * r
    dxh = dxn * gv
    dx = r * (dxh - xh * (jnp.sum(dxh * xh, axis=-1, keepdims=True) * (1.0 / QK_DIM)))
    return dx, _colsum(dxn * xh)


def _qk_prep_fwd(q_raw, k_raw, pl_, rope_cb, ct, sa, sb, gq, gk):
    T, HW = q_raw.shape
    H = HW // HEAD_W
    tm = _tile(T, _ROW_TILE, _PACKED_ROWS)

    def body(q_ref, k_ref, r_ref, ct_ref, sa_ref, sb_ref, gq_ref, gk_ref, qn_ref, kn_ref):
        ctv, sav, sbv, rv = ct_ref[...], sa_ref[...], sb_ref[...], r_ref[...]
        for h in range(H):
            hs = slice(h * HEAD_W, (h + 1) * HEAD_W)
            qn_ref[:, hs] = _norm_rope(q_ref[:, hs], gq_ref[...], ctv, sav, sbv).astype(qn_ref.dtype)
            kn_ref[:, hs] = _norm_rope(k_ref[:, hs] + rv, gk_ref[...], ctv, sav, sbv).astype(kn_ref.dtype)

    tb = _row_spec(tm, HEAD_W)
    gb = _full_spec((1, HEAD_W))
    return pl.pallas_call(
        body, name="qk_prep_fwd", grid=(T // tm,),
        in_specs=[_row_spec(tm, HW), _row_spec(tm, HW), _row_spec(tm, HEAD_W, rope_cb), tb, tb, tb, gb, gb],
        out_specs=[_row_spec(tm, HW), _row_spec(tm, HW)],
        out_shape=[jax.ShapeDtypeStruct((T, HW), _MXU_DTYPE)] * 2,
        compiler_params=_params(("parallel",), 3 * _nbytes((tm, HW), F32) + 4 * _nbytes((tm, HEAD_W), F32), 8 * _nbytes((tm, HEAD_W), F32)),
    )(q_raw, k_raw, pl_, ct, sa, sb, gq, gk)


def _qk_prep_bwd(q_raw, k_raw, pl_, rope_cb, ct, sa, sb, gq, gk, dqn, dkn):
    T, HW = q_raw.shape
    H = HW // HEAD_W
    tm = _tile(T, _ROW_TILE, _PACKED_ROWS)

    def body(q_ref, k_ref, r_ref, ct_ref, sa_ref, sb_ref, gq_ref, gk_ref, dqn_ref, dkn_ref,
             dq_ref, dk_ref, dr_ref, dgq_ref, dgk_ref):
        _init_acc([dgq_ref, dgk_ref])
        ctv, sav, sbv, rv = ct_ref[...], sa_ref[...], sb_ref[...], r_ref[...]
        dr = dgq_sum = dgk_sum = None
        for h in range(H):
            hs = slice(h * HEAD_W, (h + 1) * HEAD_W)
            dq, dgq = _norm_rope_bwd(q_ref[:, hs], gq_ref[...], ctv, sav, sbv, dqn_ref[:, hs])
            dk, dgk = _norm_rope_bwd(k_ref[:, hs] + rv, gk_ref[...], ctv, sav, sbv, dkn_ref[:, hs])
            dq_ref[:, hs] = dq.astype(dq_ref.dtype)
            dk_ref[:, hs] = dk.astype(dk_ref.dtype)
            dr = dk if h == 0 else dr + dk
            dgq_sum = dgq if h == 0 else dgq_sum + dgq
            dgk_sum = dgk if h == 0 else dgk_sum + dgk
        dr_ref[...] = dr
        dgq_ref[...] += dgq_sum
        dgk_ref[...] += dgk_sum

    tb = _row_spec(tm, HEAD_W)
    gb = _full_spec((1, HEAD_W))
    hb = _row_spec(tm, HW)
    return pl.pallas_call(
        body, name="qk_prep_bwd", grid=(T // tm,),
        in_specs=[hb, hb, _row_spec(tm, HEAD_W, rope_cb), tb, tb, tb, gb, gb, hb, hb],
        out_specs=[hb, hb, tb, gb, gb],
        out_shape=[jax.ShapeDtypeStruct((T, HW), _MXU_DTYPE)] * 2 + [jax.ShapeDtypeStruct((T, HEAD_W), F32)]
                  + [jax.ShapeDtypeStruct((1, HEAD_W), F32)] * 2,
        compiler_params=_params(("arbitrary",), 5 * _nbytes((tm, HW), F32) + 5 * _nbytes((tm, HEAD_W), F32), 12 * _nbytes((tm, HEAD_W), F32)),
    )(q_raw, k_raw, pl_, ct, sa, sb, gq, gk, dqn, dkn)


_NEG = -1e30
_LOG2E = 1.4426950408889634
_SCORE_C = QK_DIM ** -0.5 * _LOG2E


def _pieces(t, rb, diag):
    assert CHUNK % rb == 0 and rb % _SUBLANES == 0
    out = []
    for r in range(t // rb):
        lo = (r * rb // CHUNK) * CHUNK if diag else 0
        for c in range(t // _LANES):
            cut = None if (c + 1) * _LANES <= lo else max(lo - c * _LANES, 0)
            out.append((slice(r * rb, (r + 1) * rb), slice(c * _LANES, (c + 1) * _LANES), c, cut))
    return out


def _groups(x):
    return x.reshape(x.shape[0] // _SUBLANES, _SUBLANES, x.shape[1])


def _all_sublanes(x8, op):
    return jnp.broadcast_to(op(x8, axis=0, keepdims=True), x8.shape)


def _lane_ge(rb, cut):
    return lax.broadcasted_iota(jnp.int32, (rb, _LANES), 1) >= cut


def _attn_fwd(q, k, v):
    T, HW = q.shape
    H = HW // HEAD_W
    t = _tile(T, _ATT_TILE)
    rb = min(_ATT_ROWS, t)
    nc = t // _LANES

    hp = _ATT_HEADS if H % _ATT_HEADS == 0 else 1
    W = hp * HEAD_W

    def body(q_ref, k_ref, v_ref, o_ref, lse_ref, s_s, p_s, m_s, l_s, acc_s):
        i = pl.program_id(1)
        m_s[...] = jnp.full_like(m_s, _NEG)
        l_s[...] = jnp.zeros_like(l_s)
        acc_s[...] = jnp.zeros_like(acc_s)

        def head_step(hh, rows, diag):
            hs = slice(hh * HEAD_W, (hh + 1) * HEAD_W)
            s_s[hh] = _dot(k_ref[rows, hs], q_ref[:, hs], 'nt')
            pieces = _pieces(t, rb, diag)

            def scores(rs, cs, cut):
                sb = s_s[hh, rs, cs]
                return jnp.where(_lane_ge(rb, cut), sb, _NEG) if cut else sb

            mx = [None] * nc
            for rs, cs, c, cut in pieces:
                if cut is not None:
                    g = jnp.max(_groups(scores(rs, cs, cut)), axis=0)
                    mx[c] = g if mx[c] is None else jnp.maximum(mx[c], g)
            m_new, alpha = [], []
            for c in range(nc):
                cs = slice(c * _LANES, (c + 1) * _LANES)
                m_prev = m_s[hh, :, cs]
                m_new.append(jnp.maximum(m_prev, _all_sublanes(mx[c], jnp.max)))
                alpha.append(jnp.exp2((m_prev - m_new[c]) * _SCORE_C))
                m_s[hh, :, cs] = m_new[c]
            lsum = [jnp.zeros((_SUBLANES, _LANES), F32)] * nc
            for rs, cs, c, cut in pieces:
                if cut is None:
                    p_s[hh, rs, cs] = jnp.zeros((rb, _LANES), p_s.dtype)
                    continue
                p = jnp.exp2((scores(rs, cs, cut) - jnp.tile(m_new[c], (rb // _SUBLANES, 1))) * _SCORE_C)
                lsum[c] = lsum[c] + jnp.sum(_groups(p), axis=0)
                p_s[hh, rs, cs] = p.astype(p_s.dtype)
            for c in range(nc):
                cs = slice(c * _LANES, (c + 1) * _LANES)
                l_s[hh, :, cs] = alpha[c] * l_s[hh, :, cs] + _all_sublanes(lsum[c], jnp.sum)
            a = jnp.tile(jnp.concatenate(alpha, axis=1), (HEAD_W // _SUBLANES, 1))
            acc_s[hh] = a * acc_s[hh] + _dot(v_ref[rows, hs], p_s[hh], 'tn')

        def step(j, diag):
            rows = pl.ds(pl.multiple_of(j * t, t), t)
            for hh in range(hp):
                head_step(hh, rows, diag)

        def loop_body(j, carry):
            step(j, False)
            return carry

        lax.fori_loop(0, i, loop_body, 0)
        step(i, True)
        for hh in range(hp):
            l = l_s[hh]
            o_t = acc_s[hh] / jnp.tile(l, (HEAD_W // _SUBLANES, 1))
            o_ref[:, hh * HEAD_W:(hh + 1) * HEAD_W] = o_t.T.astype(o_ref.dtype)
            lse_ref[hh * _SUBLANES:(hh + 1) * _SUBLANES, :] = m_s[hh] * _SCORE_C + jnp.log(l) * _LOG2E

    qb = pl.BlockSpec((t, W), lambda h, i: (i, h))
    kb = pl.BlockSpec((T, W), lambda h, i: (0, h))
    st = pltpu.VMEM((hp, _SUBLANES, t), F32)
    return pl.pallas_call(
        body, name="attn_fwd", grid=(H // hp, T // t),
        in_specs=[qb, kb, kb], out_specs=[qb, pl.BlockSpec((hp * _SUBLANES, t), lambda h, i: (h, i))],
        out_shape=[jax.ShapeDtypeStruct((T, HW), _MXU_DTYPE), jax.ShapeDtypeStruct((H * _SUBLANES, T), F32)],
        scratch_shapes=[pltpu.VMEM((hp, t, t), F32), pltpu.VMEM((hp, t, t), _MXU_DTYPE), st, st, pltpu.VMEM((hp, HEAD_W, t), F32)],
        compiler_params=_params(("parallel", "arbitrary"), 2 * _nbytes((T, W), _MXU_DTYPE) + 3 * _nbytes((t, W), F32),
                                4 * hp * _nbytes((t, t), F32)),
    )(q, k, v)


def _attn_bwd(q, k, v, o, do, lse2):
    T, HW = q.shape
    H = HW // HEAD_W
    t = _tile(T, _ATT_TILE)
    n = T // t
    rb = min(_ATT_ROWS, t)
    scale = QK_DIM ** -0.5

    hp = _ATT_HEADS if H % _ATT_HEADS == 0 else 1
    W = hp * HEAD_W

    def body(k_ref, v_ref, q_ref, o_ref, do_ref, lse_ref, dq_ref, dk_ref, dv_ref,
             s_s, dp_s, p_s, ds_s, dk_s, dv_s, delta_s, dqt_s):
        j = pl.program_id(1)

        @pl.when(j == 0)
        def _():
            dqt_s[...] = jnp.zeros_like(dqt_s)
            for hh in range(hp):
                hs = slice(hh * HEAD_W, (hh + 1) * HEAD_W)
                for b in range(n):
                    rs = slice(b * t, (b + 1) * t)
                    d = jnp.sum(do_ref[rs, hs].astype(F32) * o_ref[rs, hs].astype(F32), axis=1, keepdims=True)
                    delta_s[hh, :, rs] = jnp.broadcast_to(d, (t, HEAD_W)).T[0:_SUBLANES, :]

        dk_s[...] = jnp.zeros_like(dk_s)
        dv_s[...] = jnp.zeros_like(dv_s)

        def head_step(hh, base, diag):
            hs = slice(hh * HEAD_W, (hh + 1) * HEAD_W)
            kv, vv = k_ref[:, hs], v_ref[:, hs]
            qv, dov = q_ref[pl.ds(base, t), hs], do_ref[pl.ds(base, t), hs]
            s_s[hh] = _dot(kv, qv, 'nt')
            dp_s[hh] = _dot(vv, dov, 'nt')
            for rs, cs, c, cut in _pieces(t, rb, diag):
                if cut is None:
                    p_s[hh, rs, cs] = jnp.zeros((rb, _LANES), p_s.dtype)
                    ds_s[hh, rs, cs] = jnp.zeros((rb, _LANES), ds_s.dtype)
                    continue
                lanes = pl.ds(pl.multiple_of(base + c * _LANES, _LANES), _LANES)
                lse = jnp.tile(lse_ref[hh * _SUBLANES:(hh + 1) * _SUBLANES, lanes], (rb // _SUBLANES, 1))
                dl = jnp.tile(delta_s[hh, :, lanes], (rb // _SUBLANES, 1))
                p = jnp.exp2(s_s[hh, rs, cs] * _SCORE_C - lse)
                if cut:
                    p = jnp.where(_lane_ge(rb, cut), p, 0.0)
                p_s[hh, rs, cs] = p.astype(p_s.dtype)
                ds_s[hh, rs, cs] = (p * (dp_s[hh, rs, cs] - dl)).astype(ds_s.dtype)
            dsv = ds_s[hh]
            dv_s[hh] += _dot(p_s[hh], dov, 'nn')
            dk_s[hh] += _dot(dsv, qv, 'nn')
            dqt_s[hh, :, pl.ds(base, t)] += _dot(kv, dsv, 'tn')

        def step(i, diag):
            base = pl.multiple_of(i * t, t)
            for hh in range(hp):
                head_step(hh, base, diag)

        def loop_body(i, carry):
            step(i, False)
            return carry

        step(j, True)
        lax.fori_loop(j + 1, n, loop_body, 0)
        for hh in range(hp):
            hs = slice(hh * HEAD_W, (hh + 1) * HEAD_W)
            dk_ref[:, hs] = dk_s[hh] * scale
            dv_ref[:, hs] = dv_s[hh].astype(dv_ref.dtype)

        @pl.when(j == n - 1)
        def _():
            for hh in range(hp):
                for b in range(n):
                    rs = slice(b * t, (b + 1) * t)
                    dq_ref[rs, hh * HEAD_W:(hh + 1) * HEAD_W] = dqt_s[hh, :, rs].T * scale

    jb = pl.BlockSpec((t, W), lambda h, j: (j, h))
    fb = pl.BlockSpec((T, W), lambda h, j: (0, h))
    sc = pltpu.VMEM((hp, t, HEAD_W), F32)
    return pl.pallas_call(
        body, name="attn_bwd", grid=(H // hp, n),
        in_specs=[jb, jb, fb, fb, fb, pl.BlockSpec((hp * _SUBLANES, T), lambda h, j: (h, 0))], out_specs=[fb, jb, jb],
        out_shape=[jax.ShapeDtypeStruct((T, HW), F32), jax.ShapeDtypeStruct((T, HW), F32),
                   jax.ShapeDtypeStruct((T, HW), _MXU_DTYPE)],
        scratch_shapes=[pltpu.VMEM((hp, t, t), F32), pltpu.VMEM((hp, t, t), F32), pltpu.VMEM((hp, t, t), _MXU_DTYPE),
                        pltpu.VMEM((hp, t, t), _MXU_DTYPE), sc, sc, pltpu.VMEM((hp, _SUBLANES, T), F32),
                        pltpu.VMEM((hp, HEAD_W, T), F32)],
        compiler_params=_params(("parallel", "arbitrary"), 3 * _nbytes((T, W), _MXU_DTYPE) + _nbytes((T, W), F32)
                                + 4 * _nbytes((t, W), F32), hp * (4 * _nbytes((t, t), F32) + _nbytes((T, HEAD_W), F32))),
    )(k, v, q, o, do, lse2)


def _gate_fwd(pg, gb, yc, ym):
    T, D = yc.shape
    tm = _tile(T, _ROW_TILE, _PACKED_ROWS)

    def body(g0_ref, g1_ref, gb_ref, yc_ref, ym_ref, y_ref):
        s0 = _sig(g0_ref[...] + gb_ref[0:1, :])
        s1 = _sig(g1_ref[...] + gb_ref[1:2, :])
        y_ref[...] = (s0 * yc_ref[...] + s1 * ym_ref[...]).astype(y_ref.dtype)

    return pl.pallas_call(
        body, name="gate_fwd", grid=(T // tm,),
        in_specs=[_row_spec(tm, D, 0), _row_spec(tm, D, 1), _full_spec((2, D)), _row_spec(tm, D), _row_spec(tm, D)],
        out_specs=_row_spec(tm, D),
        out_shape=jax.ShapeDtypeStruct((T, D), _MXU_DTYPE),
        compiler_params=_params(("parallel",), 5 * _nbytes((tm, D), F32), 4 * _nbytes((tm, D), F32)),
    )(pg, pg, gb, yc, ym)


def _gate_bwd(pg, gb, yc, ym, dy):
    T, D = yc.shape
    tm = _tile(T, _ROW_TILE, _PACKED_ROWS)

    def body(g0_ref, g1_ref, gb_ref, yc_ref, ym_ref, dy_ref, dyc_ref, dym_ref, dpg_ref, dgb_ref):
        _init_acc([dgb_ref])
        dyv = dy_ref[...]
        s0 = _sig(g0_ref[...] + gb_ref[0:1, :])
        s1 = _sig(g1_ref[...] + gb_ref[1:2, :])
        dyc_ref[...] = (dyv * s0).astype(dyc_ref.dtype)
        dym_ref[...] = (dyv * s1).astype(dym_ref.dtype)
        d0 = dyv * yc_ref[...] * s0 * (1.0 - s0)
        d1 = dyv * ym_ref[...] * s1 * (1.0 - s1)
        dpg_ref[:, :D] = d0.astype(dpg_ref.dtype)
        dpg_ref[:, D:] = d1.astype(dpg_ref.dtype)
        dgb_ref[0:1, :] += _colsum(d0)
        dgb_ref[1:2, :] += _colsum(d1)

    return pl.pallas_call(
        body, name="gate_bwd", grid=(T // tm,),
        in_specs=[_row_spec(tm, D, 0), _row_spec(tm, D, 1), _full_spec((2, D)), _row_spec(tm, D), _row_spec(tm, D),
                  _row_spec(tm, D)],
        out_specs=[_row_spec(tm, D), _row_spec(tm, D), _row_spec(tm, 2 * D), _full_spec((2, D))],
        out_shape=[jax.ShapeDtypeStruct((T, D), _MXU_DTYPE)] * 2 + [jax.ShapeDtypeStruct((T, 2 * D), _MXU_DTYPE),
                                                                   jax.ShapeDtypeStruct((2, D), F32)],
        compiler_params=_params(("arbitrary",), 7 * _nbytes((tm, D), F32), 8 * _nbytes((tm, D), F32)),
    )(pg, pg, gb, yc, ym, dy)


def _place():
    return lax.axis_index("x"), lax.axis_index("y"), lax.axis_index("c")


def _all_gather(name, xs):
    R, C = xs.shape
    MESH = pl.DeviceIdType.MESH

    def body(x_ref, out_ref, send_sems, recv_sems, local_sem):
        x, y, c = _place()
        me, sibling = (x, y, c), (x, y, 1 - c)
        chips = [(1 - x, y), (x, 1 - y), (1 - x, 1 - y)]

        def rows(px, py, pc):
            return out_ref.at[4 * px + 2 * py + pc]

        def copy(k, block, to, src=None):
            return pltpu.make_async_remote_copy(
                src_ref=rows(*block) if src is None else src, dst_ref=rows(*block),
                send_sem=send_sems.at[k], recv_sem=recv_sems.at[k], device_id=to, device_id_type=MESH)

        mine = pltpu.make_async_copy(x_ref, rows(*me), local_sem)
        mine.start()
        first = [copy(0, me, sibling, src=x_ref)]
        first += [copy(1 + j, me, (*chip, c), src=x_ref) for j, chip in enumerate(chips)]
        for cp in first:
            cp.start()
        passed = [copy(4 + j, (*chip, c), sibling) for j, chip in enumerate(chips)]
        for j, chip in enumerate(chips):
            copy(1 + j, (*chip, c), me).wait_recv()
            passed[j].start()
        copy(0, sibling, me).wait_recv()
        for j, chip in enumerate(chips):
            copy(4 + j, (*chip, 1 - c), me).wait_recv()
        for cp in first + passed:
            cp.wait_send()
        mine.wait()

    return pl.pallas_call(
        body, name=name,
        out_shape=jax.ShapeDtypeStruct((N_DEV, R, C), xs.dtype),
        in_specs=[pl.BlockSpec(memory_space=pl.ANY)], out_specs=pl.BlockSpec(memory_space=pl.ANY),
        scratch_shapes=[pltpu.SemaphoreType.DMA((7,)), pltpu.SemaphoreType.DMA((7,)), pltpu.SemaphoreType.DMA],
    )(xs)


def _pair_exchange(name, xs):
    _, G, R, C = xs.shape

    def body(x_ref, out_ref, send_sem, recv_sem):
        x, y, c = _place()
        cp = pltpu.make_async_remote_copy(
            src_ref=x_ref.at[1 - c], dst_ref=out_ref, send_sem=send_sem, recv_sem=recv_sem,
            device_id=(x, y, 1 - c), device_id_type=pl.DeviceIdType.MESH)
        cp.start()
        cp.wait()

    return pl.pallas_call(
        body, name=name,
        out_shape=jax.ShapeDtypeStruct((G, R, C), xs.dtype),
        in_specs=[pl.BlockSpec(memory_space=pl.ANY)], out_specs=pl.BlockSpec(memory_space=pl.ANY),
        scratch_shapes=[pltpu.SemaphoreType.DMA, pltpu.SemaphoreType.DMA],
    )(xs)


def _pair_sum(name, xs, got):
    _, G, R, C = xs.shape
    tr = _tile(R, _PACK_ROW_MULT, _PACKED_ROWS)

    def body(c_ref, a_ref, b_ref, o_ref):
        o_ref[...] = (a_ref[...].astype(F32) + b_ref[...].astype(F32)).astype(o_ref.dtype)

    blk = pl.BlockSpec((None, tr, C), lambda g, i, c_ref: (g, i, 0))
    return pl.pallas_call(
        body, name=name,
        grid_spec=pltpu.PrefetchScalarGridSpec(
            num_scalar_prefetch=1, grid=(G, R // tr),
            in_specs=[pl.BlockSpec((None, None, tr, C), lambda g, i, c_ref: (c_ref[0], g, i, 0)), blk],
            out_specs=blk),
        out_shape=jax.ShapeDtypeStruct((G, R, C), xs.dtype),
        compiler_params=_params(("parallel", "parallel"), 3 * _nbytes((tr, C), F32), 2 * _nbytes((tr, C), F32)),
    )(lax.axis_index("c").reshape(1).astype(jnp.int32), xs, got)


def _chip_all_to_all(name, xs):
    G, R, C = xs.shape

    def body(x_ref, out_ref, send_sems, recv_sems, local_sem):
        x, y, c = _place()
        me = 2 * x + y
        mine = pltpu.make_async_copy(x_ref.at[me], out_ref.at[me], local_sem)
        mine.start()
        copies = []
        for k in range(1, G):
            px = 1 - x if k & 2 else x
            py = 1 - y if k & 1 else y
            copies.append(pltpu.make_async_remote_copy(
                src_ref=x_ref.at[2 * px + py], dst_ref=out_ref.at[me],
                send_sem=send_sems.at[k - 1], recv_sem=recv_sems.at[k - 1],
                device_id=(px, py, c), device_id_type=pl.DeviceIdType.MESH))
        for cp in copies:
            cp.start()
        for cp in copies:
            cp.wait()
        mine.wait()

    return pl.pallas_call(
        body, name=name,
        out_shape=jax.ShapeDtypeStruct((G, R, C), xs.dtype),
        in_specs=[pl.BlockSpec(memory_space=pl.ANY)], out_specs=pl.BlockSpec(memory_space=pl.ANY),
        scratch_shapes=[pltpu.SemaphoreType.DMA((G - 1,)), pltpu.SemaphoreType.DMA((G - 1,)), pltpu.SemaphoreType.DMA],
    )(xs)


_HBM_SPEC = pl.BlockSpec(memory_space=pltpu.HBM)
_SEM_SPEC = pl.BlockSpec(memory_space=pltpu.SEMAPHORE)
_DATAFLOW = pltpu.SideEffectType.DATAFLOW_SIDE_EFFECTING


def _peers():
    x, y, c = _place()
    out = []
    for k in range(1, N_DEV):
        px = 1 - x if k & 4 else x
        py = 1 - y if k & 2 else y
        pc = 1 - c if k & 1 else c
        out.append((k - 1, (px, py, pc), 4 * px + 2 * py + pc))
    return 4 * x + 2 * y + c, out


def _exchange_copies(x_ref, land_ref, send_sems, recv_sems, scatter):
    me, peers = _peers()
    return [pltpu.make_async_remote_copy(
        src_ref=x_ref.at[idx] if scatter else x_ref, dst_ref=land_ref.at[me],
        send_sem=send_sems.at[k], recv_sem=recv_sems.at[k], device_id=dev, device_id_type=pl.DeviceIdType.MESH)
        for k, dev, idx in peers]


def _exchange_start(name, xs, land, scatter):
    def body(x_ref, land_ref, send_sems, recv_sems, x_thru, land_thru, token):
        for cp in _exchange_copies(x_ref, land_ref, send_sems, recv_sems, scatter):
            cp.start()
        token[...] = jnp.zeros_like(token)

    sems = pltpu.SemaphoreType.DMA((N_DEV - 1,))
    return pl.pallas_call(
        body, name=name,
        out_shape=(sems, sems, pltpu.HBM(xs.shape, xs.dtype), pltpu.HBM(land.shape, land.dtype),
                   jax.ShapeDtypeStruct((_SUBLANES, _LANES), F32)),
        in_specs=(_HBM_SPEC, _HBM_SPEC),
        out_specs=(_SEM_SPEC, _SEM_SPEC, _HBM_SPEC, _HBM_SPEC, pl.BlockSpec(memory_space=pltpu.VMEM)),
        input_output_aliases={0: 2, 1: 3},
        compiler_params=pltpu.CompilerParams(has_side_effects=_DATAFLOW),
    )(pltpu.with_memory_space_constraint(xs, pltpu.HBM), pltpu.with_memory_space_constraint(land, pltpu.HBM))


def _exchange_wait(name, send_sems, recv_sems, xs, land, after, scatter):
    def body(x_ref, land_ref, send_sems, recv_sems, after_ref, x_dead, got_ref):
        for cp in _exchange_copies(x_ref, land_ref, send_sems, recv_sems, scatter):
            cp.wait_send()
            cp.wait_recv()

    return pl.pallas_call(
        body, name=name,
        out_shape=(pltpu.HBM(xs.shape, xs.dtype), pltpu.HBM(land.shape, land.dtype)),
        in_specs=(_HBM_SPEC, _HBM_SPEC, _SEM_SPEC, _SEM_SPEC, pl.BlockSpec(memory_space=pl.ANY)),
        out_specs=(_HBM_SPEC, _HBM_SPEC), input_output_aliases={0: 0, 1: 1},
        compiler_params=pltpu.CompilerParams(has_side_effects=_DATAFLOW),
    )(xs, land, send_sems, recv_sems, after)[1]


def _own_slot(block, me):
    land = lax.empty((N_DEV,) + block.shape, block.dtype)
    return lax.dynamic_update_slice(land, block[None], (me,) + (0,) * block.ndim)


def _sum_blocks(name, parts, scale=None):
    n, R, C = parts.shape
    tr = _tile(R, _PACK_ROW_MULT, _PACKED_ROWS)

    def body(p_ref, o_ref):
        acc = p_ref[0].astype(F32)
        for s in range(1, n):
            acc = acc + p_ref[s].astype(F32)
        if scale is not None:
            acc = acc * scale
        o_ref[...] = acc

    return pl.pallas_call(
        body, name=name, grid=(R // tr,),
        in_specs=[pl.BlockSpec((n, tr, C), lambda i: (0, i, 0))], out_specs=pl.BlockSpec((tr, C), lambda i: (i, 0)),
        out_shape=jax.ShapeDtypeStruct((R, C), F32),
        compiler_params=_params(("parallel",), _nbytes((n, tr, C), parts.dtype) + _nbytes((tr, C), F32), 2 * _nbytes((tr, C), F32)),
    )(parts)


def _adamw(w, g, m, v):
    shape = w.shape
    w2, g2, m2, v2 = (t.reshape(-1, shape[-1]) for t in (w, g, m, v))
    R, C = w2.shape
    tr = _tile(R, _ROW_TILE, _SUBLANES)

    def body(w_ref, g_ref, m_ref, v_ref, d_ref, nm_ref, nv_ref):
        gv = g_ref[...]
        mv = ADAM_B1 * m_ref[...] + (1.0 - ADAM_B1) * gv
        vv = ADAM_B2 * v_ref[...] + (1.0 - ADAM_B2) * (gv * gv)
        m_hat = mv / (1.0 - ADAM_B1 ** ADAM_STEP)
        v_hat = vv / (1.0 - ADAM_B2 ** ADAM_STEP)
        d_ref[...] = -ADAM_LR * (m_hat / (jnp.sqrt(v_hat) + ADAM_EPS) + ADAM_WD * w_ref[...])
        nm_ref[...] = mv
        nv_ref[...] = vv

    spec = pl.BlockSpec((tr, C), lambda i: (i, 0))
    outs = pl.pallas_call(
        body, name="adamw", grid=(R // tr,),
        in_specs=[spec] * 4, out_specs=[spec] * 3,
        out_shape=[jax.ShapeDtypeStruct((R, C), F32)] * 3,
        compiler_params=_params(("parallel",), 7 * _nbytes((tr, C), F32), 4 * _nbytes((tr, C), F32)),
    )(w2, g2, m2, v2)
    return tuple(o.reshape(shape) for o in outs)


def _pack_rows(parts, lead=()):
    out, spans, r0 = [], [], 0
    unit = _PACKED_ROWS * _PACK_W
    for p in parts:
        flat = p.reshape(lead + (-1,))
        size = flat.shape[-1]
        pad = (-size) % unit
        if pad:
            flat = jnp.pad(flat, [(0, 0)] * len(lead) + [(0, pad)])
        rows = (size + pad) // _PACK_W
        out.append(flat.reshape(lead + (rows, _PACK_W)))
        spans.append((r0, rows, p.shape[len(lead):]))
        r0 += rows
    pad = (-r0) % _PACK_ROW_MULT
    if pad:
        out.append(jnp.zeros(lead + (pad, _PACK_W), out[0].dtype))
    return jnp.concatenate(out, axis=len(lead)), spans


def _unpack_rows(packed, span, lead=()):
    r0, rows, shape = span
    size = 1
    for s in shape:
        size *= s
    sl = lax.slice_in_dim(packed, r0, r0 + rows, axis=len(lead))
    return sl.reshape(lead + (-1,))[..., :size].reshape(lead + tuple(shape))


def _pack_flat(parts):
    flat, spans, at = [], [], 0
    for p in parts:
        size = p.size
        padded = size + (-size) % _LANES
        flat.append(jnp.pad(p.reshape(-1), (0, padded - size)))
        spans.append((at, size, p.shape))
        at += padded
    tail = (-at) % (_PACKED_ROWS * _PACK_W)
    if tail:
        flat.append(jnp.zeros((tail,), flat[0].dtype))
    return jnp.concatenate(flat).reshape(-1, _PACK_W), spans


def _unpack_flat(packed, span, lead=()):
    at, size, shape = span
    return lax.slice_in_dim(packed.reshape(lead + (-1,)), at, at + size, axis=len(lead)).reshape(lead + tuple(shape))


def _rope_tables(positions, T):
    inv_freq = ROPE_THETA ** (-jnp.arange(0, ROPE_DIM, 2, dtype=F32) / ROPE_DIM)
    ang = positions.reshape(T, 1).astype(F32) * inv_freq
    cos, sin = jnp.cos(ang), jnp.sin(ang)
    half = ROPE_DIM // 2
    z = lambda n: jnp.zeros((T, n), F32)
    tail = HEAD_W - QK_DIM
    ct = jnp.concatenate([jnp.ones((T, NOPE_DIM), F32), cos, cos, z(tail)], axis=1)
    sa = jnp.concatenate([z(NOPE_DIM), -sin, z(half), z(tail)], axis=1)
    sb = jnp.concatenate([z(NOPE_DIM), z(half), sin, z(tail)], axis=1)
    return ct, sa, sb


def _pad_heads(wt, per_head, keep_from=0, keep=None):
    K = wt.shape[1]
    keep = per_head if keep is None else keep
    w3 = wt.reshape(N_HEADS, per_head, K)[:, keep_from:keep_from + keep]
    return jnp.pad(w3, ((0, 0), (0, HEAD_W - keep), (0, 0))).reshape(N_HEADS * HEAD_W, K)


def _unpad_heads(g, keep):
    return g.reshape(N_HEADS, HEAD_W, g.shape[1])[:, :keep]


def _layer_weights(W):
    D = W['w_out'].shape[1]
    C = W['w_conv_out'].shape[1]
    QL, KVL = W['w_uq'].shape[1], W['w_ukv'].shape[1]
    o1 = 2 * C
    o2, o3 = o1 + QL, o1 + QL + KVL
    o4 = o3 + ROPE_DIM
    win = W['w_in']
    zr = lambda n: jnp.zeros((n, D), win.dtype)
    L = dict(W)
    L['w_c'] = win[:o1]
    L['w_l'] = jnp.concatenate([win[o1:o3], zr(NOPE_DIM), win[o3:o4], zr(HEAD_W - QK_DIM)], axis=0)
    L['w_g'] = win[o4:]
    L['w_q'] = _pad_heads(W['w_uq'], QK_DIM)
    L['w_k'] = _pad_heads(W['w_ukv'], NOPE_DIM + V_DIM, 0, NOPE_DIM)
    L['w_v'] = _pad_heads(W['w_ukv'], NOPE_DIM + V_DIM, NOPE_DIM, V_DIM)
    wmo = W['w_mla_out'].reshape(D, N_HEADS, V_DIM)
    L['w_mo'] = jnp.pad(wmo, ((0, 0), (0, 0), (0, HEAD_W - V_DIM))).reshape(D, N_HEADS * HEAD_W)
    L['dims'] = (D, C, QL, KVL)
    return L


def _row(v, width=None):
    v = v.reshape(1, -1)
    if width is not None and v.shape[1] < width:
        v = jnp.pad(v, ((0, 0), (0, width - v.shape[1])))
    return v


def _ffn_fwd(x, g, wg, wu, wd):
    h = _rms_fwd(x, _row(g))
    a, b, u = _ffn_up(h, wg, wu)
    x_out = _mm("ffn_down", u, wd, 'nn', res=x, scale=0.5)
    return x_out, (x, h, a, b, u)


def _ffn_bwd(dx, saved, g, wg, wu, wd, after):
    x, h, a, b, u = saved
    d_wd = _wgrad("ffn_dwd", u, dx, scale=0.5)
    da, db = _ffn_bwd_up(dx, wd, a, b, after)
    dh = _mm("ffn_dh_g", da, wg, 'nn')
    dh = _mm("ffn_dh_u", db, wu, 'nn', res=dh)
    d_wg = _wgrad("ffn_dwg", da, h)
    d_wu = _wgrad("ffn_dwg", db, h)
    dx_in, dg = _rms_bwd(x, _row(g), dh, dx)
    return dx_in, dg, d_wg, d_wu, d_wd


def _layer_fwd(x, L, S, tabs):
    D, C, QL, KVL = L['dims']
    rope_cb = (QL + KVL) // HEAD_W
    x1, ffn1 = _ffn_fwd(x, S['ffn1_norm'], L['ffn1_w_gate'], L['ffn1_w_up'], L['ffn1_w_down'])
    h = _rms_fwd(x1, _row(S['mix_norm']))
    pc = _mm("proj_c", h, L['w_c'], 'nt')
    pl_ = _mm("proj_l", h, L['w_l'], 'nt')
    pg = _mm("proj_g", h, L['w_g'], 'nt')
    conv_w = jnp.pad(S['conv_w'], ((0, HALO - CONV_WIDTH), (0, 0)))
    z, c = _conv_fwd(pc, conv_w, _row(S['conv_b']), _row(S['conv_ln_g']), _row(S['conv_ln_b']))
    yc = _mm("conv_out", c, L['w_conv_out'], 'nt')
    cqn, ckvn = _lat_norm_fwd(pl_, _row(S['cq_norm']), _row(S['ckv_norm']), QL, KVL)
    q_raw = _mm("up_q", cqn, L['w_q'], 'nt')
    k_raw = _mm("up_kv", ckvn, L['w_k'], 'nt')
    v = _mm("up_kv", ckvn, L['w_v'], 'nt', out_dtype=_MXU_DTYPE)
    gq, gk = _row(S['q_norm'], HEAD_W), _row(S['k_norm'], HEAD_W)
    qn, kn = _qk_prep_fwd(q_raw, k_raw, pl_, rope_cb, *tabs, gq, gk)
    o, lse = _attn_fwd(qn, kn, v)
    ym = _mm("mla_out", o, L['w_mo'], 'nt')
    y = _gate_fwd(pg, S['gate_bias'], yc, ym)
    x2 = _mm("mix_out", y, L['w_out'], 'nn', res=x1)
    x3, ffn2 = _ffn_fwd(x2, S['ffn2_norm'], L['ffn2_w_gate'], L['ffn2_w_up'], L['ffn2_w_down'])
    saved = dict(ffn1=ffn1, ffn2=ffn2, x1=x1, h=h, pc=pc, pl=pl_, pg=pg, z=z, c=c, yc=yc, cqn=cqn, ckvn=ckvn,
                 q_raw=q_raw, k_raw=k_raw, v=v, qn=qn, kn=kn, o=o, lse=lse, ym=ym, y=y, conv_w=conv_w, gq=gq, gk=gk)
    return x3, saved


def _layer_bwd_ffn1(dx, L, S, A, after):
    G, g = {}, {}
    dx, g['ffn1_norm'], G['ffn1_w_gate'], G['ffn1_w_up'], G['ffn1_w_down'] = _ffn_bwd(
        dx, A['ffn1'], S['ffn1_norm'], L['ffn1_w_gate'], L['ffn1_w_up'], L['ffn1_w_down'], after)
    return dx, G, g


def _layer_bwd_mix(dx, L, S, tabs, A, after):
    D, C, QL, KVL = L['dims']
    rope_cb = (QL + KVL) // HEAD_W
    G, g = {}, {}
    dx, g['ffn2_norm'], G['ffn2_w_gate'], G['ffn2_w_up'], G['ffn2_w_down'] = _ffn_bwd(
        dx, A['ffn2'], S['ffn2_norm'], L['ffn2_w_gate'], L['ffn2_w_up'], L['ffn2_w_down'], after)
    G['w_out'] = _wgrad("mix_dwout", A['y'], dx)
    dy = _mm("mix_dy", dx, L['w_out'], 'nt')
    dyc, dym, dpg, g['gate_bias'] = _gate_bwd(A['pg'], S['gate_bias'], A['yc'], A['ym'], dy)
    G['w_conv_out'] = _wgrad("conv_dwout", dyc, A['c'])
    dc = _mm("conv_dc", dyc, L['w_conv_out'], 'nn')
    dpc, dcw, g['conv_b'], g['conv_ln_g'], g['conv_ln_b'] = _conv_bwd(
        A['pc'], A['z'], dc, A['conv_w'], _row(S['conv_ln_g']), _row(S['conv_ln_b']))
    g['conv_w'] = dcw[:CONV_WIDTH]
    d_wmo = _wgrad("mla_dwout", dym, A['o'])
    G['w_mla_out'] = d_wmo.reshape(D, N_HEADS, HEAD_W)[:, :, :V_DIM].reshape(D, N_HEADS * V_DIM)
    do = _mm("mla_do", dym, L['w_mo'], 'nn', out_dtype=_MXU_DTYPE)
    dqn, dkn, dv = _attn_bwd(A['qn'], A['kn'], A['v'], A['o'], do, A['lse'])
    dq_raw, dk_raw, drope, dgq, dgk = _qk_prep_bwd(A['q_raw'], A['k_raw'], A['pl'], rope_cb, *tabs, A['gq'], A['gk'], dqn, dkn)
    g['q_norm'], g['k_norm'] = dgq[:, :QK_DIM], dgk[:, :QK_DIM]
    G['w_uq'] = _unpad_heads(_wgrad("up_dwq", dq_raw, A['cqn']), QK_DIM).reshape(N_HEADS * QK_DIM, QL)
    d_wk = _unpad_heads(_wgrad("up_dwkv", dk_raw, A['ckvn']), NOPE_DIM)
    d_wv = _unpad_heads(_wgrad("up_dwkv", dv, A['ckvn']), V_DIM)
    G['w_ukv'] = jnp.concatenate([d_wk, d_wv], axis=1).reshape(N_HEADS * (NOPE_DIM + V_DIM), KVL)
    dcq = _mm("up_dcq", dq_raw, L['w_q'], 'nn')
    dckv = _mm("up_dckv_k", dk_raw, L['w_k'], 'nn')
    dckv = _mm("up_dckv_v", dv, L['w_v'], 'nn', res=dckv)
    dpl, g['cq_norm'], g['ckv_norm'] = _lat_norm_bwd(A['pl'], _row(S['cq_norm']), _row(S['ckv_norm']), dcq, dckv, drope, QL, KVL)
    d_wc = _wgrad("proj_dwc", dpc, A['h'])
    d_wl = _wgrad("proj_dwl", dpl, A['h'])
    d_wg = _wgrad("proj_dwg", dpg, A['h'])
    ql = QL + KVL
    G['w_in'] = jnp.concatenate([d_wc, d_wl[:ql], d_wl[ql + NOPE_DIM:ql + NOPE_DIM + ROPE_DIM], d_wg], axis=0)
    dh = _mm("proj_dh_c", dpc, L['w_c'], 'nn')
    dh = _mm("proj_dh_l", dpl, L['w_l'], 'nn', res=dh)
    dh = _mm("proj_dh_g", dpg, L['w_g'], 'nn', res=dh)
    dx, g['mix_norm'] = _rms_bwd(A['x1'], _row(S['mix_norm']), dh, dx)
    return dx, G, g


def kernel(x, positions, ffn1_norm, ffn1_w_gate, ffn1_w_up, ffn1_w_down, mix_norm, w_in, gate_bias, conv_w, conv_b, conv_ln_g, conv_ln_b, w_conv_out, cq_norm, ckv_norm, w_uq, w_ukv, q_norm, k_norm, w_mla_out, w_out, ffn2_norm, ffn2_w_gate, ffn2_w_up, ffn2_w_down, loss_target, m_ffn1_norm, m_ffn1_w_gate, m_ffn1_w_up, m_ffn1_w_down, m_mix_norm, m_w_in, m_gate_bias, m_conv_w, m_conv_b, m_conv_ln_g, m_conv_ln_b, m_w_conv_out, m_cq_norm, m_ckv_norm, m_w_uq, m_w_ukv, m_q_norm, m_k_norm, m_w_mla_out, m_w_out, m_ffn2_norm, m_ffn2_w_gate, m_ffn2_w_up, m_ffn2_w_down, v_ffn1_norm, v_ffn1_w_gate, v_ffn1_w_up, v_ffn1_w_down, v_mix_norm, v_w_in, v_gate_bias, v_conv_w, v_conv_b, v_conv_ln_g, v_conv_ln_b, v_w_conv_out, v_cq_norm, v_ckv_norm, v_w_uq, v_w_ukv, v_q_norm, v_k_norm, v_w_mla_out, v_w_out, v_ffn2_norm, v_ffn2_w_gate, v_ffn2_w_up, v_ffn2_w_down):
    w = dict(zip(WEIGHTS, (ffn1_norm, ffn1_w_gate, ffn1_w_up, ffn1_w_down, mix_norm, w_in, gate_bias, conv_w, conv_b, conv_ln_g, conv_ln_b, w_conv_out, cq_norm, ckv_norm, w_uq, w_ukv, q_norm, k_norm, w_mla_out, w_out, ffn2_norm, ffn2_w_gate, ffn2_w_up, ffn2_w_down)))
    m = dict(zip(WEIGHTS, (m_ffn1_norm, m_ffn1_w_gate, m_ffn1_w_up, m_ffn1_w_down, m_mix_norm, m_w_in, m_gate_bias, m_conv_w, m_conv_b, m_conv_ln_g, m_conv_ln_b, m_w_conv_out, m_cq_norm, m_ckv_norm, m_w_uq, m_w_ukv, m_q_norm, m_k_norm, m_w_mla_out, m_w_out, m_ffn2_norm, m_ffn2_w_gate, m_ffn2_w_up, m_ffn2_w_down)))
    v = dict(zip(WEIGHTS, (v_ffn1_norm, v_ffn1_w_gate, v_ffn1_w_up, v_ffn1_w_down, v_mix_norm, v_w_in, v_gate_bias, v_conv_w, v_conv_b, v_conv_ln_g, v_conv_ln_b, v_w_conv_out, v_cq_norm, v_ckv_norm, v_w_uq, v_w_ukv, v_q_norm, v_k_norm, v_w_mla_out, v_w_out, v_ffn2_norm, v_ffn2_w_gate, v_ffn2_w_up, v_ffn2_w_down)))
    depth = ffn1_norm.shape[0]
    T, D = x.shape[1], x.shape[2]
    xs = x.reshape(T, D)
    target = loss_target.reshape(T, D)
    tabs = _rope_tables(positions, T)
    me = 4 * lax.axis_index("x") + 2 * lax.axis_index("y") + lax.axis_index("c")

    big = list(BIG)
    packed = []
    for l in range(depth):
        p, spans = _pack_rows([(w[n][l].T if BIG[n] else w[n][l]).astype(_WIRE_DTYPE) for n in big])
        packed.append(p)
    gathered = {0: _all_gather("gather_weights", packed[0])}
    in_flight = {l: _exchange_start(f"gather_start_{l}", packed[l], _own_slot(packed[l], me), False)
                 for l in range(1, depth)}
    small_names = [(l, n) for l in range(depth) for n in SMALL_SHARDED]
    small_packed, small_spans = _pack_flat([w[n][l].T for l, n in small_names])
    small_gathered = _all_gather("gather_small", small_packed)
    for st in in_flight.values():
        xs, _ = lax.optimization_barrier((xs, st[4]))

    def layer_weights(l):
        W = {}
        for n, span in zip(big, spans):
            sh = _unpack_rows(gathered[l], span, (N_DEV,))
            W[n] = sh.reshape(N_DEV * sh.shape[1], sh.shape[2])
        S = {n: w[n][l] for n in REPLICATED}
        for (ll, n), span in zip(small_names, small_spans):
            if ll == l:
                sh = _unpack_flat(small_gathered, span, (N_DEV,))
                S[n] = sh.reshape(N_DEV * sh.shape[1], sh.shape[2]).T
        return _layer_weights(W), S

    layers, acts = [], []
    hcur = xs
    for l in range(depth):
        if l > 0:
            send_sems, recv_sems, src, land, _ = in_flight[l]
            gathered[l] = _exchange_wait(f"gather_wait_{l}", send_sems, recv_sems, src, land, hcur, False)
        layers.append(layer_weights(l))
        hcur, saved = _layer_fwd(hcur, *layers[l], tabs)
        acts.append(saved)
    dx, loss_part = _loss_grad(hcur, target)

    stage_names = [[n for n in big if n.startswith('ffn1_')], [n for n in big if not n.startswith('ffn1_')]]
    small_grads, reduced, gspans, in_flight = [{} for _ in range(depth)], {}, {}, {}
    after = jnp.zeros((_SUBLANES, _LANES), F32)
    for l in reversed(range(depth)):
        for stage in (1, 0):
            if stage:
                dx, G, g = _layer_bwd_mix(dx, *layers[l], tabs, acts[l], after)
            else:
                dx, G, g = _layer_bwd_ffn1(dx, *layers[l], acts[l], after)
            small_grads[l].update(g)
            blocks = [G[n].reshape(N_DEV // 2, 2, -1, G[n].shape[1]).astype(_WIRE_DTYPE) for n in stage_names[stage]]
            if (l, stage) != (0, 0):
                send, gspans[stage] = _pack_rows([b.reshape((N_DEV,) + b.shape[2:]) for b in blocks], (N_DEV,))
                own = lax.dynamic_index_in_dim(send, me, 0, keepdims=False)
                in_flight[l, stage] = _exchange_start(f"scatter_start_{l}_{stage}", send, _own_slot(own, me), True)
                after = in_flight[l, stage][4]
            else:
                send, gspans[stage] = _pack_rows([jnp.swapaxes(b, 0, 1) for b in blocks], (2, N_DEV // 2))
                pair = _pair_sum("pair_sum_grads", send, _pair_exchange("pair_exchange_grads", send))
                reduced[l, stage] = _sum_blocks("sum_grads", _chip_all_to_all("scatter_grads", pair))
    for (l, stage), (send_sems, recv_sems, src, land, _) in in_flight.items():
        got = _exchange_wait(f"scatter_wait_{l}_{stage}", send_sems, recv_sems, src, land, reduced[0, 0], True)
        reduced[l, stage] = _sum_blocks("sum_grads_direct", got)
    small_list = [(l, n) for l in range(depth) for n in REPLICATED + SMALL_SHARDED]
    small_send, sspans = _pack_flat([small_grads[l][n] for l, n in small_list] + [loss_part])
    small_all = _sum_blocks("sum_small", _all_gather("gather_small_grads", small_send))

    loss = _unpack_flat(small_all, sspans[-1])[0, 0]
    grads = {}
    for n in WEIGHTS:
        per_layer = []
        for l in range(depth):
            if n in BIG:
                stage = 0 if n in stage_names[0] else 1
                gt = _unpack_rows(reduced[l, stage], gspans[stage][stage_names[stage].index(n)])
                per_layer.append(gt.T if BIG[n] else gt)
            else:
                gs = _unpack_flat(small_all, sspans[small_list.index((l, n))])
                if n in SMALL_SHARDED:
                    cols = w[n].shape[-1]
                    gs = lax.dynamic_slice_in_dim(gs, me * cols, cols, axis=1)
                per_layer.append(gs.reshape(w[n].shape[1:]))
        grads[n] = jnp.stack(per_layer, axis=0)

    delta, new_m, new_v = {}, {}, {}
    for n in WEIGHTS:
        delta[n], new_m[n], new_v[n] = _adamw(w[n], grads[n], m[n], v[n])
    return (loss, dx.reshape(1, T, D), *[grads[n] for n in WEIGHTS], *[delta[n] for n in WEIGHTS],
            *[new_m[n] for n in WEIGHTS], *[new_v[n] for n in WEIGHTS])
```

```python
import jax
import jax.numpy as jnp
from jax import lax
from jax.experimental import pallas as pl
from jax.experimental.pallas import tpu as pltpu

F32 = jnp.float32
_MXU_DTYPE = jnp.bfloat16
_WIRE_DTYPE = jnp.bfloat16

_LANES = 128
_SUBLANES = 8
_PACKED_ROWS = 16
_V7X_VMEM_BYTES = 64 * 2 ** 20
_VMEM_HEADROOM = 8 * 2 ** 20

N_DEV = 8
EPS = 1e-6
CHUNK = 64
CONV_WIDTH = 31
N_HEADS = 8
NOPE_DIM = 64
ROPE_DIM = 32
QK_DIM = NOPE_DIM + ROPE_DIM
V_DIM = 64
HEAD_W = _LANES
ROPE_THETA = 10000.0
HALO = 32

ADAM_LR = 0.001
ADAM_B1 = 0.9
ADAM_B2 = 0.999
ADAM_EPS = 1e-08
ADAM_WD = 0.01
ADAM_STEP = 10

_ROW_TILE = 512
_MM_TILE = 512
_FFN_TILE = 512
_MM_TILE_N = 1408
_MM_TILE_K = 1024
_TN_TILE_M = 1408
_ATT_TILE = 512
_ATT_ROWS = 32
_ATT_HEADS = 2
_PACK_W = 1024
_PACK_ROW_MULT = 128

WEIGHTS = ['ffn1_norm', 'ffn1_w_gate', 'ffn1_w_up', 'ffn1_w_down', 'mix_norm', 'w_in', 'gate_bias',
           'conv_w', 'conv_b', 'conv_ln_g', 'conv_ln_b', 'w_conv_out', 'cq_norm', 'ckv_norm', 'w_uq',
           'w_ukv', 'q_norm', 'k_norm', 'w_mla_out', 'w_out', 'ffn2_norm', 'ffn2_w_gate', 'ffn2_w_up',
           'ffn2_w_down']
BIG = {'ffn1_w_gate': True, 'ffn1_w_up': True, 'ffn1_w_down': False, 'w_in': True, 'w_conv_out': True,
       'w_uq': True, 'w_ukv': True, 'w_mla_out': True, 'w_out': False, 'ffn2_w_gate': True,
       'ffn2_w_up': True, 'ffn2_w_down': False}
SMALL_SHARDED = ['conv_w', 'gate_bias']
REPLICATED = ['ffn1_norm', 'mix_norm', 'conv_b', 'conv_ln_g', 'conv_ln_b', 'cq_norm', 'ckv_norm',
              'q_norm', 'k_norm', 'ffn2_norm']


def _tile(n, target, mult=_LANES):
    if n <= target:
        return n
    for d in range(target - target % mult, 0, -mult):
        if n % d == 0:
            return d
    return n


def _nbytes(shape, dtype):
    n = 1
    for s in shape:
        n *= s
    return n * jnp.dtype(dtype).itemsize


def _params(dims, block_bytes, extra_bytes=0):
    need = 2 * block_bytes + extra_bytes + _VMEM_HEADROOM
    assert need <= _V7X_VMEM_BYTES, (need, dims)
    return pltpu.CompilerParams(dimension_semantics=dims, vmem_limit_bytes=_V7X_VMEM_BYTES - _VMEM_HEADROOM)


def _mxu(x, scale=None):
    if scale is not None:
        x = x * scale
    return x if x.dtype == _MXU_DTYPE else x.astype(_MXU_DTYPE)


def _dot(a, b, mode):
    dims = {'nn': (((1,), (0,)), ((), ())), 'nt': (((1,), (1,)), ((), ())), 'tn': (((0,), (0,)), ((), ()))}[mode]
    return lax.dot_general(a, b, dims, preferred_element_type=F32)


def _colsum(x):
    return jnp.sum(x, axis=0, keepdims=True)


def _sig(x):
    return jax.nn.sigmoid(x)


def _mm(name, a, b, mode, *, out_dtype=F32, res=None, scale=None, a_scale=None, tm=None, tn=None, tk=None):
    if mode == 'nn':
        (M, K), (_, N) = a.shape, b.shape
    elif mode == 'nt':
        (M, K), (N, _) = a.shape, b.shape
    else:
        (K, M), (_, N) = a.shape, b.shape
    tm = _tile(M, tm or (_TN_TILE_M if mode == 'tn' else _MM_TILE))
    tn = _tile(N, tn or _MM_TILE_N)
    tk = _tile(K, tk or (_MM_TILE_K if mode == 'tn' else 4096))
    nk = K // tk
    grid = (M // tm, N // tn, nk)
    a_spec = pl.BlockSpec((tk, tm), lambda i, j, k: (k, i)) if mode == 'tn' else pl.BlockSpec((tm, tk), lambda i, j, k: (i, k))
    b_spec = pl.BlockSpec((tn, tk), lambda i, j, k: (j, k)) if mode == 'nt' else pl.BlockSpec((tk, tn), lambda i, j, k: (k, j))
    o_spec = pl.BlockSpec((tm, tn), lambda i, j, k: (i, j))
    has_res = res is not None

    def body(*refs):
        a_ref, b_ref = refs[0], refs[1]
        res_ref = refs[2] if has_res else None
        o_ref = refs[3] if has_res else refs[2]
        acc_ref = refs[-1] if nk > 1 else None

        def finish(p):
            if scale is not None:
                p = p * scale
            if has_res:
                p = res_ref[...] + p
            o_ref[...] = p.astype(o_ref.dtype)

        p = _dot(_mxu(a_ref[...], a_scale), _mxu(b_ref[...]), mode)
        if nk == 1:
            finish(p)
        else:
            k = pl.program_id(2)

            @pl.when(k == 0)
            def _():
                acc_ref[...] = p

            @pl.when(k > 0)
            def _():
                acc_ref[...] += p

            @pl.when(k == nk - 1)
            def _():
                finish(acc_ref[...])

    blocks = (_nbytes((tm, tk), a.dtype) + _nbytes((tk, tn), b.dtype) + _nbytes((tm, tn), out_dtype)
              + (_nbytes((tm, tn), F32) if has_res else 0))
    extra = ((2 + (nk > 1)) * _nbytes((tm, tn), F32) + (a.dtype != _MXU_DTYPE) * _nbytes((tm, tk), _MXU_DTYPE)
             + (b.dtype != _MXU_DTYPE) * _nbytes((tk, tn), _MXU_DTYPE))
    return pl.pallas_call(
        body, name=name, grid=grid,
        in_specs=[a_spec, b_spec] + ([o_spec] if has_res else []),
        out_specs=o_spec,
        out_shape=jax.ShapeDtypeStruct((M, N), out_dtype),
        scratch_shapes=[pltpu.VMEM((tm, tn), F32)] if nk > 1 else [],
        compiler_params=_params(("parallel", "parallel", "arbitrary"), blocks, extra),
    )(*([a, b] + ([res] if has_res else [])))


def _wgrad(name, a, b, scale=None):
    return _mm(name, a, b, 'tn', out_dtype=_WIRE_DTYPE, scale=scale)


def _ffn_up(h, wg, wu):
    T, D = h.shape
    F = wg.shape[0]
    tm, tn = _tile(T, _FFN_TILE), _tile(F, _MM_TILE_N)

    def body(h_ref, wg_ref, wu_ref, a_ref, b_ref, u_ref):
        hv = h_ref[...]
        a = _dot(hv, wg_ref[...], 'nt')
        b = _dot(hv, wu_ref[...], 'nt')
        a_ref[...] = a
        b_ref[...] = b
        u_ref[...] = (a * _sig(a) * b).astype(u_ref.dtype)

    w_spec = pl.BlockSpec((tn, D), lambda j, i: (j, 0))
    o_spec = pl.BlockSpec((tm, tn), lambda j, i: (i, j))
    blocks = _nbytes((tm, D), h.dtype) + 2 * _nbytes((tn, D), wg.dtype) + 2 * _nbytes((tm, tn), F32) + _nbytes((tm, tn), _MXU_DTYPE)
    return pl.pallas_call(
        body, name="ffn_up", grid=(F // tn, T // tm),
        in_specs=[pl.BlockSpec((tm, D), lambda j, i: (i, 0)), w_spec, w_spec],
        out_specs=[o_spec, o_spec, o_spec],
        out_shape=[jax.ShapeDtypeStruct((T, F), F32), jax.ShapeDtypeStruct((T, F), F32),
                   jax.ShapeDtypeStruct((T, F), _MXU_DTYPE)],
        compiler_params=_params(("parallel", "parallel"), blocks, 4 * _nbytes((tm, tn), F32)),
    )(h, wg, wu)


def _ffn_bwd_up(dx, wd, a, b, after):
    T, D = dx.shape
    F = wd.shape[0]
    tm, tn = _tile(T, _FFN_TILE), _tile(F, _MM_TILE_N)

    def body(dx_ref, wd_ref, a_ref, b_ref, after_ref, da_ref, db_ref):
        du = _dot(_mxu(dx_ref[...], 0.5), wd_ref[...], 'nt')
        av, bv = a_ref[...], b_ref[...]
        s = _sig(av)
        db_ref[...] = (du * av * s).astype(db_ref.dtype)
        da_ref[...] = (du * bv * (s * (1.0 + av * (1.0 - s)))).astype(da_ref.dtype)

    t_spec = pl.BlockSpec((tm, tn), lambda j, i: (i, j))
    blocks = _nbytes((tm, D), F32) + _nbytes((tn, D), wd.dtype) + 2 * _nbytes((tm, tn), F32) + 2 * _nbytes((tm, tn), _MXU_DTYPE)
    return pl.pallas_call(
        body, name="ffn_bwd_up", grid=(F // tn, T // tm),
        in_specs=[pl.BlockSpec((tm, D), lambda j, i: (i, 0)), pl.BlockSpec((tn, D), lambda j, i: (j, 0)), t_spec, t_spec,
                  pl.BlockSpec(memory_space=pl.ANY)],
        out_specs=[t_spec, t_spec],
        out_shape=[jax.ShapeDtypeStruct((T, F), _MXU_DTYPE)] * 2,
        compiler_params=_params(("parallel", "parallel"), blocks, 5 * _nbytes((tm, tn), F32)),
    )(dx, wd, a, b, after)


def _row_spec(tm, w, cb=0):
    return pl.BlockSpec((tm, w), lambda i: (i, cb))


def _full_spec(shape):
    return pl.BlockSpec(shape, lambda i: (0,) * len(shape))


def _init_acc(refs):
    @pl.when(pl.program_id(0) == 0)
    def _():
        for r in refs:
            r[...] = jnp.zeros_like(r)


def _rms_fwd(x, g):
    T, D = x.shape
    tm = _tile(T, _ROW_TILE, _PACKED_ROWS)

    def body(x_ref, g_ref, h_ref):
        xv = x_ref[...]
        r = lax.rsqrt(jnp.mean(xv * xv, axis=-1, keepdims=True) + EPS)
        h_ref[...] = (xv * r * g_ref[...]).astype(h_ref.dtype)

    return pl.pallas_call(
        body, name="rms_fwd", grid=(T // tm,),
        in_specs=[_row_spec(tm, D), _full_spec((1, D))], out_specs=_row_spec(tm, D),
        out_shape=jax.ShapeDtypeStruct((T, D), _MXU_DTYPE),
        compiler_params=_params(("parallel",), 2 * _nbytes((tm, D), F32), 4 * _nbytes((tm, D), F32)),
    )(x, g)


def _rms_bwd(x, g, dh, dres):
    T, D = x.shape
    tm = _tile(T, _ROW_TILE, _SUBLANES)

    def body(x_ref, g_ref, dh_ref, dres_ref, dx_ref, dg_ref):
        _init_acc([dg_ref])
        xv, dhv = x_ref[...], dh_ref[...]
        r = lax.rsqrt(jnp.mean(xv * xv, axis=-1, keepdims=True) + EPS)
        xh = xv * r
        dxh = dhv * g_ref[...]
        dx_ref[...] = dres_ref[...] + r * (dxh - xh * jnp.mean(dxh * xh, axis=-1, keepdims=True))
        dg_ref[...] += _colsum(dhv * xh)

    return pl.pallas_call(
        body, name="rms_bwd", grid=(T // tm,),
        in_specs=[_row_spec(tm, D), _full_spec((1, D)), _row_spec(tm, D), _row_spec(tm, D)],
        out_specs=[_row_spec(tm, D), _full_spec((1, D))],
        out_shape=[jax.ShapeDtypeStruct((T, D), F32), jax.ShapeDtypeStruct((1, D), F32)],
        compiler_params=_params(("arbitrary",), 4 * _nbytes((tm, D), F32), 6 * _nbytes((tm, D), F32)),
    )(x, g, dh, dres)


def _loss_grad(y, target):
    T, D = y.shape
    tm = _tile(T, _ROW_TILE, _SUBLANES)

    def body(y_ref, t_ref, dy_ref, l_ref):
        _init_acc([l_ref])
        err = y_ref[...] - t_ref[...]
        dy_ref[...] = err * (1.0 / D)
        per_row = jnp.sum(err * err, axis=-1, keepdims=True) * (0.5 / D)
        l_ref[...] += jnp.sum(per_row, axis=0, keepdims=True)

    return pl.pallas_call(
        body, name="loss_grad", grid=(T // tm,),
        in_specs=[_row_spec(tm, D), _row_spec(tm, D)],
        out_specs=[_row_spec(tm, D), _full_spec((1, _LANES))],
        out_shape=[jax.ShapeDtypeStruct((T, D), F32), jax.ShapeDtypeStruct((1, _LANES), F32)],
        compiler_params=_params(("arbitrary",), 3 * _nbytes((tm, D), F32), 3 * _nbytes((tm, D), F32)),
    )(y, target)


def _glu(pc, C):
    a, gate = pc[:, :C], pc[:, C:]
    s = _sig(gate)
    return a, s, a * s


def _conv_fwd(pc, w, cb, lg, lb):
    T, C2 = pc.shape
    C = C2 // 2
    tm = _tile(T, _ROW_TILE // 2, HALO)
    per = tm // HALO

    def body(pc_ref, ph_ref, w_ref, cb_ref, lg_ref, lb_ref, z_ref, c_ref, ubuf):
        i = pl.program_id(0)
        _, _, u_cur = _glu(pc_ref[...], C)
        _, _, u_prev = _glu(ph_ref[...], C)
        ubuf[0:HALO, :] = jnp.where(i > 0, u_prev, 0.0)
        ubuf[HALO:HALO + tm, :] = u_cur
        acc = jnp.zeros((tm, C), F32)
        for k in range(CONV_WIDTH):
            off = HALO - (CONV_WIDTH - 1) + k
            acc = acc + w_ref[k:k + 1, :] * ubuf[off:off + tm, :]
        z = acc + cb_ref[...]
        z_ref[...] = z
        zc = z - jnp.mean(z, axis=-1, keepdims=True)
        y = zc * lax.rsqrt(jnp.mean(zc * zc, axis=-1, keepdims=True) + EPS) * lg_ref[...] + lb_ref[...]
        c_ref[...] = (y * _sig(y)).astype(c_ref.dtype)

    return pl.pallas_call(
        body, name="conv_fwd", grid=(T // tm,),
        in_specs=[_row_spec(tm, C2), pl.BlockSpec((HALO, C2), lambda i: (jnp.maximum(i * per - 1, 0), 0)),
                  _full_spec((HALO, C)), _full_spec((1, C)), _full_spec((1, C)), _full_spec((1, C))],
        out_specs=[_row_spec(tm, C), _row_spec(tm, C)],
        out_shape=[jax.ShapeDtypeStruct((T, C), F32), jax.ShapeDtypeStruct((T, C), _MXU_DTYPE)],
        scratch_shapes=[pltpu.VMEM((tm + HALO, C), F32)],
        compiler_params=_params(("parallel",), 3 * _nbytes((tm, C2), F32), 8 * _nbytes((tm, C), F32)),
    )(pc, pc, w, cb, lg, lb)


def _conv_bwd(pc, z, dc, w, lg, lb):
    T, C2 = pc.shape
    C = C2 // 2
    tm = _tile(T, _ROW_TILE // 2, HALO)
    per = tm // HALO
    n = T // tm
    last_halo = T // HALO - 1

    def body(pc_ref, ph_ref, z_ref, zn_ref, dc_ref, dcn_ref, w_ref, lg_ref, lb_ref,
             dpc_ref, dw_ref, dcb_ref, dlg_ref, dlb_ref, ubuf, dzbuf):
        i = pl.program_id(0)
        _init_acc([dw_ref, dcb_ref, dlg_ref, dlb_ref])
        g, b = lg_ref[...], lb_ref[...]

        def ln_swish_bwd(zv, dcv):
            zc = zv - jnp.mean(zv, axis=-1, keepdims=True)
            r = lax.rsqrt(jnp.mean(zc * zc, axis=-1, keepdims=True) + EPS)
            zh = zc * r
            y = zh * g + b
            s = _sig(y)
            dy = dcv * (s * (1.0 + y * (1.0 - s)))
            dyg = dy * g
            dz = r * (dyg - jnp.mean(dyg, axis=-1, keepdims=True) - zh * jnp.mean(dyg * zh, axis=-1, keepdims=True))
            return dz, dy, zh

        dz_c, dy_c, zh_c = ln_swish_bwd(z_ref[...], dc_ref[...])
        dz_n, _, _ = ln_swish_bwd(zn_ref[...], dcn_ref[...])
        dlg_ref[...] += _colsum(dy_c * zh_c)
        dlb_ref[...] += _colsum(dy_c)
        dcb_ref[...] += _colsum(dz_c)
        dzbuf[0:tm, :] = dz_c
        dzbuf[tm:tm + HALO, :] = jnp.where(i < n - 1, dz_n, 0.0)
        a, s, u_cur = _glu(pc_ref[...], C)
        _, _, u_prev = _glu(ph_ref[...], C)
        ubuf[0:HALO, :] = jnp.where(i > 0, u_prev, 0.0)
        ubuf[HALO:HALO + tm, :] = u_cur
        du = jnp.zeros((tm, C), F32)
        for k in range(CONV_WIDTH):
            back = CONV_WIDTH - 1 - k
            du = du + w_ref[k:k + 1, :] * dzbuf[back:back + tm, :]
            off = HALO - (CONV_WIDTH - 1) + k
            dw_ref[k:k + 1, :] += _colsum(dz_c * ubuf[off:off + tm, :])
        dpc_ref[:, :C] = (du * s).astype(dpc_ref.dtype)
        dpc_ref[:, C:] = (du * a * s * (1.0 - s)).astype(dpc_ref.dtype)

    nxt = lambda i: (jnp.minimum((i + 1) * per, last_halo), 0)
    vec = _full_spec((1, C))
    return pl.pallas_call(
        body, name="conv_bwd", grid=(n,),
        in_specs=[_row_spec(tm, C2), pl.BlockSpec((HALO, C2), lambda i: (jnp.maximum(i * per - 1, 0), 0)),
                  _row_spec(tm, C), pl.BlockSpec((HALO, C), nxt), _row_spec(tm, C), pl.BlockSpec((HALO, C), nxt),
                  _full_spec((HALO, C)), vec, vec],
        out_specs=[_row_spec(tm, C2), _full_spec((HALO, C)), vec, vec, vec],
        out_shape=[jax.ShapeDtypeStruct((T, C2), _MXU_DTYPE), jax.ShapeDtypeStruct((HALO, C), F32)]
                  + [jax.ShapeDtypeStruct((1, C), F32)] * 3,
        scratch_shapes=[pltpu.VMEM((tm + HALO, C), F32), pltpu.VMEM((tm + HALO, C), F32)],
        compiler_params=_params(("arbitrary",), 5 * _nbytes((tm, C2), F32), 12 * _nbytes((tm, C), F32)),
    )(pc, pc, z, z, dc, dc, w, lg, lb)


def _lat_norm_fwd(pl_, gq, gkv, QL, KVL):
    T, W = pl_.shape
    tm = _tile(T, _ROW_TILE, _PACKED_ROWS)

    def body(p_ref, gq_ref, gkv_ref, cq_ref, ckv_ref):
        def norm(xv, gv):
            return xv * lax.rsqrt(jnp.mean(xv * xv, axis=-1, keepdims=True) + EPS) * gv
        cq_ref[...] = norm(p_ref[:, :QL], gq_ref[...]).astype(cq_ref.dtype)
        ckv_ref[...] = norm(p_ref[:, QL:QL + KVL], gkv_ref[...]).astype(ckv_ref.dtype)

    return pl.pallas_call(
        body, name="lat_norm_fwd", grid=(T // tm,),
        in_specs=[_row_spec(tm, W), _full_spec((1, QL)), _full_spec((1, KVL))],
        out_specs=[_row_spec(tm, QL), _row_spec(tm, KVL)],
        out_shape=[jax.ShapeDtypeStruct((T, QL), _MXU_DTYPE), jax.ShapeDtypeStruct((T, KVL), _MXU_DTYPE)],
        compiler_params=_params(("parallel",), 2 * _nbytes((tm, W), F32), 4 * _nbytes((tm, W), F32)),
    )(pl_, gq, gkv)


def _lat_norm_bwd(pl_, gq, gkv, dcq, dckv, drope, QL, KVL):
    T, W = pl_.shape
    tm = _tile(T, _ROW_TILE, _PACKED_ROWS)

    def body(p_ref, gq_ref, gkv_ref, dcq_ref, dckv_ref, dr_ref, dp_ref, dgq_ref, dgkv_ref):
        _init_acc([dgq_ref, dgkv_ref])

        def bwd(xv, gv, dv, dg_ref):
            r = lax.rsqrt(jnp.mean(xv * xv, axis=-1, keepdims=True) + EPS)
            xh = xv * r
            dxh = dv * gv
            dg_ref[...] += _colsum(dv * xh)
            return r * (dxh - xh * jnp.mean(dxh * xh, axis=-1, keepdims=True))

        dp_ref[:, :QL] = bwd(p_ref[:, :QL], gq_ref[...], dcq_ref[...], dgq_ref).astype(dp_ref.dtype)
        dp_ref[:, QL:QL + KVL] = bwd(p_ref[:, QL:QL + KVL], gkv_ref[...], dckv_ref[...], dgkv_ref).astype(dp_ref.dtype)
        dp_ref[:, QL + KVL:] = dr_ref[...].astype(dp_ref.dtype)

    return pl.pallas_call(
        body, name="lat_norm_bwd", grid=(T // tm,),
        in_specs=[_row_spec(tm, W), _full_spec((1, QL)), _full_spec((1, KVL)), _row_spec(tm, QL), _row_spec(tm, KVL),
                  _row_spec(tm, HEAD_W)],
        out_specs=[_row_spec(tm, W), _full_spec((1, QL)), _full_spec((1, KVL))],
        out_shape=[jax.ShapeDtypeStruct((T, W), _MXU_DTYPE), jax.ShapeDtypeStruct((1, QL), F32),
                   jax.ShapeDtypeStruct((1, KVL), F32)],
        compiler_params=_params(("arbitrary",), 4 * _nbytes((tm, W), F32), 6 * _nbytes((tm, W), F32)),
    )(pl_, gq, gkv, dcq, dckv, drope)


def _norm_rope(xv, gv, ct, sa, sb):
    r = lax.rsqrt(jnp.sum(xv * xv, axis=-1, keepdims=True) * (1.0 / QK_DIM) + EPS)
    xn = xv * r * gv
    half = ROPE_DIM // 2
    return xn * ct + pltpu.roll(xn, HEAD_W - half, 1) * sa + pltpu.roll(xn, half, 1) * sb


def _norm_rope_bwd(xv, gv, ct, sa, sb, dout):
    half = ROPE_DIM // 2
    dxn = dout * ct + pltpu.roll(dout * sa, half, 1) + pltpu.roll(dout * sb, HEAD_W - half, 1)
    r = lax.rsqrt(jnp.sum(xv * xv, axis=-1, keepdims=True) * (1.0 / QK_DIM) + EPS)
    xh = xv * r
    dxh = dxn * gv
    dx = r * (dxh - xh * (jnp.sum(dxh * xh, axis=-1, keepdims=True) * (1.0 / QK_DIM)))
    return dx, _colsum(dxn * xh)


def _qk_prep_fwd(q_raw, k_raw, pl_, rope_cb, ct, sa, sb, gq, gk):
    T, HW = q_raw.shape
    H = HW // HEAD_W
    tm = _tile(T, _ROW_TILE, _PACKED_ROWS)

    def body(q_ref, k_ref, r_ref, ct_ref, sa_ref, sb_ref, gq_ref, gk_ref, qn_ref, kn_ref):
        ctv, sav, sbv, rv = ct_ref[...], sa_ref[...], sb_ref[...], r_ref[...]
        for h in range(H):
            hs = slice(h * HEAD_W, (h + 1) * HEAD_W)
            qn_ref[:, hs] = _norm_rope(q_ref[:, hs], gq_ref[...], ctv, sav, sbv).astype(qn_ref.dtype)
            kn_ref[:, hs] = _norm_rope(k_ref[:, hs] + rv, gk_ref[...], ctv, sav, sbv).astype(kn_ref.dtype)

    tb = _row_spec(tm, HEAD_W)
    gb = _full_spec((1, HEAD_W))
    return pl.pallas_call(
        body, name="qk_prep_fwd", grid=(T // tm,),
        in_specs=[_row_spec(tm, HW), _row_spec(tm, HW), _row_spec(tm, HEAD_W, rope_cb), tb, tb, tb, gb, gb],
        out_specs=[_row_spec(tm, HW), _row_spec(tm, HW)],
        out_shape=[jax.ShapeDtypeStruct((T, HW), _MXU_DTYPE)] * 2,
        compiler_params=_params(("parallel",), 3 * _nbytes((tm, HW), F32) + 4 * _nbytes((tm, HEAD_W), F32), 8 * _nbytes((tm, HEAD_W), F32)),
    )(q_raw, k_raw, pl_, ct, sa, sb, gq, gk)


def _qk_prep_bwd(q_raw, k_raw, pl_, rope_cb, ct, sa, sb, gq, gk, dqn, dkn):
    T, HW = q_raw.shape
    H = HW // HEAD_W
    tm = _tile(T, _ROW_TILE, _PACKED_ROWS)

    def body(q_ref, k_ref, r_ref, ct_ref, sa_ref, sb_ref, gq_ref, gk_ref, dqn_ref, dkn_ref,
             dq_ref, dk_ref, dr_ref, dgq_ref, dgk_ref):
        _init_acc([dgq_ref, dgk_ref])
        ctv, sav, sbv, rv = ct_ref[...], sa_ref[...], sb_ref[...], r_ref[...]
        dr = dgq_sum = dgk_sum = None
        for h in range(H):
            hs = slice(h * HEAD_W, (h + 1) * HEAD_W)
            dq, dgq = _norm_rope_bwd(q_ref[:, hs], gq_ref[...], ctv, sav, sbv, dqn_ref[:, hs])
            dk, dgk = _norm_rope_bwd(k_ref[:, hs] + rv, gk_ref[...], ctv, sav, sbv, dkn_ref[:, hs])
            dq_ref[:, hs] = dq.astype(dq_ref.dtype)
            dk_ref[:, hs] = dk.astype(dk_ref.dtype)
            dr = dk if h == 0 else dr + dk
            dgq_sum = dgq if h == 0 else dgq_sum + dgq
            dgk_sum = dgk if h == 0 else dgk_sum + dgk
        dr_ref[...] = dr
        dgq_ref[...] += dgq_sum
        dgk_ref[...] += dgk_sum

    tb = _row_spec(tm, HEAD_W)
    gb = _full_spec((1, HEAD_W))
    hb = _row_spec(tm, HW)
    return pl.pallas_call(
        body, name="qk_prep_bwd", grid=(T // tm,),
        in_specs=[hb, hb, _row_spec(tm, HEAD_W, rope_cb), tb, tb, tb, gb, gb, hb, hb],
        out_specs=[hb, hb, tb, gb, gb],
        out_shape=[jax.ShapeDtypeStruct((T, HW), _MXU_DTYPE)] * 2 + [jax.ShapeDtypeStruct((T, HEAD_W), F32)]
                  + [jax.ShapeDtypeStruct((1, HEAD_W), F32)] * 2,
        compiler_params=_params(("arbitrary",), 5 * _nbytes((tm, HW), F32) + 5 * _nbytes((tm, HEAD_W), F32), 12 * _nbytes((tm, HEAD_W), F32)),
    )(q_raw, k_raw, pl_, ct, sa, sb, gq, gk, dqn, dkn)


_NEG = -1e30
_LOG2E = 1.4426950408889634
_SCORE_C = QK_DIM ** -0.5 * _LOG2E


def _pieces(t, rb, diag):
    assert CHUNK % rb == 0 and rb % _SUBLANES == 0
    out = []
    for r in range(t // rb):
        lo = (r * rb // CHUNK) * CHUNK if diag else 0
        for c in range(t // _LANES):
            cut = None if (c + 1) * _LANES <= lo else max(lo - c * _LANES, 0)
            out.append((slice(r * rb, (r + 1) * rb), slice(c * _LANES, (c + 1) * _LANES), c, cut))
    return out


def _groups(x):
    return x.reshape(x.shape[0] // _SUBLANES, _SUBLANES, x.shape[1])


def _all_sublanes(x8, op):
    return jnp.broadcast_to(op(x8, axis=0, keepdims=True), x8.shape)


def _lane_ge(rb, cut):
    return lax.broadcasted_iota(jnp.int32, (rb, _LANES), 1) >= cut


def _attn_fwd(q, k, v):
    T, HW = q.shape
    H = HW // HEAD_W
    t = _tile(T, _ATT_TILE)
    rb = min(_ATT_ROWS, t)
    nc = t // _LANES

    hp = _ATT_HEADS if H % _ATT_HEADS == 0 else 1
    W = hp * HEAD_W

    def body(q_ref, k_ref, v_ref, o_ref, lse_ref, s_s, p_s, m_s, l_s, acc_s):
        i = pl.program_id(1)
        m_s[...] = jnp.full_like(m_s, _NEG)
        l_s[...] = jnp.zeros_like(l_s)
        acc_s[...] = jnp.zeros_like(acc_s)

        def head_step(hh, rows, diag):
            hs = slice(hh * HEAD_W, (hh + 1) * HEAD_W)
            s_s[hh] = _dot(k_ref[rows, hs], q_ref[:, hs], 'nt')
            pieces = _pieces(t, rb, diag)

            def scores(rs, cs, cut):
                sb = s_s[hh, rs, cs]
                return jnp.where(_lane_ge(rb, cut), sb, _NEG) if cut else sb

            mx = [None] * nc
            for rs, cs, c, cut in pieces:
                if cut is not None:
                    g = jnp.max(_groups(scores(rs, cs, cut)), axis=0)
                    mx[c] = g if mx[c] is None else jnp.maximum(mx[c], g)
            m_new, alpha = [], []
            for c in range(nc):
                cs = slice(c * _LANES, (c + 1) * _LANES)
                m_prev = m_s[hh, :, cs]
                m_new.append(jnp.maximum(m_prev, _all_sublanes(mx[c], jnp.max)))
                alpha.append(jnp.exp2((m_prev - m_new[c]) * _SCORE_C))
                m_s[hh, :, cs] = m_new[c]
            lsum = [jnp.zeros((_SUBLANES, _LANES), F32)] * nc
            for rs, cs, c, cut in pieces:
                if cut is None:
                    p_s[hh, rs, cs] = jnp.zeros((rb, _LANES), p_s.dtype)
                    continue
                p = jnp.exp2((scores(rs, cs, cut) - jnp.tile(m_new[c], (rb // _SUBLANES, 1))) * _SCORE_C)
                lsum[c] = lsum[c] + jnp.sum(_groups(p), axis=0)
                p_s[hh, rs, cs] = p.astype(p_s.dtype)
            for c in range(nc):
                cs = slice(c * _LANES, (c + 1) * _LANES)
                l_s[hh, :, cs] = alpha[c] * l_s[hh, :, cs] + _all_sublanes(lsum[c], jnp.sum)
            a = jnp.tile(jnp.concatenate(alpha, axis=1), (HEAD_W // _SUBLANES, 1))
            acc_s[hh] = a * acc_s[hh] + _dot(v_ref[rows, hs], p_s[hh], 'tn')

        def step(j, diag):
            rows = pl.ds(pl.multiple_of(j * t, t), t)
            for hh in range(hp):
                head_step(hh, rows, diag)

        def loop_body(j, carry):
            step(j, False)
            return carry

        lax.fori_loop(0, i, loop_body, 0)
        step(i, True)
        for hh in range(hp):
            l = l_s[hh]
            o_t = acc_s[hh] / jnp.tile(l, (HEAD_W // _SUBLANES, 1))
            o_ref[:, hh * HEAD_W:(hh + 1) * HEAD_W] = o_t.T.astype(o_ref.dtype)
            lse_ref[hh * _SUBLANES:(hh + 1) * _SUBLANES, :] = m_s[hh] * _SCORE_C + jnp.log(l) * _LOG2E

    qb = pl.BlockSpec((t, W), lambda h, i: (i, h))
    kb = pl.BlockSpec((T, W), lambda h, i: (0, h))
    st = pltpu.VMEM((hp, _SUBLANES, t), F32)
    return pl.pallas_call(
        body, name="attn_fwd", grid=(H // hp, T // t),
        in_specs=[qb, kb, kb], out_specs=[qb, pl.BlockSpec((hp * _SUBLANES, t), lambda h, i: (h, i))],
        out_shape=[jax.ShapeDtypeStruct((T, HW), _MXU_DTYPE), jax.ShapeDtypeStruct((H * _SUBLANES, T), F32)],
        scratch_shapes=[pltpu.VMEM((hp, t, t), F32), pltpu.VMEM((hp, t, t), _MXU_DTYPE), st, st, pltpu.VMEM((hp, HEAD_W, t), F32)],
        compiler_params=_params(("parallel", "arbitrary"), 2 * _nbytes((T, W), _MXU_DTYPE) + 3 * _nbytes((t, W), F32),
                                4 * hp * _nbytes((t, t), F32)),
    )(q, k, v)


def _attn_bwd(q, k, v, o, do, lse2):
    T, HW = q.shape
    H = HW // HEAD_W
    t = _tile(T, _ATT_TILE)
    n = T // t
    rb = min(_ATT_ROWS, t)
    scale = QK_DIM ** -0.5

    hp = _ATT_HEADS if H % _ATT_HEADS == 0 else 1
    W = hp * HEAD_W

    def body(k_ref, v_ref, q_ref, o_ref, do_ref, lse_ref, dq_ref, dk_ref, dv_ref,
             s_s, dp_s, p_s, ds_s, dk_s, dv_s, delta_s, dqt_s):
        j = pl.program_id(1)

        @pl.when(j == 0)
        def _():
            dqt_s[...] = jnp.zeros_like(dqt_s)
            for hh in range(hp):
                hs = slice(hh * HEAD_W, (hh + 1) * HEAD_W)
                for b in range(n):
                    rs = slice(b * t, (b + 1) * t)
                    d = jnp.sum(do_ref[rs, hs].astype(F32) * o_ref[rs, hs].astype(F32), axis=1, keepdims=True)
                    delta_s[hh, :, rs] = jnp.broadcast_to(d, (t, HEAD_W)).T[0:_SUBLANES, :]

        dk_s[...] = jnp.zeros_like(dk_s)
        dv_s[...] = jnp.zeros_like(dv_s)

        def head_step(hh, base, diag):
            hs = slice(hh * HEAD_W, (hh + 1) * HEAD_W)
            kv, vv = k_ref[:, hs], v_ref[:, hs]
            qv, dov = q_ref[pl.ds(base, t), hs], do_ref[pl.ds(base, t), hs]
            s_s[hh] = _dot(kv, qv, 'nt')
            dp_s[hh] = _dot(vv, dov, 'nt')
            for rs, cs, c, cut in _pieces(t, rb, diag):
                if cut is None:
                    p_s[hh, rs, cs] = jnp.zeros((rb, _LANES), p_s.dtype)
                    ds_s[hh, rs, cs] = jnp.zeros((rb, _LANES), ds_s.dtype)
                    continue
                lanes = pl.ds(pl.multiple_of(base + c * _LANES, _LANES), _LANES)
                lse = jnp.tile(lse_ref[hh * _SUBLANES:(hh + 1) * _SUBLANES, lanes], (rb // _SUBLANES, 1))
                dl = jnp.tile(delta_s[hh, :, lanes], (rb // _SUBLANES, 1))
                p = jnp.exp2(s_s[hh, rs, cs] * _SCORE_C - lse)
                if cut:
                    p = jnp.where(_lane_ge(rb, cut), p, 0.0)
                p_s[hh, rs, cs] = p.astype(p_s.dtype)
                ds_s[hh, rs, cs] = (p * (dp_s[hh, rs, cs] - dl)).astype(ds_s.dtype)
            dsv = ds_s[hh]
            dv_s[hh] += _dot(p_s[hh], dov, 'nn')
            dk_s[hh] += _dot(dsv, qv, 'nn')
            dqt_s[hh, :, pl.ds(base, t)] += _dot(kv, dsv, 'tn')

        def step(i, diag):
            base = pl.multiple_of(i * t, t)
            for hh in range(hp):
                head_step(hh, base, diag)

        def loop_body(i, carry):
            step(i, False)
            return carry

        step(j, True)
        lax.fori_loop(j + 1, n, loop_body, 0)
        for hh in range(hp):
            hs = slice(hh * HEAD_W, (hh + 1) * HEAD_W)
            dk_ref[:, hs] = dk_s[hh] * scale
            dv_ref[:, hs] = dv_s[hh].astype(dv_ref.dtype)

        @pl.when(j == n - 1)
        def _():
            for hh in range(hp):
                for b in range(n):
                    rs = slice(b * t, (b + 1) * t)
                    dq_ref[rs, hh * HEAD_W:(hh + 1) * HEAD_W] = dqt_s[hh, :, rs].T * scale

    jb = pl.BlockSpec((t, W), lambda h, j: (j, h))
    fb = pl.BlockSpec((T, W), lambda h, j: (0, h))
    sc = pltpu.VMEM((hp, t, HEAD_W), F32)
    return pl.pallas_call(
        body, name="attn_bwd", grid=(H // hp, n),
        in_specs=[jb, jb, fb, fb, fb, pl.BlockSpec((hp * _SUBLANES, T), lambda h, j: (h, 0))], out_specs=[fb, jb, jb],
        out_shape=[jax.ShapeDtypeStruct((T, HW), F32), jax.ShapeDtypeStruct((T, HW), F32),
                   jax.ShapeDtypeStruct((T, HW), _MXU_DTYPE)],
        scratch_shapes=[pltpu.VMEM((hp, t, t), F32), pltpu.VMEM((hp, t, t), F32), pltpu.VMEM((hp, t, t), _MXU_DTYPE),
                        pltpu.VMEM((hp, t, t), _MXU_DTYPE), sc, sc, pltpu.VMEM((hp, _SUBLANES, T), F32),
                        pltpu.VMEM((hp, HEAD_W, T), F32)],
        compiler_params=_params(("parallel", "arbitrary"), 3 * _nbytes((T, W), _MXU_DTYPE) + _nbytes((T, W), F32)
                                + 4 * _nbytes((t, W), F32), hp * (4 * _nbytes((t, t), F32) + _nbytes((T, HEAD_W), F32))),
    )(k, v, q, o, do, lse2)


def _gate_fwd(pg, gb, yc, ym):
    T, D = yc.shape
    tm = _tile(T, _ROW_TILE, _PACKED_ROWS)

    def body(g0_ref, g1_ref, gb_ref, yc_ref, ym_ref, y_ref):
        s0 = _sig(g0_ref[...] + gb_ref[0:1, :])
        s1 = _sig(g1_ref[...] + gb_ref[1:2, :])
        y_ref[...] = (s0 * yc_ref[...] + s1 * ym_ref[...]).astype(y_ref.dtype)

    return pl.pallas_call(
        body, name="gate_fwd", grid=(T // tm,),
        in_specs=[_row_spec(tm, D, 0), _row_spec(tm, D, 1), _full_spec((2, D)), _row_spec(tm, D), _row_spec(tm, D)],
        out_specs=_row_spec(tm, D),
        out_shape=jax.ShapeDtypeStruct((T, D), _MXU_DTYPE),
        compiler_params=_params(("parallel",), 5 * _nbytes((tm, D), F32), 4 * _nbytes((tm, D), F32)),
    )(pg, pg, gb, yc, ym)


def _gate_bwd(pg, gb, yc, ym, dy):
    T, D = yc.shape
    tm = _tile(T, _ROW_TILE, _PACKED_ROWS)

    def body(g0_ref, g1_ref, gb_ref, yc_ref, ym_ref, dy_ref, dyc_ref, dym_ref, dpg_ref, dgb_ref):
        _init_acc([dgb_ref])
        dyv = dy_ref[...]
        s0 = _sig(g0_ref[...] + gb_ref[0:1, :])
        s1 = _sig(g1_ref[...] + gb_ref[1:2, :])
        dyc_ref[...] = (dyv * s0).astype(dyc_ref.dtype)
        dym_ref[...] = (dyv * s1).astype(dym_ref.dtype)
        d0 = dyv * yc_ref[...] * s0 * (1.0 - s0)
        d1 = dyv * ym_ref[...] * s1 * (1.0 - s1)
        dpg_ref[:, :D] = d0.astype(dpg_ref.dtype)
        dpg_ref[:, D:] = d1.astype(dpg_ref.dtype)
        dgb_ref[0:1, :] += _colsum(d0)
        dgb_ref[1:2, :] += _colsum(d1)

    return pl.pallas_call(
        body, name="gate_bwd", grid=(T // tm,),
        in_specs=[_row_spec(tm, D, 0), _row_spec(tm, D, 1), _full_spec((2, D)), _row_spec(tm, D), _row_spec(tm, D),
                  _row_spec(tm, D)],
        out_specs=[_row_spec(tm, D), _row_spec(tm, D), _row_spec(tm, 2 * D), _full_spec((2, D))],
        out_shape=[jax.ShapeDtypeStruct((T, D), _MXU_DTYPE)] * 2 + [jax.ShapeDtypeStruct((T, 2 * D), _MXU_DTYPE),
                                                                   jax.ShapeDtypeStruct((2, D), F32)],
        compiler_params=_params(("arbitrary",), 7 * _nbytes((tm, D), F32), 8 * _nbytes((tm, D), F32)),
    )(pg, pg, gb, yc, ym, dy)


def _place():
    return lax.axis_index("x"), lax.axis_index("y"), lax.axis_index("c")


def _all_gather(name, xs):
    R, C = xs.shape
    MESH = pl.DeviceIdType.MESH

    def body(x_ref, out_ref, send_sems, recv_sems, local_sem):
        x, y, c = _place()
        me, sibling = (x, y, c), (x, y, 1 - c)
        chips = [(1 - x, y), (x, 1 - y), (1 - x, 1 - y)]

        def rows(px, py, pc):
            return out_ref.at[4 * px + 2 * py + pc]

        def copy(k, block, to, src=None):
            return pltpu.make_async_remote_copy(
                src_ref=rows(*block) if src is None else src, dst_ref=rows(*block),
                send_sem=send_sems.at[k], recv_sem=recv_sems.at[k], device_id=to, device_id_type=MESH)

        mine = pltpu.make_async_copy(x_ref, rows(*me), local_sem)
        mine.start()
        first = [copy(0, me, sibling, src=x_ref)]
        first += [copy(1 + j, me, (*chip, c), src=x_ref) for j, chip in enumerate(chips)]
        for cp in first:
            cp.start()
        passed = [copy(4 + j, (*chip, c), sibling) for j, chip in enumerate(chips)]
        for j, chip in enumerate(chips):
            copy(1 + j, (*chip, c), me).wait_recv()
            passed[j].start()
        copy(0, sibling, me).wait_recv()
        for j, chip in enumerate(chips):
            copy(4 + j, (*chip, 1 - c), me).wait_recv()
        for cp in first + passed:
            cp.wait_send()
        mine.wait()

    return pl.pallas_call(
        body, name=name,
        out_shape=jax.ShapeDtypeStruct((N_DEV, R, C), xs.dtype),
        in_specs=[pl.BlockSpec(memory_space=pl.ANY)], out_specs=pl.BlockSpec(memory_space=pl.ANY),
        scratch_shapes=[pltpu.SemaphoreType.DMA((7,)), pltpu.SemaphoreType.DMA((7,)), pltpu.SemaphoreType.DMA],
    )(xs)


_HBM_SPEC = pl.BlockSpec(memory_space=pltpu.HBM)
_SEM_SPEC = pl.BlockSpec(memory_space=pltpu.SEMAPHORE)
_DATAFLOW = pltpu.SideEffectType.DATAFLOW_SIDE_EFFECTING


def _peers():
    x, y, c = _place()
    out = []
    for k in range(1, N_DEV):
        px = 1 - x if k & 4 else x
        py = 1 - y if k & 2 else y
        pc = 1 - c if k & 1 else c
        out.append((k - 1, (px, py, pc), 4 * px + 2 * py + pc))
    return 4 * x + 2 * y + c, out


def _exchange_copies(x_ref, land_ref, send_sems, recv_sems, scatter):
    me, peers = _peers()
    return [pltpu.make_async_remote_copy(
        src_ref=x_ref.at[idx] if scatter else x_ref, dst_ref=land_ref.at[me],
        send_sem=send_sems.at[k], recv_sem=recv_sems.at[k], device_id=dev, device_id_type=pl.DeviceIdType.MESH)
        for k, dev, idx in peers]


def _exchange_start(name, xs, land, scatter):
    def body(x_ref, land_ref, send_sems, recv_sems, x_thru, land_thru, token):
        for cp in _exchange_copies(x_ref, land_ref, send_sems, recv_sems, scatter):
            cp.start()
        token[...] = jnp.zeros_like(token)

    sems = pltpu.SemaphoreType.DMA((N_DEV - 1,))
    return pl.pallas_call(
        body, name=name,
        out_shape=(sems, sems, pltpu.HBM(xs.shape, xs.dtype), pltpu.HBM(land.shape, land.dtype),
                   jax.ShapeDtypeStruct((_SUBLANES, _LANES), F32)),
        in_specs=(_HBM_SPEC, _HBM_SPEC),
        out_specs=(_SEM_SPEC, _SEM_SPEC, _HBM_SPEC, _HBM_SPEC, pl.BlockSpec(memory_space=pltpu.VMEM)),
        input_output_aliases={0: 2, 1: 3},
        compiler_params=pltpu.CompilerParams(has_side_effects=_DATAFLOW),
    )(pltpu.with_memory_space_constraint(xs, pltpu.HBM), pltpu.with_memory_space_constraint(land, pltpu.HBM))


def _exchange_wait(name, send_sems, recv_sems, xs, land, after, scatter):
    def body(x_ref, land_ref, send_sems, recv_sems, after_ref, x_dead, got_ref):
        for cp in _exchange_copies(x_ref, land_ref, send_sems, recv_sems, scatter):
            cp.wait_send()
            cp.wait_recv()

    return pl.pallas_call(
        body, name=name,
        out_shape=(pltpu.HBM(xs.shape, xs.dtype), pltpu.HBM(land.shape, land.dtype)),
        in_specs=(_HBM_SPEC, _HBM_SPEC, _SEM_SPEC, _SEM_SPEC, pl.BlockSpec(memory_space=pl.ANY)),
        out_specs=(_HBM_SPEC, _HBM_SPEC), input_output_aliases={0: 0, 1: 1},
        compiler_params=pltpu.CompilerParams(has_side_effects=_DATAFLOW),
    )(xs, land, send_sems, recv_sems, after)[1]


def _own_slot(block, me):
    land = lax.empty((N_DEV,) + block.shape, block.dtype)
    return lax.dynamic_update_slice(land, block[None], (me,) + (0,) * block.ndim)


def _sum_blocks(name, parts, scale=None):
    n, R, C = parts.shape
    tr = _tile(R, _PACK_ROW_MULT, _PACKED_ROWS)

    def body(p_ref, o_ref):
        acc = p_ref[0].astype(F32)
        for s in range(1, n):
            acc = acc + p_ref[s].astype(F32)
        if scale is not None:
            acc = acc * scale
        o_ref[...] = acc

    return pl.pallas_call(
        body, name=name, grid=(R // tr,),
        in_specs=[pl.BlockSpec((n, tr, C), lambda i: (0, i, 0))], out_specs=pl.BlockSpec((tr, C), lambda i: (i, 0)),
        out_shape=jax.ShapeDtypeStruct((R, C), F32),
        compiler_params=_params(("parallel",), _nbytes((n, tr, C), parts.dtype) + _nbytes((tr, C), F32), 2 * _nbytes((tr, C), F32)),
    )(parts)


def _adamw_math(wv, gv, mv, vv):
    mv = ADAM_B1 * mv + (1.0 - ADAM_B1) * gv
    vv = ADAM_B2 * vv + (1.0 - ADAM_B2) * (gv * gv)
    m_hat = mv / (1.0 - ADAM_B1 ** ADAM_STEP)
    v_hat = vv / (1.0 - ADAM_B2 ** ADAM_STEP)
    return -ADAM_LR * (m_hat / (jnp.sqrt(v_hat) + ADAM_EPS) + ADAM_WD * wv), mv, vv


def _adamw_layer(l, w, g, m, v, outs):
    depth, R, C = w.shape
    tr = _tile(R, _ROW_TILE, _SUBLANES)

    def body(w_ref, g_ref, m_ref, v_ref, *rest):
        go_ref, d_ref, nm_ref, nv_ref = rest[-4:]
        gv = g_ref[...]
        go_ref[...] = gv
        d_ref[...], nm_ref[...], nv_ref[...] = _adamw_math(w_ref[...], gv, m_ref[...], v_ref[...])

    lay = pl.BlockSpec((None, tr, C), lambda i: (l, i, 0))
    shape = jax.ShapeDtypeStruct(w.shape, F32)
    return pl.pallas_call(
        body, name="adamw_layer", grid=(R // tr,),
        in_specs=[lay, pl.BlockSpec((tr, C), lambda i: (i, 0)), lay, lay] + [pl.BlockSpec(memory_space=pl.ANY)] * (4 if outs else 0),
        out_specs=[lay] * 4, out_shape=[shape] * 4,
        input_output_aliases={4 + k: k for k in range(4)} if outs else {},
        compiler_params=_params(("parallel",), 8 * _nbytes((tr, C), F32), 4 * _nbytes((tr, C), F32)),
    )(w, g, m, v, *(outs or ()))


def _adamw(w, g, m, v):
    shape = w.shape
    w2, g2, m2, v2 = (t.reshape(-1, shape[-1]) for t in (w, g, m, v))
    R, C = w2.shape
    tr = _tile(R, _ROW_TILE, _SUBLANES)

    def body(w_ref, g_ref, m_ref, v_ref, d_ref, nm_ref, nv_ref):
        d_ref[...], nm_ref[...], nv_ref[...] = _adamw_math(w_ref[...], g_ref[...], m_ref[...], v_ref[...])

    spec = pl.BlockSpec((tr, C), lambda i: (i, 0))
    outs = pl.pallas_call(
        body, name="adamw", grid=(R // tr,),
        in_specs=[spec] * 4, out_specs=[spec] * 3,
        out_shape=[jax.ShapeDtypeStruct((R, C), F32)] * 3,
        compiler_params=_params(("parallel",), 7 * _nbytes((tr, C), F32), 4 * _nbytes((tr, C), F32)),
    )(w2, g2, m2, v2)
    return tuple(o.reshape(shape) for o in outs)


def _pack_rows(parts, lead=()):
    out, spans, r0 = [], [], 0
    unit = _PACKED_ROWS * _PACK_W
    for p in parts:
        flat = p.reshape(lead + (-1,))
        size = flat.shape[-1]
        pad = (-size) % unit
        if pad:
            flat = jnp.pad(flat, [(0, 0)] * len(lead) + [(0, pad)])
        rows = (size + pad) // _PACK_W
        out.append(flat.reshape(lead + (rows, _PACK_W)))
        spans.append((r0, rows, p.shape[len(lead):]))
        r0 += rows
    pad = (-r0) % _PACK_ROW_MULT
    if pad:
        out.append(jnp.zeros(lead + (pad, _PACK_W), out[0].dtype))
    return jnp.concatenate(out, axis=len(lead)), spans


def _unpack_rows(packed, span, lead=()):
    r0, rows, shape = span
    size = 1
    for s in shape:
        size *= s
    sl = lax.slice_in_dim(packed, r0, r0 + rows, axis=len(lead))
    return sl.reshape(lead + (-1,))[..., :size].reshape(lead + tuple(shape))


def _pack_flat(parts):
    flat, spans, at = [], [], 0
    for p in parts:
        size = p.size
        padded = size + (-size) % _LANES
        flat.append(jnp.pad(p.reshape(-1), (0, padded - size)))
        spans.append((at, size, p.shape))
        at += padded
    tail = (-at) % (_PACKED_ROWS * _PACK_W)
    if tail:
        flat.append(jnp.zeros((tail,), flat[0].dtype))
    return jnp.concatenate(flat).reshape(-1, _PACK_W), spans


def _unpack_flat(packed, span, lead=()):
    at, size, shape = span
    return lax.slice_in_dim(packed.reshape(lead + (-1,)), at, at + size, axis=len(lead)).reshape(lead + tuple(shape))


def _rope_tables(positions, T):
    inv_freq = ROPE_THETA ** (-jnp.arange(0, ROPE_DIM, 2, dtype=F32) / ROPE_DIM)
    ang = positions.reshape(T, 1).astype(F32) * inv_freq
    cos, sin = jnp.cos(ang), jnp.sin(ang)
    half = ROPE_DIM // 2
    z = lambda n: jnp.zeros((T, n), F32)
    tail = HEAD_W - QK_DIM
    ct = jnp.concatenate([jnp.ones((T, NOPE_DIM), F32), cos, cos, z(tail)], axis=1)
    sa = jnp.concatenate([z(NOPE_DIM), -sin, z(half), z(tail)], axis=1)
    sb = jnp.concatenate([z(NOPE_DIM), z(half), sin, z(tail)], axis=1)
    return ct, sa, sb


def _pad_heads(wt, per_head, keep_from=0, keep=None):
    K = wt.shape[1]
    keep = per_head if keep is None else keep
    w3 = wt.reshape(N_HEADS, per_head, K)[:, keep_from:keep_from + keep]
    return jnp.pad(w3, ((0, 0), (0, HEAD_W - keep), (0, 0))).reshape(N_HEADS * HEAD_W, K)


def _unpad_heads(g, keep):
    return g.reshape(N_HEADS, HEAD_W, g.shape[1])[:, :keep]


def _layer_weights(W):
    D = W['w_out'].shape[1]
    C = W['w_conv_out'].shape[1]
    QL, KVL = W['w_uq'].shape[1], W['w_ukv'].shape[1]
    o1 = 2 * C
    o2, o3 = o1 + QL, o1 + QL + KVL
    o4 = o3 + ROPE_DIM
    win = W['w_in']
    zr = lambda n: jnp.zeros((n, D), win.dtype)
    L = dict(W)
    L['w_c'] = win[:o1]
    L['w_l'] = jnp.concatenate([win[o1:o3], zr(NOPE_DIM), win[o3:o4], zr(HEAD_W - QK_DIM)], axis=0)
    L['w_g'] = win[o4:]
    L['w_q'] = _pad_heads(W['w_uq'], QK_DIM)
    L['w_k'] = _pad_heads(W['w_ukv'], NOPE_DIM + V_DIM, 0, NOPE_DIM)
    L['w_v'] = _pad_heads(W['w_ukv'], NOPE_DIM + V_DIM, NOPE_DIM, V_DIM)
    wmo = W['w_mla_out'].reshape(D, N_HEADS, V_DIM)
    L['w_mo'] = jnp.pad(wmo, ((0, 0), (0, 0), (0, HEAD_W - V_DIM))).reshape(D, N_HEADS * HEAD_W)
    L['dims'] = (D, C, QL, KVL)
    return L


def _row(v, width=None):
    v = v.reshape(1, -1)
    if width is not None and v.shape[1] < width:
        v = jnp.pad(v, ((0, 0), (0, width - v.shape[1])))
    return v


def _ffn_fwd(x, g, wg, wu, wd):
    h = _rms_fwd(x, _row(g))
    a, b, u = _ffn_up(h, wg, wu)
    x_out = _mm("ffn_down", u, wd, 'nn', res=x, scale=0.5)
    return x_out, (x, h, a, b, u)


def _ffn_bwd(dx, saved, g, wg, wu, wd, after):
    x, h, a, b, u = saved
    d_wd = _wgrad("ffn_dwd", u, dx, scale=0.5)
    da, db = _ffn_bwd_up(dx, wd, a, b, after)
    dh = _mm("ffn_dh_g", da, wg, 'nn')
    dh = _mm("ffn_dh_u", db, wu, 'nn', res=dh)
    d_wg = _wgrad("ffn_dwg", da, h)
    d_wu = _wgrad("ffn_dwg", db, h)
    dx_in, dg = _rms_bwd(x, _row(g), dh, dx)
    return dx_in, dg, d_wg, d_wu, d_wd


def _layer_fwd(x, L, S, tabs):
    D, C, QL, KVL = L['dims']
    rope_cb = (QL + KVL) // HEAD_W
    x1, ffn1 = _ffn_fwd(x, S['ffn1_norm'], L['ffn1_w_gate'], L['ffn1_w_up'], L['ffn1_w_down'])
    h = _rms_fwd(x1, _row(S['mix_norm']))
    pc = _mm("proj_c", h, L['w_c'], 'nt')
    pl_ = _mm("proj_l", h, L['w_l'], 'nt')
    pg = _mm("proj_g", h, L['w_g'], 'nt')
    conv_w = jnp.pad(S['conv_w'], ((0, HALO - CONV_WIDTH), (0, 0)))
    z, c = _conv_fwd(pc, conv_w, _row(S['conv_b']), _row(S['conv_ln_g']), _row(S['conv_ln_b']))
    yc = _mm("conv_out", c, L['w_conv_out'], 'nt')
    cqn, ckvn = _lat_norm_fwd(pl_, _row(S['cq_norm']), _row(S['ckv_norm']), QL, KVL)
    q_raw = _mm("up_q", cqn, L['w_q'], 'nt')
    k_raw = _mm("up_kv", ckvn, L['w_k'], 'nt')
    v = _mm("up_kv", ckvn, L['w_v'], 'nt', out_dtype=_MXU_DTYPE)
    gq, gk = _row(S['q_norm'], HEAD_W), _row(S['k_norm'], HEAD_W)
    qn, kn = _qk_prep_fwd(q_raw, k_raw, pl_, rope_cb, *tabs, gq, gk)
    o, lse = _attn_fwd(qn, kn, v)
    ym = _mm("mla_out", o, L['w_mo'], 'nt')
    y = _gate_fwd(pg, S['gate_bias'], yc, ym)
    x2 = _mm("mix_out", y, L['w_out'], 'nn', res=x1)
    x3, ffn2 = _ffn_fwd(x2, S['ffn2_norm'], L['ffn2_w_gate'], L['ffn2_w_up'], L['ffn2_w_down'])
    saved = dict(ffn1=ffn1, ffn2=ffn2, x1=x1, h=h, pc=pc, pl=pl_, pg=pg, z=z, c=c, yc=yc, cqn=cqn, ckvn=ckvn,
                 q_raw=q_raw, k_raw=k_raw, v=v, qn=qn, kn=kn, o=o, lse=lse, ym=ym, y=y, conv_w=conv_w, gq=gq, gk=gk)
    return x3, saved


def _layer_bwd_ffn1(dx, L, S, A, after):
    G, g = {}, {}
    dx, g['ffn1_norm'], G['ffn1_w_gate'], G['ffn1_w_up'], G['ffn1_w_down'] = _ffn_bwd(
        dx, A['ffn1'], S['ffn1_norm'], L['ffn1_w_gate'], L['ffn1_w_up'], L['ffn1_w_down'], after)
    return dx, G, g


def _layer_bwd_mix(dx, L, S, tabs, A, after):
    D, C, QL, KVL = L['dims']
    rope_cb = (QL + KVL) // HEAD_W
    G, g = {}, {}
    dx, g['ffn2_norm'], G['ffn2_w_gate'], G['ffn2_w_up'], G['ffn2_w_down'] = _ffn_bwd(
        dx, A['ffn2'], S['ffn2_norm'], L['ffn2_w_gate'], L['ffn2_w_up'], L['ffn2_w_down'], after)
    G['w_out'] = _wgrad("mix_dwout", A['y'], dx)
    dy = _mm("mix_dy", dx, L['w_out'], 'nt')
    dyc, dym, dpg, g['gate_bias'] = _gate_bwd(A['pg'], S['gate_bias'], A['yc'], A['ym'], dy)
    G['w_conv_out'] = _wgrad("conv_dwout", dyc, A['c'])
    dc = _mm("conv_dc", dyc, L['w_conv_out'], 'nn')
    dpc, dcw, g['conv_b'], g['conv_ln_g'], g['conv_ln_b'] = _conv_bwd(
        A['pc'], A['z'], dc, A['conv_w'], _row(S['conv_ln_g']), _row(S['conv_ln_b']))
    g['conv_w'] = dcw[:CONV_WIDTH]
    d_wmo = _wgrad("mla_dwout", dym, A['o'])
    G['w_mla_out'] = d_wmo.reshape(D, N_HEADS, HEAD_W)[:, :, :V_DIM].reshape(D, N_HEADS * V_DIM)
    do = _mm("mla_do", dym, L['w_mo'], 'nn', out_dtype=_MXU_DTYPE)
    dqn, dkn, dv = _attn_bwd(A['qn'], A['kn'], A['v'], A['o'], do, A['lse'])
    dq_raw, dk_raw, drope, dgq, dgk = _qk_prep_bwd(A['q_raw'], A['k_raw'], A['pl'], rope_cb, *tabs, A['gq'], A['gk'], dqn, dkn)
    g['q_norm'], g['k_norm'] = dgq[:, :QK_DIM], dgk[:, :QK_DIM]
    G['w_uq'] = _unpad_heads(_wgrad("up_dwq", dq_raw, A['cqn']), QK_DIM).reshape(N_HEADS * QK_DIM, QL)
    d_wk = _unpad_heads(_wgrad("up_dwkv", dk_raw, A['ckvn']), NOPE_DIM)
    d_wv = _unpad_heads(_wgrad("up_dwkv", dv, A['ckvn']), V_DIM)
    G['w_ukv'] = jnp.concatenate([d_wk, d_wv], axis=1).reshape(N_HEADS * (NOPE_DIM + V_DIM), KVL)
    dcq = _mm("up_dcq", dq_raw, L['w_q'], 'nn')
    dckv = _mm("up_dckv_k", dk_raw, L['w_k'], 'nn')
    dckv = _mm("up_dckv_v", dv, L['w_v'], 'nn', res=dckv)
    dpl, g['cq_norm'], g['ckv_norm'] = _lat_norm_bwd(A['pl'], _row(S['cq_norm']), _row(S['ckv_norm']), dcq, dckv, drope, QL, KVL)
    d_wc = _wgrad("proj_dwc", dpc, A['h'])
    d_wl = _wgrad("proj_dwl", dpl, A['h'])
    d_wg = _wgrad("proj_dwg", dpg, A['h'])
    ql = QL + KVL
    G['w_in'] = jnp.concatenate([d_wc, d_wl[:ql], d_wl[ql + NOPE_DIM:ql + NOPE_DIM + ROPE_DIM], d_wg], axis=0)
    dh = _mm("proj_dh_c", dpc, L['w_c'], 'nn')
    dh = _mm("proj_dh_l", dpl, L['w_l'], 'nn', res=dh)
    dh = _mm("proj_dh_g", dpg, L['w_g'], 'nn', res=dh)
    dx, g['mix_norm'] = _rms_bwd(A['x1'], _row(S['mix_norm']), dh, dx)
    return dx, G, g


def kernel(x, positions, ffn1_norm, ffn1_w_gate, ffn1_w_up, ffn1_w_down, mix_norm, w_in, gate_bias, conv_w, conv_b, conv_ln_g, conv_ln_b, w_conv_out, cq_norm, ckv_norm, w_uq, w_ukv, q_norm, k_norm, w_mla_out, w_out, ffn2_norm, ffn2_w_gate, ffn2_w_up, ffn2_w_down, loss_target, m_ffn1_norm, m_ffn1_w_gate, m_ffn1_w_up, m_ffn1_w_down, m_mix_norm, m_w_in, m_gate_bias, m_conv_w, m_conv_b, m_conv_ln_g, m_conv_ln_b, m_w_conv_out, m_cq_norm, m_ckv_norm, m_w_uq, m_w_ukv, m_q_norm, m_k_norm, m_w_mla_out, m_w_out, m_ffn2_norm, m_ffn2_w_gate, m_ffn2_w_up, m_ffn2_w_down, v_ffn1_norm, v_ffn1_w_gate, v_ffn1_w_up, v_ffn1_w_down, v_mix_norm, v_w_in, v_gate_bias, v_conv_w, v_conv_b, v_conv_ln_g, v_conv_ln_b, v_w_conv_out, v_cq_norm, v_ckv_norm, v_w_uq, v_w_ukv, v_q_norm, v_k_norm, v_w_mla_out, v_w_out, v_ffn2_norm, v_ffn2_w_gate, v_ffn2_w_up, v_ffn2_w_down):
    w = dict(zip(WEIGHTS, (ffn1_norm, ffn1_w_gate, ffn1_w_up, ffn1_w_down, mix_norm, w_in, gate_bias, conv_w, conv_b, conv_ln_g, conv_ln_b, w_conv_out, cq_norm, ckv_norm, w_uq, w_ukv, q_norm, k_norm, w_mla_out, w_out, ffn2_norm, ffn2_w_gate, ffn2_w_up, ffn2_w_down)))
    m = dict(zip(WEIGHTS, (m_ffn1_norm, m_ffn1_w_gate, m_ffn1_w_up, m_ffn1_w_down, m_mix_norm, m_w_in, m_gate_bias, m_conv_w, m_conv_b, m_conv_ln_g, m_conv_ln_b, m_w_conv_out, m_cq_norm, m_ckv_norm, m_w_uq, m_w_ukv, m_q_norm, m_k_norm, m_w_mla_out, m_w_out, m_ffn2_norm, m_ffn2_w_gate, m_ffn2_w_up, m_ffn2_w_down)))
    v = dict(zip(WEIGHTS, (v_ffn1_norm, v_ffn1_w_gate, v_ffn1_w_up, v_ffn1_w_down, v_mix_norm, v_w_in, v_gate_bias, v_conv_w, v_conv_b, v_conv_ln_g, v_conv_ln_b, v_w_conv_out, v_cq_norm, v_ckv_norm, v_w_uq, v_w_ukv, v_q_norm, v_k_norm, v_w_mla_out, v_w_out, v_ffn2_norm, v_ffn2_w_gate, v_ffn2_w_up, v_ffn2_w_down)))
    depth = ffn1_norm.shape[0]
    T, D = x.shape[1], x.shape[2]
    xs = x.reshape(T, D)
    target = loss_target.reshape(T, D)
    tabs = _rope_tables(positions, T)
    me = 4 * lax.axis_index("x") + 2 * lax.axis_index("y") + lax.axis_index("c")

    big = list(BIG)
    packed = []
    for l in range(depth):
        p, spans = _pack_rows([(w[n][l].T if BIG[n] else w[n][l]).astype(_WIRE_DTYPE) for n in big])
        packed.append(p)
    gathered = {0: _all_gather("gather_weights", packed[0])}
    in_flight = {l: _exchange_start(f"gather_start_{l}", packed[l], _own_slot(packed[l], me), False)
                 for l in range(1, depth)}
    small_names = [(l, n) for l in range(depth) for n in SMALL_SHARDED]
    small_packed, small_spans = _pack_flat([w[n][l].T for l, n in small_names])
    small_gathered = _all_gather("gather_small", small_packed)
    for st in in_flight.values():
        xs, _ = lax.optimization_barrier((xs, st[4]))

    def layer_weights(l):
        W = {}
        for n, span in zip(big, spans):
            sh = _unpack_rows(gathered[l], span, (N_DEV,))
            W[n] = sh.reshape(N_DEV * sh.shape[1], sh.shape[2])
        S = {n: w[n][l] for n in REPLICATED}
        for (ll, n), span in zip(small_names, small_spans):
            if ll == l:
                sh = _unpack_flat(small_gathered, span, (N_DEV,))
                S[n] = sh.reshape(N_DEV * sh.shape[1], sh.shape[2]).T
        return _layer_weights(W), S

    layers, acts = [], []
    hcur = xs
    for l in range(depth):
        if l > 0:
            send_sems, recv_sems, src, land, _ = in_flight[l]
            gathered[l] = _exchange_wait(f"gather_wait_{l}", send_sems, recv_sems, src, land, hcur, False)
        layers.append(layer_weights(l))
        hcur, saved = _layer_fwd(hcur, *layers[l], tabs)
        acts.append(saved)
    dx, loss_part = _loss_grad(hcur, target)

    stage_names = [[n for n in big if n.startswith('ffn1_')], [n for n in big if not n.startswith('ffn1_')]]
    small_grads, reduced, gspans, in_flight = [{} for _ in range(depth)], {}, {}, {}
    after = jnp.zeros((_SUBLANES, _LANES), F32)
    for l in reversed(range(depth)):
        for stage in (1, 0):
            if stage:
                dx, G, g = _layer_bwd_mix(dx, *layers[l], tabs, acts[l], after)
            else:
                dx, G, g = _layer_bwd_ffn1(dx, *layers[l], acts[l], after)
            small_grads[l].update(g)
            blocks = [G[n].reshape(N_DEV, -1, G[n].shape[1]) for n in stage_names[stage]]
            send, gspans[stage] = _pack_rows(blocks, (N_DEV,))
            own = lax.dynamic_index_in_dim(send, me, 0, keepdims=False)
            in_flight[l, stage] = _exchange_start(f"scatter_start_{l}_{stage}", send, _own_slot(own, me), True)
            after = in_flight[l, stage][4]
    small_list = [(l, n) for l in range(depth) for n in REPLICATED + SMALL_SHARDED]
    loss_part = loss_part + after[:1, :]
    small_send, sspans = _pack_flat([small_grads[l][n] for l, n in small_list] + [loss_part])
    small_flight = _exchange_start("small_grads_start", small_send, _own_slot(small_send, me), False)

    out4 = {}
    after = small_flight[4]
    for (l, stage), (send_sems, recv_sems, src, land, _) in in_flight.items():
        got = _exchange_wait(f"scatter_wait_{l}_{stage}", send_sems, recv_sems, src, land, after, True)
        reduced = _sum_blocks("sum_grads_direct", got)
        for n, span in zip(stage_names[stage], gspans[stage]):
            gt = _unpack_rows(reduced, span)
            out4[n] = _adamw_layer(l, w[n], gt.T if BIG[n] else gt, m[n], v[n], out4.get(n))
            after = out4[n][1]
    send_sems, recv_sems, src, land, _ = small_flight
    small_all = _sum_blocks("sum_small", _exchange_wait("small_grads_wait", send_sems, recv_sems, src, land, after, False))
    loss = _unpack_flat(small_all, sspans[-1])[0, 0]
    for n in WEIGHTS:
        if n in BIG:
            continue
        per_layer = []
        for l in range(depth):
            gs = _unpack_flat(small_all, sspans[small_list.index((l, n))])
            if n in SMALL_SHARDED:
                cols = w[n].shape[-1]
                gs = lax.dynamic_slice_in_dim(gs, me * cols, cols, axis=1)
            per_layer.append(gs.reshape(w[n].shape[1:]))
        g = jnp.stack(per_layer, axis=0)
        out4[n] = (g,) + _adamw(w[n], g, m[n], v[n])
    return (loss, dx.reshape(1, T, D), *[out4[n][k] for k in range(4) for n in WEIGHTS])
```

```python
import jax
import jax.numpy as jnp
from jax import lax
from jax.experimental import pallas as pl
from jax.experimental.pallas import tpu as pltpu

F32 = jnp.float32
_MXU_DTYPE = jnp.bfloat16
_WIRE_DTYPE = jnp.bfloat16

_LANES = 128
_SUBLANES = 8
_PACKED_ROWS = 16
_V7X_VMEM_BYTES = 64 * 2 ** 20
_VMEM_HEADROOM = 8 * 2 ** 20

N_DEV = 8
EPS = 1e-6
CHUNK = 64
CONV_WIDTH = 31
N_HEADS = 8
NOPE_DIM = 64
ROPE_DIM = 32
QK_DIM = NOPE_DIM + ROPE_DIM
V_DIM = 64
HEAD_W = _LANES
ROPE_THETA = 10000.0
HALO = 32

ADAM_LR = 0.001
ADAM_B1 = 0.9
ADAM_B2 = 0.999
ADAM_EPS = 1e-08
ADAM_WD = 0.01
ADAM_STEP = 10

_ROW_TILE = 512
_MM_TILE = 512
_FFN_TILE = 512
_MM_TILE_N = 1408
_MM_TILE_K = 1024
_TN_TILE_M = 1408
_ATT_TILE = 512
_ATT_ROWS = 32
_ATT_HEADS = 2
_PACK_W = 1024
_PACK_ROW_MULT = 128

WEIGHTS = ['ffn1_norm', 'ffn1_w_gate', 'ffn1_w_up', 'ffn1_w_down', 'mix_norm', 'w_in', 'gate_bias',
           'conv_w', 'conv_b', 'conv_ln_g', 'conv_ln_b', 'w_conv_out', 'cq_norm', 'ckv_norm', 'w_uq',
           'w_ukv', 'q_norm', 'k_norm', 'w_mla_out', 'w_out', 'ffn2_norm', 'ffn2_w_gate', 'ffn2_w_up',
           'ffn2_w_down']
BIG = {'ffn1_w_gate': True, 'ffn1_w_up': True, 'ffn1_w_down': False, 'w_in': True, 'w_conv_out': True,
       'w_uq': True, 'w_ukv': True, 'w_mla_out': True, 'w_out': False, 'ffn2_w_gate': True,
       'ffn2_w_up': True, 'ffn2_w_down': False}
SMALL_SHARDED = ['conv_w', 'gate_bias']
REPLICATED = ['ffn1_norm', 'mix_norm', 'conv_b', 'conv_ln_g', 'conv_ln_b', 'cq_norm', 'ckv_norm',
              'q_norm', 'k_norm', 'ffn2_norm']


def _tile(n, target, mult=_LANES):
    if n <= target:
        return n
    for d in range(target - target % mult, 0, -mult):
        if n % d == 0:
            return d
    return n


def _nbytes(shape, dtype):
    n = 1
    for s in shape:
        n *= s
    return n * jnp.dtype(dtype).itemsize


def _params(dims, block_bytes, extra_bytes=0):
    need = 2 * block_bytes + extra_bytes + _VMEM_HEADROOM
    assert need <= _V7X_VMEM_BYTES, (need, dims)
    return pltpu.CompilerParams(dimension_semantics=dims, vmem_limit_bytes=_V7X_VMEM_BYTES - _VMEM_HEADROOM)


def _mxu(x, scale=None):
    if scale is not None:
        x = x * scale
    return x if x.dtype == _MXU_DTYPE else x.astype(_MXU_DTYPE)


def _dot(a, b, mode):
    dims = {'nn': (((1,), (0,)), ((), ())), 'nt': (((1,), (1,)), ((), ())), 'tn': (((0,), (0,)), ((), ()))}[mode]
    return lax.dot_general(a, b, dims, preferred_element_type=F32)


def _colsum(x):
    return jnp.sum(x, axis=0, keepdims=True)


def _sig(x):
    return 0.5 * jnp.tanh(0.5 * x) + 0.5


def _mm(name, a, b, mode, *, out_dtype=F32, res=None, scale=None, a_scale=None, tm=None, tn=None, tk=None):
    if mode == 'nn':
        (M, K), (_, N) = a.shape, b.shape
    elif mode == 'nt':
        (M, K), (N, _) = a.shape, b.shape
    else:
        (K, M), (_, N) = a.shape, b.shape
    tm = _tile(M, tm or (_TN_TILE_M if mode == 'tn' else _MM_TILE))
    tn = _tile(N, tn or _MM_TILE_N)
    tk = _tile(K, tk or (_MM_TILE_K if mode == 'tn' else 4096))
    nk = K // tk
    grid = (M // tm, N // tn, nk)
    a_spec = pl.BlockSpec((tk, tm), lambda i, j, k: (k, i)) if mode == 'tn' else pl.BlockSpec((tm, tk), lambda i, j, k: (i, k))
    b_spec = pl.BlockSpec((tn, tk), lambda i, j, k: (j, k)) if mode == 'nt' else pl.BlockSpec((tk, tn), lambda i, j, k: (k, j))
    o_spec = pl.BlockSpec((tm, tn), lambda i, j, k: (i, j))
    has_res = res is not None

    def body(*refs):
        a_ref, b_ref = refs[0], refs[1]
        res_ref = refs[2] if has_res else None
        o_ref = refs[3] if has_res else refs[2]
        acc_ref = refs[-1] if nk > 1 else None

        def finish(p):
            if scale is not None:
                p = p * scale
            if has_res:
                p = res_ref[...] + p
            o_ref[...] = p.astype(o_ref.dtype)

        p = _dot(_mxu(a_ref[...], a_scale), _mxu(b_ref[...]), mode)
        if nk == 1:
            finish(p)
        else:
            k = pl.program_id(2)

            @pl.when(k == 0)
            def _():
                acc_ref[...] = p

            @pl.when(k > 0)
            def _():
                acc_ref[...] += p

            @pl.when(k == nk - 1)
            def _():
                finish(acc_ref[...])

    blocks = (_nbytes((tm, tk), a.dtype) + _nbytes((tk, tn), b.dtype) + _nbytes((tm, tn), out_dtype)
              + (_nbytes((tm, tn), F32) if has_res else 0))
    extra = ((2 + (nk > 1)) * _nbytes((tm, tn), F32) + (a.dtype != _MXU_DTYPE) * _nbytes((tm, tk), _MXU_DTYPE)
             + (b.dtype != _MXU_DTYPE) * _nbytes((tk, tn), _MXU_DTYPE))
    return pl.pallas_call(
        body, name=name, grid=grid,
        in_specs=[a_spec, b_spec] + ([o_spec] if has_res else []),
        out_specs=o_spec,
        out_shape=jax.ShapeDtypeStruct((M, N), out_dtype),
        scratch_shapes=[pltpu.VMEM((tm, tn), F32)] if nk > 1 else [],
        compiler_params=_params(("parallel", "parallel", "arbitrary"), blocks, extra),
    )(*([a, b] + ([res] if has_res else [])))


def _wgrad(name, a, b, scale=None):
    return _mm(name, a, b, 'tn', out_dtype=_WIRE_DTYPE, scale=scale)


def _col_chunks(n, width=2 * _LANES):
    return [slice(c, min(c + width, n)) for c in range(0, n, width)]


def _ffn_up(h, wg, wu):
    T, D = h.shape
    F = wg.shape[0]
    tm, tn = _tile(T, _FFN_TILE), _tile(F, _MM_TILE_N)

    def body(h_ref, wg_ref, wu_ref, a_ref, b_ref, u_ref):
        hv = h_ref[...]
        for cols in _col_chunks(tn):
            a = _dot(hv, wg_ref[cols, :], 'nt')
            b = _dot(hv, wu_ref[cols, :], 'nt')
            a_ref[:, cols] = a
            b_ref[:, cols] = b
            u_ref[:, cols] = (a * _sig(a) * b).astype(u_ref.dtype)

    w_spec = pl.BlockSpec((tn, D), lambda j, i: (j, 0))
    o_spec = pl.BlockSpec((tm, tn), lambda j, i: (i, j))
    blocks = _nbytes((tm, D), h.dtype) + 2 * _nbytes((tn, D), wg.dtype) + 2 * _nbytes((tm, tn), F32) + _nbytes((tm, tn), _MXU_DTYPE)
    return pl.pallas_call(
        body, name="ffn_up", grid=(F // tn, T // tm),
        in_specs=[pl.BlockSpec((tm, D), lambda j, i: (i, 0)), w_spec, w_spec],
        out_specs=[o_spec, o_spec, o_spec],
        out_shape=[jax.ShapeDtypeStruct((T, F), F32), jax.ShapeDtypeStruct((T, F), F32),
                   jax.ShapeDtypeStruct((T, F), _MXU_DTYPE)],
        compiler_params=_params(("parallel", "parallel"), blocks, 4 * _nbytes((tm, tn), F32)),
    )(h, wg, wu)


def _ffn_bwd_up(dx, wd, a, b, after):
    T, D = dx.shape
    F = wd.shape[0]
    tm, tn = _tile(T, _FFN_TILE), _tile(F, _MM_TILE_N)

    def body(dx_ref, wd_ref, a_ref, b_ref, after_ref, da_ref, db_ref):
        dy = _mxu(dx_ref[...], 0.5)
        for cols in _col_chunks(tn):
            du = _dot(dy, wd_ref[cols, :], 'nt')
            av, bv = a_ref[:, cols], b_ref[:, cols]
            s = _sig(av)
            db_ref[:, cols] = (du * av * s).astype(db_ref.dtype)
            da_ref[:, cols] = (du * bv * (s * (1.0 + av * (1.0 - s)))).astype(da_ref.dtype)

    t_spec = pl.BlockSpec((tm, tn), lambda j, i: (i, j))
    blocks = _nbytes((tm, D), F32) + _nbytes((tn, D), wd.dtype) + 2 * _nbytes((tm, tn), F32) + 2 * _nbytes((tm, tn), _MXU_DTYPE)
    return pl.pallas_call(
        body, name="ffn_bwd_up", grid=(F // tn, T // tm),
        in_specs=[pl.BlockSpec((tm, D), lambda j, i: (i, 0)), pl.BlockSpec((tn, D), lambda j, i: (j, 0)), t_spec, t_spec,
                  pl.BlockSpec(memory_space=pl.ANY)],
        out_specs=[t_spec, t_spec],
        out_shape=[jax.ShapeDtypeStruct((T, F), _MXU_DTYPE)] * 2,
        compiler_params=_params(("parallel", "parallel"), blocks, 5 * _nbytes((tm, tn), F32)),
    )(dx, wd, a, b, after)


def _row_spec(tm, w, cb=0):
    return pl.BlockSpec((tm, w), lambda i: (i, cb))


def _full_spec(shape):
    return pl.BlockSpec(shape, lambda i: (0,) * len(shape))


def _init_acc(refs):
    @pl.when(pl.program_id(0) == 0)
    def _():
        for r in refs:
            r[...] = jnp.zeros_like(r)


def _rms_fwd(x, g):
    T, D = x.shape
    tm = _tile(T, _ROW_TILE, _PACKED_ROWS)

    def body(x_ref, g_ref, h_ref):
        xv = x_ref[...]
        r = lax.rsqrt(jnp.mean(xv * xv, axis=-1, keepdims=True) + EPS)
        h_ref[...] = (xv * r * g_ref[...]).astype(h_ref.dtype)

    return pl.pallas_call(
        body, name="rms_fwd", grid=(T // tm,),
        in_specs=[_row_spec(tm, D), _full_spec((1, D))], out_specs=_row_spec(tm, D),
        out_shape=jax.ShapeDtypeStruct((T, D), _MXU_DTYPE),
        compiler_params=_params(("parallel",), 2 * _nbytes((tm, D), F32), 4 * _nbytes((tm, D), F32)),
    )(x, g)


def _dh_rms_bwd(name, pairs, x, g, dres):
    T, D = x.shape
    tm = _tile(T, _ROW_TILE // 2, _PACKED_ROWS)
    n = len(pairs)

    def body(*refs):
        x_ref, g_ref, dres_ref, dx_ref, dg_ref = refs[2 * n:]
        _init_acc([dg_ref])
        dh = None
        for k in range(n):
            p = _dot(_mxu(refs[2 * k][...]), refs[2 * k + 1][...], 'nn')
            dh = p if dh is None else dh + p
        xv = x_ref[...]
        r = lax.rsqrt(jnp.mean(xv * xv, axis=-1, keepdims=True) + EPS)
        xh = xv * r
        dxh = dh * g_ref[...]
        dx_ref[...] = dres_ref[...] + r * (dxh - xh * jnp.mean(dxh * xh, axis=-1, keepdims=True))
        dg_ref[...] += _colsum(dh * xh)

    in_specs, args, blocks = [], [], 3 * _nbytes((tm, D), F32)
    for a, w in pairs:
        in_specs += [_row_spec(tm, a.shape[1]), _full_spec(w.shape)]
        args += [a, w]
        blocks += _nbytes((tm, a.shape[1]), a.dtype) + _nbytes(w.shape, w.dtype)
    return pl.pallas_call(
        body, name=name, grid=(T // tm,),
        in_specs=in_specs + [_row_spec(tm, D), _full_spec((1, D)), _row_spec(tm, D)],
        out_specs=[_row_spec(tm, D), _full_spec((1, D))],
        out_shape=[jax.ShapeDtypeStruct((T, D), F32), jax.ShapeDtypeStruct((1, D), F32)],
        compiler_params=_params(("arbitrary",), blocks, 6 * _nbytes((tm, D), F32)),
    )(*args, x, g, dres)


def _loss_grad(y, target):
    T, D = y.shape
    tm = _tile(T, _ROW_TILE, _SUBLANES)

    def body(y_ref, t_ref, dy_ref, l_ref):
        _init_acc([l_ref])
        err = y_ref[...] - t_ref[...]
        dy_ref[...] = err * (1.0 / D)
        per_row = jnp.sum(err * err, axis=-1, keepdims=True) * (0.5 / D)
        l_ref[...] += jnp.sum(per_row, axis=0, keepdims=True)

    return pl.pallas_call(
        body, name="loss_grad", grid=(T // tm,),
        in_specs=[_row_spec(tm, D), _row_spec(tm, D)],
        out_specs=[_row_spec(tm, D), _full_spec((1, _LANES))],
        out_shape=[jax.ShapeDtypeStruct((T, D), F32), jax.ShapeDtypeStruct((1, _LANES), F32)],
        compiler_params=_params(("arbitrary",), 3 * _nbytes((tm, D), F32), 3 * _nbytes((tm, D), F32)),
    )(y, target)


def _glu(pc, C):
    a, gate = pc[:, :C], pc[:, C:]
    s = _sig(gate)
    return a, s, a * s


def _conv_fwd(pc, w, cb, lg, lb):
    T, C2 = pc.shape
    C = C2 // 2
    tm = _tile(T, _ROW_TILE // 2, HALO)
    per = tm // HALO

    def body(pc_ref, ph_ref, w_ref, cb_ref, lg_ref, lb_ref, z_ref, c_ref, ubuf):
        i = pl.program_id(0)
        _, _, u_cur = _glu(pc_ref[...], C)
        _, _, u_prev = _glu(ph_ref[...], C)
        ubuf[0:HALO, :] = jnp.where(i > 0, u_prev, 0.0)
        ubuf[HALO:HALO + tm, :] = u_cur
        acc = jnp.zeros((tm, C), F32)
        for k in range(CONV_WIDTH):
            off = HALO - (CONV_WIDTH - 1) + k
            acc = acc + w_ref[k:k + 1, :] * ubuf[off:off + tm, :]
        z = acc + cb_ref[...]
        z_ref[...] = z
        zc = z - jnp.mean(z, axis=-1, keepdims=True)
        y = zc * lax.rsqrt(jnp.mean(zc * zc, axis=-1, keepdims=True) + EPS) * lg_ref[...] + lb_ref[...]
        c_ref[...] = (y * _sig(y)).astype(c_ref.dtype)

    return pl.pallas_call(
        body, name="conv_fwd", grid=(T // tm,),
        in_specs=[_row_spec(tm, C2), pl.BlockSpec((HALO, C2), lambda i: (jnp.maximum(i * per - 1, 0), 0)),
                  _full_spec((HALO, C)), _full_spec((1, C)), _full_spec((1, C)), _full_spec((1, C))],
        out_specs=[_row_spec(tm, C), _row_spec(tm, C)],
        out_shape=[jax.ShapeDtypeStruct((T, C), F32), jax.ShapeDtypeStruct((T, C), _MXU_DTYPE)],
        scratch_shapes=[pltpu.VMEM((tm + HALO, C), F32)],
        compiler_params=_params(("parallel",), 3 * _nbytes((tm, C2), F32), 8 * _nbytes((tm, C), F32)),
    )(pc, pc, w, cb, lg, lb)


def _conv_bwd(pc, z, dc, w, lg, lb):
    T, C2 = pc.shape
    C = C2 // 2
    tm = _tile(T, _ROW_TILE // 2, HALO)
    per = tm // HALO
    n = T // tm
    last_halo = T // HALO - 1

    def body(pc_ref, ph_ref, z_ref, zn_ref, dc_ref, dcn_ref, w_ref, lg_ref, lb_ref,
             dpc_ref, dw_ref, dcb_ref, dlg_ref, dlb_ref, ubuf, dzbuf):
        i = pl.program_id(0)
        _init_acc([dw_ref, dcb_ref, dlg_ref, dlb_ref])
        g, b = lg_ref[...], lb_ref[...]

        def ln_swish_bwd(zv, dcv):
            zc = zv - jnp.mean(zv, axis=-1, keepdims=True)
            r = lax.rsqrt(jnp.mean(zc * zc, axis=-1, keepdims=True) + EPS)
            zh = zc * r
            y = zh * g + b
            s = _sig(y)
            dy = dcv * (s * (1.0 + y * (1.0 - s)))
            dyg = dy * g
            dz = r * (dyg - jnp.mean(dyg, axis=-1, keepdims=True) - zh * jnp.mean(dyg * zh, axis=-1, keepdims=True))
            return dz, dy, zh

        dz_c, dy_c, zh_c = ln_swish_bwd(z_ref[...], dc_ref[...])
        dz_n, _, _ = ln_swish_bwd(zn_ref[...], dcn_ref[...])
        dlg_ref[...] += _colsum(dy_c * zh_c)
        dlb_ref[...] += _colsum(dy_c)
        dcb_ref[...] += _colsum(dz_c)
        dzbuf[0:tm, :] = dz_c
        dzbuf[tm:tm + HALO, :] = jnp.where(i < n - 1, dz_n, 0.0)
        a, s, u_cur = _glu(pc_ref[...], C)
        _, _, u_prev = _glu(ph_ref[...], C)
        ubuf[0:HALO, :] = jnp.where(i > 0, u_prev, 0.0)
        ubuf[HALO:HALO + tm, :] = u_cur
        du = jnp.zeros((tm, C), F32)
        for k in range(CONV_WIDTH):
            back = CONV_WIDTH - 1 - k
            du = du + w_ref[k:k + 1, :] * dzbuf[back:back + tm, :]
            off = HALO - (CONV_WIDTH - 1) + k
            dw_ref[k:k + 1, :] += _colsum(dz_c * ubuf[off:off + tm, :])
        dpc_ref[:, :C] = (du * s).astype(dpc_ref.dtype)
        dpc_ref[:, C:] = (du * a * s * (1.0 - s)).astype(dpc_ref.dtype)

    nxt = lambda i: (jnp.minimum((i + 1) * per, last_halo), 0)
    vec = _full_spec((1, C))
    return pl.pallas_call(
        body, name="conv_bwd", grid=(n,),
        in_specs=[_row_spec(tm, C2), pl.BlockSpec((HALO, C2), lambda i: (jnp.maximum(i * per - 1, 0), 0)),
                  _row_spec(tm, C), pl.BlockSpec((HALO, C), nxt), _row_spec(tm, C), pl.BlockSpec((HALO, C), nxt),
                  _full_spec((HALO, C)), vec, vec],
        out_specs=[_row_spec(tm, C2), _full_spec((HALO, C)), vec, vec, vec],
        out_shape=[jax.ShapeDtypeStruct((T, C2), _MXU_DTYPE), jax.ShapeDtypeStruct((HALO, C), F32)]
                  + [jax.ShapeDtypeStruct((1, C), F32)] * 3,
        scratch_shapes=[pltpu.VMEM((tm + HALO, C), F32), pltpu.VMEM((tm + HALO, C), F32)],
        compiler_params=_params(("arbitrary",), 5 * _nbytes((tm, C2), F32), 12 * _nbytes((tm, C), F32)),
    )(pc, pc, z, z, dc, dc, w, lg, lb)


def _lat_norm_fwd(pl_, gq, gkv, QL, KVL):
    T, W = pl_.shape
    tm = _tile(T, _ROW_TILE, _PACKED_ROWS)

    def body(p_ref, gq_ref, gkv_ref, cq_ref, ckv_ref):
        def norm(xv, gv):
            return xv * lax.rsqrt(jnp.mean(xv * xv, axis=-1, keepdims=True) + EPS) * gv
        cq_ref[...] = norm(p_ref[:, :QL], gq_ref[...]).astype(cq_ref.dtype)
        ckv_ref[...] = norm(p_ref[:, QL:QL + KVL], gkv_ref[...]).astype(ckv_ref.dtype)

    return pl.pallas_call(
        body, name="lat_norm_fwd", grid=(T // tm,),
        in_specs=[_row_spec(tm, W), _full_spec((1, QL)), _full_spec((1, KVL))],
        out_specs=[_row_spec(tm, QL), _row_spec(tm, KVL)],
        out_shape=[jax.ShapeDtypeStruct((T, QL), _MXU_DTYPE), jax.ShapeDtypeStruct((T, KVL), _MXU_DTYPE)],
        compiler_params=_params(("parallel",), 2 * _nbytes((tm, W), F32), 4 * _nbytes((tm, W), F32)),
    )(pl_, gq, gkv)


def _lat_norm_bwd(pl_, gq, gkv, dcq, dckv, drope, QL, KVL):
    T, W = pl_.shape
    tm = _tile(T, _ROW_TILE, _PACKED_ROWS)

    def body(p_ref, gq_ref, gkv_ref, dcq_ref, dckv_ref, dr_ref, dp_ref, dgq_ref, dgkv_ref):
        _init_acc([dgq_ref, dgkv_ref])

        def bwd(xv, gv, dv, dg_ref):
            r = lax.rsqrt(jnp.mean(xv * xv, axis=-1, keepdims=True) + EPS)
            xh = xv * r
            dxh = dv * gv
            dg_ref[...] += _colsum(dv * xh)
            return r * (dxh - xh * jnp.mean(dxh * xh, axis=-1, keepdims=True))

        dp_ref[:, :QL] = bwd(p_ref[:, :QL], gq_ref[...], dcq_ref[...], dgq_ref).astype(dp_ref.dtype)
        dp_ref[:, QL:QL + KVL] = bwd(p_ref[:, QL:QL + KVL], gkv_ref[...], dckv_ref[...], dgkv_ref).astype(dp_ref.dtype)
        dp_ref[:, QL + KVL:] = dr_ref[...].astype(dp_ref.dtype)

    return pl.pallas_call(
        body, name="lat_norm_bwd", grid=(T // tm,),
        in_specs=[_row_spec(tm, W), _full_spec((1, QL)), _full_spec((1, KVL)), _row_spec(tm, QL), _row_spec(tm, KVL),
                  _row_spec(tm, HEAD_W)],
        out_specs=[_row_spec(tm, W), _full_spec((1, QL)), _full_spec((1, KVL))],
        out_shape=[jax.ShapeDtypeStruct((T, W), _MXU_DTYPE), jax.ShapeDtypeStruct((1, QL), F32),
                   jax.ShapeDtypeStruct((1, KVL), F32)],
        compiler_params=_params(("arbitrary",), 4 * _nbytes((tm, W), F32), 6 * _nbytes((tm, W), F32)),
    )(pl_, gq, gkv, dcq, dckv, drope)


def _norm_rope(xv, gv, ct, sa, sb):
    r = lax.rsqrt(jnp.sum(xv * xv, axis=-1, keepdims=True) * (1.0 / QK_DIM) + EPS)
    xn = xv * r * gv
    half = ROPE_DIM // 2
    return xn * ct + pltpu.roll(xn, HEAD_W - half, 1) * sa + pltpu.roll(xn, half, 1) * sb


def _norm_rope_bwd(xv, gv, ct, sa, sb, dout):
    half = ROPE_DIM // 2
    dxn = dout * ct + pltpu.roll(dout * sa, half, 1) + pltpu.roll(dout * sb, HEAD_W - half, 1)
    r = lax.rsqrt(jnp.sum(xv * xv, axis=-1, keepdims=True) * (1.0 / QK_DIM) + EPS)
    xh = xv * r
    dxh = dxn * gv
    dx = r * (dxh - xh * (jnp.sum(dxh * xh, axis=-1, keepdims=True) * (1.0 / QK_DIM)))
    return dx, _colsum(dxn * xh)


def _qk_prep_fwd(q_raw, k_raw, pl_, rope_cb, ct, sa, sb, gq, gk):
    T, HW = q_raw.shape
    H = HW // HEAD_W
    tm = _tile(T, _ROW_TILE, _PACKED_ROWS)

    def body(q_ref, k_ref, r_ref, ct_ref, sa_ref, sb_ref, gq_ref, gk_ref, qn_ref, kn_ref):
        ctv, sav, sbv, rv = ct_ref[...], sa_ref[...], sb_ref[...], r_ref[...]
        for h in range(H):
            hs = slice(h * HEAD_W, (h + 1) * HEAD_W)
            qn_ref[:, hs] = _norm_rope(q_ref[:, hs], gq_ref[...], ctv, sav, sbv).astype(qn_ref.dtype)
            kn_ref[:, hs] = _norm_rope(k_ref[:, hs] + rv, gk_ref[...], ctv, sav, sbv).astype(kn_ref.dtype)

    tb = _row_spec(tm, HEAD_W)
    gb = _full_spec((1, HEAD_W))
    return pl.pallas_call(
        body, name="qk_prep_fwd", grid=(T // tm,),
        in_specs=[_row_spec(tm, HW), _row_spec(tm, HW), _row_spec(tm, HEAD_W, rope_cb), tb, tb, tb, gb, gb],
        out_specs=[_row_spec(tm, HW), _row_spec(tm, HW)],
        out_shape=[jax.ShapeDtypeStruct((T, HW), _MXU_DTYPE)] * 2,
        compiler_params=_params(("parallel",), 3 * _nbytes((tm, HW), F32) + 4 * _nbytes((tm, HEAD_W), F32), 8 * _nbytes((tm, HEAD_W), F32)),
    )(q_raw, k_raw, pl_, ct, sa, sb, gq, gk)


def _qk_prep_bwd(q_raw, k_raw, pl_, rope_cb, ct, sa, sb, gq, gk, dqn, dkn):
    T, HW = q_raw.shape
    H = HW // HEAD_W
    tm = _tile(T, _ROW_TILE, _PACKED_ROWS)

    def body(q_ref, k_ref, r_ref, ct_ref, sa_ref, sb_ref, gq_ref, gk_ref, dqn_ref, dkn_ref,
             dq_ref, dk_ref, dr_ref, dgq_ref, dgk_ref):
        _init_acc([dgq_ref, dgk_ref])
        ctv, sav, sbv, rv = ct_ref[...], sa_ref[...], sb_ref[...], r_ref[...]
        dr = dgq_sum = dgk_sum = None
        for h in range(H):
            hs = slice(h * HEAD_W, (h + 1) * HEAD_W)
            dq, dgq = _norm_rope_bwd(q_ref[:, hs], gq_ref[...], ctv, sav, sbv, dqn_ref[:, hs])
            dk, dgk = _norm_rope_bwd(k_ref[:, hs] + rv, gk_ref[...], ctv, sav, sbv, dkn_ref[:, hs])
            dq_ref[:, hs] = dq.astype(dq_ref.dtype)
            dk_ref[:, hs] = dk.astype(dk_ref.dtype)
            dr = dk if h == 0 else dr + dk
            dgq_sum = dgq if h == 0 else dgq_sum + dgq
            dgk_sum = dgk if h == 0 else dgk_sum + dgk
        dr_ref[...] = dr
        dgq_ref[...] += dgq_sum
        dgk_ref[...] += dgk_sum

    tb = _row_spec(tm, HEAD_W)
    gb = _full_spec((1, HEAD_W))
    hb = _row_spec(tm, HW)
    return pl.pallas_call(
        body, name="qk_prep_bwd", grid=(T // tm,),
        in_specs=[hb, hb, _row_spec(tm, HEAD_W, rope_cb), tb, tb, tb, gb, gb, hb, hb],
        out_specs=[hb, hb, tb, gb, gb],
        out_shape=[jax.ShapeDtypeStruct((T, HW), _MXU_DTYPE)] * 2 + [jax.ShapeDtypeStruct((T, HEAD_W), F32)]
                  + [jax.ShapeDtypeStruct((1, HEAD_W), F32)] * 2,
        compiler_params=_params(("arbitrary",), 5 * _nbytes((tm, HW), F32) + 5 * _nbytes((tm, HEAD_W), F32), 12 * _nbytes((tm, HEAD_W), F32)),
    )(q_raw, k_raw, pl_, ct, sa, sb, gq, gk, dqn, dkn)


_NEG = -1e30
_LOG2E = 1.4426950408889634
_SCORE_C = QK_DIM ** -0.5 * _LOG2E


def _pieces(t, rb, diag):
    assert CHUNK % rb == 0 and rb % _SUBLANES == 0
    out = []
    for r in range(t // rb):
        lo = (r * rb // CHUNK) * CHUNK if diag else 0
        for c in range(t // _LANES):
            cut = None if (c + 1) * _LANES <= lo else max(lo - c * _LANES, 0)
            out.append((slice(r * rb, (r + 1) * rb), slice(c * _LANES, (c + 1) * _LANES), c, cut))
    return out


def _groups(x):
    return x.reshape(x.shape[0] // _SUBLANES, _SUBLANES, x.shape[1])


def _all_sublanes(x8, op):
    return jnp.broadcast_to(op(x8, axis=0, keepdims=True), x8.shape)


def _lane_ge(rb, cut):
    return lax.broadcasted_iota(jnp.int32, (rb, _LANES), 1) >= cut


def _attn_fwd(q, k, v):
    T, HW = q.shape
    H = HW // HEAD_W
    t = _tile(T, _ATT_TILE)
    rb = min(_ATT_ROWS, t)
    nc = t // _LANES

    hp = _ATT_HEADS if H % _ATT_HEADS == 0 else 1
    W = hp * HEAD_W

    def body(q_ref, k_ref, v_ref, o_ref, lse_ref, s_s, p_s, m_s, l_s, acc_s):
        i = pl.program_id(1)
        m_s[...] = jnp.full_like(m_s, _NEG)
        l_s[...] = jnp.zeros_like(l_s)
        acc_s[...] = jnp.zeros_like(acc_s)

        def head_step(hh, rows, diag):
            hs = slice(hh * HEAD_W, (hh + 1) * HEAD_W)
            s_s[hh] = _dot(k_ref[rows, hs], q_ref[:, hs], 'nt')
            pieces = _pieces(t, rb, diag)

            def scores(rs, cs, cut):
                sb = s_s[hh, rs, cs]
                return jnp.where(_lane_ge(rb, cut), sb, _NEG) if cut else sb

            mx = [None] * nc
            for rs, cs, c, cut in pieces:
                if cut is not None:
                    g = jnp.max(_groups(scores(rs, cs, cut)), axis=0)
                    mx[c] = g if mx[c] is None else jnp.maximum(mx[c], g)
            m_new, alpha = [], []
            for c in range(nc):
                cs = slice(c * _LANES, (c + 1) * _LANES)
                m_prev = m_s[hh, :, cs]
                m_new.append(jnp.maximum(m_prev, _all_sublanes(mx[c], jnp.max)))
                alpha.append(jnp.exp2((m_prev - m_new[c]) * _SCORE_C))
                m_s[hh, :, cs] = m_new[c]
            lsum = [jnp.zeros((_SUBLANES, _LANES), F32)] * nc
            for rs, cs, c, cut in pieces:
                if cut is None:
                    p_s[hh, rs, cs] = jnp.zeros((rb, _LANES), p_s.dtype)
                    continue
                p = jnp.exp2((scores(rs, cs, cut) - jnp.tile(m_new[c], (rb // _SUBLANES, 1))) * _SCORE_C)
                lsum[c] = lsum[c] + jnp.sum(_groups(p), axis=0)
                p_s[hh, rs, cs] = p.astype(p_s.dtype)
            for c in range(nc):
                cs = slice(c * _LANES, (c + 1) * _LANES)
                l_s[hh, :, cs] = alpha[c] * l_s[hh, :, cs] + _all_sublanes(lsum[c], jnp.sum)
            a = jnp.tile(jnp.concatenate(alpha, axis=1), (HEAD_W // _SUBLANES, 1))
            acc_s[hh] = a * acc_s[hh] + _dot(v_ref[rows, hs], p_s[hh], 'tn')

        def step(j, diag):
            rows = pl.ds(pl.multiple_of(j * t, t), t)
            for hh in range(hp):
                head_step(hh, rows, diag)

        def loop_body(j, carry):
            step(j, False)
            return carry

        lax.fori_loop(0, i, loop_body, 0)
        step(i, True)
        for hh in range(hp):
            l = l_s[hh]
            o_t = acc_s[hh] / jnp.tile(l, (HEAD_W // _SUBLANES, 1))
            o_ref[:, hh * HEAD_W:(hh + 1) * HEAD_W] = o_t.T.astype(o_ref.dtype)
            lse_ref[hh * _SUBLANES:(hh + 1) * _SUBLANES, :] = m_s[hh] * _SCORE_C + jnp.log(l) * _LOG2E

    qb = pl.BlockSpec((t, W), lambda h, i: (i, h))
    kb = pl.BlockSpec((T, W), lambda h, i: (0, h))
    st = pltpu.VMEM((hp, _SUBLANES, t), F32)
    return pl.pallas_call(
        body, name="attn_fwd", grid=(H // hp, T // t),
        in_specs=[qb, kb, kb], out_specs=[qb, pl.BlockSpec((hp * _SUBLANES, t), lambda h, i: (h, i))],
        out_shape=[jax.ShapeDtypeStruct((T, HW), _MXU_DTYPE), jax.ShapeDtypeStruct((H * _SUBLANES, T), F32)],
        scratch_shapes=[pltpu.VMEM((hp, t, t), F32), pltpu.VMEM((hp, t, t), _MXU_DTYPE), st, st, pltpu.VMEM((hp, HEAD_W, t), F32)],
        compiler_params=_params(("parallel", "arbitrary"), 2 * _nbytes((T, W), _MXU_DTYPE) + 3 * _nbytes((t, W), F32),
                                4 * hp * _nbytes((t, t), F32)),
    )(q, k, v)


def _attn_bwd(q, k, v, o, do, lse2):
    T, HW = q.shape
    H = HW // HEAD_W
    t = _tile(T, _ATT_TILE)
    n = T // t
    rb = min(_ATT_ROWS, t)
    scale = QK_DIM ** -0.5

    hp = _ATT_HEADS if H % _ATT_HEADS == 0 else 1
    W = hp * HEAD_W

    def body(k_ref, v_ref, q_ref, o_ref, do_ref, lse_ref, dq_ref, dk_ref, dv_ref,
             s_s, dp_s, p_s, ds_s, dk_s, dv_s, delta_s, dqt_s):
        j = pl.program_id(1)

        @pl.when(j == 0)
        def _():
            dqt_s[...] = jnp.zeros_like(dqt_s)
            for hh in range(hp):
                hs = slice(hh * HEAD_W, (hh + 1) * HEAD_W)
                for b in range(n):
                    rs = slice(b * t, (b + 1) * t)
                    d = jnp.sum(do_ref[rs, hs].astype(F32) * o_ref[rs, hs].astype(F32), axis=1, keepdims=True)
                    delta_s[hh, :, rs] = jnp.broadcast_to(d, (t, HEAD_W)).T[0:_SUBLANES, :]

        dk_s[...] = jnp.zeros_like(dk_s)
        dv_s[...] = jnp.zeros_like(dv_s)

        def head_step(hh, base, diag):
            hs = slice(hh * HEAD_W, (hh + 1) * HEAD_W)
            kv, vv = k_ref[:, hs], v_ref[:, hs]
            qv, dov = q_ref[pl.ds(base, t), hs], do_ref[pl.ds(base, t), hs]
            s_s[hh] = _dot(kv, qv, 'nt')
            dp_s[hh] = _dot(vv, dov, 'nt')
            for rs, cs, c, cut in _pieces(t, rb, diag):
                if cut is None:
                    p_s[hh, rs, cs] = jnp.zeros((rb, _LANES), p_s.dtype)
                    ds_s[hh, rs, cs] = jnp.zeros((rb, _LANES), ds_s.dtype)
                    continue
                lanes = pl.ds(pl.multiple_of(base + c * _LANES, _LANES), _LANES)
                lse = jnp.tile(lse_ref[hh * _SUBLANES:(hh + 1) * _SUBLANES, lanes], (rb // _SUBLANES, 1))
                dl = jnp.tile(delta_s[hh, :, lanes], (rb // _SUBLANES, 1))
                p = jnp.exp2(s_s[hh, rs, cs] * _SCORE_C - lse)
                if cut:
                    p = jnp.where(_lane_ge(rb, cut), p, 0.0)
                p_s[hh, rs, cs] = p.astype(p_s.dtype)
                ds_s[hh, rs, cs] = (p * (dp_s[hh, rs, cs] - dl)).astype(ds_s.dtype)
            dsv = ds_s[hh]
            dv_s[hh] += _dot(p_s[hh], dov, 'nn')
            dk_s[hh] += _dot(dsv, qv, 'nn')
            dqt_s[hh, :, pl.ds(base, t)] += _dot(kv, dsv, 'tn')

        def step(i, diag):
            base = pl.multiple_of(i * t, t)
            for hh in range(hp):
                head_step(hh, base, diag)

        def loop_body(i, carry):
            step(i, False)
            return carry

        step(j, True)
        lax.fori_loop(j + 1, n, loop_body, 0)
        for hh in range(hp):
            hs = slice(hh * HEAD_W, (hh + 1) * HEAD_W)
            dk_ref[:, hs] = dk_s[hh] * scale
            dv_ref[:, hs] = dv_s[hh].astype(dv_ref.dtype)

        @pl.when(j == n - 1)
        def _():
            for hh in range(hp):
                for b in range(n):
                    rs = slice(b * t, (b + 1) * t)
                    dq_ref[rs, hh * HEAD_W:(hh + 1) * HEAD_W] = dqt_s[hh, :, rs].T * scale

    jb = pl.BlockSpec((t, W), lambda h, j: (j, h))
    fb = pl.BlockSpec((T, W), lambda h, j: (0, h))
    sc = pltpu.VMEM((hp, t, HEAD_W), F32)
    return pl.pallas_call(
        body, name="attn_bwd", grid=(H // hp, n),
        in_specs=[jb, jb, fb, fb, fb, pl.BlockSpec((hp * _SUBLANES, T), lambda h, j: (h, 0))], out_specs=[fb, jb, jb],
        out_shape=[jax.ShapeDtypeStruct((T, HW), F32), jax.ShapeDtypeStruct((T, HW), F32),
                   jax.ShapeDtypeStruct((T, HW), _MXU_DTYPE)],
        scratch_shapes=[pltpu.VMEM((hp, t, t), F32), pltpu.VMEM((hp, t, t), F32), pltpu.VMEM((hp, t, t), _MXU_DTYPE),
                        pltpu.VMEM((hp, t, t), _MXU_DTYPE), sc, sc, pltpu.VMEM((hp, _SUBLANES, T), F32),
                        pltpu.VMEM((hp, HEAD_W, T), F32)],
        compiler_params=_params(("parallel", "arbitrary"), 3 * _nbytes((T, W), _MXU_DTYPE) + _nbytes((T, W), F32)
                                + 4 * _nbytes((t, W), F32), hp * (4 * _nbytes((t, t), F32) + _nbytes((T, HEAD_W), F32))),
    )(k, v, q, o, do, lse2)


def _gate_fwd(pg, gb, yc, ym):
    T, D = yc.shape
    tm = _tile(T, _ROW_TILE, _PACKED_ROWS)

    def body(g0_ref, g1_ref, gb_ref, yc_ref, ym_ref, y_ref):
        s0 = _sig(g0_ref[...] + gb_ref[0:1, :])
        s1 = _sig(g1_ref[...] + gb_ref[1:2, :])
        y_ref[...] = (s0 * yc_ref[...] + s1 * ym_ref[...]).astype(y_ref.dtype)

    return pl.pallas_call(
        body, name="gate_fwd", grid=(T // tm,),
        in_specs=[_row_spec(tm, D, 0), _row_spec(tm, D, 1), _full_spec((2, D)), _row_spec(tm, D), _row_spec(tm, D)],
        out_specs=_row_spec(tm, D),
        out_shape=jax.ShapeDtypeStruct((T, D), _MXU_DTYPE),
        compiler_params=_params(("parallel",), 5 * _nbytes((tm, D), F32), 4 * _nbytes((tm, D), F32)),
    )(pg, pg, gb, yc, ym)


def _gate_bwd(pg, gb, yc, ym, dy):
    T, D = yc.shape
    tm = _tile(T, _ROW_TILE, _PACKED_ROWS)

    def body(g0_ref, g1_ref, gb_ref, yc_ref, ym_ref, dy_ref, dyc_ref, dym_ref, dpg_ref, dgb_ref):
        _init_acc([dgb_ref])
        dyv = dy_ref[...]
        s0 = _sig(g0_ref[...] + gb_ref[0:1, :])
        s1 = _sig(g1_ref[...] + gb_ref[1:2, :])
        dyc_ref[...] = (dyv * s0).astype(dyc_ref.dtype)
        dym_ref[...] = (dyv * s1).astype(dym_ref.dtype)
        d0 = dyv * yc_ref[...] * s0 * (1.0 - s0)
        d1 = dyv * ym_ref[...] * s1 * (1.0 - s1)
        dpg_ref[:, :D] = d0.astype(dpg_ref.dtype)
        dpg_ref[:, D:] = d1.astype(dpg_ref.dtype)
        dgb_ref[0:1, :] += _colsum(d0)
        dgb_ref[1:2, :] += _colsum(d1)

    return pl.pallas_call(
        body, name="gate_bwd", grid=(T // tm,),
        in_specs=[_row_spec(tm, D, 0), _row_spec(tm, D, 1), _full_spec((2, D)), _row_spec(tm, D), _row_spec(tm, D),
                  _row_spec(tm, D)],
        out_specs=[_row_spec(tm, D), _row_spec(tm, D), _row_spec(tm, 2 * D), _full_spec((2, D))],
        out_shape=[jax.ShapeDtypeStruct((T, D), _MXU_DTYPE)] * 2 + [jax.ShapeDtypeStruct((T, 2 * D), _MXU_DTYPE),
                                                                   jax.ShapeDtypeStruct((2, D), F32)],
        compiler_params=_params(("arbitrary",), 7 * _nbytes((tm, D), F32), 8 * _nbytes((tm, D), F32)),
    )(pg, pg, gb, yc, ym, dy)


def _place():
    return lax.axis_index("x"), lax.axis_index("y"), lax.axis_index("c")


def _all_gather(name, xs):
    R, C = xs.shape
    MESH = pl.DeviceIdType.MESH

    def body(x_ref, out_ref, send_sems, recv_sems, local_sem):
        x, y, c = _place()
        me, sibling = (x, y, c), (x, y, 1 - c)
        chips = [(1 - x, y), (x, 1 - y), (1 - x, 1 - y)]

        def rows(px, py, pc):
            return out_ref.at[4 * px + 2 * py + pc]

        def copy(k, block, to, src=None):
            return pltpu.make_async_remote_copy(
                src_ref=rows(*block) if src is None else src, dst_ref=rows(*block),
                send_sem=send_sems.at[k], recv_sem=recv_sems.at[k], device_id=to, device_id_type=MESH)

        mine = pltpu.make_async_copy(x_ref, rows(*me), local_sem)
        mine.start()
        first = [copy(0, me, sibling, src=x_ref)]
        first += [copy(1 + j, me, (*chip, c), src=x_ref) for j, chip in enumerate(chips)]
        for cp in first:
            cp.start()
        passed = [copy(4 + j, (*chip, c), sibling) for j, chip in enumerate(chips)]
        for j, chip in enumerate(chips):
            copy(1 + j, (*chip, c), me).wait_recv()
            passed[j].start()
        copy(0, sibling, me).wait_recv()
        for j, chip in enumerate(chips):
            copy(4 + j, (*chip, 1 - c), me).wait_recv()
        for cp in first + passed:
            cp.wait_send()
        mine.wait()

    return pl.pallas_call(
        body, name=name,
        out_shape=jax.ShapeDtypeStruct((N_DEV, R, C), xs.dtype),
        in_specs=[pl.BlockSpec(memory_space=pl.ANY)], out_specs=pl.BlockSpec(memory_space=pl.ANY),
        scratch_shapes=[pltpu.SemaphoreType.DMA((7,)), pltpu.SemaphoreType.DMA((7,)), pltpu.SemaphoreType.DMA],
    )(xs)


_HBM_SPEC = pl.BlockSpec(memory_space=pltpu.HBM)
_SEM_SPEC = pl.BlockSpec(memory_space=pltpu.SEMAPHORE)
_DATAFLOW = pltpu.SideEffectType.DATAFLOW_SIDE_EFFECTING


def _peers():
    x, y, c = _place()
    out = []
    for k in range(1, N_DEV):
        px = 1 - x if k & 4 else x
        py = 1 - y if k & 2 else y
        pc = 1 - c if k & 1 else c
        out.append((k - 1, (px, py, pc), 4 * px + 2 * py + pc))
    return 4 * x + 2 * y + c, out


def _exchange_copies(x_ref, land_ref, send_sems, recv_sems, scatter):
    me, peers = _peers()
    return [pltpu.make_async_remote_copy(
        src_ref=x_ref.at[idx] if scatter else x_ref, dst_ref=land_ref.at[me],
        send_sem=send_sems.at[k], recv_sem=recv_sems.at[k], device_id=dev, device_id_type=pl.DeviceIdType.MESH)
        for k, dev, idx in peers]


def _exchange_start(name, xs, land, scatter):
    def body(x_ref, land_ref, send_sems, recv_sems, x_thru, land_thru, token):
        for cp in _exchange_copies(x_ref, land_ref, send_sems, recv_sems, scatter):
            cp.start()
        token[...] = jnp.zeros_like(token)

    sems = pltpu.SemaphoreType.DMA((N_DEV - 1,))
    return pl.pallas_call(
        body, name=name,
        out_shape=(sems, sems, pltpu.HBM(xs.shape, xs.dtype), pltpu.HBM(land.shape, land.dtype),
                   jax.ShapeDtypeStruct((_SUBLANES, _LANES), F32)),
        in_specs=(_HBM_SPEC, _HBM_SPEC),
        out_specs=(_SEM_SPEC, _SEM_SPEC, _HBM_SPEC, _HBM_SPEC, pl.BlockSpec(memory_space=pltpu.VMEM)),
        input_output_aliases={0: 2, 1: 3},
        compiler_params=pltpu.CompilerParams(has_side_effects=_DATAFLOW),
    )(pltpu.with_memory_space_constraint(xs, pltpu.HBM), pltpu.with_memory_space_constraint(land, pltpu.HBM))


def _exchange_wait(name, send_sems, recv_sems, xs, land, after, scatter):
    def body(x_ref, land_ref, send_sems, recv_sems, after_ref, x_dead, got_ref):
        for cp in _exchange_copies(x_ref, land_ref, send_sems, recv_sems, scatter):
            cp.wait_send()
            cp.wait_recv()

    return pl.pallas_call(
        body, name=name,
        out_shape=(pltpu.HBM(xs.shape, xs.dtype), pltpu.HBM(land.shape, land.dtype)),
        in_specs=(_HBM_SPEC, _HBM_SPEC, _SEM_SPEC, _SEM_SPEC, pl.BlockSpec(memory_space=pl.ANY)),
        out_specs=(_HBM_SPEC, _HBM_SPEC), input_output_aliases={0: 0, 1: 1},
        compiler_params=pltpu.CompilerParams(has_side_effects=_DATAFLOW),
    )(xs, land, send_sems, recv_sems, after)[1]


def _own_slot(block, me):
    land = lax.empty((N_DEV,) + block.shape, block.dtype)
    return lax.dynamic_update_slice(land, block[None], (me,) + (0,) * block.ndim)


def _sum_blocks(name, parts, scale=None):
    n, R, C = parts.shape
    tr = _tile(R, _PACK_ROW_MULT, _PACKED_ROWS)

    def body(p_ref, o_ref):
        acc = p_ref[0].astype(F32)
        for s in range(1, n):
            acc = acc + p_ref[s].astype(F32)
        if scale is not None:
            acc = acc * scale
        o_ref[...] = acc

    return pl.pallas_call(
        body, name=name, grid=(R // tr,),
        in_specs=[pl.BlockSpec((n, tr, C), lambda i: (0, i, 0))], out_specs=pl.BlockSpec((tr, C), lambda i: (i, 0)),
        out_shape=jax.ShapeDtypeStruct((R, C), F32),
        compiler_params=_params(("parallel",), _nbytes((n, tr, C), parts.dtype) + _nbytes((tr, C), F32), 2 * _nbytes((tr, C), F32)),
    )(parts)


def _adamw_math(wv, gv, mv, vv):
    mv = ADAM_B1 * mv + (1.0 - ADAM_B1) * gv
    vv = ADAM_B2 * vv + (1.0 - ADAM_B2) * (gv * gv)
    m_hat = mv / (1.0 - ADAM_B1 ** ADAM_STEP)
    v_hat = vv / (1.0 - ADAM_B2 ** ADAM_STEP)
    return -ADAM_LR * (m_hat / (jnp.sqrt(v_hat) + ADAM_EPS) + ADAM_WD * wv), mv, vv


def _adamw_layer(l, w, g, m, v, outs):
    depth, R, C = w.shape
    tr = _tile(R, _ROW_TILE, _SUBLANES)

    def body(w_ref, g_ref, m_ref, v_ref, *rest):
        go_ref, d_ref, nm_ref, nv_ref = rest[-4:]
        gv = g_ref[...]
        go_ref[...] = gv
        d_ref[...], nm_ref[...], nv_ref[...] = _adamw_math(w_ref[...], gv, m_ref[...], v_ref[...])

    lay = pl.BlockSpec((None, tr, C), lambda i: (l, i, 0))
    shape = jax.ShapeDtypeStruct(w.shape, F32)
    return pl.pallas_call(
        body, name="adamw_layer", grid=(R // tr,),
        in_specs=[lay, pl.BlockSpec((tr, C), lambda i: (i, 0)), lay, lay] + [pl.BlockSpec(memory_space=pl.ANY)] * (4 if outs else 0),
        out_specs=[lay] * 4, out_shape=[shape] * 4,
        input_output_aliases={4 + k: k for k in range(4)} if outs else {},
        compiler_params=_params(("parallel",), 8 * _nbytes((tr, C), F32), 4 * _nbytes((tr, C), F32)),
    )(w, g, m, v, *(outs or ()))


def _adamw(w, g, m, v):
    shape = w.shape
    w2, g2, m2, v2 = (t.reshape(-1, shape[-1]) for t in (w, g, m, v))
    R, C = w2.shape
    tr = _tile(R, _ROW_TILE, _SUBLANES)

    def body(w_ref, g_ref, m_ref, v_ref, d_ref, nm_ref, nv_ref):
        d_ref[...], nm_ref[...], nv_ref[...] = _adamw_math(w_ref[...], g_ref[...], m_ref[...], v_ref[...])

    spec = pl.BlockSpec((tr, C), lambda i: (i, 0))
    outs = pl.pallas_call(
        body, name="adamw", grid=(R // tr,),
        in_specs=[spec] * 4, out_specs=[spec] * 3,
        out_shape=[jax.ShapeDtypeStruct((R, C), F32)] * 3,
        compiler_params=_params(("parallel",), 7 * _nbytes((tr, C), F32), 4 * _nbytes((tr, C), F32)),
    )(w2, g2, m2, v2)
    return tuple(o.reshape(shape) for o in outs)


def _pack_rows(parts, lead=()):
    out, spans, r0 = [], [], 0
    unit = _PACKED_ROWS * _PACK_W
    for p in parts:
        flat = p.reshape(lead + (-1,))
        size = flat.shape[-1]
        pad = (-size) % unit
        if pad:
            flat = jnp.pad(flat, [(0, 0)] * len(lead) + [(0, pad)])
        rows = (size + pad) // _PACK_W
        out.append(flat.reshape(lead + (rows, _PACK_W)))
        spans.append((r0, rows, p.shape[len(lead):]))
        r0 += rows
    pad = (-r0) % _PACK_ROW_MULT
    if pad:
        out.append(jnp.zeros(lead + (pad, _PACK_W), out[0].dtype))
    return jnp.concatenate(out, axis=len(lead)), spans


def _unpack_rows(packed, span, lead=()):
    r0, rows, shape = span
    size = 1
    for s in shape:
        size *= s
    sl = lax.slice_in_dim(packed, r0, r0 + rows, axis=len(lead))
    return sl.reshape(lead + (-1,))[..., :size].reshape(lead + tuple(shape))


def _pack_flat(parts):
    flat, spans, at = [], [], 0
    for p in parts:
        size = p.size
        padded = size + (-size) % _LANES
        flat.append(jnp.pad(p.reshape(-1), (0, padded - size)))
        spans.append((at, size, p.shape))
        at += padded
    tail = (-at) % (_PACKED_ROWS * _PACK_W)
    if tail:
        flat.append(jnp.zeros((tail,), flat[0].dtype))
    return jnp.concatenate(flat).reshape(-1, _PACK_W), spans


def _unpack_flat(packed, span, lead=()):
    at, size, shape = span
    return lax.slice_in_dim(packed.reshape(lead + (-1,)), at, at + size, axis=len(lead)).reshape(lead + tuple(shape))


def _rope_tables(positions, T):
    inv_freq = ROPE_THETA ** (-jnp.arange(0, ROPE_DIM, 2, dtype=F32) / ROPE_DIM)
    ang = positions.reshape(T, 1).astype(F32) * inv_freq
    cos, sin = jnp.cos(ang), jnp.sin(ang)
    half = ROPE_DIM // 2
    z = lambda n: jnp.zeros((T, n), F32)
    tail = HEAD_W - QK_DIM
    ct = jnp.concatenate([jnp.ones((T, NOPE_DIM), F32), cos, cos, z(tail)], axis=1)
    sa = jnp.concatenate([z(NOPE_DIM), -sin, z(half), z(tail)], axis=1)
    sb = jnp.concatenate([z(NOPE_DIM), z(half), sin, z(tail)], axis=1)
    return ct, sa, sb


def _pad_heads(wt, per_head, keep_from=0, keep=None):
    K = wt.shape[1]
    keep = per_head if keep is None else keep
    w3 = wt.reshape(N_HEADS, per_head, K)[:, keep_from:keep_from + keep]
    return jnp.pad(w3, ((0, 0), (0, HEAD_W - keep), (0, 0))).reshape(N_HEADS * HEAD_W, K)


def _unpad_heads(g, keep):
    return g.reshape(N_HEADS, HEAD_W, g.shape[1])[:, :keep]


def _layer_weights(W):
    D = W['w_out'].shape[1]
    C = W['w_conv_out'].shape[1]
    QL, KVL = W['w_uq'].shape[1], W['w_ukv'].shape[1]
    o1 = 2 * C
    o2, o3 = o1 + QL, o1 + QL + KVL
    o4 = o3 + ROPE_DIM
    win = W['w_in']
    zr = lambda n: jnp.zeros((n, D), win.dtype)
    L = dict(W)
    L['w_c'] = win[:o1]
    L['w_l'] = jnp.concatenate([win[o1:o3], zr(NOPE_DIM), win[o3:o4], zr(HEAD_W - QK_DIM)], axis=0)
    L['w_g'] = win[o4:]
    L['w_q'] = _pad_heads(W['w_uq'], QK_DIM)
    L['w_k'] = _pad_heads(W['w_ukv'], NOPE_DIM + V_DIM, 0, NOPE_DIM)
    L['w_v'] = _pad_heads(W['w_ukv'], NOPE_DIM + V_DIM, NOPE_DIM, V_DIM)
    wmo = W['w_mla_out'].reshape(D, N_HEADS, V_DIM)
    L['w_mo'] = jnp.pad(wmo, ((0, 0), (0, 0), (0, HEAD_W - V_DIM))).reshape(D, N_HEADS * HEAD_W)
    L['dims'] = (D, C, QL, KVL)
    return L


def _row(v, width=None):
    v = v.reshape(1, -1)
    if width is not None and v.shape[1] < width:
        v = jnp.pad(v, ((0, 0), (0, width - v.shape[1])))
    return v


def _ffn_fwd(x, g, wg, wu, wd):
    h = _rms_fwd(x, _row(g))
    a, b, u = _ffn_up(h, wg, wu)
    x_out = _mm("ffn_down", u, wd, 'nn', res=x, scale=0.5)
    return x_out, (x, h, a, b, u)


def _ffn_bwd(dx, saved, g, wg, wu, wd, after):
    x, h, a, b, u = saved
    d_wd = _wgrad("ffn_dwd", u, dx, scale=0.5)
    da, db = _ffn_bwd_up(dx, wd, a, b, after)
    d_wg = _wgrad("ffn_dwg", da, h)
    d_wu = _wgrad("ffn_dwg", db, h)
    dx_in, dg = _dh_rms_bwd("ffn_dh_rms", [(da, wg), (db, wu)], x, _row(g), dx)
    return dx_in, dg, d_wg, d_wu, d_wd


def _layer_fwd(x, L, S, tabs):
    D, C, QL, KVL = L['dims']
    rope_cb = (QL + KVL) // HEAD_W
    x1, ffn1 = _ffn_fwd(x, S['ffn1_norm'], L['ffn1_w_gate'], L['ffn1_w_up'], L['ffn1_w_down'])
    h = _rms_fwd(x1, _row(S['mix_norm']))
    pc = _mm("proj_c", h, L['w_c'], 'nt')
    pl_ = _mm("proj_l", h, L['w_l'], 'nt')
    pg = _mm("proj_g", h, L['w_g'], 'nt')
    conv_w = jnp.pad(S['conv_w'], ((0, HALO - CONV_WIDTH), (0, 0)))
    z, c = _conv_fwd(pc, conv_w, _row(S['conv_b']), _row(S['conv_ln_g']), _row(S['conv_ln_b']))
    yc = _mm("conv_out", c, L['w_conv_out'], 'nt')
    cqn, ckvn = _lat_norm_fwd(pl_, _row(S['cq_norm']), _row(S['ckv_norm']), QL, KVL)
    q_raw = _mm("up_q", cqn, L['w_q'], 'nt')
    k_raw = _mm("up_kv", ckvn, L['w_k'], 'nt')
    v = _mm("up_kv", ckvn, L['w_v'], 'nt', out_dtype=_MXU_DTYPE)
    gq, gk = _row(S['q_norm'], HEAD_W), _row(S['k_norm'], HEAD_W)
    qn, kn = _qk_prep_fwd(q_raw, k_raw, pl_, rope_cb, *tabs, gq, gk)
    o, lse = _attn_fwd(qn, kn, v)
    ym = _mm("mla_out", o, L['w_mo'], 'nt')
    y = _gate_fwd(pg, S['gate_bias'], yc, ym)
    x2 = _mm("mix_out", y, L['w_out'], 'nn', res=x1)
    x3, ffn2 = _ffn_fwd(x2, S['ffn2_norm'], L['ffn2_w_gate'], L['ffn2_w_up'], L['ffn2_w_down'])
    saved = dict(ffn1=ffn1, ffn2=ffn2, x1=x1, h=h, pc=pc, pl=pl_, pg=pg, z=z, c=c, yc=yc, cqn=cqn, ckvn=ckvn,
                 q_raw=q_raw, k_raw=k_raw, v=v, qn=qn, kn=kn, o=o, lse=lse, ym=ym, y=y, conv_w=conv_w, gq=gq, gk=gk)
    return x3, saved


def _layer_bwd_ffn1(dx, L, S, A, after):
    G, g = {}, {}
    dx, g['ffn1_norm'], G['ffn1_w_gate'], G['ffn1_w_up'], G['ffn1_w_down'] = _ffn_bwd(
        dx, A['ffn1'], S['ffn1_norm'], L['ffn1_w_gate'], L['ffn1_w_up'], L['ffn1_w_down'], after)
    return dx, G, g


def _layer_bwd_mix(dx, L, S, tabs, A, after):
    D, C, QL, KVL = L['dims']
    rope_cb = (QL + KVL) // HEAD_W
    G, g = {}, {}
    dx, g['ffn2_norm'], G['ffn2_w_gate'], G['ffn2_w_up'], G['ffn2_w_down'] = _ffn_bwd(
        dx, A['ffn2'], S['ffn2_norm'], L['ffn2_w_gate'], L['ffn2_w_up'], L['ffn2_w_down'], after)
    G['w_out'] = _wgrad("mix_dwout", A['y'], dx)
    dy = _mm("mix_dy", dx, L['w_out'], 'nt')
    dyc, dym, dpg, g['gate_bias'] = _gate_bwd(A['pg'], S['gate_bias'], A['yc'], A['ym'], dy)
    G['w_conv_out'] = _wgrad("conv_dwout", dyc, A['c'])
    dc = _mm("conv_dc", dyc, L['w_conv_out'], 'nn')
    dpc, dcw, g['conv_b'], g['conv_ln_g'], g['conv_ln_b'] = _conv_bwd(
        A['pc'], A['z'], dc, A['conv_w'], _row(S['conv_ln_g']), _row(S['conv_ln_b']))
    g['conv_w'] = dcw[:CONV_WIDTH]
    d_wmo = _wgrad("mla_dwout", dym, A['o'])
    G['w_mla_out'] = d_wmo.reshape(D, N_HEADS, HEAD_W)[:, :, :V_DIM].reshape(D, N_HEADS * V_DIM)
    do = _mm("mla_do", dym, L['w_mo'], 'nn', out_dtype=_MXU_DTYPE)
    dqn, dkn, dv = _attn_bwd(A['qn'], A['kn'], A['v'], A['o'], do, A['lse'])
    dq_raw, dk_raw, drope, dgq, dgk = _qk_prep_bwd(A['q_raw'], A['k_raw'], A['pl'], rope_cb, *tabs, A['gq'], A['gk'], dqn, dkn)
    g['q_norm'], g['k_norm'] = dgq[:, :QK_DIM], dgk[:, :QK_DIM]
    G['w_uq'] = _unpad_heads(_wgrad("up_dwq", dq_raw, A['cqn']), QK_DIM).reshape(N_HEADS * QK_DIM, QL)
    d_wk = _unpad_heads(_wgrad("up_dwkv", dk_raw, A['ckvn']), NOPE_DIM)
    d_wv = _unpad_heads(_wgrad("up_dwkv", dv, A['ckvn']), V_DIM)
    G['w_ukv'] = jnp.concatenate([d_wk, d_wv], axis=1).reshape(N_HEADS * (NOPE_DIM + V_DIM), KVL)
    dcq = _mm("up_dcq", dq_raw, L['w_q'], 'nn')
    dckv = _mm("up_dckv_k", dk_raw, L['w_k'], 'nn')
    dckv = _mm("up_dckv_v", dv, L['w_v'], 'nn', res=dckv)
    dpl, g['cq_norm'], g['ckv_norm'] = _lat_norm_bwd(A['pl'], _row(S['cq_norm']), _row(S['ckv_norm']), dcq, dckv, drope, QL, KVL)
    d_wc = _wgrad("proj_dwc", dpc, A['h'])
    d_wl = _wgrad("proj_dwl", dpl, A['h'])
    d_wg = _wgrad("proj_dwg", dpg, A['h'])
    ql = QL + KVL
    G['w_in'] = jnp.concatenate([d_wc, d_wl[:ql], d_wl[ql + NOPE_DIM:ql + NOPE_DIM + ROPE_DIM], d_wg], axis=0)
    dx, g['mix_norm'] = _dh_rms_bwd("proj_dh_rms", [(dpc, L['w_c']), (dpl, L['w_l']), (dpg, L['w_g'])],
                                    A['x1'], _row(S['mix_norm']), dx)
    return dx, G, g


def kernel(x, positions, ffn1_norm, ffn1_w_gate, ffn1_w_up, ffn1_w_down, mix_norm, w_in, gate_bias, conv_w, conv_b, conv_ln_g, conv_ln_b, w_conv_out, cq_norm, ckv_norm, w_uq, w_ukv, q_norm, k_norm, w_mla_out, w_out, ffn2_norm, ffn2_w_gate, ffn2_w_up, ffn2_w_down, loss_target, m_ffn1_norm, m_ffn1_w_gate, m_ffn1_w_up, m_ffn1_w_down, m_mix_norm, m_w_in, m_gate_bias, m_conv_w, m_conv_b, m_conv_ln_g, m_conv_ln_b, m_w_conv_out, m_cq_norm, m_ckv_norm, m_w_uq, m_w_ukv, m_q_norm, m_k_norm, m_w_mla_out, m_w_out, m_ffn2_norm, m_ffn2_w_gate, m_ffn2_w_up, m_ffn2_w_down, v_ffn1_norm, v_ffn1_w_gate, v_ffn1_w_up, v_ffn1_w_down, v_mix_norm, v_w_in, v_gate_bias, v_conv_w, v_conv_b, v_conv_ln_g, v_conv_ln_b, v_w_conv_out, v_cq_norm, v_ckv_norm, v_w_uq, v_w_ukv, v_q_norm, v_k_norm, v_w_mla_out, v_w_out, v_ffn2_norm, v_ffn2_w_gate, v_ffn2_w_up, v_ffn2_w_down):
    w = dict(zip(WEIGHTS, (ffn1_norm, ffn1_w_gate, ffn1_w_up, ffn1_w_down, mix_norm, w_in, gate_bias, conv_w, conv_b, conv_ln_g, conv_ln_b, w_conv_out, cq_norm, ckv_norm, w_uq, w_ukv, q_norm, k_norm, w_mla_out, w_out, ffn2_norm, ffn2_w_gate, ffn2_w_up, ffn2_w_down)))
    m = dict(zip(WEIGHTS, (m_ffn1_norm, m_ffn1_w_gate, m_ffn1_w_up, m_ffn1_w_down, m_mix_norm, m_w_in, m_gate_bias, m_conv_w, m_conv_b, m_conv_ln_g, m_conv_ln_b, m_w_conv_out, m_cq_norm, m_ckv_norm, m_w_uq, m_w_ukv, m_q_norm, m_k_norm, m_w_mla_out, m_w_out, m_ffn2_norm, m_ffn2_w_gate, m_ffn2_w_up, m_ffn2_w_down)))
    v = dict(zip(WEIGHTS, (v_ffn1_norm, v_ffn1_w_gate, v_ffn1_w_up, v_ffn1_w_down, v_mix_norm, v_w_in, v_gate_bias, v_conv_w, v_conv_b, v_conv_ln_g, v_conv_ln_b, v_w_conv_out, v_cq_norm, v_ckv_norm, v_w_uq, v_w_ukv, v_q_norm, v_k_norm, v_w_mla_out, v_w_out, v_ffn2_norm, v_ffn2_w_gate, v_ffn2_w_up, v_ffn2_w_down)))
    depth = ffn1_norm.shape[0]
    T, D = x.shape[1], x.shape[2]
    xs = x.reshape(T, D)
    target = loss_target.reshape(T, D)
    tabs = _rope_tables(positions, T)
    me = 4 * lax.axis_index("x") + 2 * lax.axis_index("y") + lax.axis_index("c")

    big = list(BIG)
    packed = []
    for l in range(depth):
        p, spans = _pack_rows([(w[n][l].T if BIG[n] else w[n][l]).astype(_WIRE_DTYPE) for n in big])
        packed.append(p)
    gathered = {0: _all_gather("gather_weights", packed[0])}
    in_flight = {l: _exchange_start(f"gather_start_{l}", packed[l], _own_slot(packed[l], me), False)
                 for l in range(1, depth)}
    small_names = [(l, n) for l in range(depth) for n in SMALL_SHARDED]
    small_packed, small_spans = _pack_flat([w[n][l].T for l, n in small_names])
    small_gathered = _all_gather("gather_small", small_packed)
    for st in in_flight.values():
        xs, _ = lax.optimization_barrier((xs, st[4]))

    def layer_weights(l):
        W = {}
        for n, span in zip(big, spans):
            sh = _unpack_rows(gathered[l], span, (N_DEV,))
            W[n] = sh.reshape(N_DEV * sh.shape[1], sh.shape[2])
        S = {n: w[n][l] for n in REPLICATED}
        for (ll, n), span in zip(small_names, small_spans):
            if ll == l:
                sh = _unpack_flat(small_gathered, span, (N_DEV,))
                S[n] = sh.reshape(N_DEV * sh.shape[1], sh.shape[2]).T
        return _layer_weights(W), S

    layers, acts = [], []
    hcur = xs
    for l in range(depth):
        if l > 0:
            send_sems, recv_sems, src, land, _ = in_flight[l]
            gathered[l] = _exchange_wait(f"gather_wait_{l}", send_sems, recv_sems, src, land, hcur, False)
        layers.append(layer_weights(l))
        hcur, saved = _layer_fwd(hcur, *layers[l], tabs)
        acts.append(saved)
    dx, loss_part = _loss_grad(hcur, target)

    stage_names = [[n for n in big if n.startswith('ffn1_')], [n for n in big if not n.startswith('ffn1_')]]
    small_grads, reduced, gspans, in_flight = [{} for _ in range(depth)], {}, {}, {}
    after = jnp.zeros((_SUBLANES, _LANES), F32)
    for l in reversed(range(depth)):
        for stage in (1, 0):
            if stage:
                dx, G, g = _layer_bwd_mix(dx, *layers[l], tabs, acts[l], after)
            else:
                dx, G, g = _layer_bwd_ffn1(dx, *layers[l], acts[l], after)
            small_grads[l].update(g)
            blocks = [G[n].reshape(N_DEV, -1, G[n].shape[1]) for n in stage_names[stage]]
            send, gspans[stage] = _pack_rows(blocks, (N_DEV,))
            own = lax.dynamic_index_in_dim(send, me, 0, keepdims=False)
            in_flight[l, stage] = _exchange_start(f"scatter_start_{l}_{stage}", send, _own_slot(own, me), True)
            after = in_flight[l, stage][4]
    small_list = [(l, n) for l in range(depth) for n in REPLICATED + SMALL_SHARDED]
    loss_part = loss_part + after[:1, :]
    small_send, sspans = _pack_flat([small_grads[l][n] for l, n in small_list] + [loss_part])
    small_flight = _exchange_start("small_grads_start", small_send, _own_slot(small_send, me), False)

    out4 = {}
    after = small_flight[4]
    for (l, stage), (send_sems, recv_sems, src, land, _) in in_flight.items():
        got = _exchange_wait(f"scatter_wait_{l}_{stage}", send_sems, recv_sems, src, land, after, True)
        reduced = _sum_blocks("sum_grads_direct", got)
        for n, span in zip(stage_names[stage], gspans[stage]):
            gt = _unpack_rows(reduced, span)
            out4[n] = _adamw_layer(l, w[n], gt.T if BIG[n] else gt, m[n], v[n], out4.get(n))
            after = out4[n][1]
    send_sems, recv_sems, src, land, _ = small_flight
    small_all = _sum_blocks("sum_small", _exchange_wait("small_grads_wait", send_sems, recv_sems, src, land, after, False))
    loss = _unpack_flat(small_all, sspans[-1])[0, 0]
    for n in WEIGHTS:
        if n in BIG:
            continue
        per_layer = []
        for l in range(depth):
            gs = _unpack_flat(small_all, sspans[small_list.index((l, n))])
            if n in SMALL_SHARDED:
                cols = w[n].shape[-1]
                gs = lax.dynamic_slice_in_dim(gs, me * cols, cols, axis=1)
            per_layer.append(gs.reshape(w[n].shape[1:]))
        g = jnp.stack(per_layer, axis=0)
        out4[n] = (g,) + _adamw(w[n], g, m[n], v[n])
    return (loss, dx.reshape(1, T, D), *[out4[n][k] for k in range(4) for n in WEIGHTS])
```

```python
import jax
import jax.numpy as jnp
from jax import lax
from jax.experimental import pallas as pl
from jax.experimental.pallas import tpu as pltpu

F32 = jnp.float32
_MXU_DTYPE = jnp.bfloat16
_WIRE_DTYPE = jnp.bfloat16

_LANES = 128
_SUBLANES = 8
_PACKED_ROWS = 16
_V7X_VMEM_BYTES = 64 * 2 ** 20
_VMEM_HEADROOM = 8 * 2 ** 20

N_DEV = 8
EPS = 1e-6
CHUNK = 64
CONV_WIDTH = 31
N_HEADS = 8
NOPE_DIM = 64
ROPE_DIM = 32
QK_DIM = NOPE_DIM + ROPE_DIM
V_DIM = 64
HEAD_W = _LANES
ROPE_THETA = 10000.0
HALO = 32

ADAM_LR = 0.001
ADAM_B1 = 0.9
ADAM_B2 = 0.999
ADAM_EPS = 1e-08
ADAM_WD = 0.01
ADAM_STEP = 10

_ROW_TILE = 512
_MM_TILE = 512
_FFN_TILE = 512
_MM_TILE_N = 1408
_MM_TILE_K = 1024
_TN_TILE_M = 1408
_ATT_TILE = 512
_ATT_ROWS = 32
_ATT_HEADS = 2
_PACK_W = 1024
_PACK_ROW_MULT = 128

WEIGHTS = ['ffn1_norm', 'ffn1_w_gate', 'ffn1_w_up', 'ffn1_w_down', 'mix_norm', 'w_in', 'gate_bias',
           'conv_w', 'conv_b', 'conv_ln_g', 'conv_ln_b', 'w_conv_out', 'cq_norm', 'ckv_norm', 'w_uq',
           'w_ukv', 'q_norm', 'k_norm', 'w_mla_out', 'w_out', 'ffn2_norm', 'ffn2_w_gate', 'ffn2_w_up',
           'ffn2_w_down']
BIG = {'ffn1_w_gate': True, 'ffn1_w_up': True, 'ffn1_w_down': False, 'w_in': True, 'w_conv_out': True,
       'w_uq': True, 'w_ukv': True, 'w_mla_out': True, 'w_out': False, 'ffn2_w_gate': True,
       'ffn2_w_up': True, 'ffn2_w_down': False}
SMALL_SHARDED = ['conv_w', 'gate_bias']
REPLICATED = ['ffn1_norm', 'mix_norm', 'conv_b', 'conv_ln_g', 'conv_ln_b', 'cq_norm', 'ckv_norm',
              'q_norm', 'k_norm', 'ffn2_norm']


def _tile(n, target, mult=_LANES):
    if n <= target:
        return n
    for d in range(target - target % mult, 0, -mult):
        if n % d == 0:
            return d
    return n


def _nbytes(shape, dtype):
    n = 1
    for s in shape:
        n *= s
    return n * jnp.dtype(dtype).itemsize


def _params(dims, block_bytes, extra_bytes=0):
    need = 2 * block_bytes + extra_bytes + _VMEM_HEADROOM
    assert need <= _V7X_VMEM_BYTES, (need, dims)
    return pltpu.CompilerParams(dimension_semantics=dims, vmem_limit_bytes=_V7X_VMEM_BYTES - _VMEM_HEADROOM)


def _mxu(x, scale=None):
    if scale is not None:
        x = x * scale
    return x if x.dtype == _MXU_DTYPE else x.astype(_MXU_DTYPE)


def _dot(a, b, mode):
    dims = {'nn': (((1,), (0,)), ((), ())), 'nt': (((1,), (1,)), ((), ())), 'tn': (((0,), (0,)), ((), ()))}[mode]
    return lax.dot_general(a, b, dims, preferred_element_type=F32)


def _colsum(x):
    return jnp.sum(x, axis=0, keepdims=True)


def _sig(x):
    return 0.5 * jnp.tanh(0.5 * x) + 0.5


def _mm(name, a, b, mode, *, out_dtype=F32, res=None, scale=None, a_scale=None, tm=None, tn=None, tk=None):
    if mode == 'nn':
        (M, K), (_, N) = a.shape, b.shape
    elif mode == 'nt':
        (M, K), (N, _) = a.shape, b.shape
    else:
        (K, M), (_, N) = a.shape, b.shape
    tm = _tile(M, tm or (_TN_TILE_M if mode == 'tn' else _MM_TILE))
    tn = _tile(N, tn or _MM_TILE_N)
    tk = _tile(K, tk or (_MM_TILE_K if mode == 'tn' else 4096))
    nk = K // tk
    grid = (M // tm, N // tn, nk)
    a_spec = pl.BlockSpec((tk, tm), lambda i, j, k: (k, i)) if mode == 'tn' else pl.BlockSpec((tm, tk), lambda i, j, k: (i, k))
    b_spec = pl.BlockSpec((tn, tk), lambda i, j, k: (j, k)) if mode == 'nt' else pl.BlockSpec((tk, tn), lambda i, j, k: (k, j))
    o_spec = pl.BlockSpec((tm, tn), lambda i, j, k: (i, j))
    has_res = res is not None

    def body(*refs):
        a_ref, b_ref = refs[0], refs[1]
        res_ref = refs[2] if has_res else None
        o_ref = refs[3] if has_res else refs[2]
        acc_ref = refs[-1] if nk > 1 else None

        def finish(p):
            if scale is not None:
                p = p * scale
            if has_res:
                p = res_ref[...] + p
            o_ref[...] = p.astype(o_ref.dtype)

        p = _dot(_mxu(a_ref[...], a_scale), _mxu(b_ref[...]), mode)
        if nk == 1:
            finish(p)
        else:
            k = pl.program_id(2)

            @pl.when(k == 0)
            def _():
                acc_ref[...] = p

            @pl.when(k > 0)
            def _():
                acc_ref[...] += p

            @pl.when(k == nk - 1)
            def _():
                finish(acc_ref[...])

    blocks = (_nbytes((tm, tk), a.dtype) + _nbytes((tk, tn), b.dtype) + _nbytes((tm, tn), out_dtype)
              + (_nbytes((tm, tn), F32) if has_res else 0))
    extra = ((2 + (nk > 1)) * _nbytes((tm, tn), F32) + (a.dtype != _MXU_DTYPE) * _nbytes((tm, tk), _MXU_DTYPE)
             + (b.dtype != _MXU_DTYPE) * _nbytes((tk, tn), _MXU_DTYPE))
    return pl.pallas_call(
        body, name=name, grid=grid,
        in_specs=[a_spec, b_spec] + ([o_spec] if has_res else []),
        out_specs=o_spec,
        out_shape=jax.ShapeDtypeStruct((M, N), out_dtype),
        scratch_shapes=[pltpu.VMEM((tm, tn), F32)] if nk > 1 else [],
        compiler_params=_params(("parallel", "parallel", "arbitrary"), blocks, extra),
    )(*([a, b] + ([res] if has_res else [])))


def _wgrad(name, a, b, scale=None):
    return _mm(name, a, b, 'tn', out_dtype=_WIRE_DTYPE, scale=scale)


def _col_chunks(n, width=2 * _LANES):
    return [slice(c, min(c + width, n)) for c in range(0, n, width)]


def _ffn_up(h, wg, wu):
    T, D = h.shape
    F = wg.shape[0]
    tm, tn = _tile(T, _FFN_TILE), _tile(F, _MM_TILE_N)

    def body(h_ref, wg_ref, wu_ref, a_ref, b_ref, u_ref):
        hv = h_ref[...]
        for cols in _col_chunks(tn):
            a = _dot(hv, wg_ref[cols, :], 'nt')
            b = _dot(hv, wu_ref[cols, :], 'nt')
            a_ref[:, cols] = a
            b_ref[:, cols] = b
            u_ref[:, cols] = (a * _sig(a) * b).astype(u_ref.dtype)

    w_spec = pl.BlockSpec((tn, D), lambda j, i: (j, 0))
    o_spec = pl.BlockSpec((tm, tn), lambda j, i: (i, j))
    blocks = _nbytes((tm, D), h.dtype) + 2 * _nbytes((tn, D), wg.dtype) + 2 * _nbytes((tm, tn), F32) + _nbytes((tm, tn), _MXU_DTYPE)
    return pl.pallas_call(
        body, name="ffn_up", grid=(F // tn, T // tm),
        in_specs=[pl.BlockSpec((tm, D), lambda j, i: (i, 0)), w_spec, w_spec],
        out_specs=[o_spec, o_spec, o_spec],
        out_shape=[jax.ShapeDtypeStruct((T, F), F32), jax.ShapeDtypeStruct((T, F), F32),
                   jax.ShapeDtypeStruct((T, F), _MXU_DTYPE)],
        compiler_params=_params(("parallel", "parallel"), blocks, 4 * _nbytes((tm, tn), F32)),
    )(h, wg, wu)


def _ffn_bwd_up(dx, wd, a, b, after):
    T, D = dx.shape
    F = wd.shape[0]
    tm, tn = _tile(T, _FFN_TILE), _tile(F, _MM_TILE_N)

    def body(dx_ref, wd_ref, a_ref, b_ref, after_ref, da_ref, db_ref):
        dy = _mxu(dx_ref[...], 0.5)
        for cols in _col_chunks(tn):
            du = _dot(dy, wd_ref[cols, :], 'nt')
            av, bv = a_ref[:, cols], b_ref[:, cols]
            s = _sig(av)
            db_ref[:, cols] = (du * av * s).astype(db_ref.dtype)
            da_ref[:, cols] = (du * bv * (s * (1.0 + av * (1.0 - s)))).astype(da_ref.dtype)

    t_spec = pl.BlockSpec((tm, tn), lambda j, i: (i, j))
    blocks = _nbytes((tm, D), F32) + _nbytes((tn, D), wd.dtype) + 2 * _nbytes((tm, tn), F32) + 2 * _nbytes((tm, tn), _MXU_DTYPE)
    return pl.pallas_call(
        body, name="ffn_bwd_up", grid=(F // tn, T // tm),
        in_specs=[pl.BlockSpec((tm, D), lambda j, i: (i, 0)), pl.BlockSpec((tn, D), lambda j, i: (j, 0)), t_spec, t_spec,
                  pl.BlockSpec(memory_space=pl.ANY)],
        out_specs=[t_spec, t_spec],
        out_shape=[jax.ShapeDtypeStruct((T, F), _MXU_DTYPE)] * 2,
        compiler_params=_params(("parallel", "parallel"), blocks, 5 * _nbytes((tm, tn), F32)),
    )(dx, wd, a, b, after)


def _row_spec(tm, w, cb=0):
    return pl.BlockSpec((tm, w), lambda i: (i, cb))


def _full_spec(shape):
    return pl.BlockSpec(shape, lambda i: (0,) * len(shape))


def _init_acc(refs):
    @pl.when(pl.program_id(0) == 0)
    def _():
        for r in refs:
            r[...] = jnp.zeros_like(r)


def _rms_fwd(x, g):
    T, D = x.shape
    tm = _tile(T, _ROW_TILE, _PACKED_ROWS)

    def body(x_ref, g_ref, h_ref):
        xv = x_ref[...]
        r = lax.rsqrt(jnp.mean(xv * xv, axis=-1, keepdims=True) + EPS)
        h_ref[...] = (xv * r * g_ref[...]).astype(h_ref.dtype)

    return pl.pallas_call(
        body, name="rms_fwd", grid=(T // tm,),
        in_specs=[_row_spec(tm, D), _full_spec((1, D))], out_specs=_row_spec(tm, D),
        out_shape=jax.ShapeDtypeStruct((T, D), _MXU_DTYPE),
        compiler_params=_params(("parallel",), 2 * _nbytes((tm, D), F32), 4 * _nbytes((tm, D), F32)),
    )(x, g)


def _dh_rms_bwd(name, pairs, x, g, dres, after):
    T, D = x.shape
    tm = _tile(T, _ROW_TILE // 2, _PACKED_ROWS)
    n = len(pairs)

    def body(*refs):
        x_ref, g_ref, dres_ref, after_ref, dx_ref, dg_ref = refs[2 * n:]
        _init_acc([dg_ref])
        dh = None
        for k in range(n):
            p = _dot(_mxu(refs[2 * k][...]), refs[2 * k + 1][...], 'nn')
            dh = p if dh is None else dh + p
        xv = x_ref[...]
        r = lax.rsqrt(jnp.mean(xv * xv, axis=-1, keepdims=True) + EPS)
        xh = xv * r
        dxh = dh * g_ref[...]
        dx_ref[...] = dres_ref[...] + r * (dxh - xh * jnp.mean(dxh * xh, axis=-1, keepdims=True))
        dg_ref[...] += _colsum(dh * xh)

    in_specs, args, blocks = [], [], 3 * _nbytes((tm, D), F32)
    for a, w in pairs:
        in_specs += [_row_spec(tm, a.shape[1]), _full_spec(w.shape)]
        args += [a, w]
        blocks += _nbytes((tm, a.shape[1]), a.dtype) + _nbytes(w.shape, w.dtype)
    return pl.pallas_call(
        body, name=name, grid=(T // tm,),
        in_specs=in_specs + [_row_spec(tm, D), _full_spec((1, D)), _row_spec(tm, D), pl.BlockSpec(memory_space=pl.ANY)],
        out_specs=[_row_spec(tm, D), _full_spec((1, D))],
        out_shape=[jax.ShapeDtypeStruct((T, D), F32), jax.ShapeDtypeStruct((1, D), F32)],
        compiler_params=_params(("arbitrary",), blocks, 6 * _nbytes((tm, D), F32)),
    )(*args, x, g, dres, after)


def _loss_grad(y, target):
    T, D = y.shape
    tm = _tile(T, _ROW_TILE, _SUBLANES)

    def body(y_ref, t_ref, dy_ref, l_ref):
        _init_acc([l_ref])
        err = y_ref[...] - t_ref[...]
        dy_ref[...] = err * (1.0 / D)
        per_row = jnp.sum(err * err, axis=-1, keepdims=True) * (0.5 / D)
        l_ref[...] += jnp.sum(per_row, axis=0, keepdims=True)

    return pl.pallas_call(
        body, name="loss_grad", grid=(T // tm,),
        in_specs=[_row_spec(tm, D), _row_spec(tm, D)],
        out_specs=[_row_spec(tm, D), _full_spec((1, _LANES))],
        out_shape=[jax.ShapeDtypeStruct((T, D), F32), jax.ShapeDtypeStruct((1, _LANES), F32)],
        compiler_params=_params(("arbitrary",), 3 * _nbytes((tm, D), F32), 3 * _nbytes((tm, D), F32)),
    )(y, target)


def _glu(pc, C):
    a, gate = pc[:, :C], pc[:, C:]
    s = _sig(gate)
    return a, s, a * s


def _tap_groups(offsets):
    groups = {}
    for k, off in enumerate(offsets):
        groups.setdefault(off % _SUBLANES, []).append((k, off - off % _SUBLANES))
    return sorted(groups.items())


def _conv_fwd(pc, w, cb, lg, lb):
    T, C2 = pc.shape
    C = C2 // 2
    tm = _tile(T, _ROW_TILE // 2, HALO)
    per = tm // HALO

    def body(pc_ref, ph_ref, w_ref, cb_ref, lg_ref, lb_ref, z_ref, c_ref, ubuf, vbuf):
        i = pl.program_id(0)
        _, _, u_cur = _glu(pc_ref[...], C)
        _, _, u_prev = _glu(ph_ref[...], C)
        ubuf[0:HALO, :] = jnp.where(i > 0, u_prev, 0.0)
        ubuf[HALO:HALO + tm, :] = u_cur
        ubuf[HALO + tm:, :] = jnp.zeros((_SUBLANES, C), F32)
        acc = jnp.zeros((tm, C), F32)
        for shift, taps in _tap_groups([HALO - (CONV_WIDTH - 1) + k for k in range(CONV_WIDTH)]):
            v = None
            for k, base in taps:
                term = w_ref[k:k + 1, :] * ubuf[base:base + tm + _SUBLANES, :]
                v = term if v is None else v + term
            if shift == 0:
                acc = acc + v[:tm]
            else:
                vbuf[...] = v
                acc = acc + vbuf[shift:shift + tm, :]
        z = acc + cb_ref[...]
        z_ref[...] = z
        zc = z - jnp.mean(z, axis=-1, keepdims=True)
        y = zc * lax.rsqrt(jnp.mean(zc * zc, axis=-1, keepdims=True) + EPS) * lg_ref[...] + lb_ref[...]
        c_ref[...] = (y * _sig(y)).astype(c_ref.dtype)

    return pl.pallas_call(
        body, name="conv_fwd", grid=(T // tm,),
        in_specs=[_row_spec(tm, C2), pl.BlockSpec((HALO, C2), lambda i: (jnp.maximum(i * per - 1, 0), 0)),
                  _full_spec((HALO, C)), _full_spec((1, C)), _full_spec((1, C)), _full_spec((1, C))],
        out_specs=[_row_spec(tm, C), _row_spec(tm, C)],
        out_shape=[jax.ShapeDtypeStruct((T, C), F32), jax.ShapeDtypeStruct((T, C), _MXU_DTYPE)],
        scratch_shapes=[pltpu.VMEM((tm + HALO + _SUBLANES, C), F32), pltpu.VMEM((tm + _SUBLANES, C), F32)],
        compiler_params=_params(("parallel",), 3 * _nbytes((tm, C2), F32), 10 * _nbytes((tm, C), F32)),
    )(pc, pc, w, cb, lg, lb)


def _conv_bwd(pc, z, dc, w, lg, lb):
    T, C2 = pc.shape
    C = C2 // 2
    tm = _tile(T, _ROW_TILE // 2, HALO)
    per = tm // HALO
    n = T // tm
    last_halo = T // HALO - 1

    def body(pc_ref, ph_ref, z_ref, zn_ref, dc_ref, dcn_ref, w_ref, lg_ref, lb_ref,
             dpc_ref, dw_ref, dcb_ref, dlg_ref, dlb_ref, ubuf, dzbuf, vbuf, sbuf):
        i = pl.program_id(0)
        _init_acc([dw_ref, dcb_ref, dlg_ref, dlb_ref])
        g, b = lg_ref[...], lb_ref[...]

        def ln_swish_bwd(zv, dcv):
            zc = zv - jnp.mean(zv, axis=-1, keepdims=True)
            r = lax.rsqrt(jnp.mean(zc * zc, axis=-1, keepdims=True) + EPS)
            zh = zc * r
            y = zh * g + b
            s = _sig(y)
            dy = dcv * (s * (1.0 + y * (1.0 - s)))
            dyg = dy * g
            dz = r * (dyg - jnp.mean(dyg, axis=-1, keepdims=True) - zh * jnp.mean(dyg * zh, axis=-1, keepdims=True))
            return dz, dy, zh

        dz_c, dy_c, zh_c = ln_swish_bwd(z_ref[...], dc_ref[...])
        dz_n, _, _ = ln_swish_bwd(zn_ref[...], dcn_ref[...])
        dlg_ref[...] += _colsum(dy_c * zh_c)
        dlb_ref[...] += _colsum(dy_c)
        dcb_ref[...] += _colsum(dz_c)
        dzbuf[0:tm, :] = dz_c
        dzbuf[tm:tm + HALO, :] = jnp.where(i < n - 1, dz_n, 0.0)
        a, s, u_cur = _glu(pc_ref[...], C)
        _, _, u_prev = _glu(ph_ref[...], C)
        ubuf[0:HALO, :] = jnp.where(i > 0, u_prev, 0.0)
        ubuf[HALO:HALO + tm, :] = u_cur
        ubuf[HALO + tm:, :] = jnp.zeros((_SUBLANES, C), F32)
        du = jnp.zeros((tm, C), F32)
        for shift, taps in _tap_groups([CONV_WIDTH - 1 - k for k in range(CONV_WIDTH)]):
            v = None
            for k, base in taps:
                term = w_ref[k:k + 1, :] * dzbuf[base:base + tm + _SUBLANES, :]
                v = term if v is None else v + term
            if shift == 0:
                du = du + v[:tm]
            else:
                vbuf[...] = v
                du = du + vbuf[shift:shift + tm, :]
        for shift, taps in _tap_groups([HALO - (CONV_WIDTH - 1) + k for k in range(CONV_WIDTH)]):
            if shift:
                sbuf[...] = ubuf[shift:shift + tm + HALO, :]
            src = sbuf if shift else ubuf
            for k, base in taps:
                dw_ref[k:k + 1, :] += _colsum(dz_c * src[base:base + tm, :])
        dpc_ref[:, :C] = (du * s).astype(dpc_ref.dtype)
        dpc_ref[:, C:] = (du * a * s * (1.0 - s)).astype(dpc_ref.dtype)

    nxt = lambda i: (jnp.minimum((i + 1) * per, last_halo), 0)
    vec = _full_spec((1, C))
    return pl.pallas_call(
        body, name="conv_bwd", grid=(n,),
        in_specs=[_row_spec(tm, C2), pl.BlockSpec((HALO, C2), lambda i: (jnp.maximum(i * per - 1, 0), 0)),
                  _row_spec(tm, C), pl.BlockSpec((HALO, C), nxt), _row_spec(tm, C), pl.BlockSpec((HALO, C), nxt),
                  _full_spec((HALO, C)), vec, vec],
        out_specs=[_row_spec(tm, C2), _full_spec((HALO, C)), vec, vec, vec],
        out_shape=[jax.ShapeDtypeStruct((T, C2), _MXU_DTYPE), jax.ShapeDtypeStruct((HALO, C), F32)]
                  + [jax.ShapeDtypeStruct((1, C), F32)] * 3,
        scratch_shapes=[pltpu.VMEM((tm + HALO + _SUBLANES, C), F32), pltpu.VMEM((tm + HALO + _SUBLANES, C), F32),
                        pltpu.VMEM((tm + _SUBLANES, C), F32), pltpu.VMEM((tm + HALO, C), F32)],
        compiler_params=_params(("arbitrary",), 5 * _nbytes((tm, C2), F32), 16 * _nbytes((tm, C), F32)),
    )(pc, pc, z, z, dc, dc, w, lg, lb)


def _lat_norm_fwd(pl_, gq, gkv, QL, KVL):
    T, W = pl_.shape
    tm = _tile(T, _ROW_TILE, _PACKED_ROWS)

    def body(p_ref, gq_ref, gkv_ref, cq_ref, ckv_ref):
        def norm(xv, gv):
            return xv * lax.rsqrt(jnp.mean(xv * xv, axis=-1, keepdims=True) + EPS) * gv
        cq_ref[...] = norm(p_ref[:, :QL], gq_ref[...]).astype(cq_ref.dtype)
        ckv_ref[...] = norm(p_ref[:, QL:QL + KVL], gkv_ref[...]).astype(ckv_ref.dtype)

    return pl.pallas_call(
        body, name="lat_norm_fwd", grid=(T // tm,),
        in_specs=[_row_spec(tm, W), _full_spec((1, QL)), _full_spec((1, KVL))],
        out_specs=[_row_spec(tm, QL), _row_spec(tm, KVL)],
        out_shape=[jax.ShapeDtypeStruct((T, QL), _MXU_DTYPE), jax.ShapeDtypeStruct((T, KVL), _MXU_DTYPE)],
        compiler_params=_params(("parallel",), 2 * _nbytes((tm, W), F32), 4 * _nbytes((tm, W), F32)),
    )(pl_, gq, gkv)


def _lat_norm_bwd(pl_, gq, gkv, dcq, dckv, drope, QL, KVL):
    T, W = pl_.shape
    tm = _tile(T, _ROW_TILE, _PACKED_ROWS)

    def body(p_ref, gq_ref, gkv_ref, dcq_ref, dckv_ref, dr_ref, dp_ref, dgq_ref, dgkv_ref):
        _init_acc([dgq_ref, dgkv_ref])

        def bwd(xv, gv, dv, dg_ref):
            r = lax.rsqrt(jnp.mean(xv * xv, axis=-1, keepdims=True) + EPS)
            xh = xv * r
            dxh = dv * gv
            dg_ref[...] += _colsum(dv * xh)
            return r * (dxh - xh * jnp.mean(dxh * xh, axis=-1, keepdims=True))

        dp_ref[:, :QL] = bwd(p_ref[:, :QL], gq_ref[...], dcq_ref[...], dgq_ref).astype(dp_ref.dtype)
        dp_ref[:, QL:QL + KVL] = bwd(p_ref[:, QL:QL + KVL], gkv_ref[...], dckv_ref[...], dgkv_ref).astype(dp_ref.dtype)
        dp_ref[:, QL + KVL:] = dr_ref[...].astype(dp_ref.dtype)

    return pl.pallas_call(
        body, name="lat_norm_bwd", grid=(T // tm,),
        in_specs=[_row_spec(tm, W), _full_spec((1, QL)), _full_spec((1, KVL)), _row_spec(tm, QL), _row_spec(tm, KVL),
                  _row_spec(tm, HEAD_W)],
        out_specs=[_row_spec(tm, W), _full_spec((1, QL)), _full_spec((1, KVL))],
        out_shape=[jax.ShapeDtypeStruct((T, W), _MXU_DTYPE), jax.ShapeDtypeStruct((1, QL), F32),
                   jax.ShapeDtypeStruct((1, KVL), F32)],
        compiler_params=_params(("arbitrary",), 4 * _nbytes((tm, W), F32), 6 * _nbytes((tm, W), F32)),
    )(pl_, gq, gkv, dcq, dckv, drope)


def _norm_rope(xv, gv, ct, sa, sb):
    r = lax.rsqrt(jnp.sum(xv * xv, axis=-1, keepdims=True) * (1.0 / QK_DIM) + EPS)
    xn = xv * r * gv
    half = ROPE_DIM // 2
    return xn * ct + pltpu.roll(xn, HEAD_W - half, 1) * sa + pltpu.roll(xn, half, 1) * sb


def _norm_rope_bwd(xv, gv, ct, sa, sb, dout):
    half = ROPE_DIM // 2
    dxn = dout * ct + pltpu.roll(dout * sa, half, 1) + pltpu.roll(dout * sb, HEAD_W - half, 1)
    r = lax.rsqrt(jnp.sum(xv * xv, axis=-1, keepdims=True) * (1.0 / QK_DIM) + EPS)
    xh = xv * r
    dxh = dxn * gv
    dx = r * (dxh - xh * (jnp.sum(dxh * xh, axis=-1, keepdims=True) * (1.0 / QK_DIM)))
    return dx, _colsum(dxn * xh)


def _qk_prep_fwd(q_raw, k_raw, pl_, rope_cb, ct, sa, sb, gq, gk):
    T, HW = q_raw.shape
    H = HW // HEAD_W
    tm = _tile(T, _ROW_TILE, _PACKED_ROWS)

    def body(q_ref, k_ref, r_ref, ct_ref, sa_ref, sb_ref, gq_ref, gk_ref, qn_ref, kn_ref):
        ctv, sav, sbv, rv = ct_ref[...], sa_ref[...], sb_ref[...], r_ref[...]
        for h in range(H):
            hs = slice(h * HEAD_W, (h + 1) * HEAD_W)
            qn_ref[:, hs] = _norm_rope(q_ref[:, hs], gq_ref[...], ctv, sav, sbv).astype(qn_ref.dtype)
            kn_ref[:, hs] = _norm_rope(k_ref[:, hs] + rv, gk_ref[...], ctv, sav, sbv).astype(kn_ref.dtype)

    tb = _row_spec(tm, HEAD_W)
    gb = _full_spec((1, HEAD_W))
    return pl.pallas_call(
        body, name="qk_prep_fwd", grid=(T // tm,),
        in_specs=[_row_spec(tm, HW), _row_spec(tm, HW), _row_spec(tm, HEAD_W, rope_cb), tb, tb, tb, gb, gb],
        out_specs=[_row_spec(tm, HW), _row_spec(tm, HW)],
        out_shape=[jax.ShapeDtypeStruct((T, HW), _MXU_DTYPE)] * 2,
        compiler_params=_params(("parallel",), 3 * _nbytes((tm, HW), F32) + 4 * _nbytes((tm, HEAD_W), F32), 8 * _nbytes((tm, HEAD_W), F32)),
    )(q_raw, k_raw, pl_, ct, sa, sb, gq, gk)


def _qk_prep_bwd(q_raw, k_raw, pl_, rope_cb, ct, sa, sb, gq, gk, dqn, dkn):
    T, HW = q_raw.shape
    H = HW // HEAD_W
    tm = _tile(T, _ROW_TILE, _PACKED_ROWS)

    def body(q_ref, k_ref, r_ref, ct_ref, sa_ref, sb_ref, gq_ref, gk_ref, dqn_ref, dkn_ref,
             dq_ref, dk_ref, dr_ref, dgq_ref, dgk_ref):
        _init_acc([dgq_ref, dgk_ref])
        ctv, sav, sbv, rv = ct_ref[...], sa_ref[...], sb_ref[...], r_ref[...]
        dr = dgq_sum = dgk_sum = None
        for h in range(H):
            hs = slice(h * HEAD_W, (h + 1) * HEAD_W)
            dq, dgq = _norm_rope_bwd(q_ref[:, hs], gq_ref[...], ctv, sav, sbv, dqn_ref[:, hs])
            dk, dgk = _norm_rope_bwd(k_ref[:, hs] + rv, gk_ref[...], ctv, sav, sbv, dkn_ref[:, hs])
            dq_ref[:, hs] = dq.astype(dq_ref.dtype)
            dk_ref[:, hs] = dk.astype(dk_ref.dtype)
            dr = dk if h == 0 else dr + dk
            dgq_sum = dgq if h == 0 else dgq_sum + dgq
            dgk_sum = dgk if h == 0 else dgk_sum + dgk
        dr_ref[...] = dr
        dgq_ref[...] += dgq_sum
        dgk_ref[...] += dgk_sum

    tb = _row_spec(tm, HEAD_W)
    gb = _full_spec((1, HEAD_W))
    hb = _row_spec(tm, HW)
    return pl.pallas_call(
        body, name="qk_prep_bwd", grid=(T // tm,),
        in_specs=[hb, hb, _row_spec(tm, HEAD_W, rope_cb), tb, tb, tb, gb, gb, hb, hb],
        out_specs=[hb, hb, tb, gb, gb],
        out_shape=[jax.ShapeDtypeStruct((T, HW), _MXU_DTYPE)] * 2 + [jax.ShapeDtypeStruct((T, HEAD_W), F32)]
                  + [jax.ShapeDtypeStruct((1, HEAD_W), F32)] * 2,
        compiler_params=_params(("arbitrary",), 5 * _nbytes((tm, HW), F32) + 5 * _nbytes((tm, HEAD_W), F32), 12 * _nbytes((tm, HEAD_W), F32)),
    )(q_raw, k_raw, pl_, ct, sa, sb, gq, gk, dqn, dkn)


_NEG = -1e30
_LOG2E = 1.4426950408889634
_SCORE_C = QK_DIM ** -0.5 * _LOG2E


def _pieces(t, rb, diag):
    assert CHUNK % rb == 0 and rb % _SUBLANES == 0
    out = []
    for r in range(t // rb):
        lo = (r * rb // CHUNK) * CHUNK if diag else 0
        for c in range(t // _LANES):
            cut = None if (c + 1) * _LANES <= lo else max(lo - c * _LANES, 0)
            out.append((slice(r * rb, (r + 1) * rb), slice(c * _LANES, (c + 1) * _LANES), c, cut))
    return out


def _groups(x):
    return x.reshape(x.shape[0] // _SUBLANES, _SUBLANES, x.shape[1])


def _all_sublanes(x8, op):
    return jnp.broadcast_to(op(x8, axis=0, keepdims=True), x8.shape)


def _lane_ge(rb, cut):
    return lax.broadcasted_iota(jnp.int32, (rb, _LANES), 1) >= cut


def _attn_fwd(q, k, v):
    T, HW = q.shape
    H = HW // HEAD_W
    t = _tile(T, _ATT_TILE)
    rb = min(_ATT_ROWS, t)
    nc = t // _LANES

    hp = _ATT_HEADS if H % _ATT_HEADS == 0 else 1
    W = hp * HEAD_W

    def body(q_ref, k_ref, v_ref, o_ref, lse_ref, s_s, p_s, m_s, l_s, acc_s):
        i = pl.program_id(1)
        m_s[...] = jnp.full_like(m_s, _NEG)
        l_s[...] = jnp.zeros_like(l_s)
        acc_s[...] = jnp.zeros_like(acc_s)

        def head_step(hh, rows, diag):
            hs = slice(hh * HEAD_W, (hh + 1) * HEAD_W)
            s_s[hh] = _dot(k_ref[rows, hs], q_ref[:, hs], 'nt')
            pieces = _pieces(t, rb, diag)

            def scores(rs, cs, cut):
                sb = s_s[hh, rs, cs]
                return jnp.where(_lane_ge(rb, cut), sb, _NEG) if cut else sb

            mx = [None] * nc
            for rs, cs, c, cut in pieces:
                if cut is not None:
                    g = jnp.max(_groups(scores(rs, cs, cut)), axis=0)
                    mx[c] = g if mx[c] is None else jnp.maximum(mx[c], g)
            m_new, alpha = [], []
            for c in range(nc):
                cs = slice(c * _LANES, (c + 1) * _LANES)
                m_prev = m_s[hh, :, cs]
                m_new.append(jnp.maximum(m_prev, _all_sublanes(mx[c], jnp.max)))
                alpha.append(jnp.exp2((m_prev - m_new[c]) * _SCORE_C))
                m_s[hh, :, cs] = m_new[c]
            lsum = [jnp.zeros((_SUBLANES, _LANES), F32)] * nc
            for rs, cs, c, cut in pieces:
                if cut is None:
                    p_s[hh, rs, cs] = jnp.zeros((rb, _LANES), p_s.dtype)
                    continue
                p = jnp.exp2((scores(rs, cs, cut) - jnp.tile(m_new[c], (rb // _SUBLANES, 1))) * _SCORE_C)
                lsum[c] = lsum[c] + jnp.sum(_groups(p), axis=0)
                p_s[hh, rs, cs] = p.astype(p_s.dtype)
            for c in range(nc):
                cs = slice(c * _LANES, (c + 1) * _LANES)
                l_s[hh, :, cs] = alpha[c] * l_s[hh, :, cs] + _all_sublanes(lsum[c], jnp.sum)
            a = jnp.tile(jnp.concatenate(alpha, axis=1), (HEAD_W // _SUBLANES, 1))
            acc_s[hh] = a * acc_s[hh] + _dot(v_ref[rows, hs], p_s[hh], 'tn')

        def step(j, diag):
            rows = pl.ds(pl.multiple_of(j * t, t), t)
            for hh in range(hp):
                head_step(hh, rows, diag)

        def loop_body(j, carry):
            step(j, False)
            return carry

        lax.fori_loop(0, i, loop_body, 0)
        step(i, True)
        for hh in range(hp):
            l = l_s[hh]
            o_t = acc_s[hh] / jnp.tile(l, (HEAD_W // _SUBLANES, 1))
            o_ref[:, hh * HEAD_W:(hh + 1) * HEAD_W] = o_t.T.astype(o_ref.dtype)
            lse_ref[hh * _SUBLANES:(hh + 1) * _SUBLANES, :] = m_s[hh] * _SCORE_C + jnp.log(l) * _LOG2E

    qb = pl.BlockSpec((t, W), lambda h, i: (i, h))
    kb = pl.BlockSpec((T, W), lambda h, i: (0, h))
    st = pltpu.VMEM((hp, _SUBLANES, t), F32)
    return pl.pallas_call(
        body, name="attn_fwd", grid=(H // hp, T // t),
        in_specs=[qb, kb, kb], out_specs=[qb, pl.BlockSpec((hp * _SUBLANES, t), lambda h, i: (h, i))],
        out_shape=[jax.ShapeDtypeStruct((T, HW), _MXU_DTYPE), jax.ShapeDtypeStruct((H * _SUBLANES, T), F32)],
        scratch_shapes=[pltpu.VMEM((hp, t, t), F32), pltpu.VMEM((hp, t, t), _MXU_DTYPE), st, st, pltpu.VMEM((hp, HEAD_W, t), F32)],
        compiler_params=_params(("parallel", "arbitrary"), 2 * _nbytes((T, W), _MXU_DTYPE) + 3 * _nbytes((t, W), F32),
                                4 * hp * _nbytes((t, t), F32)),
    )(q, k, v)


def _attn_bwd(q, k, v, o, do, lse2):
    T, HW = q.shape
    H = HW // HEAD_W
    t = _tile(T, _ATT_TILE)
    n = T // t
    rb = min(_ATT_ROWS, t)
    scale = QK_DIM ** -0.5

    hp = _ATT_HEADS if H % _ATT_HEADS == 0 else 1
    W = hp * HEAD_W

    def body(k_ref, v_ref, q_ref, o_ref, do_ref, lse_ref, dq_ref, dk_ref, dv_ref,
             s_s, dp_s, p_s, ds_s, dk_s, dv_s, delta_s, dqt_s):
        j = pl.program_id(1)

        @pl.when(j == 0)
        def _():
            dqt_s[...] = jnp.zeros_like(dqt_s)
            for hh in range(hp):
                hs = slice(hh * HEAD_W, (hh + 1) * HEAD_W)
                for b in range(n):
                    rs = slice(b * t, (b + 1) * t)
                    d = jnp.sum(do_ref[rs, hs].astype(F32) * o_ref[rs, hs].astype(F32), axis=1, keepdims=True)
                    delta_s[hh, :, rs] = jnp.broadcast_to(d, (t, HEAD_W)).T[0:_SUBLANES, :]

        dk_s[...] = jnp.zeros_like(dk_s)
        dv_s[...] = jnp.zeros_like(dv_s)

        def head_step(hh, base, diag):
            hs = slice(hh * HEAD_W, (hh + 1) * HEAD_W)
            kv, vv = k_ref[:, hs], v_ref[:, hs]
            qv, dov = q_ref[pl.ds(base, t), hs], do_ref[pl.ds(base, t), hs]
            s_s[hh] = _dot(kv, qv, 'nt')
            dp_s[hh] = _dot(vv, dov, 'nt')
            for rs, cs, c, cut in _pieces(t, rb, diag):
                if cut is None:
                    p_s[hh, rs, cs] = jnp.zeros((rb, _LANES), p_s.dtype)
                    ds_s[hh, rs, cs] = jnp.zeros((rb, _LANES), ds_s.dtype)
                    continue
                lanes = pl.ds(pl.multiple_of(base + c * _LANES, _LANES), _LANES)
                lse = jnp.tile(lse_ref[hh * _SUBLANES:(hh + 1) * _SUBLANES, lanes], (rb // _SUBLANES, 1))
                dl = jnp.tile(delta_s[hh, :, lanes], (rb // _SUBLANES, 1))
                p = jnp.exp2(s_s[hh, rs, cs] * _SCORE_C - lse)
                if cut:
                    p = jnp.where(_lane_ge(rb, cut), p, 0.0)
                p_s[hh, rs, cs] = p.astype(p_s.dtype)
                ds_s[hh, rs, cs] = (p * (dp_s[hh, rs, cs] - dl)).astype(ds_s.dtype)
            dsv = ds_s[hh]
            dv_s[hh] += _dot(p_s[hh], dov, 'nn')
            dk_s[hh] += _dot(dsv, qv, 'nn')
            dqt_s[hh, :, pl.ds(base, t)] += _dot(kv, dsv, 'tn')

        def step(i, diag):
            base = pl.multiple_of(i * t, t)
            for hh in range(hp):
                head_step(hh, base, diag)

        def loop_body(i, carry):
            step(i, False)
            return carry

        step(j, True)
        lax.fori_loop(j + 1, n, loop_body, 0)
        for hh in range(hp):
            hs = slice(hh * HEAD_W, (hh + 1) * HEAD_W)
            dk_ref[:, hs] = dk_s[hh] * scale
            dv_ref[:, hs] = dv_s[hh].astype(dv_ref.dtype)

        @pl.when(j == n - 1)
        def _():
            for hh in range(hp):
                for b in range(n):
                    rs = slice(b * t, (b + 1) * t)
                    dq_ref[rs, hh * HEAD_W:(hh + 1) * HEAD_W] = dqt_s[hh, :, rs].T * scale

    jb = pl.BlockSpec((t, W), lambda h, j: (j, h))
    fb = pl.BlockSpec((T, W), lambda h, j: (0, h))
    sc = pltpu.VMEM((hp, t, HEAD_W), F32)
    return pl.pallas_call(
        body, name="attn_bwd", grid=(H // hp, n),
        in_specs=[jb, jb, fb, fb, fb, pl.BlockSpec((hp * _SUBLANES, T), lambda h, j: (h, 0))], out_specs=[fb, jb, jb],
        out_shape=[jax.ShapeDtypeStruct((T, HW), F32), jax.ShapeDtypeStruct((T, HW), F32),
                   jax.ShapeDtypeStruct((T, HW), _MXU_DTYPE)],
        scratch_shapes=[pltpu.VMEM((hp, t, t), F32), pltpu.VMEM((hp, t, t), F32), pltpu.VMEM((hp, t, t), _MXU_DTYPE),
                        pltpu.VMEM((hp, t, t), _MXU_DTYPE), sc, sc, pltpu.VMEM((hp, _SUBLANES, T), F32),
                        pltpu.VMEM((hp, HEAD_W, T), F32)],
        compiler_params=_params(("parallel", "arbitrary"), 3 * _nbytes((T, W), _MXU_DTYPE) + _nbytes((T, W), F32)
                                + 4 * _nbytes((t, W), F32), hp * (4 * _nbytes((t, t), F32) + _nbytes((T, HEAD_W), F32))),
    )(k, v, q, o, do, lse2)


def _gate_fwd(pg, gb, yc, ym):
    T, D = yc.shape
    tm = _tile(T, _ROW_TILE, _PACKED_ROWS)

    def body(g0_ref, g1_ref, gb_ref, yc_ref, ym_ref, y_ref):
        s0 = _sig(g0_ref[...] + gb_ref[0:1, :])
        s1 = _sig(g1_ref[...] + gb_ref[1:2, :])
        y_ref[...] = (s0 * yc_ref[...] + s1 * ym_ref[...]).astype(y_ref.dtype)

    return pl.pallas_call(
        body, name="gate_fwd", grid=(T // tm,),
        in_specs=[_row_spec(tm, D, 0), _row_spec(tm, D, 1), _full_spec((2, D)), _row_spec(tm, D), _row_spec(tm, D)],
        out_specs=_row_spec(tm, D),
        out_shape=jax.ShapeDtypeStruct((T, D), _MXU_DTYPE),
        compiler_params=_params(("parallel",), 5 * _nbytes((tm, D), F32), 4 * _nbytes((tm, D), F32)),
    )(pg, pg, gb, yc, ym)


def _gate_bwd(pg, gb, yc, ym, dy):
    T, D = yc.shape
    tm = _tile(T, _ROW_TILE, _PACKED_ROWS)

    def body(g0_ref, g1_ref, gb_ref, yc_ref, ym_ref, dy_ref, dyc_ref, dym_ref, dpg_ref, dgb_ref):
        _init_acc([dgb_ref])
        dyv = dy_ref[...]
        s0 = _sig(g0_ref[...] + gb_ref[0:1, :])
        s1 = _sig(g1_ref[...] + gb_ref[1:2, :])
        dyc_ref[...] = (dyv * s0).astype(dyc_ref.dtype)
        dym_ref[...] = (dyv * s1).astype(dym_ref.dtype)
        d0 = dyv * yc_ref[...] * s0 * (1.0 - s0)
        d1 = dyv * ym_ref[...] * s1 * (1.0 - s1)
        dpg_ref[:, :D] = d0.astype(dpg_ref.dtype)
        dpg_ref[:, D:] = d1.astype(dpg_ref.dtype)
        dgb_ref[0:1, :] += _colsum(d0)
        dgb_ref[1:2, :] += _colsum(d1)

    return pl.pallas_call(
        body, name="gate_bwd", grid=(T // tm,),
        in_specs=[_row_spec(tm, D, 0), _row_spec(tm, D, 1), _full_spec((2, D)), _row_spec(tm, D), _row_spec(tm, D),
                  _row_spec(tm, D)],
        out_specs=[_row_spec(tm, D), _row_spec(tm, D), _row_spec(tm, 2 * D), _full_spec((2, D))],
        out_shape=[jax.ShapeDtypeStruct((T, D), _MXU_DTYPE)] * 2 + [jax.ShapeDtypeStruct((T, 2 * D), _MXU_DTYPE),
                                                                   jax.ShapeDtypeStruct((2, D), F32)],
        compiler_params=_params(("arbitrary",), 7 * _nbytes((tm, D), F32), 8 * _nbytes((tm, D), F32)),
    )(pg, pg, gb, yc, ym, dy)


def _place():
    return lax.axis_index("x"), lax.axis_index("y"), lax.axis_index("c")


def _all_gather(name, xs):
    R, C = xs.shape
    MESH = pl.DeviceIdType.MESH

    def body(x_ref, out_ref, send_sems, recv_sems, local_sem):
        x, y, c = _place()
        me, sibling = (x, y, c), (x, y, 1 - c)
        chips = [(1 - x, y), (x, 1 - y), (1 - x, 1 - y)]

        def rows(px, py, pc):
            return out_ref.at[4 * px + 2 * py + pc]

        def copy(k, block, to, src=None):
            return pltpu.make_async_remote_copy(
                src_ref=rows(*block) if src is None else src, dst_ref=rows(*block),
                send_sem=send_sems.at[k], recv_sem=recv_sems.at[k], device_id=to, device_id_type=MESH)

        mine = pltpu.make_async_copy(x_ref, rows(*me), local_sem)
        mine.start()
        first = [copy(0, me, sibling, src=x_ref)]
        first += [copy(1 + j, me, (*chip, c), src=x_ref) for j, chip in enumerate(chips)]
        for cp in first:
            cp.start()
        passed = [copy(4 + j, (*chip, c), sibling) for j, chip in enumerate(chips)]
        for j, chip in enumerate(chips):
            copy(1 + j, (*chip, c), me).wait_recv()
            passed[j].start()
        copy(0, sibling, me).wait_recv()
        for j, chip in enumerate(chips):
            copy(4 + j, (*chip, 1 - c), me).wait_recv()
        for cp in first + passed:
            cp.wait_send()
        mine.wait()

    return pl.pallas_call(
        body, name=name,
        out_shape=jax.ShapeDtypeStruct((N_DEV, R, C), xs.dtype),
        in_specs=[pl.BlockSpec(memory_space=pl.ANY)], out_specs=pl.BlockSpec(memory_space=pl.ANY),
        scratch_shapes=[pltpu.SemaphoreType.DMA((7,)), pltpu.SemaphoreType.DMA((7,)), pltpu.SemaphoreType.DMA],
    )(xs)


_HBM_SPEC = pl.BlockSpec(memory_space=pltpu.HBM)
_SEM_SPEC = pl.BlockSpec(memory_space=pltpu.SEMAPHORE)
_DATAFLOW = pltpu.SideEffectType.DATAFLOW_SIDE_EFFECTING


def _peers():
    x, y, c = _place()
    out = []
    for k in range(1, N_DEV):
        px = 1 - x if k & 4 else x
        py = 1 - y if k & 2 else y
        pc = 1 - c if k & 1 else c
        out.append((k - 1, (px, py, pc), 4 * px + 2 * py + pc))
    return 4 * x + 2 * y + c, out


def _exchange_copies(x_ref, land_ref, send_sems, recv_sems, scatter):
    me, peers = _peers()
    return [pltpu.make_async_remote_copy(
        src_ref=x_ref.at[idx] if scatter else x_ref, dst_ref=land_ref.at[me],
        send_sem=send_sems.at[k], recv_sem=recv_sems.at[k], device_id=dev, device_id_type=pl.DeviceIdType.MESH)
        for k, dev, idx in peers]


def _exchange_start(name, xs, land, scatter):
    def body(x_ref, land_ref, send_sems, recv_sems, x_thru, land_thru, token):
        for cp in _exchange_copies(x_ref, land_ref, send_sems, recv_sems, scatter):
            cp.start()
        token[...] = jnp.zeros_like(token)

    sems = pltpu.SemaphoreType.DMA((N_DEV - 1,))
    return pl.pallas_call(
        body, name=name,
        out_shape=(sems, sems, pltpu.HBM(xs.shape, xs.dtype), pltpu.HBM(land.shape, land.dtype),
                   jax.ShapeDtypeStruct((_SUBLANES, _LANES), F32)),
        in_specs=(_HBM_SPEC, _HBM_SPEC),
        out_specs=(_SEM_SPEC, _SEM_SPEC, _HBM_SPEC, _HBM_SPEC, pl.BlockSpec(memory_space=pltpu.VMEM)),
        input_output_aliases={0: 2, 1: 3},
        compiler_params=pltpu.CompilerParams(has_side_effects=_DATAFLOW),
    )(pltpu.with_memory_space_constraint(xs, pltpu.HBM), pltpu.with_memory_space_constraint(land, pltpu.HBM))


def _exchange_wait(name, send_sems, recv_sems, xs, land, after, scatter):
    def body(x_ref, land_ref, send_sems, recv_sems, after_ref, x_dead, got_ref):
        for cp in _exchange_copies(x_ref, land_ref, send_sems, recv_sems, scatter):
            cp.wait_send()
            cp.wait_recv()

    return pl.pallas_call(
        body, name=name,
        out_shape=(pltpu.HBM(xs.shape, xs.dtype), pltpu.HBM(land.shape, land.dtype)),
        in_specs=(_HBM_SPEC, _HBM_SPEC, _SEM_SPEC, _SEM_SPEC, pl.BlockSpec(memory_space=pl.ANY)),
        out_specs=(_HBM_SPEC, _HBM_SPEC), input_output_aliases={0: 0, 1: 1},
        compiler_params=pltpu.CompilerParams(has_side_effects=_DATAFLOW),
    )(xs, land, send_sems, recv_sems, after)[1]


def _own_slot(block, me):
    land = lax.empty((N_DEV,) + block.shape, block.dtype)
    return lax.dynamic_update_slice(land, block[None], (me,) + (0,) * block.ndim)


def _sum_blocks(name, parts, scale=None):
    n, R, C = parts.shape
    tr = _tile(R, _PACK_ROW_MULT, _PACKED_ROWS)

    def body(p_ref, o_ref):
        acc = p_ref[0].astype(F32)
        for s in range(1, n):
            acc = acc + p_ref[s].astype(F32)
        if scale is not None:
            acc = acc * scale
        o_ref[...] = acc

    return pl.pallas_call(
        body, name=name, grid=(R // tr,),
        in_specs=[pl.BlockSpec((n, tr, C), lambda i: (0, i, 0))], out_specs=pl.BlockSpec((tr, C), lambda i: (i, 0)),
        out_shape=jax.ShapeDtypeStruct((R, C), F32),
        compiler_params=_params(("parallel",), _nbytes((n, tr, C), parts.dtype) + _nbytes((tr, C), F32), 2 * _nbytes((tr, C), F32)),
    )(parts)


def _adamw_math(wv, gv, mv, vv):
    mv = ADAM_B1 * mv + (1.0 - ADAM_B1) * gv
    vv = ADAM_B2 * vv + (1.0 - ADAM_B2) * (gv * gv)
    m_hat = mv / (1.0 - ADAM_B1 ** ADAM_STEP)
    v_hat = vv / (1.0 - ADAM_B2 ** ADAM_STEP)
    return -ADAM_LR * (m_hat / (jnp.sqrt(v_hat) + ADAM_EPS) + ADAM_WD * wv), mv, vv


def _adamw_layer(l, w, g, m, v, outs):
    depth, R, C = w.shape
    tr = _tile(R, _ROW_TILE, _SUBLANES)

    def body(w_ref, g_ref, m_ref, v_ref, *rest):
        go_ref, d_ref, nm_ref, nv_ref = rest[-4:]
        gv = g_ref[...]
        go_ref[...] = gv
        d_ref[...], nm_ref[...], nv_ref[...] = _adamw_math(w_ref[...], gv, m_ref[...], v_ref[...])

    lay = pl.BlockSpec((None, tr, C), lambda i: (l, i, 0))
    shape = jax.ShapeDtypeStruct(w.shape, F32)
    return pl.pallas_call(
        body, name="adamw_layer", grid=(R // tr,),
        in_specs=[lay, pl.BlockSpec((tr, C), lambda i: (i, 0)), lay, lay] + [pl.BlockSpec(memory_space=pl.ANY)] * (4 if outs else 0),
        out_specs=[lay] * 4, out_shape=[shape] * 4,
        input_output_aliases={4 + k: k for k in range(4)} if outs else {},
        compiler_params=_params(("parallel",), 8 * _nbytes((tr, C), F32), 4 * _nbytes((tr, C), F32)),
    )(w, g, m, v, *(outs or ()))


def _adamw(w, g, m, v):
    shape = w.shape
    w2, g2, m2, v2 = (t.reshape(-1, shape[-1]) for t in (w, g, m, v))
    R, C = w2.shape
    tr = _tile(R, _ROW_TILE, _SUBLANES)

    def body(w_ref, g_ref, m_ref, v_ref, d_ref, nm_ref, nv_ref):
        d_ref[...], nm_ref[...], nv_ref[...] = _adamw_math(w_ref[...], g_ref[...], m_ref[...], v_ref[...])

    spec = pl.BlockSpec((tr, C), lambda i: (i, 0))
    outs = pl.pallas_call(
        body, name="adamw", grid=(R // tr,),
        in_specs=[spec] * 4, out_specs=[spec] * 3,
        out_shape=[jax.ShapeDtypeStruct((R, C), F32)] * 3,
        compiler_params=_params(("parallel",), 7 * _nbytes((tr, C), F32), 4 * _nbytes((tr, C), F32)),
    )(w2, g2, m2, v2)
    return tuple(o.reshape(shape) for o in outs)


def _pack_rows(parts, lead=()):
    out, spans, r0 = [], [], 0
    unit = _PACKED_ROWS * _PACK_W
    for p in parts:
        flat = p.reshape(lead + (-1,))
        size = flat.shape[-1]
        pad = (-size) % unit
        if pad:
            flat = jnp.pad(flat, [(0, 0)] * len(lead) + [(0, pad)])
        rows = (size + pad) // _PACK_W
        out.append(flat.reshape(lead + (rows, _PACK_W)))
        spans.append((r0, rows, p.shape[len(lead):]))
        r0 += rows
    pad = (-r0) % _PACK_ROW_MULT
    if pad:
        out.append(jnp.zeros(lead + (pad, _PACK_W), out[0].dtype))
    return jnp.concatenate(out, axis=len(lead)), spans


def _unpack_rows(packed, span, lead=()):
    r0, rows, shape = span
    size = 1
    for s in shape:
        size *= s
    sl = lax.slice_in_dim(packed, r0, r0 + rows, axis=len(lead))
    return sl.reshape(lead + (-1,))[..., :size].reshape(lead + tuple(shape))


def _pack_flat(parts):
    flat, spans, at = [], [], 0
    for p in parts:
        size = p.size
        padded = size + (-size) % _LANES
        flat.append(jnp.pad(p.reshape(-1), (0, padded - size)))
        spans.append((at, size, p.shape))
        at += padded
    tail = (-at) % (_PACKED_ROWS * _PACK_W)
    if tail:
        flat.append(jnp.zeros((tail,), flat[0].dtype))
    return jnp.concatenate(flat).reshape(-1, _PACK_W), spans


def _unpack_flat(packed, span, lead=()):
    at, size, shape = span
    return lax.slice_in_dim(packed.reshape(lead + (-1,)), at, at + size, axis=len(lead)).reshape(lead + tuple(shape))


def _rope_tables(positions, T):
    inv_freq = ROPE_THETA ** (-jnp.arange(0, ROPE_DIM, 2, dtype=F32) / ROPE_DIM)
    ang = positions.reshape(T, 1).astype(F32) * inv_freq
    cos, sin = jnp.cos(ang), jnp.sin(ang)
    half = ROPE_DIM // 2
    z = lambda n: jnp.zeros((T, n), F32)
    tail = HEAD_W - QK_DIM
    ct = jnp.concatenate([jnp.ones((T, NOPE_DIM), F32), cos, cos, z(tail)], axis=1)
    sa = jnp.concatenate([z(NOPE_DIM), -sin, z(half), z(tail)], axis=1)
    sb = jnp.concatenate([z(NOPE_DIM), z(half), sin, z(tail)], axis=1)
    return ct, sa, sb


def _pad_heads(wt, per_head, keep_from=0, keep=None):
    K = wt.shape[1]
    keep = per_head if keep is None else keep
    w3 = wt.reshape(N_HEADS, per_head, K)[:, keep_from:keep_from + keep]
    return jnp.pad(w3, ((0, 0), (0, HEAD_W - keep), (0, 0))).reshape(N_HEADS * HEAD_W, K)


def _unpad_heads(g, keep):
    return g.reshape(N_HEADS, HEAD_W, g.shape[1])[:, :keep]


def _layer_weights(W):
    D = W['w_out'].shape[1]
    C = W['w_conv_out'].shape[1]
    QL, KVL = W['w_uq'].shape[1], W['w_ukv'].shape[1]
    o1 = 2 * C
    o2, o3 = o1 + QL, o1 + QL + KVL
    o4 = o3 + ROPE_DIM
    win = W['w_in']
    zr = lambda n: jnp.zeros((n, D), win.dtype)
    L = dict(W)
    L['w_c'] = win[:o1]
    L['w_l'] = jnp.concatenate([win[o1:o3], zr(NOPE_DIM), win[o3:o4], zr(HEAD_W - QK_DIM)], axis=0)
    L['w_g'] = win[o4:]
    L['w_q'] = _pad_heads(W['w_uq'], QK_DIM)
    L['w_k'] = _pad_heads(W['w_ukv'], NOPE_DIM + V_DIM, 0, NOPE_DIM)
    L['w_v'] = _pad_heads(W['w_ukv'], NOPE_DIM + V_DIM, NOPE_DIM, V_DIM)
    wmo = W['w_mla_out'].reshape(D, N_HEADS, V_DIM)
    L['w_mo'] = jnp.pad(wmo, ((0, 0), (0, 0), (0, HEAD_W - V_DIM))).reshape(D, N_HEADS * HEAD_W)
    L['dims'] = (D, C, QL, KVL)
    return L


def _row(v, width=None):
    v = v.reshape(1, -1)
    if width is not None and v.shape[1] < width:
        v = jnp.pad(v, ((0, 0), (0, width - v.shape[1])))
    return v


def _ffn_fwd(x, g, wg, wu, wd):
    h = _rms_fwd(x, _row(g))
    a, b, u = _ffn_up(h, wg, wu)
    x_out = _mm("ffn_down", u, wd, 'nn', res=x, scale=0.5)
    return x_out, (x, h, a, b, u)


def _ffn_bwd_weights(dx, saved, wd, after):
    x, h, a, b, u = saved
    d_wd = _wgrad("ffn_dwd", u, dx, scale=0.5)
    da, db = _ffn_bwd_up(dx, wd, a, b, after)
    return da, db, (_wgrad("ffn_dwg", da, h), _wgrad("ffn_dwg", db, h), d_wd)


def _ffn_bwd_input(dx, saved, g, wg, wu, da, db, after):
    return _dh_rms_bwd("ffn_dh_rms", [(da, wg), (db, wu)], saved[0], _row(g), dx, after)


def _layer_fwd(x, L, S, tabs):
    D, C, QL, KVL = L['dims']
    rope_cb = (QL + KVL) // HEAD_W
    x1, ffn1 = _ffn_fwd(x, S['ffn1_norm'], L['ffn1_w_gate'], L['ffn1_w_up'], L['ffn1_w_down'])
    h = _rms_fwd(x1, _row(S['mix_norm']))
    pc = _mm("proj_c", h, L['w_c'], 'nt')
    pl_ = _mm("proj_l", h, L['w_l'], 'nt')
    pg = _mm("proj_g", h, L['w_g'], 'nt')
    conv_w = jnp.pad(S['conv_w'], ((0, HALO - CONV_WIDTH), (0, 0)))
    z, c = _conv_fwd(pc, conv_w, _row(S['conv_b']), _row(S['conv_ln_g']), _row(S['conv_ln_b']))
    yc = _mm("conv_out", c, L['w_conv_out'], 'nt')
    cqn, ckvn = _lat_norm_fwd(pl_, _row(S['cq_norm']), _row(S['ckv_norm']), QL, KVL)
    q_raw = _mm("up_q", cqn, L['w_q'], 'nt')
    k_raw = _mm("up_kv", ckvn, L['w_k'], 'nt')
    v = _mm("up_kv", ckvn, L['w_v'], 'nt', out_dtype=_MXU_DTYPE)
    gq, gk = _row(S['q_norm'], HEAD_W), _row(S['k_norm'], HEAD_W)
    qn, kn = _qk_prep_fwd(q_raw, k_raw, pl_, rope_cb, *tabs, gq, gk)
    o, lse = _attn_fwd(qn, kn, v)
    ym = _mm("mla_out", o, L['w_mo'], 'nt')
    y = _gate_fwd(pg, S['gate_bias'], yc, ym)
    x2 = _mm("mix_out", y, L['w_out'], 'nn', res=x1)
    x3, ffn2 = _ffn_fwd(x2, S['ffn2_norm'], L['ffn2_w_gate'], L['ffn2_w_up'], L['ffn2_w_down'])
    saved = dict(ffn1=ffn1, ffn2=ffn2, x1=x1, h=h, pc=pc, pl=pl_, pg=pg, z=z, c=c, yc=yc, cqn=cqn, ckvn=ckvn,
                 q_raw=q_raw, k_raw=k_raw, v=v, qn=qn, kn=kn, o=o, lse=lse, ym=ym, y=y, conv_w=conv_w, gq=gq, gk=gk)
    return x3, saved


def _layer_bwd_mix(dx, L, S, tabs, A, after):
    D, C, QL, KVL = L['dims']
    rope_cb = (QL + KVL) // HEAD_W
    G, g = {}, {}
    da, db, (G['ffn2_w_gate'], G['ffn2_w_up'], G['ffn2_w_down']) = _ffn_bwd_weights(dx, A['ffn2'], L['ffn2_w_down'], after)
    dx, g['ffn2_norm'] = _ffn_bwd_input(dx, A['ffn2'], S['ffn2_norm'], L['ffn2_w_gate'], L['ffn2_w_up'], da, db, after)
    G['w_out'] = _wgrad("mix_dwout", A['y'], dx)
    dy = _mm("mix_dy", dx, L['w_out'], 'nt')
    dyc, dym, dpg, g['gate_bias'] = _gate_bwd(A['pg'], S['gate_bias'], A['yc'], A['ym'], dy)
    G['w_conv_out'] = _wgrad("conv_dwout", dyc, A['c'])
    dc = _mm("conv_dc", dyc, L['w_conv_out'], 'nn')
    dpc, dcw, g['conv_b'], g['conv_ln_g'], g['conv_ln_b'] = _conv_bwd(
        A['pc'], A['z'], dc, A['conv_w'], _row(S['conv_ln_g']), _row(S['conv_ln_b']))
    g['conv_w'] = dcw[:CONV_WIDTH]
    d_wmo = _wgrad("mla_dwout", dym, A['o'])
    G['w_mla_out'] = d_wmo.reshape(D, N_HEADS, HEAD_W)[:, :, :V_DIM].reshape(D, N_HEADS * V_DIM)
    do = _mm("mla_do", dym, L['w_mo'], 'nn', out_dtype=_MXU_DTYPE)
    dqn, dkn, dv = _attn_bwd(A['qn'], A['kn'], A['v'], A['o'], do, A['lse'])
    dq_raw, dk_raw, drope, dgq, dgk = _qk_prep_bwd(A['q_raw'], A['k_raw'], A['pl'], rope_cb, *tabs, A['gq'], A['gk'], dqn, dkn)
    g['q_norm'], g['k_norm'] = dgq[:, :QK_DIM], dgk[:, :QK_DIM]
    G['w_uq'] = _unpad_heads(_wgrad("up_dwq", dq_raw, A['cqn']), QK_DIM).reshape(N_HEADS * QK_DIM, QL)
    d_wk = _unpad_heads(_wgrad("up_dwkv", dk_raw, A['ckvn']), NOPE_DIM)
    d_wv = _unpad_heads(_wgrad("up_dwkv", dv, A['ckvn']), V_DIM)
    G['w_ukv'] = jnp.concatenate([d_wk, d_wv], axis=1).reshape(N_HEADS * (NOPE_DIM + V_DIM), KVL)
    dcq = _mm("up_dcq", dq_raw, L['w_q'], 'nn')
    dckv = _mm("up_dckv_k", dk_raw, L['w_k'], 'nn')
    dckv = _mm("up_dckv_v", dv, L['w_v'], 'nn', res=dckv)
    dpl, g['cq_norm'], g['ckv_norm'] = _lat_norm_bwd(A['pl'], _row(S['cq_norm']), _row(S['ckv_norm']), dcq, dckv, drope, QL, KVL)
    d_wc = _wgrad("proj_dwc", dpc, A['h'])
    d_wl = _wgrad("proj_dwl", dpl, A['h'])
    d_wg = _wgrad("proj_dwg", dpg, A['h'])
    ql = QL + KVL
    G['w_in'] = jnp.concatenate([d_wc, d_wl[:ql], d_wl[ql + NOPE_DIM:ql + NOPE_DIM + ROPE_DIM], d_wg], axis=0)
    dx, g['mix_norm'] = _dh_rms_bwd("proj_dh_rms", [(dpc, L['w_c']), (dpl, L['w_l']), (dpg, L['w_g'])],
                                    A['x1'], _row(S['mix_norm']), dx, after)
    return dx, G, g


def kernel(x, positions, ffn1_norm, ffn1_w_gate, ffn1_w_up, ffn1_w_down, mix_norm, w_in, gate_bias, conv_w, conv_b, conv_ln_g, conv_ln_b, w_conv_out, cq_norm, ckv_norm, w_uq, w_ukv, q_norm, k_norm, w_mla_out, w_out, ffn2_norm, ffn2_w_gate, ffn2_w_up, ffn2_w_down, loss_target, m_ffn1_norm, m_ffn1_w_gate, m_ffn1_w_up, m_ffn1_w_down, m_mix_norm, m_w_in, m_gate_bias, m_conv_w, m_conv_b, m_conv_ln_g, m_conv_ln_b, m_w_conv_out, m_cq_norm, m_ckv_norm, m_w_uq, m_w_ukv, m_q_norm, m_k_norm, m_w_mla_out, m_w_out, m_ffn2_norm, m_ffn2_w_gate, m_ffn2_w_up, m_ffn2_w_down, v_ffn1_norm, v_ffn1_w_gate, v_ffn1_w_up, v_ffn1_w_down, v_mix_norm, v_w_in, v_gate_bias, v_conv_w, v_conv_b, v_conv_ln_g, v_conv_ln_b, v_w_conv_out, v_cq_norm, v_ckv_norm, v_w_uq, v_w_ukv, v_q_norm, v_k_norm, v_w_mla_out, v_w_out, v_ffn2_norm, v_ffn2_w_gate, v_ffn2_w_up, v_ffn2_w_down):
    w = dict(zip(WEIGHTS, (ffn1_norm, ffn1_w_gate, ffn1_w_up, ffn1_w_down, mix_norm, w_in, gate_bias, conv_w, conv_b, conv_ln_g, conv_ln_b, w_conv_out, cq_norm, ckv_norm, w_uq, w_ukv, q_norm, k_norm, w_mla_out, w_out, ffn2_norm, ffn2_w_gate, ffn2_w_up, ffn2_w_down)))
    m = dict(zip(WEIGHTS, (m_ffn1_norm, m_ffn1_w_gate, m_ffn1_w_up, m_ffn1_w_down, m_mix_norm, m_w_in, m_gate_bias, m_conv_w, m_conv_b, m_conv_ln_g, m_conv_ln_b, m_w_conv_out, m_cq_norm, m_ckv_norm, m_w_uq, m_w_ukv, m_q_norm, m_k_norm, m_w_mla_out, m_w_out, m_ffn2_norm, m_ffn2_w_gate, m_ffn2_w_up, m_ffn2_w_down)))
    v = dict(zip(WEIGHTS, (v_ffn1_norm, v_ffn1_w_gate, v_ffn1_w_up, v_ffn1_w_down, v_mix_norm, v_w_in, v_gate_bias, v_conv_w, v_conv_b, v_conv_ln_g, v_conv_ln_b, v_w_conv_out, v_cq_norm, v_ckv_norm, v_w_uq, v_w_ukv, v_q_norm, v_k_norm, v_w_mla_out, v_w_out, v_ffn2_norm, v_ffn2_w_gate, v_ffn2_w_up, v_ffn2_w_down)))
    depth = ffn1_norm.shape[0]
    T, D = x.shape[1], x.shape[2]
    xs = x.reshape(T, D)
    target = loss_target.reshape(T, D)
    tabs = _rope_tables(positions, T)
    me = 4 * lax.axis_index("x") + 2 * lax.axis_index("y") + lax.axis_index("c")

    big = list(BIG)
    packed = []
    for l in range(depth):
        p, spans = _pack_rows([(w[n][l].T if BIG[n] else w[n][l]).astype(_WIRE_DTYPE) for n in big])
        packed.append(p)
    gathered = {0: _all_gather("gather_weights", packed[0])}
    in_flight = {l: _exchange_start(f"gather_start_{l}", packed[l], _own_slot(packed[l], me), False)
                 for l in range(1, depth)}
    small_names = [(l, n) for l in range(depth) for n in SMALL_SHARDED]
    small_packed, small_spans = _pack_flat([w[n][l].T for l, n in small_names])
    small_gathered = _all_gather("gather_small", small_packed)
    for st in in_flight.values():
        xs, _ = lax.optimization_barrier((xs, st[4]))

    def layer_weights(l):
        W = {}
        for n, span in zip(big, spans):
            sh = _unpack_rows(gathered[l], span, (N_DEV,))
            W[n] = sh.reshape(N_DEV * sh.shape[1], sh.shape[2])
        S = {n: w[n][l] for n in REPLICATED}
        for (ll, n), span in zip(small_names, small_spans):
            if ll == l:
                sh = _unpack_flat(small_gathered, span, (N_DEV,))
                S[n] = sh.reshape(N_DEV * sh.shape[1], sh.shape[2]).T
        return _layer_weights(W), S

    layers, acts = [], []
    hcur = xs
    for l in range(depth):
        if l > 0:
            send_sems, recv_sems, src, land, _ = in_flight[l]
            gathered[l] = _exchange_wait(f"gather_wait_{l}", send_sems, recv_sems, src, land, hcur, False)
        layers.append(layer_weights(l))
        hcur, saved = _layer_fwd(hcur, *layers[l], tabs)
        acts.append(saved)
    dx, loss_part = _loss_grad(hcur, target)

    stage_names = [[n for n in big if n.startswith('ffn1_')], [n for n in big if not n.startswith('ffn1_')]]
    small_grads, reduced, gspans, in_flight = [{} for _ in range(depth)], {}, {}, {}
    after = jnp.zeros((_SUBLANES, _LANES), F32)
    for l in reversed(range(depth)):
        L, S = layers[l]
        for stage in (1, 0):
            if stage:
                dx, G, g = _layer_bwd_mix(dx, L, S, tabs, acts[l], after)
            else:
                da, db, Gw = _ffn_bwd_weights(dx, acts[l]['ffn1'], L['ffn1_w_down'], after)
                G = dict(zip(('ffn1_w_gate', 'ffn1_w_up', 'ffn1_w_down'), Gw))
            blocks = [G[n].reshape(N_DEV, -1, G[n].shape[1]) for n in stage_names[stage]]
            send, gspans[stage] = _pack_rows(blocks, (N_DEV,))
            own = lax.dynamic_index_in_dim(send, me, 0, keepdims=False)
            in_flight[l, stage] = _exchange_start(f"scatter_start_{l}_{stage}", send, _own_slot(own, me), True)
            after = in_flight[l, stage][4]
            if not stage:
                dx, dg = _ffn_bwd_input(dx, acts[l]['ffn1'], S['ffn1_norm'], L['ffn1_w_gate'], L['ffn1_w_up'], da, db, after)
                g = {'ffn1_norm': dg}
            small_grads[l].update(g)
    small_list = [(l, n) for l in range(depth) for n in REPLICATED + SMALL_SHARDED]
    loss_part = loss_part + after[:1, :]
    small_send, sspans = _pack_flat([small_grads[l][n] for l, n in small_list] + [loss_part])
    small_flight = _exchange_start("small_grads_start", small_send, _own_slot(small_send, me), False)

    out4 = {}
    after = small_flight[4]
    for (l, stage), (send_sems, recv_sems, src, land, _) in in_flight.items():
        got = _exchange_wait(f"scatter_wait_{l}_{stage}", send_sems, recv_sems, src, land, after, True)
        reduced = _sum_blocks("sum_grads_direct", got)
        for n, span in zip(stage_names[stage], gspans[stage]):
            gt = _unpack_rows(reduced, span)
            out4[n] = _adamw_layer(l, w[n], gt.T if BIG[n] else gt, m[n], v[n], out4.get(n))
            after = out4[n][1]
    send_sems, recv_sems, src, land, _ = small_flight
    small_all = _sum_blocks("sum_small", _exchange_wait("small_grads_wait", send_sems, recv_sems, src, land, after, False))
    loss = _unpack_flat(small_all, sspans[-1])[0, 0]
    for n in WEIGHTS:
        if n in BIG:
            continue
        per_layer = []
        for l in range(depth):
            gs = _unpack_flat(small_all, sspans[small_list.index((l, n))])
            if n in SMALL_SHARDED:
                cols = w[n].shape[-1]
                gs = lax.dynamic_slice_in_dim(gs, me * cols, cols, axis=1)
            per_layer.append(gs.reshape(w[n].shape[1:]))
        g = jnp.stack(per_layer, axis=0)
        out4[n] = (g,) + _adamw(w[n], g, m[n], v[n])
    return (loss, dx.reshape(1, T, D), *[out4[n][k] for k in range(4) for n in WEIGHTS])
```

```python
import jax
import jax.numpy as jnp
from jax import lax
from jax.experimental import pallas as pl
from jax.experimental.pallas import tpu as pltpu

F32 = jnp.float32
_MXU_DTYPE = jnp.bfloat16
_WIRE_DTYPE = jnp.bfloat16

_LANES = 128
_SUBLANES = 8
_PACKED_ROWS = 16
_V7X_VMEM_BYTES = 64 * 2 ** 20
_VMEM_HEADROOM = 8 * 2 ** 20

N_DEV = 8
EPS = 1e-6
CHUNK = 64
CONV_WIDTH = 31
N_HEADS = 8
NOPE_DIM = 64
ROPE_DIM = 32
QK_DIM = NOPE_DIM + ROPE_DIM
V_DIM = 64
HEAD_W = _LANES
ROPE_THETA = 10000.0
HALO = 32

ADAM_LR = 0.001
ADAM_B1 = 0.9
ADAM_B2 = 0.999
ADAM_EPS = 1e-08
ADAM_WD = 0.01
ADAM_STEP = 10

_ROW_TILE = 512
_MM_TILE = 512
_FFN_TILE = 512
_MM_TILE_N = 1408
_MM_TILE_K = 1024
_TN_TILE_M = 1408
_ATT_TILE = 512
_ATT_ROWS = 32
_ATT_HEADS = 2
_PACK_W = 1024
_PACK_ROW_MULT = 128

WEIGHTS = ['ffn1_norm', 'ffn1_w_gate', 'ffn1_w_up', 'ffn1_w_down', 'mix_norm', 'w_in', 'gate_bias',
           'conv_w', 'conv_b', 'conv_ln_g', 'conv_ln_b', 'w_conv_out', 'cq_norm', 'ckv_norm', 'w_uq',
           'w_ukv', 'q_norm', 'k_norm', 'w_mla_out', 'w_out', 'ffn2_norm', 'ffn2_w_gate', 'ffn2_w_up',
           'ffn2_w_down']
BIG = {'ffn1_w_gate': True, 'ffn1_w_up': True, 'ffn1_w_down': False, 'w_in': True, 'w_conv_out': True,
       'w_uq': True, 'w_ukv': True, 'w_mla_out': True, 'w_out': False, 'ffn2_w_gate': True,
       'ffn2_w_up': True, 'ffn2_w_down': False}
SMALL_SHARDED = ['conv_w', 'gate_bias']
REPLICATED = ['ffn1_norm', 'mix_norm', 'conv_b', 'conv_ln_g', 'conv_ln_b', 'cq_norm', 'ckv_norm',
              'q_norm', 'k_norm', 'ffn2_norm']


def _tile(n, target, mult=_LANES):
    if n <= target:
        return n
    for d in range(target - target % mult, 0, -mult):
        if n % d == 0:
            return d
    return n


def _nbytes(shape, dtype):
    n = 1
    for s in shape:
        n *= s
    return n * jnp.dtype(dtype).itemsize


def _params(dims, block_bytes, extra_bytes=0):
    need = 2 * block_bytes + extra_bytes + _VMEM_HEADROOM
    assert need <= _V7X_VMEM_BYTES, (need, dims)
    return pltpu.CompilerParams(dimension_semantics=dims, vmem_limit_bytes=_V7X_VMEM_BYTES - _VMEM_HEADROOM)


def _mxu(x, scale=None):
    if scale is not None:
        x = x * scale
    return x if x.dtype == _MXU_DTYPE else x.astype(_MXU_DTYPE)


def _dot(a, b, mode):
    dims = {'nn': (((1,), (0,)), ((), ())), 'nt': (((1,), (1,)), ((), ())), 'tn': (((0,), (0,)), ((), ()))}[mode]
    return lax.dot_general(a, b, dims, preferred_element_type=F32)


def _colsum(x):
    return jnp.sum(x, axis=0, keepdims=True)


def _sig(x):
    return 0.5 * jnp.tanh(0.5 * x) + 0.5


def _mm(name, a, b, mode, *, out_dtype=F32, res=None, scale=None, a_scale=None, tm=None, tn=None, tk=None):
    if mode == 'nn':
        (M, K), (_, N) = a.shape, b.shape
    elif mode == 'nt':
        (M, K), (N, _) = a.shape, b.shape
    else:
        (K, M), (_, N) = a.shape, b.shape
    tm = _tile(M, tm or (_TN_TILE_M if mode == 'tn' else _MM_TILE))
    tn = _tile(N, tn or _MM_TILE_N)
    tk = _tile(K, tk or (_MM_TILE_K if mode == 'tn' else 4096))
    nk = K // tk
    grid = (M // tm, N // tn, nk)
    a_spec = pl.BlockSpec((tk, tm), lambda i, j, k: (k, i)) if mode == 'tn' else pl.BlockSpec((tm, tk), lambda i, j, k: (i, k))
    b_spec = pl.BlockSpec((tn, tk), lambda i, j, k: (j, k)) if mode == 'nt' else pl.BlockSpec((tk, tn), lambda i, j, k: (k, j))
    o_spec = pl.BlockSpec((tm, tn), lambda i, j, k: (i, j))
    has_res = res is not None

    def body(*refs):
        a_ref, b_ref = refs[0], refs[1]
        res_ref = refs[2] if has_res else None
        o_ref = refs[3] if has_res else refs[2]
        acc_ref = refs[-1] if nk > 1 else None

        def finish(p):
            if scale is not None:
                p = p * scale
            if has_res:
                p = res_ref[...] + p
            o_ref[...] = p.astype(o_ref.dtype)

        p = _dot(_mxu(a_ref[...], a_scale), _mxu(b_ref[...]), mode)
        if nk == 1:
            finish(p)
        else:
            k = pl.program_id(2)

            @pl.when(k == 0)
            def _():
                acc_ref[...] = p

            @pl.when(k > 0)
            def _():
                acc_ref[...] += p

            @pl.when(k == nk - 1)
            def _():
                finish(acc_ref[...])

    blocks = (_nbytes((tm, tk), a.dtype) + _nbytes((tk, tn), b.dtype) + _nbytes((tm, tn), out_dtype)
              + (_nbytes((tm, tn), F32) if has_res else 0))
    extra = ((2 + (nk > 1)) * _nbytes((tm, tn), F32) + (a.dtype != _MXU_DTYPE) * _nbytes((tm, tk), _MXU_DTYPE)
             + (b.dtype != _MXU_DTYPE) * _nbytes((tk, tn), _MXU_DTYPE))
    return pl.pallas_call(
        body, name=name, grid=grid,
        in_specs=[a_spec, b_spec] + ([o_spec] if has_res else []),
        out_specs=o_spec,
        out_shape=jax.ShapeDtypeStruct((M, N), out_dtype),
        scratch_shapes=[pltpu.VMEM((tm, tn), F32)] if nk > 1 else [],
        compiler_params=_params(("parallel", "parallel", "arbitrary"), blocks, extra),
    )(*([a, b] + ([res] if has_res else [])))


def _wgrad(name, a, b, scale=None):
    return _mm(name, a, b, 'tn', out_dtype=_WIRE_DTYPE, scale=scale)


def _col_chunks(n, width=2 * _LANES):
    return [slice(c, min(c + width, n)) for c in range(0, n, width)]


def _ffn_up(h, wg, wu):
    T, D = h.shape
    F = wg.shape[0]
    tm, tn = _tile(T, _FFN_TILE), _tile(F, _MM_TILE_N)

    def body(h_ref, wg_ref, wu_ref, a_ref, b_ref, u_ref):
        hv = h_ref[...]
        for cols in _col_chunks(tn):
            a = _dot(hv, wg_ref[cols, :], 'nt')
            b = _dot(hv, wu_ref[cols, :], 'nt')
            a_ref[:, cols] = a
            b_ref[:, cols] = b
            u_ref[:, cols] = (a * _sig(a) * b).astype(u_ref.dtype)

    w_spec = pl.BlockSpec((tn, D), lambda j, i: (j, 0))
    o_spec = pl.BlockSpec((tm, tn), lambda j, i: (i, j))
    blocks = _nbytes((tm, D), h.dtype) + 2 * _nbytes((tn, D), wg.dtype) + 2 * _nbytes((tm, tn), F32) + _nbytes((tm, tn), _MXU_DTYPE)
    return pl.pallas_call(
        body, name="ffn_up", grid=(F // tn, T // tm),
        in_specs=[pl.BlockSpec((tm, D), lambda j, i: (i, 0)), w_spec, w_spec],
        out_specs=[o_spec, o_spec, o_spec],
        out_shape=[jax.ShapeDtypeStruct((T, F), F32), jax.ShapeDtypeStruct((T, F), F32),
                   jax.ShapeDtypeStruct((T, F), _MXU_DTYPE)],
        compiler_params=_params(("parallel", "parallel"), blocks, 4 * _nbytes((tm, tn), F32)),
    )(h, wg, wu)


def _ffn_bwd_up(dx, wd, a, b, after):
    T, D = dx.shape
    F = wd.shape[0]
    tm, tn = _tile(T, _FFN_TILE), _tile(F, _MM_TILE_N)

    def body(dx_ref, wd_ref, a_ref, b_ref, after_ref, da_ref, db_ref):
        dy = _mxu(dx_ref[...], 0.5)
        for cols in _col_chunks(tn):
            du = _dot(dy, wd_ref[cols, :], 'nt')
            av, bv = a_ref[:, cols], b_ref[:, cols]
            s = _sig(av)
            db_ref[:, cols] = (du * av * s).astype(db_ref.dtype)
            da_ref[:, cols] = (du * bv * (s * (1.0 + av * (1.0 - s)))).astype(da_ref.dtype)

    t_spec = pl.BlockSpec((tm, tn), lambda j, i: (i, j))
    blocks = _nbytes((tm, D), F32) + _nbytes((tn, D), wd.dtype) + 2 * _nbytes((tm, tn), F32) + 2 * _nbytes((tm, tn), _MXU_DTYPE)
    return pl.pallas_call(
        body, name="ffn_bwd_up", grid=(F // tn, T // tm),
        in_specs=[pl.BlockSpec((tm, D), lambda j, i: (i, 0)), pl.BlockSpec((tn, D), lambda j, i: (j, 0)), t_spec, t_spec,
                  pl.BlockSpec(memory_space=pl.ANY)],
        out_specs=[t_spec, t_spec],
        out_shape=[jax.ShapeDtypeStruct((T, F), _MXU_DTYPE)] * 2,
        compiler_params=_params(("parallel", "parallel"), blocks, 5 * _nbytes((tm, tn), F32)),
    )(dx, wd, a, b, after)


def _row_spec(tm, w, cb=0):
    return pl.BlockSpec((tm, w), lambda i: (i, cb))


def _full_spec(shape):
    return pl.BlockSpec(shape, lambda i: (0,) * len(shape))


def _init_acc(refs):
    @pl.when(pl.program_id(0) == 0)
    def _():
        for r in refs:
            r[...] = jnp.zeros_like(r)


def _rms_fwd(x, g):
    T, D = x.shape
    tm = _tile(T, _ROW_TILE, _PACKED_ROWS)

    def body(x_ref, g_ref, h_ref):
        xv = x_ref[...]
        r = lax.rsqrt(jnp.mean(xv * xv, axis=-1, keepdims=True) + EPS)
        h_ref[...] = (xv * r * g_ref[...]).astype(h_ref.dtype)

    return pl.pallas_call(
        body, name="rms_fwd", grid=(T // tm,),
        in_specs=[_row_spec(tm, D), _full_spec((1, D))], out_specs=_row_spec(tm, D),
        out_shape=jax.ShapeDtypeStruct((T, D), _MXU_DTYPE),
        compiler_params=_params(("parallel",), 2 * _nbytes((tm, D), F32), 4 * _nbytes((tm, D), F32)),
    )(x, g)


def _dh_rms_bwd(name, pairs, x, g, dres, after):
    T, D = x.shape
    tm = _tile(T, _ROW_TILE // 2, _PACKED_ROWS)
    n = len(pairs)

    def body(*refs):
        x_ref, g_ref, dres_ref, after_ref, dx_ref, dg_ref = refs[2 * n:]
        _init_acc([dg_ref])
        dh = None
        for k in range(n):
            p = _dot(_mxu(refs[2 * k][...]), refs[2 * k + 1][...], 'nn')
            dh = p if dh is None else dh + p
        xv = x_ref[...]
        r = lax.rsqrt(jnp.mean(xv * xv, axis=-1, keepdims=True) + EPS)
        xh = xv * r
        dxh = dh * g_ref[...]
        dx_ref[...] = dres_ref[...] + r * (dxh - xh * jnp.mean(dxh * xh, axis=-1, keepdims=True))
        dg_ref[...] += _colsum(dh * xh)

    in_specs, args, blocks = [], [], 3 * _nbytes((tm, D), F32)
    for a, w in pairs:
        in_specs += [_row_spec(tm, a.shape[1]), _full_spec(w.shape)]
        args += [a, w]
        blocks += _nbytes((tm, a.shape[1]), a.dtype) + _nbytes(w.shape, w.dtype)
    return pl.pallas_call(
        body, name=name, grid=(T // tm,),
        in_specs=in_specs + [_row_spec(tm, D), _full_spec((1, D)), _row_spec(tm, D), pl.BlockSpec(memory_space=pl.ANY)],
        out_specs=[_row_spec(tm, D), _full_spec((1, D))],
        out_shape=[jax.ShapeDtypeStruct((T, D), F32), jax.ShapeDtypeStruct((1, D), F32)],
        compiler_params=_params(("arbitrary",), blocks, 6 * _nbytes((tm, D), F32)),
    )(*args, x, g, dres, after)


def _loss_grad(y, target):
    T, D = y.shape
    tm = _tile(T, _ROW_TILE, _SUBLANES)

    def body(y_ref, t_ref, dy_ref, l_ref):
        _init_acc([l_ref])
        err = y_ref[...] - t_ref[...]
        dy_ref[...] = err * (1.0 / D)
        per_row = jnp.sum(err * err, axis=-1, keepdims=True) * (0.5 / D)
        l_ref[...] += jnp.sum(per_row, axis=0, keepdims=True)

    return pl.pallas_call(
        body, name="loss_grad", grid=(T // tm,),
        in_specs=[_row_spec(tm, D), _row_spec(tm, D)],
        out_specs=[_row_spec(tm, D), _full_spec((1, _LANES))],
        out_shape=[jax.ShapeDtypeStruct((T, D), F32), jax.ShapeDtypeStruct((1, _LANES), F32)],
        compiler_params=_params(("arbitrary",), 3 * _nbytes((tm, D), F32), 3 * _nbytes((tm, D), F32)),
    )(y, target)


def _glu(pc, C):
    a, gate = pc[:, :C], pc[:, C:]
    s = _sig(gate)
    return a, s, a * s


def _tap_groups(offsets):
    groups = {}
    for k, off in enumerate(offsets):
        groups.setdefault(off % _SUBLANES, []).append((k, off - off % _SUBLANES))
    return sorted(groups.items())


def _conv_fwd(pc, w, cb, lg, lb):
    T, C2 = pc.shape
    C = C2 // 2
    tm = _tile(T, _ROW_TILE // 2, HALO)
    per = tm // HALO

    def body(pc_ref, ph_ref, w_ref, cb_ref, lg_ref, lb_ref, z_ref, c_ref, ubuf, vbuf):
        i = pl.program_id(0)
        _, _, u_cur = _glu(pc_ref[...], C)
        _, _, u_prev = _glu(ph_ref[...], C)
        ubuf[0:HALO, :] = jnp.where(i > 0, u_prev, 0.0)
        ubuf[HALO:HALO + tm, :] = u_cur
        ubuf[HALO + tm:, :] = jnp.zeros((_SUBLANES, C), F32)
        acc = jnp.zeros((tm, C), F32)
        for shift, taps in _tap_groups([HALO - (CONV_WIDTH - 1) + k for k in range(CONV_WIDTH)]):
            v = None
            for k, base in taps:
                term = w_ref[k:k + 1, :] * ubuf[base:base + tm + _SUBLANES, :]
                v = term if v is None else v + term
            if shift == 0:
                acc = acc + v[:tm]
            else:
                vbuf[...] = v
                acc = acc + vbuf[shift:shift + tm, :]
        z = acc + cb_ref[...]
        z_ref[...] = z
        zc = z - jnp.mean(z, axis=-1, keepdims=True)
        y = zc * lax.rsqrt(jnp.mean(zc * zc, axis=-1, keepdims=True) + EPS) * lg_ref[...] + lb_ref[...]
        c_ref[...] = (y * _sig(y)).astype(c_ref.dtype)

    return pl.pallas_call(
        body, name="conv_fwd", grid=(T // tm,),
        in_specs=[_row_spec(tm, C2), pl.BlockSpec((HALO, C2), lambda i: (jnp.maximum(i * per - 1, 0), 0)),
                  _full_spec((HALO, C)), _full_spec((1, C)), _full_spec((1, C)), _full_spec((1, C))],
        out_specs=[_row_spec(tm, C), _row_spec(tm, C)],
        out_shape=[jax.ShapeDtypeStruct((T, C), F32), jax.ShapeDtypeStruct((T, C), _MXU_DTYPE)],
        scratch_shapes=[pltpu.VMEM((tm + HALO + _SUBLANES, C), F32), pltpu.VMEM((tm + _SUBLANES, C), F32)],
        compiler_params=_params(("parallel",), 3 * _nbytes((tm, C2), F32), 10 * _nbytes((tm, C), F32)),
    )(pc, pc, w, cb, lg, lb)


def _conv_bwd(pc, z, dc, w, lg, lb):
    T, C2 = pc.shape
    C = C2 // 2
    tm = _tile(T, _ROW_TILE // 2, HALO)
    per = tm // HALO
    n = T // tm
    last_halo = T // HALO - 1

    def body(pc_ref, ph_ref, z_ref, zn_ref, dc_ref, dcn_ref, w_ref, lg_ref, lb_ref,
             dpc_ref, dw_ref, dcb_ref, dlg_ref, dlb_ref, ubuf, dzbuf, vbuf, sbuf):
        i = pl.program_id(0)
        _init_acc([dw_ref, dcb_ref, dlg_ref, dlb_ref])
        g, b = lg_ref[...], lb_ref[...]

        def ln_swish_bwd(zv, dcv):
            zc = zv - jnp.mean(zv, axis=-1, keepdims=True)
            r = lax.rsqrt(jnp.mean(zc * zc, axis=-1, keepdims=True) + EPS)
            zh = zc * r
            y = zh * g + b
            s = _sig(y)
            dy = dcv * (s * (1.0 + y * (1.0 - s)))
            dyg = dy * g
            dz = r * (dyg - jnp.mean(dyg, axis=-1, keepdims=True) - zh * jnp.mean(dyg * zh, axis=-1, keepdims=True))
            return dz, dy, zh

        dz_c, dy_c, zh_c = ln_swish_bwd(z_ref[...], dc_ref[...])
        dz_n, _, _ = ln_swish_bwd(zn_ref[...], dcn_ref[...])
        dlg_ref[...] += _colsum(dy_c * zh_c)
        dlb_ref[...] += _colsum(dy_c)
        dcb_ref[...] += _colsum(dz_c)
        dzbuf[0:tm, :] = dz_c
        dzbuf[tm:tm + HALO, :] = jnp.where(i < n - 1, dz_n, 0.0)
        a, s, u_cur = _glu(pc_ref[...], C)
        _, _, u_prev = _glu(ph_ref[...], C)
        ubuf[0:HALO, :] = jnp.where(i > 0, u_prev, 0.0)
        ubuf[HALO:HALO + tm, :] = u_cur
        ubuf[HALO + tm:, :] = jnp.zeros((_SUBLANES, C), F32)
        du = jnp.zeros((tm, C), F32)
        for shift, taps in _tap_groups([CONV_WIDTH - 1 - k for k in range(CONV_WIDTH)]):
            v = None
            for k, base in taps:
                term = w_ref[k:k + 1, :] * dzbuf[base:base + tm + _SUBLANES, :]
                v = term if v is None else v + term
            if shift == 0:
                du = du + v[:tm]
            else:
                vbuf[...] = v
                du = du + vbuf[shift:shift + tm, :]
        for shift, taps in _tap_groups([HALO - (CONV_WIDTH - 1) + k for k in range(CONV_WIDTH)]):
            if shift:
                sbuf[...] = ubuf[shift:shift + tm + HALO, :]
            src = sbuf if shift else ubuf
            for k, base in taps:
                dw_ref[k:k + 1, :] += _colsum(dz_c * src[base:base + tm, :])
        dpc_ref[:, :C] = (du * s).astype(dpc_ref.dtype)
        dpc_ref[:, C:] = (du * a * s * (1.0 - s)).astype(dpc_ref.dtype)

    nxt = lambda i: (jnp.minimum((i + 1) * per, last_halo), 0)
    vec = _full_spec((1, C))
    return pl.pallas_call(
        body, name="conv_bwd", grid=(n,),
        in_specs=[_row_spec(tm, C2), pl.BlockSpec((HALO, C2), lambda i: (jnp.maximum(i * per - 1, 0), 0)),
                  _row_spec(tm, C), pl.BlockSpec((HALO, C), nxt), _row_spec(tm, C), pl.BlockSpec((HALO, C), nxt),
                  _full_spec((HALO, C)), vec, vec],
        out_specs=[_row_spec(tm, C2), _full_spec((HALO, C)), vec, vec, vec],
        out_shape=[jax.ShapeDtypeStruct((T, C2), _MXU_DTYPE), jax.ShapeDtypeStruct((HALO, C), F32)]
                  + [jax.ShapeDtypeStruct((1, C), F32)] * 3,
        scratch_shapes=[pltpu.VMEM((tm + HALO + _SUBLANES, C), F32), pltpu.VMEM((tm + HALO + _SUBLANES, C), F32),
                        pltpu.VMEM((tm + _SUBLANES, C), F32), pltpu.VMEM((tm + HALO, C), F32)],
        compiler_params=_params(("arbitrary",), 5 * _nbytes((tm, C2), F32), 16 * _nbytes((tm, C), F32)),
    )(pc, pc, z, z, dc, dc, w, lg, lb)


def _lat_norm_fwd(pl_, gq, gkv, QL, KVL):
    T, W = pl_.shape
    tm = _tile(T, _ROW_TILE, _PACKED_ROWS)

    def body(p_ref, gq_ref, gkv_ref, cq_ref, ckv_ref):
        def norm(xv, gv):
            return xv * lax.rsqrt(jnp.mean(xv * xv, axis=-1, keepdims=True) + EPS) * gv
        cq_ref[...] = norm(p_ref[:, :QL], gq_ref[...]).astype(cq_ref.dtype)
        ckv_ref[...] = norm(p_ref[:, QL:QL + KVL], gkv_ref[...]).astype(ckv_ref.dtype)

    return pl.pallas_call(
        body, name="lat_norm_fwd", grid=(T // tm,),
        in_specs=[_row_spec(tm, W), _full_spec((1, QL)), _full_spec((1, KVL))],
        out_specs=[_row_spec(tm, QL), _row_spec(tm, KVL)],
        out_shape=[jax.ShapeDtypeStruct((T, QL), _MXU_DTYPE), jax.ShapeDtypeStruct((T, KVL), _MXU_DTYPE)],
        compiler_params=_params(("parallel",), 2 * _nbytes((tm, W), F32), 4 * _nbytes((tm, W), F32)),
    )(pl_, gq, gkv)


def _lat_norm_bwd(pl_, gq, gkv, dcq, dckv, drope, QL, KVL):
    T, W = pl_.shape
    tm = _tile(T, _ROW_TILE, _PACKED_ROWS)

    def body(p_ref, gq_ref, gkv_ref, dcq_ref, dckv_ref, dr_ref, dp_ref, dgq_ref, dgkv_ref):
        _init_acc([dgq_ref, dgkv_ref])

        def bwd(xv, gv, dv, dg_ref):
            r = lax.rsqrt(jnp.mean(xv * xv, axis=-1, keepdims=True) + EPS)
            xh = xv * r
            dxh = dv * gv
            dg_ref[...] += _colsum(dv * xh)
            return r * (dxh - xh * jnp.mean(dxh * xh, axis=-1, keepdims=True))

        dp_ref[:, :QL] = bwd(p_ref[:, :QL], gq_ref[...], dcq_ref[...], dgq_ref).astype(dp_ref.dtype)
        dp_ref[:, QL:QL + KVL] = bwd(p_ref[:, QL:QL + KVL], gkv_ref[...], dckv_ref[...], dgkv_ref).astype(dp_ref.dtype)
        dp_ref[:, QL + KVL:] = dr_ref[...].astype(dp_ref.dtype)

    return pl.pallas_call(
        body, name="lat_norm_bwd", grid=(T // tm,),
        in_specs=[_row_spec(tm, W), _full_spec((1, QL)), _full_spec((1, KVL)), _row_spec(tm, QL), _row_spec(tm, KVL),
                  _row_spec(tm, HEAD_W)],
        out_specs=[_row_spec(tm, W), _full_spec((1, QL)), _full_spec((1, KVL))],
        out_shape=[jax.ShapeDtypeStruct((T, W), _MXU_DTYPE), jax.ShapeDtypeStruct((1, QL), F32),
                   jax.ShapeDtypeStruct((1, KVL), F32)],
        compiler_params=_params(("arbitrary",), 4 * _nbytes((tm, W), F32), 6 * _nbytes((tm, W), F32)),
    )(pl_, gq, gkv, dcq, dckv, drope)


def _norm_rope(xv, gv, ct, sa, sb):
    r = lax.rsqrt(jnp.sum(xv * xv, axis=-1, keepdims=True) * (1.0 / QK_DIM) + EPS)
    xn = xv * r * gv
    half = ROPE_DIM // 2
    return xn * ct + pltpu.roll(xn, HEAD_W - half, 1) * sa + pltpu.roll(xn, half, 1) * sb


def _norm_rope_bwd(xv, gv, ct, sa, sb, dout):
    half = ROPE_DIM // 2
    dxn = dout * ct + pltpu.roll(dout * sa, half, 1) + pltpu.roll(dout * sb, HEAD_W - half, 1)
    r = lax.rsqrt(jnp.sum(xv * xv, axis=-1, keepdims=True) * (1.0 / QK_DIM) + EPS)
    xh = xv * r
    dxh = dxn * gv
    dx = r * (dxh - xh * (jnp.sum(dxh * xh, axis=-1, keepdims=True) * (1.0 / QK_DIM)))
    return dx, _colsum(dxn * xh)


def _qk_prep_fwd(q_raw, k_raw, pl_, rope_cb, ct, sa, sb, gq, gk):
    T, HW = q_raw.shape
    H = HW // HEAD_W
    tm = _tile(T, _ROW_TILE, _PACKED_ROWS)

    def body(q_ref, k_ref, r_ref, ct_ref, sa_ref, sb_ref, gq_ref, gk_ref, qn_ref, kn_ref):
        ctv, sav, sbv, rv = ct_ref[...], sa_ref[...], sb_ref[...], r_ref[...]
        for h in range(H):
            hs = slice(h * HEAD_W, (h + 1) * HEAD_W)
            qn_ref[:, hs] = _norm_rope(q_ref[:, hs], gq_ref[...], ctv, sav, sbv).astype(qn_ref.dtype)
            kn_ref[:, hs] = _norm_rope(k_ref[:, hs] + rv, gk_ref[...], ctv, sav, sbv).astype(kn_ref.dtype)

    tb = _row_spec(tm, HEAD_W)
    gb = _full_spec((1, HEAD_W))
    return pl.pallas_call(
        body, name="qk_prep_fwd", grid=(T // tm,),
        in_specs=[_row_spec(tm, HW), _row_spec(tm, HW), _row_spec(tm, HEAD_W, rope_cb), tb, tb, tb, gb, gb],
        out_specs=[_row_spec(tm, HW), _row_spec(tm, HW)],
        out_shape=[jax.ShapeDtypeStruct((T, HW), _MXU_DTYPE)] * 2,
        compiler_params=_params(("parallel",), 3 * _nbytes((tm, HW), F32) + 4 * _nbytes((tm, HEAD_W), F32), 8 * _nbytes((tm, HEAD_W), F32)),
    )(q_raw, k_raw, pl_, ct, sa, sb, gq, gk)


def _qk_prep_bwd(q_raw, k_raw, pl_, rope_cb, ct, sa, sb, gq, gk, dqn, dkn):
    T, HW = q_raw.shape
    H = HW // HEAD_W
    tm = _tile(T, _ROW_TILE, _PACKED_ROWS)

    def body(q_ref, k_ref, r_ref, ct_ref, sa_ref, sb_ref, gq_ref, gk_ref, dqn_ref, dkn_ref,
             dq_ref, dk_ref, dr_ref, dgq_ref, dgk_ref):
        _init_acc([dgq_ref, dgk_ref])
        ctv, sav, sbv, rv = ct_ref[...], sa_ref[...], sb_ref[...], r_ref[...]
        dr = dgq_sum = dgk_sum = None
        for h in range(H):
            hs = slice(h * HEAD_W, (h + 1) * HEAD_W)
            dq, dgq = _norm_rope_bwd(q_ref[:, hs], gq_ref[...], ctv, sav, sbv, dqn_ref[:, hs])
            dk, dgk = _norm_rope_bwd(k_ref[:, hs] + rv, gk_ref[...], ctv, sav, sbv, dkn_ref[:, hs])
            dq_ref[:, hs] = dq.astype(dq_ref.dtype)
            dk_ref[:, hs] = dk.astype(dk_ref.dtype)
            dr = dk if h == 0 else dr + dk
            dgq_sum = dgq if h == 0 else dgq_sum + dgq
            dgk_sum = dgk if h == 0 else dgk_sum + dgk
        dr_ref[...] = dr
        dgq_ref[...] += dgq_sum
        dgk_ref[...] += dgk_sum

    tb = _row_spec(tm, HEAD_W)
    gb = _full_spec((1, HEAD_W))
    hb = _row_spec(tm, HW)
    return pl.pallas_call(
        body, name="qk_prep_bwd", grid=(T // tm,),
        in_specs=[hb, hb, _row_spec(tm, HEAD_W, rope_cb), tb, tb, tb, gb, gb, hb, hb],
        out_specs=[hb, hb, tb, gb, gb],
        out_shape=[jax.ShapeDtypeStruct((T, HW), _MXU_DTYPE)] * 2 + [jax.ShapeDtypeStruct((T, HEAD_W), F32)]
                  + [jax.ShapeDtypeStruct((1, HEAD_W), F32)] * 2,
        compiler_params=_params(("arbitrary",), 5 * _nbytes((tm, HW), F32) + 5 * _nbytes((tm, HEAD_W), F32), 12 * _nbytes((tm, HEAD_W), F32)),
    )(q_raw, k_raw, pl_, ct, sa, sb, gq, gk, dqn, dkn)


_NEG = -1e30
_LOG2E = 1.4426950408889634
_SCORE_C = QK_DIM ** -0.5 * _LOG2E


def _pieces(t, rb, diag):
    assert CHUNK % rb == 0 and rb % _SUBLANES == 0
    out = []
    for r in range(t // rb):
        lo = (r * rb // CHUNK) * CHUNK if diag else 0
        for c in range(t // _LANES):
            cut = None if (c + 1) * _LANES <= lo else max(lo - c * _LANES, 0)
            out.append((slice(r * rb, (r + 1) * rb), slice(c * _LANES, (c + 1) * _LANES), c, cut))
    return out


def _groups(x):
    return x.reshape(x.shape[0] // _SUBLANES, _SUBLANES, x.shape[1])


def _all_sublanes(x8, op):
    return jnp.broadcast_to(op(x8, axis=0, keepdims=True), x8.shape)


def _lane_ge(rb, cut):
    return lax.broadcasted_iota(jnp.int32, (rb, _LANES), 1) >= cut


def _attn_fwd(q, k, v):
    T, HW = q.shape
    H = HW // HEAD_W
    t = _tile(T, _ATT_TILE)
    rb = min(_ATT_ROWS, t)
    nc = t // _LANES

    hp = _ATT_HEADS if H % _ATT_HEADS == 0 else 1
    W = hp * HEAD_W

    def body(q_ref, k_ref, v_ref, o_ref, lse_ref, s_s, p_s, m_s, l_s, acc_s):
        i = pl.program_id(1)
        m_s[...] = jnp.full_like(m_s, _NEG)
        l_s[...] = jnp.zeros_like(l_s)
        acc_s[...] = jnp.zeros_like(acc_s)

        def head_step(hh, rows, diag):
            hs = slice(hh * HEAD_W, (hh + 1) * HEAD_W)
            s_s[hh] = _dot(k_ref[rows, hs], q_ref[:, hs], 'nt')
            pieces = _pieces(t, rb, diag)

            def scores(rs, cs, cut):
                sb = s_s[hh, rs, cs]
                return jnp.where(_lane_ge(rb, cut), sb, _NEG) if cut else sb

            mx = [None] * nc
            for rs, cs, c, cut in pieces:
                if cut is not None:
                    g = jnp.max(_groups(scores(rs, cs, cut)), axis=0)
                    mx[c] = g if mx[c] is None else jnp.maximum(mx[c], g)
            m_new, alpha = [], []
            for c in range(nc):
                cs = slice(c * _LANES, (c + 1) * _LANES)
                m_prev = m_s[hh, :, cs]
                m_new.append(jnp.maximum(m_prev, _all_sublanes(mx[c], jnp.max)))
                alpha.append(jnp.exp2((m_prev - m_new[c]) * _SCORE_C))
                m_s[hh, :, cs] = m_new[c]
            lsum = [jnp.zeros((_SUBLANES, _LANES), F32)] * nc
            for rs, cs, c, cut in pieces:
                if cut is None:
                    p_s[hh, rs, cs] = jnp.zeros((rb, _LANES), p_s.dtype)
                    continue
                p = jnp.exp2((scores(rs, cs, cut) - jnp.tile(m_new[c], (rb // _SUBLANES, 1))) * _SCORE_C)
                lsum[c] = lsum[c] + jnp.sum(_groups(p), axis=0)
                p_s[hh, rs, cs] = p.astype(p_s.dtype)
            for c in range(nc):
                cs = slice(c * _LANES, (c + 1) * _LANES)
                l_s[hh, :, cs] = alpha[c] * l_s[hh, :, cs] + _all_sublanes(lsum[c], jnp.sum)
            a = jnp.tile(jnp.concatenate(alpha, axis=1), (HEAD_W // _SUBLANES, 1))
            acc_s[hh] = a * acc_s[hh] + _dot(v_ref[rows, hs], p_s[hh], 'tn')

        def step(j, diag):
            rows = pl.ds(pl.multiple_of(j * t, t), t)
            for hh in range(hp):
                head_step(hh, rows, diag)

        def loop_body(j, carry):
            step(j, False)
            return carry

        lax.fori_loop(0, i, loop_body, 0)
        step(i, True)
        for hh in range(hp):
            l = l_s[hh]
            o_t = acc_s[hh] / jnp.tile(l, (HEAD_W // _SUBLANES, 1))
            o_ref[:, hh * HEAD_W:(hh + 1) * HEAD_W] = o_t.T.astype(o_ref.dtype)
            lse_ref[hh * _SUBLANES:(hh + 1) * _SUBLANES, :] = m_s[hh] * _SCORE_C + jnp.log(l) * _LOG2E

    qb = pl.BlockSpec((t, W), lambda h, i: (i, h))
    kb = pl.BlockSpec((T, W), lambda h, i: (0, h))
    st = pltpu.VMEM((hp, _SUBLANES, t), F32)
    return pl.pallas_call(
        body, name="attn_fwd", grid=(H // hp, T // t),
        in_specs=[qb, kb, kb], out_specs=[qb, pl.BlockSpec((hp * _SUBLANES, t), lambda h, i: (h, i))],
        out_shape=[jax.ShapeDtypeStruct((T, HW), _MXU_DTYPE), jax.ShapeDtypeStruct((H * _SUBLANES, T), F32)],
        scratch_shapes=[pltpu.VMEM((hp, t, t), F32), pltpu.VMEM((hp, t, t), _MXU_DTYPE), st, st, pltpu.VMEM((hp, HEAD_W, t), F32)],
        compiler_params=_params(("parallel", "arbitrary"), 2 * _nbytes((T, W), _MXU_DTYPE) + 3 * _nbytes((t, W), F32),
                                4 * hp * _nbytes((t, t), F32)),
    )(q, k, v)


def _attn_bwd(q, k, v, o, do, lse2):
    T, HW = q.shape
    H = HW // HEAD_W
    t = _tile(T, _ATT_TILE)
    n = T // t
    rb = min(_ATT_ROWS, t)
    scale = QK_DIM ** -0.5

    hp = _ATT_HEADS if H % _ATT_HEADS == 0 else 1
    W = hp * HEAD_W

    def body(k_ref, v_ref, q_ref, o_ref, do_ref, lse_ref, dq_ref, dk_ref, dv_ref,
             s_s, dp_s, p_s, ds_s, dk_s, dv_s, delta_s, dqt_s):
        j = pl.program_id(1)

        @pl.when(j == 0)
        def _():
            dqt_s[...] = jnp.zeros_like(dqt_s)
            for hh in range(hp):
                hs = slice(hh * HEAD_W, (hh + 1) * HEAD_W)
                for b in range(n):
                    rs = slice(b * t, (b + 1) * t)
                    d = jnp.sum(do_ref[rs, hs].astype(F32) * o_ref[rs, hs].astype(F32), axis=1, keepdims=True)
                    delta_s[hh, :, rs] = jnp.broadcast_to(d, (t, HEAD_W)).T[0:_SUBLANES, :]

        dk_s[...] = jnp.zeros_like(dk_s)
        dv_s[...] = jnp.zeros_like(dv_s)

        def head_step(hh, base, diag):
            hs = slice(hh * HEAD_W, (hh + 1) * HEAD_W)
            kv, vv = k_ref[:, hs], v_ref[:, hs]
            qv, dov = q_ref[pl.ds(base, t), hs], do_ref[pl.ds(base, t), hs]
            s_s[hh] = _dot(kv, qv, 'nt')
            dp_s[hh] = _dot(vv, dov, 'nt')
            for rs, cs, c, cut in _pieces(t, rb, diag):
                if cut is None:
                    p_s[hh, rs, cs] = jnp.zeros((rb, _LANES), p_s.dtype)
                    ds_s[hh, rs, cs] = jnp.zeros((rb, _LANES), ds_s.dtype)
                    continue
                lanes = pl.ds(pl.multiple_of(base + c * _LANES, _LANES), _LANES)
                lse = jnp.tile(lse_ref[hh * _SUBLANES:(hh + 1) * _SUBLANES, lanes], (rb // _SUBLANES, 1))
                dl = jnp.tile(delta_s[hh, :, lanes], (rb // _SUBLANES, 1))
                p = jnp.exp2(s_s[hh, rs, cs] * _SCORE_C - lse)
                if cut:
                    p = jnp.where(_lane_ge(rb, cut), p, 0.0)
                p_s[hh, rs, cs] = p.astype(p_s.dtype)
                ds_s[hh, rs, cs] = (p * (dp_s[hh, rs, cs] - dl)).astype(ds_s.dtype)
            dsv = ds_s[hh]
            dv_s[hh] += _dot(p_s[hh], dov, 'nn')
            dk_s[hh] += _dot(dsv, qv, 'nn')
            dqt_s[hh, :, pl.ds(base, t)] += _dot(kv, dsv, 'tn')

        def step(i, diag):
            base = pl.multiple_of(i * t, t)
            for hh in range(hp):
                head_step(hh, base, diag)

        def loop_body(i, carry):
            step(i, False)
            return carry

        step(j, True)
        lax.fori_loop(j + 1, n, loop_body, 0)
        for hh in range(hp):
            hs = slice(hh * HEAD_W, (hh + 1) * HEAD_W)
            dk_ref[:, hs] = dk_s[hh] * scale
            dv_ref[:, hs] = dv_s[hh].astype(dv_ref.dtype)

        @pl.when(j == n - 1)
        def _():
            for hh in range(hp):
                for b in range(n):
                    rs = slice(b * t, (b + 1) * t)
                    dq_ref[rs, hh * HEAD_W:(hh + 1) * HEAD_W] = dqt_s[hh, :, rs].T * scale

    jb = pl.BlockSpec((t, W), lambda h, j: (j, h))
    fb = pl.BlockSpec((T, W), lambda h, j: (0, h))
    sc = pltpu.VMEM((hp, t, HEAD_W), F32)
    return pl.pallas_call(
        body, name="attn_bwd", grid=(H // hp, n),
        in_specs=[jb, jb, fb, fb, fb, pl.BlockSpec((hp * _SUBLANES, T), lambda h, j: (h, 0))], out_specs=[fb, jb, jb],
        out_shape=[jax.ShapeDtypeStruct((T, HW), F32), jax.ShapeDtypeStruct((T, HW), F32),
                   jax.ShapeDtypeStruct((T, HW), _MXU_DTYPE)],
        scratch_shapes=[pltpu.VMEM((hp, t, t), F32), pltpu.VMEM((hp, t, t), F32), pltpu.VMEM((hp, t, t), _MXU_DTYPE),
                        pltpu.VMEM((hp, t, t), _MXU_DTYPE), sc, sc, pltpu.VMEM((hp, _SUBLANES, T), F32),
                        pltpu.VMEM((hp, HEAD_W, T), F32)],
        compiler_params=_params(("parallel", "arbitrary"), 3 * _nbytes((T, W), _MXU_DTYPE) + _nbytes((T, W), F32)
                                + 4 * _nbytes((t, W), F32), hp * (4 * _nbytes((t, t), F32) + _nbytes((T, HEAD_W), F32))),
    )(k, v, q, o, do, lse2)


def _gate_fwd(pg, gb, yc, ym):
    T, D = yc.shape
    tm = _tile(T, _ROW_TILE, _PACKED_ROWS)

    def body(g0_ref, g1_ref, gb_ref, yc_ref, ym_ref, y_ref):
        s0 = _sig(g0_ref[...] + gb_ref[0:1, :])
        s1 = _sig(g1_ref[...] + gb_ref[1:2, :])
        y_ref[...] = (s0 * yc_ref[...] + s1 * ym_ref[...]).astype(y_ref.dtype)

    return pl.pallas_call(
        body, name="gate_fwd", grid=(T // tm,),
        in_specs=[_row_spec(tm, D, 0), _row_spec(tm, D, 1), _full_spec((2, D)), _row_spec(tm, D), _row_spec(tm, D)],
        out_specs=_row_spec(tm, D),
        out_shape=jax.ShapeDtypeStruct((T, D), _MXU_DTYPE),
        compiler_params=_params(("parallel",), 5 * _nbytes((tm, D), F32), 4 * _nbytes((tm, D), F32)),
    )(pg, pg, gb, yc, ym)


def _gate_bwd(pg, gb, yc, ym, dy):
    T, D = yc.shape
    tm = _tile(T, _ROW_TILE, _PACKED_ROWS)

    def body(g0_ref, g1_ref, gb_ref, yc_ref, ym_ref, dy_ref, dyc_ref, dym_ref, dpg_ref, dgb_ref):
        _init_acc([dgb_ref])
        dyv = dy_ref[...]
        s0 = _sig(g0_ref[...] + gb_ref[0:1, :])
        s1 = _sig(g1_ref[...] + gb_ref[1:2, :])
        dyc_ref[...] = (dyv * s0).astype(dyc_ref.dtype)
        dym_ref[...] = (dyv * s1).astype(dym_ref.dtype)
        d0 = dyv * yc_ref[...] * s0 * (1.0 - s0)
        d1 = dyv * ym_ref[...] * s1 * (1.0 - s1)
        dpg_ref[:, :D] = d0.astype(dpg_ref.dtype)
        dpg_ref[:, D:] = d1.astype(dpg_ref.dtype)
        dgb_ref[0:1, :] += _colsum(d0)
        dgb_ref[1:2, :] += _colsum(d1)

    return pl.pallas_call(
        body, name="gate_bwd", grid=(T // tm,),
        in_specs=[_row_spec(tm, D, 0), _row_spec(tm, D, 1), _full_spec((2, D)), _row_spec(tm, D), _row_spec(tm, D),
                  _row_spec(tm, D)],
        out_specs=[_row_spec(tm, D), _row_spec(tm, D), _row_spec(tm, 2 * D), _full_spec((2, D))],
        out_shape=[jax.ShapeDtypeStruct((T, D), _MXU_DTYPE)] * 2 + [jax.ShapeDtypeStruct((T, 2 * D), _MXU_DTYPE),
                                                                   jax.ShapeDtypeStruct((2, D), F32)],
        compiler_params=_params(("arbitrary",), 7 * _nbytes((tm, D), F32), 8 * _nbytes((tm, D), F32)),
    )(pg, pg, gb, yc, ym, dy)


def _place():
    return lax.axis_index("x"), lax.axis_index("y"), lax.axis_index("c")


def _all_gather(name, xs):
    R, C = xs.shape
    MESH = pl.DeviceIdType.MESH

    def body(x_ref, out_ref, send_sems, recv_sems, local_sem):
        x, y, c = _place()
        me, sibling = (x, y, c), (x, y, 1 - c)
        chips = [(1 - x, y), (x, 1 - y), (1 - x, 1 - y)]

        def rows(px, py, pc):
            return out_ref.at[4 * px + 2 * py + pc]

        def copy(k, block, to, src=None):
            return pltpu.make_async_remote_copy(
                src_ref=rows(*block) if src is None else src, dst_ref=rows(*block),
                send_sem=send_sems.at[k], recv_sem=recv_sems.at[k], device_id=to, device_id_type=MESH)

        mine = pltpu.make_async_copy(x_ref, rows(*me), local_sem)
        mine.start()
        first = [copy(0, me, sibling, src=x_ref)]
        first += [copy(1 + j, me, (*chip, c), src=x_ref) for j, chip in enumerate(chips)]
        for cp in first:
            cp.start()
        passed = [copy(4 + j, (*chip, c), sibling) for j, chip in enumerate(chips)]
        for j, chip in enumerate(chips):
            copy(1 + j, (*chip, c), me).wait_recv()
            passed[j].start()
        copy(0, sibling, me).wait_recv()
        for j, chip in enumerate(chips):
            copy(4 + j, (*chip, 1 - c), me).wait_recv()
        for cp in first + passed:
            cp.wait_send()
        mine.wait()

    return pl.pallas_call(
        body, name=name,
        out_shape=jax.ShapeDtypeStruct((N_DEV, R, C), xs.dtype),
        in_specs=[pl.BlockSpec(memory_space=pl.ANY)], out_specs=pl.BlockSpec(memory_space=pl.ANY),
        scratch_shapes=[pltpu.SemaphoreType.DMA((7,)), pltpu.SemaphoreType.DMA((7,)), pltpu.SemaphoreType.DMA],
    )(xs)


_HBM_SPEC = pl.BlockSpec(memory_space=pltpu.HBM)
_SEM_SPEC = pl.BlockSpec(memory_space=pltpu.SEMAPHORE)
_DATAFLOW = pltpu.SideEffectType.DATAFLOW_SIDE_EFFECTING


def _peers():
    x, y, c = _place()
    out = []
    for k in range(1, N_DEV):
        px = 1 - x if k & 4 else x
        py = 1 - y if k & 2 else y
        pc = 1 - c if k & 1 else c
        out.append((k - 1, (px, py, pc), 4 * px + 2 * py + pc))
    return 4 * x + 2 * y + c, out


def _exchange_copies(x_ref, land_ref, send_sems, recv_sems, scatter):
    me, peers = _peers()
    return [pltpu.make_async_remote_copy(
        src_ref=x_ref.at[idx] if scatter else x_ref, dst_ref=land_ref.at[me],
        send_sem=send_sems.at[k], recv_sem=recv_sems.at[k], device_id=dev, device_id_type=pl.DeviceIdType.MESH)
        for k, dev, idx in peers]


def _exchange_start(name, xs, land, scatter):
    def body(x_ref, land_ref, send_sems, recv_sems, x_thru, land_thru, token):
        for cp in _exchange_copies(x_ref, land_ref, send_sems, recv_sems, scatter):
            cp.start()
        token[...] = jnp.zeros_like(token)

    sems = pltpu.SemaphoreType.DMA((N_DEV - 1,))
    return pl.pallas_call(
        body, name=name,
        out_shape=(sems, sems, pltpu.HBM(xs.shape, xs.dtype), pltpu.HBM(land.shape, land.dtype),
                   jax.ShapeDtypeStruct((_SUBLANES, _LANES), F32)),
        in_specs=(_HBM_SPEC, _HBM_SPEC),
        out_specs=(_SEM_SPEC, _SEM_SPEC, _HBM_SPEC, _HBM_SPEC, pl.BlockSpec(memory_space=pltpu.VMEM)),
        input_output_aliases={0: 2, 1: 3},
        compiler_params=pltpu.CompilerParams(has_side_effects=_DATAFLOW),
    )(pltpu.with_memory_space_constraint(xs, pltpu.HBM), pltpu.with_memory_space_constraint(land, pltpu.HBM))


def _exchange_wait(name, send_sems, recv_sems, xs, land, after, scatter):
    def body(x_ref, land_ref, send_sems, recv_sems, after_ref, x_dead, got_ref):
        for cp in _exchange_copies(x_ref, land_ref, send_sems, recv_sems, scatter):
            cp.wait_send()
            cp.wait_recv()

    return pl.pallas_call(
        body, name=name,
        out_shape=(pltpu.HBM(xs.shape, xs.dtype), pltpu.HBM(land.shape, land.dtype)),
        in_specs=(_HBM_SPEC, _HBM_SPEC, _SEM_SPEC, _SEM_SPEC, pl.BlockSpec(memory_space=pl.ANY)),
        out_specs=(_HBM_SPEC, _HBM_SPEC), input_output_aliases={0: 0, 1: 1},
        compiler_params=pltpu.CompilerParams(has_side_effects=_DATAFLOW),
    )(xs, land, send_sems, recv_sems, after)[1]


def _own_slot(block, me):
    land = lax.empty((N_DEV,) + block.shape, block.dtype)
    return lax.dynamic_update_slice(land, block[None], (me,) + (0,) * block.ndim)


def _sum_blocks(name, parts, scale=None):
    n, R, C = parts.shape
    tr = _tile(R, _PACK_ROW_MULT, _PACKED_ROWS)

    def body(p_ref, o_ref):
        acc = p_ref[0].astype(F32)
        for s in range(1, n):
            acc = acc + p_ref[s].astype(F32)
        if scale is not None:
            acc = acc * scale
        o_ref[...] = acc

    return pl.pallas_call(
        body, name=name, grid=(R // tr,),
        in_specs=[pl.BlockSpec((n, tr, C), lambda i: (0, i, 0))], out_specs=pl.BlockSpec((tr, C), lambda i: (i, 0)),
        out_shape=jax.ShapeDtypeStruct((R, C), F32),
        compiler_params=_params(("parallel",), _nbytes((n, tr, C), parts.dtype) + _nbytes((tr, C), F32), 2 * _nbytes((tr, C), F32)),
    )(parts)


def _adamw_math(wv, gv, mv, vv):
    mv = ADAM_B1 * mv + (1.0 - ADAM_B1) * gv
    vv = ADAM_B2 * vv + (1.0 - ADAM_B2) * (gv * gv)
    m_hat = mv / (1.0 - ADAM_B1 ** ADAM_STEP)
    v_hat = vv / (1.0 - ADAM_B2 ** ADAM_STEP)
    return -ADAM_LR * (m_hat / (jnp.sqrt(v_hat) + ADAM_EPS) + ADAM_WD * wv), mv, vv


def _adamw_layer(l, w, g, m, v, outs, after):
    depth, R, C = w.shape
    tr = _tile(R, _ROW_TILE, _SUBLANES)

    def body(w_ref, g_ref, m_ref, v_ref, after_ref, *rest):
        go_ref, d_ref, nm_ref, nv_ref = rest[-4:]
        gv = g_ref[...]
        go_ref[...] = gv
        d_ref[...], nm_ref[...], nv_ref[...] = _adamw_math(w_ref[...], gv, m_ref[...], v_ref[...])

    lay = pl.BlockSpec((None, tr, C), lambda i: (l, i, 0))
    shape = jax.ShapeDtypeStruct(w.shape, F32)
    return pl.pallas_call(
        body, name="adamw_layer", grid=(R // tr,),
        in_specs=[lay, pl.BlockSpec((tr, C), lambda i: (i, 0)), lay, lay] + [pl.BlockSpec(memory_space=pl.ANY)] * (5 if outs else 1),
        out_specs=[lay] * 4, out_shape=[shape] * 4,
        input_output_aliases={5 + k: k for k in range(4)} if outs else {},
        compiler_params=_params(("parallel",), 8 * _nbytes((tr, C), F32), 4 * _nbytes((tr, C), F32)),
    )(w, g, m, v, after, *(outs or ()))


def _adamw(w, g, m, v):
    shape = w.shape
    w2, g2, m2, v2 = (t.reshape(-1, shape[-1]) for t in (w, g, m, v))
    R, C = w2.shape
    tr = _tile(R, _ROW_TILE, _SUBLANES)

    def body(w_ref, g_ref, m_ref, v_ref, d_ref, nm_ref, nv_ref):
        d_ref[...], nm_ref[...], nv_ref[...] = _adamw_math(w_ref[...], g_ref[...], m_ref[...], v_ref[...])

    spec = pl.BlockSpec((tr, C), lambda i: (i, 0))
    outs = pl.pallas_call(
        body, name="adamw", grid=(R // tr,),
        in_specs=[spec] * 4, out_specs=[spec] * 3,
        out_shape=[jax.ShapeDtypeStruct((R, C), F32)] * 3,
        compiler_params=_params(("parallel",), 7 * _nbytes((tr, C), F32), 4 * _nbytes((tr, C), F32)),
    )(w2, g2, m2, v2)
    return tuple(o.reshape(shape) for o in outs)


def _pack_rows(parts, lead=()):
    out, spans, r0 = [], [], 0
    unit = _PACKED_ROWS * _PACK_W
    for p in parts:
        flat = p.reshape(lead + (-1,))
        size = flat.shape[-1]
        pad = (-size) % unit
        if pad:
            flat = jnp.pad(flat, [(0, 0)] * len(lead) + [(0, pad)])
        rows = (size + pad) // _PACK_W
        out.append(flat.reshape(lead + (rows, _PACK_W)))
        spans.append((r0, rows, p.shape[len(lead):]))
        r0 += rows
    pad = (-r0) % _PACK_ROW_MULT
    if pad:
        out.append(jnp.zeros(lead + (pad, _PACK_W), out[0].dtype))
    return jnp.concatenate(out, axis=len(lead)), spans


def _unpack_rows(packed, span, lead=()):
    r0, rows, shape = span
    size = 1
    for s in shape:
        size *= s
    sl = lax.slice_in_dim(packed, r0, r0 + rows, axis=len(lead))
    return sl.reshape(lead + (-1,))[..., :size].reshape(lead + tuple(shape))


def _pack_flat(parts):
    flat, spans, at = [], [], 0
    for p in parts:
        size = p.size
        padded = size + (-size) % _LANES
        flat.append(jnp.pad(p.reshape(-1), (0, padded - size)))
        spans.append((at, size, p.shape))
        at += padded
    tail = (-at) % (_PACKED_ROWS * _PACK_W)
    if tail:
        flat.append(jnp.zeros((tail,), flat[0].dtype))
    return jnp.concatenate(flat).reshape(-1, _PACK_W), spans


def _unpack_flat(packed, span, lead=()):
    at, size, shape = span
    return lax.slice_in_dim(packed.reshape(lead + (-1,)), at, at + size, axis=len(lead)).reshape(lead + tuple(shape))


def _rope_tables(positions, T):
    inv_freq = ROPE_THETA ** (-jnp.arange(0, ROPE_DIM, 2, dtype=F32) / ROPE_DIM)
    ang = positions.reshape(T, 1).astype(F32) * inv_freq
    cos, sin = jnp.cos(ang), jnp.sin(ang)
    half = ROPE_DIM // 2
    z = lambda n: jnp.zeros((T, n), F32)
    tail = HEAD_W - QK_DIM
    ct = jnp.concatenate([jnp.ones((T, NOPE_DIM), F32), cos, cos, z(tail)], axis=1)
    sa = jnp.concatenate([z(NOPE_DIM), -sin, z(half), z(tail)], axis=1)
    sb = jnp.concatenate([z(NOPE_DIM), z(half), sin, z(tail)], axis=1)
    return ct, sa, sb


def _pad_heads(wt, per_head, keep_from=0, keep=None):
    K = wt.shape[1]
    keep = per_head if keep is None else keep
    w3 = wt.reshape(N_HEADS, per_head, K)[:, keep_from:keep_from + keep]
    return jnp.pad(w3, ((0, 0), (0, HEAD_W - keep), (0, 0))).reshape(N_HEADS * HEAD_W, K)


def _unpad_heads(g, keep):
    return g.reshape(N_HEADS, HEAD_W, g.shape[1])[:, :keep]


def _layer_weights(W):
    D = W['w_out'].shape[1]
    C = W['w_conv_out'].shape[1]
    QL, KVL = W['w_uq'].shape[1], W['w_ukv'].shape[1]
    o1 = 2 * C
    o2, o3 = o1 + QL, o1 + QL + KVL
    o4 = o3 + ROPE_DIM
    win = W['w_in']
    zr = lambda n: jnp.zeros((n, D), win.dtype)
    L = dict(W)
    L['w_c'] = win[:o1]
    L['w_l'] = jnp.concatenate([win[o1:o3], zr(NOPE_DIM), win[o3:o4], zr(HEAD_W - QK_DIM)], axis=0)
    L['w_g'] = win[o4:]
    L['w_q'] = _pad_heads(W['w_uq'], QK_DIM)
    L['w_k'] = _pad_heads(W['w_ukv'], NOPE_DIM + V_DIM, 0, NOPE_DIM)
    L['w_v'] = _pad_heads(W['w_ukv'], NOPE_DIM + V_DIM, NOPE_DIM, V_DIM)
    wmo = W['w_mla_out'].reshape(D, N_HEADS, V_DIM)
    L['w_mo'] = jnp.pad(wmo, ((0, 0), (0, 0), (0, HEAD_W - V_DIM))).reshape(D, N_HEADS * HEAD_W)
    L['dims'] = (D, C, QL, KVL)
    return L


def _row(v, width=None):
    v = v.reshape(1, -1)
    if width is not None and v.shape[1] < width:
        v = jnp.pad(v, ((0, 0), (0, width - v.shape[1])))
    return v


def _ffn_fwd(x, g, wg, wu, wd):
    h = _rms_fwd(x, _row(g))
    a, b, u = _ffn_up(h, wg, wu)
    x_out = _mm("ffn_down", u, wd, 'nn', res=x, scale=0.5)
    return x_out, (x, h, a, b, u)


def _ffn_bwd_weights(dx, saved, wd, after):
    x, h, a, b, u = saved
    d_wd = _wgrad("ffn_dwd", u, dx, scale=0.5)
    da, db = _ffn_bwd_up(dx, wd, a, b, after)
    return da, db, (_wgrad("ffn_dwg", da, h), _wgrad("ffn_dwg", db, h), d_wd)


def _ffn_bwd_input(dx, saved, g, wg, wu, da, db, after):
    return _dh_rms_bwd("ffn_dh_rms", [(da, wg), (db, wu)], saved[0], _row(g), dx, after)


def _layer_fwd(x, L, S, tabs):
    D, C, QL, KVL = L['dims']
    rope_cb = (QL + KVL) // HEAD_W
    x1, ffn1 = _ffn_fwd(x, S['ffn1_norm'], L['ffn1_w_gate'], L['ffn1_w_up'], L['ffn1_w_down'])
    h = _rms_fwd(x1, _row(S['mix_norm']))
    pc = _mm("proj_c", h, L['w_c'], 'nt')
    pl_ = _mm("proj_l", h, L['w_l'], 'nt')
    pg = _mm("proj_g", h, L['w_g'], 'nt')
    conv_w = jnp.pad(S['conv_w'], ((0, HALO - CONV_WIDTH), (0, 0)))
    z, c = _conv_fwd(pc, conv_w, _row(S['conv_b']), _row(S['conv_ln_g']), _row(S['conv_ln_b']))
    yc = _mm("conv_out", c, L['w_conv_out'], 'nt')
    cqn, ckvn = _lat_norm_fwd(pl_, _row(S['cq_norm']), _row(S['ckv_norm']), QL, KVL)
    q_raw = _mm("up_q", cqn, L['w_q'], 'nt')
    k_raw = _mm("up_kv", ckvn, L['w_k'], 'nt')
    v = _mm("up_kv", ckvn, L['w_v'], 'nt', out_dtype=_MXU_DTYPE)
    gq, gk = _row(S['q_norm'], HEAD_W), _row(S['k_norm'], HEAD_W)
    qn, kn = _qk_prep_fwd(q_raw, k_raw, pl_, rope_cb, *tabs, gq, gk)
    o, lse = _attn_fwd(qn, kn, v)
    ym = _mm("mla_out", o, L['w_mo'], 'nt')
    y = _gate_fwd(pg, S['gate_bias'], yc, ym)
    x2 = _mm("mix_out", y, L['w_out'], 'nn', res=x1)
    x3, ffn2 = _ffn_fwd(x2, S['ffn2_norm'], L['ffn2_w_gate'], L['ffn2_w_up'], L['ffn2_w_down'])
    saved = dict(ffn1=ffn1, ffn2=ffn2, x1=x1, h=h, pc=pc, pl=pl_, pg=pg, z=z, c=c, yc=yc, cqn=cqn, ckvn=ckvn,
                 q_raw=q_raw, k_raw=k_raw, v=v, qn=qn, kn=kn, o=o, lse=lse, ym=ym, y=y, conv_w=conv_w, gq=gq, gk=gk)
    return x3, saved


def _layer_bwd_mix(dx, L, S, tabs, A, after):
    D, C, QL, KVL = L['dims']
    rope_cb = (QL + KVL) // HEAD_W
    G, g = {}, {}
    da, db, (G['ffn2_w_gate'], G['ffn2_w_up'], G['ffn2_w_down']) = _ffn_bwd_weights(dx, A['ffn2'], L['ffn2_w_down'], after)
    dx, g['ffn2_norm'] = _ffn_bwd_input(dx, A['ffn2'], S['ffn2_norm'], L['ffn2_w_gate'], L['ffn2_w_up'], da, db, after)
    G['w_out'] = _wgrad("mix_dwout", A['y'], dx)
    dy = _mm("mix_dy", dx, L['w_out'], 'nt')
    dyc, dym, dpg, g['gate_bias'] = _gate_bwd(A['pg'], S['gate_bias'], A['yc'], A['ym'], dy)
    G['w_conv_out'] = _wgrad("conv_dwout", dyc, A['c'])
    dc = _mm("conv_dc", dyc, L['w_conv_out'], 'nn')
    dpc, dcw, g['conv_b'], g['conv_ln_g'], g['conv_ln_b'] = _conv_bwd(
        A['pc'], A['z'], dc, A['conv_w'], _row(S['conv_ln_g']), _row(S['conv_ln_b']))
    g['conv_w'] = dcw[:CONV_WIDTH]
    d_wmo = _wgrad("mla_dwout", dym, A['o'])
    G['w_mla_out'] = d_wmo.reshape(D, N_HEADS, HEAD_W)[:, :, :V_DIM].reshape(D, N_HEADS * V_DIM)
    do = _mm("mla_do", dym, L['w_mo'], 'nn', out_dtype=_MXU_DTYPE)
    dqn, dkn, dv = _attn_bwd(A['qn'], A['kn'], A['v'], A['o'], do, A['lse'])
    dq_raw, dk_raw, drope, dgq, dgk = _qk_prep_bwd(A['q_raw'], A['k_raw'], A['pl'], rope_cb, *tabs, A['gq'], A['gk'], dqn, dkn)
    g['q_norm'], g['k_norm'] = dgq[:, :QK_DIM], dgk[:, :QK_DIM]
    G['w_uq'] = _unpad_heads(_wgrad("up_dwq", dq_raw, A['cqn']), QK_DIM).reshape(N_HEADS * QK_DIM, QL)
    d_wk = _unpad_heads(_wgrad("up_dwkv", dk_raw, A['ckvn']), NOPE_DIM)
    d_wv = _unpad_heads(_wgrad("up_dwkv", dv, A['ckvn']), V_DIM)
    G['w_ukv'] = jnp.concatenate([d_wk, d_wv], axis=1).reshape(N_HEADS * (NOPE_DIM + V_DIM), KVL)
    dcq = _mm("up_dcq", dq_raw, L['w_q'], 'nn')
    dckv = _mm("up_dckv_k", dk_raw, L['w_k'], 'nn')
    dckv = _mm("up_dckv_v", dv, L['w_v'], 'nn', res=dckv)
    dpl, g['cq_norm'], g['ckv_norm'] = _lat_norm_bwd(A['pl'], _row(S['cq_norm']), _row(S['ckv_norm']), dcq, dckv, drope, QL, KVL)
    d_wc = _wgrad("proj_dwc", dpc, A['h'])
    d_wl = _wgrad("proj_dwl", dpl, A['h'])
    d_wg = _wgrad("proj_dwg", dpg, A['h'])
    ql = QL + KVL
    G['w_in'] = jnp.concatenate([d_wc, d_wl[:ql], d_wl[ql + NOPE_DIM:ql + NOPE_DIM + ROPE_DIM], d_wg], axis=0)
    dx, g['mix_norm'] = _dh_rms_bwd("proj_dh_rms", [(dpc, L['w_c']), (dpl, L['w_l']), (dpg, L['w_g'])],
                                    A['x1'], _row(S['mix_norm']), dx, after)
    return dx, G, g


def kernel(x, positions, ffn1_norm, ffn1_w_gate, ffn1_w_up, ffn1_w_down, mix_norm, w_in, gate_bias, conv_w, conv_b, conv_ln_g, conv_ln_b, w_conv_out, cq_norm, ckv_norm, w_uq, w_ukv, q_norm, k_norm, w_mla_out, w_out, ffn2_norm, ffn2_w_gate, ffn2_w_up, ffn2_w_down, loss_target, m_ffn1_norm, m_ffn1_w_gate, m_ffn1_w_up, m_ffn1_w_down, m_mix_norm, m_w_in, m_gate_bias, m_conv_w, m_conv_b, m_conv_ln_g, m_conv_ln_b, m_w_conv_out, m_cq_norm, m_ckv_norm, m_w_uq, m_w_ukv, m_q_norm, m_k_norm, m_w_mla_out, m_w_out, m_ffn2_norm, m_ffn2_w_gate, m_ffn2_w_up, m_ffn2_w_down, v_ffn1_norm, v_ffn1_w_gate, v_ffn1_w_up, v_ffn1_w_down, v_mix_norm, v_w_in, v_gate_bias, v_conv_w, v_conv_b, v_conv_ln_g, v_conv_ln_b, v_w_conv_out, v_cq_norm, v_ckv_norm, v_w_uq, v_w_ukv, v_q_norm, v_k_norm, v_w_mla_out, v_w_out, v_ffn2_norm, v_ffn2_w_gate, v_ffn2_w_up, v_ffn2_w_down):
    w = dict(zip(WEIGHTS, (ffn1_norm, ffn1_w_gate, ffn1_w_up, ffn1_w_down, mix_norm, w_in, gate_bias, conv_w, conv_b, conv_ln_g, conv_ln_b, w_conv_out, cq_norm, ckv_norm, w_uq, w_ukv, q_norm, k_norm, w_mla_out, w_out, ffn2_norm, ffn2_w_gate, ffn2_w_up, ffn2_w_down)))
    m = dict(zip(WEIGHTS, (m_ffn1_norm, m_ffn1_w_gate, m_ffn1_w_up, m_ffn1_w_down, m_mix_norm, m_w_in, m_gate_bias, m_conv_w, m_conv_b, m_conv_ln_g, m_conv_ln_b, m_w_conv_out, m_cq_norm, m_ckv_norm, m_w_uq, m_w_ukv, m_q_norm, m_k_norm, m_w_mla_out, m_w_out, m_ffn2_norm, m_ffn2_w_gate, m_ffn2_w_up, m_ffn2_w_down)))
    v = dict(zip(WEIGHTS, (v_ffn1_norm, v_ffn1_w_gate, v_ffn1_w_up, v_ffn1_w_down, v_mix_norm, v_w_in, v_gate_bias, v_conv_w, v_conv_b, v_conv_ln_g, v_conv_ln_b, v_w_conv_out, v_cq_norm, v_ckv_norm, v_w_uq, v_w_ukv, v_q_norm, v_k_norm, v_w_mla_out, v_w_out, v_ffn2_norm, v_ffn2_w_gate, v_ffn2_w_up, v_ffn2_w_down)))
    depth = ffn1_norm.shape[0]
    T, D = x.shape[1], x.shape[2]
    xs = x.reshape(T, D)
    target = loss_target.reshape(T, D)
    tabs = _rope_tables(positions, T)
    me = 4 * lax.axis_index("x") + 2 * lax.axis_index("y") + lax.axis_index("c")

    big = list(BIG)
    packed = []
    for l in range(depth):
        p, spans = _pack_rows([(w[n][l].T if BIG[n] else w[n][l]).astype(_WIRE_DTYPE) for n in big])
        packed.append(p)
    small_names = [(l, n) for l in range(depth) for n in SMALL_SHARDED]
    small_packed, small_spans = _pack_flat([w[n][l].T for l, n in small_names])
    small_flight = _exchange_start("small_weights_start", small_packed, _own_slot(small_packed, me), False)
    gathered = {0: _all_gather("gather_weights", packed[0])}
    in_flight = {l: _exchange_start(f"gather_start_{l}", packed[l], _own_slot(packed[l], me), False)
                 for l in range(1, depth)}
    send_sems, recv_sems, src, land, _ = small_flight
    small_gathered = _exchange_wait("small_weights_wait", send_sems, recv_sems, src, land, gathered[0], False)
    for st in in_flight.values():
        xs, _ = lax.optimization_barrier((xs, st[4]))

    def layer_weights(l):
        W = {}
        for n, span in zip(big, spans):
            sh = _unpack_rows(gathered[l], span, (N_DEV,))
            W[n] = sh.reshape(N_DEV * sh.shape[1], sh.shape[2])
        S = {n: w[n][l] for n in REPLICATED}
        for (ll, n), span in zip(small_names, small_spans):
            if ll == l:
                sh = _unpack_flat(small_gathered, span, (N_DEV,))
                S[n] = sh.reshape(N_DEV * sh.shape[1], sh.shape[2]).T
        return _layer_weights(W), S

    layers, acts = [], []
    hcur = xs
    for l in range(depth):
        if l > 0:
            send_sems, recv_sems, src, land, _ = in_flight[l]
            gathered[l] = _exchange_wait(f"gather_wait_{l}", send_sems, recv_sems, src, land, hcur, False)
        layers.append(layer_weights(l))
        hcur, saved = _layer_fwd(hcur, *layers[l], tabs)
        acts.append(saved)
    dx, loss_part = _loss_grad(hcur, target)

    stage_names = [[n for n in big if n.startswith('ffn1_')], [n for n in big if not n.startswith('ffn1_')]]
    small_grads, reduced, gspans, in_flight = [{} for _ in range(depth)], {}, {}, {}
    after = jnp.zeros((_SUBLANES, _LANES), F32)
    for l in reversed(range(depth)):
        L, S = layers[l]
        for stage in (1, 0):
            if stage:
                dx, G, g = _layer_bwd_mix(dx, L, S, tabs, acts[l], after)
            else:
                da, db, Gw = _ffn_bwd_weights(dx, acts[l]['ffn1'], L['ffn1_w_down'], after)
                G = dict(zip(('ffn1_w_gate', 'ffn1_w_up', 'ffn1_w_down'), Gw))
            blocks = [G[n].reshape(N_DEV, -1, G[n].shape[1]) for n in stage_names[stage]]
            send, gspans[stage] = _pack_rows(blocks, (N_DEV,))
            own = lax.dynamic_index_in_dim(send, me, 0, keepdims=False)
            in_flight[l, stage] = _exchange_start(f"scatter_start_{l}_{stage}", send, _own_slot(own, me), True)
            after = in_flight[l, stage][4]
            if not stage:
                dx, dg = _ffn_bwd_input(dx, acts[l]['ffn1'], S['ffn1_norm'], L['ffn1_w_gate'], L['ffn1_w_up'], da, db, after)
                g = {'ffn1_norm': dg}
            small_grads[l].update(g)
    small_list = [(l, n) for l in range(depth) for n in REPLICATED + SMALL_SHARDED]
    loss_part = loss_part + after[:1, :]
    small_send, sspans = _pack_flat([small_grads[l][n] for l, n in small_list] + [loss_part])
    small_flight = _exchange_start("small_grads_start", small_send, _own_slot(small_send, me), False)

    out4 = {}
    after = small_flight[4]
    for (l, stage), (send_sems, recv_sems, src, land, _) in in_flight.items():
        got = _exchange_wait(f"scatter_wait_{l}_{stage}", send_sems, recv_sems, src, land, after, True)
        reduced = _sum_blocks("sum_grads_direct", got)
        for n, span in zip(stage_names[stage], gspans[stage]):
            gt = _unpack_rows(reduced, span)
            out4[n] = _adamw_layer(l, w[n], gt.T if BIG[n] else gt, m[n], v[n], out4.get(n), after)
            after = out4[n][1]
    send_sems, recv_sems, src, land, _ = small_flight
    small_all = _sum_blocks("sum_small", _exchange_wait("small_grads_wait", send_sems, recv_sems, src, land, after, False))
    loss = _unpack_flat(small_all, sspans[-1])[0, 0]
    for n in WEIGHTS:
        if n in BIG:
            continue
        per_layer = []
        for l in range(depth):
            gs = _unpack_flat(small_all, sspans[small_list.index((l, n))])
            if n in SMALL_SHARDED:
                cols = w[n].shape[-1]
                gs = lax.dynamic_slice_in_dim(gs, me * cols, cols, axis=1)
            per_layer.append(gs.reshape(w[n].shape[1:]))
        g = jnp.stack(per_layer, axis=0)
        out4[n] = (g,) + _adamw(w[n], g, m[n], v[n])
    return (loss, dx.reshape(1, T, D), *[out4[n][k] for k in range(4) for n in WEIGHTS])
```

```python
import jax
import jax.numpy as jnp
from jax import lax
from jax.experimental import pallas as pl
from jax.experimental.pallas import tpu as pltpu

F32 = jnp.float32
_MXU_DTYPE = jnp.bfloat16
_WIRE_DTYPE = jnp.bfloat16

_LANES = 128
_SUBLANES = 8
_PACKED_ROWS = 16
_V7X_VMEM_BYTES = 64 * 2 ** 20
_VMEM_HEADROOM = 8 * 2 ** 20

N_DEV = 8
EPS = 1e-6
CHUNK = 64
CONV_WIDTH = 31
N_HEADS = 8
NOPE_DIM = 64
ROPE_DIM = 32
QK_DIM = NOPE_DIM + ROPE_DIM
V_DIM = 64
HEAD_W = _LANES
ROPE_THETA = 10000.0
HALO = 32

ADAM_LR = 0.001
ADAM_B1 = 0.9
ADAM_B2 = 0.999
ADAM_EPS = 1e-08
ADAM_WD = 0.01
ADAM_STEP = 10

_ROW_TILE = 512
_MM_TILE = 512
_FFN_TILE = 512
_MM_TILE_N = 1408
_MM_TILE_K = 1024
_TN_TILE_M = 1408
_ATT_TILE = 512
_ATT_ROWS = 32
_ATT_HEADS = 2
_PACK_W = 1024
_PACK_ROW_MULT = 128

WEIGHTS = ['ffn1_norm', 'ffn1_w_gate', 'ffn1_w_up', 'ffn1_w_down', 'mix_norm', 'w_in', 'gate_bias',
           'conv_w', 'conv_b', 'conv_ln_g', 'conv_ln_b', 'w_conv_out', 'cq_norm', 'ckv_norm', 'w_uq',
           'w_ukv', 'q_norm', 'k_norm', 'w_mla_out', 'w_out', 'ffn2_norm', 'ffn2_w_gate', 'ffn2_w_up',
           'ffn2_w_down']
BIG = {'ffn1_w_gate': True, 'ffn1_w_up': True, 'ffn1_w_down': False, 'w_in': True, 'w_conv_out': True,
       'w_uq': True, 'w_ukv': True, 'w_mla_out': True, 'w_out': False, 'ffn2_w_gate': True,
       'ffn2_w_up': True, 'ffn2_w_down': False}
SMALL_SHARDED = ['conv_w', 'gate_bias']
REPLICATED = ['ffn1_norm', 'mix_norm', 'conv_b', 'conv_ln_g', 'conv_ln_b', 'cq_norm', 'ckv_norm',
              'q_norm', 'k_norm', 'ffn2_norm']


def _tile(n, target, mult=_LANES):
    if n <= target:
        return n
    for d in range(target - target % mult, 0, -mult):
        if n % d == 0:
            return d
    return n


def _nbytes(shape, dtype):
    n = 1
    for s in shape:
        n *= s
    return n * jnp.dtype(dtype).itemsize


def _params(dims, block_bytes, extra_bytes=0):
    need = 2 * block_bytes + extra_bytes + _VMEM_HEADROOM
    assert need <= _V7X_VMEM_BYTES, (need, dims)
    return pltpu.CompilerParams(dimension_semantics=dims, vmem_limit_bytes=_V7X_VMEM_BYTES - _VMEM_HEADROOM)


def _mxu(x, scale=None):
    if scale is not None:
        x = x * scale
    return x if x.dtype == _MXU_DTYPE else x.astype(_MXU_DTYPE)


def _dot(a, b, mode):
    dims = {'nn': (((1,), (0,)), ((), ())), 'nt': (((1,), (1,)), ((), ())), 'tn': (((0,), (0,)), ((), ()))}[mode]
    return lax.dot_general(a, b, dims, preferred_element_type=F32)


def _colsum(x):
    return jnp.sum(x, axis=0, keepdims=True)


def _sig(x):
    return 0.5 * jnp.tanh(0.5 * x) + 0.5


def _mm(name, a, b, mode, *, out_dtype=F32, res=None, scale=None, a_scale=None, tm=None, tn=None, tk=None):
    if mode == 'nn':
        (M, K), (_, N) = a.shape, b.shape
    elif mode == 'nt':
        (M, K), (N, _) = a.shape, b.shape
    else:
        (K, M), (_, N) = a.shape, b.shape
    tm = _tile(M, tm or (_TN_TILE_M if mode == 'tn' else _MM_TILE))
    tn = _tile(N, tn or _MM_TILE_N)
    tk = _tile(K, tk or (_MM_TILE_K if mode == 'tn' else 4096))
    nk = K // tk
    grid = (M // tm, N // tn, nk)
    a_spec = pl.BlockSpec((tk, tm), lambda i, j, k: (k, i)) if mode == 'tn' else pl.BlockSpec((tm, tk), lambda i, j, k: (i, k))
    b_spec = pl.BlockSpec((tn, tk), lambda i, j, k: (j, k)) if mode == 'nt' else pl.BlockSpec((tk, tn), lambda i, j, k: (k, j))
    o_spec = pl.BlockSpec((tm, tn), lambda i, j, k: (i, j))
    has_res = res is not None

    def body(*refs):
        a_ref, b_ref = refs[0], refs[1]
        res_ref = refs[2] if has_res else None
        o_ref = refs[3] if has_res else refs[2]
        acc_ref = refs[-1] if nk > 1 else None

        def finish(p):
            if scale is not None:
                p = p * scale
            if has_res:
                p = res_ref[...] + p
            o_ref[...] = p.astype(o_ref.dtype)

        p = _dot(_mxu(a_ref[...], a_scale), _mxu(b_ref[...]), mode)
        if nk == 1:
            finish(p)
        else:
            k = pl.program_id(2)

            @pl.when(k == 0)
            def _():
                acc_ref[...] = p

            @pl.when(k > 0)
            def _():
                acc_ref[...] += p

            @pl.when(k == nk - 1)
            def _():
                finish(acc_ref[...])

    blocks = (_nbytes((tm, tk), a.dtype) + _nbytes((tk, tn), b.dtype) + _nbytes((tm, tn), out_dtype)
              + (_nbytes((tm, tn), F32) if has_res else 0))
    extra = ((2 + (nk > 1)) * _nbytes((tm, tn), F32) + (a.dtype != _MXU_DTYPE) * _nbytes((tm, tk), _MXU_DTYPE)
             + (b.dtype != _MXU_DTYPE) * _nbytes((tk, tn), _MXU_DTYPE))
    return pl.pallas_call(
        body, name=name, grid=grid,
        in_specs=[a_spec, b_spec] + ([o_spec] if has_res else []),
        out_specs=o_spec,
        out_shape=jax.ShapeDtypeStruct((M, N), out_dtype),
        scratch_shapes=[pltpu.VMEM((tm, tn), F32)] if nk > 1 else [],
        compiler_params=_params(("parallel", "parallel", "arbitrary"), blocks, extra),
    )(*([a, b] + ([res] if has_res else [])))


def _wgrad(name, a, b, scale=None):
    return _mm(name, a, b, 'tn', out_dtype=_WIRE_DTYPE, scale=scale)


def _col_chunks(n, width=2 * _LANES):
    return [slice(c, min(c + width, n)) for c in range(0, n, width)]


def _ffn_up(h, wg, wu):
    T, D = h.shape
    F = wg.shape[0]
    tm, tn = _tile(T, _FFN_TILE), _tile(F, _MM_TILE_N)

    def body(h_ref, wg_ref, wu_ref, a_ref, b_ref, u_ref):
        hv = h_ref[...]
        for cols in _col_chunks(tn):
            a = _dot(hv, wg_ref[cols, :], 'nt')
            b = _dot(hv, wu_ref[cols, :], 'nt')
            a_ref[:, cols] = a
            b_ref[:, cols] = b
            u_ref[:, cols] = (a * _sig(a) * b).astype(u_ref.dtype)

    w_spec = pl.BlockSpec((tn, D), lambda j, i: (j, 0))
    o_spec = pl.BlockSpec((tm, tn), lambda j, i: (i, j))
    blocks = _nbytes((tm, D), h.dtype) + 2 * _nbytes((tn, D), wg.dtype) + 2 * _nbytes((tm, tn), F32) + _nbytes((tm, tn), _MXU_DTYPE)
    return pl.pallas_call(
        body, name="ffn_up", grid=(F // tn, T // tm),
        in_specs=[pl.BlockSpec((tm, D), lambda j, i: (i, 0)), w_spec, w_spec],
        out_specs=[o_spec, o_spec, o_spec],
        out_shape=[jax.ShapeDtypeStruct((T, F), F32), jax.ShapeDtypeStruct((T, F), F32),
                   jax.ShapeDtypeStruct((T, F), _MXU_DTYPE)],
        compiler_params=_params(("parallel", "parallel"), blocks, 4 * _nbytes((tm, tn), F32)),
    )(h, wg, wu)


def _ffn_bwd_up(dx, wd, a, b, after):
    T, D = dx.shape
    F = wd.shape[0]
    tm, tn = _tile(T, _FFN_TILE), _tile(F, _MM_TILE_N)

    def body(dx_ref, wd_ref, a_ref, b_ref, after_ref, da_ref, db_ref):
        dy = _mxu(dx_ref[...], 0.5)
        for cols in _col_chunks(tn):
            du = _dot(dy, wd_ref[cols, :], 'nt')
            av, bv = a_ref[:, cols], b_ref[:, cols]
            s = _sig(av)
            db_ref[:, cols] = (du * av * s).astype(db_ref.dtype)
            da_ref[:, cols] = (du * bv * (s * (1.0 + av * (1.0 - s)))).astype(da_ref.dtype)

    t_spec = pl.BlockSpec((tm, tn), lambda j, i: (i, j))
    blocks = _nbytes((tm, D), F32) + _nbytes((tn, D), wd.dtype) + 2 * _nbytes((tm, tn), F32) + 2 * _nbytes((tm, tn), _MXU_DTYPE)
    return pl.pallas_call(
        body, name="ffn_bwd_up", grid=(F // tn, T // tm),
        in_specs=[pl.BlockSpec((tm, D), lambda j, i: (i, 0)), pl.BlockSpec((tn, D), lambda j, i: (j, 0)), t_spec, t_spec,
                  pl.BlockSpec(memory_space=pl.ANY)],
        out_specs=[t_spec, t_spec],
        out_shape=[jax.ShapeDtypeStruct((T, F), _MXU_DTYPE)] * 2,
        compiler_params=_params(("parallel", "parallel"), blocks, 5 * _nbytes((tm, tn), F32)),
    )(dx, wd, a, b, after)


def _row_spec(tm, w, cb=0):
    return pl.BlockSpec((tm, w), lambda i: (i, cb))


def _full_spec(shape):
    return pl.BlockSpec(shape, lambda i: (0,) * len(shape))


def _init_acc(refs):
    @pl.when(pl.program_id(0) == 0)
    def _():
        for r in refs:
            r[...] = jnp.zeros_like(r)


def _rms_fwd(x, g):
    T, D = x.shape
    tm = _tile(T, _ROW_TILE, _PACKED_ROWS)

    def body(x_ref, g_ref, h_ref):
        xv = x_ref[...]
        r = lax.rsqrt(jnp.mean(xv * xv, axis=-1, keepdims=True) + EPS)
        h_ref[...] = (xv * r * g_ref[...]).astype(h_ref.dtype)

    return pl.pallas_call(
        body, name="rms_fwd", grid=(T // tm,),
        in_specs=[_row_spec(tm, D), _full_spec((1, D))], out_specs=_row_spec(tm, D),
        out_shape=jax.ShapeDtypeStruct((T, D), _MXU_DTYPE),
        compiler_params=_params(("parallel",), 2 * _nbytes((tm, D), F32), 4 * _nbytes((tm, D), F32)),
    )(x, g)


def _dh_rms_bwd(name, pairs, x, g, dres, after):
    T, D = x.shape
    tm = _tile(T, _ROW_TILE // 2, _PACKED_ROWS)
    n = len(pairs)

    def body(*refs):
        x_ref, g_ref, dres_ref, after_ref, dx_ref, dg_ref = refs[2 * n:]
        _init_acc([dg_ref])
        dh = None
        for k in range(n):
            p = _dot(_mxu(refs[2 * k][...]), refs[2 * k + 1][...], 'nn')
            dh = p if dh is None else dh + p
        xv = x_ref[...]
        r = lax.rsqrt(jnp.mean(xv * xv, axis=-1, keepdims=True) + EPS)
        xh = xv * r
        dxh = dh * g_ref[...]
        dx_ref[...] = dres_ref[...] + r * (dxh - xh * jnp.mean(dxh * xh, axis=-1, keepdims=True))
        dg_ref[...] += _colsum(dh * xh)

    in_specs, args, blocks = [], [], 3 * _nbytes((tm, D), F32)
    for a, w in pairs:
        in_specs += [_row_spec(tm, a.shape[1]), _full_spec(w.shape)]
        args += [a, w]
        blocks += _nbytes((tm, a.shape[1]), a.dtype) + _nbytes(w.shape, w.dtype)
    return pl.pallas_call(
        body, name=name, grid=(T // tm,),
        in_specs=in_specs + [_row_spec(tm, D), _full_spec((1, D)), _row_spec(tm, D), pl.BlockSpec(memory_space=pl.ANY)],
        out_specs=[_row_spec(tm, D), _full_spec((1, D))],
        out_shape=[jax.ShapeDtypeStruct((T, D), F32), jax.ShapeDtypeStruct((1, D), F32)],
        compiler_params=_params(("arbitrary",), blocks, 6 * _nbytes((tm, D), F32)),
    )(*args, x, g, dres, after)


def _loss_grad(y, target):
    T, D = y.shape
    tm = _tile(T, _ROW_TILE, _SUBLANES)

    def body(y_ref, t_ref, dy_ref, l_ref):
        _init_acc([l_ref])
        err = y_ref[...] - t_ref[...]
        dy_ref[...] = err * (1.0 / D)
        per_row = jnp.sum(err * err, axis=-1, keepdims=True) * (0.5 / D)
        l_ref[...] += jnp.sum(per_row, axis=0, keepdims=True)

    return pl.pallas_call(
        body, name="loss_grad", grid=(T // tm,),
        in_specs=[_row_spec(tm, D), _row_spec(tm, D)],
        out_specs=[_row_spec(tm, D), _full_spec((1, _LANES))],
        out_shape=[jax.ShapeDtypeStruct((T, D), F32), jax.ShapeDtypeStruct((1, _LANES), F32)],
        compiler_params=_params(("arbitrary",), 3 * _nbytes((tm, D), F32), 3 * _nbytes((tm, D), F32)),
    )(y, target)


def _glu(pc, C):
    a, gate = pc[:, :C], pc[:, C:]
    s = _sig(gate)
    return a, s, a * s


def _tap_groups(offsets):
    groups = {}
    for k, off in enumerate(offsets):
        groups.setdefault(off % _SUBLANES, []).append((k, off - off % _SUBLANES))
    return sorted(groups.items())


def _conv_fwd(pc, w, cb, lg, lb):
    T, C2 = pc.shape
    C = C2 // 2
    tm = _tile(T, _ROW_TILE // 2, HALO)
    per = tm // HALO

    def body(pc_ref, ph_ref, w_ref, cb_ref, lg_ref, lb_ref, z_ref, c_ref, ubuf, vbuf):
        i = pl.program_id(0)
        _, _, u_cur = _glu(pc_ref[...], C)
        _, _, u_prev = _glu(ph_ref[...], C)
        ubuf[0:HALO, :] = jnp.where(i > 0, u_prev, 0.0)
        ubuf[HALO:HALO + tm, :] = u_cur
        ubuf[HALO + tm:, :] = jnp.zeros((_SUBLANES, C), F32)
        acc = jnp.zeros((tm, C), F32)
        for shift, taps in _tap_groups([HALO - (CONV_WIDTH - 1) + k for k in range(CONV_WIDTH)]):
            v = None
            for k, base in taps:
                term = w_ref[k:k + 1, :] * ubuf[base:base + tm + _SUBLANES, :]
                v = term if v is None else v + term
            if shift == 0:
                acc = acc + v[:tm]
            else:
                vbuf[...] = v
                acc = acc + vbuf[shift:shift + tm, :]
        z = acc + cb_ref[...]
        z_ref[...] = z
        zc = z - jnp.mean(z, axis=-1, keepdims=True)
        y = zc * lax.rsqrt(jnp.mean(zc * zc, axis=-1, keepdims=True) + EPS) * lg_ref[...] + lb_ref[...]
        c_ref[...] = (y * _sig(y)).astype(c_ref.dtype)

    return pl.pallas_call(
        body, name="conv_fwd", grid=(T // tm,),
        in_specs=[_row_spec(tm, C2), pl.BlockSpec((HALO, C2), lambda i: (jnp.maximum(i * per - 1, 0), 0)),
                  _full_spec((HALO, C)), _full_spec((1, C)), _full_spec((1, C)), _full_spec((1, C))],
        out_specs=[_row_spec(tm, C), _row_spec(tm, C)],
        out_shape=[jax.ShapeDtypeStruct((T, C), F32), jax.ShapeDtypeStruct((T, C), _MXU_DTYPE)],
        scratch_shapes=[pltpu.VMEM((tm + HALO + _SUBLANES, C), F32), pltpu.VMEM((tm + _SUBLANES, C), F32)],
        compiler_params=_params(("parallel",), 3 * _nbytes((tm, C2), F32), 10 * _nbytes((tm, C), F32)),
    )(pc, pc, w, cb, lg, lb)


def _conv_bwd(pc, z, dc, w, lg, lb):
    T, C2 = pc.shape
    C = C2 // 2
    tm = _tile(T, _ROW_TILE // 2, HALO)
    per = tm // HALO
    n = T // tm
    last_halo = T // HALO - 1

    def body(pc_ref, ph_ref, z_ref, zn_ref, dc_ref, dcn_ref, w_ref, lg_ref, lb_ref,
             dpc_ref, dw_ref, dcb_ref, dlg_ref, dlb_ref, ubuf, dzbuf, vbuf, sbuf):
        i = pl.program_id(0)
        _init_acc([dw_ref, dcb_ref, dlg_ref, dlb_ref])
        g, b = lg_ref[...], lb_ref[...]

        def ln_swish_bwd(zv, dcv):
            zc = zv - jnp.mean(zv, axis=-1, keepdims=True)
            r = lax.rsqrt(jnp.mean(zc * zc, axis=-1, keepdims=True) + EPS)
            zh = zc * r
            y = zh * g + b
            s = _sig(y)
            dy = dcv * (s * (1.0 + y * (1.0 - s)))
            dyg = dy * g
            dz = r * (dyg - jnp.mean(dyg, axis=-1, keepdims=True) - zh * jnp.mean(dyg * zh, axis=-1, keepdims=True))
            return dz, dy, zh

        dz_c, dy_c, zh_c = ln_swish_bwd(z_ref[...], dc_ref[...])
        dz_n, _, _ = ln_swish_bwd(zn_ref[...], dcn_ref[...])
        dlg_ref[...] += _colsum(dy_c * zh_c)
        dlb_ref[...] += _colsum(dy_c)
        dcb_ref[...] += _colsum(dz_c)
        dzbuf[0:tm, :] = dz_c
        dzbuf[tm:tm + HALO, :] = jnp.where(i < n - 1, dz_n, 0.0)
        a, s, u_cur = _glu(pc_ref[...], C)
        _, _, u_prev = _glu(ph_ref[...], C)
        ubuf[0:HALO, :] = jnp.where(i > 0, u_prev, 0.0)
        ubuf[HALO:HALO + tm, :] = u_cur
        ubuf[HALO + tm:, :] = jnp.zeros((_SUBLANES, C), F32)
        du = jnp.zeros((tm, C), F32)
        for shift, taps in _tap_groups([CONV_WIDTH - 1 - k for k in range(CONV_WIDTH)]):
            v = None
            for k, base in taps:
                term = w_ref[k:k + 1, :] * dzbuf[base:base + tm + _SUBLANES, :]
                v = term if v is None else v + term
            if shift == 0:
                du = du + v[:tm]
            else:
                vbuf[...] = v
                du = du + vbuf[shift:shift + tm, :]
        for shift, taps in _tap_groups([HALO - (CONV_WIDTH - 1) + k for k in range(CONV_WIDTH)]):
            if shift:
                sbuf[...] = ubuf[shift:shift + tm + HALO, :]
            src = sbuf if shift else ubuf
            for k, base in taps:
                dw_ref[k:k + 1, :] += _colsum(dz_c * src[base:base + tm, :])
        dpc_ref[:, :C] = (du * s).astype(dpc_ref.dtype)
        dpc_ref[:, C:] = (du * a * s * (1.0 - s)).astype(dpc_ref.dtype)

    nxt = lambda i: (jnp.minimum((i + 1) * per, last_halo), 0)
    vec = _full_spec((1, C))
    return pl.pallas_call(
        body, name="conv_bwd", grid=(n,),
        in_specs=[_row_spec(tm, C2), pl.BlockSpec((HALO, C2), lambda i: (jnp.maximum(i * per - 1, 0), 0)),
                  _row_spec(tm, C), pl.BlockSpec((HALO, C), nxt), _row_spec(tm, C), pl.BlockSpec((HALO, C), nxt),
                  _full_spec((HALO, C)), vec, vec],
        out_specs=[_row_spec(tm, C2), _full_spec((HALO, C)), vec, vec, vec],
        out_shape=[jax.ShapeDtypeStruct((T, C2), _MXU_DTYPE), jax.ShapeDtypeStruct((HALO, C), F32)]
                  + [jax.ShapeDtypeStruct((1, C), F32)] * 3,
        scratch_shapes=[pltpu.VMEM((tm + HALO + _SUBLANES, C), F32), pltpu.VMEM((tm + HALO + _SUBLANES, C), F32),
                        pltpu.VMEM((tm + _SUBLANES, C), F32), pltpu.VMEM((tm + HALO, C), F32)],
        compiler_params=_params(("arbitrary",), 5 * _nbytes((tm, C2), F32), 16 * _nbytes((tm, C), F32)),
    )(pc, pc, z, z, dc, dc, w, lg, lb)


def _lat_norm_fwd(pl_, gq, gkv, QL, KVL):
    T, W = pl_.shape
    tm = _tile(T, _ROW_TILE, _PACKED_ROWS)

    def body(p_ref, gq_ref, gkv_ref, cq_ref, ckv_ref):
        def norm(xv, gv):
            return xv * lax.rsqrt(jnp.mean(xv * xv, axis=-1, keepdims=True) + EPS) * gv
        cq_ref[...] = norm(p_ref[:, :QL], gq_ref[...]).astype(cq_ref.dtype)
        ckv_ref[...] = norm(p_ref[:, QL:QL + KVL], gkv_ref[...]).astype(ckv_ref.dtype)

    return pl.pallas_call(
        body, name="lat_norm_fwd", grid=(T // tm,),
        in_specs=[_row_spec(tm, W), _full_spec((1, QL)), _full_spec((1, KVL))],
        out_specs=[_row_spec(tm, QL), _row_spec(tm, KVL)],
        out_shape=[jax.ShapeDtypeStruct((T, QL), _MXU_DTYPE), jax.ShapeDtypeStruct((T, KVL), _MXU_DTYPE)],
        compiler_params=_params(("parallel",), 2 * _nbytes((tm, W), F32), 4 * _nbytes((tm, W), F32)),
    )(pl_, gq, gkv)


def _lat_norm_bwd(pl_, gq, gkv, dcq, dckv, drope, QL, KVL):
    T, W = pl_.shape
    tm = _tile(T, _ROW_TILE, _PACKED_ROWS)

    def body(p_ref, gq_ref, gkv_ref, dcq_ref, dckv_ref, dr_ref, dp_ref, dgq_ref, dgkv_ref):
        _init_acc([dgq_ref, dgkv_ref])

        def bwd(xv, gv, dv, dg_ref):
            r = lax.rsqrt(jnp.mean(xv * xv, axis=-1, keepdims=True) + EPS)
            xh = xv * r
            dxh = dv * gv
            dg_ref[...] += _colsum(dv * xh)
            return r * (dxh - xh * jnp.mean(dxh * xh, axis=-1, keepdims=True))

        dp_ref[:, :QL] = bwd(p_ref[:, :QL], gq_ref[...], dcq_ref[...], dgq_ref).astype(dp_ref.dtype)
        dp_ref[:, QL:QL + KVL] = bwd(p_ref[:, QL:QL + KVL], gkv_ref[...], dckv_ref[...], dgkv_ref).astype(dp_ref.dtype)
        dp_ref[:, QL + KVL:] = dr_ref[...].astype(dp_ref.dtype)

    return pl.pallas_call(
        body, name="lat_norm_bwd", grid=(T // tm,),
        in_specs=[_row_spec(tm, W), _full_spec((1, QL)), _full_spec((1, KVL)), _row_spec(tm, QL), _row_spec(tm, KVL),
                  _row_spec(tm, HEAD_W)],
        out_specs=[_row_spec(tm, W), _full_spec((1, QL)), _full_spec((1, KVL))],
        out_shape=[jax.ShapeDtypeStruct((T, W), _MXU_DTYPE), jax.ShapeDtypeStruct((1, QL), F32),
                   jax.ShapeDtypeStruct((1, KVL), F32)],
        compiler_params=_params(("arbitrary",), 4 * _nbytes((tm, W), F32), 6 * _nbytes((tm, W), F32)),
    )(pl_, gq, gkv, dcq, dckv, drope)


def _norm_rope(xv, gv, ct, sa, sb):
    r = lax.rsqrt(jnp.sum(xv * xv, axis=-1, keepdims=True) * (1.0 / QK_DIM) + EPS)
    xn = xv * r * gv
    half = ROPE_DIM // 2
    return xn * ct + pltpu.roll(xn, HEAD_W - half, 1) * sa + pltpu.roll(xn, half, 1) * sb


def _norm_rope_bwd(xv, gv, ct, sa, sb, dout):
    half = ROPE_DIM // 2
    dxn = dout * ct + pltpu.roll(dout * sa, half, 1) + pltpu.roll(dout * sb, HEAD_W - half, 1)
    r = lax.rsqrt(jnp.sum(xv * xv, axis=-1, keepdims=True) * (1.0 / QK_DIM) + EPS)
    xh = xv * r
    dxh = dxn * gv
    dx = r * (dxh - xh * (jnp.sum(dxh * xh, axis=-1, keepdims=True) * (1.0 / QK_DIM)))
    return dx, _colsum(dxn * xh)


def _qk_prep_fwd(q_raw, k_raw, pl_, rope_cb, ct, sa, sb, gq, gk):
    T, HW = q_raw.shape
    H = HW // HEAD_W
    tm = _tile(T, _ROW_TILE, _PACKED_ROWS)

    def body(q_ref, k_ref, r_ref, ct_ref, sa_ref, sb_ref, gq_ref, gk_ref, qn_ref, kn_ref):
        ctv, sav, sbv, rv = ct_ref[...], sa_ref[...], sb_ref[...], r_ref[...]
        for h in range(H):
            hs = slice(h * HEAD_W, (h + 1) * HEAD_W)
            qn_ref[:, hs] = _norm_rope(q_ref[:, hs], gq_ref[...], ctv, sav, sbv).astype(qn_ref.dtype)
            kn_ref[:, hs] = _norm_rope(k_ref[:, hs] + rv, gk_ref[...], ctv, sav, sbv).astype(kn_ref.dtype)

    tb = _row_spec(tm, HEAD_W)
    gb = _full_spec((1, HEAD_W))
    return pl.pallas_call(
        body, name="qk_prep_fwd", grid=(T // tm,),
        in_specs=[_row_spec(tm, HW), _row_spec(tm, HW), _row_spec(tm, HEAD_W, rope_cb), tb, tb, tb, gb, gb],
        out_specs=[_row_spec(tm, HW), _row_spec(tm, HW)],
        out_shape=[jax.ShapeDtypeStruct((T, HW), _MXU_DTYPE)] * 2,
        compiler_params=_params(("parallel",), 3 * _nbytes((tm, HW), F32) + 4 * _nbytes((tm, HEAD_W), F32), 8 * _nbytes((tm, HEAD_W), F32)),
    )(q_raw, k_raw, pl_, ct, sa, sb, gq, gk)


def _qk_prep_bwd(q_raw, k_raw, pl_, rope_cb, ct, sa, sb, gq, gk, dqn, dkn):
    T, HW = q_raw.shape
    H = HW // HEAD_W
    tm = _tile(T, _ROW_TILE, _PACKED_ROWS)

    def body(q_ref, k_ref, r_ref, ct_ref, sa_ref, sb_ref, gq_ref, gk_ref, dqn_ref, dkn_ref,
             dq_ref, dk_ref, dr_ref, dgq_ref, dgk_ref):
        _init_acc([dgq_ref, dgk_ref])
        ctv, sav, sbv, rv = ct_ref[...], sa_ref[...], sb_ref[...], r_ref[...]
        dr = dgq_sum = dgk_sum = None
        for h in range(H):
            hs = slice(h * HEAD_W, (h + 1) * HEAD_W)
            dq, dgq = _norm_rope_bwd(q_ref[:, hs], gq_ref[...], ctv, sav, sbv, dqn_ref[:, hs])
            dk, dgk = _norm_rope_bwd(k_ref[:, hs] + rv, gk_ref[...], ctv, sav, sbv, dkn_ref[:, hs])
            dq_ref[:, hs] = dq.astype(dq_ref.dtype)
            dk_ref[:, hs] = dk.astype(dk_ref.dtype)
            dr = dk if h == 0 else dr + dk
            dgq_sum = dgq if h == 0 else dgq_sum + dgq
            dgk_sum = dgk if h == 0 else dgk_sum + dgk
        dr_ref[...] = dr
        dgq_ref[...] += dgq_sum
        dgk_ref[...] += dgk_sum

    tb = _row_spec(tm, HEAD_W)
    gb = _full_spec((1, HEAD_W))
    hb = _row_spec(tm, HW)
    return pl.pallas_call(
        body, name="qk_prep_bwd", grid=(T // tm,),
        in_specs=[hb, hb, _row_spec(tm, HEAD_W, rope_cb), tb, tb, tb, gb, gb, hb, hb],
        out_specs=[hb, hb, tb, gb, gb],
        out_shape=[jax.ShapeDtypeStruct((T, HW), _MXU_DTYPE)] * 2 + [jax.ShapeDtypeStruct((T, HEAD_W), F32)]
                  + [jax.ShapeDtypeStruct((1, HEAD_W), F32)] * 2,
        compiler_params=_params(("arbitrary",), 5 * _nbytes((tm, HW), F32) + 5 * _nbytes((tm, HEAD_W), F32), 12 * _nbytes((tm, HEAD_W), F32)),
    )(q_raw, k_raw, pl_, ct, sa, sb, gq, gk, dqn, dkn)


_NEG = -1e30
_LOG2E = 1.4426950408889634
_SCORE_C = QK_DIM ** -0.5 * _LOG2E


def _pieces(t, rb, diag):
    assert CHUNK % rb == 0 and rb % _SUBLANES == 0
    out = []
    for r in range(t // rb):
        lo = (r * rb // CHUNK) * CHUNK if diag else 0
        for c in range(t // _LANES):
            cut = None if (c + 1) * _LANES <= lo else max(lo - c * _LANES, 0)
            out.append((slice(r * rb, (r + 1) * rb), slice(c * _LANES, (c + 1) * _LANES), c, cut))
    return out


def _groups(x):
    return x.reshape(x.shape[0] // _SUBLANES, _SUBLANES, x.shape[1])


def _all_sublanes(x8, op):
    return jnp.broadcast_to(op(x8, axis=0, keepdims=True), x8.shape)


def _lane_ge(rb, cut):
    return lax.broadcasted_iota(jnp.int32, (rb, _LANES), 1) >= cut


def _attn_fwd(q, k, v):
    T, HW = q.shape
    H = HW // HEAD_W
    t = _tile(T, _ATT_TILE)
    rb = min(_ATT_ROWS, t)
    nc = t // _LANES

    hp = _ATT_HEADS if H % _ATT_HEADS == 0 else 1
    W = hp * HEAD_W

    def body(q_ref, k_ref, v_ref, o_ref, lse_ref, s_s, p_s, m_s, l_s, acc_s):
        i = pl.program_id(1)
        m_s[...] = jnp.full_like(m_s, _NEG)
        l_s[...] = jnp.zeros_like(l_s)
        acc_s[...] = jnp.zeros_like(acc_s)

        def head_step(hh, rows, diag):
            hs = slice(hh * HEAD_W, (hh + 1) * HEAD_W)
            s_s[hh] = _dot(k_ref[rows, hs], q_ref[:, hs], 'nt')
            pieces = _pieces(t, rb, diag)

            def scores(rs, cs, cut):
                sb = s_s[hh, rs, cs]
                return jnp.where(_lane_ge(rb, cut), sb, _NEG) if cut else sb

            mx = [None] * nc
            for rs, cs, c, cut in pieces:
                if cut is not None:
                    g = jnp.max(_groups(scores(rs, cs, cut)), axis=0)
                    mx[c] = g if mx[c] is None else jnp.maximum(mx[c], g)
            m_new, alpha = [], []
            for c in range(nc):
                cs = slice(c * _LANES, (c + 1) * _LANES)
                m_prev = m_s[hh, :, cs]
                m_new.append(jnp.maximum(m_prev, _all_sublanes(mx[c], jnp.max)))
                alpha.append(jnp.exp2((m_prev - m_new[c]) * _SCORE_C))
                m_s[hh, :, cs] = m_new[c]
            lsum = [jnp.zeros((_SUBLANES, _LANES), F32)] * nc
            for rs, cs, c, cut in pieces:
                if cut is None:
                    p_s[hh, rs, cs] = jnp.zeros((rb, _LANES), p_s.dtype)
                    continue
                p = jnp.exp2((scores(rs, cs, cut) - jnp.tile(m_new[c], (rb // _SUBLANES, 1))) * _SCORE_C)
                lsum[c] = lsum[c] + jnp.sum(_groups(p), axis=0)
                p_s[hh, rs, cs] = p.astype(p_s.dtype)
            for c in range(nc):
                cs = slice(c * _LANES, (c + 1) * _LANES)
                l_s[hh, :, cs] = alpha[c] * l_s[hh, :, cs] + _all_sublanes(lsum[c], jnp.sum)
            a = jnp.tile(jnp.concatenate(alpha, axis=1), (HEAD_W // _SUBLANES, 1))
            acc_s[hh] = a * acc_s[hh] + _dot(v_ref[rows, hs], p_s[hh], 'tn')

        def step(j, diag):
            rows = pl.ds(pl.multiple_of(j * t, t), t)
            for hh in range(hp):
                head_step(hh, rows, diag)

        def loop_body(j, carry):
            step(j, False)
            return carry

        lax.fori_loop(0, i, loop_body, 0)
        step(i, True)
        for hh in range(hp):
            l = l_s[hh]
            o_t = acc_s[hh] / jnp.tile(l, (HEAD_W // _SUBLANES, 1))
            o_ref[:, hh * HEAD_W:(hh + 1) * HEAD_W] = o_t.T.astype(o_ref.dtype)
            lse_ref[hh * _SUBLANES:(hh + 1) * _SUBLANES, :] = m_s[hh] * _SCORE_C + jnp.log(l) * _LOG2E

    qb = pl.BlockSpec((t, W), lambda h, i: (i, h))
    kb = pl.BlockSpec((T, W), lambda h, i: (0, h))
    st = pltpu.VMEM((hp, _SUBLANES, t), F32)
    return pl.pallas_call(
        body, name="attn_fwd", grid=(H // hp, T // t),
        in_specs=[qb, kb, kb], out_specs=[qb, pl.BlockSpec((hp * _SUBLANES, t), lambda h, i: (h, i))],
        out_shape=[jax.ShapeDtypeStruct((T, HW), _MXU_DTYPE), jax.ShapeDtypeStruct((H * _SUBLANES, T), F32)],
        scratch_shapes=[pltpu.VMEM((hp, t, t), F32), pltpu.VMEM((hp, t, t), _MXU_DTYPE), st, st, pltpu.VMEM((hp, HEAD_W, t), F32)],
        compiler_params=_params(("parallel", "arbitrary"), 2 * _nbytes((T, W), _MXU_DTYPE) + 3 * _nbytes((t, W), F32),
                                4 * hp * _nbytes((t, t), F32)),
    )(q, k, v)


def _attn_bwd(q, k, v, o, do, lse2):
    T, HW = q.shape
    H = HW // HEAD_W
    t = _tile(T, _ATT_TILE)
    n = T // t
    rb = min(_ATT_ROWS, t)
    scale = QK_DIM ** -0.5

    hp = _ATT_HEADS if H % _ATT_HEADS == 0 else 1
    W = hp * HEAD_W

    def body(k_ref, v_ref, q_ref, o_ref, do_ref, lse_ref, dq_ref, dk_ref, dv_ref,
             s_s, dp_s, p_s, ds_s, dk_s, dv_s, delta_s, dqt_s):
        j = pl.program_id(1)

        @pl.when(j == 0)
        def _():
            dqt_s[...] = jnp.zeros_like(dqt_s)
            for hh in range(hp):
                hs = slice(hh * HEAD_W, (hh + 1) * HEAD_W)
                for b in range(n):
                    rs = slice(b * t, (b + 1) * t)
                    d = jnp.sum(do_ref[rs, hs].astype(F32) * o_ref[rs, hs].astype(F32), axis=1, keepdims=True)
                    delta_s[hh, :, rs] = jnp.broadcast_to(d, (t, HEAD_W)).T[0:_SUBLANES, :]

        dk_s[...] = jnp.zeros_like(dk_s)
        dv_s[...] = jnp.zeros_like(dv_s)

        def head_step(hh, base, diag):
            hs = slice(hh * HEAD_W, (hh + 1) * HEAD_W)
            kv, vv = k_ref[:, hs], v_ref[:, hs]
            qv, dov = q_ref[pl.ds(base, t), hs], do_ref[pl.ds(base, t), hs]
            s_s[hh] = _dot(kv, qv, 'nt')
            dp_s[hh] = _dot(vv, dov, 'nt')
            for rs, cs, c, cut in _pieces(t, rb, diag):
                if cut is None:
                    p_s[hh, rs, cs] = jnp.zeros((rb, _LANES), p_s.dtype)
                    ds_s[hh, rs, cs] = jnp.zeros((rb, _LANES), ds_s.dtype)
                    continue
                lanes = pl.ds(pl.multiple_of(base + c * _LANES, _LANES), _LANES)
                lse = jnp.tile(lse_ref[hh * _SUBLANES:(hh + 1) * _SUBLANES, lanes], (rb // _SUBLANES, 1))
                dl = jnp.tile(delta_s[hh, :, lanes], (rb // _SUBLANES, 1))
                p = jnp.exp2(s_s[hh, rs, cs] * _SCORE_C - lse)
                if cut:
                    p = jnp.where(_lane_ge(rb, cut), p, 0.0)
                p_s[hh, rs, cs] = p.astype(p_s.dtype)
                ds_s[hh, rs, cs] = (p * (dp_s[hh, rs, cs] - dl)).astype(ds_s.dtype)
            dsv = ds_s[hh]
            dv_s[hh] += _dot(p_s[hh], dov, 'nn')
            dk_s[hh] += _dot(dsv, qv, 'nn')
            dqt_s[hh, :, pl.ds(base, t)] += _dot(kv, dsv, 'tn')

        def step(i, diag):
            base = pl.multiple_of(i * t, t)
            for hh in range(hp):
                head_step(hh, base, diag)

        def loop_body(i, carry):
            step(i, False)
            return carry

        step(j, True)
        lax.fori_loop(j + 1, n, loop_body, 0)
        for hh in range(hp):
            hs = slice(hh * HEAD_W, (hh + 1) * HEAD_W)
            dk_ref[:, hs] = dk_s[hh] * scale
            dv_ref[:, hs] = dv_s[hh].astype(dv_ref.dtype)

        @pl.when(j == n - 1)
        def _():
            for hh in range(hp):
                for b in range(n):
                    rs = slice(b * t, (b + 1) * t)
                    dq_ref[rs, hh * HEAD_W:(hh + 1) * HEAD_W] = dqt_s[hh, :, rs].T * scale

    jb = pl.BlockSpec((t, W), lambda h, j: (j, h))
    fb = pl.BlockSpec((T, W), lambda h, j: (0, h))
    sc = pltpu.VMEM((hp, t, HEAD_W), F32)
    return pl.pallas_call(
        body, name="attn_bwd", grid=(H // hp, n),
        in_specs=[jb, jb, fb, fb, fb, pl.BlockSpec((hp * _SUBLANES, T), lambda h, j: (h, 0))], out_specs=[fb, jb, jb],
        out_shape=[jax.ShapeDtypeStruct((T, HW), F32), jax.ShapeDtypeStruct((T, HW), F32),
                   jax.ShapeDtypeStruct((T, HW), _MXU_DTYPE)],
        scratch_shapes=[pltpu.VMEM((hp, t, t), F32), pltpu.VMEM((hp, t, t), F32), pltpu.VMEM((hp, t, t), _MXU_DTYPE),
                        pltpu.VMEM((hp, t, t), _MXU_DTYPE), sc, sc, pltpu.VMEM((hp, _SUBLANES, T), F32),
                        pltpu.VMEM((hp, HEAD_W, T), F32)],
        compiler_params=_params(("parallel", "arbitrary"), 3 * _nbytes((T, W), _MXU_DTYPE) + _nbytes((T, W), F32)
                                + 4 * _nbytes((t, W), F32), hp * (4 * _nbytes((t, t), F32) + _nbytes((T, HEAD_W), F32))),
    )(k, v, q, o, do, lse2)


def _gate_fwd(pg, gb, yc, ym):
    T, D = yc.shape
    tm = _tile(T, _ROW_TILE, _PACKED_ROWS)

    def body(g0_ref, g1_ref, gb_ref, yc_ref, ym_ref, y_ref):
        s0 = _sig(g0_ref[...] + gb_ref[0:1, :])
        s1 = _sig(g1_ref[...] + gb_ref[1:2, :])
        y_ref[...] = (s0 * yc_ref[...] + s1 * ym_ref[...]).astype(y_ref.dtype)

    return pl.pallas_call(
        body, name="gate_fwd", grid=(T // tm,),
        in_specs=[_row_spec(tm, D, 0), _row_spec(tm, D, 1), _full_spec((2, D)), _row_spec(tm, D), _row_spec(tm, D)],
        out_specs=_row_spec(tm, D),
        out_shape=jax.ShapeDtypeStruct((T, D), _MXU_DTYPE),
        compiler_params=_params(("parallel",), 5 * _nbytes((tm, D), F32), 4 * _nbytes((tm, D), F32)),
    )(pg, pg, gb, yc, ym)


def _gate_bwd(pg, gb, yc, ym, dy):
    T, D = yc.shape
    tm = _tile(T, _ROW_TILE, _PACKED_ROWS)

    def body(g0_ref, g1_ref, gb_ref, yc_ref, ym_ref, dy_ref, dyc_ref, dym_ref, dpg_ref, dgb_ref):
        _init_acc([dgb_ref])
        dyv = dy_ref[...]
        s0 = _sig(g0_ref[...] + gb_ref[0:1, :])
        s1 = _sig(g1_ref[...] + gb_ref[1:2, :])
        dyc_ref[...] = (dyv * s0).astype(dyc_ref.dtype)
        dym_ref[...] = (dyv * s1).astype(dym_ref.dtype)
        d0 = dyv * yc_ref[...] * s0 * (1.0 - s0)
        d1 = dyv * ym_ref[...] * s1 * (1.0 - s1)
        dpg_ref[:, :D] = d0.astype(dpg_ref.dtype)
        dpg_ref[:, D:] = d1.astype(dpg_ref.dtype)
        dgb_ref[0:1, :] += _colsum(d0)
        dgb_ref[1:2, :] += _colsum(d1)

    return pl.pallas_call(
        body, name="gate_bwd", grid=(T // tm,),
        in_specs=[_row_spec(tm, D, 0), _row_spec(tm, D, 1), _full_spec((2, D)), _row_spec(tm, D), _row_spec(tm, D),
                  _row_spec(tm, D)],
        out_specs=[_row_spec(tm, D), _row_spec(tm, D), _row_spec(tm, 2 * D), _full_spec((2, D))],
        out_shape=[jax.ShapeDtypeStruct((T, D), _MXU_DTYPE)] * 2 + [jax.ShapeDtypeStruct((T, 2 * D), _MXU_DTYPE),
                                                                   jax.ShapeDtypeStruct((2, D), F32)],
        compiler_params=_params(("arbitrary",), 7 * _nbytes((tm, D), F32), 8 * _nbytes((tm, D), F32)),
    )(pg, pg, gb, yc, ym, dy)


def _place():
    return lax.axis_index("x"), lax.axis_index("y"), lax.axis_index("c")


def _all_gather(name, xs):
    R, C = xs.shape
    MESH = pl.DeviceIdType.MESH

    def body(x_ref, out_ref, send_sems, recv_sems, local_sem):
        x, y, c = _place()
        me, sibling = (x, y, c), (x, y, 1 - c)
        chips = [(1 - x, y), (x, 1 - y), (1 - x, 1 - y)]

        def rows(px, py, pc):
            return out_ref.at[4 * px + 2 * py + pc]

        def copy(k, block, to, src=None):
            return pltpu.make_async_remote_copy(
                src_ref=rows(*block) if src is None else src, dst_ref=rows(*block),
                send_sem=send_sems.at[k], recv_sem=recv_sems.at[k], device_id=to, device_id_type=MESH)

        mine = pltpu.make_async_copy(x_ref, rows(*me), local_sem)
        mine.start()
        first = [copy(0, me, sibling, src=x_ref)]
        first += [copy(1 + j, me, (*chip, c), src=x_ref) for j, chip in enumerate(chips)]
        for cp in first:
            cp.start()
        passed = [copy(4 + j, (*chip, c), sibling) for j, chip in enumerate(chips)]
        for j, chip in enumerate(chips):
            copy(1 + j, (*chip, c), me).wait_recv()
            passed[j].start()
        copy(0, sibling, me).wait_recv()
        for j, chip in enumerate(chips):
            copy(4 + j, (*chip, 1 - c), me).wait_recv()
        for cp in first + passed:
            cp.wait_send()
        mine.wait()

    return pl.pallas_call(
        body, name=name,
        out_shape=jax.ShapeDtypeStruct((N_DEV, R, C), xs.dtype),
        in_specs=[pl.BlockSpec(memory_space=pl.ANY)], out_specs=pl.BlockSpec(memory_space=pl.ANY),
        scratch_shapes=[pltpu.SemaphoreType.DMA((7,)), pltpu.SemaphoreType.DMA((7,)), pltpu.SemaphoreType.DMA],
    )(xs)


_HBM_SPEC = pl.BlockSpec(memory_space=pltpu.HBM)
_SEM_SPEC = pl.BlockSpec(memory_space=pltpu.SEMAPHORE)
_DATAFLOW = pltpu.SideEffectType.DATAFLOW_SIDE_EFFECTING


def _peers():
    x, y, c = _place()
    out = []
    for k in range(1, N_DEV):
        px = 1 - x if k & 4 else x
        py = 1 - y if k & 2 else y
        pc = 1 - c if k & 1 else c
        out.append((k - 1, (px, py, pc), 4 * px + 2 * py + pc))
    return 4 * x + 2 * y + c, out


def _exchange_copies(x_ref, land_ref, send_sems, recv_sems, scatter):
    me, peers = _peers()
    return [pltpu.make_async_remote_copy(
        src_ref=x_ref.at[idx] if scatter else x_ref, dst_ref=land_ref.at[me],
        send_sem=send_sems.at[k], recv_sem=recv_sems.at[k], device_id=dev, device_id_type=pl.DeviceIdType.MESH)
        for k, dev, idx in peers]


def _exchange_start(name, xs, land, scatter):
    def body(x_ref, land_ref, send_sems, recv_sems, x_thru, land_thru, token):
        for cp in _exchange_copies(x_ref, land_ref, send_sems, recv_sems, scatter):
            cp.start()
        token[...] = jnp.zeros_like(token)

    sems = pltpu.SemaphoreType.DMA((N_DEV - 1,))
    return pl.pallas_call(
        body, name=name,
        out_shape=(sems, sems, pltpu.HBM(xs.shape, xs.dtype), pltpu.HBM(land.shape, land.dtype),
                   jax.ShapeDtypeStruct((_SUBLANES, _LANES), F32)),
        in_specs=(_HBM_SPEC, _HBM_SPEC),
        out_specs=(_SEM_SPEC, _SEM_SPEC, _HBM_SPEC, _HBM_SPEC, pl.BlockSpec(memory_space=pltpu.VMEM)),
        input_output_aliases={0: 2, 1: 3},
        compiler_params=pltpu.CompilerParams(has_side_effects=_DATAFLOW),
    )(pltpu.with_memory_space_constraint(xs, pltpu.HBM), pltpu.with_memory_space_constraint(land, pltpu.HBM))


def _exchange_wait(name, send_sems, recv_sems, xs, land, after, scatter):
    def body(x_ref, land_ref, send_sems, recv_sems, after_ref, x_dead, got_ref):
        for cp in _exchange_copies(x_ref, land_ref, send_sems, recv_sems, scatter):
            cp.wait_send()
            cp.wait_recv()

    return pl.pallas_call(
        body, name=name,
        out_shape=(pltpu.HBM(xs.shape, xs.dtype), pltpu.HBM(land.shape, land.dtype)),
        in_specs=(_HBM_SPEC, _HBM_SPEC, _SEM_SPEC, _SEM_SPEC, pl.BlockSpec(memory_space=pl.ANY)),
        out_specs=(_HBM_SPEC, _HBM_SPEC), input_output_aliases={0: 0, 1: 1},
        compiler_params=pltpu.CompilerParams(has_side_effects=_DATAFLOW),
    )(xs, land, send_sems, recv_sems, after)[1]


def _own_slot(block, me):
    land = lax.empty((N_DEV,) + block.shape, block.dtype)
    return lax.dynamic_update_slice(land, block[None], (me,) + (0,) * block.ndim)


def _sum_blocks(name, parts, scale=None):
    n, R, C = parts.shape
    tr = _tile(R, _PACK_ROW_MULT, _PACKED_ROWS)

    def body(p_ref, o_ref):
        acc = p_ref[0].astype(F32)
        for s in range(1, n):
            acc = acc + p_ref[s].astype(F32)
        if scale is not None:
            acc = acc * scale
        o_ref[...] = acc

    return pl.pallas_call(
        body, name=name, grid=(R // tr,),
        in_specs=[pl.BlockSpec((n, tr, C), lambda i: (0, i, 0))], out_specs=pl.BlockSpec((tr, C), lambda i: (i, 0)),
        out_shape=jax.ShapeDtypeStruct((R, C), F32),
        compiler_params=_params(("parallel",), _nbytes((n, tr, C), parts.dtype) + _nbytes((tr, C), F32), 2 * _nbytes((tr, C), F32)),
    )(parts)


def _adamw_math(wv, gv, mv, vv):
    mv = ADAM_B1 * mv + (1.0 - ADAM_B1) * gv
    vv = ADAM_B2 * vv + (1.0 - ADAM_B2) * (gv * gv)
    m_hat = mv / (1.0 - ADAM_B1 ** ADAM_STEP)
    v_hat = vv / (1.0 - ADAM_B2 ** ADAM_STEP)
    return -ADAM_LR * (m_hat / (jnp.sqrt(v_hat) + ADAM_EPS) + ADAM_WD * wv), mv, vv


def _adamw_layer(l, w, g, m, v, outs, after):
    depth, R, C = w.shape
    tr = _tile(R, _ROW_TILE, _SUBLANES)

    def body(w_ref, g_ref, m_ref, v_ref, after_ref, *rest):
        go_ref, d_ref, nm_ref, nv_ref = rest[-4:]
        gv = g_ref[...]
        go_ref[...] = gv
        d_ref[...], nm_ref[...], nv_ref[...] = _adamw_math(w_ref[...], gv, m_ref[...], v_ref[...])

    lay = pl.BlockSpec((None, tr, C), lambda i: (l, i, 0))
    shape = jax.ShapeDtypeStruct(w.shape, F32)
    return pl.pallas_call(
        body, name="adamw_layer", grid=(R // tr,),
        in_specs=[lay, pl.BlockSpec((tr, C), lambda i: (i, 0)), lay, lay] + [pl.BlockSpec(memory_space=pl.ANY)] * (5 if outs else 1),
        out_specs=[lay] * 4, out_shape=[shape] * 4,
        input_output_aliases={5 + k: k for k in range(4)} if outs else {},
        compiler_params=_params(("parallel",), 8 * _nbytes((tr, C), F32), 4 * _nbytes((tr, C), F32)),
    )(w, g, m, v, after, *(outs or ()))


def _adamw(w, g, m, v):
    shape = w.shape
    w2, g2, m2, v2 = (t.reshape(-1, shape[-1]) for t in (w, g, m, v))
    R, C = w2.shape
    tr = _tile(R, _ROW_TILE, _SUBLANES)

    def body(w_ref, g_ref, m_ref, v_ref, d_ref, nm_ref, nv_ref):
        d_ref[...], nm_ref[...], nv_ref[...] = _adamw_math(w_ref[...], g_ref[...], m_ref[...], v_ref[...])

    spec = pl.BlockSpec((tr, C), lambda i: (i, 0))
    outs = pl.pallas_call(
        body, name="adamw", grid=(R // tr,),
        in_specs=[spec] * 4, out_specs=[spec] * 3,
        out_shape=[jax.ShapeDtypeStruct((R, C), F32)] * 3,
        compiler_params=_params(("parallel",), 7 * _nbytes((tr, C), F32), 4 * _nbytes((tr, C), F32)),
    )(w2, g2, m2, v2)
    return tuple(o.reshape(shape) for o in outs)


def _pack_rows(parts, lead=()):
    out, spans, r0 = [], [], 0
    unit = _PACKED_ROWS * _PACK_W
    for p in parts:
        flat = p.reshape(lead + (-1,))
        size = flat.shape[-1]
        pad = (-size) % unit
        if pad:
            flat = jnp.pad(flat, [(0, 0)] * len(lead) + [(0, pad)])
        rows = (size + pad) // _PACK_W
        out.append(flat.reshape(lead + (rows, _PACK_W)))
        spans.append((r0, rows, p.shape[len(lead):]))
        r0 += rows
    pad = (-r0) % _PACK_ROW_MULT
    if pad:
        out.append(jnp.zeros(lead + (pad, _PACK_W), out[0].dtype))
    return jnp.concatenate(out, axis=len(lead)), spans


def _unpack_rows(packed, span, lead=()):
    r0, rows, shape = span
    size = 1
    for s in shape:
        size *= s
    sl = lax.slice_in_dim(packed, r0, r0 + rows, axis=len(lead))
    return sl.reshape(lead + (-1,))[..., :size].reshape(lead + tuple(shape))


def _pack_flat(parts):
    flat, spans, at = [], [], 0
    for p in parts:
        size = p.size
        padded = size + (-size) % _LANES
        flat.append(jnp.pad(p.reshape(-1), (0, padded - size)))
        spans.append((at, size, p.shape))
        at += padded
    tail = (-at) % (_PACKED_ROWS * _PACK_W)
    if tail:
        flat.append(jnp.zeros((tail,), flat[0].dtype))
    return jnp.concatenate(flat).reshape(-1, _PACK_W), spans


def _unpack_flat(packed, span, lead=()):
    at, size, shape = span
    return lax.slice_in_dim(packed.reshape(lead + (-1,)), at, at + size, axis=len(lead)).reshape(lead + tuple(shape))


def _rope_tables(positions, T):
    inv_freq = ROPE_THETA ** (-jnp.arange(0, ROPE_DIM, 2, dtype=F32) / ROPE_DIM)
    ang = positions.reshape(T, 1).astype(F32) * inv_freq
    cos, sin = jnp.cos(ang), jnp.sin(ang)
    half = ROPE_DIM // 2
    z = lambda n: jnp.zeros((T, n), F32)
    tail = HEAD_W - QK_DIM
    ct = jnp.concatenate([jnp.ones((T, NOPE_DIM), F32), cos, cos, z(tail)], axis=1)
    sa = jnp.concatenate([z(NOPE_DIM), -sin, z(half), z(tail)], axis=1)
    sb = jnp.concatenate([z(NOPE_DIM), z(half), sin, z(tail)], axis=1)
    return ct, sa, sb


def _pad_heads(wt, per_head, keep_from=0, keep=None):
    K = wt.shape[1]
    keep = per_head if keep is None else keep
    w3 = wt.reshape(N_HEADS, per_head, K)[:, keep_from:keep_from + keep]
    return jnp.pad(w3, ((0, 0), (0, HEAD_W - keep), (0, 0))).reshape(N_HEADS * HEAD_W, K)


def _unpad_heads(g, keep):
    return g.reshape(N_HEADS, HEAD_W, g.shape[1])[:, :keep]


def _layer_weights(W):
    D = W['w_out'].shape[1]
    C = W['w_conv_out'].shape[1]
    QL, KVL = W['w_uq'].shape[1], W['w_ukv'].shape[1]
    o1 = 2 * C
    o2, o3 = o1 + QL, o1 + QL + KVL
    o4 = o3 + ROPE_DIM
    win = W['w_in']
    zr = lambda n: jnp.zeros((n, D), win.dtype)
    L = dict(W)
    L['w_c'] = win[:o1]
    L['w_l'] = jnp.concatenate([win[o1:o3], zr(NOPE_DIM), win[o3:o4], zr(HEAD_W - QK_DIM)], axis=0)
    L['w_g'] = win[o4:]
    L['w_q'] = _pad_heads(W['w_uq'], QK_DIM)
    L['w_k'] = _pad_heads(W['w_ukv'], NOPE_DIM + V_DIM, 0, NOPE_DIM)
    L['w_v'] = _pad_heads(W['w_ukv'], NOPE_DIM + V_DIM, NOPE_DIM, V_DIM)
    wmo = W['w_mla_out'].reshape(D, N_HEADS, V_DIM)
    L['w_mo'] = jnp.pad(wmo, ((0, 0), (0, 0), (0, HEAD_W - V_DIM))).reshape(D, N_HEADS * HEAD_W)
    L['dims'] = (D, C, QL, KVL)
    return L


def _row(v, width=None):
    v = v.reshape(1, -1)
    if width is not None and v.shape[1] < width:
        v = jnp.pad(v, ((0, 0), (0, width - v.shape[1])))
    return v


def _ffn_fwd(x, g, wg, wu, wd):
    h = _rms_fwd(x, _row(g))
    a, b, u = _ffn_up(h, wg, wu)
    x_out = _mm("ffn_down", u, wd, 'nn', res=x, scale=0.5)
    return x_out, (x, h, a, b, u)


def _ffn_bwd_weights(dx, saved, wd, after):
    x, h, a, b, u = saved
    d_wd = _wgrad("ffn_dwd", u, dx, scale=0.5)
    da, db = _ffn_bwd_up(dx, wd, a, b, after)
    return da, db, (_wgrad("ffn_dwg", da, h), _wgrad("ffn_dwg", db, h), d_wd)


def _ffn_bwd_input(dx, saved, g, wg, wu, da, db, after):
    return _dh_rms_bwd("ffn_dh_rms", [(da, wg), (db, wu)], saved[0], _row(g), dx, after)


def _layer_fwd(x, L, S, tabs):
    D, C, QL, KVL = L['dims']
    rope_cb = (QL + KVL) // HEAD_W
    x1, ffn1 = _ffn_fwd(x, S['ffn1_norm'], L['ffn1_w_gate'], L['ffn1_w_up'], L['ffn1_w_down'])
    h = _rms_fwd(x1, _row(S['mix_norm']))
    pc = _mm("proj_c", h, L['w_c'], 'nt')
    pl_ = _mm("proj_l", h, L['w_l'], 'nt')
    pg = _mm("proj_g", h, L['w_g'], 'nt')
    conv_w = jnp.pad(S['conv_w'], ((0, HALO - CONV_WIDTH), (0, 0)))
    z, c = _conv_fwd(pc, conv_w, _row(S['conv_b']), _row(S['conv_ln_g']), _row(S['conv_ln_b']))
    yc = _mm("conv_out", c, L['w_conv_out'], 'nt')
    cqn, ckvn = _lat_norm_fwd(pl_, _row(S['cq_norm']), _row(S['ckv_norm']), QL, KVL)
    q_raw = _mm("up_q", cqn, L['w_q'], 'nt')
    k_raw = _mm("up_kv", ckvn, L['w_k'], 'nt')
    v = _mm("up_kv", ckvn, L['w_v'], 'nt', out_dtype=_MXU_DTYPE)
    gq, gk = _row(S['q_norm'], HEAD_W), _row(S['k_norm'], HEAD_W)
    qn, kn = _qk_prep_fwd(q_raw, k_raw, pl_, rope_cb, *tabs, gq, gk)
    o, lse = _attn_fwd(qn, kn, v)
    ym = _mm("mla_out", o, L['w_mo'], 'nt')
    y = _gate_fwd(pg, S['gate_bias'], yc, ym)
    x2 = _mm("mix_out", y, L['w_out'], 'nn', res=x1)
    x3, ffn2 = _ffn_fwd(x2, S['ffn2_norm'], L['ffn2_w_gate'], L['ffn2_w_up'], L['ffn2_w_down'])
    saved = dict(ffn1=ffn1, ffn2=ffn2, x1=x1, h=h, pc=pc, pl=pl_, pg=pg, z=z, c=c, yc=yc, cqn=cqn, ckvn=ckvn,
                 q_raw=q_raw, k_raw=k_raw, v=v, qn=qn, kn=kn, o=o, lse=lse, ym=ym, y=y, conv_w=conv_w, gq=gq, gk=gk)
    return x3, saved


def _layer_bwd_mix(dx, L, S, tabs, A, after):
    D, C, QL, KVL = L['dims']
    rope_cb = (QL + KVL) // HEAD_W
    G, g = {}, {}
    da, db, (G['ffn2_w_gate'], G['ffn2_w_up'], G['ffn2_w_down']) = _ffn_bwd_weights(dx, A['ffn2'], L['ffn2_w_down'], after)
    dx, g['ffn2_norm'] = _ffn_bwd_input(dx, A['ffn2'], S['ffn2_norm'], L['ffn2_w_gate'], L['ffn2_w_up'], da, db, after)
    G['w_out'] = _wgrad("mix_dwout", A['y'], dx)
    dy = _mm("mix_dy", dx, L['w_out'], 'nt')
    dyc, dym, dpg, g['gate_bias'] = _gate_bwd(A['pg'], S['gate_bias'], A['yc'], A['ym'], dy)
    G['w_conv_out'] = _wgrad("conv_dwout", dyc, A['c'])
    dc = _mm("conv_dc", dyc, L['w_conv_out'], 'nn')
    dpc, dcw, g['conv_b'], g['conv_ln_g'], g['conv_ln_b'] = _conv_bwd(
        A['pc'], A['z'], dc, A['conv_w'], _row(S['conv_ln_g']), _row(S['conv_ln_b']))
    g['conv_w'] = dcw[:CONV_WIDTH]
    d_wmo = _wgrad("mla_dwout", dym, A['o'])
    G['w_mla_out'] = d_wmo.reshape(D, N_HEADS, HEAD_W)[:, :, :V_DIM].reshape(D, N_HEADS * V_DIM)
    do = _mm("mla_do", dym, L['w_mo'], 'nn', out_dtype=_MXU_DTYPE)
    dqn, dkn, dv = _attn_bwd(A['qn'], A['kn'], A['v'], A['o'], do, A['lse'])
    dq_raw, dk_raw, drope, dgq, dgk = _qk_prep_bwd(A['q_raw'], A['k_raw'], A['pl'], rope_cb, *tabs, A['gq'], A['gk'], dqn, dkn)
    g['q_norm'], g['k_norm'] = dgq[:, :QK_DIM], dgk[:, :QK_DIM]
    G['w_uq'] = _unpad_heads(_wgrad("up_dwq", dq_raw, A['cqn']), QK_DIM).reshape(N_HEADS * QK_DIM, QL)
    d_wk = _unpad_heads(_wgrad("up_dwkv", dk_raw, A['ckvn']), NOPE_DIM)
    d_wv = _unpad_heads(_wgrad("up_dwkv", dv, A['ckvn']), V_DIM)
    G['w_ukv'] = jnp.concatenate([d_wk, d_wv], axis=1).reshape(N_HEADS * (NOPE_DIM + V_DIM), KVL)
    dcq = _mm("up_dcq", dq_raw, L['w_q'], 'nn')
    dckv = _mm("up_dckv_k", dk_raw, L['w_k'], 'nn')
    dckv = _mm("up_dckv_v", dv, L['w_v'], 'nn', res=dckv)
    dpl, g['cq_norm'], g['ckv_norm'] = _lat_norm_bwd(A['pl'], _row(S['cq_norm']), _row(S['ckv_norm']), dcq, dckv, drope, QL, KVL)
    d_wc = _wgrad("proj_dwc", dpc, A['h'])
    d_wl = _wgrad("proj_dwl", dpl, A['h'])
    d_wg = _wgrad("proj_dwg", dpg, A['h'])
    ql = QL + KVL
    G['w_in'] = jnp.concatenate([d_wc, d_wl[:ql], d_wl[ql + NOPE_DIM:ql + NOPE_DIM + ROPE_DIM], d_wg], axis=0)
    dx, g['mix_norm'] = _dh_rms_bwd("proj_dh_rms", [(dpc, L['w_c']), (dpl, L['w_l']), (dpg, L['w_g'])],
                                    A['x1'], _row(S['mix_norm']), dx, after)
    return dx, G, g


def kernel(x, positions, ffn1_norm, ffn1_w_gate, ffn1_w_up, ffn1_w_down, mix_norm, w_in, gate_bias, conv_w, conv_b, conv_ln_g, conv_ln_b, w_conv_out, cq_norm, ckv_norm, w_uq, w_ukv, q_norm, k_norm, w_mla_out, w_out, ffn2_norm, ffn2_w_gate, ffn2_w_up, ffn2_w_down, loss_target, m_ffn1_norm, m_ffn1_w_gate, m_ffn1_w_up, m_ffn1_w_down, m_mix_norm, m_w_in, m_gate_bias, m_conv_w, m_conv_b, m_conv_ln_g, m_conv_ln_b, m_w_conv_out, m_cq_norm, m_ckv_norm, m_w_uq, m_w_ukv, m_q_norm, m_k_norm, m_w_mla_out, m_w_out, m_ffn2_norm, m_ffn2_w_gate, m_ffn2_w_up, m_ffn2_w_down, v_ffn1_norm, v_ffn1_w_gate, v_ffn1_w_up, v_ffn1_w_down, v_mix_norm, v_w_in, v_gate_bias, v_conv_w, v_conv_b, v_conv_ln_g, v_conv_ln_b, v_w_conv_out, v_cq_norm, v_ckv_norm, v_w_uq, v_w_ukv, v_q_norm, v_k_norm, v_w_mla_out, v_w_out, v_ffn2_norm, v_ffn2_w_gate, v_ffn2_w_up, v_ffn2_w_down):
    w = dict(zip(WEIGHTS, (ffn1_norm, ffn1_w_gate, ffn1_w_up, ffn1_w_down, mix_norm, w_in, gate_bias, conv_w, conv_b, conv_ln_g, conv_ln_b, w_conv_out, cq_norm, ckv_norm, w_uq, w_ukv, q_norm, k_norm, w_mla_out, w_out, ffn2_norm, ffn2_w_gate, ffn2_w_up, ffn2_w_down)))
    m = dict(zip(WEIGHTS, (m_ffn1_norm, m_ffn1_w_gate, m_ffn1_w_up, m_ffn1_w_down, m_mix_norm, m_w_in, m_gate_bias, m_conv_w, m_conv_b, m_conv_ln_g, m_conv_ln_b, m_w_conv_out, m_cq_norm, m_ckv_norm, m_w_uq, m_w_ukv, m_q_norm, m_k_norm, m_w_mla_out, m_w_out, m_ffn2_norm, m_ffn2_w_gate, m_ffn2_w_up, m_ffn2_w_down)))
    v = dict(zip(WEIGHTS, (v_ffn1_norm, v_ffn1_w_gate, v_ffn1_w_up, v_ffn1_w_down, v_mix_norm, v_w_in, v_gate_bias, v_conv_w, v_conv_b, v_conv_ln_g, v_conv_ln_b, v_w_conv_out, v_cq_norm, v_ckv_norm, v_w_uq, v_w_ukv, v_q_norm, v_k_norm, v_w_mla_out, v_w_out, v_ffn2_norm, v_ffn2_w_gate, v_ffn2_w_up, v_ffn2_w_down)))
    depth = ffn1_norm.shape[0]
    T, D = x.shape[1], x.shape[2]
    xs = x.reshape(T, D)
    target = loss_target.reshape(T, D)
    tabs = _rope_tables(positions, T)
    me = 4 * lax.axis_index("x") + 2 * lax.axis_index("y") + lax.axis_index("c")

    big = list(BIG)
    packed = []
    for l in range(depth):
        p, spans = _pack_rows([(w[n][l].T if BIG[n] else w[n][l]).astype(_WIRE_DTYPE) for n in big])
        packed.append(p)
    small_names = [(l, n) for l in range(depth) for n in SMALL_SHARDED]
    small_packed, small_spans = _pack_flat([w[n][l].T for l, n in small_names])
    small_flight = _exchange_start("small_weights_start", small_packed, _own_slot(small_packed, me), False)
    gathered = {0: _all_gather("gather_weights", packed[0])}
    in_flight = {l: _exchange_start(f"gather_start_{l}", packed[l], _own_slot(packed[l], me), False)
                 for l in range(1, depth)}
    send_sems, recv_sems, src, land, _ = small_flight
    small_gathered = _exchange_wait("small_weights_wait", send_sems, recv_sems, src, land, gathered[0], False)
    for st in in_flight.values():
        xs, _ = lax.optimization_barrier((xs, st[4]))

    def layer_weights(l):
        W = {}
        for n, span in zip(big, spans):
            sh = _unpack_rows(gathered[l], span, (N_DEV,))
            W[n] = sh.reshape(N_DEV * sh.shape[1], sh.shape[2])
        S = {n: w[n][l] for n in REPLICATED}
        for (ll, n), span in zip(small_names, small_spans):
            if ll == l:
                sh = _unpack_flat(small_gathered, span, (N_DEV,))
                S[n] = sh.reshape(N_DEV * sh.shape[1], sh.shape[2]).T
        return _layer_weights(W), S

    layers, acts = [], []
    hcur = xs
    for l in range(depth):
        if l > 0:
            send_sems, recv_sems, src, land, _ = in_flight[l]
            gathered[l] = _exchange_wait(f"gather_wait_{l}", send_sems, recv_sems, src, land, hcur, False)
        layers.append(layer_weights(l))
        hcur, saved = _layer_fwd(hcur, *layers[l], tabs)
        acts.append(saved)
    dx, loss_part = _loss_grad(hcur, target)

    stage_names = {'mix': [n for n in big if not n.startswith('ffn1_')], 'down': ['ffn1_w_down'],
                   'gateup': ['ffn1_w_gate', 'ffn1_w_up']}
    small_grads, gspans, in_flight = [{} for _ in range(depth)], {}, {}

    def send_stage(l, stage, G):
        blocks = [G[n].reshape(N_DEV, -1, G[n].shape[1]) for n in stage_names[stage]]
        send, gspans[stage] = _pack_rows(blocks, (N_DEV,))
        own = lax.dynamic_index_in_dim(send, me, 0, keepdims=False)
        in_flight[l, stage] = _exchange_start(f"scatter_start_{l}_{stage}", send, _own_slot(own, me), True)
        return in_flight[l, stage][4]

    after = jnp.zeros((_SUBLANES, _LANES), F32)
    for l in reversed(range(depth)):
        L, S = layers[l]
        dx, G, g = _layer_bwd_mix(dx, L, S, tabs, acts[l], after)
        small_grads[l].update(g)
        after = send_stage(l, 'mix', G)
        x_in, h, a, b, u = acts[l]['ffn1']
        after = send_stage(l, 'down', {'ffn1_w_down': _wgrad("ffn_dwd", u, dx, scale=0.5)})
        da, db = _ffn_bwd_up(dx, L['ffn1_w_down'], a, b, after)
        after = send_stage(l, 'gateup', {'ffn1_w_gate': _wgrad("ffn_dwg", da, h), 'ffn1_w_up': _wgrad("ffn_dwg", db, h)})
        dx, small_grads[l]['ffn1_norm'] = _ffn_bwd_input(dx, acts[l]['ffn1'], S['ffn1_norm'], L['ffn1_w_gate'],
                                                          L['ffn1_w_up'], da, db, after)
    small_list = [(l, n) for l in range(depth) for n in REPLICATED + SMALL_SHARDED]
    loss_part = loss_part + after[:1, :]
    small_send, sspans = _pack_flat([small_grads[l][n] for l, n in small_list] + [loss_part])
    small_flight = _exchange_start("small_grads_start", small_send, _own_slot(small_send, me), False)

    out4 = {}
    after = small_flight[4]
    for (l, stage), (send_sems, recv_sems, src, land, _) in in_flight.items():
        got = _exchange_wait(f"scatter_wait_{l}_{stage}", send_sems, recv_sems, src, land, after, True)
        reduced = _sum_blocks("sum_grads_direct", got)
        for n, span in zip(stage_names[stage], gspans[stage]):
            form = (lambda t: jnp.swapaxes(t, 1, 2)) if BIG[n] else (lambda t: t)
            out4[n] = _adamw_layer(l, form(w[n]), _unpack_rows(reduced, span), form(m[n]), form(v[n]), out4.get(n), after)
            after = out4[n][1]
    for n in big:
        if BIG[n]:
            out4[n] = [jnp.swapaxes(t, 1, 2) for t in out4[n]]
    send_sems, recv_sems, src, land, _ = small_flight
    small_all = _sum_blocks("sum_small", _exchange_wait("small_grads_wait", send_sems, recv_sems, src, land, after, False))
    loss = _unpack_flat(small_all, sspans[-1])[0, 0]
    for n in WEIGHTS:
        if n in BIG:
            continue
        per_layer = []
        for l in range(depth):
            gs = _unpack_flat(small_all, sspans[small_list.index((l, n))])
            if n in SMALL_SHARDED:
                cols = w[n].shape[-1]
                gs = lax.dynamic_slice_in_dim(gs, me * cols, cols, axis=1)
            per_layer.append(gs.reshape(w[n].shape[1:]))
        g = jnp.stack(per_layer, axis=0)
        out4[n] = (g,) + _adamw(w[n], g, m[n], v[n])
    return (loss, dx.reshape(1, T, D), *[out4[n][k] for k in range(4) for n in WEIGHTS])
```

```python
import jax
import jax.numpy as jnp
from jax import lax
from jax.experimental import pallas as pl
from jax.experimental.pallas import tpu as pltpu

F32 = jnp.float32
_MXU_DTYPE = jnp.bfloat16
_WIRE_DTYPE = jnp.bfloat16

_LANES = 128
_SUBLANES = 8
_PACKED_ROWS = 16
_V7X_VMEM_BYTES = 64 * 2 ** 20
_VMEM_HEADROOM = 8 * 2 ** 20

N_DEV = 8
EPS = 1e-6
CHUNK = 64
CONV_WIDTH = 31
N_HEADS = 8
NOPE_DIM = 64
ROPE_DIM = 32
QK_DIM = NOPE_DIM + ROPE_DIM
V_DIM = 64
HEAD_W = _LANES
ROPE_THETA = 10000.0
HALO = 32

ADAM_LR = 0.001
ADAM_B1 = 0.9
ADAM_B2 = 0.999
ADAM_EPS = 1e-08
ADAM_WD = 0.01
ADAM_STEP = 10

_ROW_TILE = 512
_MM_TILE = 512
_FFN_TILE = 512
_MM_TILE_N = 1408
_MM_TILE_K = 1024
_TN_TILE_M = 1408
_ATT_TILE = 512
_ATT_ROWS = 32
_ATT_HEADS = 2
_PACK_W = 1024
_PACK_ROW_MULT = 128

WEIGHTS = ['ffn1_norm', 'ffn1_w_gate', 'ffn1_w_up', 'ffn1_w_down', 'mix_norm', 'w_in', 'gate_bias',
           'conv_w', 'conv_b', 'conv_ln_g', 'conv_ln_b', 'w_conv_out', 'cq_norm', 'ckv_norm', 'w_uq',
           'w_ukv', 'q_norm', 'k_norm', 'w_mla_out', 'w_out', 'ffn2_norm', 'ffn2_w_gate', 'ffn2_w_up',
           'ffn2_w_down']
BIG = {'ffn1_w_gate': True, 'ffn1_w_up': True, 'ffn1_w_down': False, 'w_in': True, 'w_conv_out': True,
       'w_uq': True, 'w_ukv': True, 'w_mla_out': True, 'w_out': False, 'ffn2_w_gate': True,
       'ffn2_w_up': True, 'ffn2_w_down': False}
SMALL_SHARDED = ['conv_w', 'gate_bias']
REPLICATED = ['ffn1_norm', 'mix_norm', 'conv_b', 'conv_ln_g', 'conv_ln_b', 'cq_norm', 'ckv_norm',
              'q_norm', 'k_norm', 'ffn2_norm']


def _tile(n, target, mult=_LANES):
    if n <= target:
        return n
    for d in range(target - target % mult, 0, -mult):
        if n % d == 0:
            return d
    return n


def _nbytes(shape, dtype):
    n = 1
    for s in shape:
        n *= s
    return n * jnp.dtype(dtype).itemsize


def _params(dims, block_bytes, extra_bytes=0):
    need = 2 * block_bytes + extra_bytes + _VMEM_HEADROOM
    assert need <= _V7X_VMEM_BYTES, (need, dims)
    return pltpu.CompilerParams(dimension_semantics=dims, vmem_limit_bytes=_V7X_VMEM_BYTES - _VMEM_HEADROOM)


def _mxu(x, scale=None):
    if scale is not None:
        x = x * scale
    return x if x.dtype == _MXU_DTYPE else x.astype(_MXU_DTYPE)


def _dot(a, b, mode):
    dims = {'nn': (((1,), (0,)), ((), ())), 'nt': (((1,), (1,)), ((), ())), 'tn': (((0,), (0,)), ((), ()))}[mode]
    return lax.dot_general(a, b, dims, preferred_element_type=F32)


def _colsum(x):
    return jnp.sum(x, axis=0, keepdims=True)


def _sig(x):
    return 0.5 * jnp.tanh(0.5 * x) + 0.5


def _mm(name, a, b, mode, *, out_dtype=F32, res=None, scale=None, a_scale=None, tm=None, tn=None, tk=None):
    if mode == 'nn':
        (M, K), (_, N) = a.shape, b.shape
    elif mode == 'nt':
        (M, K), (N, _) = a.shape, b.shape
    else:
        (K, M), (_, N) = a.shape, b.shape
    tm = _tile(M, tm or (_TN_TILE_M if mode == 'tn' else _MM_TILE))
    tn = _tile(N, tn or _MM_TILE_N)
    tk = _tile(K, tk or (_MM_TILE_K if mode == 'tn' else 4096))
    nk = K // tk
    grid = (M // tm, N // tn, nk)
    a_spec = pl.BlockSpec((tk, tm), lambda i, j, k: (k, i)) if mode == 'tn' else pl.BlockSpec((tm, tk), lambda i, j, k: (i, k))
    b_spec = pl.BlockSpec((tn, tk), lambda i, j, k: (j, k)) if mode == 'nt' else pl.BlockSpec((tk, tn), lambda i, j, k: (k, j))
    o_spec = pl.BlockSpec((tm, tn), lambda i, j, k: (i, j))
    has_res = res is not None

    def body(*refs):
        a_ref, b_ref = refs[0], refs[1]
        res_ref = refs[2] if has_res else None
        o_ref = refs[3] if has_res else refs[2]
        acc_ref = refs[-1] if nk > 1 else None

        def finish(p):
            if scale is not None:
                p = p * scale
            if has_res:
                p = res_ref[...] + p
            o_ref[...] = p.astype(o_ref.dtype)

        p = _dot(_mxu(a_ref[...], a_scale), _mxu(b_ref[...]), mode)
        if nk == 1:
            finish(p)
        else:
            k = pl.program_id(2)

            @pl.when(k == 0)
            def _():
                acc_ref[...] = p

            @pl.when(k > 0)
            def _():
                acc_ref[...] += p

            @pl.when(k == nk - 1)
            def _():
                finish(acc_ref[...])

    blocks = (_nbytes((tm, tk), a.dtype) + _nbytes((tk, tn), b.dtype) + _nbytes((tm, tn), out_dtype)
              + (_nbytes((tm, tn), F32) if has_res else 0))
    extra = ((2 + (nk > 1)) * _nbytes((tm, tn), F32) + (a.dtype != _MXU_DTYPE) * _nbytes((tm, tk), _MXU_DTYPE)
             + (b.dtype != _MXU_DTYPE) * _nbytes((tk, tn), _MXU_DTYPE))
    return pl.pallas_call(
        body, name=name, grid=grid,
        in_specs=[a_spec, b_spec] + ([o_spec] if has_res else []),
        out_specs=o_spec,
        out_shape=jax.ShapeDtypeStruct((M, N), out_dtype),
        scratch_shapes=[pltpu.VMEM((tm, tn), F32)] if nk > 1 else [],
        compiler_params=_params(("parallel", "parallel", "arbitrary"), blocks, extra),
    )(*([a, b] + ([res] if has_res else [])))


def _wgrad(name, a, b, scale=None):
    return _mm(name, a, b, 'tn', out_dtype=_WIRE_DTYPE, scale=scale)


def _col_chunks(n, width=2 * _LANES):
    return [slice(c, min(c + width, n)) for c in range(0, n, width)]


def _ffn_up(h, wg, wu):
    T, D = h.shape
    F = wg.shape[0]
    tm, tn = _tile(T, _FFN_TILE), _tile(F, _MM_TILE_N)

    def body(h_ref, wg_ref, wu_ref, a_ref, b_ref, u_ref):
        hv = h_ref[...]
        for cols in _col_chunks(tn):
            a = _dot(hv, wg_ref[cols, :], 'nt')
            b = _dot(hv, wu_ref[cols, :], 'nt')
            a_ref[:, cols] = a
            b_ref[:, cols] = b
            u_ref[:, cols] = (a * _sig(a) * b).astype(u_ref.dtype)

    w_spec = pl.BlockSpec((tn, D), lambda j, i: (j, 0))
    o_spec = pl.BlockSpec((tm, tn), lambda j, i: (i, j))
    blocks = _nbytes((tm, D), h.dtype) + 2 * _nbytes((tn, D), wg.dtype) + 2 * _nbytes((tm, tn), F32) + _nbytes((tm, tn), _MXU_DTYPE)
    return pl.pallas_call(
        body, name="ffn_up", grid=(F // tn, T // tm),
        in_specs=[pl.BlockSpec((tm, D), lambda j, i: (i, 0)), w_spec, w_spec],
        out_specs=[o_spec, o_spec, o_spec],
        out_shape=[jax.ShapeDtypeStruct((T, F), F32), jax.ShapeDtypeStruct((T, F), F32),
                   jax.ShapeDtypeStruct((T, F), _MXU_DTYPE)],
        compiler_params=_params(("parallel", "parallel"), blocks, 4 * _nbytes((tm, tn), F32)),
    )(h, wg, wu)


def _ffn_bwd_up(dx, wd, a, b, after):
    T, D = dx.shape
    F = wd.shape[0]
    tm, tn = _tile(T, _FFN_TILE), _tile(F, _MM_TILE_N)

    def body(dx_ref, wd_ref, a_ref, b_ref, after_ref, da_ref, db_ref):
        dy = _mxu(dx_ref[...], 0.5)
        for cols in _col_chunks(tn):
            du = _dot(dy, wd_ref[cols, :], 'nt')
            av, bv = a_ref[:, cols], b_ref[:, cols]
            s = _sig(av)
            db_ref[:, cols] = (du * av * s).astype(db_ref.dtype)
            da_ref[:, cols] = (du * bv * (s * (1.0 + av * (1.0 - s)))).astype(da_ref.dtype)

    t_spec = pl.BlockSpec((tm, tn), lambda j, i: (i, j))
    blocks = _nbytes((tm, D), F32) + _nbytes((tn, D), wd.dtype) + 2 * _nbytes((tm, tn), F32) + 2 * _nbytes((tm, tn), _MXU_DTYPE)
    return pl.pallas_call(
        body, name="ffn_bwd_up", grid=(F // tn, T // tm),
        in_specs=[pl.BlockSpec((tm, D), lambda j, i: (i, 0)), pl.BlockSpec((tn, D), lambda j, i: (j, 0)), t_spec, t_spec,
                  pl.BlockSpec(memory_space=pl.ANY)],
        out_specs=[t_spec, t_spec],
        out_shape=[jax.ShapeDtypeStruct((T, F), _MXU_DTYPE)] * 2,
        compiler_params=_params(("parallel", "parallel"), blocks, 5 * _nbytes((tm, tn), F32)),
    )(dx, wd, a, b, after)


def _row_spec(tm, w, cb=0):
    return pl.BlockSpec((tm, w), lambda i: (i, cb))


def _full_spec(shape):
    return pl.BlockSpec(shape, lambda i: (0,) * len(shape))


def _init_acc(refs):
    @pl.when(pl.program_id(0) == 0)
    def _():
        for r in refs:
            r[...] = jnp.zeros_like(r)


def _rms_fwd(x, g):
    T, D = x.shape
    tm = _tile(T, _ROW_TILE, _PACKED_ROWS)

    def body(x_ref, g_ref, h_ref):
        xv = x_ref[...]
        r = lax.rsqrt(jnp.mean(xv * xv, axis=-1, keepdims=True) + EPS)
        h_ref[...] = (xv * r * g_ref[...]).astype(h_ref.dtype)

    return pl.pallas_call(
        body, name="rms_fwd", grid=(T // tm,),
        in_specs=[_row_spec(tm, D), _full_spec((1, D))], out_specs=_row_spec(tm, D),
        out_shape=jax.ShapeDtypeStruct((T, D), _MXU_DTYPE),
        compiler_params=_params(("parallel",), 2 * _nbytes((tm, D), F32), 4 * _nbytes((tm, D), F32)),
    )(x, g)


def _dh_rms_bwd(name, pairs, x, g, dres, after):
    T, D = x.shape
    tm = _tile(T, _ROW_TILE // 2, _PACKED_ROWS)
    n = len(pairs)

    def body(*refs):
        x_ref, g_ref, dres_ref, after_ref, dx_ref, dg_ref = refs[2 * n:]
        _init_acc([dg_ref])
        dh = None
        for k in range(n):
            p = _dot(_mxu(refs[2 * k][...]), refs[2 * k + 1][...], 'nn')
            dh = p if dh is None else dh + p
        xv = x_ref[...]
        r = lax.rsqrt(jnp.mean(xv * xv, axis=-1, keepdims=True) + EPS)
        xh = xv * r
        dxh = dh * g_ref[...]
        dx_ref[...] = dres_ref[...] + r * (dxh - xh * jnp.mean(dxh * xh, axis=-1, keepdims=True))
        dg_ref[...] += _colsum(dh * xh)

    in_specs, args, blocks = [], [], 3 * _nbytes((tm, D), F32)
    for a, w in pairs:
        in_specs += [_row_spec(tm, a.shape[1]), _full_spec(w.shape)]
        args += [a, w]
        blocks += _nbytes((tm, a.shape[1]), a.dtype) + _nbytes(w.shape, w.dtype)
    return pl.pallas_call(
        body, name=name, grid=(T // tm,),
        in_specs=in_specs + [_row_spec(tm, D), _full_spec((1, D)), _row_spec(tm, D), pl.BlockSpec(memory_space=pl.ANY)],
        out_specs=[_row_spec(tm, D), _full_spec((1, D))],
        out_shape=[jax.ShapeDtypeStruct((T, D), F32), jax.ShapeDtypeStruct((1, D), F32)],
        compiler_params=_params(("arbitrary",), blocks, 6 * _nbytes((tm, D), F32)),
    )(*args, x, g, dres, after)


def _loss_grad(y, target):
    T, D = y.shape
    tm = _tile(T, _ROW_TILE, _SUBLANES)

    def body(y_ref, t_ref, dy_ref, l_ref):
        _init_acc([l_ref])
        err = y_ref[...] - t_ref[...]
        dy_ref[...] = err * (1.0 / D)
        per_row = jnp.sum(err * err, axis=-1, keepdims=True) * (0.5 / D)
        l_ref[...] += jnp.sum(per_row, axis=0, keepdims=True)

    return pl.pallas_call(
        body, name="loss_grad", grid=(T // tm,),
        in_specs=[_row_spec(tm, D), _row_spec(tm, D)],
        out_specs=[_row_spec(tm, D), _full_spec((1, _LANES))],
        out_shape=[jax.ShapeDtypeStruct((T, D), F32), jax.ShapeDtypeStruct((1, _LANES), F32)],
        compiler_params=_params(("arbitrary",), 3 * _nbytes((tm, D), F32), 3 * _nbytes((tm, D), F32)),
    )(y, target)


def _glu(pc, C):
    a, gate = pc[:, :C], pc[:, C:]
    s = _sig(gate)
    return a, s, a * s


def _tap_groups(offsets):
    groups = {}
    for k, off in enumerate(offsets):
        groups.setdefault(off % _SUBLANES, []).append((k, off - off % _SUBLANES))
    return sorted(groups.items())


def _conv_fwd(pc, w, cb, lg, lb):
    T, C2 = pc.shape
    C = C2 // 2
    tm = _tile(T, _ROW_TILE // 2, HALO)
    per = tm // HALO

    def body(pc_ref, ph_ref, w_ref, cb_ref, lg_ref, lb_ref, z_ref, c_ref, ubuf, vbuf):
        i = pl.program_id(0)
        _, _, u_cur = _glu(pc_ref[...], C)
        _, _, u_prev = _glu(ph_ref[...], C)
        ubuf[0:HALO, :] = jnp.where(i > 0, u_prev, 0.0)
        ubuf[HALO:HALO + tm, :] = u_cur
        ubuf[HALO + tm:, :] = jnp.zeros((_SUBLANES, C), F32)
        acc = jnp.zeros((tm, C), F32)
        for shift, taps in _tap_groups([HALO - (CONV_WIDTH - 1) + k for k in range(CONV_WIDTH)]):
            v = None
            for k, base in taps:
                term = w_ref[k:k + 1, :] * ubuf[base:base + tm + _SUBLANES, :]
                v = term if v is None else v + term
            if shift == 0:
                acc = acc + v[:tm]
            else:
                vbuf[...] = v
                acc = acc + vbuf[shift:shift + tm, :]
        z = acc + cb_ref[...]
        z_ref[...] = z
        zc = z - jnp.mean(z, axis=-1, keepdims=True)
        y = zc * lax.rsqrt(jnp.mean(zc * zc, axis=-1, keepdims=True) + EPS) * lg_ref[...] + lb_ref[...]
        c_ref[...] = (y * _sig(y)).astype(c_ref.dtype)

    return pl.pallas_call(
        body, name="conv_fwd", grid=(T // tm,),
        in_specs=[_row_spec(tm, C2), pl.BlockSpec((HALO, C2), lambda i: (jnp.maximum(i * per - 1, 0), 0)),
                  _full_spec((HALO, C)), _full_spec((1, C)), _full_spec((1, C)), _full_spec((1, C))],
        out_specs=[_row_spec(tm, C), _row_spec(tm, C)],
        out_shape=[jax.ShapeDtypeStruct((T, C), F32), jax.ShapeDtypeStruct((T, C), _MXU_DTYPE)],
        scratch_shapes=[pltpu.VMEM((tm + HALO + _SUBLANES, C), F32), pltpu.VMEM((tm + _SUBLANES, C), F32)],
        compiler_params=_params(("parallel",), 3 * _nbytes((tm, C2), F32), 10 * _nbytes((tm, C), F32)),
    )(pc, pc, w, cb, lg, lb)


def _conv_bwd(pc, z, dc, w, lg, lb):
    T, C2 = pc.shape
    C = C2 // 2
    tm = _tile(T, _ROW_TILE // 2, HALO)
    per = tm // HALO
    n = T // tm
    last_halo = T // HALO - 1

    def body(pc_ref, ph_ref, z_ref, zn_ref, dc_ref, dcn_ref, w_ref, lg_ref, lb_ref,
             dpc_ref, dw_ref, dcb_ref, dlg_ref, dlb_ref, ubuf, dzbuf, vbuf, sbuf):
        i = pl.program_id(0)
        _init_acc([dw_ref, dcb_ref, dlg_ref, dlb_ref])
        g, b = lg_ref[...], lb_ref[...]

        def ln_swish_bwd(zv, dcv):
            zc = zv - jnp.mean(zv, axis=-1, keepdims=True)
            r = lax.rsqrt(jnp.mean(zc * zc, axis=-1, keepdims=True) + EPS)
            zh = zc * r
            y = zh * g + b
            s = _sig(y)
            dy = dcv * (s * (1.0 + y * (1.0 - s)))
            dyg = dy * g
            dz = r * (dyg - jnp.mean(dyg, axis=-1, keepdims=True) - zh * jnp.mean(dyg * zh, axis=-1, keepdims=True))
            return dz, dy, zh

        dz_c, dy_c, zh_c = ln_swish_bwd(z_ref[...], dc_ref[...])
        dz_n, _, _ = ln_swish_bwd(zn_ref[...], dcn_ref[...])
        dlg_ref[...] += _colsum(dy_c * zh_c)
        dlb_ref[...] += _colsum(dy_c)
        dcb_ref[...] += _colsum(dz_c)
        dzbuf[0:tm, :] = dz_c
        dzbuf[tm:tm + HALO, :] = jnp.where(i < n - 1, dz_n, 0.0)
        a, s, u_cur = _glu(pc_ref[...], C)
        _, _, u_prev = _glu(ph_ref[...], C)
        ubuf[0:HALO, :] = jnp.where(i > 0, u_prev, 0.0)
        ubuf[HALO:HALO + tm, :] = u_cur
        ubuf[HALO + tm:, :] = jnp.zeros((_SUBLANES, C), F32)
        du = jnp.zeros((tm, C), F32)
        for shift, taps in _tap_groups([CONV_WIDTH - 1 - k for k in range(CONV_WIDTH)]):
            v = None
            for k, base in taps:
                term = w_ref[k:k + 1, :] * dzbuf[base:base + tm + _SUBLANES, :]
                v = term if v is None else v + term
            if shift == 0:
                du = du + v[:tm]
            else:
                vbuf[...] = v
                du = du + vbuf[shift:shift + tm, :]
        for shift, taps in _tap_groups([HALO - (CONV_WIDTH - 1) + k for k in range(CONV_WIDTH)]):
            if shift:
                sbuf[...] = ubuf[shift:shift + tm + HALO, :]
            src = sbuf if shift else ubuf
            for k, base in taps:
                dw_ref[k:k + 1, :] += _colsum(dz_c * src[base:base + tm, :])
        dpc_ref[:, :C] = (du * s).astype(dpc_ref.dtype)
        dpc_ref[:, C:] = (du * a * s * (1.0 - s)).astype(dpc_ref.dtype)

    nxt = lambda i: (jnp.minimum((i + 1) * per, last_halo), 0)
    vec = _full_spec((1, C))
    return pl.pallas_call(
        body, name="conv_bwd", grid=(n,),
        in_specs=[_row_spec(tm, C2), pl.BlockSpec((HALO, C2), lambda i: (jnp.maximum(i * per - 1, 0), 0)),
                  _row_spec(tm, C), pl.BlockSpec((HALO, C), nxt), _row_spec(tm, C), pl.BlockSpec((HALO, C), nxt),
                  _full_spec((HALO, C)), vec, vec],
        out_specs=[_row_spec(tm, C2), _full_spec((HALO, C)), vec, vec, vec],
        out_shape=[jax.ShapeDtypeStruct((T, C2), _MXU_DTYPE), jax.ShapeDtypeStruct((HALO, C), F32)]
                  + [jax.ShapeDtypeStruct((1, C), F32)] * 3,
        scratch_shapes=[pltpu.VMEM((tm + HALO + _SUBLANES, C), F32), pltpu.VMEM((tm + HALO + _SUBLANES, C), F32),
                        pltpu.VMEM((tm + _SUBLANES, C), F32), pltpu.VMEM((tm + HALO, C), F32)],
        compiler_params=_params(("arbitrary",), 5 * _nbytes((tm, C2), F32), 16 * _nbytes((tm, C), F32)),
    )(pc, pc, z, z, dc, dc, w, lg, lb)


def _lat_norm_fwd(pl_, gq, gkv, QL, KVL):
    T, W = pl_.shape
    tm = _tile(T, _ROW_TILE, _PACKED_ROWS)

    def body(p_ref, gq_ref, gkv_ref, cq_ref, ckv_ref):
        def norm(xv, gv):
            return xv * lax.rsqrt(jnp.mean(xv * xv, axis=-1, keepdims=True) + EPS) * gv
        cq_ref[...] = norm(p_ref[:, :QL], gq_ref[...]).astype(cq_ref.dtype)
        ckv_ref[...] = norm(p_ref[:, QL:QL + KVL], gkv_ref[...]).astype(ckv_ref.dtype)

    return pl.pallas_call(
        body, name="lat_norm_fwd", grid=(T // tm,),
        in_specs=[_row_spec(tm, W), _full_spec((1, QL)), _full_spec((1, KVL))],
        out_specs=[_row_spec(tm, QL), _row_spec(tm, KVL)],
        out_shape=[jax.ShapeDtypeStruct((T, QL), _MXU_DTYPE), jax.ShapeDtypeStruct((T, KVL), _MXU_DTYPE)],
        compiler_params=_params(("parallel",), 2 * _nbytes((tm, W), F32), 4 * _nbytes((tm, W), F32)),
    )(pl_, gq, gkv)


def _lat_norm_bwd(pl_, gq, gkv, dcq, dckv, drope, QL, KVL):
    T, W = pl_.shape
    tm = _tile(T, _ROW_TILE, _PACKED_ROWS)

    def body(p_ref, gq_ref, gkv_ref, dcq_ref, dckv_ref, dr_ref, dp_ref, dgq_ref, dgkv_ref):
        _init_acc([dgq_ref, dgkv_ref])

        def bwd(xv, gv, dv, dg_ref):
            r = lax.rsqrt(jnp.mean(xv * xv, axis=-1, keepdims=True) + EPS)
            xh = xv * r
            dxh = dv * gv
            dg_ref[...] += _colsum(dv * xh)
            return r * (dxh - xh * jnp.mean(dxh * xh, axis=-1, keepdims=True))

        dp_ref[:, :QL] = bwd(p_ref[:, :QL], gq_ref[...], dcq_ref[...], dgq_ref).astype(dp_ref.dtype)
        dp_ref[:, QL:QL + KVL] = bwd(p_ref[:, QL:QL + KVL], gkv_ref[...], dckv_ref[...], dgkv_ref).astype(dp_ref.dtype)
        dp_ref[:, QL + KVL:] = dr_ref[...].astype(dp_ref.dtype)

    return pl.pallas_call(
        body, name="lat_norm_bwd", grid=(T // tm,),
        in_specs=[_row_spec(tm, W), _full_spec((1, QL)), _full_spec((1, KVL)), _row_spec(tm, QL), _row_spec(tm, KVL),
                  _row_spec(tm, HEAD_W)],
        out_specs=[_row_spec(tm, W), _full_spec((1, QL)), _full_spec((1, KVL))],
        out_shape=[jax.ShapeDtypeStruct((T, W), _MXU_DTYPE), jax.ShapeDtypeStruct((1, QL), F32),
                   jax.ShapeDtypeStruct((1, KVL), F32)],
        compiler_params=_params(("arbitrary",), 4 * _nbytes((tm, W), F32), 6 * _nbytes((tm, W), F32)),
    )(pl_, gq, gkv, dcq, dckv, drope)


def _norm_rope(xv, gv, ct, sa, sb):
    r = lax.rsqrt(jnp.sum(xv * xv, axis=-1, keepdims=True) * (1.0 / QK_DIM) + EPS)
    xn = xv * r * gv
    half = ROPE_DIM // 2
    return xn * ct + pltpu.roll(xn, HEAD_W - half, 1) * sa + pltpu.roll(xn, half, 1) * sb


def _norm_rope_bwd(xv, gv, ct, sa, sb, dout):
    half = ROPE_DIM // 2
    dxn = dout * ct + pltpu.roll(dout * sa, half, 1) + pltpu.roll(dout * sb, HEAD_W - half, 1)
    r = lax.rsqrt(jnp.sum(xv * xv, axis=-1, keepdims=True) * (1.0 / QK_DIM) + EPS)
    xh = xv * r
    dxh = dxn * gv
    dx = r * (dxh - xh * (jnp.sum(dxh * xh, axis=-1, keepdims=True) * (1.0 / QK_DIM)))
    return dx, _colsum(dxn * xh)


def _qk_prep_fwd(q_raw, k_raw, pl_, rope_cb, ct, sa, sb, gq, gk):
    T, HW = q_raw.shape
    H = HW // HEAD_W
    tm = _tile(T, _ROW_TILE, _PACKED_ROWS)

    def body(q_ref, k_ref, r_ref, ct_ref, sa_ref, sb_ref, gq_ref, gk_ref, qn_ref, kn_ref):
        ctv, sav, sbv, rv = ct_ref[...], sa_ref[...], sb_ref[...], r_ref[...]
        for h in range(H):
            hs = slice(h * HEAD_W, (h + 1) * HEAD_W)
            qn_ref[:, hs] = _norm_rope(q_ref[:, hs], gq_ref[...], ctv, sav, sbv).astype(qn_ref.dtype)
            kn_ref[:, hs] = _norm_rope(k_ref[:, hs] + rv, gk_ref[...], ctv, sav, sbv).astype(kn_ref.dtype)

    tb = _row_spec(tm, HEAD_W)
    gb = _full_spec((1, HEAD_W))
    return pl.pallas_call(
        body, name="qk_prep_fwd", grid=(T // tm,),
        in_specs=[_row_spec(tm, HW), _row_spec(tm, HW), _row_spec(tm, HEAD_W, rope_cb), tb, tb, tb, gb, gb],
        out_specs=[_row_spec(tm, HW), _row_spec(tm, HW)],
        out_shape=[jax.ShapeDtypeStruct((T, HW), _MXU_DTYPE)] * 2,
        compiler_params=_params(("parallel",), 3 * _nbytes((tm, HW), F32) + 4 * _nbytes((tm, HEAD_W), F32), 8 * _nbytes((tm, HEAD_W), F32)),
    )(q_raw, k_raw, pl_, ct, sa, sb, gq, gk)


def _qk_prep_bwd(q_raw, k_raw, pl_, rope_cb, ct, sa, sb, gq, gk, dqn, dkn):
    T, HW = q_raw.shape
    H = HW // HEAD_W
    tm = _tile(T, _ROW_TILE, _PACKED_ROWS)

    def body(q_ref, k_ref, r_ref, ct_ref, sa_ref, sb_ref, gq_ref, gk_ref, dqn_ref, dkn_ref,
             dq_ref, dk_ref, dr_ref, dgq_ref, dgk_ref):
        _init_acc([dgq_ref, dgk_ref])
        ctv, sav, sbv, rv = ct_ref[...], sa_ref[...], sb_ref[...], r_ref[...]
        dr = dgq_sum = dgk_sum = None
        for h in range(H):
            hs = slice(h * HEAD_W, (h + 1) * HEAD_W)
            dq, dgq = _norm_rope_bwd(q_ref[:, hs], gq_ref[...], ctv, sav, sbv, dqn_ref[:, hs])
            dk, dgk = _norm_rope_bwd(k_ref[:, hs] + rv, gk_ref[...], ctv, sav, sbv, dkn_ref[:, hs])
            dq_ref[:, hs] = dq.astype(dq_ref.dtype)
            dk_ref[:, hs] = dk.astype(dk_ref.dtype)
            dr = dk if h == 0 else dr + dk
            dgq_sum = dgq if h == 0 else dgq_sum + dgq
            dgk_sum = dgk if h == 0 else dgk_sum + dgk
        dr_ref[...] = dr
        dgq_ref[...] += dgq_sum
        dgk_ref[...] += dgk_sum

    tb = _row_spec(tm, HEAD_W)
    gb = _full_spec((1, HEAD_W))
    hb = _row_spec(tm, HW)
    return pl.pallas_call(
        body, name="qk_prep_bwd", grid=(T // tm,),
        in_specs=[hb, hb, _row_spec(tm, HEAD_W, rope_cb), tb, tb, tb, gb, gb, hb, hb],
        out_specs=[hb, hb, tb, gb, gb],
        out_shape=[jax.ShapeDtypeStruct((T, HW), _MXU_DTYPE)] * 2 + [jax.ShapeDtypeStruct((T, HEAD_W), F32)]
                  + [jax.ShapeDtypeStruct((1, HEAD_W), F32)] * 2,
        compiler_params=_params(("arbitrary",), 5 * _nbytes((tm, HW), F32) + 5 * _nbytes((tm, HEAD_W), F32), 12 * _nbytes((tm, HEAD_W), F32)),
    )(q_raw, k_raw, pl_, ct, sa, sb, gq, gk, dqn, dkn)


_NEG = -1e30
_LOG2E = 1.4426950408889634
_SCORE_C = QK_DIM ** -0.5 * _LOG2E


def _pieces(t, rb, diag):
    assert CHUNK % rb == 0 and rb % _SUBLANES == 0
    out = []
    for r in range(t // rb):
        lo = (r * rb // CHUNK) * CHUNK if diag else 0
        for c in range(t // _LANES):
            cut = None if (c + 1) * _LANES <= lo else max(lo - c * _LANES, 0)
            out.append((slice(r * rb, (r + 1) * rb), slice(c * _LANES, (c + 1) * _LANES), c, cut))
    return out


def _groups(x):
    return x.reshape(x.shape[0] // _SUBLANES, _SUBLANES, x.shape[1])


def _all_sublanes(x8, op):
    return jnp.broadcast_to(op(x8, axis=0, keepdims=True), x8.shape)


def _lane_ge(rb, cut):
    return lax.broadcasted_iota(jnp.int32, (rb, _LANES), 1) >= cut


def _attn_fwd(q, k, v):
    T, HW = q.shape
    H = HW // HEAD_W
    t = _tile(T, _ATT_TILE)
    rb = min(_ATT_ROWS, t)
    nc = t // _LANES

    hp = _ATT_HEADS if H % _ATT_HEADS == 0 else 1
    W = hp * HEAD_W

    def body(q_ref, k_ref, v_ref, o_ref, lse_ref, s_s, p_s, m_s, l_s, acc_s):
        i = pl.program_id(1)
        m_s[...] = jnp.full_like(m_s, _NEG)
        l_s[...] = jnp.zeros_like(l_s)
        acc_s[...] = jnp.zeros_like(acc_s)

        def head_step(hh, rows, diag):
            hs = slice(hh * HEAD_W, (hh + 1) * HEAD_W)
            s_s[hh] = _dot(k_ref[rows, hs], q_ref[:, hs], 'nt')
            pieces = _pieces(t, rb, diag)

            def scores(rs, cs, cut):
                sb = s_s[hh, rs, cs]
                return jnp.where(_lane_ge(rb, cut), sb, _NEG) if cut else sb

            mx = [None] * nc
            for rs, cs, c, cut in pieces:
                if cut is not None:
                    g = jnp.max(_groups(scores(rs, cs, cut)), axis=0)
                    mx[c] = g if mx[c] is None else jnp.maximum(mx[c], g)
            m_new, alpha = [], []
            for c in range(nc):
                cs = slice(c * _LANES, (c + 1) * _LANES)
                m_prev = m_s[hh, :, cs]
                m_new.append(jnp.maximum(m_prev, _all_sublanes(mx[c], jnp.max)))
                alpha.append(jnp.exp2((m_prev - m_new[c]) * _SCORE_C))
                m_s[hh, :, cs] = m_new[c]
            lsum = [jnp.zeros((_SUBLANES, _LANES), F32)] * nc
            for rs, cs, c, cut in pieces:
                if cut is None:
                    p_s[hh, rs, cs] = jnp.zeros((rb, _LANES), p_s.dtype)
                    continue
                p = jnp.exp2((scores(rs, cs, cut) - jnp.tile(m_new[c], (rb // _SUBLANES, 1))) * _SCORE_C)
                lsum[c] = lsum[c] + jnp.sum(_groups(p), axis=0)
                p_s[hh, rs, cs] = p.astype(p_s.dtype)
            for c in range(nc):
                cs = slice(c * _LANES, (c + 1) * _LANES)
                l_s[hh, :, cs] = alpha[c] * l_s[hh, :, cs] + _all_sublanes(lsum[c], jnp.sum)
            a = jnp.tile(jnp.concatenate(alpha, axis=1), (HEAD_W // _SUBLANES, 1))
            acc_s[hh] = a * acc_s[hh] + _dot(v_ref[rows, hs], p_s[hh], 'tn')

        def step(j, diag):
            rows = pl.ds(pl.multiple_of(j * t, t), t)
            for hh in range(hp):
                head_step(hh, rows, diag)

        def loop_body(j, carry):
            step(j, False)
            return carry

        lax.fori_loop(0, i, loop_body, 0)
        step(i, True)
        for hh in range(hp):
            l = l_s[hh]
            o_t = acc_s[hh] / jnp.tile(l, (HEAD_W // _SUBLANES, 1))
            o_ref[:, hh * HEAD_W:(hh + 1) * HEAD_W] = o_t.T.astype(o_ref.dtype)
            lse_ref[hh * _SUBLANES:(hh + 1) * _SUBLANES, :] = m_s[hh] * _SCORE_C + jnp.log(l) * _LOG2E

    qb = pl.BlockSpec((t, W), lambda h, i: (i, h))
    kb = pl.BlockSpec((T, W), lambda h, i: (0, h))
    st = pltpu.VMEM((hp, _SUBLANES, t), F32)
    return pl.pallas_call(
        body, name="attn_fwd", grid=(H // hp, T // t),
        in_specs=[qb, kb, kb], out_specs=[qb, pl.BlockSpec((hp * _SUBLANES, t), lambda h, i: (h, i))],
        out_shape=[jax.ShapeDtypeStruct((T, HW), _MXU_DTYPE), jax.ShapeDtypeStruct((H * _SUBLANES, T), F32)],
        scratch_shapes=[pltpu.VMEM((hp, t, t), F32), pltpu.VMEM((hp, t, t), _MXU_DTYPE), st, st, pltpu.VMEM((hp, HEAD_W, t), F32)],
        compiler_params=_params(("parallel", "arbitrary"), 2 * _nbytes((T, W), _MXU_DTYPE) + 3 * _nbytes((t, W), F32),
                                4 * hp * _nbytes((t, t), F32)),
    )(q, k, v)


def _attn_bwd(q, k, v, o, do, lse2):
    T, HW = q.shape
    H = HW // HEAD_W
    t = _tile(T, _ATT_TILE)
    n = T // t
    rb = min(_ATT_ROWS, t)
    scale = QK_DIM ** -0.5

    hp = _ATT_HEADS if H % _ATT_HEADS == 0 else 1
    W = hp * HEAD_W

    def body(k_ref, v_ref, q_ref, o_ref, do_ref, lse_ref, dq_ref, dk_ref, dv_ref,
             s_s, dp_s, p_s, ds_s, dk_s, dv_s, delta_s, dqt_s):
        j = pl.program_id(1)

        @pl.when(j == 0)
        def _():
            dqt_s[...] = jnp.zeros_like(dqt_s)
            for hh in range(hp):
                hs = slice(hh * HEAD_W, (hh + 1) * HEAD_W)
                for b in range(n):
                    rs = slice(b * t, (b + 1) * t)
                    d = jnp.sum(do_ref[rs, hs].astype(F32) * o_ref[rs, hs].astype(F32), axis=1, keepdims=True)
                    delta_s[hh, :, rs] = jnp.broadcast_to(d, (t, HEAD_W)).T[0:_SUBLANES, :]

        dk_s[...] = jnp.zeros_like(dk_s)
        dv_s[...] = jnp.zeros_like(dv_s)

        def head_step(hh, base, diag):
            hs = slice(hh * HEAD_W, (hh + 1) * HEAD_W)
            kv, vv = k_ref[:, hs], v_ref[:, hs]
            qv, dov = q_ref[pl.ds(base, t), hs], do_ref[pl.ds(base, t), hs]
            s_s[hh] = _dot(kv, qv, 'nt')
            dp_s[hh] = _dot(vv, dov, 'nt')
            for rs, cs, c, cut in _pieces(t, rb, diag):
                if cut is None:
                    p_s[hh, rs, cs] = jnp.zeros((rb, _LANES), p_s.dtype)
                    ds_s[hh, rs, cs] = jnp.zeros((rb, _LANES), ds_s.dtype)
                    continue
                lanes = pl.ds(pl.multiple_of(base + c * _LANES, _LANES), _LANES)
                lse = jnp.tile(lse_ref[hh * _SUBLANES:(hh + 1) * _SUBLANES, lanes], (rb // _SUBLANES, 1))
                dl = jnp.tile(delta_s[hh, :, lanes], (rb // _SUBLANES, 1))
                p = jnp.exp2(s_s[hh, rs, cs] * _SCORE_C - lse)
                if cut:
                    p = jnp.where(_lane_ge(rb, cut), p, 0.0)
                p_s[hh, rs, cs] = p.astype(p_s.dtype)
                ds_s[hh, rs, cs] = (p * (dp_s[hh, rs, cs] - dl)).astype(ds_s.dtype)
            dsv = ds_s[hh]
            dv_s[hh] += _dot(p_s[hh], dov, 'nn')
            dk_s[hh] += _dot(dsv, qv, 'nn')
            dqt_s[hh, :, pl.ds(base, t)] += _dot(kv, dsv, 'tn')

        def step(i, diag):
            base = pl.multiple_of(i * t, t)
            for hh in range(hp):
                head_step(hh, base, diag)

        def loop_body(i, carry):
            step(i, False)
            return carry

        step(j, True)
        lax.fori_loop(j + 1, n, loop_body, 0)
        for hh in range(hp):
            hs = slice(hh * HEAD_W, (hh + 1) * HEAD_W)
            dk_ref[:, hs] = dk_s[hh] * scale
            dv_ref[:, hs] = dv_s[hh].astype(dv_ref.dtype)

        @pl.when(j == n - 1)
        def _():
            for hh in range(hp):
                for b in range(n):
                    rs = slice(b * t, (b + 1) * t)
                    dq_ref[rs, hh * HEAD_W:(hh + 1) * HEAD_W] = dqt_s[hh, :, rs].T * scale

    jb = pl.BlockSpec((t, W), lambda h, j: (j, h))
    fb = pl.BlockSpec((T, W), lambda h, j: (0, h))
    sc = pltpu.VMEM((hp, t, HEAD_W), F32)
    return pl.pallas_call(
        body, name="attn_bwd", grid=(H // hp, n),
        in_specs=[jb, jb, fb, fb, fb, pl.BlockSpec((hp * _SUBLANES, T), lambda h, j: (h, 0))], out_specs=[fb, jb, jb],
        out_shape=[jax.ShapeDtypeStruct((T, HW), F32), jax.ShapeDtypeStruct((T, HW), F32),
                   jax.ShapeDtypeStruct((T, HW), _MXU_DTYPE)],
        scratch_shapes=[pltpu.VMEM((hp, t, t), F32), pltpu.VMEM((hp, t, t), F32), pltpu.VMEM((hp, t, t), _MXU_DTYPE),
                        pltpu.VMEM((hp, t, t), _MXU_DTYPE), sc, sc, pltpu.VMEM((hp, _SUBLANES, T), F32),
                        pltpu.VMEM((hp, HEAD_W, T), F32)],
        compiler_params=_params(("parallel", "arbitrary"), 3 * _nbytes((T, W), _MXU_DTYPE) + _nbytes((T, W), F32)
                                + 4 * _nbytes((t, W), F32), hp * (4 * _nbytes((t, t), F32) + _nbytes((T, HEAD_W), F32))),
    )(k, v, q, o, do, lse2)


def _gate_fwd(pg, gb, yc, ym):
    T, D = yc.shape
    tm = _tile(T, _ROW_TILE, _PACKED_ROWS)

    def body(g0_ref, g1_ref, gb_ref, yc_ref, ym_ref, y_ref):
        s0 = _sig(g0_ref[...] + gb_ref[0:1, :])
        s1 = _sig(g1_ref[...] + gb_ref[1:2, :])
        y_ref[...] = (s0 * yc_ref[...] + s1 * ym_ref[...]).astype(y_ref.dtype)

    return pl.pallas_call(
        body, name="gate_fwd", grid=(T // tm,),
        in_specs=[_row_spec(tm, D, 0), _row_spec(tm, D, 1), _full_spec((2, D)), _row_spec(tm, D), _row_spec(tm, D)],
        out_specs=_row_spec(tm, D),
        out_shape=jax.ShapeDtypeStruct((T, D), _MXU_DTYPE),
        compiler_params=_params(("parallel",), 5 * _nbytes((tm, D), F32), 4 * _nbytes((tm, D), F32)),
    )(pg, pg, gb, yc, ym)


def _gate_bwd(pg, gb, yc, ym, dy):
    T, D = yc.shape
    tm = _tile(T, _ROW_TILE, _PACKED_ROWS)

    def body(g0_ref, g1_ref, gb_ref, yc_ref, ym_ref, dy_ref, dyc_ref, dym_ref, dpg_ref, dgb_ref):
        _init_acc([dgb_ref])
        dyv = dy_ref[...]
        s0 = _sig(g0_ref[...] + gb_ref[0:1, :])
        s1 = _sig(g1_ref[...] + gb_ref[1:2, :])
        dyc_ref[...] = (dyv * s0).astype(dyc_ref.dtype)
        dym_ref[...] = (dyv * s1).astype(dym_ref.dtype)
        d0 = dyv * yc_ref[...] * s0 * (1.0 - s0)
        d1 = dyv * ym_ref[...] * s1 * (1.0 - s1)
        dpg_ref[:, :D] = d0.astype(dpg_ref.dtype)
        dpg_ref[:, D:] = d1.astype(dpg_ref.dtype)
        dgb_ref[0:1, :] += _colsum(d0)
        dgb_ref[1:2, :] += _colsum(d1)

    return pl.pallas_call(
        body, name="gate_bwd", grid=(T // tm,),
        in_specs=[_row_spec(tm, D, 0), _row_spec(tm, D, 1), _full_spec((2, D)), _row_spec(tm, D), _row_spec(tm, D),
                  _row_spec(tm, D)],
        out_specs=[_row_spec(tm, D), _row_spec(tm, D), _row_spec(tm, 2 * D), _full_spec((2, D))],
        out_shape=[jax.ShapeDtypeStruct((T, D), _MXU_DTYPE)] * 2 + [jax.ShapeDtypeStruct((T, 2 * D), _MXU_DTYPE),
                                                                   jax.ShapeDtypeStruct((2, D), F32)],
        compiler_params=_params(("arbitrary",), 7 * _nbytes((tm, D), F32), 8 * _nbytes((tm, D), F32)),
    )(pg, pg, gb, yc, ym, dy)


def _place():
    return lax.axis_index("x"), lax.axis_index("y"), lax.axis_index("c")


def _all_gather(name, xs):
    R, C = xs.shape
    MESH = pl.DeviceIdType.MESH

    def body(x_ref, out_ref, send_sems, recv_sems, local_sem):
        x, y, c = _place()
        me, sibling = (x, y, c), (x, y, 1 - c)
        chips = [(1 - x, y), (x, 1 - y), (1 - x, 1 - y)]

        def rows(px, py, pc):
            return out_ref.at[4 * px + 2 * py + pc]

        def copy(k, block, to, src=None):
            return pltpu.make_async_remote_copy(
                src_ref=rows(*block) if src is None else src, dst_ref=rows(*block),
                send_sem=send_sems.at[k], recv_sem=recv_sems.at[k], device_id=to, device_id_type=MESH)

        mine = pltpu.make_async_copy(x_ref, rows(*me), local_sem)
        mine.start()
        first = [copy(0, me, sibling, src=x_ref)]
        first += [copy(1 + j, me, (*chip, c), src=x_ref) for j, chip in enumerate(chips)]
        for cp in first:
            cp.start()
        passed = [copy(4 + j, (*chip, c), sibling) for j, chip in enumerate(chips)]
        for j, chip in enumerate(chips):
            copy(1 + j, (*chip, c), me).wait_recv()
            passed[j].start()
        copy(0, sibling, me).wait_recv()
        for j, chip in enumerate(chips):
            copy(4 + j, (*chip, 1 - c), me).wait_recv()
        for cp in first + passed:
            cp.wait_send()
        mine.wait()

    return pl.pallas_call(
        body, name=name,
        out_shape=jax.ShapeDtypeStruct((N_DEV, R, C), xs.dtype),
        in_specs=[pl.BlockSpec(memory_space=pl.ANY)], out_specs=pl.BlockSpec(memory_space=pl.ANY),
        scratch_shapes=[pltpu.SemaphoreType.DMA((7,)), pltpu.SemaphoreType.DMA((7,)), pltpu.SemaphoreType.DMA],
    )(xs)


_HBM_SPEC = pl.BlockSpec(memory_space=pltpu.HBM)
_SEM_SPEC = pl.BlockSpec(memory_space=pltpu.SEMAPHORE)
_DATAFLOW = pltpu.SideEffectType.DATAFLOW_SIDE_EFFECTING


def _peers():
    x, y, c = _place()
    out = []
    for k in range(1, N_DEV):
        px = 1 - x if k & 4 else x
        py = 1 - y if k & 2 else y
        pc = 1 - c if k & 1 else c
        out.append((k - 1, (px, py, pc), 4 * px + 2 * py + pc))
    return 4 * x + 2 * y + c, out


def _exchange_copies(x_ref, land_ref, send_sems, recv_sems, scatter):
    me, peers = _peers()
    return [pltpu.make_async_remote_copy(
        src_ref=x_ref.at[idx] if scatter else x_ref, dst_ref=land_ref.at[me],
        send_sem=send_sems.at[k], recv_sem=recv_sems.at[k], device_id=dev, device_id_type=pl.DeviceIdType.MESH)
        for k, dev, idx in peers]


def _exchange_start(name, xs, land, scatter):
    def body(x_ref, land_ref, send_sems, recv_sems, x_thru, land_thru, token):
        for cp in _exchange_copies(x_ref, land_ref, send_sems, recv_sems, scatter):
            cp.start()
        token[...] = jnp.zeros_like(token)

    sems = pltpu.SemaphoreType.DMA((N_DEV - 1,))
    return pl.pallas_call(
        body, name=name,
        out_shape=(sems, sems, pltpu.HBM(xs.shape, xs.dtype), pltpu.HBM(land.shape, land.dtype),
                   jax.ShapeDtypeStruct((_SUBLANES, _LANES), F32)),
        in_specs=(_HBM_SPEC, _HBM_SPEC),
        out_specs=(_SEM_SPEC, _SEM_SPEC, _HBM_SPEC, _HBM_SPEC, pl.BlockSpec(memory_space=pltpu.VMEM)),
        input_output_aliases={0: 2, 1: 3},
        compiler_params=pltpu.CompilerParams(has_side_effects=_DATAFLOW),
    )(pltpu.with_memory_space_constraint(xs, pltpu.HBM), pltpu.with_memory_space_constraint(land, pltpu.HBM))


def _exchange_wait(name, send_sems, recv_sems, xs, land, after, scatter):
    def body(x_ref, land_ref, send_sems, recv_sems, after_ref, x_dead, got_ref):
        for cp in _exchange_copies(x_ref, land_ref, send_sems, recv_sems, scatter):
            cp.wait_send()
            cp.wait_recv()

    return pl.pallas_call(
        body, name=name,
        out_shape=(pltpu.HBM(xs.shape, xs.dtype), pltpu.HBM(land.shape, land.dtype)),
        in_specs=(_HBM_SPEC, _HBM_SPEC, _SEM_SPEC, _SEM_SPEC, pl.BlockSpec(memory_space=pl.ANY)),
        out_specs=(_HBM_SPEC, _HBM_SPEC), input_output_aliases={0: 0, 1: 1},
        compiler_params=pltpu.CompilerParams(has_side_effects=_DATAFLOW),
    )(xs, land, send_sems, recv_sems, after)[1]


def _own_slot(block, me):
    land = lax.empty((N_DEV,) + block.shape, block.dtype)
    return lax.dynamic_update_slice(land, block[None], (me,) + (0,) * block.ndim)


def _sum_blocks(name, parts, scale=None):
    n, R, C = parts.shape
    tr = _tile(R, _PACK_ROW_MULT, _PACKED_ROWS)

    def body(p_ref, o_ref):
        acc = p_ref[0].astype(F32)
        for s in range(1, n):
            acc = acc + p_ref[s].astype(F32)
        if scale is not None:
            acc = acc * scale
        o_ref[...] = acc

    return pl.pallas_call(
        body, name=name, grid=(R // tr,),
        in_specs=[pl.BlockSpec((n, tr, C), lambda i: (0, i, 0))], out_specs=pl.BlockSpec((tr, C), lambda i: (i, 0)),
        out_shape=jax.ShapeDtypeStruct((R, C), F32),
        compiler_params=_params(("parallel",), _nbytes((n, tr, C), parts.dtype) + _nbytes((tr, C), F32), 2 * _nbytes((tr, C), F32)),
    )(parts)


def _adamw_math(wv, gv, mv, vv):
    mv = ADAM_B1 * mv + (1.0 - ADAM_B1) * gv
    vv = ADAM_B2 * vv + (1.0 - ADAM_B2) * (gv * gv)
    m_hat = mv / (1.0 - ADAM_B1 ** ADAM_STEP)
    v_hat = vv / (1.0 - ADAM_B2 ** ADAM_STEP)
    return -ADAM_LR * (m_hat / (jnp.sqrt(v_hat) + ADAM_EPS) + ADAM_WD * wv), mv, vv


def _adamw_layer(l, w, g, m, v, outs, after):
    depth, R, C = w.shape
    tr = _tile(R, _ROW_TILE, _SUBLANES)

    def body(w_ref, g_ref, m_ref, v_ref, after_ref, *rest):
        go_ref, d_ref, nm_ref, nv_ref = rest[-4:]
        gv = g_ref[...]
        go_ref[...] = gv
        d_ref[...], nm_ref[...], nv_ref[...] = _adamw_math(w_ref[...], gv, m_ref[...], v_ref[...])

    lay = pl.BlockSpec((None, tr, C), lambda i: (l, i, 0))
    shape = jax.ShapeDtypeStruct(w.shape, F32)
    return pl.pallas_call(
        body, name="adamw_layer", grid=(R // tr,),
        in_specs=[lay, pl.BlockSpec((tr, C), lambda i: (i, 0)), lay, lay] + [pl.BlockSpec(memory_space=pl.ANY)] * (5 if outs else 1),
        out_specs=[lay] * 4, out_shape=[shape] * 4,
        input_output_aliases={5 + k: k for k in range(4)} if outs else {},
        compiler_params=_params(("parallel",), 8 * _nbytes((tr, C), F32), 4 * _nbytes((tr, C), F32)),
    )(w, g, m, v, after, *(outs or ()))


def _adamw(w, g, m, v):
    shape = w.shape
    w2, g2, m2, v2 = (t.reshape(-1, shape[-1]) for t in (w, g, m, v))
    R, C = w2.shape
    tr = _tile(R, _ROW_TILE, _SUBLANES)

    def body(w_ref, g_ref, m_ref, v_ref, d_ref, nm_ref, nv_ref):
        d_ref[...], nm_ref[...], nv_ref[...] = _adamw_math(w_ref[...], g_ref[...], m_ref[...], v_ref[...])

    spec = pl.BlockSpec((tr, C), lambda i: (i, 0))
    outs = pl.pallas_call(
        body, name="adamw", grid=(R // tr,),
        in_specs=[spec] * 4, out_specs=[spec] * 3,
        out_shape=[jax.ShapeDtypeStruct((R, C), F32)] * 3,
        compiler_params=_params(("parallel",), 7 * _nbytes((tr, C), F32), 4 * _nbytes((tr, C), F32)),
    )(w2, g2, m2, v2)
    return tuple(o.reshape(shape) for o in outs)


def _pack_rows(parts, lead=()):
    out, spans, r0 = [], [], 0
    unit = _PACKED_ROWS * _PACK_W
    for p in parts:
        flat = p.reshape(lead + (-1,))
        size = flat.shape[-1]
        pad = (-size) % unit
        if pad:
            flat = jnp.pad(flat, [(0, 0)] * len(lead) + [(0, pad)])
        rows = (size + pad) // _PACK_W
        out.append(flat.reshape(lead + (rows, _PACK_W)))
        spans.append((r0, rows, p.shape[len(lead):]))
        r0 += rows
    pad = (-r0) % _PACK_ROW_MULT
    if pad:
        out.append(jnp.zeros(lead + (pad, _PACK_W), out[0].dtype))
    return jnp.concatenate(out, axis=len(lead)), spans


def _unpack_rows(packed, span, lead=()):
    r0, rows, shape = span
    size = 1
    for s in shape:
        size *= s
    sl = lax.slice_in_dim(packed, r0, r0 + rows, axis=len(lead))
    return sl.reshape(lead + (-1,))[..., :size].reshape(lead + tuple(shape))


def _pack_flat(parts):
    flat, spans, at = [], [], 0
    for p in parts:
        size = p.size
        padded = size + (-size) % _LANES
        flat.append(jnp.pad(p.reshape(-1), (0, padded - size)))
        spans.append((at, size, p.shape))
        at += padded
    tail = (-at) % (_PACKED_ROWS * _PACK_W)
    if tail:
        flat.append(jnp.zeros((tail,), flat[0].dtype))
    return jnp.concatenate(flat).reshape(-1, _PACK_W), spans


def _unpack_flat(packed, span, lead=()):
    at, size, shape = span
    return lax.slice_in_dim(packed.reshape(lead + (-1,)), at, at + size, axis=len(lead)).reshape(lead + tuple(shape))


def _rope_tables(positions, T):
    inv_freq = ROPE_THETA ** (-jnp.arange(0, ROPE_DIM, 2, dtype=F32) / ROPE_DIM)
    ang = positions.reshape(T, 1).astype(F32) * inv_freq
    cos, sin = jnp.cos(ang), jnp.sin(ang)
    half = ROPE_DIM // 2
    z = lambda n: jnp.zeros((T, n), F32)
    tail = HEAD_W - QK_DIM
    ct = jnp.concatenate([jnp.ones((T, NOPE_DIM), F32), cos, cos, z(tail)], axis=1)
    sa = jnp.concatenate([z(NOPE_DIM), -sin, z(half), z(tail)], axis=1)
    sb = jnp.concatenate([z(NOPE_DIM), z(half), sin, z(tail)], axis=1)
    return ct, sa, sb


def _pad_heads(wt, per_head, keep_from=0, keep=None):
    K = wt.shape[1]
    keep = per_head if keep is None else keep
    w3 = wt.reshape(N_HEADS, per_head, K)[:, keep_from:keep_from + keep]
    return jnp.pad(w3, ((0, 0), (0, HEAD_W - keep), (0, 0))).reshape(N_HEADS * HEAD_W, K)


def _unpad_heads(g, keep):
    return g.reshape(N_HEADS, HEAD_W, g.shape[1])[:, :keep]


def _layer_weights(W):
    D = W['w_out'].shape[1]
    C = W['w_conv_out'].shape[1]
    QL, KVL = W['w_uq'].shape[1], W['w_ukv'].shape[1]
    o1 = 2 * C
    o2, o3 = o1 + QL, o1 + QL + KVL
    o4 = o3 + ROPE_DIM
    win = W['w_in']
    zr = lambda n: jnp.zeros((n, D), win.dtype)
    L = dict(W)
    L['w_c'] = win[:o1]
    L['w_l'] = jnp.concatenate([win[o1:o3], zr(NOPE_DIM), win[o3:o4], zr(HEAD_W - QK_DIM)], axis=0)
    L['w_g'] = win[o4:]
    L['w_q'] = _pad_heads(W['w_uq'], QK_DIM)
    L['w_k'] = _pad_heads(W['w_ukv'], NOPE_DIM + V_DIM, 0, NOPE_DIM)
    L['w_v'] = _pad_heads(W['w_ukv'], NOPE_DIM + V_DIM, NOPE_DIM, V_DIM)
    wmo = W['w_mla_out'].reshape(D, N_HEADS, V_DIM)
    L['w_mo'] = jnp.pad(wmo, ((0, 0), (0, 0), (0, HEAD_W - V_DIM))).reshape(D, N_HEADS * HEAD_W)
    L['dims'] = (D, C, QL, KVL)
    return L


def _row(v, width=None):
    v = v.reshape(1, -1)
    if width is not None and v.shape[1] < width:
        v = jnp.pad(v, ((0, 0), (0, width - v.shape[1])))
    return v


def _ffn_fwd(x, g, wg, wu, wd):
    h = _rms_fwd(x, _row(g))
    a, b, u = _ffn_up(h, wg, wu)
    x_out = _mm("ffn_down", u, wd, 'nn', res=x, scale=0.5)
    return x_out, (x, h, a, b, u)


def _ffn_bwd_weights(dx, saved, wd, after):
    x, h, a, b, u = saved
    d_wd = _wgrad("ffn_dwd", u, dx, scale=0.5)
    da, db = _ffn_bwd_up(dx, wd, a, b, after)
    return da, db, (_wgrad("ffn_dwg", da, h), _wgrad("ffn_dwg", db, h), d_wd)


def _ffn_bwd_input(dx, saved, g, wg, wu, da, db, after):
    return _dh_rms_bwd("ffn_dh_rms", [(da, wg), (db, wu)], saved[0], _row(g), dx, after)


def _layer_fwd(x, L, S, tabs):
    D, C, QL, KVL = L['dims']
    rope_cb = (QL + KVL) // HEAD_W
    x1, ffn1 = _ffn_fwd(x, S['ffn1_norm'], L['ffn1_w_gate'], L['ffn1_w_up'], L['ffn1_w_down'])
    h = _rms_fwd(x1, _row(S['mix_norm']))
    pc = _mm("proj_c", h, L['w_c'], 'nt')
    pl_ = _mm("proj_l", h, L['w_l'], 'nt')
    pg = _mm("proj_g", h, L['w_g'], 'nt')
    conv_w = jnp.pad(S['conv_w'], ((0, HALO - CONV_WIDTH), (0, 0)))
    z, c = _conv_fwd(pc, conv_w, _row(S['conv_b']), _row(S['conv_ln_g']), _row(S['conv_ln_b']))
    yc = _mm("conv_out", c, L['w_conv_out'], 'nt')
    cqn, ckvn = _lat_norm_fwd(pl_, _row(S['cq_norm']), _row(S['ckv_norm']), QL, KVL)
    q_raw = _mm("up_q", cqn, L['w_q'], 'nt')
    k_raw = _mm("up_kv", ckvn, L['w_k'], 'nt')
    v = _mm("up_kv", ckvn, L['w_v'], 'nt', out_dtype=_MXU_DTYPE)
    gq, gk = _row(S['q_norm'], HEAD_W), _row(S['k_norm'], HEAD_W)
    qn, kn = _qk_prep_fwd(q_raw, k_raw, pl_, rope_cb, *tabs, gq, gk)
    o, lse = _attn_fwd(qn, kn, v)
    ym = _mm("mla_out", o, L['w_mo'], 'nt')
    y = _gate_fwd(pg, S['gate_bias'], yc, ym)
    x2 = _mm("mix_out", y, L['w_out'], 'nn', res=x1)
    x3, ffn2 = _ffn_fwd(x2, S['ffn2_norm'], L['ffn2_w_gate'], L['ffn2_w_up'], L['ffn2_w_down'])
    saved = dict(ffn1=ffn1, ffn2=ffn2, x1=x1, h=h, pc=pc, pl=pl_, pg=pg, z=z, c=c, yc=yc, cqn=cqn, ckvn=ckvn,
                 q_raw=q_raw, k_raw=k_raw, v=v, qn=qn, kn=kn, o=o, lse=lse, ym=ym, y=y, conv_w=conv_w, gq=gq, gk=gk)
    return x3, saved


def _layer_bwd_mix(dx, L, S, tabs, A, after):
    D, C, QL, KVL = L['dims']
    rope_cb = (QL + KVL) // HEAD_W
    G, g = {}, {}
    da, db, (G['ffn2_w_gate'], G['ffn2_w_up'], G['ffn2_w_down']) = _ffn_bwd_weights(dx, A['ffn2'], L['ffn2_w_down'], after)
    dx, g['ffn2_norm'] = _ffn_bwd_input(dx, A['ffn2'], S['ffn2_norm'], L['ffn2_w_gate'], L['ffn2_w_up'], da, db, after)
    G['w_out'] = _wgrad("mix_dwout", A['y'], dx)
    dy = _mm("mix_dy", dx, L['w_out'], 'nt')
    dyc, dym, dpg, g['gate_bias'] = _gate_bwd(A['pg'], S['gate_bias'], A['yc'], A['ym'], dy)
    G['w_conv_out'] = _wgrad("conv_dwout", dyc, A['c'])
    dc = _mm("conv_dc", dyc, L['w_conv_out'], 'nn')
    dpc, dcw, g['conv_b'], g['conv_ln_g'], g['conv_ln_b'] = _conv_bwd(
        A['pc'], A['z'], dc, A['conv_w'], _row(S['conv_ln_g']), _row(S['conv_ln_b']))
    g['conv_w'] = dcw[:CONV_WIDTH]
    d_wmo = _wgrad("mla_dwout", dym, A['o'])
    G['w_mla_out'] = d_wmo.reshape(D, N_HEADS, HEAD_W)[:, :, :V_DIM].reshape(D, N_HEADS * V_DIM)
    do = _mm("mla_do", dym, L['w_mo'], 'nn', out_dtype=_MXU_DTYPE)
    dqn, dkn, dv = _attn_bwd(A['qn'], A['kn'], A['v'], A['o'], do, A['lse'])
    dq_raw, dk_raw, drope, dgq, dgk = _qk_prep_bwd(A['q_raw'], A['k_raw'], A['pl'], rope_cb, *tabs, A['gq'], A['gk'], dqn, dkn)
    g['q_norm'], g['k_norm'] = dgq[:, :QK_DIM], dgk[:, :QK_DIM]
    G['w_uq'] = _unpad_heads(_wgrad("up_dwq", dq_raw, A['cqn']), QK_DIM).reshape(N_HEADS * QK_DIM, QL)
    d_wk = _unpad_heads(_wgrad("up_dwkv", dk_raw, A['ckvn']), NOPE_DIM)
    d_wv = _unpad_heads(_wgrad("up_dwkv", dv, A['ckvn']), V_DIM)
    G['w_ukv'] = jnp.concatenate([d_wk, d_wv], axis=1).reshape(N_HEADS * (NOPE_DIM + V_DIM), KVL)
    dcq = _mm("up_dcq", dq_raw, L['w_q'], 'nn')
    dckv = _mm("up_dckv_k", dk_raw, L['w_k'], 'nn')
    dckv = _mm("up_dckv_v", dv, L['w_v'], 'nn', res=dckv)
    dpl, g['cq_norm'], g['ckv_norm'] = _lat_norm_bwd(A['pl'], _row(S['cq_norm']), _row(S['ckv_norm']), dcq, dckv, drope, QL, KVL)
    d_wc = _wgrad("proj_dwc", dpc, A['h'])
    d_wl = _wgrad("proj_dwl", dpl, A['h'])
    d_wg = _wgrad("proj_dwg", dpg, A['h'])
    ql = QL + KVL
    G['w_in'] = jnp.concatenate([d_wc, d_wl[:ql], d_wl[ql + NOPE_DIM:ql + NOPE_DIM + ROPE_DIM], d_wg], axis=0)
    dx, g['mix_norm'] = _dh_rms_bwd("proj_dh_rms", [(dpc, L['w_c']), (dpl, L['w_l']), (dpg, L['w_g'])],
                                    A['x1'], _row(S['mix_norm']), dx, after)
    return dx, G, g


def kernel(x, positions, ffn1_norm, ffn1_w_gate, ffn1_w_up, ffn1_w_down, mix_norm, w_in, gate_bias, conv_w, conv_b, conv_ln_g, conv_ln_b, w_conv_out, cq_norm, ckv_norm, w_uq, w_ukv, q_norm, k_norm, w_mla_out, w_out, ffn2_norm, ffn2_w_gate, ffn2_w_up, ffn2_w_down, loss_target, m_ffn1_norm, m_ffn1_w_gate, m_ffn1_w_up, m_ffn1_w_down, m_mix_norm, m_w_in, m_gate_bias, m_conv_w, m_conv_b, m_conv_ln_g, m_conv_ln_b, m_w_conv_out, m_cq_norm, m_ckv_norm, m_w_uq, m_w_ukv, m_q_norm, m_k_norm, m_w_mla_out, m_w_out, m_ffn2_norm, m_ffn2_w_gate, m_ffn2_w_up, m_ffn2_w_down, v_ffn1_norm, v_ffn1_w_gate, v_ffn1_w_up, v_ffn1_w_down, v_mix_norm, v_w_in, v_gate_bias, v_conv_w, v_conv_b, v_conv_ln_g, v_conv_ln_b, v_w_conv_out, v_cq_norm, v_ckv_norm, v_w_uq, v_w_ukv, v_q_norm, v_k_norm, v_w_mla_out, v_w_out, v_ffn2_norm, v_ffn2_w_gate, v_ffn2_w_up, v_ffn2_w_down):
    w = dict(zip(WEIGHTS, (ffn1_norm, ffn1_w_gate, ffn1_w_up, ffn1_w_down, mix_norm, w_in, gate_bias, conv_w, conv_b, conv_ln_g, conv_ln_b, w_conv_out, cq_norm, ckv_norm, w_uq, w_ukv, q_norm, k_norm, w_mla_out, w_out, ffn2_norm, ffn2_w_gate, ffn2_w_up, ffn2_w_down)))
    m = dict(zip(WEIGHTS, (m_ffn1_norm, m_ffn1_w_gate, m_ffn1_w_up, m_ffn1_w_down, m_mix_norm, m_w_in, m_gate_bias, m_conv_w, m_conv_b, m_conv_ln_g, m_conv_ln_b, m_w_conv_out, m_cq_norm, m_ckv_norm, m_w_uq, m_w_ukv, m_q_norm, m_k_norm, m_w_mla_out, m_w_out, m_ffn2_norm, m_ffn2_w_gate, m_ffn2_w_up, m_ffn2_w_down)))
    v = dict(zip(WEIGHTS, (v_ffn1_norm, v_ffn1_w_gate, v_ffn1_w_up, v_ffn1_w_down, v_mix_norm, v_w_in, v_gate_bias, v_conv_w, v_conv_b, v_conv_ln_g, v_conv_ln_b, v_w_conv_out, v_cq_norm, v_ckv_norm, v_w_uq, v_w_ukv, v_q_norm, v_k_norm, v_w_mla_out, v_w_out, v_ffn2_norm, v_ffn2_w_gate, v_ffn2_w_up, v_ffn2_w_down)))
    depth = ffn1_norm.shape[0]
    T, D = x.shape[1], x.shape[2]
    xs = x.reshape(T, D)
    target = loss_target.reshape(T, D)
    tabs = _rope_tables(positions, T)
    me = 4 * lax.axis_index("x") + 2 * lax.axis_index("y") + lax.axis_index("c")

    big = list(BIG)
    packed = []
    for l in range(depth):
        p, spans = _pack_rows([(w[n][l].T if BIG[n] else w[n][l]).astype(_WIRE_DTYPE) for n in big])
        packed.append(p)
    small_names = [(l, n) for l in range(depth) for n in SMALL_SHARDED]
    small_packed, small_spans = _pack_flat([w[n][l].T for l, n in small_names])
    small_flight = _exchange_start("small_weights_start", small_packed, _own_slot(small_packed, me), False)
    gathered = {0: _all_gather("gather_weights", packed[0])}
    in_flight = {l: _exchange_start(f"gather_start_{l}", packed[l], _own_slot(packed[l], me), False)
                 for l in range(1, depth)}
    send_sems, recv_sems, src, land, _ = small_flight
    small_gathered = _exchange_wait("small_weights_wait", send_sems, recv_sems, src, land, gathered[0], False)
    for st in in_flight.values():
        xs, _ = lax.optimization_barrier((xs, st[4]))

    def layer_weights(l):
        W = {}
        for n, span in zip(big, spans):
            sh = _unpack_rows(gathered[l], span, (N_DEV,))
            W[n] = sh.reshape(N_DEV * sh.shape[1], sh.shape[2])
        S = {n: w[n][l] for n in REPLICATED}
        for (ll, n), span in zip(small_names, small_spans):
            if ll == l:
                sh = _unpack_flat(small_gathered, span, (N_DEV,))
                S[n] = sh.reshape(N_DEV * sh.shape[1], sh.shape[2]).T
        return _layer_weights(W), S

    layers, acts = [], []
    hcur = xs
    for l in range(depth):
        if l > 0:
            send_sems, recv_sems, src, land, _ = in_flight[l]
            gathered[l] = _exchange_wait(f"gather_wait_{l}", send_sems, recv_sems, src, land, hcur, False)
        layers.append(layer_weights(l))
        hcur, saved = _layer_fwd(hcur, *layers[l], tabs)
        acts.append(saved)
    dx, loss_part = _loss_grad(hcur, target)

    stage_names = {'mix': [n for n in big if not n.startswith('ffn1_')], 'down': ['ffn1_w_down'],
                   'gateup': ['ffn1_w_gate', 'ffn1_w_up']}
    small_grads, gspans, in_flight = [{} for _ in range(depth)], {}, {}

    def send_stage(l, stage, G):
        blocks = [G[n].reshape(N_DEV, -1, G[n].shape[1]) for n in stage_names[stage]]
        send, gspans[stage] = _pack_rows(blocks, (N_DEV,))
        own = lax.dynamic_index_in_dim(send, me, 0, keepdims=False)
        in_flight[l, stage] = _exchange_start(f"scatter_start_{l}_{stage}", send, _own_slot(own, me), True)
        return in_flight[l, stage][4]

    after = jnp.zeros((_SUBLANES, _LANES), F32)
    for l in reversed(range(depth)):
        L, S = layers[l]
        dx, G, g = _layer_bwd_mix(dx, L, S, tabs, acts[l], after)
        small_grads[l].update(g)
        after = send_stage(l, 'mix', G)
        x_in, h, a, b, u = acts[l]['ffn1']
        after = after + send_stage(l, 'down', {'ffn1_w_down': _wgrad("ffn_dwd", u, dx, scale=0.5)})
        da, db = _ffn_bwd_up(dx, L['ffn1_w_down'], a, b, after)
        after = send_stage(l, 'gateup', {'ffn1_w_gate': _wgrad("ffn_dwg", da, h), 'ffn1_w_up': _wgrad("ffn_dwg", db, h)})
        dx, small_grads[l]['ffn1_norm'] = _ffn_bwd_input(dx, acts[l]['ffn1'], S['ffn1_norm'], L['ffn1_w_gate'],
                                                          L['ffn1_w_up'], da, db, after)
    small_list = [(l, n) for l in range(depth) for n in REPLICATED + SMALL_SHARDED]
    loss_part = loss_part + after[:1, :]
    small_send, sspans = _pack_flat([small_grads[l][n] for l, n in small_list] + [loss_part])
    small_flight = _exchange_start("small_grads_start", small_send, _own_slot(small_send, me), False)

    out4 = {}
    after = small_flight[4]
    for (l, stage), (send_sems, recv_sems, src, land, _) in in_flight.items():
        got = _exchange_wait(f"scatter_wait_{l}_{stage}", send_sems, recv_sems, src, land, after, True)
        reduced = _sum_blocks("sum_grads_direct", got)
        for n, span in zip(stage_names[stage], gspans[stage]):
            form = (lambda t: jnp.swapaxes(t, 1, 2)) if BIG[n] else (lambda t: t)
            out4[n] = _adamw_layer(l, form(w[n]), _unpack_rows(reduced, span), form(m[n]), form(v[n]), out4.get(n), after)
            after = out4[n][1]
    for n in big:
        if BIG[n]:
            out4[n] = [jnp.swapaxes(t, 1, 2) for t in out4[n]]
    send_sems, recv_sems, src, land, _ = small_flight
    small_all = _sum_blocks("sum_small", _exchange_wait("small_grads_wait", send_sems, recv_sems, src, land, after, False))
    loss = _unpack_flat(small_all, sspans[-1])[0, 0]
    for n in WEIGHTS:
        if n in BIG:
            continue
        per_layer = []
        for l in range(depth):
            gs = _unpack_flat(small_all, sspans[small_list.index((l, n))])
            if n in SMALL_SHARDED:
                cols = w[n].shape[-1]
                gs = lax.dynamic_slice_in_dim(gs, me * cols, cols, axis=1)
            per_layer.append(gs.reshape(w[n].shape[1:]))
        g = jnp.stack(per_layer, axis=0)
        out4[n] = (g,) + _adamw(w[n], g, m[n], v[n])
    return (loss, dx.reshape(1, T, D), *[out4[n][k] for k in range(4) for n in WEIGHTS])
```

```python
import jax
import jax.numpy as jnp
from jax import lax
from jax.experimental import pallas as pl
from jax.experimental.pallas import tpu as pltpu

F32 = jnp.float32
_MXU_DTYPE = jnp.bfloat16
_WIRE_DTYPE = jnp.bfloat16

_LANES = 128
_SUBLANES = 8
_PACKED_ROWS = 16
_V7X_VMEM_BYTES = 64 * 2 ** 20
_VMEM_HEADROOM = 8 * 2 ** 20

N_DEV = 8
EPS = 1e-6
CHUNK = 64
CONV_WIDTH = 31
N_HEADS = 8
NOPE_DIM = 64
ROPE_DIM = 32
QK_DIM = NOPE_DIM + ROPE_DIM
V_DIM = 64
HEAD_W = _LANES
ROPE_THETA = 10000.0
HALO = 32

ADAM_LR = 0.001
ADAM_B1 = 0.9
ADAM_B2 = 0.999
ADAM_EPS = 1e-08
ADAM_WD = 0.01
ADAM_STEP = 10

_ROW_TILE = 512
_MM_TILE = 512
_FFN_TILE = 512
_MM_TILE_N = 1408
_MM_TILE_K = 1024
_TN_TILE_M = 1408
_ATT_TILE = 512
_ATT_ROWS = 32
_ATT_HEADS = 2
_ATT_HEADS_FWD = 4
_PACK_W = 1024
_PACK_ROW_MULT = 128

WEIGHTS = ['ffn1_norm', 'ffn1_w_gate', 'ffn1_w_up', 'ffn1_w_down', 'mix_norm', 'w_in', 'gate_bias',
           'conv_w', 'conv_b', 'conv_ln_g', 'conv_ln_b', 'w_conv_out', 'cq_norm', 'ckv_norm', 'w_uq',
           'w_ukv', 'q_norm', 'k_norm', 'w_mla_out', 'w_out', 'ffn2_norm', 'ffn2_w_gate', 'ffn2_w_up',
           'ffn2_w_down']
BIG = {'ffn1_w_gate': True, 'ffn1_w_up': True, 'ffn1_w_down': False, 'w_in': True, 'w_conv_out': True,
       'w_uq': True, 'w_ukv': True, 'w_mla_out': True, 'w_out': False, 'ffn2_w_gate': True,
       'ffn2_w_up': True, 'ffn2_w_down': False}
SMALL_SHARDED = ['conv_w', 'gate_bias']
REPLICATED = ['ffn1_norm', 'mix_norm', 'conv_b', 'conv_ln_g', 'conv_ln_b', 'cq_norm', 'ckv_norm',
              'q_norm', 'k_norm', 'ffn2_norm']


def _tile(n, target, mult=_LANES):
    if n <= target:
        return n
    for d in range(target - target % mult, 0, -mult):
        if n % d == 0:
            return d
    return n


def _nbytes(shape, dtype):
    n = 1
    for s in shape:
        n *= s
    return n * jnp.dtype(dtype).itemsize


def _params(dims, block_bytes, extra_bytes=0):
    need = 2 * block_bytes + extra_bytes + _VMEM_HEADROOM
    assert need <= _V7X_VMEM_BYTES, (need, dims)
    return pltpu.CompilerParams(dimension_semantics=dims, vmem_limit_bytes=_V7X_VMEM_BYTES - _VMEM_HEADROOM)


def _mxu(x, scale=None):
    if scale is not None:
        x = x * scale
    return x if x.dtype == _MXU_DTYPE else x.astype(_MXU_DTYPE)


def _dot(a, b, mode):
    dims = {'nn': (((1,), (0,)), ((), ())), 'nt': (((1,), (1,)), ((), ())), 'tn': (((0,), (0,)), ((), ()))}[mode]
    return lax.dot_general(a, b, dims, preferred_element_type=F32)


def _colsum(x):
    return jnp.sum(x, axis=0, keepdims=True)


def _sig(x):
    return 0.5 * jnp.tanh(0.5 * x) + 0.5


def _mm(name, a, b, mode, *, out_dtype=F32, res=None, scale=None, a_scale=None, tm=None, tn=None, tk=None):
    if mode == 'nn':
        (M, K), (_, N) = a.shape, b.shape
    elif mode == 'nt':
        (M, K), (N, _) = a.shape, b.shape
    else:
        (K, M), (_, N) = a.shape, b.shape
    tm = _tile(M, tm or (_TN_TILE_M if mode == 'tn' else _MM_TILE))
    tn = _tile(N, tn or _MM_TILE_N)
    tk = _tile(K, tk or (_MM_TILE_K if mode == 'tn' else 4096))
    nk = K // tk
    grid = (M // tm, N // tn, nk)
    a_spec = pl.BlockSpec((tk, tm), lambda i, j, k: (k, i)) if mode == 'tn' else pl.BlockSpec((tm, tk), lambda i, j, k: (i, k))
    b_spec = pl.BlockSpec((tn, tk), lambda i, j, k: (j, k)) if mode == 'nt' else pl.BlockSpec((tk, tn), lambda i, j, k: (k, j))
    o_spec = pl.BlockSpec((tm, tn), lambda i, j, k: (i, j))
    has_res = res is not None

    def body(*refs):
        a_ref, b_ref = refs[0], refs[1]
        res_ref = refs[2] if has_res else None
        o_ref = refs[3] if has_res else refs[2]
        acc_ref = refs[-1] if nk > 1 else None

        def finish(p):
            if scale is not None:
                p = p * scale
            if has_res:
                p = res_ref[...] + p
            o_ref[...] = p.astype(o_ref.dtype)

        p = _dot(_mxu(a_ref[...], a_scale), _mxu(b_ref[...]), mode)
        if nk == 1:
            finish(p)
        else:
            k = pl.program_id(2)

            @pl.when(k == 0)
            def _():
                acc_ref[...] = p

            @pl.when(k > 0)
            def _():
                acc_ref[...] += p

            @pl.when(k == nk - 1)
            def _():
                finish(acc_ref[...])

    blocks = (_nbytes((tm, tk), a.dtype) + _nbytes((tk, tn), b.dtype) + _nbytes((tm, tn), out_dtype)
              + (_nbytes((tm, tn), F32) if has_res else 0))
    extra = ((2 + (nk > 1)) * _nbytes((tm, tn), F32) + (a.dtype != _MXU_DTYPE) * _nbytes((tm, tk), _MXU_DTYPE)
             + (b.dtype != _MXU_DTYPE) * _nbytes((tk, tn), _MXU_DTYPE))
    return pl.pallas_call(
        body, name=name, grid=grid,
        in_specs=[a_spec, b_spec] + ([o_spec] if has_res else []),
        out_specs=o_spec,
        out_shape=jax.ShapeDtypeStruct((M, N), out_dtype),
        scratch_shapes=[pltpu.VMEM((tm, tn), F32)] if nk > 1 else [],
        compiler_params=_params(("parallel", "parallel", "arbitrary"), blocks, extra),
    )(*([a, b] + ([res] if has_res else [])))


def _mm_res_norm(name, a, b, res, scale, g):
    (M, K), (_, N) = a.shape, b.shape
    tm = _tile(M, _MM_TILE)

    def body(a_ref, b_ref, res_ref, g_ref, o_ref, h_ref):
        xv = res_ref[...] + _dot(a_ref[...], b_ref[...], 'nn') * scale
        o_ref[...] = xv
        r = lax.rsqrt(jnp.mean(xv * xv, axis=-1, keepdims=True) + EPS)
        h_ref[...] = (xv * r * g_ref[...]).astype(h_ref.dtype)

    blocks = _nbytes((tm, K), a.dtype) + _nbytes((K, N), b.dtype) + 2 * _nbytes((tm, N), F32) + _nbytes((tm, N), _MXU_DTYPE)
    return pl.pallas_call(
        body, name=name, grid=(M // tm,),
        in_specs=[_row_spec(tm, K), _full_spec((K, N)), _row_spec(tm, N), _full_spec((1, N))],
        out_specs=[_row_spec(tm, N), _row_spec(tm, N)],
        out_shape=[jax.ShapeDtypeStruct((M, N), F32), jax.ShapeDtypeStruct((M, N), _MXU_DTYPE)],
        compiler_params=_params(("parallel",), blocks, 3 * _nbytes((tm, N), F32)),
    )(a, b, res, g)


def _wgrad(name, a, b, scale=None):
    return _mm(name, a, b, 'tn', out_dtype=_WIRE_DTYPE, scale=scale)


def _col_chunks(n, width=2 * _LANES):
    return [slice(c, min(c + width, n)) for c in range(0, n, width)]


def _ffn_up(h, wg, wu):
    T, D = h.shape
    F = wg.shape[0]
    tm, tn = _tile(T, _FFN_TILE), _tile(F, _MM_TILE_N)

    def body(h_ref, wg_ref, wu_ref, a_ref, b_ref, u_ref):
        hv = h_ref[...]
        for cols in _col_chunks(tn):
            a = _dot(hv, wg_ref[cols, :], 'nt')
            b = _dot(hv, wu_ref[cols, :], 'nt')
            a_ref[:, cols] = a
            b_ref[:, cols] = b
            u_ref[:, cols] = (a * _sig(a) * b).astype(u_ref.dtype)

    w_spec = pl.BlockSpec((tn, D), lambda j, i: (j, 0))
    o_spec = pl.BlockSpec((tm, tn), lambda j, i: (i, j))
    blocks = _nbytes((tm, D), h.dtype) + 2 * _nbytes((tn, D), wg.dtype) + 2 * _nbytes((tm, tn), F32) + _nbytes((tm, tn), _MXU_DTYPE)
    return pl.pallas_call(
        body, name="ffn_up", grid=(F // tn, T // tm),
        in_specs=[pl.BlockSpec((tm, D), lambda j, i: (i, 0)), w_spec, w_spec],
        out_specs=[o_spec, o_spec, o_spec],
        out_shape=[jax.ShapeDtypeStruct((T, F), F32), jax.ShapeDtypeStruct((T, F), F32),
                   jax.ShapeDtypeStruct((T, F), _MXU_DTYPE)],
        compiler_params=_params(("parallel", "parallel"), blocks, 4 * _nbytes((tm, tn), F32)),
    )(h, wg, wu)


def _ffn_bwd_up(dx, wd, a, b, after):
    T, D = dx.shape
    F = wd.shape[0]
    tm, tn = _tile(T, _FFN_TILE), _tile(F, _MM_TILE_N)

    def body(dx_ref, wd_ref, a_ref, b_ref, after_ref, da_ref, db_ref):
        dy = _mxu(dx_ref[...], 0.5)
        for cols in _col_chunks(tn):
            du = _dot(dy, wd_ref[cols, :], 'nt')
            av, bv = a_ref[:, cols], b_ref[:, cols]
            s = _sig(av)
            db_ref[:, cols] = (du * av * s).astype(db_ref.dtype)
            da_ref[:, cols] = (du * bv * (s * (1.0 + av * (1.0 - s)))).astype(da_ref.dtype)

    t_spec = pl.BlockSpec((tm, tn), lambda j, i: (i, j))
    blocks = _nbytes((tm, D), F32) + _nbytes((tn, D), wd.dtype) + 2 * _nbytes((tm, tn), F32) + 2 * _nbytes((tm, tn), _MXU_DTYPE)
    return pl.pallas_call(
        body, name="ffn_bwd_up", grid=(F // tn, T // tm),
        in_specs=[pl.BlockSpec((tm, D), lambda j, i: (i, 0)), pl.BlockSpec((tn, D), lambda j, i: (j, 0)), t_spec, t_spec,
                  pl.BlockSpec(memory_space=pl.ANY)],
        out_specs=[t_spec, t_spec],
        out_shape=[jax.ShapeDtypeStruct((T, F), _MXU_DTYPE)] * 2,
        compiler_params=_params(("parallel", "parallel"), blocks, 5 * _nbytes((tm, tn), F32)),
    )(dx, wd, a, b, after)


def _row_spec(tm, w, cb=0):
    return pl.BlockSpec((tm, w), lambda i: (i, cb))


def _full_spec(shape):
    return pl.BlockSpec(shape, lambda i: (0,) * len(shape))


def _init_acc(refs):
    @pl.when(pl.program_id(0) == 0)
    def _():
        for r in refs:
            r[...] = jnp.zeros_like(r)


def _rms_fwd(x, g):
    T, D = x.shape
    tm = _tile(T, _ROW_TILE, _PACKED_ROWS)

    def body(x_ref, g_ref, h_ref):
        xv = x_ref[...]
        r = lax.rsqrt(jnp.mean(xv * xv, axis=-1, keepdims=True) + EPS)
        h_ref[...] = (xv * r * g_ref[...]).astype(h_ref.dtype)

    return pl.pallas_call(
        body, name="rms_fwd", grid=(T // tm,),
        in_specs=[_row_spec(tm, D), _full_spec((1, D))], out_specs=_row_spec(tm, D),
        out_shape=jax.ShapeDtypeStruct((T, D), _MXU_DTYPE),
        compiler_params=_params(("parallel",), 2 * _nbytes((tm, D), F32), 4 * _nbytes((tm, D), F32)),
    )(x, g)


def _dh_rms_bwd(name, pairs, x, g, dres, after):
    T, D = x.shape
    tm = _tile(T, _ROW_TILE // 2, _PACKED_ROWS)
    n = len(pairs)

    def body(*refs):
        x_ref, g_ref, dres_ref, after_ref, dx_ref, dg_ref = refs[2 * n:]
        _init_acc([dg_ref])
        dh = None
        for k in range(n):
            p = _dot(_mxu(refs[2 * k][...]), refs[2 * k + 1][...], 'nn')
            dh = p if dh is None else dh + p
        xv = x_ref[...]
        r = lax.rsqrt(jnp.mean(xv * xv, axis=-1, keepdims=True) + EPS)
        xh = xv * r
        dxh = dh * g_ref[...]
        dx_ref[...] = dres_ref[...] + r * (dxh - xh * jnp.mean(dxh * xh, axis=-1, keepdims=True))
        dg_ref[...] += _colsum(dh * xh)

    in_specs, args, blocks = [], [], 3 * _nbytes((tm, D), F32)
    for a, w in pairs:
        in_specs += [_row_spec(tm, a.shape[1]), _full_spec(w.shape)]
        args += [a, w]
        blocks += _nbytes((tm, a.shape[1]), a.dtype) + _nbytes(w.shape, w.dtype)
    return pl.pallas_call(
        body, name=name, grid=(T // tm,),
        in_specs=in_specs + [_row_spec(tm, D), _full_spec((1, D)), _row_spec(tm, D), pl.BlockSpec(memory_space=pl.ANY)],
        out_specs=[_row_spec(tm, D), _full_spec((1, D))],
        out_shape=[jax.ShapeDtypeStruct((T, D), F32), jax.ShapeDtypeStruct((1, D), F32)],
        compiler_params=_params(("arbitrary",), blocks, 6 * _nbytes((tm, D), F32)),
    )(*args, x, g, dres, after)


def _loss_grad(y, target):
    T, D = y.shape
    tm = _tile(T, _ROW_TILE, _SUBLANES)

    def body(y_ref, t_ref, dy_ref, l_ref):
        _init_acc([l_ref])
        err = y_ref[...] - t_ref[...]
        dy_ref[...] = err * (1.0 / D)
        per_row = jnp.sum(err * err, axis=-1, keepdims=True) * (0.5 / D)
        l_ref[...] += jnp.sum(per_row, axis=0, keepdims=True)

    return pl.pallas_call(
        body, name="loss_grad", grid=(T // tm,),
        in_specs=[_row_spec(tm, D), _row_spec(tm, D)],
        out_specs=[_row_spec(tm, D), _full_spec((1, _LANES))],
        out_shape=[jax.ShapeDtypeStruct((T, D), F32), jax.ShapeDtypeStruct((1, _LANES), F32)],
        compiler_params=_params(("arbitrary",), 3 * _nbytes((tm, D), F32), 3 * _nbytes((tm, D), F32)),
    )(y, target)


def _glu(pc, C):
    a, gate = pc[:, :C], pc[:, C:]
    s = _sig(gate)
    return a, s, a * s


def _tap_groups(offsets):
    groups = {}
    for k, off in enumerate(offsets):
        groups.setdefault(off % _SUBLANES, []).append((k, off - off % _SUBLANES))
    return sorted(groups.items())


def _conv_fwd(pc, w, cb, lg, lb):
    T, C2 = pc.shape
    C = C2 // 2
    tm = _tile(T, _ROW_TILE // 2, HALO)
    per = tm // HALO

    def body(pc_ref, ph_ref, w_ref, cb_ref, lg_ref, lb_ref, z_ref, c_ref, ubuf, vbuf):
        i = pl.program_id(0)
        _, _, u_cur = _glu(pc_ref[...], C)
        _, _, u_prev = _glu(ph_ref[...], C)
        ubuf[0:HALO, :] = jnp.where(i > 0, u_prev, 0.0)
        ubuf[HALO:HALO + tm, :] = u_cur
        ubuf[HALO + tm:, :] = jnp.zeros((_SUBLANES, C), F32)
        acc = jnp.zeros((tm, C), F32)
        for shift, taps in _tap_groups([HALO - (CONV_WIDTH - 1) + k for k in range(CONV_WIDTH)]):
            v = None
            for k, base in taps:
                term = w_ref[k:k + 1, :] * ubuf[base:base + tm + _SUBLANES, :]
                v = term if v is None else v + term
            if shift == 0:
                acc = acc + v[:tm]
            else:
                vbuf[...] = v
                acc = acc + vbuf[shift:shift + tm, :]
        z = acc + cb_ref[...]
        z_ref[...] = z
        zc = z - jnp.mean(z, axis=-1, keepdims=True)
        y = zc * lax.rsqrt(jnp.mean(zc * zc, axis=-1, keepdims=True) + EPS) * lg_ref[...] + lb_ref[...]
        c_ref[...] = (y * _sig(y)).astype(c_ref.dtype)

    return pl.pallas_call(
        body, name="conv_fwd", grid=(T // tm,),
        in_specs=[_row_spec(tm, C2), pl.BlockSpec((HALO, C2), lambda i: (jnp.maximum(i * per - 1, 0), 0)),
                  _full_spec((HALO, C)), _full_spec((1, C)), _full_spec((1, C)), _full_spec((1, C))],
        out_specs=[_row_spec(tm, C), _row_spec(tm, C)],
        out_shape=[jax.ShapeDtypeStruct((T, C), F32), jax.ShapeDtypeStruct((T, C), _MXU_DTYPE)],
        scratch_shapes=[pltpu.VMEM((tm + HALO + _SUBLANES, C), F32), pltpu.VMEM((tm + _SUBLANES, C), F32)],
        compiler_params=_params(("parallel",), 3 * _nbytes((tm, C2), F32), 10 * _nbytes((tm, C), F32)),
    )(pc, pc, w, cb, lg, lb)


def _conv_bwd(pc, z, dc, w, lg, lb):
    T, C2 = pc.shape
    C = C2 // 2
    tm = _tile(T, _ROW_TILE // 2, HALO)
    per = tm // HALO
    n = T // tm
    last_halo = T // HALO - 1

    def body(pc_ref, ph_ref, z_ref, zn_ref, dc_ref, dcn_ref, w_ref, lg_ref, lb_ref,
             dpc_ref, dw_ref, dcb_ref, dlg_ref, dlb_ref, ubuf, dzbuf, vbuf, sbuf):
        i = pl.program_id(0)
        _init_acc([dw_ref, dcb_ref, dlg_ref, dlb_ref])
        g, b = lg_ref[...], lb_ref[...]

        def ln_swish_bwd(zv, dcv):
            zc = zv - jnp.mean(zv, axis=-1, keepdims=True)
            r = lax.rsqrt(jnp.mean(zc * zc, axis=-1, keepdims=True) + EPS)
            zh = zc * r
            y = zh * g + b
            s = _sig(y)
            dy = dcv * (s * (1.0 + y * (1.0 - s)))
            dyg = dy * g
            dz = r * (dyg - jnp.mean(dyg, axis=-1, keepdims=True) - zh * jnp.mean(dyg * zh, axis=-1, keepdims=True))
            return dz, dy, zh

        dz_c, dy_c, zh_c = ln_swish_bwd(z_ref[...], dc_ref[...])
        dz_n, _, _ = ln_swish_bwd(zn_ref[...], dcn_ref[...])
        dlg_ref[...] += _colsum(dy_c * zh_c)
        dlb_ref[...] += _colsum(dy_c)
        dcb_ref[...] += _colsum(dz_c)
        dzbuf[0:tm, :] = dz_c
        dzbuf[tm:tm + HALO, :] = jnp.where(i < n - 1, dz_n, 0.0)
        a, s, u_cur = _glu(pc_ref[...], C)
        _, _, u_prev = _glu(ph_ref[...], C)
        ubuf[0:HALO, :] = jnp.where(i > 0, u_prev, 0.0)
        ubuf[HALO:HALO + tm, :] = u_cur
        ubuf[HALO + tm:, :] = jnp.zeros((_SUBLANES, C), F32)
        du = jnp.zeros((tm, C), F32)
        for shift, taps in _tap_groups([CONV_WIDTH - 1 - k for k in range(CONV_WIDTH)]):
            v = None
            for k, base in taps:
                term = w_ref[k:k + 1, :] * dzbuf[base:base + tm + _SUBLANES, :]
                v = term if v is None else v + term
            if shift == 0:
                du = du + v[:tm]
            else:
                vbuf[...] = v
                du = du + vbuf[shift:shift + tm, :]
        for shift, taps in _tap_groups([HALO - (CONV_WIDTH - 1) + k for k in range(CONV_WIDTH)]):
            if shift:
                sbuf[...] = ubuf[shift:shift + tm + HALO, :]
            src = sbuf if shift else ubuf
            for k, base in taps:
                dw_ref[k:k + 1, :] += _colsum(dz_c * src[base:base + tm, :])
        dpc_ref[:, :C] = (du * s).astype(dpc_ref.dtype)
        dpc_ref[:, C:] = (du * a * s * (1.0 - s)).astype(dpc_ref.dtype)

    nxt = lambda i: (jnp.minimum((i + 1) * per, last_halo), 0)
    vec = _full_spec((1, C))
    return pl.pallas_call(
        body, name="conv_bwd", grid=(n,),
        in_specs=[_row_spec(tm, C2), pl.BlockSpec((HALO, C2), lambda i: (jnp.maximum(i * per - 1, 0), 0)),
                  _row_spec(tm, C), pl.BlockSpec((HALO, C), nxt), _row_spec(tm, C), pl.BlockSpec((HALO, C), nxt),
                  _full_spec((HALO, C)), vec, vec],
        out_specs=[_row_spec(tm, C2), _full_spec((HALO, C)), vec, vec, vec],
        out_shape=[jax.ShapeDtypeStruct((T, C2), _MXU_DTYPE), jax.ShapeDtypeStruct((HALO, C), F32)]
                  + [jax.ShapeDtypeStruct((1, C), F32)] * 3,
        scratch_shapes=[pltpu.VMEM((tm + HALO + _SUBLANES, C), F32), pltpu.VMEM((tm + HALO + _SUBLANES, C), F32),
                        pltpu.VMEM((tm + _SUBLANES, C), F32), pltpu.VMEM((tm + HALO, C), F32)],
        compiler_params=_params(("arbitrary",), 5 * _nbytes((tm, C2), F32), 16 * _nbytes((tm, C), F32)),
    )(pc, pc, z, z, dc, dc, w, lg, lb)


def _lat_norm_fwd(pl_, gq, gkv, QL, KVL):
    T, W = pl_.shape
    tm = _tile(T, _ROW_TILE, _PACKED_ROWS)

    def body(p_ref, gq_ref, gkv_ref, cq_ref, ckv_ref):
        def norm(xv, gv):
            return xv * lax.rsqrt(jnp.mean(xv * xv, axis=-1, keepdims=True) + EPS) * gv
        cq_ref[...] = norm(p_ref[:, :QL], gq_ref[...]).astype(cq_ref.dtype)
        ckv_ref[...] = norm(p_ref[:, QL:QL + KVL], gkv_ref[...]).astype(ckv_ref.dtype)

    return pl.pallas_call(
        body, name="lat_norm_fwd", grid=(T // tm,),
        in_specs=[_row_spec(tm, W), _full_spec((1, QL)), _full_spec((1, KVL))],
        out_specs=[_row_spec(tm, QL), _row_spec(tm, KVL)],
        out_shape=[jax.ShapeDtypeStruct((T, QL), _MXU_DTYPE), jax.ShapeDtypeStruct((T, KVL), _MXU_DTYPE)],
        compiler_params=_params(("parallel",), 2 * _nbytes((tm, W), F32), 4 * _nbytes((tm, W), F32)),
    )(pl_, gq, gkv)


def _lat_norm_bwd(pl_, gq, gkv, dcq, dckv, drope, QL, KVL):
    T, W = pl_.shape
    tm = _tile(T, _ROW_TILE, _PACKED_ROWS)

    def body(p_ref, gq_ref, gkv_ref, dcq_ref, dckv_ref, dr_ref, dp_ref, dgq_ref, dgkv_ref):
        _init_acc([dgq_ref, dgkv_ref])

        def bwd(xv, gv, dv, dg_ref):
            r = lax.rsqrt(jnp.mean(xv * xv, axis=-1, keepdims=True) + EPS)
            xh = xv * r
            dxh = dv * gv
            dg_ref[...] += _colsum(dv * xh)
            return r * (dxh - xh * jnp.mean(dxh * xh, axis=-1, keepdims=True))

        dp_ref[:, :QL] = bwd(p_ref[:, :QL], gq_ref[...], dcq_ref[...], dgq_ref).astype(dp_ref.dtype)
        dp_ref[:, QL:QL + KVL] = bwd(p_ref[:, QL:QL + KVL], gkv_ref[...], dckv_ref[...], dgkv_ref).astype(dp_ref.dtype)
        dp_ref[:, QL + KVL:] = dr_ref[...].astype(dp_ref.dtype)

    return pl.pallas_call(
        body, name="lat_norm_bwd", grid=(T // tm,),
        in_specs=[_row_spec(tm, W), _full_spec((1, QL)), _full_spec((1, KVL)), _row_spec(tm, QL), _row_spec(tm, KVL),
                  _row_spec(tm, HEAD_W)],
        out_specs=[_row_spec(tm, W), _full_spec((1, QL)), _full_spec((1, KVL))],
        out_shape=[jax.ShapeDtypeStruct((T, W), _MXU_DTYPE), jax.ShapeDtypeStruct((1, QL), F32),
                   jax.ShapeDtypeStruct((1, KVL), F32)],
        compiler_params=_params(("arbitrary",), 4 * _nbytes((tm, W), F32), 6 * _nbytes((tm, W), F32)),
    )(pl_, gq, gkv, dcq, dckv, drope)


def _norm_rope(xv, gv, ct, sa, sb):
    r = lax.rsqrt(jnp.sum(xv * xv, axis=-1, keepdims=True) * (1.0 / QK_DIM) + EPS)
    xn = xv * r * gv
    half = ROPE_DIM // 2
    return xn * ct + pltpu.roll(xn, HEAD_W - half, 1) * sa + pltpu.roll(xn, half, 1) * sb


def _norm_rope_bwd(xv, gv, ct, sa, sb, dout):
    half = ROPE_DIM // 2
    dxn = dout * ct + pltpu.roll(dout * sa, half, 1) + pltpu.roll(dout * sb, HEAD_W - half, 1)
    r = lax.rsqrt(jnp.sum(xv * xv, axis=-1, keepdims=True) * (1.0 / QK_DIM) + EPS)
    xh = xv * r
    dxh = dxn * gv
    dx = r * (dxh - xh * (jnp.sum(dxh * xh, axis=-1, keepdims=True) * (1.0 / QK_DIM)))
    return dx, _colsum(dxn * xh)


def _qk_prep_fwd(q_raw, k_raw, pl_, rope_cb, ct, sa, sb, gq, gk):
    T, HW = q_raw.shape
    H = HW // HEAD_W
    tm = _tile(T, _ROW_TILE, _PACKED_ROWS)

    def body(q_ref, k_ref, r_ref, ct_ref, sa_ref, sb_ref, gq_ref, gk_ref, qn_ref, kn_ref):
        ctv, sav, sbv, rv = ct_ref[...], sa_ref[...], sb_ref[...], r_ref[...]
        for h in range(H):
            hs = slice(h * HEAD_W, (h + 1) * HEAD_W)
            qn_ref[:, hs] = _norm_rope(q_ref[:, hs], gq_ref[...], ctv, sav, sbv).astype(qn_ref.dtype)
            kn_ref[:, hs] = _norm_rope(k_ref[:, hs] + rv, gk_ref[...], ctv, sav, sbv).astype(kn_ref.dtype)

    tb = _row_spec(tm, HEAD_W)
    gb = _full_spec((1, HEAD_W))
    return pl.pallas_call(
        body, name="qk_prep_fwd", grid=(T // tm,),
        in_specs=[_row_spec(tm, HW), _row_spec(tm, HW), _row_spec(tm, HEAD_W, rope_cb), tb, tb, tb, gb, gb],
        out_specs=[_row_spec(tm, HW), _row_spec(tm, HW)],
        out_shape=[jax.ShapeDtypeStruct((T, HW), _MXU_DTYPE)] * 2,
        compiler_params=_params(("parallel",), 3 * _nbytes((tm, HW), F32) + 4 * _nbytes((tm, HEAD_W), F32), 8 * _nbytes((tm, HEAD_W), F32)),
    )(q_raw, k_raw, pl_, ct, sa, sb, gq, gk)


def _qk_prep_bwd(q_raw, k_raw, pl_, rope_cb, ct, sa, sb, gq, gk, dqn, dkn):
    T, HW = q_raw.shape
    H = HW // HEAD_W
    tm = _tile(T, _ROW_TILE, _PACKED_ROWS)

    def body(q_ref, k_ref, r_ref, ct_ref, sa_ref, sb_ref, gq_ref, gk_ref, dqn_ref, dkn_ref,
             dq_ref, dk_ref, dr_ref, dgq_ref, dgk_ref):
        _init_acc([dgq_ref, dgk_ref])
        ctv, sav, sbv, rv = ct_ref[...], sa_ref[...], sb_ref[...], r_ref[...]
        dr = dgq_sum = dgk_sum = None
        for h in range(H):
            hs = slice(h * HEAD_W, (h + 1) * HEAD_W)
            dq, dgq = _norm_rope_bwd(q_ref[:, hs], gq_ref[...], ctv, sav, sbv, dqn_ref[:, hs])
            dk, dgk = _norm_rope_bwd(k_ref[:, hs] + rv, gk_ref[...], ctv, sav, sbv, dkn_ref[:, hs])
            dq_ref[:, hs] = dq.astype(dq_ref.dtype)
            dk_ref[:, hs] = dk.astype(dk_ref.dtype)
            dr = dk if h == 0 else dr + dk
            dgq_sum = dgq if h == 0 else dgq_sum + dgq
            dgk_sum = dgk if h == 0 else dgk_sum + dgk
        dr_ref[...] = dr
        dgq_ref[...] += dgq_sum
        dgk_ref[...] += dgk_sum

    tb = _row_spec(tm, HEAD_W)
    gb = _full_spec((1, HEAD_W))
    hb = _row_spec(tm, HW)
    return pl.pallas_call(
        body, name="qk_prep_bwd", grid=(T // tm,),
        in_specs=[hb, hb, _row_spec(tm, HEAD_W, rope_cb), tb, tb, tb, gb, gb, hb, hb],
        out_specs=[hb, hb, tb, gb, gb],
        out_shape=[jax.ShapeDtypeStruct((T, HW), _MXU_DTYPE)] * 2 + [jax.ShapeDtypeStruct((T, HEAD_W), F32)]
                  + [jax.ShapeDtypeStruct((1, HEAD_W), F32)] * 2,
        compiler_params=_params(("arbitrary",), 5 * _nbytes((tm, HW), F32) + 5 * _nbytes((tm, HEAD_W), F32), 12 * _nbytes((tm, HEAD_W), F32)),
    )(q_raw, k_raw, pl_, ct, sa, sb, gq, gk, dqn, dkn)


_NEG = -1e30
_LOG2E = 1.4426950408889634
_SCORE_C = QK_DIM ** -0.5 * _LOG2E


def _pieces(t, rb, diag):
    assert CHUNK % rb == 0 and rb % _SUBLANES == 0
    out = []
    for r in range(t // rb):
        lo = (r * rb // CHUNK) * CHUNK if diag else 0
        for c in range(t // _LANES):
            cut = None if (c + 1) * _LANES <= lo else max(lo - c * _LANES, 0)
            out.append((slice(r * rb, (r + 1) * rb), slice(c * _LANES, (c + 1) * _LANES), c, cut))
    return out


def _groups(x):
    return x.reshape(x.shape[0] // _SUBLANES, _SUBLANES, x.shape[1])


def _all_sublanes(x8, op):
    return jnp.broadcast_to(op(x8, axis=0, keepdims=True), x8.shape)


def _lane_ge(rb, cut):
    return lax.broadcasted_iota(jnp.int32, (rb, _LANES), 1) >= cut


def _attn_fwd(q, k, v):
    T, HW = q.shape
    H = HW // HEAD_W
    t = _tile(T, _ATT_TILE)
    rb = min(_ATT_ROWS, t)
    nc = t // _LANES

    hp = _ATT_HEADS_FWD if H % _ATT_HEADS_FWD == 0 else 1
    W = hp * HEAD_W

    def body(q_ref, k_ref, v_ref, o_ref, lse_ref, s_s, p_s, m_s, l_s, acc_s):
        i = pl.program_id(1)
        m_s[...] = jnp.full_like(m_s, _NEG)
        l_s[...] = jnp.zeros_like(l_s)
        acc_s[...] = jnp.zeros_like(acc_s)

        def head_step(hh, rows, diag):
            hs = slice(hh * HEAD_W, (hh + 1) * HEAD_W)
            s_s[hh] = _dot(k_ref[rows, hs], q_ref[:, hs], 'nt')
            pieces = _pieces(t, rb, diag)

            def scores(rs, cs, cut):
                sb = s_s[hh, rs, cs]
                return jnp.where(_lane_ge(rb, cut), sb, _NEG) if cut else sb

            mx = [None] * nc
            for rs, cs, c, cut in pieces:
                if cut is not None:
                    g = jnp.max(_groups(scores(rs, cs, cut)), axis=0)
                    mx[c] = g if mx[c] is None else jnp.maximum(mx[c], g)
            m_new, alpha = [], []
            for c in range(nc):
                cs = slice(c * _LANES, (c + 1) * _LANES)
                m_prev = m_s[hh, :, cs]
                m_new.append(jnp.maximum(m_prev, _all_sublanes(mx[c], jnp.max)))
                alpha.append(jnp.exp2((m_prev - m_new[c]) * _SCORE_C))
                m_s[hh, :, cs] = m_new[c]
            lsum = [jnp.zeros((_SUBLANES, _LANES), F32)] * nc
            for rs, cs, c, cut in pieces:
                if cut is None:
                    p_s[hh, rs, cs] = jnp.zeros((rb, _LANES), p_s.dtype)
                    continue
                p = jnp.exp2((scores(rs, cs, cut) - jnp.tile(m_new[c], (rb // _SUBLANES, 1))) * _SCORE_C)
                lsum[c] = lsum[c] + jnp.sum(_groups(p), axis=0)
                p_s[hh, rs, cs] = p.astype(p_s.dtype)
            for c in range(nc):
                cs = slice(c * _LANES, (c + 1) * _LANES)
                l_s[hh, :, cs] = alpha[c] * l_s[hh, :, cs] + _all_sublanes(lsum[c], jnp.sum)
            a = jnp.tile(jnp.concatenate(alpha, axis=1), (HEAD_W // _SUBLANES, 1))
            acc_s[hh] = a * acc_s[hh] + _dot(v_ref[rows, hs], p_s[hh], 'tn')

        def step(j, diag):
            rows = pl.ds(pl.multiple_of(j * t, t), t)
            for hh in range(hp):
                head_step(hh, rows, diag)

        def loop_body(j, carry):
            step(j, False)
            return carry

        lax.fori_loop(0, i, loop_body, 0)
        step(i, True)
        for hh in range(hp):
            l = l_s[hh]
            o_t = acc_s[hh] / jnp.tile(l, (HEAD_W // _SUBLANES, 1))
            o_ref[:, hh * HEAD_W:(hh + 1) * HEAD_W] = o_t.T.astype(o_ref.dtype)
            lse_ref[hh * _SUBLANES:(hh + 1) * _SUBLANES, :] = m_s[hh] * _SCORE_C + jnp.log(l) * _LOG2E

    qb = pl.BlockSpec((t, W), lambda h, i: (i, h))
    kb = pl.BlockSpec((T, W), lambda h, i: (0, h))
    st = pltpu.VMEM((hp, _SUBLANES, t), F32)
    return pl.pallas_call(
        body, name="attn_fwd", grid=(H // hp, T // t),
        in_specs=[qb, kb, kb], out_specs=[qb, pl.BlockSpec((hp * _SUBLANES, t), lambda h, i: (h, i))],
        out_shape=[jax.ShapeDtypeStruct((T, HW), _MXU_DTYPE), jax.ShapeDtypeStruct((H * _SUBLANES, T), F32)],
        scratch_shapes=[pltpu.VMEM((hp, t, t), F32), pltpu.VMEM((hp, t, t), _MXU_DTYPE), st, st, pltpu.VMEM((hp, HEAD_W, t), F32)],
        compiler_params=_params(("parallel", "arbitrary"), 2 * _nbytes((T, W), _MXU_DTYPE) + 3 * _nbytes((t, W), F32),
                                4 * hp * _nbytes((t, t), F32)),
    )(q, k, v)


def _attn_bwd(q, k, v, o, do, lse2):
    T, HW = q.shape
    H = HW // HEAD_W
    t = _tile(T, _ATT_TILE)
    n = T // t
    rb = min(_ATT_ROWS, t)
    scale = QK_DIM ** -0.5

    hp = _ATT_HEADS if H % _ATT_HEADS == 0 else 1
    W = hp * HEAD_W

    def body(k_ref, v_ref, q_ref, o_ref, do_ref, lse_ref, dq_ref, dk_ref, dv_ref,
             s_s, dp_s, p_s, ds_s, dk_s, dv_s, delta_s, dqt_s):
        j = pl.program_id(1)

        @pl.when(j == 0)
        def _():
            dqt_s[...] = jnp.zeros_like(dqt_s)
            for hh in range(hp):
                hs = slice(hh * HEAD_W, (hh + 1) * HEAD_W)
                for b in range(n):
                    rs = slice(b * t, (b + 1) * t)
                    d = jnp.sum(do_ref[rs, hs].astype(F32) * o_ref[rs, hs].astype(F32), axis=1, keepdims=True)
                    delta_s[hh, :, rs] = jnp.broadcast_to(d, (t, HEAD_W)).T[0:_SUBLANES, :]

        dk_s[...] = jnp.zeros_like(dk_s)
        dv_s[...] = jnp.zeros_like(dv_s)

        def head_step(hh, base, diag):
            hs = slice(hh * HEAD_W, (hh + 1) * HEAD_W)
            kv, vv = k_ref[:, hs], v_ref[:, hs]
            qv, dov = q_ref[pl.ds(base, t), hs], do_ref[pl.ds(base, t), hs]
            s_s[hh] = _dot(kv, qv, 'nt')
            dp_s[hh] = _dot(vv, dov, 'nt')
            for rs, cs, c, cut in _pieces(t, rb, diag):
                if cut is None:
                    p_s[hh, rs, cs] = jnp.zeros((rb, _LANES), p_s.dtype)
                    ds_s[hh, rs, cs] = jnp.zeros((rb, _LANES), ds_s.dtype)
                    continue
                lanes = pl.ds(pl.multiple_of(base + c * _LANES, _LANES), _LANES)
                lse = jnp.tile(lse_ref[hh * _SUBLANES:(hh + 1) * _SUBLANES, lanes], (rb // _SUBLANES, 1))
                dl = jnp.tile(delta_s[hh, :, lanes], (rb // _SUBLANES, 1))
                p = jnp.exp2(s_s[hh, rs, cs] * _SCORE_C - lse)
                if cut:
                    p = jnp.where(_lane_ge(rb, cut), p, 0.0)
                p_s[hh, rs, cs] = p.astype(p_s.dtype)
                ds_s[hh, rs, cs] = (p * (dp_s[hh, rs, cs] - dl)).astype(ds_s.dtype)
            dsv = ds_s[hh]
            dv_s[hh] += _dot(p_s[hh], dov, 'nn')
            dk_s[hh] += _dot(dsv, qv, 'nn')
            dqt_s[hh, :, pl.ds(base, t)] += _dot(kv, dsv, 'tn')

        def step(i, diag):
            base = pl.multiple_of(i * t, t)
            for hh in range(hp):
                head_step(hh, base, diag)

        def loop_body(i, carry):
            step(i, False)
            return carry

        step(j, True)
        lax.fori_loop(j + 1, n, loop_body, 0)
        for hh in range(hp):
            hs = slice(hh * HEAD_W, (hh + 1) * HEAD_W)
            dk_ref[:, hs] = dk_s[hh] * scale
            dv_ref[:, hs] = dv_s[hh].astype(dv_ref.dtype)

        @pl.when(j == n - 1)
        def _():
            for hh in range(hp):
                for b in range(n):
                    rs = slice(b * t, (b + 1) * t)
                    dq_ref[rs, hh * HEAD_W:(hh + 1) * HEAD_W] = dqt_s[hh, :, rs].T * scale

    jb = pl.BlockSpec((t, W), lambda h, j: (j, h))
    fb = pl.BlockSpec((T, W), lambda h, j: (0, h))
    sc = pltpu.VMEM((hp, t, HEAD_W), F32)
    return pl.pallas_call(
        body, name="attn_bwd", grid=(H // hp, n),
        in_specs=[jb, jb, fb, fb, fb, pl.BlockSpec((hp * _SUBLANES, T), lambda h, j: (h, 0))], out_specs=[fb, jb, jb],
        out_shape=[jax.ShapeDtypeStruct((T, HW), F32), jax.ShapeDtypeStruct((T, HW), F32),
                   jax.ShapeDtypeStruct((T, HW), _MXU_DTYPE)],
        scratch_shapes=[pltpu.VMEM((hp, t, t), F32), pltpu.VMEM((hp, t, t), F32), pltpu.VMEM((hp, t, t), _MXU_DTYPE),
                        pltpu.VMEM((hp, t, t), _MXU_DTYPE), sc, sc, pltpu.VMEM((hp, _SUBLANES, T), F32),
                        pltpu.VMEM((hp, HEAD_W, T), F32)],
        compiler_params=_params(("parallel", "arbitrary"), 3 * _nbytes((T, W), _MXU_DTYPE) + _nbytes((T, W), F32)
                                + 4 * _nbytes((t, W), F32), hp * (4 * _nbytes((t, t), F32) + _nbytes((T, HEAD_W), F32))),
    )(k, v, q, o, do, lse2)


def _gate_fwd(pg, gb, yc, ym):
    T, D = yc.shape
    tm = _tile(T, _ROW_TILE, _PACKED_ROWS)

    def body(g0_ref, g1_ref, gb_ref, yc_ref, ym_ref, y_ref):
        s0 = _sig(g0_ref[...] + gb_ref[0:1, :])
        s1 = _sig(g1_ref[...] + gb_ref[1:2, :])
        y_ref[...] = (s0 * yc_ref[...] + s1 * ym_ref[...]).astype(y_ref.dtype)

    return pl.pallas_call(
        body, name="gate_fwd", grid=(T // tm,),
        in_specs=[_row_spec(tm, D, 0), _row_spec(tm, D, 1), _full_spec((2, D)), _row_spec(tm, D), _row_spec(tm, D)],
        out_specs=_row_spec(tm, D),
        out_shape=jax.ShapeDtypeStruct((T, D), _MXU_DTYPE),
        compiler_params=_params(("parallel",), 5 * _nbytes((tm, D), F32), 4 * _nbytes((tm, D), F32)),
    )(pg, pg, gb, yc, ym)


def _gate_bwd(pg, gb, yc, ym, dy):
    T, D = yc.shape
    tm = _tile(T, _ROW_TILE, _PACKED_ROWS)

    def body(g0_ref, g1_ref, gb_ref, yc_ref, ym_ref, dy_ref, dyc_ref, dym_ref, dpg_ref, dgb_ref):
        _init_acc([dgb_ref])
        dyv = dy_ref[...]
        s0 = _sig(g0_ref[...] + gb_ref[0:1, :])
        s1 = _sig(g1_ref[...] + gb_ref[1:2, :])
        dyc_ref[...] = (dyv * s0).astype(dyc_ref.dtype)
        dym_ref[...] = (dyv * s1).astype(dym_ref.dtype)
        d0 = dyv * yc_ref[...] * s0 * (1.0 - s0)
        d1 = dyv * ym_ref[...] * s1 * (1.0 - s1)
        dpg_ref[:, :D] = d0.astype(dpg_ref.dtype)
        dpg_ref[:, D:] = d1.astype(dpg_ref.dtype)
        dgb_ref[0:1, :] += _colsum(d0)
        dgb_ref[1:2, :] += _colsum(d1)

    return pl.pallas_call(
        body, name="gate_bwd", grid=(T // tm,),
        in_specs=[_row_spec(tm, D, 0), _row_spec(tm, D, 1), _full_spec((2, D)), _row_spec(tm, D), _row_spec(tm, D),
                  _row_spec(tm, D)],
        out_specs=[_row_spec(tm, D), _row_spec(tm, D), _row_spec(tm, 2 * D), _full_spec((2, D))],
        out_shape=[jax.ShapeDtypeStruct((T, D), _MXU_DTYPE)] * 2 + [jax.ShapeDtypeStruct((T, 2 * D), _MXU_DTYPE),
                                                                   jax.ShapeDtypeStruct((2, D), F32)],
        compiler_params=_params(("arbitrary",), 7 * _nbytes((tm, D), F32), 8 * _nbytes((tm, D), F32)),
    )(pg, pg, gb, yc, ym, dy)


def _place():
    return lax.axis_index("x"), lax.axis_index("y"), lax.axis_index("c")


def _all_gather(name, xs):
    R, C = xs.shape
    MESH = pl.DeviceIdType.MESH

    def body(x_ref, out_ref, send_sems, recv_sems, local_sem):
        x, y, c = _place()
        me, sibling = (x, y, c), (x, y, 1 - c)
        chips = [(1 - x, y), (x, 1 - y), (1 - x, 1 - y)]

        def rows(px, py, pc):
            return out_ref.at[4 * px + 2 * py + pc]

        def copy(k, block, to, src=None):
            return pltpu.make_async_remote_copy(
                src_ref=rows(*block) if src is None else src, dst_ref=rows(*block),
                send_sem=send_sems.at[k], recv_sem=recv_sems.at[k], device_id=to, device_id_type=MESH)

        mine = pltpu.make_async_copy(x_ref, rows(*me), local_sem)
        mine.start()
        first = [copy(0, me, sibling, src=x_ref)]
        first += [copy(1 + j, me, (*chip, c), src=x_ref) for j, chip in enumerate(chips)]
        for cp in first:
            cp.start()
        passed = [copy(4 + j, (*chip, c), sibling) for j, chip in enumerate(chips)]
        for j, chip in enumerate(chips):
            copy(1 + j, (*chip, c), me).wait_recv()
            passed[j].start()
        copy(0, sibling, me).wait_recv()
        for j, chip in enumerate(chips):
            copy(4 + j, (*chip, 1 - c), me).wait_recv()
        for cp in first + passed:
            cp.wait_send()
        mine.wait()

    return pl.pallas_call(
        body, name=name,
        out_shape=jax.ShapeDtypeStruct((N_DEV, R, C), xs.dtype),
        in_specs=[pl.BlockSpec(memory_space=pl.ANY)], out_specs=pl.BlockSpec(memory_space=pl.ANY),
        scratch_shapes=[pltpu.SemaphoreType.DMA((7,)), pltpu.SemaphoreType.DMA((7,)), pltpu.SemaphoreType.DMA],
    )(xs)


_HBM_SPEC = pl.BlockSpec(memory_space=pltpu.HBM)
_SEM_SPEC = pl.BlockSpec(memory_space=pltpu.SEMAPHORE)
_DATAFLOW = pltpu.SideEffectType.DATAFLOW_SIDE_EFFECTING


def _peers():
    x, y, c = _place()
    out = []
    for k in range(1, N_DEV):
        px = 1 - x if k & 4 else x
        py = 1 - y if k & 2 else y
        pc = 1 - c if k & 1 else c
        out.append((k - 1, (px, py, pc), 4 * px + 2 * py + pc))
    return 4 * x + 2 * y + c, out


def _exchange_copies(x_ref, land_ref, send_sems, recv_sems, scatter):
    me, peers = _peers()
    return [pltpu.make_async_remote_copy(
        src_ref=x_ref.at[idx] if scatter else x_ref, dst_ref=land_ref.at[me],
        send_sem=send_sems.at[k], recv_sem=recv_sems.at[k], device_id=dev, device_id_type=pl.DeviceIdType.MESH)
        for k, dev, idx in peers]


def _exchange_start(name, xs, land, scatter):
    def body(x_ref, land_ref, send_sems, recv_sems, x_thru, land_thru, token):
        for cp in _exchange_copies(x_ref, land_ref, send_sems, recv_sems, scatter):
            cp.start()
        token[...] = jnp.zeros_like(token)

    sems = pltpu.SemaphoreType.DMA((N_DEV - 1,))
    return pl.pallas_call(
        body, name=name,
        out_shape=(sems, sems, pltpu.HBM(xs.shape, xs.dtype), pltpu.HBM(land.shape, land.dtype),
                   jax.ShapeDtypeStruct((_SUBLANES, _LANES), F32)),
        in_specs=(_HBM_SPEC, _HBM_SPEC),
        out_specs=(_SEM_SPEC, _SEM_SPEC, _HBM_SPEC, _HBM_SPEC, pl.BlockSpec(memory_space=pltpu.VMEM)),
        input_output_aliases={0: 2, 1: 3},
        compiler_params=pltpu.CompilerParams(has_side_effects=_DATAFLOW),
    )(pltpu.with_memory_space_constraint(xs, pltpu.HBM), pltpu.with_memory_space_constraint(land, pltpu.HBM))


def _exchange_wait(name, send_sems, recv_sems, xs, land, after, scatter):
    def body(x_ref, land_ref, send_sems, recv_sems, after_ref, x_dead, got_ref):
        for cp in _exchange_copies(x_ref, land_ref, send_sems, recv_sems, scatter):
            cp.wait_send()
            cp.wait_recv()

    return pl.pallas_call(
        body, name=name,
        out_shape=(pltpu.HBM(xs.shape, xs.dtype), pltpu.HBM(land.shape, land.dtype)),
        in_specs=(_HBM_SPEC, _HBM_SPEC, _SEM_SPEC, _SEM_SPEC, pl.BlockSpec(memory_space=pl.ANY)),
        out_specs=(_HBM_SPEC, _HBM_SPEC), input_output_aliases={0: 0, 1: 1},
        compiler_params=pltpu.CompilerParams(has_side_effects=_DATAFLOW),
    )(xs, land, send_sems, recv_sems, after)[1]


def _own_slot(block, me):
    land = lax.empty((N_DEV,) + block.shape, block.dtype)
    return lax.dynamic_update_slice(land, block[None], (me,) + (0,) * block.ndim)


def _sum_blocks(name, parts, scale=None):
    n, R, C = parts.shape
    tr = _tile(R, _PACK_ROW_MULT, _PACKED_ROWS)

    def body(p_ref, o_ref):
        acc = p_ref[0].astype(F32)
        for s in range(1, n):
            acc = acc + p_ref[s].astype(F32)
        if scale is not None:
            acc = acc * scale
        o_ref[...] = acc

    return pl.pallas_call(
        body, name=name, grid=(R // tr,),
        in_specs=[pl.BlockSpec((n, tr, C), lambda i: (0, i, 0))], out_specs=pl.BlockSpec((tr, C), lambda i: (i, 0)),
        out_shape=jax.ShapeDtypeStruct((R, C), F32),
        compiler_params=_params(("parallel",), _nbytes((n, tr, C), parts.dtype) + _nbytes((tr, C), F32), 2 * _nbytes((tr, C), F32)),
    )(parts)


def _adamw_math(wv, gv, mv, vv):
    mv = ADAM_B1 * mv + (1.0 - ADAM_B1) * gv
    vv = ADAM_B2 * vv + (1.0 - ADAM_B2) * (gv * gv)
    m_hat = mv / (1.0 - ADAM_B1 ** ADAM_STEP)
    v_hat = vv / (1.0 - ADAM_B2 ** ADAM_STEP)
    return -ADAM_LR * (m_hat / (jnp.sqrt(v_hat) + ADAM_EPS) + ADAM_WD * wv), mv, vv


def _adamw_layer(l, w, g, m, v, outs, after):
    depth, R, C = w.shape
    tr = _tile(R, _ROW_TILE, _SUBLANES)

    def body(w_ref, g_ref, m_ref, v_ref, after_ref, *rest):
        go_ref, d_ref, nm_ref, nv_ref = rest[-4:]
        gv = g_ref[...]
        go_ref[...] = gv
        d_ref[...], nm_ref[...], nv_ref[...] = _adamw_math(w_ref[...], gv, m_ref[...], v_ref[...])

    lay = pl.BlockSpec((None, tr, C), lambda i: (l, i, 0))
    shape = jax.ShapeDtypeStruct(w.shape, F32)
    return pl.pallas_call(
        body, name="adamw_layer", grid=(R // tr,),
        in_specs=[lay, pl.BlockSpec((tr, C), lambda i: (i, 0)), lay, lay] + [pl.BlockSpec(memory_space=pl.ANY)] * (5 if outs else 1),
        out_specs=[lay] * 4, out_shape=[shape] * 4,
        input_output_aliases={5 + k: k for k in range(4)} if outs else {},
        compiler_params=_params(("parallel",), 8 * _nbytes((tr, C), F32), 4 * _nbytes((tr, C), F32)),
    )(w, g, m, v, after, *(outs or ()))


def _adamw(w, g, m, v):
    shape = w.shape
    w2, g2, m2, v2 = (t.reshape(-1, shape[-1]) for t in (w, g, m, v))
    R, C = w2.shape
    tr = _tile(R, _ROW_TILE, _SUBLANES)

    def body(w_ref, g_ref, m_ref, v_ref, d_ref, nm_ref, nv_ref):
        d_ref[...], nm_ref[...], nv_ref[...] = _adamw_math(w_ref[...], g_ref[...], m_ref[...], v_ref[...])

    spec = pl.BlockSpec((tr, C), lambda i: (i, 0))
    outs = pl.pallas_call(
        body, name="adamw", grid=(R // tr,),
        in_specs=[spec] * 4, out_specs=[spec] * 3,
        out_shape=[jax.ShapeDtypeStruct((R, C), F32)] * 3,
        compiler_params=_params(("parallel",), 7 * _nbytes((tr, C), F32), 4 * _nbytes((tr, C), F32)),
    )(w2, g2, m2, v2)
    return tuple(o.reshape(shape) for o in outs)


def _pack_rows(parts, lead=()):
    out, spans, r0 = [], [], 0
    unit = _PACKED_ROWS * _PACK_W
    for p in parts:
        flat = p.reshape(lead + (-1,))
        size = flat.shape[-1]
        pad = (-size) % unit
        if pad:
            flat = jnp.pad(flat, [(0, 0)] * len(lead) + [(0, pad)])
        rows = (size + pad) // _PACK_W
        out.append(flat.reshape(lead + (rows, _PACK_W)))
        spans.append((r0, rows, p.shape[len(lead):]))
        r0 += rows
    pad = (-r0) % _PACK_ROW_MULT
    if pad:
        out.append(jnp.zeros(lead + (pad, _PACK_W), out[0].dtype))
    return jnp.concatenate(out, axis=len(lead)), spans


def _unpack_rows(packed, span, lead=()):
    r0, rows, shape = span
    size = 1
    for s in shape:
        size *= s
    sl = lax.slice_in_dim(packed, r0, r0 + rows, axis=len(lead))
    return sl.reshape(lead + (-1,))[..., :size].reshape(lead + tuple(shape))


def _pack_flat(parts):
    flat, spans, at = [], [], 0
    for p in parts:
        size = p.size
        padded = size + (-size) % _LANES
        flat.append(jnp.pad(p.reshape(-1), (0, padded - size)))
        spans.append((at, size, p.shape))
        at += padded
    tail = (-at) % (_PACKED_ROWS * _PACK_W)
    if tail:
        flat.append(jnp.zeros((tail,), flat[0].dtype))
    return jnp.concatenate(flat).reshape(-1, _PACK_W), spans


def _unpack_flat(packed, span, lead=()):
    at, size, shape = span
    return lax.slice_in_dim(packed.reshape(lead + (-1,)), at, at + size, axis=len(lead)).reshape(lead + tuple(shape))


def _rope_tables(positions, T):
    inv_freq = ROPE_THETA ** (-jnp.arange(0, ROPE_DIM, 2, dtype=F32) / ROPE_DIM)
    ang = positions.reshape(T, 1).astype(F32) * inv_freq
    cos, sin = jnp.cos(ang), jnp.sin(ang)
    half = ROPE_DIM // 2
    z = lambda n: jnp.zeros((T, n), F32)
    tail = HEAD_W - QK_DIM
    ct = jnp.concatenate([jnp.ones((T, NOPE_DIM), F32), cos, cos, z(tail)], axis=1)
    sa = jnp.concatenate([z(NOPE_DIM), -sin, z(half), z(tail)], axis=1)
    sb = jnp.concatenate([z(NOPE_DIM), z(half), sin, z(tail)], axis=1)
    return ct, sa, sb


def _pad_heads(wt, per_head, keep_from=0, keep=None):
    K = wt.shape[1]
    keep = per_head if keep is None else keep
    w3 = wt.reshape(N_HEADS, per_head, K)[:, keep_from:keep_from + keep]
    return jnp.pad(w3, ((0, 0), (0, HEAD_W - keep), (0, 0))).reshape(N_HEADS * HEAD_W, K)


def _unpad_heads(g, keep):
    return g.reshape(N_HEADS, HEAD_W, g.shape[1])[:, :keep]


def _layer_weights(W):
    D = W['w_out'].shape[1]
    C = W['w_conv_out'].shape[1]
    QL, KVL = W['w_uq'].shape[1], W['w_ukv'].shape[1]
    o1 = 2 * C
    o2, o3 = o1 + QL, o1 + QL + KVL
    o4 = o3 + ROPE_DIM
    win = W['w_in']
    zr = lambda n: jnp.zeros((n, D), win.dtype)
    L = dict(W)
    L['w_c'] = win[:o1]
    L['w_l'] = jnp.concatenate([win[o1:o3], zr(NOPE_DIM), win[o3:o4], zr(HEAD_W - QK_DIM)], axis=0)
    L['w_g'] = win[o4:]
    L['w_q'] = _pad_heads(W['w_uq'], QK_DIM)
    L['w_k'] = _pad_heads(W['w_ukv'], NOPE_DIM + V_DIM, 0, NOPE_DIM)
    L['w_v'] = _pad_heads(W['w_ukv'], NOPE_DIM + V_DIM, NOPE_DIM, V_DIM)
    wmo = W['w_mla_out'].reshape(D, N_HEADS, V_DIM)
    L['w_mo'] = jnp.pad(wmo, ((0, 0), (0, 0), (0, HEAD_W - V_DIM))).reshape(D, N_HEADS * HEAD_W)
    L['dims'] = (D, C, QL, KVL)
    return L


def _row(v, width=None):
    v = v.reshape(1, -1)
    if width is not None and v.shape[1] < width:
        v = jnp.pad(v, ((0, 0), (0, width - v.shape[1])))
    return v


def _ffn_fwd(x, h, wg, wu, wd, next_g):
    a, b, u = _ffn_up(h, wg, wu)
    if next_g is None:
        x_out, h_out = _mm("ffn_down", u, wd, 'nn', res=x, scale=0.5), None
    else:
        x_out, h_out = _mm_res_norm("ffn_down_norm", u, wd, x, 0.5, _row(next_g))
    return x_out, h_out, (x, h, a, b, u)


def _ffn_bwd_weights(dx, saved, wd, after):
    x, h, a, b, u = saved
    d_wd = _wgrad("ffn_dwd", u, dx, scale=0.5)
    da, db = _ffn_bwd_up(dx, wd, a, b, after)
    return da, db, (_wgrad("ffn_dwg", da, h), _wgrad("ffn_dwg", db, h), d_wd)


def _ffn_bwd_input(dx, saved, g, wg, wu, da, db, after):
    return _dh_rms_bwd("ffn_dh_rms", [(da, wg), (db, wu)], saved[0], _row(g), dx, after)


def _layer_fwd(x, h_in, L, S, tabs, next_g):
    D, C, QL, KVL = L['dims']
    rope_cb = (QL + KVL) // HEAD_W
    x1, h, ffn1 = _ffn_fwd(x, h_in, L['ffn1_w_gate'], L['ffn1_w_up'], L['ffn1_w_down'], S['mix_norm'])
    pc = _mm("proj_c", h, L['w_c'], 'nt')
    pl_ = _mm("proj_l", h, L['w_l'], 'nt')
    pg = _mm("proj_g", h, L['w_g'], 'nt')
    conv_w = jnp.pad(S['conv_w'], ((0, HALO - CONV_WIDTH), (0, 0)))
    z, c = _conv_fwd(pc, conv_w, _row(S['conv_b']), _row(S['conv_ln_g']), _row(S['conv_ln_b']))
    yc = _mm("conv_out", c, L['w_conv_out'], 'nt')
    cqn, ckvn = _lat_norm_fwd(pl_, _row(S['cq_norm']), _row(S['ckv_norm']), QL, KVL)
    q_raw = _mm("up_q", cqn, L['w_q'], 'nt')
    k_raw = _mm("up_kv", ckvn, L['w_k'], 'nt')
    v = _mm("up_kv", ckvn, L['w_v'], 'nt', out_dtype=_MXU_DTYPE)
    gq, gk = _row(S['q_norm'], HEAD_W), _row(S['k_norm'], HEAD_W)
    qn, kn = _qk_prep_fwd(q_raw, k_raw, pl_, rope_cb, *tabs, gq, gk)
    o, lse = _attn_fwd(qn, kn, v)
    ym = _mm("mla_out", o, L['w_mo'], 'nt')
    y = _gate_fwd(pg, S['gate_bias'], yc, ym)
    x2, h2 = _mm_res_norm("mix_out_norm", y, L['w_out'], x1, 1.0, _row(S['ffn2_norm']))
    x3, h3, ffn2 = _ffn_fwd(x2, h2, L['ffn2_w_gate'], L['ffn2_w_up'], L['ffn2_w_down'], next_g)
    saved = dict(ffn1=ffn1, ffn2=ffn2, x1=x1, h=h, pc=pc, pl=pl_, pg=pg, z=z, c=c, yc=yc, cqn=cqn, ckvn=ckvn,
                 q_raw=q_raw, k_raw=k_raw, v=v, qn=qn, kn=kn, o=o, lse=lse, ym=ym, y=y, conv_w=conv_w, gq=gq, gk=gk)
    return x3, h3, saved


def _layer_bwd_mix(dx, L, S, tabs, A, after):
    D, C, QL, KVL = L['dims']
    rope_cb = (QL + KVL) // HEAD_W
    G, g = {}, {}
    da, db, (G['ffn2_w_gate'], G['ffn2_w_up'], G['ffn2_w_down']) = _ffn_bwd_weights(dx, A['ffn2'], L['ffn2_w_down'], after)
    dx, g['ffn2_norm'] = _ffn_bwd_input(dx, A['ffn2'], S['ffn2_norm'], L['ffn2_w_gate'], L['ffn2_w_up'], da, db, after)
    G['w_out'] = _wgrad("mix_dwout", A['y'], dx)
    dy = _mm("mix_dy", dx, L['w_out'], 'nt')
    dyc, dym, dpg, g['gate_bias'] = _gate_bwd(A['pg'], S['gate_bias'], A['yc'], A['ym'], dy)
    G['w_conv_out'] = _wgrad("conv_dwout", dyc, A['c'])
    dc = _mm("conv_dc", dyc, L['w_conv_out'], 'nn')
    dpc, dcw, g['conv_b'], g['conv_ln_g'], g['conv_ln_b'] = _conv_bwd(
        A['pc'], A['z'], dc, A['conv_w'], _row(S['conv_ln_g']), _row(S['conv_ln_b']))
    g['conv_w'] = dcw[:CONV_WIDTH]
    d_wmo = _wgrad("mla_dwout", dym, A['o'])
    G['w_mla_out'] = d_wmo.reshape(D, N_HEADS, HEAD_W)[:, :, :V_DIM].reshape(D, N_HEADS * V_DIM)
    do = _mm("mla_do", dym, L['w_mo'], 'nn', out_dtype=_MXU_DTYPE)
    dqn, dkn, dv = _attn_bwd(A['qn'], A['kn'], A['v'], A['o'], do, A['lse'])
    dq_raw, dk_raw, drope, dgq, dgk = _qk_prep_bwd(A['q_raw'], A['k_raw'], A['pl'], rope_cb, *tabs, A['gq'], A['gk'], dqn, dkn)
    g['q_norm'], g['k_norm'] = dgq[:, :QK_DIM], dgk[:, :QK_DIM]
    G['w_uq'] = _unpad_heads(_wgrad("up_dwq", dq_raw, A['cqn']), QK_DIM).reshape(N_HEADS * QK_DIM, QL)
    d_wk = _unpad_heads(_wgrad("up_dwkv", dk_raw, A['ckvn']), NOPE_DIM)
    d_wv = _unpad_heads(_wgrad("up_dwkv", dv, A['ckvn']), V_DIM)
    G['w_ukv'] = jnp.concatenate([d_wk, d_wv], axis=1).reshape(N_HEADS * (NOPE_DIM + V_DIM), KVL)
    dcq = _mm("up_dcq", dq_raw, L['w_q'], 'nn')
    dckv = _mm("up_dckv_k", dk_raw, L['w_k'], 'nn')
    dckv = _mm("up_dckv_v", dv, L['w_v'], 'nn', res=dckv)
    dpl, g['cq_norm'], g['ckv_norm'] = _lat_norm_bwd(A['pl'], _row(S['cq_norm']), _row(S['ckv_norm']), dcq, dckv, drope, QL, KVL)
    d_wc = _wgrad("proj_dwc", dpc, A['h'])
    d_wl = _wgrad("proj_dwl", dpl, A['h'])
    d_wg = _wgrad("proj_dwg", dpg, A['h'])
    ql = QL + KVL
    G['w_in'] = jnp.concatenate([d_wc, d_wl[:ql], d_wl[ql + NOPE_DIM:ql + NOPE_DIM + ROPE_DIM], d_wg], axis=0)
    dx, g['mix_norm'] = _dh_rms_bwd("proj_dh_rms", [(dpc, L['w_c']), (dpl, L['w_l']), (dpg, L['w_g'])],
                                    A['x1'], _row(S['mix_norm']), dx, after)
    return dx, G, g


def kernel(x, positions, ffn1_norm, ffn1_w_gate, ffn1_w_up, ffn1_w_down, mix_norm, w_in, gate_bias, conv_w, conv_b, conv_ln_g, conv_ln_b, w_conv_out, cq_norm, ckv_norm, w_uq, w_ukv, q_norm, k_norm, w_mla_out, w_out, ffn2_norm, ffn2_w_gate, ffn2_w_up, ffn2_w_down, loss_target, m_ffn1_norm, m_ffn1_w_gate, m_ffn1_w_up, m_ffn1_w_down, m_mix_norm, m_w_in, m_gate_bias, m_conv_w, m_conv_b, m_conv_ln_g, m_conv_ln_b, m_w_conv_out, m_cq_norm, m_ckv_norm, m_w_uq, m_w_ukv, m_q_norm, m_k_norm, m_w_mla_out, m_w_out, m_ffn2_norm, m_ffn2_w_gate, m_ffn2_w_up, m_ffn2_w_down, v_ffn1_norm, v_ffn1_w_gate, v_ffn1_w_up, v_ffn1_w_down, v_mix_norm, v_w_in, v_gate_bias, v_conv_w, v_conv_b, v_conv_ln_g, v_conv_ln_b, v_w_conv_out, v_cq_norm, v_ckv_norm, v_w_uq, v_w_ukv, v_q_norm, v_k_norm, v_w_mla_out, v_w_out, v_ffn2_norm, v_ffn2_w_gate, v_ffn2_w_up, v_ffn2_w_down):
    w = dict(zip(WEIGHTS, (ffn1_norm, ffn1_w_gate, ffn1_w_up, ffn1_w_down, mix_norm, w_in, gate_bias, conv_w, conv_b, conv_ln_g, conv_ln_b, w_conv_out, cq_norm, ckv_norm, w_uq, w_ukv, q_norm, k_norm, w_mla_out, w_out, ffn2_norm, ffn2_w_gate, ffn2_w_up, ffn2_w_down)))
    m = dict(zip(WEIGHTS, (m_ffn1_norm, m_ffn1_w_gate, m_ffn1_w_up, m_ffn1_w_down, m_mix_norm, m_w_in, m_gate_bias, m_conv_w, m_conv_b, m_conv_ln_g, m_conv_ln_b, m_w_conv_out, m_cq_norm, m_ckv_norm, m_w_uq, m_w_ukv, m_q_norm, m_k_norm, m_w_mla_out, m_w_out, m_ffn2_norm, m_ffn2_w_gate, m_ffn2_w_up, m_ffn2_w_down)))
    v = dict(zip(WEIGHTS, (v_ffn1_norm, v_ffn1_w_gate, v_ffn1_w_up, v_ffn1_w_down, v_mix_norm, v_w_in, v_gate_bias, v_conv_w, v_conv_b, v_conv_ln_g, v_conv_ln_b, v_w_conv_out, v_cq_norm, v_ckv_norm, v_w_uq, v_w_ukv, v_q_norm, v_k_norm, v_w_mla_out, v_w_out, v_ffn2_norm, v_ffn2_w_gate, v_ffn2_w_up, v_ffn2_w_down)))
    depth = ffn1_norm.shape[0]
    T, D = x.shape[1], x.shape[2]
    xs = x.reshape(T, D)
    target = loss_target.reshape(T, D)
    tabs = _rope_tables(positions, T)
    me = 4 * lax.axis_index("x") + 2 * lax.axis_index("y") + lax.axis_index("c")

    big = list(BIG)
    packed = []
    for l in range(depth):
        p, spans = _pack_rows([(w[n][l].T if BIG[n] else w[n][l]).astype(_WIRE_DTYPE) for n in big])
        packed.append(p)
    small_names = [(l, n) for l in range(depth) for n in SMALL_SHARDED]
    small_packed, small_spans = _pack_flat([w[n][l].T for l, n in small_names])
    small_flight = _exchange_start("small_weights_start", small_packed, _own_slot(small_packed, me), False)
    gathered = {0: _all_gather("gather_weights", packed[0])}
    in_flight = {l: _exchange_start(f"gather_start_{l}", packed[l], _own_slot(packed[l], me), False)
                 for l in range(1, depth)}
    send_sems, recv_sems, src, land, _ = small_flight
    small_gathered = _exchange_wait("small_weights_wait", send_sems, recv_sems, src, land, gathered[0], False)
    for st in in_flight.values():
        xs, _ = lax.optimization_barrier((xs, st[4]))

    def layer_weights(l):
        W = {}
        for n, span in zip(big, spans):
            sh = _unpack_rows(gathered[l], span, (N_DEV,))
            W[n] = sh.reshape(N_DEV * sh.shape[1], sh.shape[2])
        S = {n: w[n][l] for n in REPLICATED}
        for (ll, n), span in zip(small_names, small_spans):
            if ll == l:
                sh = _unpack_flat(small_gathered, span, (N_DEV,))
                S[n] = sh.reshape(N_DEV * sh.shape[1], sh.shape[2]).T
        return _layer_weights(W), S

    layers, acts = [], []
    hcur = xs
    for l in range(depth):
        if l > 0:
            send_sems, recv_sems, src, land, _ = in_flight[l]
            gathered[l] = _exchange_wait(f"gather_wait_{l}", send_sems, recv_sems, src, land, hcur, False)
        layers.append(layer_weights(l))
        if l == 0:
            hn = _rms_fwd(hcur, _row(w['ffn1_norm'][0]))
        hcur, hn, saved = _layer_fwd(hcur, hn, *layers[l], tabs, w['ffn1_norm'][l + 1] if l + 1 < depth else None)
        acts.append(saved)
    dx, loss_part = _loss_grad(hcur, target)

    stage_names = {'mix': [n for n in big if not n.startswith('ffn1_')], 'down': ['ffn1_w_down'],
                   'gateup': ['ffn1_w_gate', 'ffn1_w_up']}
    small_grads, gspans, in_flight = [{} for _ in range(depth)], {}, {}

    def send_stage(l, stage, G):
        blocks = [G[n].reshape(N_DEV, -1, G[n].shape[1]) for n in stage_names[stage]]
        send, gspans[stage] = _pack_rows(blocks, (N_DEV,))
        own = lax.dynamic_index_in_dim(send, me, 0, keepdims=False)
        in_flight[l, stage] = _exchange_start(f"scatter_start_{l}_{stage}", send, _own_slot(own, me), True)
        return in_flight[l, stage][4]

    after = jnp.zeros((_SUBLANES, _LANES), F32)
    for l in reversed(range(depth)):
        L, S = layers[l]
        dx, G, g = _layer_bwd_mix(dx, L, S, tabs, acts[l], after)
        small_grads[l].update(g)
        after = send_stage(l, 'mix', G)
        x_in, h, a, b, u = acts[l]['ffn1']
        after = after + send_stage(l, 'down', {'ffn1_w_down': _wgrad("ffn_dwd", u, dx, scale=0.5)})
        da, db = _ffn_bwd_up(dx, L['ffn1_w_down'], a, b, after)
        after = send_stage(l, 'gateup', {'ffn1_w_gate': _wgrad("ffn_dwg", da, h), 'ffn1_w_up': _wgrad("ffn_dwg", db, h)})
        dx, small_grads[l]['ffn1_norm'] = _ffn_bwd_input(dx, acts[l]['ffn1'], S['ffn1_norm'], L['ffn1_w_gate'],
                                                          L['ffn1_w_up'], da, db, after)
    small_list = [(l, n) for l in range(depth) for n in REPLICATED + SMALL_SHARDED]
    loss_part = loss_part + after[:1, :]
    small_send, sspans = _pack_flat([small_grads[l][n] for l, n in small_list] + [loss_part])
    small_flight = _exchange_start("small_grads_start", small_send, _own_slot(small_send, me), False)

    out4 = {}
    after = small_flight[4]
    for (l, stage), (send_sems, recv_sems, src, land, _) in in_flight.items():
        got = _exchange_wait(f"scatter_wait_{l}_{stage}", send_sems, recv_sems, src, land, after, True)
        reduced = _sum_blocks("sum_grads_direct", got)
        for n, span in zip(stage_names[stage], gspans[stage]):
            form = (lambda t: jnp.swapaxes(t, 1, 2)) if BIG[n] else (lambda t: t)
            out4[n] = _adamw_layer(l, form(w[n]), _unpack_rows(reduced, span), form(m[n]), form(v[n]), out4.get(n), after)
            after = out4[n][1]
    for n in big:
        if BIG[n]:
            out4[n] = [jnp.swapaxes(t, 1, 2) for t in out4[n]]
    send_sems, recv_sems, src, land, _ = small_flight
    small_all = _sum_blocks("sum_small", _exchange_wait("small_grads_wait", send_sems, recv_sems, src, land, after, False))
    loss = _unpack_flat(small_all, sspans[-1])[0, 0]
    for n in WEIGHTS:
        if n in BIG:
            continue
        per_layer = []
        for l in range(depth):
            gs = _unpack_flat(small_all, sspans[small_list.index((l, n))])
            if n in SMALL_SHARDED:
                cols = w[n].shape[-1]
                gs = lax.dynamic_slice_in_dim(gs, me * cols, cols, axis=1)
            per_layer.append(gs.reshape(w[n].shape[1:]))
        g = jnp.stack(per_layer, axis=0)
        out4[n] = (g,) + _adamw(w[n], g, m[n], v[n])
    return (loss, dx.reshape(1, T, D), *[out4[n][k] for k in range(4) for n in WEIGHTS])
```

```python
import jax
import jax.numpy as jnp
from jax import lax
from jax.experimental import pallas as pl
from jax.experimental.pallas import tpu as pltpu

F32 = jnp.float32
_MXU_DTYPE = jnp.bfloat16
_WIRE_DTYPE = jnp.bfloat16

_LANES = 128
_SUBLANES = 8
_PACKED_ROWS = 16
_V7X_VMEM_BYTES = 64 * 2 ** 20
_VMEM_HEADROOM = 8 * 2 ** 20

N_DEV = 8
EPS = 1e-6
CHUNK = 64
CONV_WIDTH = 31
N_HEADS = 8
NOPE_DIM = 64
ROPE_DIM = 32
QK_DIM = NOPE_DIM + ROPE_DIM
V_DIM = 64
HEAD_W = _LANES
ROPE_THETA = 10000.0
HALO = 32

ADAM_LR = 0.001
ADAM_B1 = 0.9
ADAM_B2 = 0.999
ADAM_EPS = 1e-08
ADAM_WD = 0.01
ADAM_STEP = 10

_ROW_TILE = 512
_MM_TILE = 512
_FFN_TILE = 512
_MM_TILE_N = 1408
_MM_TILE_K = 1024
_TN_TILE_M = 1408
_ATT_TILE = 512
_ATT_ROWS = 32
_ATT_HEADS = 2
_ATT_HEADS_FWD = 4
_PACK_W = 1024
_PACK_ROW_MULT = 128

WEIGHTS = ['ffn1_norm', 'ffn1_w_gate', 'ffn1_w_up', 'ffn1_w_down', 'mix_norm', 'w_in', 'gate_bias',
           'conv_w', 'conv_b', 'conv_ln_g', 'conv_ln_b', 'w_conv_out', 'cq_norm', 'ckv_norm', 'w_uq',
           'w_ukv', 'q_norm', 'k_norm', 'w_mla_out', 'w_out', 'ffn2_norm', 'ffn2_w_gate', 'ffn2_w_up',
           'ffn2_w_down']
BIG = {'ffn1_w_gate': True, 'ffn1_w_up': True, 'ffn1_w_down': False, 'w_in': True, 'w_conv_out': True,
       'w_uq': True, 'w_ukv': True, 'w_mla_out': True, 'w_out': False, 'ffn2_w_gate': True,
       'ffn2_w_up': True, 'ffn2_w_down': False}
SMALL_SHARDED = ['conv_w', 'gate_bias']
REPLICATED = ['ffn1_norm', 'mix_norm', 'conv_b', 'conv_ln_g', 'conv_ln_b', 'cq_norm', 'ckv_norm',
              'q_norm', 'k_norm', 'ffn2_norm']


def _tile(n, target, mult=_LANES):
    if n <= target:
        return n
    for d in range(target - target % mult, 0, -mult):
        if n % d == 0:
            return d
    return n


def _nbytes(shape, dtype):
    n = 1
    for s in shape:
        n *= s
    return n * jnp.dtype(dtype).itemsize


def _params(dims, block_bytes, extra_bytes=0):
    need = 2 * block_bytes + extra_bytes + _VMEM_HEADROOM
    assert need <= _V7X_VMEM_BYTES, (need, dims)
    return pltpu.CompilerParams(dimension_semantics=dims, vmem_limit_bytes=_V7X_VMEM_BYTES - _VMEM_HEADROOM)


def _mxu(x, scale=None):
    if scale is not None:
        x = x * scale
    return x if x.dtype == _MXU_DTYPE else x.astype(_MXU_DTYPE)


def _dot(a, b, mode):
    dims = {'nn': (((1,), (0,)), ((), ())), 'nt': (((1,), (1,)), ((), ())), 'tn': (((0,), (0,)), ((), ()))}[mode]
    return lax.dot_general(a, b, dims, preferred_element_type=F32)


def _colsum(x):
    return jnp.sum(x, axis=0, keepdims=True)


def _sig(x):
    return 0.5 * jnp.tanh(0.5 * x) + 0.5


def _mm(name, a, b, mode, *, out_dtype=F32, res=None, scale=None, a_scale=None, tm=None, tn=None, tk=None):
    if mode == 'nn':
        (M, K), (_, N) = a.shape, b.shape
    elif mode == 'nt':
        (M, K), (N, _) = a.shape, b.shape
    else:
        (K, M), (_, N) = a.shape, b.shape
    tm = _tile(M, tm or (_TN_TILE_M if mode == 'tn' else _MM_TILE))
    tn = _tile(N, tn or _MM_TILE_N)
    tk = _tile(K, tk or (_MM_TILE_K if mode == 'tn' else 4096))
    nk = K // tk
    grid = (M // tm, N // tn, nk)
    a_spec = pl.BlockSpec((tk, tm), lambda i, j, k: (k, i)) if mode == 'tn' else pl.BlockSpec((tm, tk), lambda i, j, k: (i, k))
    b_spec = pl.BlockSpec((tn, tk), lambda i, j, k: (j, k)) if mode == 'nt' else pl.BlockSpec((tk, tn), lambda i, j, k: (k, j))
    o_spec = pl.BlockSpec((tm, tn), lambda i, j, k: (i, j))
    has_res = res is not None

    def body(*refs):
        a_ref, b_ref = refs[0], refs[1]
        res_ref = refs[2] if has_res else None
        o_ref = refs[3] if has_res else refs[2]
        acc_ref = refs[-1] if nk > 1 else None

        def finish(p):
            if scale is not None:
                p = p * scale
            if has_res:
                p = res_ref[...] + p
            o_ref[...] = p.astype(o_ref.dtype)

        p = _dot(_mxu(a_ref[...], a_scale), _mxu(b_ref[...]), mode)
        if nk == 1:
            finish(p)
        else:
            k = pl.program_id(2)

            @pl.when(k == 0)
            def _():
                acc_ref[...] = p

            @pl.when(k > 0)
            def _():
                acc_ref[...] += p

            @pl.when(k == nk - 1)
            def _():
                finish(acc_ref[...])

    blocks = (_nbytes((tm, tk), a.dtype) + _nbytes((tk, tn), b.dtype) + _nbytes((tm, tn), out_dtype)
              + (_nbytes((tm, tn), F32) if has_res else 0))
    extra = ((2 + (nk > 1)) * _nbytes((tm, tn), F32) + (a.dtype != _MXU_DTYPE) * _nbytes((tm, tk), _MXU_DTYPE)
             + (b.dtype != _MXU_DTYPE) * _nbytes((tk, tn), _MXU_DTYPE))
    return pl.pallas_call(
        body, name=name, grid=grid,
        in_specs=[a_spec, b_spec] + ([o_spec] if has_res else []),
        out_specs=o_spec,
        out_shape=jax.ShapeDtypeStruct((M, N), out_dtype),
        scratch_shapes=[pltpu.VMEM((tm, tn), F32)] if nk > 1 else [],
        compiler_params=_params(("parallel", "parallel", "arbitrary"), blocks, extra),
    )(*([a, b] + ([res] if has_res else [])))


def _mm_res_norm(name, a, b, res, scale, g):
    (M, K), (_, N) = a.shape, b.shape
    tm = _tile(M, _MM_TILE)

    def body(a_ref, b_ref, res_ref, g_ref, o_ref, h_ref):
        xv = res_ref[...] + _dot(a_ref[...], b_ref[...], 'nn') * scale
        o_ref[...] = xv
        r = lax.rsqrt(jnp.mean(xv * xv, axis=-1, keepdims=True) + EPS)
        h_ref[...] = (xv * r * g_ref[...]).astype(h_ref.dtype)

    blocks = _nbytes((tm, K), a.dtype) + _nbytes((K, N), b.dtype) + 2 * _nbytes((tm, N), F32) + _nbytes((tm, N), _MXU_DTYPE)
    return pl.pallas_call(
        body, name=name, grid=(M // tm,),
        in_specs=[_row_spec(tm, K), _full_spec((K, N)), _row_spec(tm, N), _full_spec((1, N))],
        out_specs=[_row_spec(tm, N), _row_spec(tm, N)],
        out_shape=[jax.ShapeDtypeStruct((M, N), F32), jax.ShapeDtypeStruct((M, N), _MXU_DTYPE)],
        compiler_params=_params(("parallel",), blocks, 3 * _nbytes((tm, N), F32)),
    )(a, b, res, g)


def _wgrad(name, a, b, scale=None):
    return _mm(name, a, b, 'tn', out_dtype=_WIRE_DTYPE, scale=scale)


def _col_chunks(n, width=2 * _LANES):
    return [slice(c, min(c + width, n)) for c in range(0, n, width)]


def _ffn_up(h, wg, wu):
    T, D = h.shape
    F = wg.shape[0]
    tm, tn = _tile(T, _FFN_TILE), _tile(F, _MM_TILE_N)

    def body(h_ref, wg_ref, wu_ref, a_ref, b_ref, u_ref):
        hv = h_ref[...]
        for cols in _col_chunks(tn):
            a = _dot(hv, wg_ref[cols, :], 'nt')
            b = _dot(hv, wu_ref[cols, :], 'nt')
            a_ref[:, cols] = a
            b_ref[:, cols] = b
            u_ref[:, cols] = (a * _sig(a) * b).astype(u_ref.dtype)

    w_spec = pl.BlockSpec((tn, D), lambda j, i: (j, 0))
    o_spec = pl.BlockSpec((tm, tn), lambda j, i: (i, j))
    blocks = _nbytes((tm, D), h.dtype) + 2 * _nbytes((tn, D), wg.dtype) + 2 * _nbytes((tm, tn), F32) + _nbytes((tm, tn), _MXU_DTYPE)
    return pl.pallas_call(
        body, name="ffn_up", grid=(F // tn, T // tm),
        in_specs=[pl.BlockSpec((tm, D), lambda j, i: (i, 0)), w_spec, w_spec],
        out_specs=[o_spec, o_spec, o_spec],
        out_shape=[jax.ShapeDtypeStruct((T, F), F32), jax.ShapeDtypeStruct((T, F), F32),
                   jax.ShapeDtypeStruct((T, F), _MXU_DTYPE)],
        compiler_params=_params(("parallel", "parallel"), blocks, 4 * _nbytes((tm, tn), F32)),
    )(h, wg, wu)


def _ffn_bwd_up(dx, wd, a, b, after):
    T, D = dx.shape
    F = wd.shape[0]
    tm, tn = _tile(T, _FFN_TILE), _tile(F, _MM_TILE_N)

    def body(dx_ref, wd_ref, a_ref, b_ref, after_ref, da_ref, db_ref):
        dy = _mxu(dx_ref[...], 0.5)
        for cols in _col_chunks(tn):
            du = _dot(dy, wd_ref[cols, :], 'nt')
            av, bv = a_ref[:, cols], b_ref[:, cols]
            s = _sig(av)
            db_ref[:, cols] = (du * av * s).astype(db_ref.dtype)
            da_ref[:, cols] = (du * bv * (s * (1.0 + av * (1.0 - s)))).astype(da_ref.dtype)

    t_spec = pl.BlockSpec((tm, tn), lambda j, i: (i, j))
    blocks = _nbytes((tm, D), F32) + _nbytes((tn, D), wd.dtype) + 2 * _nbytes((tm, tn), F32) + 2 * _nbytes((tm, tn), _MXU_DTYPE)
    return pl.pallas_call(
        body, name="ffn_bwd_up", grid=(F // tn, T // tm),
        in_specs=[pl.BlockSpec((tm, D), lambda j, i: (i, 0)), pl.BlockSpec((tn, D), lambda j, i: (j, 0)), t_spec, t_spec,
                  pl.BlockSpec(memory_space=pl.ANY)],
        out_specs=[t_spec, t_spec],
        out_shape=[jax.ShapeDtypeStruct((T, F), _MXU_DTYPE)] * 2,
        compiler_params=_params(("parallel", "parallel"), blocks, 5 * _nbytes((tm, tn), F32)),
    )(dx, wd, a, b, after)


def _row_spec(tm, w, cb=0):
    return pl.BlockSpec((tm, w), lambda i: (i, cb))


def _full_spec(shape):
    return pl.BlockSpec(shape, lambda i: (0,) * len(shape))


def _init_acc(refs):
    @pl.when(pl.program_id(0) == 0)
    def _():
        for r in refs:
            r[...] = jnp.zeros_like(r)


def _rms_fwd(x, g):
    T, D = x.shape
    tm = _tile(T, _ROW_TILE, _PACKED_ROWS)

    def body(x_ref, g_ref, h_ref):
        xv = x_ref[...]
        r = lax.rsqrt(jnp.mean(xv * xv, axis=-1, keepdims=True) + EPS)
        h_ref[...] = (xv * r * g_ref[...]).astype(h_ref.dtype)

    return pl.pallas_call(
        body, name="rms_fwd", grid=(T // tm,),
        in_specs=[_row_spec(tm, D), _full_spec((1, D))], out_specs=_row_spec(tm, D),
        out_shape=jax.ShapeDtypeStruct((T, D), _MXU_DTYPE),
        compiler_params=_params(("parallel",), 2 * _nbytes((tm, D), F32), 4 * _nbytes((tm, D), F32)),
    )(x, g)


def _dh_rms_bwd(name, pairs, x, g, dres, after):
    T, D = x.shape
    tm = _tile(T, _ROW_TILE // 2, _PACKED_ROWS)
    n = len(pairs)

    def body(*refs):
        x_ref, g_ref, dres_ref, after_ref, dx_ref, dg_ref = refs[2 * n:]
        _init_acc([dg_ref])
        dh = None
        for k in range(n):
            p = _dot(_mxu(refs[2 * k][...]), refs[2 * k + 1][...], 'nn')
            dh = p if dh is None else dh + p
        xv = x_ref[...]
        r = lax.rsqrt(jnp.mean(xv * xv, axis=-1, keepdims=True) + EPS)
        xh = xv * r
        dxh = dh * g_ref[...]
        dx_ref[...] = dres_ref[...] + r * (dxh - xh * jnp.mean(dxh * xh, axis=-1, keepdims=True))
        dg_ref[...] += _colsum(dh * xh)

    in_specs, args, blocks = [], [], 3 * _nbytes((tm, D), F32)
    for a, w in pairs:
        in_specs += [_row_spec(tm, a.shape[1]), _full_spec(w.shape)]
        args += [a, w]
        blocks += _nbytes((tm, a.shape[1]), a.dtype) + _nbytes(w.shape, w.dtype)
    return pl.pallas_call(
        body, name=name, grid=(T // tm,),
        in_specs=in_specs + [_row_spec(tm, D), _full_spec((1, D)), _row_spec(tm, D), pl.BlockSpec(memory_space=pl.ANY)],
        out_specs=[_row_spec(tm, D), _full_spec((1, D))],
        out_shape=[jax.ShapeDtypeStruct((T, D), F32), jax.ShapeDtypeStruct((1, D), F32)],
        compiler_params=_params(("arbitrary",), blocks, 6 * _nbytes((tm, D), F32)),
    )(*args, x, g, dres, after)


def _loss_grad(y, target):
    T, D = y.shape
    tm = _tile(T, _ROW_TILE, _SUBLANES)

    def body(y_ref, t_ref, dy_ref, l_ref):
        _init_acc([l_ref])
        err = y_ref[...] - t_ref[...]
        dy_ref[...] = err * (1.0 / D)
        per_row = jnp.sum(err * err, axis=-1, keepdims=True) * (0.5 / D)
        l_ref[...] += jnp.sum(per_row, axis=0, keepdims=True)

    return pl.pallas_call(
        body, name="loss_grad", grid=(T // tm,),
        in_specs=[_row_spec(tm, D), _row_spec(tm, D)],
        out_specs=[_row_spec(tm, D), _full_spec((1, _LANES))],
        out_shape=[jax.ShapeDtypeStruct((T, D), F32), jax.ShapeDtypeStruct((1, _LANES), F32)],
        compiler_params=_params(("arbitrary",), 3 * _nbytes((tm, D), F32), 3 * _nbytes((tm, D), F32)),
    )(y, target)


def _glu(pc, C):
    a, gate = pc[:, :C], pc[:, C:]
    s = _sig(gate)
    return a, s, a * s


def _tap_groups(offsets):
    groups = {}
    for k, off in enumerate(offsets):
        groups.setdefault(off % _SUBLANES, []).append((k, off - off % _SUBLANES))
    return sorted(groups.items())


def _conv_fwd(pc, w, cb, lg, lb):
    T, C2 = pc.shape
    C = C2 // 2
    tm = _tile(T, _ROW_TILE // 2, HALO)
    per = tm // HALO

    def body(pc_ref, ph_ref, w_ref, cb_ref, lg_ref, lb_ref, z_ref, c_ref, ubuf, vbuf):
        i = pl.program_id(0)
        _, _, u_cur = _glu(pc_ref[...], C)
        _, _, u_prev = _glu(ph_ref[...], C)
        ubuf[0:HALO, :] = jnp.where(i > 0, u_prev, 0.0)
        ubuf[HALO:HALO + tm, :] = u_cur
        ubuf[HALO + tm:, :] = jnp.zeros((_SUBLANES, C), F32)
        acc = jnp.zeros((tm, C), F32)
        for shift, taps in _tap_groups([HALO - (CONV_WIDTH - 1) + k for k in range(CONV_WIDTH)]):
            v = None
            for k, base in taps:
                term = w_ref[k:k + 1, :] * ubuf[base:base + tm + _SUBLANES, :]
                v = term if v is None else v + term
            if shift == 0:
                acc = acc + v[:tm]
            else:
                vbuf[...] = v
                acc = acc + vbuf[shift:shift + tm, :]
        z = acc + cb_ref[...]
        z_ref[...] = z
        zc = z - jnp.mean(z, axis=-1, keepdims=True)
        y = zc * lax.rsqrt(jnp.mean(zc * zc, axis=-1, keepdims=True) + EPS) * lg_ref[...] + lb_ref[...]
        c_ref[...] = (y * _sig(y)).astype(c_ref.dtype)

    return pl.pallas_call(
        body, name="conv_fwd", grid=(T // tm,),
        in_specs=[_row_spec(tm, C2), pl.BlockSpec((HALO, C2), lambda i: (jnp.maximum(i * per - 1, 0), 0)),
                  _full_spec((HALO, C)), _full_spec((1, C)), _full_spec((1, C)), _full_spec((1, C))],
        out_specs=[_row_spec(tm, C), _row_spec(tm, C)],
        out_shape=[jax.ShapeDtypeStruct((T, C), F32), jax.ShapeDtypeStruct((T, C), _MXU_DTYPE)],
        scratch_shapes=[pltpu.VMEM((tm + HALO + _SUBLANES, C), F32), pltpu.VMEM((tm + _SUBLANES, C), F32)],
        compiler_params=_params(("parallel",), 3 * _nbytes((tm, C2), F32), 10 * _nbytes((tm, C), F32)),
    )(pc, pc, w, cb, lg, lb)


def _conv_bwd(pc, z, dc, w, lg, lb):
    T, C2 = pc.shape
    C = C2 // 2
    tm = _tile(T, _ROW_TILE // 2, HALO)
    per = tm // HALO
    n = T // tm
    last_halo = T // HALO - 1

    def body(pc_ref, ph_ref, z_ref, zn_ref, dc_ref, dcn_ref, w_ref, lg_ref, lb_ref,
             dpc_ref, dw_ref, dcb_ref, dlg_ref, dlb_ref, ubuf, dzbuf, vbuf, sbuf):
        i = pl.program_id(0)
        _init_acc([dw_ref, dcb_ref, dlg_ref, dlb_ref])
        g, b = lg_ref[...], lb_ref[...]

        def ln_swish_bwd(zv, dcv):
            zc = zv - jnp.mean(zv, axis=-1, keepdims=True)
            r = lax.rsqrt(jnp.mean(zc * zc, axis=-1, keepdims=True) + EPS)
            zh = zc * r
            y = zh * g + b
            s = _sig(y)
            dy = dcv * (s * (1.0 + y * (1.0 - s)))
            dyg = dy * g
            dz = r * (dyg - jnp.mean(dyg, axis=-1, keepdims=True) - zh * jnp.mean(dyg * zh, axis=-1, keepdims=True))
            return dz, dy, zh

        dz_c, dy_c, zh_c = ln_swish_bwd(z_ref[...], dc_ref[...])
        dz_n, _, _ = ln_swish_bwd(zn_ref[...], dcn_ref[...])
        dlg_ref[...] += _colsum(dy_c * zh_c)
        dlb_ref[...] += _colsum(dy_c)
        dcb_ref[...] += _colsum(dz_c)
        dzbuf[0:tm, :] = dz_c
        dzbuf[tm:tm + HALO, :] = jnp.where(i < n - 1, dz_n, 0.0)
        a, s, u_cur = _glu(pc_ref[...], C)
        _, _, u_prev = _glu(ph_ref[...], C)
        ubuf[0:HALO, :] = jnp.where(i > 0, u_prev, 0.0)
        ubuf[HALO:HALO + tm, :] = u_cur
        ubuf[HALO + tm:, :] = jnp.zeros((_SUBLANES, C), F32)
        du = jnp.zeros((tm, C), F32)
        for shift, taps in _tap_groups([CONV_WIDTH - 1 - k for k in range(CONV_WIDTH)]):
            v = None
            for k, base in taps:
                term = w_ref[k:k + 1, :] * dzbuf[base:base + tm + _SUBLANES, :]
                v = term if v is None else v + term
            if shift == 0:
                du = du + v[:tm]
            else:
                vbuf[...] = v
                du = du + vbuf[shift:shift + tm, :]
        for shift, taps in _tap_groups([HALO - (CONV_WIDTH - 1) + k for k in range(CONV_WIDTH)]):
            if shift:
                sbuf[...] = ubuf[shift:shift + tm + HALO, :]
            src = sbuf if shift else ubuf
            for k, base in taps:
                dw_ref[k:k + 1, :] += _colsum(dz_c * src[base:base + tm, :])
        dpc_ref[:, :C] = (du * s).astype(dpc_ref.dtype)
        dpc_ref[:, C:] = (du * a * s * (1.0 - s)).astype(dpc_ref.dtype)

    nxt = lambda i: (jnp.minimum((i + 1) * per, last_halo), 0)
    vec = _full_spec((1, C))
    return pl.pallas_call(
        body, name="conv_bwd", grid=(n,),
        in_specs=[_row_spec(tm, C2), pl.BlockSpec((HALO, C2), lambda i: (jnp.maximum(i * per - 1, 0), 0)),
                  _row_spec(tm, C), pl.BlockSpec((HALO, C), nxt), _row_spec(tm, C), pl.BlockSpec((HALO, C), nxt),
                  _full_spec((HALO, C)), vec, vec],
        out_specs=[_row_spec(tm, C2), _full_spec((HALO, C)), vec, vec, vec],
        out_shape=[jax.ShapeDtypeStruct((T, C2), _MXU_DTYPE), jax.ShapeDtypeStruct((HALO, C), F32)]
                  + [jax.ShapeDtypeStruct((1, C), F32)] * 3,
        scratch_shapes=[pltpu.VMEM((tm + HALO + _SUBLANES, C), F32), pltpu.VMEM((tm + HALO + _SUBLANES, C), F32),
                        pltpu.VMEM((tm + _SUBLANES, C), F32), pltpu.VMEM((tm + HALO, C), F32)],
        compiler_params=_params(("arbitrary",), 5 * _nbytes((tm, C2), F32), 16 * _nbytes((tm, C), F32)),
    )(pc, pc, z, z, dc, dc, w, lg, lb)


def _lat_norm_fwd(pl_, gq, gkv, QL, KVL):
    T, W = pl_.shape
    tm = _tile(T, _ROW_TILE, _PACKED_ROWS)

    def body(p_ref, gq_ref, gkv_ref, cq_ref, ckv_ref):
        def norm(xv, gv):
            return xv * lax.rsqrt(jnp.mean(xv * xv, axis=-1, keepdims=True) + EPS) * gv
        cq_ref[...] = norm(p_ref[:, :QL], gq_ref[...]).astype(cq_ref.dtype)
        ckv_ref[...] = norm(p_ref[:, QL:QL + KVL], gkv_ref[...]).astype(ckv_ref.dtype)

    return pl.pallas_call(
        body, name="lat_norm_fwd", grid=(T // tm,),
        in_specs=[_row_spec(tm, W), _full_spec((1, QL)), _full_spec((1, KVL))],
        out_specs=[_row_spec(tm, QL), _row_spec(tm, KVL)],
        out_shape=[jax.ShapeDtypeStruct((T, QL), _MXU_DTYPE), jax.ShapeDtypeStruct((T, KVL), _MXU_DTYPE)],
        compiler_params=_params(("parallel",), 2 * _nbytes((tm, W), F32), 4 * _nbytes((tm, W), F32)),
    )(pl_, gq, gkv)


def _lat_norm_bwd(pl_, gq, gkv, dcq, dckv, drope, QL, KVL):
    T, W = pl_.shape
    tm = _tile(T, _ROW_TILE, _PACKED_ROWS)

    def body(p_ref, gq_ref, gkv_ref, dcq_ref, dckv_ref, dr_ref, dp_ref, dgq_ref, dgkv_ref):
        _init_acc([dgq_ref, dgkv_ref])

        def bwd(xv, gv, dv, dg_ref):
            r = lax.rsqrt(jnp.mean(xv * xv, axis=-1, keepdims=True) + EPS)
            xh = xv * r
            dxh = dv * gv
            dg_ref[...] += _colsum(dv * xh)
            return r * (dxh - xh * jnp.mean(dxh * xh, axis=-1, keepdims=True))

        dp_ref[:, :QL] = bwd(p_ref[:, :QL], gq_ref[...], dcq_ref[...], dgq_ref).astype(dp_ref.dtype)
        dp_ref[:, QL:QL + KVL] = bwd(p_ref[:, QL:QL + KVL], gkv_ref[...], dckv_ref[...], dgkv_ref).astype(dp_ref.dtype)
        dp_ref[:, QL + KVL:] = dr_ref[...].astype(dp_ref.dtype)

    return pl.pallas_call(
        body, name="lat_norm_bwd", grid=(T // tm,),
        in_specs=[_row_spec(tm, W), _full_spec((1, QL)), _full_spec((1, KVL)), _row_spec(tm, QL), _row_spec(tm, KVL),
                  _row_spec(tm, HEAD_W)],
        out_specs=[_row_spec(tm, W), _full_spec((1, QL)), _full_spec((1, KVL))],
        out_shape=[jax.ShapeDtypeStruct((T, W), _MXU_DTYPE), jax.ShapeDtypeStruct((1, QL), F32),
                   jax.ShapeDtypeStruct((1, KVL), F32)],
        compiler_params=_params(("arbitrary",), 4 * _nbytes((tm, W), F32), 6 * _nbytes((tm, W), F32)),
    )(pl_, gq, gkv, dcq, dckv, drope)


def _norm_rope(xv, gv, ct, sa, sb):
    r = lax.rsqrt(jnp.sum(xv * xv, axis=-1, keepdims=True) * (1.0 / QK_DIM) + EPS)
    xn = xv * r * gv
    half = ROPE_DIM // 2
    return xn * ct + pltpu.roll(xn, HEAD_W - half, 1) * sa + pltpu.roll(xn, half, 1) * sb


def _norm_rope_bwd(xv, gv, ct, sa, sb, dout):
    half = ROPE_DIM // 2
    dxn = dout * ct + pltpu.roll(dout * sa, half, 1) + pltpu.roll(dout * sb, HEAD_W - half, 1)
    r = lax.rsqrt(jnp.sum(xv * xv, axis=-1, keepdims=True) * (1.0 / QK_DIM) + EPS)
    xh = xv * r
    dxh = dxn * gv
    dx = r * (dxh - xh * (jnp.sum(dxh * xh, axis=-1, keepdims=True) * (1.0 / QK_DIM)))
    return dx, _colsum(dxn * xh)


def _qk_prep_fwd(q_raw, k_raw, pl_, rope_cb, ct, sa, sb, gq, gk):
    T, HW = q_raw.shape
    H = HW // HEAD_W
    tm = _tile(T, _ROW_TILE, _PACKED_ROWS)

    def body(q_ref, k_ref, r_ref, ct_ref, sa_ref, sb_ref, gq_ref, gk_ref, qn_ref, kn_ref):
        ctv, sav, sbv, rv = ct_ref[...], sa_ref[...], sb_ref[...], r_ref[...]
        for h in range(H):
            hs = slice(h * HEAD_W, (h + 1) * HEAD_W)
            qn_ref[:, hs] = _norm_rope(q_ref[:, hs], gq_ref[...], ctv, sav, sbv).astype(qn_ref.dtype)
            kn_ref[:, hs] = _norm_rope(k_ref[:, hs] + rv, gk_ref[...], ctv, sav, sbv).astype(kn_ref.dtype)

    tb = _row_spec(tm, HEAD_W)
    gb = _full_spec((1, HEAD_W))
    return pl.pallas_call(
        body, name="qk_prep_fwd", grid=(T // tm,),
        in_specs=[_row_spec(tm, HW), _row_spec(tm, HW), _row_spec(tm, HEAD_W, rope_cb), tb, tb, tb, gb, gb],
        out_specs=[_row_spec(tm, HW), _row_spec(tm, HW)],
        out_shape=[jax.ShapeDtypeStruct((T, HW), _MXU_DTYPE)] * 2,
        compiler_params=_params(("parallel",), 3 * _nbytes((tm, HW), F32) + 4 * _nbytes((tm, HEAD_W), F32), 8 * _nbytes((tm, HEAD_W), F32)),
    )(q_raw, k_raw, pl_, ct, sa, sb, gq, gk)


def _qk_prep_bwd(q_raw, k_raw, pl_, rope_cb, ct, sa, sb, gq, gk, dqn, dkn):
    T, HW = q_raw.shape
    H = HW // HEAD_W
    tm = _tile(T, _ROW_TILE, _PACKED_ROWS)

    def body(q_ref, k_ref, r_ref, ct_ref, sa_ref, sb_ref, gq_ref, gk_ref, dqn_ref, dkn_ref,
             dq_ref, dk_ref, dr_ref, dgq_ref, dgk_ref):
        _init_acc([dgq_ref, dgk_ref])
        ctv, sav, sbv, rv = ct_ref[...], sa_ref[...], sb_ref[...], r_ref[...]
        dr = dgq_sum = dgk_sum = None
        for h in range(H):
            hs = slice(h * HEAD_W, (h + 1) * HEAD_W)
            dq, dgq = _norm_rope_bwd(q_ref[:, hs], gq_ref[...], ctv, sav, sbv, dqn_ref[:, hs])
            dk, dgk = _norm_rope_bwd(k_ref[:, hs] + rv, gk_ref[...], ctv, sav, sbv, dkn_ref[:, hs])
            dq_ref[:, hs] = dq.astype(dq_ref.dtype)
            dk_ref[:, hs] = dk.astype(dk_ref.dtype)
            dr = dk if h == 0 else dr + dk
            dgq_sum = dgq if h == 0 else dgq_sum + dgq
            dgk_sum = dgk if h == 0 else dgk_sum + dgk
        dr_ref[...] = dr
        dgq_ref[...] += dgq_sum
        dgk_ref[...] += dgk_sum

    tb = _row_spec(tm, HEAD_W)
    gb = _full_spec((1, HEAD_W))
    hb = _row_spec(tm, HW)
    return pl.pallas_call(
        body, name="qk_prep_bwd", grid=(T // tm,),
        in_specs=[hb, hb, _row_spec(tm, HEAD_W, rope_cb), tb, tb, tb, gb, gb, hb, hb],
        out_specs=[hb, hb, tb, gb, gb],
        out_shape=[jax.ShapeDtypeStruct((T, HW), _MXU_DTYPE)] * 2 + [jax.ShapeDtypeStruct((T, HEAD_W), F32)]
                  + [jax.ShapeDtypeStruct((1, HEAD_W), F32)] * 2,
        compiler_params=_params(("arbitrary",), 5 * _nbytes((tm, HW), F32) + 5 * _nbytes((tm, HEAD_W), F32), 12 * _nbytes((tm, HEAD_W), F32)),
    )(q_raw, k_raw, pl_, ct, sa, sb, gq, gk, dqn, dkn)


_NEG = -1e30
_LOG2E = 1.4426950408889634
_SCORE_C = QK_DIM ** -0.5 * _LOG2E


def _pieces(t, rb, diag):
    assert CHUNK % rb == 0 and rb % _SUBLANES == 0
    out = []
    for r in range(t // rb):
        lo = (r * rb // CHUNK) * CHUNK if diag else 0
        for c in range(t // _LANES):
            cut = None if (c + 1) * _LANES <= lo else max(lo - c * _LANES, 0)
            out.append((slice(r * rb, (r + 1) * rb), slice(c * _LANES, (c + 1) * _LANES), c, cut))
    return out


def _groups(x):
    return x.reshape(x.shape[0] // _SUBLANES, _SUBLANES, x.shape[1])


def _all_sublanes(x8, op):
    return jnp.broadcast_to(op(x8, axis=0, keepdims=True), x8.shape)


def _lane_ge(rb, cut):
    return lax.broadcasted_iota(jnp.int32, (rb, _LANES), 1) >= cut


def _attn_fwd(q, k, v):
    T, HW = q.shape
    H = HW // HEAD_W
    t = _tile(T, _ATT_TILE)
    rb = min(_ATT_ROWS, t)
    nc = t // _LANES

    hp = _ATT_HEADS_FWD if H % _ATT_HEADS_FWD == 0 else 1
    W = hp * HEAD_W

    def body(q_ref, k_ref, v_ref, o_ref, lse_ref, s_s, p_s, m_s, l_s, acc_s):
        i = pl.program_id(1)
        m_s[...] = jnp.full_like(m_s, _NEG)
        l_s[...] = jnp.zeros_like(l_s)
        acc_s[...] = jnp.zeros_like(acc_s)

        def head_step(hh, rows, diag):
            hs = slice(hh * HEAD_W, (hh + 1) * HEAD_W)
            s_s[hh] = _dot(k_ref[rows, hs], q_ref[:, hs], 'nt')
            pieces = _pieces(t, rb, diag)

            def scores(rs, cs, cut):
                sb = s_s[hh, rs, cs]
                return jnp.where(_lane_ge(rb, cut), sb, _NEG) if cut else sb

            mx = [None] * nc
            for rs, cs, c, cut in pieces:
                if cut is not None:
                    g = jnp.max(_groups(scores(rs, cs, cut)), axis=0)
                    mx[c] = g if mx[c] is None else jnp.maximum(mx[c], g)
            m_new, alpha = [], []
            for c in range(nc):
                cs = slice(c * _LANES, (c + 1) * _LANES)
                m_prev = m_s[hh, :, cs]
                m_new.append(jnp.maximum(m_prev, _all_sublanes(mx[c], jnp.max)))
                alpha.append(jnp.exp2((m_prev - m_new[c]) * _SCORE_C))
                m_s[hh, :, cs] = m_new[c]
            lsum = [jnp.zeros((_SUBLANES, _LANES), F32)] * nc
            for rs, cs, c, cut in pieces:
                if cut is None:
                    p_s[hh, rs, cs] = jnp.zeros((rb, _LANES), p_s.dtype)
                    continue
                p = jnp.exp2((scores(rs, cs, cut) - jnp.tile(m_new[c], (rb // _SUBLANES, 1))) * _SCORE_C)
                lsum[c] = lsum[c] + jnp.sum(_groups(p), axis=0)
                p_s[hh, rs, cs] = p.astype(p_s.dtype)
            for c in range(nc):
                cs = slice(c * _LANES, (c + 1) * _LANES)
                l_s[hh, :, cs] = alpha[c] * l_s[hh, :, cs] + _all_sublanes(lsum[c], jnp.sum)
            a = jnp.tile(jnp.concatenate(alpha, axis=1), (HEAD_W // _SUBLANES, 1))
            acc_s[hh] = a * acc_s[hh] + _dot(v_ref[rows, hs], p_s[hh], 'tn')

        def step(j, diag):
            rows = pl.ds(pl.multiple_of(j * t, t), t)
            for hh in range(hp):
                head_step(hh, rows, diag)

        def loop_body(j, carry):
            step(j, False)
            return carry

        lax.fori_loop(0, i, loop_body, 0)
        step(i, True)
        for hh in range(hp):
            l = l_s[hh]
            o_t = acc_s[hh] / jnp.tile(l, (HEAD_W // _SUBLANES, 1))
            o_ref[:, hh * HEAD_W:(hh + 1) * HEAD_W] = o_t.T.astype(o_ref.dtype)
            lse_ref[hh * _SUBLANES:(hh + 1) * _SUBLANES, :] = m_s[hh] * _SCORE_C + jnp.log(l) * _LOG2E

    qb = pl.BlockSpec((t, W), lambda h, i: (i, h))
    kb = pl.BlockSpec((T, W), lambda h, i: (0, h))
    st = pltpu.VMEM((hp, _SUBLANES, t), F32)
    return pl.pallas_call(
        body, name="attn_fwd", grid=(H // hp, T // t),
        in_specs=[qb, kb, kb], out_specs=[qb, pl.BlockSpec((hp * _SUBLANES, t), lambda h, i: (h, i))],
        out_shape=[jax.ShapeDtypeStruct((T, HW), _MXU_DTYPE), jax.ShapeDtypeStruct((H * _SUBLANES, T), F32)],
        scratch_shapes=[pltpu.VMEM((hp, t, t), F32), pltpu.VMEM((hp, t, t), _MXU_DTYPE), st, st, pltpu.VMEM((hp, HEAD_W, t), F32)],
        compiler_params=_params(("parallel", "arbitrary"), 2 * _nbytes((T, W), _MXU_DTYPE) + 3 * _nbytes((t, W), F32),
                                4 * hp * _nbytes((t, t), F32)),
    )(q, k, v)


def _attn_bwd(q, k, v, o, do, lse2):
    T, HW = q.shape
    H = HW // HEAD_W
    t = _tile(T, _ATT_TILE)
    n = T // t
    rb = min(_ATT_ROWS, t)
    scale = QK_DIM ** -0.5

    hp = _ATT_HEADS if H % _ATT_HEADS == 0 else 1
    W = hp * HEAD_W

    def body(k_ref, v_ref, q_ref, o_ref, do_ref, lse_ref, dq_ref, dk_ref, dv_ref,
             s_s, dp_s, p_s, ds_s, dk_s, dv_s, delta_s, dqt_s):
        j = pl.program_id(1)

        @pl.when(j == 0)
        def _():
            dqt_s[...] = jnp.zeros_like(dqt_s)
            for hh in range(hp):
                hs = slice(hh * HEAD_W, (hh + 1) * HEAD_W)
                for b in range(n):
                    rs = slice(b * t, (b + 1) * t)
                    d = jnp.sum(do_ref[rs, hs].astype(F32) * o_ref[rs, hs].astype(F32), axis=1, keepdims=True)
                    delta_s[hh, :, rs] = jnp.broadcast_to(d, (t, HEAD_W)).T[0:_SUBLANES, :]

        dk_s[...] = jnp.zeros_like(dk_s)
        dv_s[...] = jnp.zeros_like(dv_s)

        def head_step(hh, base, diag):
            hs = slice(hh * HEAD_W, (hh + 1) * HEAD_W)
            kv, vv = k_ref[:, hs], v_ref[:, hs]
            qv, dov = q_ref[pl.ds(base, t), hs], do_ref[pl.ds(base, t), hs]
            s_s[hh] = _dot(kv, qv, 'nt')
            dp_s[hh] = _dot(vv, dov, 'nt')
            for rs, cs, c, cut in _pieces(t, rb, diag):
                if cut is None:
                    p_s[hh, rs, cs] = jnp.zeros((rb, _LANES), p_s.dtype)
                    ds_s[hh, rs, cs] = jnp.zeros((rb, _LANES), ds_s.dtype)
                    continue
                lanes = pl.ds(pl.multiple_of(base + c * _LANES, _LANES), _LANES)
                lse = jnp.tile(lse_ref[hh * _SUBLANES:(hh + 1) * _SUBLANES, lanes], (rb // _SUBLANES, 1))
                dl = jnp.tile(delta_s[hh, :, lanes], (rb // _SUBLANES, 1))
                p = jnp.exp2(s_s[hh, rs, cs] * _SCORE_C - lse)
                if cut:
                    p = jnp.where(_lane_ge(rb, cut), p, 0.0)
                p_s[hh, rs, cs] = p.astype(p_s.dtype)
                ds_s[hh, rs, cs] = (p * (dp_s[hh, rs, cs] - dl)).astype(ds_s.dtype)
            dsv = ds_s[hh]
            dv_s[hh] += _dot(p_s[hh], dov, 'nn')
            dk_s[hh] += _dot(dsv, qv, 'nn')
            dqt_s[hh, :, pl.ds(base, t)] += _dot(kv, dsv, 'tn')

        def step(i, diag):
            base = pl.multiple_of(i * t, t)
            for hh in range(hp):
                head_step(hh, base, diag)

        def loop_body(i, carry):
            step(i, False)
            return carry

        step(j, True)
        lax.fori_loop(j + 1, n, loop_body, 0)
        for hh in range(hp):
            hs = slice(hh * HEAD_W, (hh + 1) * HEAD_W)
            dk_ref[:, hs] = dk_s[hh] * scale
            dv_ref[:, hs] = dv_s[hh].astype(dv_ref.dtype)

        @pl.when(j == n - 1)
        def _():
            for hh in range(hp):
                for b in range(n):
                    rs = slice(b * t, (b + 1) * t)
                    dq_ref[rs, hh * HEAD_W:(hh + 1) * HEAD_W] = dqt_s[hh, :, rs].T * scale

    jb = pl.BlockSpec((t, W), lambda h, j: (j, h))
    fb = pl.BlockSpec((T, W), lambda h, j: (0, h))
    sc = pltpu.VMEM((hp, t, HEAD_W), F32)
    return pl.pallas_call(
        body, name="attn_bwd", grid=(H // hp, n),
        in_specs=[jb, jb, fb, fb, fb, pl.BlockSpec((hp * _SUBLANES, T), lambda h, j: (h, 0))], out_specs=[fb, jb, jb],
        out_shape=[jax.ShapeDtypeStruct((T, HW), F32), jax.ShapeDtypeStruct((T, HW), F32),
                   jax.ShapeDtypeStruct((T, HW), _MXU_DTYPE)],
        scratch_shapes=[pltpu.VMEM((hp, t, t), F32), pltpu.VMEM((hp, t, t), F32), pltpu.VMEM((hp, t, t), _MXU_DTYPE),
                        pltpu.VMEM((hp, t, t), _MXU_DTYPE), sc, sc, pltpu.VMEM((hp, _SUBLANES, T), F32),
                        pltpu.VMEM((hp, HEAD_W, T), F32)],
        compiler_params=_params(("parallel", "arbitrary"), 3 * _nbytes((T, W), _MXU_DTYPE) + _nbytes((T, W), F32)
                                + 4 * _nbytes((t, W), F32), hp * (4 * _nbytes((t, t), F32) + _nbytes((T, HEAD_W), F32))),
    )(k, v, q, o, do, lse2)


def _gate_fwd(pg, gb, yc, ym):
    T, D = yc.shape
    tm = _tile(T, _ROW_TILE, _PACKED_ROWS)

    def body(g0_ref, g1_ref, gb_ref, yc_ref, ym_ref, y_ref):
        s0 = _sig(g0_ref[...] + gb_ref[0:1, :])
        s1 = _sig(g1_ref[...] + gb_ref[1:2, :])
        y_ref[...] = (s0 * yc_ref[...] + s1 * ym_ref[...]).astype(y_ref.dtype)

    return pl.pallas_call(
        body, name="gate_fwd", grid=(T // tm,),
        in_specs=[_row_spec(tm, D, 0), _row_spec(tm, D, 1), _full_spec((2, D)), _row_spec(tm, D), _row_spec(tm, D)],
        out_specs=_row_spec(tm, D),
        out_shape=jax.ShapeDtypeStruct((T, D), _MXU_DTYPE),
        compiler_params=_params(("parallel",), 5 * _nbytes((tm, D), F32), 4 * _nbytes((tm, D), F32)),
    )(pg, pg, gb, yc, ym)


def _gate_bwd(pg, gb, yc, ym, dy):
    T, D = yc.shape
    tm = _tile(T, _ROW_TILE, _PACKED_ROWS)

    def body(g0_ref, g1_ref, gb_ref, yc_ref, ym_ref, dy_ref, dyc_ref, dym_ref, dpg_ref, dgb_ref):
        _init_acc([dgb_ref])
        dyv = dy_ref[...]
        s0 = _sig(g0_ref[...] + gb_ref[0:1, :])
        s1 = _sig(g1_ref[...] + gb_ref[1:2, :])
        dyc_ref[...] = (dyv * s0).astype(dyc_ref.dtype)
        dym_ref[...] = (dyv * s1).astype(dym_ref.dtype)
        d0 = dyv * yc_ref[...] * s0 * (1.0 - s0)
        d1 = dyv * ym_ref[...] * s1 * (1.0 - s1)
        dpg_ref[:, :D] = d0.astype(dpg_ref.dtype)
        dpg_ref[:, D:] = d1.astype(dpg_ref.dtype)
        dgb_ref[0:1, :] += _colsum(d0)
        dgb_ref[1:2, :] += _colsum(d1)

    return pl.pallas_call(
        body, name="gate_bwd", grid=(T // tm,),
        in_specs=[_row_spec(tm, D, 0), _row_spec(tm, D, 1), _full_spec((2, D)), _row_spec(tm, D), _row_spec(tm, D),
                  _row_spec(tm, D)],
        out_specs=[_row_spec(tm, D), _row_spec(tm, D), _row_spec(tm, 2 * D), _full_spec((2, D))],
        out_shape=[jax.ShapeDtypeStruct((T, D), _MXU_DTYPE)] * 2 + [jax.ShapeDtypeStruct((T, 2 * D), _MXU_DTYPE),
                                                                   jax.ShapeDtypeStruct((2, D), F32)],
        compiler_params=_params(("arbitrary",), 7 * _nbytes((tm, D), F32), 8 * _nbytes((tm, D), F32)),
    )(pg, pg, gb, yc, ym, dy)


def _place():
    return lax.axis_index("x"), lax.axis_index("y"), lax.axis_index("c")


def _all_gather(name, xs):
    R, C = xs.shape
    MESH = pl.DeviceIdType.MESH

    def body(x_ref, out_ref, send_sems, recv_sems, local_sem):
        x, y, c = _place()
        me, sibling = (x, y, c), (x, y, 1 - c)
        chips = [(1 - x, y), (x, 1 - y), (1 - x, 1 - y)]

        def rows(px, py, pc):
            return out_ref.at[4 * px + 2 * py + pc]

        def copy(k, block, to, src=None):
            return pltpu.make_async_remote_copy(
                src_ref=rows(*block) if src is None else src, dst_ref=rows(*block),
                send_sem=send_sems.at[k], recv_sem=recv_sems.at[k], device_id=to, device_id_type=MESH)

        mine = pltpu.make_async_copy(x_ref, rows(*me), local_sem)
        mine.start()
        first = [copy(0, me, sibling, src=x_ref)]
        first += [copy(1 + j, me, (*chip, c), src=x_ref) for j, chip in enumerate(chips)]
        for cp in first:
            cp.start()
        passed = [copy(4 + j, (*chip, c), sibling) for j, chip in enumerate(chips)]
        for j, chip in enumerate(chips):
            copy(1 + j, (*chip, c), me).wait_recv()
            passed[j].start()
        copy(0, sibling, me).wait_recv()
        for j, chip in enumerate(chips):
            copy(4 + j, (*chip, 1 - c), me).wait_recv()
        for cp in first + passed:
            cp.wait_send()
        mine.wait()

    return pl.pallas_call(
        body, name=name,
        out_shape=jax.ShapeDtypeStruct((N_DEV, R, C), xs.dtype),
        in_specs=[pl.BlockSpec(memory_space=pl.ANY)], out_specs=pl.BlockSpec(memory_space=pl.ANY),
        scratch_shapes=[pltpu.SemaphoreType.DMA((7,)), pltpu.SemaphoreType.DMA((7,)), pltpu.SemaphoreType.DMA],
    )(xs)


_HBM_SPEC = pl.BlockSpec(memory_space=pltpu.HBM)
_SEM_SPEC = pl.BlockSpec(memory_space=pltpu.SEMAPHORE)
_DATAFLOW = pltpu.SideEffectType.DATAFLOW_SIDE_EFFECTING


def _peers():
    x, y, c = _place()
    out = []
    for k in range(1, N_DEV):
        px = 1 - x if k & 4 else x
        py = 1 - y if k & 2 else y
        pc = 1 - c if k & 1 else c
        out.append((k - 1, (px, py, pc), 4 * px + 2 * py + pc))
    return 4 * x + 2 * y + c, out


def _exchange_copies(x_ref, land_ref, send_sems, recv_sems, scatter):
    me, peers = _peers()
    return [pltpu.make_async_remote_copy(
        src_ref=x_ref.at[idx] if scatter else x_ref, dst_ref=land_ref.at[me],
        send_sem=send_sems.at[k], recv_sem=recv_sems.at[k], device_id=dev, device_id_type=pl.DeviceIdType.MESH)
        for k, dev, idx in peers]


def _exchange_start(name, xs, land, scatter):
    def body(x_ref, land_ref, send_sems, recv_sems, x_thru, land_thru, token):
        for cp in _exchange_copies(x_ref, land_ref, send_sems, recv_sems, scatter):
            cp.start()
        token[...] = jnp.zeros_like(token)

    sems = pltpu.SemaphoreType.DMA((N_DEV - 1,))
    return pl.pallas_call(
        body, name=name,
        out_shape=(sems, sems, pltpu.HBM(xs.shape, xs.dtype), pltpu.HBM(land.shape, land.dtype),
                   jax.ShapeDtypeStruct((_SUBLANES, _LANES), F32)),
        in_specs=(_HBM_SPEC, _HBM_SPEC),
        out_specs=(_SEM_SPEC, _SEM_SPEC, _HBM_SPEC, _HBM_SPEC, pl.BlockSpec(memory_space=pltpu.VMEM)),
        input_output_aliases={0: 2, 1: 3},
        compiler_params=pltpu.CompilerParams(has_side_effects=_DATAFLOW),
    )(pltpu.with_memory_space_constraint(xs, pltpu.HBM), pltpu.with_memory_space_constraint(land, pltpu.HBM))


def _exchange_wait(name, send_sems, recv_sems, xs, land, after, scatter):
    def body(x_ref, land_ref, send_sems, recv_sems, after_ref, x_dead, got_ref):
        for cp in _exchange_copies(x_ref, land_ref, send_sems, recv_sems, scatter):
            cp.wait_send()
            cp.wait_recv()

    return pl.pallas_call(
        body, name=name,
        out_shape=(pltpu.HBM(xs.shape, xs.dtype), pltpu.HBM(land.shape, land.dtype)),
        in_specs=(_HBM_SPEC, _HBM_SPEC, _SEM_SPEC, _SEM_SPEC, pl.BlockSpec(memory_space=pl.ANY)),
        out_specs=(_HBM_SPEC, _HBM_SPEC), input_output_aliases={0: 0, 1: 1},
        compiler_params=pltpu.CompilerParams(has_side_effects=_DATAFLOW),
    )(xs, land, send_sems, recv_sems, after)[1]


def _own_slot(block, me):
    land = lax.empty((N_DEV,) + block.shape, block.dtype)
    return lax.dynamic_update_slice(land, block[None], (me,) + (0,) * block.ndim)


def _relay_copies(x_ref, land_ref, sems_a, sems_b):
    x, y, c = _place()
    sib = (x, y, 1 - c)
    chips = [(1 - x, y), (x, 1 - y), (1 - x, 1 - y)]
    mesh = pl.DeviceIdType.MESH
    (send_a, recv_a), (send_b, recv_b) = sems_a, sems_b or (None, None)
    first =[pltpu.make_async_remote_copy(src_ref=x_ref, dst_ref=land_ref.at[4 * x + 2 * y + c], send_sem=send_a.at[k],
                                          recv_sem=recv_a.at[k], device_id=dev, device_id_type=mesh)
             for k, dev in enumerate([sib] + [(*ch, c) for ch in chips])]
    relayed = [pltpu.make_async_remote_copy(src_ref=land_ref.at[4 * ch[0] + 2 * ch[1] + c], dst_ref=land_ref.at[4 * ch[0] + 2 * ch[1] + c],
                                            send_sem=send_b.at[j], recv_sem=recv_b.at[j], device_id=sib, device_id_type=mesh)
               for j, ch in enumerate(chips)] if sems_b else []
    return first, relayed


def _relay_start(name, xs, land):
    def body(x_ref, land_ref, send_a, recv_a, x_thru, land_thru, token):
        for cp in _relay_copies(x_ref, land_ref, (send_a, recv_a), None)[0]:
            cp.start()
        token[...] = jnp.zeros_like(token)

    sems = pltpu.SemaphoreType.DMA((4,))
    return pl.pallas_call(
        body, name=name,
        out_shape=(sems, sems, pltpu.HBM(xs.shape, xs.dtype), pltpu.HBM(land.shape, land.dtype),
                   jax.ShapeDtypeStruct((_SUBLANES, _LANES), F32)),
        in_specs=(_HBM_SPEC, _HBM_SPEC),
        out_specs=(_SEM_SPEC, _SEM_SPEC, _HBM_SPEC, _HBM_SPEC, pl.BlockSpec(memory_space=pltpu.VMEM)),
        input_output_aliases={0: 2, 1: 3},
        compiler_params=pltpu.CompilerParams(has_side_effects=_DATAFLOW),
    )(pltpu.with_memory_space_constraint(xs, pltpu.HBM), pltpu.with_memory_space_constraint(land, pltpu.HBM))


def _relay_forward(name, send_a, recv_a, xs, land, after):
    def body(x_ref, land_ref, send_a, recv_a, after_ref, x_thru, land_thru, send_b, recv_b):
        first, relayed = _relay_copies(x_ref, land_ref, (send_a, recv_a), (send_b, recv_b))
        for arrived, onward in zip(first[1:], relayed):
            arrived.wait_recv()
            onward.start()

    sems = pltpu.SemaphoreType.DMA((3,))
    return pl.pallas_call(
        body, name=name,
        out_shape=(pltpu.HBM(xs.shape, xs.dtype), pltpu.HBM(land.shape, land.dtype), sems, sems),
        in_specs=(_HBM_SPEC, _HBM_SPEC, _SEM_SPEC, _SEM_SPEC, pl.BlockSpec(memory_space=pl.ANY)),
        out_specs=(_HBM_SPEC, _HBM_SPEC, _SEM_SPEC, _SEM_SPEC), input_output_aliases={0: 0, 1: 1},
        compiler_params=pltpu.CompilerParams(has_side_effects=_DATAFLOW),
    )(xs, land, send_a, recv_a, after)


def _relay_wait(name, send_a, recv_a, send_b, recv_b, xs, land):
    def body(x_ref, land_ref, send_a, recv_a, send_b, recv_b, x_dead, got_ref):
        first, relayed = _relay_copies(x_ref, land_ref, (send_a, recv_a), (send_b, recv_b))
        for cp in first:
            cp.wait_send()
        first[0].wait_recv()
        for cp in relayed:
            cp.wait_send()
            cp.wait_recv()

    return pl.pallas_call(
        body, name=name,
        out_shape=(pltpu.HBM(xs.shape, xs.dtype), pltpu.HBM(land.shape, land.dtype)),
        in_specs=(_HBM_SPEC, _HBM_SPEC, _SEM_SPEC, _SEM_SPEC, _SEM_SPEC, _SEM_SPEC),
        out_specs=(_HBM_SPEC, _HBM_SPEC), input_output_aliases={0: 0, 1: 1},
        compiler_params=pltpu.CompilerParams(has_side_effects=_DATAFLOW),
    )(xs, land, send_a, recv_a, send_b, recv_b)[1]


def _sum_blocks(name, parts, scale=None):
    n, R, C = parts.shape
    tr = _tile(R, _PACK_ROW_MULT, _PACKED_ROWS)

    def body(p_ref, o_ref):
        acc = p_ref[0].astype(F32)
        for s in range(1, n):
            acc = acc + p_ref[s].astype(F32)
        if scale is not None:
            acc = acc * scale
        o_ref[...] = acc

    return pl.pallas_call(
        body, name=name, grid=(R // tr,),
        in_specs=[pl.BlockSpec((n, tr, C), lambda i: (0, i, 0))], out_specs=pl.BlockSpec((tr, C), lambda i: (i, 0)),
        out_shape=jax.ShapeDtypeStruct((R, C), F32),
        compiler_params=_params(("parallel",), _nbytes((n, tr, C), parts.dtype) + _nbytes((tr, C), F32), 2 * _nbytes((tr, C), F32)),
    )(parts)


def _adamw_math(wv, gv, mv, vv):
    mv = ADAM_B1 * mv + (1.0 - ADAM_B1) * gv
    vv = ADAM_B2 * vv + (1.0 - ADAM_B2) * (gv * gv)
    m_hat = mv / (1.0 - ADAM_B1 ** ADAM_STEP)
    v_hat = vv / (1.0 - ADAM_B2 ** ADAM_STEP)
    return -ADAM_LR * (m_hat / (jnp.sqrt(v_hat) + ADAM_EPS) + ADAM_WD * wv), mv, vv


def _adamw_layer(l, w, g, m, v, outs, after):
    depth, R, C = w.shape
    tr = _tile(R, _ROW_TILE, _SUBLANES)

    def body(w_ref, g_ref, m_ref, v_ref, after_ref, *rest):
        go_ref, d_ref, nm_ref, nv_ref = rest[-4:]
        gv = g_ref[...]
        go_ref[...] = gv
        d_ref[...], nm_ref[...], nv_ref[...] = _adamw_math(w_ref[...], gv, m_ref[...], v_ref[...])

    lay = pl.BlockSpec((None, tr, C), lambda i: (l, i, 0))
    shape = jax.ShapeDtypeStruct(w.shape, F32)
    return pl.pallas_call(
        body, name="adamw_layer", grid=(R // tr,),
        in_specs=[lay, pl.BlockSpec((tr, C), lambda i: (i, 0)), lay, lay] + [pl.BlockSpec(memory_space=pl.ANY)] * (5 if outs else 1),
        out_specs=[lay] * 4, out_shape=[shape] * 4,
        input_output_aliases={5 + k: k for k in range(4)} if outs else {},
        compiler_params=_params(("parallel",), 8 * _nbytes((tr, C), F32), 4 * _nbytes((tr, C), F32)),
    )(w, g, m, v, after, *(outs or ()))


def _adamw(w, g, m, v):
    shape = w.shape
    w2, g2, m2, v2 = (t.reshape(-1, shape[-1]) for t in (w, g, m, v))
    R, C = w2.shape
    tr = _tile(R, _ROW_TILE, _SUBLANES)

    def body(w_ref, g_ref, m_ref, v_ref, d_ref, nm_ref, nv_ref):
        d_ref[...], nm_ref[...], nv_ref[...] = _adamw_math(w_ref[...], g_ref[...], m_ref[...], v_ref[...])

    spec = pl.BlockSpec((tr, C), lambda i: (i, 0))
    outs = pl.pallas_call(
        body, name="adamw", grid=(R // tr,),
        in_specs=[spec] * 4, out_specs=[spec] * 3,
        out_shape=[jax.ShapeDtypeStruct((R, C), F32)] * 3,
        compiler_params=_params(("parallel",), 7 * _nbytes((tr, C), F32), 4 * _nbytes((tr, C), F32)),
    )(w2, g2, m2, v2)
    return tuple(o.reshape(shape) for o in outs)


def _pack_rows(parts, lead=()):
    out, spans, r0 = [], [], 0
    unit = _PACKED_ROWS * _PACK_W
    for p in parts:
        flat = p.reshape(lead + (-1,))
        size = flat.shape[-1]
        pad = (-size) % unit
        if pad:
            flat = jnp.pad(flat, [(0, 0)] * len(lead) + [(0, pad)])
        rows = (size + pad) // _PACK_W
        out.append(flat.reshape(lead + (rows, _PACK_W)))
        spans.append((r0, rows, p.shape[len(lead):]))
        r0 += rows
    pad = (-r0) % _PACK_ROW_MULT
    if pad:
        out.append(jnp.zeros(lead + (pad, _PACK_W), out[0].dtype))
    return jnp.concatenate(out, axis=len(lead)), spans


def _unpack_rows(packed, span, lead=()):
    r0, rows, shape = span
    size = 1
    for s in shape:
        size *= s
    sl = lax.slice_in_dim(packed, r0, r0 + rows, axis=len(lead))
    return sl.reshape(lead + (-1,))[..., :size].reshape(lead + tuple(shape))


def _pack_flat(parts):
    flat, spans, at = [], [], 0
    for p in parts:
        size = p.size
        padded = size + (-size) % _LANES
        flat.append(jnp.pad(p.reshape(-1), (0, padded - size)))
        spans.append((at, size, p.shape))
        at += padded
    tail = (-at) % (_PACKED_ROWS * _PACK_W)
    if tail:
        flat.append(jnp.zeros((tail,), flat[0].dtype))
    return jnp.concatenate(flat).reshape(-1, _PACK_W), spans


def _unpack_flat(packed, span, lead=()):
    at, size, shape = span
    return lax.slice_in_dim(packed.reshape(lead + (-1,)), at, at + size, axis=len(lead)).reshape(lead + tuple(shape))


def _rope_tables(positions, T):
    inv_freq = ROPE_THETA ** (-jnp.arange(0, ROPE_DIM, 2, dtype=F32) / ROPE_DIM)
    ang = positions.reshape(T, 1).astype(F32) * inv_freq
    cos, sin = jnp.cos(ang), jnp.sin(ang)
    half = ROPE_DIM // 2
    z = lambda n: jnp.zeros((T, n), F32)
    tail = HEAD_W - QK_DIM
    ct = jnp.concatenate([jnp.ones((T, NOPE_DIM), F32), cos, cos, z(tail)], axis=1)
    sa = jnp.concatenate([z(NOPE_DIM), -sin, z(half), z(tail)], axis=1)
    sb = jnp.concatenate([z(NOPE_DIM), z(half), sin, z(tail)], axis=1)
    return ct, sa, sb


def _pad_heads(wt, per_head, keep_from=0, keep=None):
    K = wt.shape[1]
    keep = per_head if keep is None else keep
    w3 = wt.reshape(N_HEADS, per_head, K)[:, keep_from:keep_from + keep]
    return jnp.pad(w3, ((0, 0), (0, HEAD_W - keep), (0, 0))).reshape(N_HEADS * HEAD_W, K)


def _unpad_heads(g, keep):
    return g.reshape(N_HEADS, HEAD_W, g.shape[1])[:, :keep]


def _layer_weights(W):
    D = W['w_out'].shape[1]
    C = W['w_conv_out'].shape[1]
    QL, KVL = W['w_uq'].shape[1], W['w_ukv'].shape[1]
    o1 = 2 * C
    o2, o3 = o1 + QL, o1 + QL + KVL
    o4 = o3 + ROPE_DIM
    win = W['w_in']
    zr = lambda n: jnp.zeros((n, D), win.dtype)
    L = dict(W)
    L['w_c'] = win[:o1]
    L['w_l'] = jnp.concatenate([win[o1:o3], zr(NOPE_DIM), win[o3:o4], zr(HEAD_W - QK_DIM)], axis=0)
    L['w_g'] = win[o4:]
    L['w_q'] = _pad_heads(W['w_uq'], QK_DIM)
    L['w_k'] = _pad_heads(W['w_ukv'], NOPE_DIM + V_DIM, 0, NOPE_DIM)
    L['w_v'] = _pad_heads(W['w_ukv'], NOPE_DIM + V_DIM, NOPE_DIM, V_DIM)
    wmo = W['w_mla_out'].reshape(D, N_HEADS, V_DIM)
    L['w_mo'] = jnp.pad(wmo, ((0, 0), (0, 0), (0, HEAD_W - V_DIM))).reshape(D, N_HEADS * HEAD_W)
    L['dims'] = (D, C, QL, KVL)
    return L


def _row(v, width=None):
    v = v.reshape(1, -1)
    if width is not None and v.shape[1] < width:
        v = jnp.pad(v, ((0, 0), (0, width - v.shape[1])))
    return v


def _ffn_fwd(x, h, wg, wu, wd, next_g):
    a, b, u = _ffn_up(h, wg, wu)
    if next_g is None:
        x_out, h_out = _mm("ffn_down", u, wd, 'nn', res=x, scale=0.5), None
    else:
        x_out, h_out = _mm_res_norm("ffn_down_norm", u, wd, x, 0.5, _row(next_g))
    return x_out, h_out, (x, h, a, b, u)


def _ffn_bwd_weights(dx, saved, wd, after):
    x, h, a, b, u = saved
    d_wd = _wgrad("ffn_dwd", u, dx, scale=0.5)
    da, db = _ffn_bwd_up(dx, wd, a, b, after)
    return da, db, (_wgrad("ffn_dwg", da, h), _wgrad("ffn_dwg", db, h), d_wd)


def _ffn_bwd_input(dx, saved, g, wg, wu, da, db, after):
    return _dh_rms_bwd("ffn_dh_rms", [(da, wg), (db, wu)], saved[0], _row(g), dx, after)


def _layer_fwd_rest(x1, h, ffn1, L, S, tabs, next_g):
    D, C, QL, KVL = L['dims']
    rope_cb = (QL + KVL) // HEAD_W
    pc = _mm("proj_c", h, L['w_c'], 'nt')
    pl_ = _mm("proj_l", h, L['w_l'], 'nt')
    pg = _mm("proj_g", h, L['w_g'], 'nt')
    conv_w = jnp.pad(S['conv_w'], ((0, HALO - CONV_WIDTH), (0, 0)))
    z, c = _conv_fwd(pc, conv_w, _row(S['conv_b']), _row(S['conv_ln_g']), _row(S['conv_ln_b']))
    yc = _mm("conv_out", c, L['w_conv_out'], 'nt')
    cqn, ckvn = _lat_norm_fwd(pl_, _row(S['cq_norm']), _row(S['ckv_norm']), QL, KVL)
    q_raw = _mm("up_q", cqn, L['w_q'], 'nt')
    k_raw = _mm("up_kv", ckvn, L['w_k'], 'nt')
    v = _mm("up_kv", ckvn, L['w_v'], 'nt', out_dtype=_MXU_DTYPE)
    gq, gk = _row(S['q_norm'], HEAD_W), _row(S['k_norm'], HEAD_W)
    qn, kn = _qk_prep_fwd(q_raw, k_raw, pl_, rope_cb, *tabs, gq, gk)
    o, lse = _attn_fwd(qn, kn, v)
    ym = _mm("mla_out", o, L['w_mo'], 'nt')
    y = _gate_fwd(pg, S['gate_bias'], yc, ym)
    x2, h2 = _mm_res_norm("mix_out_norm", y, L['w_out'], x1, 1.0, _row(S['ffn2_norm']))
    x3, h3, ffn2 = _ffn_fwd(x2, h2, L['ffn2_w_gate'], L['ffn2_w_up'], L['ffn2_w_down'], next_g)
    saved = dict(ffn1=ffn1, ffn2=ffn2, x1=x1, h=h, pc=pc, pl=pl_, pg=pg, z=z, c=c, yc=yc, cqn=cqn, ckvn=ckvn,
                 q_raw=q_raw, k_raw=k_raw, v=v, qn=qn, kn=kn, o=o, lse=lse, ym=ym, y=y, conv_w=conv_w, gq=gq, gk=gk)
    return x3, h3, saved


def _layer_bwd_mix(dx, L, S, tabs, A, after):
    D, C, QL, KVL = L['dims']
    rope_cb = (QL + KVL) // HEAD_W
    G, g = {}, {}
    da, db, (G['ffn2_w_gate'], G['ffn2_w_up'], G['ffn2_w_down']) = _ffn_bwd_weights(dx, A['ffn2'], L['ffn2_w_down'], after)
    dx, g['ffn2_norm'] = _ffn_bwd_input(dx, A['ffn2'], S['ffn2_norm'], L['ffn2_w_gate'], L['ffn2_w_up'], da, db, after)
    G['w_out'] = _wgrad("mix_dwout", A['y'], dx)
    dy = _mm("mix_dy", dx, L['w_out'], 'nt')
    dyc, dym, dpg, g['gate_bias'] = _gate_bwd(A['pg'], S['gate_bias'], A['yc'], A['ym'], dy)
    G['w_conv_out'] = _wgrad("conv_dwout", dyc, A['c'])
    dc = _mm("conv_dc", dyc, L['w_conv_out'], 'nn')
    dpc, dcw, g['conv_b'], g['conv_ln_g'], g['conv_ln_b'] = _conv_bwd(
        A['pc'], A['z'], dc, A['conv_w'], _row(S['conv_ln_g']), _row(S['conv_ln_b']))
    g['conv_w'] = dcw[:CONV_WIDTH]
    d_wmo = _wgrad("mla_dwout", dym, A['o'])
    G['w_mla_out'] = d_wmo.reshape(D, N_HEADS, HEAD_W)[:, :, :V_DIM].reshape(D, N_HEADS * V_DIM)
    do = _mm("mla_do", dym, L['w_mo'], 'nn', out_dtype=_MXU_DTYPE)
    dqn, dkn, dv = _attn_bwd(A['qn'], A['kn'], A['v'], A['o'], do, A['lse'])
    dq_raw, dk_raw, drope, dgq, dgk = _qk_prep_bwd(A['q_raw'], A['k_raw'], A['pl'], rope_cb, *tabs, A['gq'], A['gk'], dqn, dkn)
    g['q_norm'], g['k_norm'] = dgq[:, :QK_DIM], dgk[:, :QK_DIM]
    G['w_uq'] = _unpad_heads(_wgrad("up_dwq", dq_raw, A['cqn']), QK_DIM).reshape(N_HEADS * QK_DIM, QL)
    d_wk = _unpad_heads(_wgrad("up_dwkv", dk_raw, A['ckvn']), NOPE_DIM)
    d_wv = _unpad_heads(_wgrad("up_dwkv", dv, A['ckvn']), V_DIM)
    G['w_ukv'] = jnp.concatenate([d_wk, d_wv], axis=1).reshape(N_HEADS * (NOPE_DIM + V_DIM), KVL)
    dcq = _mm("up_dcq", dq_raw, L['w_q'], 'nn')
    dckv = _mm("up_dckv_k", dk_raw, L['w_k'], 'nn')
    dckv = _mm("up_dckv_v", dv, L['w_v'], 'nn', res=dckv)
    dpl, g['cq_norm'], g['ckv_norm'] = _lat_norm_bwd(A['pl'], _row(S['cq_norm']), _row(S['ckv_norm']), dcq, dckv, drope, QL, KVL)
    d_wc = _wgrad("proj_dwc", dpc, A['h'])
    d_wl = _wgrad("proj_dwl", dpl, A['h'])
    d_wg = _wgrad("proj_dwg", dpg, A['h'])
    ql = QL + KVL
    G['w_in'] = jnp.concatenate([d_wc, d_wl[:ql], d_wl[ql + NOPE_DIM:ql + NOPE_DIM + ROPE_DIM], d_wg], axis=0)
    dx, g['mix_norm'] = _dh_rms_bwd("proj_dh_rms", [(dpc, L['w_c']), (dpl, L['w_l']), (dpg, L['w_g'])],
                                    A['x1'], _row(S['mix_norm']), dx, after)
    return dx, G, g


def kernel(x, positions, ffn1_norm, ffn1_w_gate, ffn1_w_up, ffn1_w_down, mix_norm, w_in, gate_bias, conv_w, conv_b, conv_ln_g, conv_ln_b, w_conv_out, cq_norm, ckv_norm, w_uq, w_ukv, q_norm, k_norm, w_mla_out, w_out, ffn2_norm, ffn2_w_gate, ffn2_w_up, ffn2_w_down, loss_target, m_ffn1_norm, m_ffn1_w_gate, m_ffn1_w_up, m_ffn1_w_down, m_mix_norm, m_w_in, m_gate_bias, m_conv_w, m_conv_b, m_conv_ln_g, m_conv_ln_b, m_w_conv_out, m_cq_norm, m_ckv_norm, m_w_uq, m_w_ukv, m_q_norm, m_k_norm, m_w_mla_out, m_w_out, m_ffn2_norm, m_ffn2_w_gate, m_ffn2_w_up, m_ffn2_w_down, v_ffn1_norm, v_ffn1_w_gate, v_ffn1_w_up, v_ffn1_w_down, v_mix_norm, v_w_in, v_gate_bias, v_conv_w, v_conv_b, v_conv_ln_g, v_conv_ln_b, v_w_conv_out, v_cq_norm, v_ckv_norm, v_w_uq, v_w_ukv, v_q_norm, v_k_norm, v_w_mla_out, v_w_out, v_ffn2_norm, v_ffn2_w_gate, v_ffn2_w_up, v_ffn2_w_down):
    w = dict(zip(WEIGHTS, (ffn1_norm, ffn1_w_gate, ffn1_w_up, ffn1_w_down, mix_norm, w_in, gate_bias, conv_w, conv_b, conv_ln_g, conv_ln_b, w_conv_out, cq_norm, ckv_norm, w_uq, w_ukv, q_norm, k_norm, w_mla_out, w_out, ffn2_norm, ffn2_w_gate, ffn2_w_up, ffn2_w_down)))
    m = dict(zip(WEIGHTS, (m_ffn1_norm, m_ffn1_w_gate, m_ffn1_w_up, m_ffn1_w_down, m_mix_norm, m_w_in, m_gate_bias, m_conv_w, m_conv_b, m_conv_ln_g, m_conv_ln_b, m_w_conv_out, m_cq_norm, m_ckv_norm, m_w_uq, m_w_ukv, m_q_norm, m_k_norm, m_w_mla_out, m_w_out, m_ffn2_norm, m_ffn2_w_gate, m_ffn2_w_up, m_ffn2_w_down)))
    v = dict(zip(WEIGHTS, (v_ffn1_norm, v_ffn1_w_gate, v_ffn1_w_up, v_ffn1_w_down, v_mix_norm, v_w_in, v_gate_bias, v_conv_w, v_conv_b, v_conv_ln_g, v_conv_ln_b, v_w_conv_out, v_cq_norm, v_ckv_norm, v_w_uq, v_w_ukv, v_q_norm, v_k_norm, v_w_mla_out, v_w_out, v_ffn2_norm, v_ffn2_w_gate, v_ffn2_w_up, v_ffn2_w_down)))
    depth = ffn1_norm.shape[0]
    T, D = x.shape[1], x.shape[2]
    xs = x.reshape(T, D)
    target = loss_target.reshape(T, D)
    tabs = _rope_tables(positions, T)
    me = 4 * lax.axis_index("x") + 2 * lax.axis_index("y") + lax.axis_index("c")

    big = list(BIG)
    groups = [[n for n in big if n.startswith('ffn1_')], [n for n in big if not n.startswith('ffn1_')]]
    packed, wspans = {}, {}
    for l in range(depth):
        for gi, names in enumerate(groups):
            packed[l, gi], wspans[gi] = _pack_rows([(w[n][l].T if BIG[n] else w[n][l]).astype(_WIRE_DTYPE) for n in names])
    small_names = [(l, n) for l in range(depth) for n in SMALL_SHARDED]
    small_packed, small_spans = _pack_flat([w[n][l].T for l, n in small_names])
    small_flight = _exchange_start("small_weights_start", small_packed, _own_slot(small_packed, me), False)
    relay = _relay_start("gather0_start", packed[0, 1], _own_slot(packed[0, 1], me))
    gathered = {(0, 0): _all_gather("gather_weights", packed[0, 0])}
    send_sems, recv_sems, src, land, _ = small_flight
    small_gathered = _exchange_wait("small_weights_wait", send_sems, recv_sems, src, land, gathered[0, 0], False)

    def unpack(l, gi):
        W = {}
        for n, span in zip(groups[gi], wspans[gi]):
            sh = _unpack_rows(gathered[l, gi], span, (N_DEV,))
            W[n] = sh.reshape(N_DEV * sh.shape[1], sh.shape[2])
        return W

    def small_weights(l):
        S = {n: w[n][l] for n in REPLICATED}
        for (ll, n), span in zip(small_names, small_spans):
            if ll == l:
                sh = _unpack_flat(small_gathered, span, (N_DEV,))
                S[n] = sh.reshape(N_DEV * sh.shape[1], sh.shape[2]).T
        return S

    layers, acts, in_flight = [], [], {}
    hcur = xs
    hn = _rms_fwd(hcur, _row(w['ffn1_norm'][0]))
    for l in range(depth):
        S = small_weights(l)
        if l > 0:
            send_sems, recv_sems, src, land, _ = in_flight[l, 0]
            gathered[l, 0] = _exchange_wait(f"gather_wait_{l}_0", send_sems, recv_sems, src, land, hcur, False)
        W = unpack(l, 0)
        x1, h, ffn1 = _ffn_fwd(hcur, hn, W['ffn1_w_gate'], W['ffn1_w_up'], W['ffn1_w_down'], S['mix_norm'])
        if l == 0:
            send_a, recv_a, src, land, _ = relay
            src, land, send_b, recv_b = _relay_forward("gather0_relay", send_a, recv_a, src, land, x1)
            gathered[0, 1] = _relay_wait("gather0_wait", send_a, recv_a, send_b, recv_b, src, land)
            for ll in range(1, depth):
                for gi in range(len(groups)):
                    in_flight[ll, gi] = _exchange_start(f"gather_start_{ll}_{gi}", packed[ll, gi], _own_slot(packed[ll, gi], me), False)
        else:
            send_sems, recv_sems, src, land, _ = in_flight[l, 1]
            gathered[l, 1] = _exchange_wait(f"gather_wait_{l}_1", send_sems, recv_sems, src, land, x1, False)
        W.update(unpack(l, 1))
        layers.append((_layer_weights(W), S))
        hcur, hn, saved = _layer_fwd_rest(x1, h, ffn1, *layers[l], tabs, w['ffn1_norm'][l + 1] if l + 1 < depth else None)
        acts.append(saved)
    dx, loss_part = _loss_grad(hcur, target)

    stage_names = {'mix': [n for n in big if not n.startswith('ffn1_')], 'down': ['ffn1_w_down'],
                   'gateup': ['ffn1_w_gate', 'ffn1_w_up']}
    small_grads, gspans, in_flight = [{} for _ in range(depth)], {}, {}

    def send_stage(l, stage, G):
        blocks = [G[n].reshape(N_DEV, -1, G[n].shape[1]) for n in stage_names[stage]]
        send, gspans[stage] = _pack_rows(blocks, (N_DEV,))
        own = lax.dynamic_index_in_dim(send, me, 0, keepdims=False)
        in_flight[l, stage] = _exchange_start(f"scatter_start_{l}_{stage}", send, _own_slot(own, me), True)
        return in_flight[l, stage][4]

    after = jnp.zeros((_SUBLANES, _LANES), F32)
    for l in reversed(range(depth)):
        L, S = layers[l]
        dx, G, g = _layer_bwd_mix(dx, L, S, tabs, acts[l], after)
        small_grads[l].update(g)
        after = send_stage(l, 'mix', G)
        x_in, h, a, b, u = acts[l]['ffn1']
        after = after + send_stage(l, 'down', {'ffn1_w_down': _wgrad("ffn_dwd", u, dx, scale=0.5)})
        da, db = _ffn_bwd_up(dx, L['ffn1_w_down'], a, b, after)
        after = send_stage(l, 'gateup', {'ffn1_w_gate': _wgrad("ffn_dwg", da, h), 'ffn1_w_up': _wgrad("ffn_dwg", db, h)})
        dx, small_grads[l]['ffn1_norm'] = _ffn_bwd_input(dx, acts[l]['ffn1'], S['ffn1_norm'], L['ffn1_w_gate'],
                                                          L['ffn1_w_up'], da, db, after)
    small_list = [(l, n) for l in range(depth) for n in REPLICATED + SMALL_SHARDED]
    loss_part = loss_part + after[:1, :]
    small_send, sspans = _pack_flat([small_grads[l][n] for l, n in small_list] + [loss_part])
    small_flight = _exchange_start("small_grads_start", small_send, _own_slot(small_send, me), False)

    out4 = {}
    after = small_flight[4]
    for (l, stage), (send_sems, recv_sems, src, land, _) in in_flight.items():
        got = _exchange_wait(f"scatter_wait_{l}_{stage}", send_sems, recv_sems, src, land, after, True)
        reduced = _sum_blocks("sum_grads_direct", got)
        for n, span in zip(stage_names[stage], gspans[stage]):
            form = (lambda t: jnp.swapaxes(t, 1, 2)) if BIG[n] else (lambda t: t)
            out4[n] = _adamw_layer(l, form(w[n]), _unpack_rows(reduced, span), form(m[n]), form(v[n]), out4.get(n), after)
            after = out4[n][1]
    for n in big:
        if BIG[n]:
            out4[n] = [jnp.swapaxes(t, 1, 2) for t in out4[n]]
    send_sems, recv_sems, src, land, _ = small_flight
    small_all = _sum_blocks("sum_small", _exchange_wait("small_grads_wait", send_sems, recv_sems, src, land, after, False))
    loss = _unpack_flat(small_all, sspans[-1])[0, 0]
    for n in WEIGHTS:
        if n in BIG:
            continue
        per_layer = []
        for l in range(depth):
            gs = _unpack_flat(small_all, sspans[small_list.index((l, n))])
            if n in SMALL_SHARDED:
                cols = w[n].shape[-1]
                gs = lax.dynamic_slice_in_dim(gs, me * cols, cols, axis=1)
            per_layer.append(gs.reshape(w[n].shape[1:]))
        g = jnp.stack(per_layer, axis=0)
        out4[n] = (g,) + _adamw(w[n], g, m[n], v[n])
    return (loss, dx.reshape(1, T, D), *[out4[n][k] for k in range(4) for n in WEIGHTS])
```

```python
import jax
import jax.numpy as jnp
from jax import lax
from jax.experimental import pallas as pl
from jax.experimental.pallas import tpu as pltpu

F32 = jnp.float32
_MXU_DTYPE = jnp.bfloat16
_WIRE_DTYPE = jnp.bfloat16

_LANES = 128
_SUBLANES = 8
_PACKED_ROWS = 16
_V7X_VMEM_BYTES = 64 * 2 ** 20
_VMEM_HEADROOM = 8 * 2 ** 20

N_DEV = 8
EPS = 1e-6
CHUNK = 64
CONV_WIDTH = 31
N_HEADS = 8
NOPE_DIM = 64
ROPE_DIM = 32
QK_DIM = NOPE_DIM + ROPE_DIM
V_DIM = 64
HEAD_W = _LANES
ROPE_THETA = 10000.0
HALO = 32

ADAM_LR = 0.001
ADAM_B1 = 0.9
ADAM_B2 = 0.999
ADAM_EPS = 1e-08
ADAM_WD = 0.01
ADAM_STEP = 10

_ROW_TILE = 512
_MM_TILE = 512
_FFN_TILE = 512
_MM_TILE_N = 1408
_MM_TILE_K = 1024
_TN_TILE_M = 1408
_ATT_TILE = 512
_ATT_ROWS = 32
_ATT_HEADS = 2
_ATT_HEADS_FWD = 4
_PACK_W = 1024
_PACK_ROW_MULT = 128

WEIGHTS = ['ffn1_norm', 'ffn1_w_gate', 'ffn1_w_up', 'ffn1_w_down', 'mix_norm', 'w_in', 'gate_bias',
           'conv_w', 'conv_b', 'conv_ln_g', 'conv_ln_b', 'w_conv_out', 'cq_norm', 'ckv_norm', 'w_uq',
           'w_ukv', 'q_norm', 'k_norm', 'w_mla_out', 'w_out', 'ffn2_norm', 'ffn2_w_gate', 'ffn2_w_up',
           'ffn2_w_down']
BIG = {'ffn1_w_gate': True, 'ffn1_w_up': True, 'ffn1_w_down': False, 'w_in': True, 'w_conv_out': True,
       'w_uq': True, 'w_ukv': True, 'w_mla_out': True, 'w_out': False, 'ffn2_w_gate': True,
       'ffn2_w_up': True, 'ffn2_w_down': False}
SMALL_SHARDED = ['conv_w', 'gate_bias']
REPLICATED = ['ffn1_norm', 'mix_norm', 'conv_b', 'conv_ln_g', 'conv_ln_b', 'cq_norm', 'ckv_norm',
              'q_norm', 'k_norm', 'ffn2_norm']


def _tile(n, target, mult=_LANES):
    if n <= target:
        return n
    for d in range(target - target % mult, 0, -mult):
        if n % d == 0:
            return d
    return n


def _nbytes(shape, dtype):
    n = 1
    for s in shape:
        n *= s
    return n * jnp.dtype(dtype).itemsize


def _params(dims, block_bytes, extra_bytes=0):
    need = 2 * block_bytes + extra_bytes + _VMEM_HEADROOM
    assert need <= _V7X_VMEM_BYTES, (need, dims)
    return pltpu.CompilerParams(dimension_semantics=dims, vmem_limit_bytes=_V7X_VMEM_BYTES - _VMEM_HEADROOM)


def _mxu(x, scale=None):
    if scale is not None:
        x = x * scale
    return x if x.dtype == _MXU_DTYPE else x.astype(_MXU_DTYPE)


def _dot(a, b, mode):
    dims = {'nn': (((1,), (0,)), ((), ())), 'nt': (((1,), (1,)), ((), ())), 'tn': (((0,), (0,)), ((), ()))}[mode]
    return lax.dot_general(a, b, dims, preferred_element_type=F32)


def _colsum(x):
    return jnp.sum(x, axis=0, keepdims=True)


def _sig(x):
    return 0.5 * jnp.tanh(0.5 * x) + 0.5


def _mm(name, a, b, mode, *, out_dtype=F32, res=None, scale=None, a_scale=None, tm=None, tn=None, tk=None):
    if mode == 'nn':
        (M, K), (_, N) = a.shape, b.shape
    elif mode == 'nt':
        (M, K), (N, _) = a.shape, b.shape
    else:
        (K, M), (_, N) = a.shape, b.shape
    tm = _tile(M, tm or (_TN_TILE_M if mode == 'tn' else _MM_TILE))
    tn = _tile(N, tn or _MM_TILE_N)
    tk = _tile(K, tk or (_MM_TILE_K if mode == 'tn' else 4096))
    nk = K // tk
    grid = (M // tm, N // tn, nk)
    a_spec = pl.BlockSpec((tk, tm), lambda i, j, k: (k, i)) if mode == 'tn' else pl.BlockSpec((tm, tk), lambda i, j, k: (i, k))
    b_spec = pl.BlockSpec((tn, tk), lambda i, j, k: (j, k)) if mode == 'nt' else pl.BlockSpec((tk, tn), lambda i, j, k: (k, j))
    o_spec = pl.BlockSpec((tm, tn), lambda i, j, k: (i, j))
    has_res = res is not None

    def body(*refs):
        a_ref, b_ref = refs[0], refs[1]
        res_ref = refs[2] if has_res else None
        o_ref = refs[3] if has_res else refs[2]
        acc_ref = refs[-1] if nk > 1 else None

        def finish(p):
            if scale is not None:
                p = p * scale
            if has_res:
                p = res_ref[...] + p
            o_ref[...] = p.astype(o_ref.dtype)

        p = _dot(_mxu(a_ref[...], a_scale), _mxu(b_ref[...]), mode)
        if nk == 1:
            finish(p)
        else:
            k = pl.program_id(2)

            @pl.when(k == 0)
            def _():
                acc_ref[...] = p

            @pl.when(k > 0)
            def _():
                acc_ref[...] += p

            @pl.when(k == nk - 1)
            def _():
                finish(acc_ref[...])

    blocks = (_nbytes((tm, tk), a.dtype) + _nbytes((tk, tn), b.dtype) + _nbytes((tm, tn), out_dtype)
              + (_nbytes((tm, tn), F32) if has_res else 0))
    extra = ((2 + (nk > 1)) * _nbytes((tm, tn), F32) + (a.dtype != _MXU_DTYPE) * _nbytes((tm, tk), _MXU_DTYPE)
             + (b.dtype != _MXU_DTYPE) * _nbytes((tk, tn), _MXU_DTYPE))
    return pl.pallas_call(
        body, name=name, grid=grid,
        in_specs=[a_spec, b_spec] + ([o_spec] if has_res else []),
        out_specs=o_spec,
        out_shape=jax.ShapeDtypeStruct((M, N), out_dtype),
        scratch_shapes=[pltpu.VMEM((tm, tn), F32)] if nk > 1 else [],
        compiler_params=_params(("parallel", "parallel", "arbitrary"), blocks, extra),
    )(*([a, b] + ([res] if has_res else [])))


def _mm_res_norm(name, a, b, res, scale, g):
    (M, K), (_, N) = a.shape, b.shape
    tm = _tile(M, _MM_TILE)

    def body(a_ref, b_ref, res_ref, g_ref, o_ref, h_ref):
        xv = res_ref[...] + _dot(a_ref[...], b_ref[...], 'nn') * scale
        o_ref[...] = xv
        r = lax.rsqrt(jnp.mean(xv * xv, axis=-1, keepdims=True) + EPS)
        h_ref[...] = (xv * r * g_ref[...]).astype(h_ref.dtype)

    blocks = _nbytes((tm, K), a.dtype) + _nbytes((K, N), b.dtype) + 2 * _nbytes((tm, N), F32) + _nbytes((tm, N), _MXU_DTYPE)
    return pl.pallas_call(
        body, name=name, grid=(M // tm,),
        in_specs=[_row_spec(tm, K), _full_spec((K, N)), _row_spec(tm, N), _full_spec((1, N))],
        out_specs=[_row_spec(tm, N), _row_spec(tm, N)],
        out_shape=[jax.ShapeDtypeStruct((M, N), F32), jax.ShapeDtypeStruct((M, N), _MXU_DTYPE)],
        compiler_params=_params(("parallel",), blocks, 3 * _nbytes((tm, N), F32)),
    )(a, b, res, g)


def _wgrad(name, a, b, scale=None):
    return _mm(name, a, b, 'tn', out_dtype=_WIRE_DTYPE, scale=scale)


def _col_chunks(n, width=2 * _LANES):
    return [slice(c, min(c + width, n)) for c in range(0, n, width)]


def _ffn_up(h, wg, wu):
    T, D = h.shape
    F = wg.shape[0]
    tm, tn = _tile(T, _FFN_TILE), _tile(F, _MM_TILE_N)

    def body(h_ref, wg_ref, wu_ref, a_ref, b_ref, u_ref):
        hv = h_ref[...]
        for cols in _col_chunks(tn):
            a = _dot(hv, wg_ref[cols, :], 'nt')
            b = _dot(hv, wu_ref[cols, :], 'nt')
            a_ref[:, cols] = a
            b_ref[:, cols] = b
            u_ref[:, cols] = (a * _sig(a) * b).astype(u_ref.dtype)

    w_spec = pl.BlockSpec((tn, D), lambda j, i: (j, 0))
    o_spec = pl.BlockSpec((tm, tn), lambda j, i: (i, j))
    blocks = _nbytes((tm, D), h.dtype) + 2 * _nbytes((tn, D), wg.dtype) + 2 * _nbytes((tm, tn), F32) + _nbytes((tm, tn), _MXU_DTYPE)
    return pl.pallas_call(
        body, name="ffn_up", grid=(F // tn, T // tm),
        in_specs=[pl.BlockSpec((tm, D), lambda j, i: (i, 0)), w_spec, w_spec],
        out_specs=[o_spec, o_spec, o_spec],
        out_shape=[jax.ShapeDtypeStruct((T, F), F32), jax.ShapeDtypeStruct((T, F), F32),
                   jax.ShapeDtypeStruct((T, F), _MXU_DTYPE)],
        compiler_params=_params(("parallel", "parallel"), blocks, 4 * _nbytes((tm, tn), F32)),
    )(h, wg, wu)


def _ffn_bwd_up(dx, wd, a, b, after):
    T, D = dx.shape
    F = wd.shape[0]
    tm, tn = _tile(T, _FFN_TILE), _tile(F, _MM_TILE_N)

    def body(dx_ref, wd_ref, a_ref, b_ref, after_ref, da_ref, db_ref):
        dy = _mxu(dx_ref[...], 0.5)
        for cols in _col_chunks(tn):
            du = _dot(dy, wd_ref[cols, :], 'nt')
            av, bv = a_ref[:, cols], b_ref[:, cols]
            s = _sig(av)
            db_ref[:, cols] = (du * av * s).astype(db_ref.dtype)
            da_ref[:, cols] = (du * bv * (s * (1.0 + av * (1.0 - s)))).astype(da_ref.dtype)

    t_spec = pl.BlockSpec((tm, tn), lambda j, i: (i, j))
    blocks = _nbytes((tm, D), F32) + _nbytes((tn, D), wd.dtype) + 2 * _nbytes((tm, tn), F32) + 2 * _nbytes((tm, tn), _MXU_DTYPE)
    return pl.pallas_call(
        body, name="ffn_bwd_up", grid=(F // tn, T // tm),
        in_specs=[pl.BlockSpec((tm, D), lambda j, i: (i, 0)), pl.BlockSpec((tn, D), lambda j, i: (j, 0)), t_spec, t_spec,
                  pl.BlockSpec(memory_space=pl.ANY)],
        out_specs=[t_spec, t_spec],
        out_shape=[jax.ShapeDtypeStruct((T, F), _MXU_DTYPE)] * 2,
        compiler_params=_params(("parallel", "parallel"), blocks, 5 * _nbytes((tm, tn), F32)),
    )(dx, wd, a, b, after)


def _row_spec(tm, w, cb=0):
    return pl.BlockSpec((tm, w), lambda i: (i, cb))


def _full_spec(shape):
    return pl.BlockSpec(shape, lambda i: (0,) * len(shape))


def _init_acc(refs):
    @pl.when(pl.program_id(0) == 0)
    def _():
        for r in refs:
            r[...] = jnp.zeros_like(r)


def _rms_fwd(x, g):
    T, D = x.shape
    tm = _tile(T, _ROW_TILE, _PACKED_ROWS)

    def body(x_ref, g_ref, h_ref):
        xv = x_ref[...]
        r = lax.rsqrt(jnp.mean(xv * xv, axis=-1, keepdims=True) + EPS)
        h_ref[...] = (xv * r * g_ref[...]).astype(h_ref.dtype)

    return pl.pallas_call(
        body, name="rms_fwd", grid=(T // tm,),
        in_specs=[_row_spec(tm, D), _full_spec((1, D))], out_specs=_row_spec(tm, D),
        out_shape=jax.ShapeDtypeStruct((T, D), _MXU_DTYPE),
        compiler_params=_params(("parallel",), 2 * _nbytes((tm, D), F32), 4 * _nbytes((tm, D), F32)),
    )(x, g)


def _dh_rms_bwd(name, pairs, x, g, dres, after):
    T, D = x.shape
    tm = _tile(T, _ROW_TILE // 2, _PACKED_ROWS)
    n = len(pairs)

    def body(*refs):
        x_ref, g_ref, dres_ref, after_ref, dx_ref, dg_ref = refs[2 * n:]
        _init_acc([dg_ref])
        dh = None
        for k in range(n):
            p = _dot(_mxu(refs[2 * k][...]), refs[2 * k + 1][...], 'nn')
            dh = p if dh is None else dh + p
        xv = x_ref[...]
        r = lax.rsqrt(jnp.mean(xv * xv, axis=-1, keepdims=True) + EPS)
        xh = xv * r
        dxh = dh * g_ref[...]
        dx_ref[...] = dres_ref[...] + r * (dxh - xh * jnp.mean(dxh * xh, axis=-1, keepdims=True))
        dg_ref[...] += _colsum(dh * xh)

    in_specs, args, blocks = [], [], 3 * _nbytes((tm, D), F32)
    for a, w in pairs:
        in_specs += [_row_spec(tm, a.shape[1]), _full_spec(w.shape)]
        args += [a, w]
        blocks += _nbytes((tm, a.shape[1]), a.dtype) + _nbytes(w.shape, w.dtype)
    return pl.pallas_call(
        body, name=name, grid=(T // tm,),
        in_specs=in_specs + [_row_spec(tm, D), _full_spec((1, D)), _row_spec(tm, D), pl.BlockSpec(memory_space=pl.ANY)],
        out_specs=[_row_spec(tm, D), _full_spec((1, D))],
        out_shape=[jax.ShapeDtypeStruct((T, D), F32), jax.ShapeDtypeStruct((1, D), F32)],
        compiler_params=_params(("arbitrary",), blocks, 6 * _nbytes((tm, D), F32)),
    )(*args, x, g, dres, after)


def _loss_grad(y, target):
    T, D = y.shape
    tm = _tile(T, _ROW_TILE, _SUBLANES)

    def body(y_ref, t_ref, dy_ref, l_ref):
        _init_acc([l_ref])
        err = y_ref[...] - t_ref[...]
        dy_ref[...] = err * (1.0 / D)
        per_row = jnp.sum(err * err, axis=-1, keepdims=True) * (0.5 / D)
        l_ref[...] += jnp.sum(per_row, axis=0, keepdims=True)

    return pl.pallas_call(
        body, name="loss_grad", grid=(T // tm,),
        in_specs=[_row_spec(tm, D), _row_spec(tm, D)],
        out_specs=[_row_spec(tm, D), _full_spec((1, _LANES))],
        out_shape=[jax.ShapeDtypeStruct((T, D), F32), jax.ShapeDtypeStruct((1, _LANES), F32)],
        compiler_params=_params(("arbitrary",), 3 * _nbytes((tm, D), F32), 3 * _nbytes((tm, D), F32)),
    )(y, target)


def _glu(pc, C):
    a, gate = pc[:, :C], pc[:, C:]
    s = _sig(gate)
    return a, s, a * s


def _tap_groups(offsets):
    groups = {}
    for k, off in enumerate(offsets):
        groups.setdefault(off % _SUBLANES, []).append((k, off - off % _SUBLANES))
    return sorted(groups.items())


def _conv_fwd(pc, w, cb, lg, lb):
    T, C2 = pc.shape
    C = C2 // 2
    tm = _tile(T, _ROW_TILE // 2, HALO)
    per = tm // HALO

    def body(pc_ref, ph_ref, w_ref, cb_ref, lg_ref, lb_ref, z_ref, c_ref, ubuf, vbuf):
        i = pl.program_id(0)
        _, _, u_cur = _glu(pc_ref[...], C)
        _, _, u_prev = _glu(ph_ref[...], C)
        ubuf[0:HALO, :] = jnp.where(i > 0, u_prev, 0.0)
        ubuf[HALO:HALO + tm, :] = u_cur
        ubuf[HALO + tm:, :] = jnp.zeros((_SUBLANES, C), F32)
        acc = jnp.zeros((tm, C), F32)
        for shift, taps in _tap_groups([HALO - (CONV_WIDTH - 1) + k for k in range(CONV_WIDTH)]):
            v = None
            for k, base in taps:
                term = w_ref[k:k + 1, :] * ubuf[base:base + tm + _SUBLANES, :]
                v = term if v is None else v + term
            if shift == 0:
                acc = acc + v[:tm]
            else:
                vbuf[...] = v
                acc = acc + vbuf[shift:shift + tm, :]
        z = acc + cb_ref[...]
        z_ref[...] = z
        zc = z - jnp.mean(z, axis=-1, keepdims=True)
        y = zc * lax.rsqrt(jnp.mean(zc * zc, axis=-1, keepdims=True) + EPS) * lg_ref[...] + lb_ref[...]
        c_ref[...] = (y * _sig(y)).astype(c_ref.dtype)

    return pl.pallas_call(
        body, name="conv_fwd", grid=(T // tm,),
        in_specs=[_row_spec(tm, C2), pl.BlockSpec((HALO, C2), lambda i: (jnp.maximum(i * per - 1, 0), 0)),
                  _full_spec((HALO, C)), _full_spec((1, C)), _full_spec((1, C)), _full_spec((1, C))],
        out_specs=[_row_spec(tm, C), _row_spec(tm, C)],
        out_shape=[jax.ShapeDtypeStruct((T, C), F32), jax.ShapeDtypeStruct((T, C), _MXU_DTYPE)],
        scratch_shapes=[pltpu.VMEM((tm + HALO + _SUBLANES, C), F32), pltpu.VMEM((tm + _SUBLANES, C), F32)],
        compiler_params=_params(("parallel",), 3 * _nbytes((tm, C2), F32), 10 * _nbytes((tm, C), F32)),
    )(pc, pc, w, cb, lg, lb)


def _conv_bwd(pc, z, dc, w, lg, lb):
    T, C2 = pc.shape
    C = C2 // 2
    tm = _tile(T, _ROW_TILE // 2, HALO)
    per = tm // HALO
    n = T // tm
    last_halo = T // HALO - 1

    def body(pc_ref, ph_ref, z_ref, zn_ref, dc_ref, dcn_ref, w_ref, lg_ref, lb_ref,
             dpc_ref, dw_ref, dcb_ref, dlg_ref, dlb_ref, ubuf, dzbuf, vbuf, sbuf):
        i = pl.program_id(0)
        _init_acc([dw_ref, dcb_ref, dlg_ref, dlb_ref])
        g, b = lg_ref[...], lb_ref[...]

        def ln_swish_bwd(zv, dcv):
            zc = zv - jnp.mean(zv, axis=-1, keepdims=True)
            r = lax.rsqrt(jnp.mean(zc * zc, axis=-1, keepdims=True) + EPS)
            zh = zc * r
            y = zh * g + b
            s = _sig(y)
            dy = dcv * (s * (1.0 + y * (1.0 - s)))
            dyg = dy * g
            dz = r * (dyg - jnp.mean(dyg, axis=-1, keepdims=True) - zh * jnp.mean(dyg * zh, axis=-1, keepdims=True))
            return dz, dy, zh

        dz_c, dy_c, zh_c = ln_swish_bwd(z_ref[...], dc_ref[...])
        dz_n, _, _ = ln_swish_bwd(zn_ref[...], dcn_ref[...])
        dlg_ref[...] += _colsum(dy_c * zh_c)
        dlb_ref[...] += _colsum(dy_c)
        dcb_ref[...] += _colsum(dz_c)
        dzbuf[0:tm, :] = dz_c
        dzbuf[tm:tm + HALO, :] = jnp.where(i < n - 1, dz_n, 0.0)
        a, s, u_cur = _glu(pc_ref[...], C)
        _, _, u_prev = _glu(ph_ref[...], C)
        ubuf[0:HALO, :] = jnp.where(i > 0, u_prev, 0.0)
        ubuf[HALO:HALO + tm, :] = u_cur
        ubuf[HALO + tm:, :] = jnp.zeros((_SUBLANES, C), F32)
        du = jnp.zeros((tm, C), F32)
        for shift, taps in _tap_groups([CONV_WIDTH - 1 - k for k in range(CONV_WIDTH)]):
            v = None
            for k, base in taps:
                term = w_ref[k:k + 1, :] * dzbuf[base:base + tm + _SUBLANES, :]
                v = term if v is None else v + term
            if shift == 0:
                du = du + v[:tm]
            else:
                vbuf[...] = v
                du = du + vbuf[shift:shift + tm, :]
        for shift, taps in _tap_groups([HALO - (CONV_WIDTH - 1) + k for k in range(CONV_WIDTH)]):
            if shift:
                sbuf[...] = ubuf[shift:shift + tm + HALO, :]
            src = sbuf if shift else ubuf
            for k, base in taps:
                dw_ref[k:k + 1, :] += _colsum(dz_c * src[base:base + tm, :])
        dpc_ref[:, :C] = (du * s).astype(dpc_ref.dtype)
        dpc_ref[:, C:] = (du * a * s * (1.0 - s)).astype(dpc_ref.dtype)

    nxt = lambda i: (jnp.minimum((i + 1) * per, last_halo), 0)
    vec = _full_spec((1, C))
    return pl.pallas_call(
        body, name="conv_bwd", grid=(n,),
        in_specs=[_row_spec(tm, C2), pl.BlockSpec((HALO, C2), lambda i: (jnp.maximum(i * per - 1, 0), 0)),
                  _row_spec(tm, C), pl.BlockSpec((HALO, C), nxt), _row_spec(tm, C), pl.BlockSpec((HALO, C), nxt),
                  _full_spec((HALO, C)), vec, vec],
        out_specs=[_row_spec(tm, C2), _full_spec((HALO, C)), vec, vec, vec],
        out_shape=[jax.ShapeDtypeStruct((T, C2), _MXU_DTYPE), jax.ShapeDtypeStruct((HALO, C), F32)]
                  + [jax.ShapeDtypeStruct((1, C), F32)] * 3,
        scratch_shapes=[pltpu.VMEM((tm + HALO + _SUBLANES, C), F32), pltpu.VMEM((tm + HALO + _SUBLANES, C), F32),
                        pltpu.VMEM((tm + _SUBLANES, C), F32), pltpu.VMEM((tm + HALO, C), F32)],
        compiler_params=_params(("arbitrary",), 5 * _nbytes((tm, C2), F32), 16 * _nbytes((tm, C), F32)),
    )(pc, pc, z, z, dc, dc, w, lg, lb)


def _lat_norm_fwd(pl_, gq, gkv, QL, KVL):
    T, W = pl_.shape
    tm = _tile(T, _ROW_TILE, _PACKED_ROWS)

    def body(p_ref, gq_ref, gkv_ref, cq_ref, ckv_ref):
        def norm(xv, gv):
            return xv * lax.rsqrt(jnp.mean(xv * xv, axis=-1, keepdims=True) + EPS) * gv
        cq_ref[...] = norm(p_ref[:, :QL], gq_ref[...]).astype(cq_ref.dtype)
        ckv_ref[...] = norm(p_ref[:, QL:QL + KVL], gkv_ref[...]).astype(ckv_ref.dtype)

    return pl.pallas_call(
        body, name="lat_norm_fwd", grid=(T // tm,),
        in_specs=[_row_spec(tm, W), _full_spec((1, QL)), _full_spec((1, KVL))],
        out_specs=[_row_spec(tm, QL), _row_spec(tm, KVL)],
        out_shape=[jax.ShapeDtypeStruct((T, QL), _MXU_DTYPE), jax.ShapeDtypeStruct((T, KVL), _MXU_DTYPE)],
        compiler_params=_params(("parallel",), 2 * _nbytes((tm, W), F32), 4 * _nbytes((tm, W), F32)),
    )(pl_, gq, gkv)


def _lat_norm_bwd(pl_, gq, gkv, dcq, dckv, drope, QL, KVL):
    T, W = pl_.shape
    tm = _tile(T, _ROW_TILE, _PACKED_ROWS)

    def body(p_ref, gq_ref, gkv_ref, dcq_ref, dckv_ref, dr_ref, dp_ref, dgq_ref, dgkv_ref):
        _init_acc([dgq_ref, dgkv_ref])

        def bwd(xv, gv, dv, dg_ref):
            r = lax.rsqrt(jnp.mean(xv * xv, axis=-1, keepdims=True) + EPS)
            xh = xv * r
            dxh = dv * gv
            dg_ref[...] += _colsum(dv * xh)
            return r * (dxh - xh * jnp.mean(dxh * xh, axis=-1, keepdims=True))

        dp_ref[:, :QL] = bwd(p_ref[:, :QL], gq_ref[...], dcq_ref[...], dgq_ref).astype(dp_ref.dtype)
        dp_ref[:, QL:QL + KVL] = bwd(p_ref[:, QL:QL + KVL], gkv_ref[...], dckv_ref[...], dgkv_ref).astype(dp_ref.dtype)
        dp_ref[:, QL + KVL:] = dr_ref[...].astype(dp_ref.dtype)

    return pl.pallas_call(
        body, name="lat_norm_bwd", grid=(T // tm,),
        in_specs=[_row_spec(tm, W), _full_spec((1, QL)), _full_spec((1, KVL)), _row_spec(tm, QL), _row_spec(tm, KVL),
                  _row_spec(tm, HEAD_W)],
        out_specs=[_row_spec(tm, W), _full_spec((1, QL)), _full_spec((1, KVL))],
        out_shape=[jax.ShapeDtypeStruct((T, W), _MXU_DTYPE), jax.ShapeDtypeStruct((1, QL), F32),
                   jax.ShapeDtypeStruct((1, KVL), F32)],
        compiler_params=_params(("arbitrary",), 4 * _nbytes((tm, W), F32), 6 * _nbytes((tm, W), F32)),
    )(pl_, gq, gkv, dcq, dckv, drope)


def _norm_rope(xv, gv, ct, sa, sb):
    r = lax.rsqrt(jnp.sum(xv * xv, axis=-1, keepdims=True) * (1.0 / QK_DIM) + EPS)
    xn = xv * r * gv
    half = ROPE_DIM // 2
    return xn * ct + pltpu.roll(xn, HEAD_W - half, 1) * sa + pltpu.roll(xn, half, 1) * sb


def _norm_rope_bwd(xv, gv, ct, sa, sb, dout):
    half = ROPE_DIM // 2
    dxn = dout * ct + pltpu.roll(dout * sa, half, 1) + pltpu.roll(dout * sb, HEAD_W - half, 1)
    r = lax.rsqrt(jnp.sum(xv * xv, axis=-1, keepdims=True) * (1.0 / QK_DIM) + EPS)
    xh = xv * r
    dxh = dxn * gv
    dx = r * (dxh - xh * (jnp.sum(dxh * xh, axis=-1, keepdims=True) * (1.0 / QK_DIM)))
    return dx, _colsum(dxn * xh)


def _qk_prep_fwd(q_raw, k_raw, pl_, rope_cb, ct, sa, sb, gq, gk):
    T, HW = q_raw.shape
    H = HW // HEAD_W
    tm = _tile(T, _ROW_TILE, _PACKED_ROWS)

    def body(q_ref, k_ref, r_ref, ct_ref, sa_ref, sb_ref, gq_ref, gk_ref, qn_ref, kn_ref):
        ctv, sav, sbv, rv = ct_ref[...], sa_ref[...], sb_ref[...], r_ref[...]
        for h in range(H):
            hs = slice(h * HEAD_W, (h + 1) * HEAD_W)
            qn_ref[:, hs] = _norm_rope(q_ref[:, hs], gq_ref[...], ctv, sav, sbv).astype(qn_ref.dtype)
            kn_ref[:, hs] = _norm_rope(k_ref[:, hs] + rv, gk_ref[...], ctv, sav, sbv).astype(kn_ref.dtype)

    tb = _row_spec(tm, HEAD_W)
    gb = _full_spec((1, HEAD_W))
    return pl.pallas_call(
        body, name="qk_prep_fwd", grid=(T // tm,),
        in_specs=[_row_spec(tm, HW), _row_spec(tm, HW), _row_spec(tm, HEAD_W, rope_cb), tb, tb, tb, gb, gb],
        out_specs=[_row_spec(tm, HW), _row_spec(tm, HW)],
        out_shape=[jax.ShapeDtypeStruct((T, HW), _MXU_DTYPE)] * 2,
        compiler_params=_params(("parallel",), 3 * _nbytes((tm, HW), F32) + 4 * _nbytes((tm, HEAD_W), F32), 8 * _nbytes((tm, HEAD_W), F32)),
    )(q_raw, k_raw, pl_, ct, sa, sb, gq, gk)


def _qk_prep_bwd(q_raw, k_raw, pl_, rope_cb, ct, sa, sb, gq, gk, dqn, dkn):
    T, HW = q_raw.shape
    H = HW // HEAD_W
    tm = _tile(T, _ROW_TILE, _PACKED_ROWS)

    def body(q_ref, k_ref, r_ref, ct_ref, sa_ref, sb_ref, gq_ref, gk_ref, dqn_ref, dkn_ref,
             dq_ref, dk_ref, dr_ref, dgq_ref, dgk_ref):
        _init_acc([dgq_ref, dgk_ref])
        ctv, sav, sbv, rv = ct_ref[...], sa_ref[...], sb_ref[...], r_ref[...]
        dr = dgq_sum = dgk_sum = None
        for h in range(H):
            hs = slice(h * HEAD_W, (h + 1) * HEAD_W)
            dq, dgq = _norm_rope_bwd(q_ref[:, hs], gq_ref[...], ctv, sav, sbv, dqn_ref[:, hs])
            dk, dgk = _norm_rope_bwd(k_ref[:, hs] + rv, gk_ref[...], ctv, sav, sbv, dkn_ref[:, hs])
            dq_ref[:, hs] = dq.astype(dq_ref.dtype)
            dk_ref[:, hs] = dk.astype(dk_ref.dtype)
            dr = dk if h == 0 else dr + dk
            dgq_sum = dgq if h == 0 else dgq_sum + dgq
            dgk_sum = dgk if h == 0 else dgk_sum + dgk
        dr_ref[...] = dr
        dgq_ref[...] += dgq_sum
        dgk_ref[...] += dgk_sum

    tb = _row_spec(tm, HEAD_W)
    gb = _full_spec((1, HEAD_W))
    hb = _row_spec(tm, HW)
    return pl.pallas_call(
        body, name="qk_prep_bwd", grid=(T // tm,),
        in_specs=[hb, hb, _row_spec(tm, HEAD_W, rope_cb), tb, tb, tb, gb, gb, hb, hb],
        out_specs=[hb, hb, tb, gb, gb],
        out_shape=[jax.ShapeDtypeStruct((T, HW), _MXU_DTYPE)] * 2 + [jax.ShapeDtypeStruct((T, HEAD_W), F32)]
                  + [jax.ShapeDtypeStruct((1, HEAD_W), F32)] * 2,
        compiler_params=_params(("arbitrary",), 5 * _nbytes((tm, HW), F32) + 5 * _nbytes((tm, HEAD_W), F32), 12 * _nbytes((tm, HEAD_W), F32)),
    )(q_raw, k_raw, pl_, ct, sa, sb, gq, gk, dqn, dkn)


_NEG = -1e30
_LOG2E = 1.4426950408889634
_SCORE_C = QK_DIM ** -0.5 * _LOG2E


def _pieces(t, rb, diag):
    assert CHUNK % rb == 0 and rb % _SUBLANES == 0
    out = []
    for r in range(t // rb):
        lo = (r * rb // CHUNK) * CHUNK if diag else 0
        for c in range(t // _LANES):
            cut = None if (c + 1) * _LANES <= lo else max(lo - c * _LANES, 0)
            out.append((slice(r * rb, (r + 1) * rb), slice(c * _LANES, (c + 1) * _LANES), c, cut))
    return out


def _groups(x):
    return x.reshape(x.shape[0] // _SUBLANES, _SUBLANES, x.shape[1])


def _all_sublanes(x8, op):
    return jnp.broadcast_to(op(x8, axis=0, keepdims=True), x8.shape)


def _lane_ge(rb, cut):
    return lax.broadcasted_iota(jnp.int32, (rb, _LANES), 1) >= cut


def _attn_fwd(q, k, v):
    T, HW = q.shape
    H = HW // HEAD_W
    t = _tile(T, _ATT_TILE)
    rb = min(_ATT_ROWS, t)
    nc = t // _LANES

    hp = _ATT_HEADS_FWD if H % _ATT_HEADS_FWD == 0 else 1
    W = hp * HEAD_W

    def body(q_ref, k_ref, v_ref, o_ref, lse_ref, s_s, p_s, m_s, l_s, acc_s):
        i = pl.program_id(1)
        m_s[...] = jnp.full_like(m_s, _NEG)
        l_s[...] = jnp.zeros_like(l_s)
        acc_s[...] = jnp.zeros_like(acc_s)

        def head_step(hh, rows, diag):
            hs = slice(hh * HEAD_W, (hh + 1) * HEAD_W)
            s_s[hh] = _dot(k_ref[rows, hs], q_ref[:, hs], 'nt')
            pieces = _pieces(t, rb, diag)

            def scores(rs, cs, cut):
                sb = s_s[hh, rs, cs]
                return jnp.where(_lane_ge(rb, cut), sb, _NEG) if cut else sb

            mx = [None] * nc
            for rs, cs, c, cut in pieces:
                if cut is not None:
                    g = jnp.max(_groups(scores(rs, cs, cut)), axis=0)
                    mx[c] = g if mx[c] is None else jnp.maximum(mx[c], g)
            m_new, alpha = [], []
            for c in range(nc):
                cs = slice(c * _LANES, (c + 1) * _LANES)
                m_prev = m_s[hh, :, cs]
                m_new.append(jnp.maximum(m_prev, _all_sublanes(mx[c], jnp.max)))
                alpha.append(jnp.exp2((m_prev - m_new[c]) * _SCORE_C))
                m_s[hh, :, cs] = m_new[c]
            lsum = [jnp.zeros((_SUBLANES, _LANES), F32)] * nc
            for rs, cs, c, cut in pieces:
                if cut is None:
                    p_s[hh, rs, cs] = jnp.zeros((rb, _LANES), p_s.dtype)
                    continue
                p = jnp.exp2((scores(rs, cs, cut) - jnp.tile(m_new[c], (rb // _SUBLANES, 1))) * _SCORE_C)
                lsum[c] = lsum[c] + jnp.sum(_groups(p), axis=0)
                p_s[hh, rs, cs] = p.astype(p_s.dtype)
            for c in range(nc):
                cs = slice(c * _LANES, (c + 1) * _LANES)
                l_s[hh, :, cs] = alpha[c] * l_s[hh, :, cs] + _all_sublanes(lsum[c], jnp.sum)
            a = jnp.tile(jnp.concatenate(alpha, axis=1), (HEAD_W // _SUBLANES, 1))
            acc_s[hh] = a * acc_s[hh] + _dot(v_ref[rows, hs], p_s[hh], 'tn')

        def step(j, diag):
            rows = pl.ds(pl.multiple_of(j * t, t), t)
            for hh in range(hp):
                head_step(hh, rows, diag)

        def loop_body(j, carry):
            step(j, False)
            return carry

        lax.fori_loop(0, i, loop_body, 0)
        step(i, True)
        for hh in range(hp):
            l = l_s[hh]
            o_t = acc_s[hh] / jnp.tile(l, (HEAD_W // _SUBLANES, 1))
            o_ref[:, hh * HEAD_W:(hh + 1) * HEAD_W] = o_t.T.astype(o_ref.dtype)
            lse_ref[hh * _SUBLANES:(hh + 1) * _SUBLANES, :] = m_s[hh] * _SCORE_C + jnp.log(l) * _LOG2E

    qb = pl.BlockSpec((t, W), lambda h, i: (i, h))
    kb = pl.BlockSpec((T, W), lambda h, i: (0, h))
    st = pltpu.VMEM((hp, _SUBLANES, t), F32)
    return pl.pallas_call(
        body, name="attn_fwd", grid=(H // hp, T // t),
        in_specs=[qb, kb, kb], out_specs=[qb, pl.BlockSpec((hp * _SUBLANES, t), lambda h, i: (h, i))],
        out_shape=[jax.ShapeDtypeStruct((T, HW), _MXU_DTYPE), jax.ShapeDtypeStruct((H * _SUBLANES, T), F32)],
        scratch_shapes=[pltpu.VMEM((hp, t, t), F32), pltpu.VMEM((hp, t, t), _MXU_DTYPE), st, st, pltpu.VMEM((hp, HEAD_W, t), F32)],
        compiler_params=_params(("parallel", "arbitrary"), 2 * _nbytes((T, W), _MXU_DTYPE) + 3 * _nbytes((t, W), F32),
                                4 * hp * _nbytes((t, t), F32)),
    )(q, k, v)


def _attn_bwd(q, k, v, o, do, lse2):
    T, HW = q.shape
    H = HW // HEAD_W
    t = _tile(T, _ATT_TILE)
    n = T // t
    rb = min(_ATT_ROWS, t)
    scale = QK_DIM ** -0.5

    hp = _ATT_HEADS if H % _ATT_HEADS == 0 else 1
    W = hp * HEAD_W

    def body(k_ref, v_ref, q_ref, o_ref, do_ref, lse_ref, dq_ref, dk_ref, dv_ref,
             s_s, dp_s, p_s, ds_s, dk_s, dv_s, delta_s, dqt_s):
        j = pl.program_id(1)

        @pl.when(j == 0)
        def _():
            dqt_s[...] = jnp.zeros_like(dqt_s)
            for hh in range(hp):
                hs = slice(hh * HEAD_W, (hh + 1) * HEAD_W)
                for b in range(n):
                    rs = slice(b * t, (b + 1) * t)
                    d = jnp.sum(do_ref[rs, hs].astype(F32) * o_ref[rs, hs].astype(F32), axis=1, keepdims=True)
                    delta_s[hh, :, rs] = jnp.broadcast_to(d, (t, HEAD_W)).T[0:_SUBLANES, :]

        dk_s[...] = jnp.zeros_like(dk_s)
        dv_s[...] = jnp.zeros_like(dv_s)

        def head_step(hh, base, diag):
            hs = slice(hh * HEAD_W, (hh + 1) * HEAD_W)
            kv, vv = k_ref[:, hs], v_ref[:, hs]
            qv, dov = q_ref[pl.ds(base, t), hs], do_ref[pl.ds(base, t), hs]
            s_s[hh] = _dot(kv, qv, 'nt')
            dp_s[hh] = _dot(vv, dov, 'nt')
            for rs, cs, c, cut in _pieces(t, rb, diag):
                if cut is None:
                    p_s[hh, rs, cs] = jnp.zeros((rb, _LANES), p_s.dtype)
                    ds_s[hh, rs, cs] = jnp.zeros((rb, _LANES), ds_s.dtype)
                    continue
                lanes = pl.ds(pl.multiple_of(base + c * _LANES, _LANES), _LANES)
                lse = jnp.tile(lse_ref[hh * _SUBLANES:(hh + 1) * _SUBLANES, lanes], (rb // _SUBLANES, 1))
                dl = jnp.tile(delta_s[hh, :, lanes], (rb // _SUBLANES, 1))
                p = jnp.exp2(s_s[hh, rs, cs] * _SCORE_C - lse)
                if cut:
                    p = jnp.where(_lane_ge(rb, cut), p, 0.0)
                p_s[hh, rs, cs] = p.astype(p_s.dtype)
                ds_s[hh, rs, cs] = (p * (dp_s[hh, rs, cs] - dl)).astype(ds_s.dtype)
            dsv = ds_s[hh]
            dv_s[hh] += _dot(p_s[hh], dov, 'nn')
            dk_s[hh] += _dot(dsv, qv, 'nn')
            dqt_s[hh, :, pl.ds(base, t)] += _dot(kv, dsv, 'tn')

        def step(i, diag):
            base = pl.multiple_of(i * t, t)
            for hh in range(hp):
                head_step(hh, base, diag)

        def loop_body(i, carry):
            step(i, False)
            return carry

        step(j, True)
        lax.fori_loop(j + 1, n, loop_body, 0)
        for hh in range(hp):
            hs = slice(hh * HEAD_W, (hh + 1) * HEAD_W)
            dk_ref[:, hs] = dk_s[hh] * scale
            dv_ref[:, hs] = dv_s[hh].astype(dv_ref.dtype)

        @pl.when(j == n - 1)
        def _():
            for hh in range(hp):
                for b in range(n):
                    rs = slice(b * t, (b + 1) * t)
                    dq_ref[rs, hh * HEAD_W:(hh + 1) * HEAD_W] = dqt_s[hh, :, rs].T * scale

    jb = pl.BlockSpec((t, W), lambda h, j: (j, h))
    fb = pl.BlockSpec((T, W), lambda h, j: (0, h))
    sc = pltpu.VMEM((hp, t, HEAD_W), F32)
    return pl.pallas_call(
        body, name="attn_bwd", grid=(H // hp, n),
        in_specs=[jb, jb, fb, fb, fb, pl.BlockSpec((hp * _SUBLANES, T), lambda h, j: (h, 0))], out_specs=[fb, jb, jb],
        out_shape=[jax.ShapeDtypeStruct((T, HW), F32), jax.ShapeDtypeStruct((T, HW), F32),
                   jax.ShapeDtypeStruct((T, HW), _MXU_DTYPE)],
        scratch_shapes=[pltpu.VMEM((hp, t, t), F32), pltpu.VMEM((hp, t, t), F32), pltpu.VMEM((hp, t, t), _MXU_DTYPE),
                        pltpu.VMEM((hp, t, t), _MXU_DTYPE), sc, sc, pltpu.VMEM((hp, _SUBLANES, T), F32),
                        pltpu.VMEM((hp, HEAD_W, T), F32)],
        compiler_params=_params(("parallel", "arbitrary"), 3 * _nbytes((T, W), _MXU_DTYPE) + _nbytes((T, W), F32)
                                + 4 * _nbytes((t, W), F32), hp * (4 * _nbytes((t, t), F32) + _nbytes((T, HEAD_W), F32))),
    )(k, v, q, o, do, lse2)


def _gate_fwd(pg, gb, yc, ym):
    T, D = yc.shape
    tm = _tile(T, _ROW_TILE, _PACKED_ROWS)

    def body(g0_ref, g1_ref, gb_ref, yc_ref, ym_ref, y_ref):
        s0 = _sig(g0_ref[...] + gb_ref[0:1, :])
        s1 = _sig(g1_ref[...] + gb_ref[1:2, :])
        y_ref[...] = (s0 * yc_ref[...] + s1 * ym_ref[...]).astype(y_ref.dtype)

    return pl.pallas_call(
        body, name="gate_fwd", grid=(T // tm,),
        in_specs=[_row_spec(tm, D, 0), _row_spec(tm, D, 1), _full_spec((2, D)), _row_spec(tm, D), _row_spec(tm, D)],
        out_specs=_row_spec(tm, D),
        out_shape=jax.ShapeDtypeStruct((T, D), _MXU_DTYPE),
        compiler_params=_params(("parallel",), 5 * _nbytes((tm, D), F32), 4 * _nbytes((tm, D), F32)),
    )(pg, pg, gb, yc, ym)


def _gate_bwd(pg, gb, yc, ym, dy):
    T, D = yc.shape
    tm = _tile(T, _ROW_TILE, _PACKED_ROWS)

    def body(g0_ref, g1_ref, gb_ref, yc_ref, ym_ref, dy_ref, dyc_ref, dym_ref, dpg_ref, dgb_ref):
        _init_acc([dgb_ref])
        dyv = dy_ref[...]
        s0 = _sig(g0_ref[...] + gb_ref[0:1, :])
        s1 = _sig(g1_ref[...] + gb_ref[1:2, :])
        dyc_ref[...] = (dyv * s0).astype(dyc_ref.dtype)
        dym_ref[...] = (dyv * s1).astype(dym_ref.dtype)
        d0 = dyv * yc_ref[...] * s0 * (1.0 - s0)
        d1 = dyv * ym_ref[...] * s1 * (1.0 - s1)
        dpg_ref[:, :D] = d0.astype(dpg_ref.dtype)
        dpg_ref[:, D:] = d1.astype(dpg_ref.dtype)
        dgb_ref[0:1, :] += _colsum(d0)
        dgb_ref[1:2, :] += _colsum(d1)

    return pl.pallas_call(
        body, name="gate_bwd", grid=(T // tm,),
        in_specs=[_row_spec(tm, D, 0), _row_spec(tm, D, 1), _full_spec((2, D)), _row_spec(tm, D), _row_spec(tm, D),
                  _row_spec(tm, D)],
        out_specs=[_row_spec(tm, D), _row_spec(tm, D), _row_spec(tm, 2 * D), _full_spec((2, D))],
        out_shape=[jax.ShapeDtypeStruct((T, D), _MXU_DTYPE)] * 2 + [jax.ShapeDtypeStruct((T, 2 * D), _MXU_DTYPE),
                                                                   jax.ShapeDtypeStruct((2, D), F32)],
        compiler_params=_params(("arbitrary",), 7 * _nbytes((tm, D), F32), 8 * _nbytes((tm, D), F32)),
    )(pg, pg, gb, yc, ym, dy)


def _place():
    return lax.axis_index("x"), lax.axis_index("y"), lax.axis_index("c")


def _all_gather(name, xs):
    R, C = xs.shape
    MESH = pl.DeviceIdType.MESH

    def body(x_ref, out_ref, send_sems, recv_sems, local_sem):
        x, y, c = _place()
        me, sibling = (x, y, c), (x, y, 1 - c)
        chips = [(1 - x, y), (x, 1 - y), (1 - x, 1 - y)]

        def rows(px, py, pc):
            return out_ref.at[4 * px + 2 * py + pc]

        def copy(k, block, to, src=None):
            return pltpu.make_async_remote_copy(
                src_ref=rows(*block) if src is None else src, dst_ref=rows(*block),
                send_sem=send_sems.at[k], recv_sem=recv_sems.at[k], device_id=to, device_id_type=MESH)

        mine = pltpu.make_async_copy(x_ref, rows(*me), local_sem)
        mine.start()
        first = [copy(0, me, sibling, src=x_ref)]
        first += [copy(1 + j, me, (*chip, c), src=x_ref) for j, chip in enumerate(chips)]
        for cp in first:
            cp.start()
        passed = [copy(4 + j, (*chip, c), sibling) for j, chip in enumerate(chips)]
        for j, chip in enumerate(chips):
            copy(1 + j, (*chip, c), me).wait_recv()
            passed[j].start()
        copy(0, sibling, me).wait_recv()
        for j, chip in enumerate(chips):
            copy(4 + j, (*chip, 1 - c), me).wait_recv()
        for cp in first + passed:
            cp.wait_send()
        mine.wait()

    return pl.pallas_call(
        body, name=name,
        out_shape=jax.ShapeDtypeStruct((N_DEV, R, C), xs.dtype),
        in_specs=[pl.BlockSpec(memory_space=pl.ANY)], out_specs=pl.BlockSpec(memory_space=pl.ANY),
        scratch_shapes=[pltpu.SemaphoreType.DMA((7,)), pltpu.SemaphoreType.DMA((7,)), pltpu.SemaphoreType.DMA],
    )(xs)


_HBM_SPEC = pl.BlockSpec(memory_space=pltpu.HBM)
_SEM_SPEC = pl.BlockSpec(memory_space=pltpu.SEMAPHORE)
_DATAFLOW = pltpu.SideEffectType.DATAFLOW_SIDE_EFFECTING


def _peers():
    x, y, c = _place()
    out = []
    for k in range(1, N_DEV):
        px = 1 - x if k & 4 else x
        py = 1 - y if k & 2 else y
        pc = 1 - c if k & 1 else c
        out.append((k - 1, (px, py, pc), 4 * px + 2 * py + pc))
    return 4 * x + 2 * y + c, out


def _exchange_copies(x_ref, land_ref, send_sems, recv_sems, scatter):
    me, peers = _peers()
    return [pltpu.make_async_remote_copy(
        src_ref=x_ref.at[idx] if scatter else x_ref, dst_ref=land_ref.at[me],
        send_sem=send_sems.at[k], recv_sem=recv_sems.at[k], device_id=dev, device_id_type=pl.DeviceIdType.MESH)
        for k, dev, idx in peers]


def _exchange_start(name, xs, land, scatter):
    def body(x_ref, land_ref, send_sems, recv_sems, x_thru, land_thru, token):
        for cp in _exchange_copies(x_ref, land_ref, send_sems, recv_sems, scatter):
            cp.start()
        token[...] = jnp.zeros_like(token)

    sems = pltpu.SemaphoreType.DMA((N_DEV - 1,))
    return pl.pallas_call(
        body, name=name,
        out_shape=(sems, sems, pltpu.HBM(xs.shape, xs.dtype), pltpu.HBM(land.shape, land.dtype),
                   jax.ShapeDtypeStruct((_SUBLANES, _LANES), F32)),
        in_specs=(_HBM_SPEC, _HBM_SPEC),
        out_specs=(_SEM_SPEC, _SEM_SPEC, _HBM_SPEC, _HBM_SPEC, pl.BlockSpec(memory_space=pltpu.VMEM)),
        input_output_aliases={0: 2, 1: 3},
        compiler_params=pltpu.CompilerParams(has_side_effects=_DATAFLOW),
    )(pltpu.with_memory_space_constraint(xs, pltpu.HBM), pltpu.with_memory_space_constraint(land, pltpu.HBM))


def _exchange_wait(name, send_sems, recv_sems, xs, land, after, scatter):
    def body(x_ref, land_ref, send_sems, recv_sems, after_ref, x_dead, got_ref):
        for cp in _exchange_copies(x_ref, land_ref, send_sems, recv_sems, scatter):
            cp.wait_send()
            cp.wait_recv()

    return pl.pallas_call(
        body, name=name,
        out_shape=(pltpu.HBM(xs.shape, xs.dtype), pltpu.HBM(land.shape, land.dtype)),
        in_specs=(_HBM_SPEC, _HBM_SPEC, _SEM_SPEC, _SEM_SPEC, pl.BlockSpec(memory_space=pl.ANY)),
        out_specs=(_HBM_SPEC, _HBM_SPEC), input_output_aliases={0: 0, 1: 1},
        compiler_params=pltpu.CompilerParams(has_side_effects=_DATAFLOW),
    )(xs, land, send_sems, recv_sems, after)[1]


def _own_slot(block, me):
    land = lax.empty((N_DEV,) + block.shape, block.dtype)
    return lax.dynamic_update_slice(land, block[None], (me,) + (0,) * block.ndim)


def _relay_copies(x_ref, land_ref, sems_a, sems_b):
    x, y, c = _place()
    sib = (x, y, 1 - c)
    chips = [(1 - x, y), (x, 1 - y), (1 - x, 1 - y)]
    mesh = pl.DeviceIdType.MESH
    (send_a, recv_a), (send_b, recv_b) = sems_a, sems_b or (None, None)
    first =[pltpu.make_async_remote_copy(src_ref=x_ref, dst_ref=land_ref.at[4 * x + 2 * y + c], send_sem=send_a.at[k],
                                          recv_sem=recv_a.at[k], device_id=dev, device_id_type=mesh)
             for k, dev in enumerate([sib] + [(*ch, c) for ch in chips])]
    relayed = [pltpu.make_async_remote_copy(src_ref=land_ref.at[4 * ch[0] + 2 * ch[1] + c], dst_ref=land_ref.at[4 * ch[0] + 2 * ch[1] + c],
                                            send_sem=send_b.at[j], recv_sem=recv_b.at[j], device_id=sib, device_id_type=mesh)
               for j, ch in enumerate(chips)] if sems_b else []
    return first, relayed


def _relay_start(name, xs, land):
    def body(x_ref, land_ref, send_a, recv_a, x_thru, land_thru, token):
        for cp in _relay_copies(x_ref, land_ref, (send_a, recv_a), None)[0]:
            cp.start()
        token[...] = jnp.zeros_like(token)

    sems = pltpu.SemaphoreType.DMA((4,))
    return pl.pallas_call(
        body, name=name,
        out_shape=(sems, sems, pltpu.HBM(xs.shape, xs.dtype), pltpu.HBM(land.shape, land.dtype),
                   jax.ShapeDtypeStruct((_SUBLANES, _LANES), F32)),
        in_specs=(_HBM_SPEC, _HBM_SPEC),
        out_specs=(_SEM_SPEC, _SEM_SPEC, _HBM_SPEC, _HBM_SPEC, pl.BlockSpec(memory_space=pltpu.VMEM)),
        input_output_aliases={0: 2, 1: 3},
        compiler_params=pltpu.CompilerParams(has_side_effects=_DATAFLOW),
    )(pltpu.with_memory_space_constraint(xs, pltpu.HBM), pltpu.with_memory_space_constraint(land, pltpu.HBM))


def _relay_forward(name, send_a, recv_a, xs, land, after):
    def body(x_ref, land_ref, send_a, recv_a, after_ref, x_thru, land_thru, send_b, recv_b):
        first, relayed = _relay_copies(x_ref, land_ref, (send_a, recv_a), (send_b, recv_b))
        for arrived, onward in zip(first[1:], relayed):
            arrived.wait_recv()
            onward.start()

    sems = pltpu.SemaphoreType.DMA((3,))
    return pl.pallas_call(
        body, name=name,
        out_shape=(pltpu.HBM(xs.shape, xs.dtype), pltpu.HBM(land.shape, land.dtype), sems, sems),
        in_specs=(_HBM_SPEC, _HBM_SPEC, _SEM_SPEC, _SEM_SPEC, pl.BlockSpec(memory_space=pl.ANY)),
        out_specs=(_HBM_SPEC, _HBM_SPEC, _SEM_SPEC, _SEM_SPEC), input_output_aliases={0: 0, 1: 1},
        compiler_params=pltpu.CompilerParams(has_side_effects=_DATAFLOW),
    )(xs, land, send_a, recv_a, after)


def _relay_wait(name, send_a, recv_a, send_b, recv_b, xs, land):
    def body(x_ref, land_ref, send_a, recv_a, send_b, recv_b, x_dead, got_ref):
        first, relayed = _relay_copies(x_ref, land_ref, (send_a, recv_a), (send_b, recv_b))
        for cp in first:
            cp.wait_send()
        first[0].wait_recv()
        for cp in relayed:
            cp.wait_send()
            cp.wait_recv()

    return pl.pallas_call(
        body, name=name,
        out_shape=(pltpu.HBM(xs.shape, xs.dtype), pltpu.HBM(land.shape, land.dtype)),
        in_specs=(_HBM_SPEC, _HBM_SPEC, _SEM_SPEC, _SEM_SPEC, _SEM_SPEC, _SEM_SPEC),
        out_specs=(_HBM_SPEC, _HBM_SPEC), input_output_aliases={0: 0, 1: 1},
        compiler_params=pltpu.CompilerParams(has_side_effects=_DATAFLOW),
    )(xs, land, send_a, recv_a, send_b, recv_b)[1]


def _sum_blocks(name, parts, scale=None):
    n, R, C = parts.shape
    tr = _tile(R, _PACK_ROW_MULT, _PACKED_ROWS)

    def body(p_ref, o_ref):
        acc = p_ref[0].astype(F32)
        for s in range(1, n):
            acc = acc + p_ref[s].astype(F32)
        if scale is not None:
            acc = acc * scale
        o_ref[...] = acc

    return pl.pallas_call(
        body, name=name, grid=(R // tr,),
        in_specs=[pl.BlockSpec((n, tr, C), lambda i: (0, i, 0))], out_specs=pl.BlockSpec((tr, C), lambda i: (i, 0)),
        out_shape=jax.ShapeDtypeStruct((R, C), F32),
        compiler_params=_params(("parallel",), _nbytes((n, tr, C), parts.dtype) + _nbytes((tr, C), F32), 2 * _nbytes((tr, C), F32)),
    )(parts)


def _adamw_math(wv, gv, mv, vv):
    mv = ADAM_B1 * mv + (1.0 - ADAM_B1) * gv
    vv = ADAM_B2 * vv + (1.0 - ADAM_B2) * (gv * gv)
    m_hat = mv / (1.0 - ADAM_B1 ** ADAM_STEP)
    v_hat = vv / (1.0 - ADAM_B2 ** ADAM_STEP)
    return -ADAM_LR * (m_hat / (jnp.sqrt(v_hat) + ADAM_EPS) + ADAM_WD * wv), mv, vv


def _adamw_layer(l, w, g, m, v, outs, after):
    depth, R, C = w.shape
    tr = _tile(R, _ROW_TILE, _SUBLANES)

    def body(w_ref, g_ref, m_ref, v_ref, after_ref, *rest):
        go_ref, d_ref, nm_ref, nv_ref = rest[-4:]
        gv = g_ref[...]
        go_ref[...] = gv
        d_ref[...], nm_ref[...], nv_ref[...] = _adamw_math(w_ref[...], gv, m_ref[...], v_ref[...])

    lay = pl.BlockSpec((None, tr, C), lambda i: (l, i, 0))
    shape = jax.ShapeDtypeStruct(w.shape, F32)
    return pl.pallas_call(
        body, name="adamw_layer", grid=(R // tr,),
        in_specs=[lay, pl.BlockSpec((tr, C), lambda i: (i, 0)), lay, lay] + [pl.BlockSpec(memory_space=pl.ANY)] * (5 if outs else 1),
        out_specs=[lay] * 4, out_shape=[shape] * 4,
        input_output_aliases={5 + k: k for k in range(4)} if outs else {},
        compiler_params=_params(("parallel",), 8 * _nbytes((tr, C), F32), 4 * _nbytes((tr, C), F32)),
    )(w, g, m, v, after, *(outs or ()))


def _adamw(w, g, m, v):
    shape = w.shape
    w2, g2, m2, v2 = (t.reshape(-1, shape[-1]) for t in (w, g, m, v))
    R, C = w2.shape
    tr = _tile(R, _ROW_TILE, _SUBLANES)

    def body(w_ref, g_ref, m_ref, v_ref, d_ref, nm_ref, nv_ref):
        d_ref[...], nm_ref[...], nv_ref[...] = _adamw_math(w_ref[...], g_ref[...], m_ref[...], v_ref[...])

    spec = pl.BlockSpec((tr, C), lambda i: (i, 0))
    outs = pl.pallas_call(
        body, name="adamw", grid=(R // tr,),
        in_specs=[spec] * 4, out_specs=[spec] * 3,
        out_shape=[jax.ShapeDtypeStruct((R, C), F32)] * 3,
        compiler_params=_params(("parallel",), 7 * _nbytes((tr, C), F32), 4 * _nbytes((tr, C), F32)),
    )(w2, g2, m2, v2)
    return tuple(o.reshape(shape) for o in outs)


def _pack_rows(parts, lead=()):
    out, spans, r0 = [], [], 0
    unit = _PACKED_ROWS * _PACK_W
    for p in parts:
        flat = p.reshape(lead + (-1,))
        size = flat.shape[-1]
        pad = (-size) % unit
        if pad:
            flat = jnp.pad(flat, [(0, 0)] * len(lead) + [(0, pad)])
        rows = (size + pad) // _PACK_W
        out.append(flat.reshape(lead + (rows, _PACK_W)))
        spans.append((r0, rows, p.shape[len(lead):]))
        r0 += rows
    pad = (-r0) % _PACK_ROW_MULT
    if pad:
        out.append(jnp.zeros(lead + (pad, _PACK_W), out[0].dtype))
    return jnp.concatenate(out, axis=len(lead)), spans


def _unpack_rows(packed, span, lead=()):
    r0, rows, shape = span
    size = 1
    for s in shape:
        size *= s
    sl = lax.slice_in_dim(packed, r0, r0 + rows, axis=len(lead))
    return sl.reshape(lead + (-1,))[..., :size].reshape(lead + tuple(shape))


def _pack_flat(parts):
    flat, spans, at = [], [], 0
    for p in parts:
        size = p.size
        padded = size + (-size) % _LANES
        flat.append(jnp.pad(p.reshape(-1), (0, padded - size)))
        spans.append((at, size, p.shape))
        at += padded
    tail = (-at) % (_PACKED_ROWS * _PACK_W)
    if tail:
        flat.append(jnp.zeros((tail,), flat[0].dtype))
    return jnp.concatenate(flat).reshape(-1, _PACK_W), spans


def _unpack_flat(packed, span, lead=()):
    at, size, shape = span
    return lax.slice_in_dim(packed.reshape(lead + (-1,)), at, at + size, axis=len(lead)).reshape(lead + tuple(shape))


def _rope_tables(positions, T):
    inv_freq = ROPE_THETA ** (-jnp.arange(0, ROPE_DIM, 2, dtype=F32) / ROPE_DIM)
    ang = positions.reshape(T, 1).astype(F32) * inv_freq
    cos, sin = jnp.cos(ang), jnp.sin(ang)
    half = ROPE_DIM // 2
    z = lambda n: jnp.zeros((T, n), F32)
    tail = HEAD_W - QK_DIM
    ct = jnp.concatenate([jnp.ones((T, NOPE_DIM), F32), cos, cos, z(tail)], axis=1)
    sa = jnp.concatenate([z(NOPE_DIM), -sin, z(half), z(tail)], axis=1)
    sb = jnp.concatenate([z(NOPE_DIM), z(half), sin, z(tail)], axis=1)
    return ct, sa, sb


def _pad_heads(wt, per_head, keep_from=0, keep=None):
    K = wt.shape[1]
    keep = per_head if keep is None else keep
    w3 = wt.reshape(N_HEADS, per_head, K)[:, keep_from:keep_from + keep]
    return jnp.pad(w3, ((0, 0), (0, HEAD_W - keep), (0, 0))).reshape(N_HEADS * HEAD_W, K)


def _unpad_heads(g, keep):
    return g.reshape(N_HEADS, HEAD_W, g.shape[1])[:, :keep]


def _layer_weights(W):
    D = W['w_out'].shape[1]
    C = W['w_conv_out'].shape[1]
    QL, KVL = W['w_uq'].shape[1], W['w_ukv'].shape[1]
    o1 = 2 * C
    o2, o3 = o1 + QL, o1 + QL + KVL
    o4 = o3 + ROPE_DIM
    win = W['w_in']
    zr = lambda n: jnp.zeros((n, D), win.dtype)
    L = dict(W)
    L['w_c'] = win[:o1]
    L['w_l'] = jnp.concatenate([win[o1:o3], zr(NOPE_DIM), win[o3:o4], zr(HEAD_W - QK_DIM)], axis=0)
    L['w_g'] = win[o4:]
    L['w_q'] = _pad_heads(W['w_uq'], QK_DIM)
    L['w_k'] = _pad_heads(W['w_ukv'], NOPE_DIM + V_DIM, 0, NOPE_DIM)
    L['w_v'] = _pad_heads(W['w_ukv'], NOPE_DIM + V_DIM, NOPE_DIM, V_DIM)
    wmo = W['w_mla_out'].reshape(D, N_HEADS, V_DIM)
    L['w_mo'] = jnp.pad(wmo, ((0, 0), (0, 0), (0, HEAD_W - V_DIM))).reshape(D, N_HEADS * HEAD_W)
    L['dims'] = (D, C, QL, KVL)
    return L


def _row(v, width=None):
    v = v.reshape(1, -1)
    if width is not None and v.shape[1] < width:
        v = jnp.pad(v, ((0, 0), (0, width - v.shape[1])))
    return v


def _ffn_fwd(x, h, wg, wu, wd, next_g):
    a, b, u = _ffn_up(h, wg, wu)
    if next_g is None:
        x_out, h_out = _mm("ffn_down", u, wd, 'nn', res=x, scale=0.5), None
    else:
        x_out, h_out = _mm_res_norm("ffn_down_norm", u, wd, x, 0.5, _row(next_g))
    return x_out, h_out, (x, h, a, b, u)


def _ffn_bwd_weights(dx, saved, wd, after):
    x, h, a, b, u = saved
    d_wd = _wgrad("ffn_dwd", u, dx, scale=0.5)
    da, db = _ffn_bwd_up(dx, wd, a, b, after)
    return da, db, (_wgrad("ffn_dwg", da, h), _wgrad("ffn_dwg", db, h), d_wd)


def _ffn_bwd_input(dx, saved, g, wg, wu, da, db, after):
    return _dh_rms_bwd("ffn_dh_rms", [(da, wg), (db, wu)], saved[0], _row(g), dx, after)


def _layer_fwd_rest(x1, h, ffn1, L, S, tabs, next_g):
    D, C, QL, KVL = L['dims']
    rope_cb = (QL + KVL) // HEAD_W
    pc = _mm("proj_c", h, L['w_c'], 'nt')
    pl_ = _mm("proj_l", h, L['w_l'], 'nt')
    pg = _mm("proj_g", h, L['w_g'], 'nt')
    conv_w = jnp.pad(S['conv_w'], ((0, HALO - CONV_WIDTH), (0, 0)))
    z, c = _conv_fwd(pc, conv_w, _row(S['conv_b']), _row(S['conv_ln_g']), _row(S['conv_ln_b']))
    yc = _mm("conv_out", c, L['w_conv_out'], 'nt')
    cqn, ckvn = _lat_norm_fwd(pl_, _row(S['cq_norm']), _row(S['ckv_norm']), QL, KVL)
    q_raw = _mm("up_q", cqn, L['w_q'], 'nt')
    k_raw = _mm("up_kv", ckvn, L['w_k'], 'nt')
    v = _mm("up_kv", ckvn, L['w_v'], 'nt', out_dtype=_MXU_DTYPE)
    gq, gk = _row(S['q_norm'], HEAD_W), _row(S['k_norm'], HEAD_W)
    qn, kn = _qk_prep_fwd(q_raw, k_raw, pl_, rope_cb, *tabs, gq, gk)
    o, lse = _attn_fwd(qn, kn, v)
    ym = _mm("mla_out", o, L['w_mo'], 'nt')
    y = _gate_fwd(pg, S['gate_bias'], yc, ym)
    x2, h2 = _mm_res_norm("mix_out_norm", y, L['w_out'], x1, 1.0, _row(S['ffn2_norm']))
    x3, h3, ffn2 = _ffn_fwd(x2, h2, L['ffn2_w_gate'], L['ffn2_w_up'], L['ffn2_w_down'], next_g)
    saved = dict(ffn1=ffn1, ffn2=ffn2, x1=x1, h=h, pc=pc, pl=pl_, pg=pg, z=z, c=c, yc=yc, cqn=cqn, ckvn=ckvn,
                 q_raw=q_raw, k_raw=k_raw, v=v, qn=qn, kn=kn, o=o, lse=lse, ym=ym, y=y, conv_w=conv_w, gq=gq, gk=gk)
    return x3, h3, saved


def _layer_bwd_mix(dx, L, S, tabs, A, after):
    D, C, QL, KVL = L['dims']
    rope_cb = (QL + KVL) // HEAD_W
    G, g = {}, {}
    da, db, (G['ffn2_w_gate'], G['ffn2_w_up'], G['ffn2_w_down']) = _ffn_bwd_weights(dx, A['ffn2'], L['ffn2_w_down'], after)
    dx, g['ffn2_norm'] = _ffn_bwd_input(dx, A['ffn2'], S['ffn2_norm'], L['ffn2_w_gate'], L['ffn2_w_up'], da, db, after)
    G['w_out'] = _wgrad("mix_dwout", A['y'], dx)
    dy = _mm("mix_dy", dx, L['w_out'], 'nt')
    dyc, dym, dpg, g['gate_bias'] = _gate_bwd(A['pg'], S['gate_bias'], A['yc'], A['ym'], dy)
    G['w_conv_out'] = _wgrad("conv_dwout", dyc, A['c'])
    dc = _mm("conv_dc", dyc, L['w_conv_out'], 'nn')
    dpc, dcw, g['conv_b'], g['conv_ln_g'], g['conv_ln_b'] = _conv_bwd(
        A['pc'], A['z'], dc, A['conv_w'], _row(S['conv_ln_g']), _row(S['conv_ln_b']))
    g['conv_w'] = dcw[:CONV_WIDTH]
    d_wmo = _wgrad("mla_dwout", dym, A['o'])
    G['w_mla_out'] = d_wmo.reshape(D, N_HEADS, HEAD_W)[:, :, :V_DIM].reshape(D, N_HEADS * V_DIM)
    do = _mm("mla_do", dym, L['w_mo'], 'nn', out_dtype=_MXU_DTYPE)
    dqn, dkn, dv = _attn_bwd(A['qn'], A['kn'], A['v'], A['o'], do, A['lse'])
    dq_raw, dk_raw, drope, dgq, dgk = _qk_prep_bwd(A['q_raw'], A['k_raw'], A['pl'], rope_cb, *tabs, A['gq'], A['gk'], dqn, dkn)
    g['q_norm'], g['k_norm'] = dgq[:, :QK_DIM], dgk[:, :QK_DIM]
    G['w_uq'] = _unpad_heads(_wgrad("up_dwq", dq_raw, A['cqn']), QK_DIM).reshape(N_HEADS * QK_DIM, QL)
    d_wk = _unpad_heads(_wgrad("up_dwkv", dk_raw, A['ckvn']), NOPE_DIM)
    d_wv = _unpad_heads(_wgrad("up_dwkv", dv, A['ckvn']), V_DIM)
    G['w_ukv'] = jnp.concatenate([d_wk, d_wv], axis=1).reshape(N_HEADS * (NOPE_DIM + V_DIM), KVL)
    dcq = _mm("up_dcq", dq_raw, L['w_q'], 'nn')
    dckv = _mm("up_dckv_k", dk_raw, L['w_k'], 'nn')
    dckv = _mm("up_dckv_v", dv, L['w_v'], 'nn', res=dckv)
    dpl, g['cq_norm'], g['ckv_norm'] = _lat_norm_bwd(A['pl'], _row(S['cq_norm']), _row(S['ckv_norm']), dcq, dckv, drope, QL, KVL)
    d_wc = _wgrad("proj_dwc", dpc, A['h'])
    d_wl = _wgrad("proj_dwl", dpl, A['h'])
    d_wg = _wgrad("proj_dwg", dpg, A['h'])
    ql = QL + KVL
    G['w_in'] = jnp.concatenate([d_wc, d_wl[:ql], d_wl[ql + NOPE_DIM:ql + NOPE_DIM + ROPE_DIM], d_wg], axis=0)
    dx, g['mix_norm'] = _dh_rms_bwd("proj_dh_rms", [(dpc, L['w_c']), (dpl, L['w_l']), (dpg, L['w_g'])],
                                    A['x1'], _row(S['mix_norm']), dx, after)
    return dx, G, g


def kernel(x, positions, ffn1_norm, ffn1_w_gate, ffn1_w_up, ffn1_w_down, mix_norm, w_in, gate_bias, conv_w, conv_b, conv_ln_g, conv_ln_b, w_conv_out, cq_norm, ckv_norm, w_uq, w_ukv, q_norm, k_norm, w_mla_out, w_out, ffn2_norm, ffn2_w_gate, ffn2_w_up, ffn2_w_down, loss_target, m_ffn1_norm, m_ffn1_w_gate, m_ffn1_w_up, m_ffn1_w_down, m_mix_norm, m_w_in, m_gate_bias, m_conv_w, m_conv_b, m_conv_ln_g, m_conv_ln_b, m_w_conv_out, m_cq_norm, m_ckv_norm, m_w_uq, m_w_ukv, m_q_norm, m_k_norm, m_w_mla_out, m_w_out, m_ffn2_norm, m_ffn2_w_gate, m_ffn2_w_up, m_ffn2_w_down, v_ffn1_norm, v_ffn1_w_gate, v_ffn1_w_up, v_ffn1_w_down, v_mix_norm, v_w_in, v_gate_bias, v_conv_w, v_conv_b, v_conv_ln_g, v_conv_ln_b, v_w_conv_out, v_cq_norm, v_ckv_norm, v_w_uq, v_w_ukv, v_q_norm, v_k_norm, v_w_mla_out, v_w_out, v_ffn2_norm, v_ffn2_w_gate, v_ffn2_w_up, v_ffn2_w_down):
    w = dict(zip(WEIGHTS, (ffn1_norm, ffn1_w_gate, ffn1_w_up, ffn1_w_down, mix_norm, w_in, gate_bias, conv_w, conv_b, conv_ln_g, conv_ln_b, w_conv_out, cq_norm, ckv_norm, w_uq, w_ukv, q_norm, k_norm, w_mla_out, w_out, ffn2_norm, ffn2_w_gate, ffn2_w_up, ffn2_w_down)))
    m = dict(zip(WEIGHTS, (m_ffn1_norm, m_ffn1_w_gate, m_ffn1_w_up, m_ffn1_w_down, m_mix_norm, m_w_in, m_gate_bias, m_conv_w, m_conv_b, m_conv_ln_g, m_conv_ln_b, m_w_conv_out, m_cq_norm, m_ckv_norm, m_w_uq, m_w_ukv, m_q_norm, m_k_norm, m_w_mla_out, m_w_out, m_ffn2_norm, m_ffn2_w_gate, m_ffn2_w_up, m_ffn2_w_down)))
    v = dict(zip(WEIGHTS, (v_ffn1_norm, v_ffn1_w_gate, v_ffn1_w_up, v_ffn1_w_down, v_mix_norm, v_w_in, v_gate_bias, v_conv_w, v_conv_b, v_conv_ln_g, v_conv_ln_b, v_w_conv_out, v_cq_norm, v_ckv_norm, v_w_uq, v_w_ukv, v_q_norm, v_k_norm, v_w_mla_out, v_w_out, v_ffn2_norm, v_ffn2_w_gate, v_ffn2_w_up, v_ffn2_w_down)))
    depth = ffn1_norm.shape[0]
    T, D = x.shape[1], x.shape[2]
    xs = x.reshape(T, D)
    target = loss_target.reshape(T, D)
    tabs = _rope_tables(positions, T)
    me = 4 * lax.axis_index("x") + 2 * lax.axis_index("y") + lax.axis_index("c")

    big = list(BIG)
    groups = [[n for n in big if n.startswith('ffn1_')], [n for n in big if not n.startswith('ffn1_')]]
    packed, wspans = {}, {}
    for l in range(depth):
        for gi, names in enumerate(groups):
            packed[l, gi], wspans[gi] = _pack_rows([(w[n][l].T if BIG[n] else w[n][l]).astype(_WIRE_DTYPE) for n in names])
    small_names = [(l, n) for l in range(depth) for n in SMALL_SHARDED]
    small_packed, small_spans = _pack_flat([w[n][l].T for l, n in small_names])
    small_flight = _exchange_start("small_weights_start", small_packed, _own_slot(small_packed, me), False)
    relay = _relay_start("gather0_start", packed[0, 1], _own_slot(packed[0, 1], me))
    gathered = {(0, 0): _all_gather("gather_weights", packed[0, 0] + relay[4][0, 0].astype(_WIRE_DTYPE))}
    send_sems, recv_sems, src, land, _ = small_flight
    small_gathered = _exchange_wait("small_weights_wait", send_sems, recv_sems, src, land, gathered[0, 0], False)

    def unpack(l, gi):
        W = {}
        for n, span in zip(groups[gi], wspans[gi]):
            sh = _unpack_rows(gathered[l, gi], span, (N_DEV,))
            W[n] = sh.reshape(N_DEV * sh.shape[1], sh.shape[2])
        return W

    def small_weights(l):
        S = {n: w[n][l] for n in REPLICATED}
        for (ll, n), span in zip(small_names, small_spans):
            if ll == l:
                sh = _unpack_flat(small_gathered, span, (N_DEV,))
                S[n] = sh.reshape(N_DEV * sh.shape[1], sh.shape[2]).T
        return S

    layers, acts, in_flight = [], [], {}
    hcur = xs
    hn = _rms_fwd(hcur, _row(w['ffn1_norm'][0]))
    for l in range(depth):
        S = small_weights(l)
        if l > 0:
            send_sems, recv_sems, src, land, _ = in_flight[l, 0]
            gathered[l, 0] = _exchange_wait(f"gather_wait_{l}_0", send_sems, recv_sems, src, land, hcur, False)
        W = unpack(l, 0)
        x1, h, ffn1 = _ffn_fwd(hcur, hn, W['ffn1_w_gate'], W['ffn1_w_up'], W['ffn1_w_down'], S['mix_norm'])
        if l == 0:
            send_a, recv_a, src, land, _ = relay
            src, land, send_b, recv_b = _relay_forward("gather0_relay", send_a, recv_a, src, land, x1)
            gathered[0, 1] = _relay_wait("gather0_wait", send_a, recv_a, send_b, recv_b, src, land)
            for ll in range(1, depth):
                for gi in range(len(groups)):
                    in_flight[ll, gi] = _exchange_start(f"gather_start_{ll}_{gi}", packed[ll, gi], _own_slot(packed[ll, gi], me), False)
        else:
            send_sems, recv_sems, src, land, _ = in_flight[l, 1]
            gathered[l, 1] = _exchange_wait(f"gather_wait_{l}_1", send_sems, recv_sems, src, land, x1, False)
        W.update(unpack(l, 1))
        layers.append((_layer_weights(W), S))
        hcur, hn, saved = _layer_fwd_rest(x1, h, ffn1, *layers[l], tabs, w['ffn1_norm'][l + 1] if l + 1 < depth else None)
        acts.append(saved)
    dx, loss_part = _loss_grad(hcur, target)

    stage_names = {'mix': [n for n in big if not n.startswith('ffn1_')], 'down': ['ffn1_w_down'],
                   'gateup': ['ffn1_w_gate', 'ffn1_w_up']}
    small_grads, gspans, in_flight = [{} for _ in range(depth)], {}, {}

    def send_stage(l, stage, G):
        blocks = [G[n].reshape(N_DEV, -1, G[n].shape[1]) for n in stage_names[stage]]
        send, gspans[stage] = _pack_rows(blocks, (N_DEV,))
        own = lax.dynamic_index_in_dim(send, me, 0, keepdims=False)
        in_flight[l, stage] = _exchange_start(f"scatter_start_{l}_{stage}", send, _own_slot(own, me), True)
        return in_flight[l, stage][4]

    after = jnp.zeros((_SUBLANES, _LANES), F32)
    for l in reversed(range(depth)):
        L, S = layers[l]
        dx, G, g = _layer_bwd_mix(dx, L, S, tabs, acts[l], after)
        small_grads[l].update(g)
        after = send_stage(l, 'mix', G)
        x_in, h, a, b, u = acts[l]['ffn1']
        after = after + send_stage(l, 'down', {'ffn1_w_down': _wgrad("ffn_dwd", u, dx, scale=0.5)})
        da, db = _ffn_bwd_up(dx, L['ffn1_w_down'], a, b, after)
        after = send_stage(l, 'gateup', {'ffn1_w_gate': _wgrad("ffn_dwg", da, h), 'ffn1_w_up': _wgrad("ffn_dwg", db, h)})
        dx, small_grads[l]['ffn1_norm'] = _ffn_bwd_input(dx, acts[l]['ffn1'], S['ffn1_norm'], L['ffn1_w_gate'],
                                                          L['ffn1_w_up'], da, db, after)
    small_list = [(l, n) for l in range(depth) for n in REPLICATED + SMALL_SHARDED]
    loss_part = loss_part + after[:1, :]
    small_send, sspans = _pack_flat([small_grads[l][n] for l, n in small_list] + [loss_part])
    small_flight = _exchange_start("small_grads_start", small_send, _own_slot(small_send, me), False)

    out4 = {}
    after = small_flight[4]
    for (l, stage), (send_sems, recv_sems, src, land, _) in in_flight.items():
        got = _exchange_wait(f"scatter_wait_{l}_{stage}", send_sems, recv_sems, src, land, after, True)
        reduced = _sum_blocks("sum_grads_direct", got)
        for n, span in zip(stage_names[stage], gspans[stage]):
            form = (lambda t: jnp.swapaxes(t, 1, 2)) if BIG[n] else (lambda t: t)
            out4[n] = _adamw_layer(l, form(w[n]), _unpack_rows(reduced, span), form(m[n]), form(v[n]), out4.get(n), after)
            after = out4[n][1]
    for n in big:
        if BIG[n]:
            out4[n] = [jnp.swapaxes(t, 1, 2) for t in out4[n]]
    send_sems, recv_sems, src, land, _ = small_flight
    small_all = _sum_blocks("sum_small", _exchange_wait("small_grads_wait", send_sems, recv_sems, src, land, after, False))
    loss = _unpack_flat(small_all, sspans[-1])[0, 0]
    for n in WEIGHTS:
        if n in BIG:
            continue
        per_layer = []
        for l in range(depth):
            gs = _unpack_flat(small_all, sspans[small_list.index((l, n))])
            if n in SMALL_SHARDED:
                cols = w[n].shape[-1]
                gs = lax.dynamic_slice_in_dim(gs, me * cols, cols, axis=1)
            per_layer.append(gs.reshape(w[n].shape[1:]))
        g = jnp.stack(per_layer, axis=0)
        out4[n] = (g,) + _adamw(w[n], g, m[n], v[n])
    return (loss, dx.reshape(1, T, D), *[out4[n][k] for k in range(4) for n in WEIGHTS])
```

```python
import jax
import jax.numpy as jnp
from jax import lax
from jax.experimental import pallas as pl
from jax.experimental.pallas import tpu as pltpu

F32 = jnp.float32
_MXU_DTYPE = jnp.bfloat16
_WIRE_DTYPE = jnp.bfloat16

_LANES = 128
_SUBLANES = 8
_PACKED_ROWS = 16
_V7X_VMEM_BYTES = 64 * 2 ** 20
_VMEM_HEADROOM = 8 * 2 ** 20

N_DEV = 8
EPS = 1e-6
CHUNK = 64
CONV_WIDTH = 31
N_HEADS = 8
NOPE_DIM = 64
ROPE_DIM = 32
QK_DIM = NOPE_DIM + ROPE_DIM
V_DIM = 64
HEAD_W = _LANES
ROPE_THETA = 10000.0
HALO = 32

ADAM_LR = 0.001
ADAM_B1 = 0.9
ADAM_B2 = 0.999
ADAM_EPS = 1e-08
ADAM_WD = 0.01
ADAM_STEP = 10

_ROW_TILE = 512
_MM_TILE = 512
_FFN_TILE = 512
_MM_TILE_N = 1408
_MM_TILE_K = 1024
_TN_TILE_M = 1408
_ATT_TILE = 512
_ATT_ROWS = 32
_ATT_HEADS = 2
_ATT_HEADS_FWD = 4
_PACK_W = 1024
_PACK_ROW_MULT = 128

WEIGHTS = ['ffn1_norm', 'ffn1_w_gate', 'ffn1_w_up', 'ffn1_w_down', 'mix_norm', 'w_in', 'gate_bias',
           'conv_w', 'conv_b', 'conv_ln_g', 'conv_ln_b', 'w_conv_out', 'cq_norm', 'ckv_norm', 'w_uq',
           'w_ukv', 'q_norm', 'k_norm', 'w_mla_out', 'w_out', 'ffn2_norm', 'ffn2_w_gate', 'ffn2_w_up',
           'ffn2_w_down']
BIG = {'ffn1_w_gate': True, 'ffn1_w_up': True, 'ffn1_w_down': False, 'w_in': True, 'w_conv_out': True,
       'w_uq': True, 'w_ukv': True, 'w_mla_out': True, 'w_out': False, 'ffn2_w_gate': True,
       'ffn2_w_up': True, 'ffn2_w_down': False}
SMALL_SHARDED = ['conv_w', 'gate_bias']
REPLICATED = ['ffn1_norm', 'mix_norm', 'conv_b', 'conv_ln_g', 'conv_ln_b', 'cq_norm', 'ckv_norm',
              'q_norm', 'k_norm', 'ffn2_norm']


def _tile(n, target, mult=_LANES):
    if n <= target:
        return n
    for d in range(target - target % mult, 0, -mult):
        if n % d == 0:
            return d
    return n


def _nbytes(shape, dtype):
    n = 1
    for s in shape:
        n *= s
    return n * jnp.dtype(dtype).itemsize


def _params(dims, block_bytes, extra_bytes=0):
    need = 2 * block_bytes + extra_bytes + _VMEM_HEADROOM
    assert need <= _V7X_VMEM_BYTES, (need, dims)
    return pltpu.CompilerParams(dimension_semantics=dims, vmem_limit_bytes=_V7X_VMEM_BYTES - _VMEM_HEADROOM)


def _mxu(x, scale=None):
    if scale is not None:
        x = x * scale
    return x if x.dtype == _MXU_DTYPE else x.astype(_MXU_DTYPE)


def _dot(a, b, mode):
    dims = {'nn': (((1,), (0,)), ((), ())), 'nt': (((1,), (1,)), ((), ())), 'tn': (((0,), (0,)), ((), ()))}[mode]
    return lax.dot_general(a, b, dims, preferred_element_type=F32)


def _colsum(x):
    return jnp.sum(x, axis=0, keepdims=True)


def _sig(x):
    return 0.5 * jnp.tanh(0.5 * x) + 0.5


def _mm(name, a, b, mode, *, out_dtype=F32, res=None, scale=None, a_scale=None, tm=None, tn=None, tk=None):
    if mode == 'nn':
        (M, K), (_, N) = a.shape, b.shape
    elif mode == 'nt':
        (M, K), (N, _) = a.shape, b.shape
    else:
        (K, M), (_, N) = a.shape, b.shape
    tm = _tile(M, tm or (_TN_TILE_M if mode == 'tn' else _MM_TILE))
    tn = _tile(N, tn or _MM_TILE_N)
    tk = _tile(K, tk or (_MM_TILE_K if mode == 'tn' else 4096))
    nk = K // tk
    grid = (M // tm, N // tn, nk)
    a_spec = pl.BlockSpec((tk, tm), lambda i, j, k: (k, i)) if mode == 'tn' else pl.BlockSpec((tm, tk), lambda i, j, k: (i, k))
    b_spec = pl.BlockSpec((tn, tk), lambda i, j, k: (j, k)) if mode == 'nt' else pl.BlockSpec((tk, tn), lambda i, j, k: (k, j))
    o_spec = pl.BlockSpec((tm, tn), lambda i, j, k: (i, j))
    has_res = res is not None

    def body(*refs):
        a_ref, b_ref = refs[0], refs[1]
        res_ref = refs[2] if has_res else None
        o_ref = refs[3] if has_res else refs[2]
        acc_ref = refs[-1] if nk > 1 else None

        def finish(p):
            if scale is not None:
                p = p * scale
            if has_res:
                p = res_ref[...] + p
            o_ref[...] = p.astype(o_ref.dtype)

        p = _dot(_mxu(a_ref[...], a_scale), _mxu(b_ref[...]), mode)
        if nk == 1:
            finish(p)
        else:
            k = pl.program_id(2)

            @pl.when(k == 0)
            def _():
                acc_ref[...] = p

            @pl.when(k > 0)
            def _():
                acc_ref[...] += p

            @pl.when(k == nk - 1)
            def _():
                finish(acc_ref[...])

    blocks = (_nbytes((tm, tk), a.dtype) + _nbytes((tk, tn), b.dtype) + _nbytes((tm, tn), out_dtype)
              + (_nbytes((tm, tn), F32) if has_res else 0))
    extra = ((2 + (nk > 1)) * _nbytes((tm, tn), F32) + (a.dtype != _MXU_DTYPE) * _nbytes((tm, tk), _MXU_DTYPE)
             + (b.dtype != _MXU_DTYPE) * _nbytes((tk, tn), _MXU_DTYPE))
    return pl.pallas_call(
        body, name=name, grid=grid,
        in_specs=[a_spec, b_spec] + ([o_spec] if has_res else []),
        out_specs=o_spec,
        out_shape=jax.ShapeDtypeStruct((M, N), out_dtype),
        scratch_shapes=[pltpu.VMEM((tm, tn), F32)] if nk > 1 else [],
        compiler_params=_params(("parallel", "parallel", "arbitrary"), blocks, extra),
    )(*([a, b] + ([res] if has_res else [])))


def _mm_res_norm(name, a, b, res, scale, g):
    (M, K), (_, N) = a.shape, b.shape
    tm = _tile(M, _MM_TILE)

    def body(a_ref, b_ref, res_ref, g_ref, o_ref, h_ref):
        xv = res_ref[...] + _dot(a_ref[...], b_ref[...], 'nn') * scale
        o_ref[...] = xv
        r = lax.rsqrt(jnp.mean(xv * xv, axis=-1, keepdims=True) + EPS)
        h_ref[...] = (xv * r * g_ref[...]).astype(h_ref.dtype)

    blocks = _nbytes((tm, K), a.dtype) + _nbytes((K, N), b.dtype) + 2 * _nbytes((tm, N), F32) + _nbytes((tm, N), _MXU_DTYPE)
    return pl.pallas_call(
        body, name=name, grid=(M // tm,),
        in_specs=[_row_spec(tm, K), _full_spec((K, N)), _row_spec(tm, N), _full_spec((1, N))],
        out_specs=[_row_spec(tm, N), _row_spec(tm, N)],
        out_shape=[jax.ShapeDtypeStruct((M, N), F32), jax.ShapeDtypeStruct((M, N), _MXU_DTYPE)],
        compiler_params=_params(("parallel",), blocks, 3 * _nbytes((tm, N), F32)),
    )(a, b, res, g)


def _wgrad(name, a, b, scale=None):
    return _mm(name, a, b, 'tn', out_dtype=_WIRE_DTYPE, scale=scale)


def _col_chunks(n, width=2 * _LANES):
    return [slice(c, min(c + width, n)) for c in range(0, n, width)]


def _ffn_up(h, wg, wu):
    T, D = h.shape
    F = wg.shape[0]
    tm, tn = _tile(T, _FFN_TILE), _tile(F, _MM_TILE_N)

    def body(h_ref, wg_ref, wu_ref, a_ref, b_ref, u_ref):
        hv = h_ref[...]
        for cols in _col_chunks(tn):
            a = _dot(hv, wg_ref[cols, :], 'nt')
            b = _dot(hv, wu_ref[cols, :], 'nt')
            s = _sig(a)
            sg = a * s
            a_ref[:, cols] = sg.astype(a_ref.dtype)
            b_ref[:, cols] = (b * (s * (1.0 + a * (1.0 - s)))).astype(b_ref.dtype)
            u_ref[:, cols] = (sg * b).astype(u_ref.dtype)

    w_spec = pl.BlockSpec((tn, D), lambda j, i: (j, 0))
    o_spec = pl.BlockSpec((tm, tn), lambda j, i: (i, j))
    blocks = _nbytes((tm, D), h.dtype) + 2 * _nbytes((tn, D), wg.dtype) + 2 * _nbytes((tm, tn), F32) + _nbytes((tm, tn), _MXU_DTYPE)
    return pl.pallas_call(
        body, name="ffn_up", grid=(F // tn, T // tm),
        in_specs=[pl.BlockSpec((tm, D), lambda j, i: (i, 0)), w_spec, w_spec],
        out_specs=[o_spec, o_spec, o_spec],
        out_shape=[jax.ShapeDtypeStruct((T, F), _MXU_DTYPE)] * 3,
        compiler_params=_params(("parallel", "parallel"), blocks, 4 * _nbytes((tm, tn), F32)),
    )(h, wg, wu)


def _ffn_bwd_up(dx, wd, a, b, after):
    T, D = dx.shape
    F = wd.shape[0]
    tm, tn = _tile(T, _FFN_TILE), _tile(F, _MM_TILE_N)

    def body(dx_ref, wd_ref, a_ref, b_ref, after_ref, da_ref, db_ref):
        dy = _mxu(dx_ref[...], 0.5)
        for cols in _col_chunks(tn):
            du = _dot(dy, wd_ref[cols, :], 'nt')
            db_ref[:, cols] = (du * a_ref[:, cols].astype(F32)).astype(db_ref.dtype)
            da_ref[:, cols] = (du * b_ref[:, cols].astype(F32)).astype(da_ref.dtype)

    t_spec = pl.BlockSpec((tm, tn), lambda j, i: (i, j))
    blocks = _nbytes((tm, D), F32) + _nbytes((tn, D), wd.dtype) + 2 * _nbytes((tm, tn), F32) + 2 * _nbytes((tm, tn), _MXU_DTYPE)
    return pl.pallas_call(
        body, name="ffn_bwd_up", grid=(F // tn, T // tm),
        in_specs=[pl.BlockSpec((tm, D), lambda j, i: (i, 0)), pl.BlockSpec((tn, D), lambda j, i: (j, 0)), t_spec, t_spec,
                  pl.BlockSpec(memory_space=pl.ANY)],
        out_specs=[t_spec, t_spec],
        out_shape=[jax.ShapeDtypeStruct((T, F), _MXU_DTYPE)] * 2,
        compiler_params=_params(("parallel", "parallel"), blocks, 5 * _nbytes((tm, tn), F32)),
    )(dx, wd, a, b, after)


def _row_spec(tm, w, cb=0):
    return pl.BlockSpec((tm, w), lambda i: (i, cb))


def _full_spec(shape):
    return pl.BlockSpec(shape, lambda i: (0,) * len(shape))


def _init_acc(refs):
    @pl.when(pl.program_id(0) == 0)
    def _():
        for r in refs:
            r[...] = jnp.zeros_like(r)


def _rms_fwd(x, g):
    T, D = x.shape
    tm = _tile(T, _ROW_TILE, _PACKED_ROWS)

    def body(x_ref, g_ref, h_ref):
        xv = x_ref[...]
        r = lax.rsqrt(jnp.mean(xv * xv, axis=-1, keepdims=True) + EPS)
        h_ref[...] = (xv * r * g_ref[...]).astype(h_ref.dtype)

    return pl.pallas_call(
        body, name="rms_fwd", grid=(T // tm,),
        in_specs=[_row_spec(tm, D), _full_spec((1, D))], out_specs=_row_spec(tm, D),
        out_shape=jax.ShapeDtypeStruct((T, D), _MXU_DTYPE),
        compiler_params=_params(("parallel",), 2 * _nbytes((tm, D), F32), 4 * _nbytes((tm, D), F32)),
    )(x, g)


def _dh_rms_bwd(name, pairs, x, g, dres, after):
    T, D = x.shape
    tm = _tile(T, _ROW_TILE // 2, _PACKED_ROWS)
    n = len(pairs)

    def body(*refs):
        x_ref, g_ref, dres_ref, after_ref, dx_ref, dg_ref = refs[2 * n:]
        _init_acc([dg_ref])
        dh = None
        for k in range(n):
            p = _dot(_mxu(refs[2 * k][...]), refs[2 * k + 1][...], 'nn')
            dh = p if dh is None else dh + p
        xv = x_ref[...]
        r = lax.rsqrt(jnp.mean(xv * xv, axis=-1, keepdims=True) + EPS)
        xh = xv * r
        dxh = dh * g_ref[...]
        dx_ref[...] = dres_ref[...] + r * (dxh - xh * jnp.mean(dxh * xh, axis=-1, keepdims=True))
        dg_ref[...] += _colsum(dh * xh)

    in_specs, args, blocks = [], [], 3 * _nbytes((tm, D), F32)
    for a, w in pairs:
        in_specs += [_row_spec(tm, a.shape[1]), _full_spec(w.shape)]
        args += [a, w]
        blocks += _nbytes((tm, a.shape[1]), a.dtype) + _nbytes(w.shape, w.dtype)
    return pl.pallas_call(
        body, name=name, grid=(T // tm,),
        in_specs=in_specs + [_row_spec(tm, D), _full_spec((1, D)), _row_spec(tm, D), pl.BlockSpec(memory_space=pl.ANY)],
        out_specs=[_row_spec(tm, D), _full_spec((1, D))],
        out_shape=[jax.ShapeDtypeStruct((T, D), F32), jax.ShapeDtypeStruct((1, D), F32)],
        compiler_params=_params(("arbitrary",), blocks, 6 * _nbytes((tm, D), F32)),
    )(*args, x, g, dres, after)


def _loss_grad(y, target):
    T, D = y.shape
    tm = _tile(T, _ROW_TILE, _SUBLANES)

    def body(y_ref, t_ref, dy_ref, l_ref):
        _init_acc([l_ref])
        err = y_ref[...] - t_ref[...]
        dy_ref[...] = err * (1.0 / D)
        per_row = jnp.sum(err * err, axis=-1, keepdims=True) * (0.5 / D)
        l_ref[...] += jnp.sum(per_row, axis=0, keepdims=True)

    return pl.pallas_call(
        body, name="loss_grad", grid=(T // tm,),
        in_specs=[_row_spec(tm, D), _row_spec(tm, D)],
        out_specs=[_row_spec(tm, D), _full_spec((1, _LANES))],
        out_shape=[jax.ShapeDtypeStruct((T, D), F32), jax.ShapeDtypeStruct((1, _LANES), F32)],
        compiler_params=_params(("arbitrary",), 3 * _nbytes((tm, D), F32), 3 * _nbytes((tm, D), F32)),
    )(y, target)


def _glu(pc, C):
    a, gate = pc[:, :C], pc[:, C:]
    s = _sig(gate)
    return a, s, a * s


def _tap_groups(offsets):
    groups = {}
    for k, off in enumerate(offsets):
        groups.setdefault(off % _SUBLANES, []).append((k, off - off % _SUBLANES))
    return sorted(groups.items())


def _conv_fwd(pc, w, cb, lg, lb):
    T, C2 = pc.shape
    C = C2 // 2
    tm = _tile(T, _ROW_TILE // 2, HALO)
    per = tm // HALO

    def body(pc_ref, ph_ref, w_ref, cb_ref, lg_ref, lb_ref, z_ref, c_ref, ubuf, vbuf):
        i = pl.program_id(0)
        _, _, u_cur = _glu(pc_ref[...], C)
        _, _, u_prev = _glu(ph_ref[...], C)
        ubuf[0:HALO, :] = jnp.where(i > 0, u_prev, 0.0)
        ubuf[HALO:HALO + tm, :] = u_cur
        ubuf[HALO + tm:, :] = jnp.zeros((_SUBLANES, C), F32)
        acc = jnp.zeros((tm, C), F32)
        for shift, taps in _tap_groups([HALO - (CONV_WIDTH - 1) + k for k in range(CONV_WIDTH)]):
            v = None
            for k, base in taps:
                term = w_ref[k:k + 1, :] * ubuf[base:base + tm + _SUBLANES, :]
                v = term if v is None else v + term
            if shift == 0:
                acc = acc + v[:tm]
            else:
                vbuf[...] = v
                acc = acc + vbuf[shift:shift + tm, :]
        z = acc + cb_ref[...]
        z_ref[...] = z
        zc = z - jnp.mean(z, axis=-1, keepdims=True)
        y = zc * lax.rsqrt(jnp.mean(zc * zc, axis=-1, keepdims=True) + EPS) * lg_ref[...] + lb_ref[...]
        c_ref[...] = (y * _sig(y)).astype(c_ref.dtype)

    return pl.pallas_call(
        body, name="conv_fwd", grid=(T // tm,),
        in_specs=[_row_spec(tm, C2), pl.BlockSpec((HALO, C2), lambda i: (jnp.maximum(i * per - 1, 0), 0)),
                  _full_spec((HALO, C)), _full_spec((1, C)), _full_spec((1, C)), _full_spec((1, C))],
        out_specs=[_row_spec(tm, C), _row_spec(tm, C)],
        out_shape=[jax.ShapeDtypeStruct((T, C), F32), jax.ShapeDtypeStruct((T, C), _MXU_DTYPE)],
        scratch_shapes=[pltpu.VMEM((tm + HALO + _SUBLANES, C), F32), pltpu.VMEM((tm + _SUBLANES, C), F32)],
        compiler_params=_params(("parallel",), 3 * _nbytes((tm, C2), F32), 10 * _nbytes((tm, C), F32)),
    )(pc, pc, w, cb, lg, lb)


def _conv_bwd(pc, z, dc, w, lg, lb):
    T, C2 = pc.shape
    C = C2 // 2
    tm = _tile(T, _ROW_TILE // 2, HALO)
    per = tm // HALO
    n = T // tm
    last_halo = T // HALO - 1

    def body(pc_ref, ph_ref, z_ref, zn_ref, dc_ref, dcn_ref, w_ref, lg_ref, lb_ref,
             dpc_ref, dw_ref, dcb_ref, dlg_ref, dlb_ref, ubuf, dzbuf, vbuf, sbuf):
        i = pl.program_id(0)
        _init_acc([dw_ref, dcb_ref, dlg_ref, dlb_ref])
        g, b = lg_ref[...], lb_ref[...]

        def ln_swish_bwd(zv, dcv):
            zc = zv - jnp.mean(zv, axis=-1, keepdims=True)
            r = lax.rsqrt(jnp.mean(zc * zc, axis=-1, keepdims=True) + EPS)
            zh = zc * r
            y = zh * g + b
            s = _sig(y)
            dy = dcv * (s * (1.0 + y * (1.0 - s)))
            dyg = dy * g
            dz = r * (dyg - jnp.mean(dyg, axis=-1, keepdims=True) - zh * jnp.mean(dyg * zh, axis=-1, keepdims=True))
            return dz, dy, zh

        dz_c, dy_c, zh_c = ln_swish_bwd(z_ref[...], dc_ref[...])
        dz_n, _, _ = ln_swish_bwd(zn_ref[...], dcn_ref[...])
        dlg_ref[...] += _colsum(dy_c * zh_c)
        dlb_ref[...] += _colsum(dy_c)
        dcb_ref[...] += _colsum(dz_c)
        dzbuf[0:tm, :] = dz_c
        dzbuf[tm:tm + HALO, :] = jnp.where(i < n - 1, dz_n, 0.0)
        a, s, u_cur = _glu(pc_ref[...], C)
        _, _, u_prev = _glu(ph_ref[...], C)
        ubuf[0:HALO, :] = jnp.where(i > 0, u_prev, 0.0)
        ubuf[HALO:HALO + tm, :] = u_cur
        ubuf[HALO + tm:, :] = jnp.zeros((_SUBLANES, C), F32)
        du = jnp.zeros((tm, C), F32)
        for shift, taps in _tap_groups([CONV_WIDTH - 1 - k for k in range(CONV_WIDTH)]):
            v = None
            for k, base in taps:
                term = w_ref[k:k + 1, :] * dzbuf[base:base + tm + _SUBLANES, :]
                v = term if v is None else v + term
            if shift == 0:
                du = du + v[:tm]
            else:
                vbuf[...] = v
                du = du + vbuf[shift:shift + tm, :]
        for shift, taps in _tap_groups([HALO - (CONV_WIDTH - 1) + k for k in range(CONV_WIDTH)]):
            if shift:
                sbuf[...] = ubuf[shift:shift + tm + HALO, :]
            src = sbuf if shift else ubuf
            for k, base in taps:
                dw_ref[k:k + 1, :] += _colsum(dz_c * src[base:base + tm, :])
        dpc_ref[:, :C] = (du * s).astype(dpc_ref.dtype)
        dpc_ref[:, C:] = (du * a * s * (1.0 - s)).astype(dpc_ref.dtype)

    nxt = lambda i: (jnp.minimum((i + 1) * per, last_halo), 0)
    vec = _full_spec((1, C))
    return pl.pallas_call(
        body, name="conv_bwd", grid=(n,),
        in_specs=[_row_spec(tm, C2), pl.BlockSpec((HALO, C2), lambda i: (jnp.maximum(i * per - 1, 0), 0)),
                  _row_spec(tm, C), pl.BlockSpec((HALO, C), nxt), _row_spec(tm, C), pl.BlockSpec((HALO, C), nxt),
                  _full_spec((HALO, C)), vec, vec],
        out_specs=[_row_spec(tm, C2), _full_spec((HALO, C)), vec, vec, vec],
        out_shape=[jax.ShapeDtypeStruct((T, C2), _MXU_DTYPE), jax.ShapeDtypeStruct((HALO, C), F32)]
                  + [jax.ShapeDtypeStruct((1, C), F32)] * 3,
        scratch_shapes=[pltpu.VMEM((tm + HALO + _SUBLANES, C), F32), pltpu.VMEM((tm + HALO + _SUBLANES, C), F32),
                        pltpu.VMEM((tm + _SUBLANES, C), F32), pltpu.VMEM((tm + HALO, C), F32)],
        compiler_params=_params(("arbitrary",), 5 * _nbytes((tm, C2), F32), 16 * _nbytes((tm, C), F32)),
    )(pc, pc, z, z, dc, dc, w, lg, lb)


def _lat_norm_fwd(pl_, gq, gkv, QL, KVL):
    T, W = pl_.shape
    tm = _tile(T, _ROW_TILE, _PACKED_ROWS)

    def body(p_ref, gq_ref, gkv_ref, cq_ref, ckv_ref):
        def norm(xv, gv):
            return xv * lax.rsqrt(jnp.mean(xv * xv, axis=-1, keepdims=True) + EPS) * gv
        cq_ref[...] = norm(p_ref[:, :QL], gq_ref[...]).astype(cq_ref.dtype)
        ckv_ref[...] = norm(p_ref[:, QL:QL + KVL], gkv_ref[...]).astype(ckv_ref.dtype)

    return pl.pallas_call(
        body, name="lat_norm_fwd", grid=(T // tm,),
        in_specs=[_row_spec(tm, W), _full_spec((1, QL)), _full_spec((1, KVL))],
        out_specs=[_row_spec(tm, QL), _row_spec(tm, KVL)],
        out_shape=[jax.ShapeDtypeStruct((T, QL), _MXU_DTYPE), jax.ShapeDtypeStruct((T, KVL), _MXU_DTYPE)],
        compiler_params=_params(("parallel",), 2 * _nbytes((tm, W), F32), 4 * _nbytes((tm, W), F32)),
    )(pl_, gq, gkv)


def _lat_norm_bwd(pl_, gq, gkv, dcq, dckv, drope, QL, KVL):
    T, W = pl_.shape
    tm = _tile(T, _ROW_TILE, _PACKED_ROWS)

    def body(p_ref, gq_ref, gkv_ref, dcq_ref, dckv_ref, dr_ref, dp_ref, dgq_ref, dgkv_ref):
        _init_acc([dgq_ref, dgkv_ref])

        def bwd(xv, gv, dv, dg_ref):
            r = lax.rsqrt(jnp.mean(xv * xv, axis=-1, keepdims=True) + EPS)
            xh = xv * r
            dxh = dv * gv
            dg_ref[...] += _colsum(dv * xh)
            return r * (dxh - xh * jnp.mean(dxh * xh, axis=-1, keepdims=True))

        dp_ref[:, :QL] = bwd(p_ref[:, :QL], gq_ref[...], dcq_ref[...], dgq_ref).astype(dp_ref.dtype)
        dp_ref[:, QL:QL + KVL] = bwd(p_ref[:, QL:QL + KVL], gkv_ref[...], dckv_ref[...], dgkv_ref).astype(dp_ref.dtype)
        dp_ref[:, QL + KVL:] = dr_ref[...].astype(dp_ref.dtype)

    return pl.pallas_call(
        body, name="lat_norm_bwd", grid=(T // tm,),
        in_specs=[_row_spec(tm, W), _full_spec((1, QL)), _full_spec((1, KVL)), _row_spec(tm, QL), _row_spec(tm, KVL),
                  _row_spec(tm, HEAD_W)],
        out_specs=[_row_spec(tm, W), _full_spec((1, QL)), _full_spec((1, KVL))],
        out_shape=[jax.ShapeDtypeStruct((T, W), _MXU_DTYPE), jax.ShapeDtypeStruct((1, QL), F32),
                   jax.ShapeDtypeStruct((1, KVL), F32)],
        compiler_params=_params(("arbitrary",), 4 * _nbytes((tm, W), F32), 6 * _nbytes((tm, W), F32)),
    )(pl_, gq, gkv, dcq, dckv, drope)


def _norm_rope(xv, gv, ct, sa, sb):
    r = lax.rsqrt(jnp.sum(xv * xv, axis=-1, keepdims=True) * (1.0 / QK_DIM) + EPS)
    xn = xv * r * gv
    half = ROPE_DIM // 2
    return xn * ct + pltpu.roll(xn, HEAD_W - half, 1) * sa + pltpu.roll(xn, half, 1) * sb


def _norm_rope_bwd(xv, gv, ct, sa, sb, dout):
    half = ROPE_DIM // 2
    dxn = dout * ct + pltpu.roll(dout * sa, half, 1) + pltpu.roll(dout * sb, HEAD_W - half, 1)
    r = lax.rsqrt(jnp.sum(xv * xv, axis=-1, keepdims=True) * (1.0 / QK_DIM) + EPS)
    xh = xv * r
    dxh = dxn * gv
    dx = r * (dxh - xh * (jnp.sum(dxh * xh, axis=-1, keepdims=True) * (1.0 / QK_DIM)))
    return dx, _colsum(dxn * xh)


def _qk_prep_fwd(q_raw, k_raw, pl_, rope_cb, ct, sa, sb, gq, gk):
    T, HW = q_raw.shape
    H = HW // HEAD_W
    tm = _tile(T, _ROW_TILE, _PACKED_ROWS)

    def body(q_ref, k_ref, r_ref, ct_ref, sa_ref, sb_ref, gq_ref, gk_ref, qn_ref, kn_ref):
        ctv, sav, sbv, rv = ct_ref[...], sa_ref[...], sb_ref[...], r_ref[...]
        for h in range(H):
            hs = slice(h * HEAD_W, (h + 1) * HEAD_W)
            qn_ref[:, hs] = _norm_rope(q_ref[:, hs], gq_ref[...], ctv, sav, sbv).astype(qn_ref.dtype)
            kn_ref[:, hs] = _norm_rope(k_ref[:, hs] + rv, gk_ref[...], ctv, sav, sbv).astype(kn_ref.dtype)

    tb = _row_spec(tm, HEAD_W)
    gb = _full_spec((1, HEAD_W))
    return pl.pallas_call(
        body, name="qk_prep_fwd", grid=(T // tm,),
        in_specs=[_row_spec(tm, HW), _row_spec(tm, HW), _row_spec(tm, HEAD_W, rope_cb), tb, tb, tb, gb, gb],
        out_specs=[_row_spec(tm, HW), _row_spec(tm, HW)],
        out_shape=[jax.ShapeDtypeStruct((T, HW), _MXU_DTYPE)] * 2,
        compiler_params=_params(("parallel",), 3 * _nbytes((tm, HW), F32) + 4 * _nbytes((tm, HEAD_W), F32), 8 * _nbytes((tm, HEAD_W), F32)),
    )(q_raw, k_raw, pl_, ct, sa, sb, gq, gk)


def _qk_prep_bwd(q_raw, k_raw, pl_, rope_cb, ct, sa, sb, gq, gk, dqn, dkn):
    T, HW = q_raw.shape
    H = HW // HEAD_W
    tm = _tile(T, _ROW_TILE, _PACKED_ROWS)

    def body(q_ref, k_ref, r_ref, ct_ref, sa_ref, sb_ref, gq_ref, gk_ref, dqn_ref, dkn_ref,
             dq_ref, dk_ref, dr_ref, dgq_ref, dgk_ref):
        _init_acc([dgq_ref, dgk_ref])
        ctv, sav, sbv, rv = ct_ref[...], sa_ref[...], sb_ref[...], r_ref[...]
        dr = dgq_sum = dgk_sum = None
        for h in range(H):
            hs = slice(h * HEAD_W, (h + 1) * HEAD_W)
            dq, dgq = _norm_rope_bwd(q_ref[:, hs], gq_ref[...], ctv, sav, sbv, dqn_ref[:, hs])
            dk, dgk = _norm_rope_bwd(k_ref[:, hs] + rv, gk_ref[...], ctv, sav, sbv, dkn_ref[:, hs])
            dq_ref[:, hs] = dq.astype(dq_ref.dtype)
            dk_ref[:, hs] = dk.astype(dk_ref.dtype)
            dr = dk if h == 0 else dr + dk
            dgq_sum = dgq if h == 0 else dgq_sum + dgq
            dgk_sum = dgk if h == 0 else dgk_sum + dgk
        dr_ref[...] = dr
        dgq_ref[...] += dgq_sum
        dgk_ref[...] += dgk_sum

    tb = _row_spec(tm, HEAD_W)
    gb = _full_spec((1, HEAD_W))
    hb = _row_spec(tm, HW)
    return pl.pallas_call(
        body, name="qk_prep_bwd", grid=(T // tm,),
        in_specs=[hb, hb, _row_spec(tm, HEAD_W, rope_cb), tb, tb, tb, gb, gb, hb, hb],
        out_specs=[hb, hb, tb, gb, gb],
        out_shape=[jax.ShapeDtypeStruct((T, HW), _MXU_DTYPE)] * 2 + [jax.ShapeDtypeStruct((T, HEAD_W), F32)]
                  + [jax.ShapeDtypeStruct((1, HEAD_W), F32)] * 2,
        compiler_params=_params(("arbitrary",), 5 * _nbytes((tm, HW), F32) + 5 * _nbytes((tm, HEAD_W), F32), 12 * _nbytes((tm, HEAD_W), F32)),
    )(q_raw, k_raw, pl_, ct, sa, sb, gq, gk, dqn, dkn)


_NEG = -1e30
_LOG2E = 1.4426950408889634
_SCORE_C = QK_DIM ** -0.5 * _LOG2E


def _pieces(t, rb, diag):
    assert CHUNK % rb == 0 and rb % _SUBLANES == 0
    out = []
    for r in range(t // rb):
        lo = (r * rb // CHUNK) * CHUNK if diag else 0
        for c in range(t // _LANES):
            cut = None if (c + 1) * _LANES <= lo else max(lo - c * _LANES, 0)
            out.append((slice(r * rb, (r + 1) * rb), slice(c * _LANES, (c + 1) * _LANES), c, cut))
    return out


def _groups(x):
    return x.reshape(x.shape[0] // _SUBLANES, _SUBLANES, x.shape[1])


def _all_sublanes(x8, op):
    return jnp.broadcast_to(op(x8, axis=0, keepdims=True), x8.shape)


def _lane_ge(rb, cut):
    return lax.broadcasted_iota(jnp.int32, (rb, _LANES), 1) >= cut


def _attn_fwd(q, k, v):
    T, HW = q.shape
    H = HW // HEAD_W
    t = _tile(T, _ATT_TILE)
    rb = min(_ATT_ROWS, t)
    nc = t // _LANES

    hp = _ATT_HEADS_FWD if H % _ATT_HEADS_FWD == 0 else 1
    W = hp * HEAD_W

    def body(q_ref, k_ref, v_ref, o_ref, lse_ref, s_s, p_s, m_s, l_s, acc_s):
        i = pl.program_id(1)
        m_s[...] = jnp.full_like(m_s, _NEG)
        l_s[...] = jnp.zeros_like(l_s)
        acc_s[...] = jnp.zeros_like(acc_s)

        def head_step(hh, rows, diag):
            hs = slice(hh * HEAD_W, (hh + 1) * HEAD_W)
            s_s[hh] = _dot(k_ref[rows, hs], q_ref[:, hs], 'nt')
            pieces = _pieces(t, rb, diag)

            def scores(rs, cs, cut):
                sb = s_s[hh, rs, cs]
                return jnp.where(_lane_ge(rb, cut), sb, _NEG) if cut else sb

            mx = [None] * nc
            for rs, cs, c, cut in pieces:
                if cut is not None:
                    g = jnp.max(_groups(scores(rs, cs, cut)), axis=0)
                    mx[c] = g if mx[c] is None else jnp.maximum(mx[c], g)
            m_new, alpha = [], []
            for c in range(nc):
                cs = slice(c * _LANES, (c + 1) * _LANES)
                m_prev = m_s[hh, :, cs]
                m_new.append(jnp.maximum(m_prev, _all_sublanes(mx[c], jnp.max)))
                alpha.append(jnp.exp2((m_prev - m_new[c]) * _SCORE_C))
                m_s[hh, :, cs] = m_new[c]
            lsum = [jnp.zeros((_SUBLANES, _LANES), F32)] * nc
            for rs, cs, c, cut in pieces:
                if cut is None:
                    p_s[hh, rs, cs] = jnp.zeros((rb, _LANES), p_s.dtype)
                    continue
                p = jnp.exp2((scores(rs, cs, cut) - jnp.tile(m_new[c], (rb // _SUBLANES, 1))) * _SCORE_C)
                lsum[c] = lsum[c] + jnp.sum(_groups(p), axis=0)
                p_s[hh, rs, cs] = p.astype(p_s.dtype)
            for c in range(nc):
                cs = slice(c * _LANES, (c + 1) * _LANES)
                l_s[hh, :, cs] = alpha[c] * l_s[hh, :, cs] + _all_sublanes(lsum[c], jnp.sum)
            a = jnp.tile(jnp.concatenate(alpha, axis=1), (HEAD_W // _SUBLANES, 1))
            acc_s[hh] = a * acc_s[hh] + _dot(v_ref[rows, hs], p_s[hh], 'tn')

        def step(j, diag):
            rows = pl.ds(pl.multiple_of(j * t, t), t)
            for hh in range(hp):
                head_step(hh, rows, diag)

        def loop_body(j, carry):
            step(j, False)
            return carry

        lax.fori_loop(0, i, loop_body, 0)
        step(i, True)
        for hh in range(hp):
            l = l_s[hh]
            o_t = acc_s[hh] / jnp.tile(l, (HEAD_W // _SUBLANES, 1))
            o_ref[:, hh * HEAD_W:(hh + 1) * HEAD_W] = o_t.T.astype(o_ref.dtype)
            lse_ref[hh * _SUBLANES:(hh + 1) * _SUBLANES, :] = m_s[hh] * _SCORE_C + jnp.log(l) * _LOG2E

    qb = pl.BlockSpec((t, W), lambda h, i: (i, h))
    kb = pl.BlockSpec((T, W), lambda h, i: (0, h))
    st = pltpu.VMEM((hp, _SUBLANES, t), F32)
    return pl.pallas_call(
        body, name="attn_fwd", grid=(H // hp, T // t),
        in_specs=[qb, kb, kb], out_specs=[qb, pl.BlockSpec((hp * _SUBLANES, t), lambda h, i: (h, i))],
        out_shape=[jax.ShapeDtypeStruct((T, HW), _MXU_DTYPE), jax.ShapeDtypeStruct((H * _SUBLANES, T), F32)],
        scratch_shapes=[pltpu.VMEM((hp, t, t), F32), pltpu.VMEM((hp, t, t), _MXU_DTYPE), st, st, pltpu.VMEM((hp, HEAD_W, t), F32)],
        compiler_params=_params(("parallel", "arbitrary"), 2 * _nbytes((T, W), _MXU_DTYPE) + 3 * _nbytes((t, W), F32),
                                4 * hp * _nbytes((t, t), F32)),
    )(q, k, v)


def _attn_bwd(q, k, v, o, do, lse2):
    T, HW = q.shape
    H = HW // HEAD_W
    t = _tile(T, _ATT_TILE)
    n = T // t
    rb = min(_ATT_ROWS, t)
    scale = QK_DIM ** -0.5

    hp = _ATT_HEADS if H % _ATT_HEADS == 0 else 1
    W = hp * HEAD_W

    def body(k_ref, v_ref, q_ref, o_ref, do_ref, lse_ref, dq_ref, dk_ref, dv_ref,
             s_s, dp_s, p_s, ds_s, dk_s, dv_s, delta_s, dqt_s):
        j = pl.program_id(1)

        @pl.when(j == 0)
        def _():
            dqt_s[...] = jnp.zeros_like(dqt_s)
            for hh in range(hp):
                hs = slice(hh * HEAD_W, (hh + 1) * HEAD_W)
                for b in range(n):
                    rs = slice(b * t, (b + 1) * t)
                    d = jnp.sum(do_ref[rs, hs].astype(F32) * o_ref[rs, hs].astype(F32), axis=1, keepdims=True)
                    delta_s[hh, :, rs] = jnp.broadcast_to(d, (t, HEAD_W)).T[0:_SUBLANES, :]

        dk_s[...] = jnp.zeros_like(dk_s)
        dv_s[...] = jnp.zeros_like(dv_s)

        def head_step(hh, base, diag):
            hs = slice(hh * HEAD_W, (hh + 1) * HEAD_W)
            kv, vv = k_ref[:, hs], v_ref[:, hs]
            qv, dov = q_ref[pl.ds(base, t), hs], do_ref[pl.ds(base, t), hs]
            s_s[hh] = _dot(kv, qv, 'nt')
            dp_s[hh] = _dot(vv, dov, 'nt')
            for rs, cs, c, cut in _pieces(t, rb, diag):
                if cut is None:
                    p_s[hh, rs, cs] = jnp.zeros((rb, _LANES), p_s.dtype)
                    ds_s[hh, rs, cs] = jnp.zeros((rb, _LANES), ds_s.dtype)
                    continue
                lanes = pl.ds(pl.multiple_of(base + c * _LANES, _LANES), _LANES)
                lse = jnp.tile(lse_ref[hh * _SUBLANES:(hh + 1) * _SUBLANES, lanes], (rb // _SUBLANES, 1))
                dl = jnp.tile(delta_s[hh, :, lanes], (rb // _SUBLANES, 1))
                p = jnp.exp2(s_s[hh, rs, cs] * _SCORE_C - lse)
                if cut:
                    p = jnp.where(_lane_ge(rb, cut), p, 0.0)
                p_s[hh, rs, cs] = p.astype(p_s.dtype)
                ds_s[hh, rs, cs] = (p * (dp_s[hh, rs, cs] - dl)).astype(ds_s.dtype)
            dsv = ds_s[hh]
            dv_s[hh] += _dot(p_s[hh], dov, 'nn')
            dk_s[hh] += _dot(dsv, qv, 'nn')
            dqt_s[hh, :, pl.ds(base, t)] += _dot(kv, dsv, 'tn')

        def step(i, diag):
            base = pl.multiple_of(i * t, t)
            for hh in range(hp):
                head_step(hh, base, diag)

        def loop_body(i, carry):
            step(i, False)
            return carry

        step(j, True)
        lax.fori_loop(j + 1, n, loop_body, 0)
        for hh in range(hp):
            hs = slice(hh * HEAD_W, (hh + 1) * HEAD_W)
            dk_ref[:, hs] = dk_s[hh] * scale
            dv_ref[:, hs] = dv_s[hh].astype(dv_ref.dtype)

        @pl.when(j == n - 1)
        def _():
            for hh in range(hp):
                for b in range(n):
                    rs = slice(b * t, (b + 1) * t)
                    dq_ref[rs, hh * HEAD_W:(hh + 1) * HEAD_W] = dqt_s[hh, :, rs].T * scale

    jb = pl.BlockSpec((t, W), lambda h, j: (j, h))
    fb = pl.BlockSpec((T, W), lambda h, j: (0, h))
    sc = pltpu.VMEM((hp, t, HEAD_W), F32)
    return pl.pallas_call(
        body, name="attn_bwd", grid=(H // hp, n),
        in_specs=[jb, jb, fb, fb, fb, pl.BlockSpec((hp * _SUBLANES, T), lambda h, j: (h, 0))], out_specs=[fb, jb, jb],
        out_shape=[jax.ShapeDtypeStruct((T, HW), F32), jax.ShapeDtypeStruct((T, HW), F32),
                   jax.ShapeDtypeStruct((T, HW), _MXU_DTYPE)],
        scratch_shapes=[pltpu.VMEM((hp, t, t), F32), pltpu.VMEM((hp, t, t), F32), pltpu.VMEM((hp, t, t), _MXU_DTYPE),
                        pltpu.VMEM((hp, t, t), _MXU_DTYPE), sc, sc, pltpu.VMEM((hp, _SUBLANES, T), F32),
                        pltpu.VMEM((hp, HEAD_W, T), F32)],
        compiler_params=_params(("parallel", "arbitrary"), 3 * _nbytes((T, W), _MXU_DTYPE) + _nbytes((T, W), F32)
                                + 4 * _nbytes((t, W), F32), hp * (4 * _nbytes((t, t), F32) + _nbytes((T, HEAD_W), F32))),
    )(k, v, q, o, do, lse2)


def _gate_fwd(pg, gb, yc, ym):
    T, D = yc.shape
    tm = _tile(T, _ROW_TILE, _PACKED_ROWS)

    def body(g0_ref, g1_ref, gb_ref, yc_ref, ym_ref, y_ref):
        s0 = _sig(g0_ref[...] + gb_ref[0:1, :])
        s1 = _sig(g1_ref[...] + gb_ref[1:2, :])
        y_ref[...] = (s0 * yc_ref[...] + s1 * ym_ref[...]).astype(y_ref.dtype)

    return pl.pallas_call(
        body, name="gate_fwd", grid=(T // tm,),
        in_specs=[_row_spec(tm, D, 0), _row_spec(tm, D, 1), _full_spec((2, D)), _row_spec(tm, D), _row_spec(tm, D)],
        out_specs=_row_spec(tm, D),
        out_shape=jax.ShapeDtypeStruct((T, D), _MXU_DTYPE),
        compiler_params=_params(("parallel",), 5 * _nbytes((tm, D), F32), 4 * _nbytes((tm, D), F32)),
    )(pg, pg, gb, yc, ym)


def _gate_bwd(pg, gb, yc, ym, dy):
    T, D = yc.shape
    tm = _tile(T, _ROW_TILE, _PACKED_ROWS)

    def body(g0_ref, g1_ref, gb_ref, yc_ref, ym_ref, dy_ref, dyc_ref, dym_ref, dpg_ref, dgb_ref):
        _init_acc([dgb_ref])
        dyv = dy_ref[...]
        s0 = _sig(g0_ref[...] + gb_ref[0:1, :])
        s1 = _sig(g1_ref[...] + gb_ref[1:2, :])
        dyc_ref[...] = (dyv * s0).astype(dyc_ref.dtype)
        dym_ref[...] = (dyv * s1).astype(dym_ref.dtype)
        d0 = dyv * yc_ref[...] * s0 * (1.0 - s0)
        d1 = dyv * ym_ref[...] * s1 * (1.0 - s1)
        dpg_ref[:, :D] = d0.astype(dpg_ref.dtype)
        dpg_ref[:, D:] = d1.astype(dpg_ref.dtype)
        dgb_ref[0:1, :] += _colsum(d0)
        dgb_ref[1:2, :] += _colsum(d1)

    return pl.pallas_call(
        body, name="gate_bwd", grid=(T // tm,),
        in_specs=[_row_spec(tm, D, 0), _row_spec(tm, D, 1), _full_spec((2, D)), _row_spec(tm, D), _row_spec(tm, D),
                  _row_spec(tm, D)],
        out_specs=[_row_spec(tm, D), _row_spec(tm, D), _row_spec(tm, 2 * D), _full_spec((2, D))],
        out_shape=[jax.ShapeDtypeStruct((T, D), _MXU_DTYPE)] * 2 + [jax.ShapeDtypeStruct((T, 2 * D), _MXU_DTYPE),
                                                                   jax.ShapeDtypeStruct((2, D), F32)],
        compiler_params=_params(("arbitrary",), 7 * _nbytes((tm, D), F32), 8 * _nbytes((tm, D), F32)),
    )(pg, pg, gb, yc, ym, dy)


def _place():
    return lax.axis_index("x"), lax.axis_index("y"), lax.axis_index("c")


def _all_gather(name, xs):
    R, C = xs.shape
    MESH = pl.DeviceIdType.MESH

    def body(x_ref, out_ref, send_sems, recv_sems, local_sem):
        x, y, c = _place()
        me, sibling = (x, y, c), (x, y, 1 - c)
        chips = [(1 - x, y), (x, 1 - y), (1 - x, 1 - y)]

        def rows(px, py, pc):
            return out_ref.at[4 * px + 2 * py + pc]

        def copy(k, block, to, src=None):
            return pltpu.make_async_remote_copy(
                src_ref=rows(*block) if src is None else src, dst_ref=rows(*block),
                send_sem=send_sems.at[k], recv_sem=recv_sems.at[k], device_id=to, device_id_type=MESH)

        mine = pltpu.make_async_copy(x_ref, rows(*me), local_sem)
        mine.start()
        first = [copy(0, me, sibling, src=x_ref)]
        first += [copy(1 + j, me, (*chip, c), src=x_ref) for j, chip in enumerate(chips)]
        for cp in first:
            cp.start()
        passed = [copy(4 + j, (*chip, c), sibling) for j, chip in enumerate(chips)]
        for j, chip in enumerate(chips):
            copy(1 + j, (*chip, c), me).wait_recv()
            passed[j].start()
        copy(0, sibling, me).wait_recv()
        for j, chip in enumerate(chips):
            copy(4 + j, (*chip, 1 - c), me).wait_recv()
        for cp in first + passed:
            cp.wait_send()
        mine.wait()

    return pl.pallas_call(
        body, name=name,
        out_shape=jax.ShapeDtypeStruct((N_DEV, R, C), xs.dtype),
        in_specs=[pl.BlockSpec(memory_space=pl.ANY)], out_specs=pl.BlockSpec(memory_space=pl.ANY),
        scratch_shapes=[pltpu.SemaphoreType.DMA((7,)), pltpu.SemaphoreType.DMA((7,)), pltpu.SemaphoreType.DMA],
    )(xs)


_HBM_SPEC = pl.BlockSpec(memory_space=pltpu.HBM)
_SEM_SPEC = pl.BlockSpec(memory_space=pltpu.SEMAPHORE)
_DATAFLOW = pltpu.SideEffectType.DATAFLOW_SIDE_EFFECTING


def _peers():
    x, y, c = _place()
    out = []
    for k in range(1, N_DEV):
        px = 1 - x if k & 4 else x
        py = 1 - y if k & 2 else y
        pc = 1 - c if k & 1 else c
        out.append((k - 1, (px, py, pc), 4 * px + 2 * py + pc))
    return 4 * x + 2 * y + c, out


def _exchange_copies(x_ref, land_ref, send_sems, recv_sems, scatter):
    me, peers = _peers()
    return [pltpu.make_async_remote_copy(
        src_ref=x_ref.at[idx] if scatter else x_ref, dst_ref=land_ref.at[me],
        send_sem=send_sems.at[k], recv_sem=recv_sems.at[k], device_id=dev, device_id_type=pl.DeviceIdType.MESH)
        for k, dev, idx in peers]


def _exchange_start(name, xs, land, scatter):
    def body(x_ref, land_ref, send_sems, recv_sems, x_thru, land_thru, token):
        for cp in _exchange_copies(x_ref, land_ref, send_sems, recv_sems, scatter):
            cp.start()
        token[...] = jnp.zeros_like(token)

    sems = pltpu.SemaphoreType.DMA((N_DEV - 1,))
    return pl.pallas_call(
        body, name=name,
        out_shape=(sems, sems, pltpu.HBM(xs.shape, xs.dtype), pltpu.HBM(land.shape, land.dtype),
                   jax.ShapeDtypeStruct((_SUBLANES, _LANES), F32)),
        in_specs=(_HBM_SPEC, _HBM_SPEC),
        out_specs=(_SEM_SPEC, _SEM_SPEC, _HBM_SPEC, _HBM_SPEC, pl.BlockSpec(memory_space=pltpu.VMEM)),
        input_output_aliases={0: 2, 1: 3},
        compiler_params=pltpu.CompilerParams(has_side_effects=_DATAFLOW),
    )(pltpu.with_memory_space_constraint(xs, pltpu.HBM), pltpu.with_memory_space_constraint(land, pltpu.HBM))


def _exchange_wait(name, send_sems, recv_sems, xs, land, after, scatter):
    def body(x_ref, land_ref, send_sems, recv_sems, after_ref, x_dead, got_ref):
        for cp in _exchange_copies(x_ref, land_ref, send_sems, recv_sems, scatter):
            cp.wait_send()
            cp.wait_recv()

    return pl.pallas_call(
        body, name=name,
        out_shape=(pltpu.HBM(xs.shape, xs.dtype), pltpu.HBM(land.shape, land.dtype)),
        in_specs=(_HBM_SPEC, _HBM_SPEC, _SEM_SPEC, _SEM_SPEC, pl.BlockSpec(memory_space=pl.ANY)),
        out_specs=(_HBM_SPEC, _HBM_SPEC), input_output_aliases={0: 0, 1: 1},
        compiler_params=pltpu.CompilerParams(has_side_effects=_DATAFLOW),
    )(xs, land, send_sems, recv_sems, after)[1]


def _own_slot(block, me):
    land = lax.empty((N_DEV,) + block.shape, block.dtype)
    return lax.dynamic_update_slice(land, block[None], (me,) + (0,) * block.ndim)


def _relay_copies(x_ref, land_ref, sems_a, sems_b):
    x, y, c = _place()
    sib = (x, y, 1 - c)
    chips = [(1 - x, y), (x, 1 - y), (1 - x, 1 - y)]
    mesh = pl.DeviceIdType.MESH
    (send_a, recv_a), (send_b, recv_b) = sems_a, sems_b or (None, None)
    first =[pltpu.make_async_remote_copy(src_ref=x_ref, dst_ref=land_ref.at[4 * x + 2 * y + c], send_sem=send_a.at[k],
                                          recv_sem=recv_a.at[k], device_id=dev, device_id_type=mesh)
             for k, dev in enumerate([sib] + [(*ch, c) for ch in chips])]
    relayed = [pltpu.make_async_remote_copy(src_ref=land_ref.at[4 * ch[0] + 2 * ch[1] + c], dst_ref=land_ref.at[4 * ch[0] + 2 * ch[1] + c],
                                            send_sem=send_b.at[j], recv_sem=recv_b.at[j], device_id=sib, device_id_type=mesh)
               for j, ch in enumerate(chips)] if sems_b else []
    return first, relayed


def _relay_start(name, xs, land):
    def body(x_ref, land_ref, send_a, recv_a, x_thru, land_thru, token):
        for cp in _relay_copies(x_ref, land_ref, (send_a, recv_a), None)[0]:
            cp.start()
        token[...] = jnp.zeros_like(token)

    sems = pltpu.SemaphoreType.DMA((4,))
    return pl.pallas_call(
        body, name=name,
        out_shape=(sems, sems, pltpu.HBM(xs.shape, xs.dtype), pltpu.HBM(land.shape, land.dtype),
                   jax.ShapeDtypeStruct((_SUBLANES, _LANES), F32)),
        in_specs=(_HBM_SPEC, _HBM_SPEC),
        out_specs=(_SEM_SPEC, _SEM_SPEC, _HBM_SPEC, _HBM_SPEC, pl.BlockSpec(memory_space=pltpu.VMEM)),
        input_output_aliases={0: 2, 1: 3},
        compiler_params=pltpu.CompilerParams(has_side_effects=_DATAFLOW),
    )(pltpu.with_memory_space_constraint(xs, pltpu.HBM), pltpu.with_memory_space_constraint(land, pltpu.HBM))


def _relay_forward(name, send_a, recv_a, xs, land, after):
    def body(x_ref, land_ref, send_a, recv_a, after_ref, x_thru, land_thru, send_b, recv_b):
        first, relayed = _relay_copies(x_ref, land_ref, (send_a, recv_a), (send_b, recv_b))
        for arrived, onward in zip(first[1:], relayed):
            arrived.wait_recv()
            onward.start()

    sems = pltpu.SemaphoreType.DMA((3,))
    return pl.pallas_call(
        body, name=name,
        out_shape=(pltpu.HBM(xs.shape, xs.dtype), pltpu.HBM(land.shape, land.dtype), sems, sems),
        in_specs=(_HBM_SPEC, _HBM_SPEC, _SEM_SPEC, _SEM_SPEC, pl.BlockSpec(memory_space=pl.ANY)),
        out_specs=(_HBM_SPEC, _HBM_SPEC, _SEM_SPEC, _SEM_SPEC), input_output_aliases={0: 0, 1: 1},
        compiler_params=pltpu.CompilerParams(has_side_effects=_DATAFLOW),
    )(xs, land, send_a, recv_a, after)


def _relay_wait(name, send_a, recv_a, send_b, recv_b, xs, land):
    def body(x_ref, land_ref, send_a, recv_a, send_b, recv_b, x_dead, got_ref):
        first, relayed = _relay_copies(x_ref, land_ref, (send_a, recv_a), (send_b, recv_b))
        for cp in first:
            cp.wait_send()
        first[0].wait_recv()
        for cp in relayed:
            cp.wait_send()
            cp.wait_recv()

    return pl.pallas_call(
        body, name=name,
        out_shape=(pltpu.HBM(xs.shape, xs.dtype), pltpu.HBM(land.shape, land.dtype)),
        in_specs=(_HBM_SPEC, _HBM_SPEC, _SEM_SPEC, _SEM_SPEC, _SEM_SPEC, _SEM_SPEC),
        out_specs=(_HBM_SPEC, _HBM_SPEC), input_output_aliases={0: 0, 1: 1},
        compiler_params=pltpu.CompilerParams(has_side_effects=_DATAFLOW),
    )(xs, land, send_a, recv_a, send_b, recv_b)[1]


def _sum_blocks(name, parts, scale=None):
    n, R, C = parts.shape
    tr = _tile(R, _PACK_ROW_MULT, _PACKED_ROWS)

    def body(p_ref, o_ref):
        acc = p_ref[0].astype(F32)
        for s in range(1, n):
            acc = acc + p_ref[s].astype(F32)
        if scale is not None:
            acc = acc * scale
        o_ref[...] = acc

    return pl.pallas_call(
        body, name=name, grid=(R // tr,),
        in_specs=[pl.BlockSpec((n, tr, C), lambda i: (0, i, 0))], out_specs=pl.BlockSpec((tr, C), lambda i: (i, 0)),
        out_shape=jax.ShapeDtypeStruct((R, C), F32),
        compiler_params=_params(("parallel",), _nbytes((n, tr, C), parts.dtype) + _nbytes((tr, C), F32), 2 * _nbytes((tr, C), F32)),
    )(parts)


def _adamw_math(wv, gv, mv, vv):
    mv = ADAM_B1 * mv + (1.0 - ADAM_B1) * gv
    vv = ADAM_B2 * vv + (1.0 - ADAM_B2) * (gv * gv)
    m_hat = mv / (1.0 - ADAM_B1 ** ADAM_STEP)
    v_hat = vv / (1.0 - ADAM_B2 ** ADAM_STEP)
    return -ADAM_LR * (m_hat / (jnp.sqrt(v_hat) + ADAM_EPS) + ADAM_WD * wv), mv, vv


def _adamw_layer(l, w, g, m, v, outs, after):
    depth, R, C = w.shape
    tr = _tile(R, _ROW_TILE, _SUBLANES)

    def body(w_ref, g_ref, m_ref, v_ref, after_ref, *rest):
        go_ref, d_ref, nm_ref, nv_ref = rest[-4:]
        gv = g_ref[...]
        go_ref[...] = gv
        d_ref[...], nm_ref[...], nv_ref[...] = _adamw_math(w_ref[...], gv, m_ref[...], v_ref[...])

    lay = pl.BlockSpec((None, tr, C), lambda i: (l, i, 0))
    shape = jax.ShapeDtypeStruct(w.shape, F32)
    return pl.pallas_call(
        body, name="adamw_layer", grid=(R // tr,),
        in_specs=[lay, pl.BlockSpec((tr, C), lambda i: (i, 0)), lay, lay] + [pl.BlockSpec(memory_space=pl.ANY)] * (5 if outs else 1),
        out_specs=[lay] * 4, out_shape=[shape] * 4,
        input_output_aliases={5 + k: k for k in range(4)} if outs else {},
        compiler_params=_params(("parallel",), 8 * _nbytes((tr, C), F32), 4 * _nbytes((tr, C), F32)),
    )(w, g, m, v, after, *(outs or ()))


def _adamw(w, g, m, v):
    shape = w.shape
    w2, g2, m2, v2 = (t.reshape(-1, shape[-1]) for t in (w, g, m, v))
    R, C = w2.shape
    tr = _tile(R, _ROW_TILE, _SUBLANES)

    def body(w_ref, g_ref, m_ref, v_ref, d_ref, nm_ref, nv_ref):
        d_ref[...], nm_ref[...], nv_ref[...] = _adamw_math(w_ref[...], g_ref[...], m_ref[...], v_ref[...])

    spec = pl.BlockSpec((tr, C), lambda i: (i, 0))
    outs = pl.pallas_call(
        body, name="adamw", grid=(R // tr,),
        in_specs=[spec] * 4, out_specs=[spec] * 3,
        out_shape=[jax.ShapeDtypeStruct((R, C), F32)] * 3,
        compiler_params=_params(("parallel",), 7 * _nbytes((tr, C), F32), 4 * _nbytes((tr, C), F32)),
    )(w2, g2, m2, v2)
    return tuple(o.reshape(shape) for o in outs)


def _pack_rows(parts, lead=()):
    out, spans, r0 = [], [], 0
    unit = _PACKED_ROWS * _PACK_W
    for p in parts:
        flat = p.reshape(lead + (-1,))
        size = flat.shape[-1]
        pad = (-size) % unit
        if pad:
            flat = jnp.pad(flat, [(0, 0)] * len(lead) + [(0, pad)])
        rows = (size + pad) // _PACK_W
        out.append(flat.reshape(lead + (rows, _PACK_W)))
        spans.append((r0, rows, p.shape[len(lead):]))
        r0 += rows
    pad = (-r0) % _PACK_ROW_MULT
    if pad:
        out.append(jnp.zeros(lead + (pad, _PACK_W), out[0].dtype))
    return jnp.concatenate(out, axis=len(lead)), spans


def _unpack_rows(packed, span, lead=()):
    r0, rows, shape = span
    size = 1
    for s in shape:
        size *= s
    sl = lax.slice_in_dim(packed, r0, r0 + rows, axis=len(lead))
    return sl.reshape(lead + (-1,))[..., :size].reshape(lead + tuple(shape))


def _pack_flat(parts):
    flat, spans, at = [], [], 0
    for p in parts:
        size = p.size
        padded = size + (-size) % _LANES
        flat.append(jnp.pad(p.reshape(-1), (0, padded - size)))
        spans.append((at, size, p.shape))
        at += padded
    tail = (-at) % (_PACKED_ROWS * _PACK_W)
    if tail:
        flat.append(jnp.zeros((tail,), flat[0].dtype))
    return jnp.concatenate(flat).reshape(-1, _PACK_W), spans


def _unpack_flat(packed, span, lead=()):
    at, size, shape = span
    return lax.slice_in_dim(packed.reshape(lead + (-1,)), at, at + size, axis=len(lead)).reshape(lead + tuple(shape))


def _rope_tables(positions, T):
    inv_freq = ROPE_THETA ** (-jnp.arange(0, ROPE_DIM, 2, dtype=F32) / ROPE_DIM)
    ang = positions.reshape(T, 1).astype(F32) * inv_freq
    cos, sin = jnp.cos(ang), jnp.sin(ang)
    half = ROPE_DIM // 2
    z = lambda n: jnp.zeros((T, n), F32)
    tail = HEAD_W - QK_DIM
    ct = jnp.concatenate([jnp.ones((T, NOPE_DIM), F32), cos, cos, z(tail)], axis=1)
    sa = jnp.concatenate([z(NOPE_DIM), -sin, z(half), z(tail)], axis=1)
    sb = jnp.concatenate([z(NOPE_DIM), z(half), sin, z(tail)], axis=1)
    return ct, sa, sb


def _pad_heads(wt, per_head, keep_from=0, keep=None):
    K = wt.shape[1]
    keep = per_head if keep is None else keep
    w3 = wt.reshape(N_HEADS, per_head, K)[:, keep_from:keep_from + keep]
    return jnp.pad(w3, ((0, 0), (0, HEAD_W - keep), (0, 0))).reshape(N_HEADS * HEAD_W, K)


def _unpad_heads(g, keep):
    return g.reshape(N_HEADS, HEAD_W, g.shape[1])[:, :keep]


def _layer_weights(W):
    D = W['w_out'].shape[1]
    C = W['w_conv_out'].shape[1]
    QL, KVL = W['w_uq'].shape[1], W['w_ukv'].shape[1]
    o1 = 2 * C
    o2, o3 = o1 + QL, o1 + QL + KVL
    o4 = o3 + ROPE_DIM
    win = W['w_in']
    zr = lambda n: jnp.zeros((n, D), win.dtype)
    L = dict(W)
    L['w_c'] = win[:o1]
    L['w_l'] = jnp.concatenate([win[o1:o3], zr(NOPE_DIM), win[o3:o4], zr(HEAD_W - QK_DIM)], axis=0)
    L['w_g'] = win[o4:]
    L['w_q'] = _pad_heads(W['w_uq'], QK_DIM)
    L['w_k'] = _pad_heads(W['w_ukv'], NOPE_DIM + V_DIM, 0, NOPE_DIM)
    L['w_v'] = _pad_heads(W['w_ukv'], NOPE_DIM + V_DIM, NOPE_DIM, V_DIM)
    wmo = W['w_mla_out'].reshape(D, N_HEADS, V_DIM)
    L['w_mo'] = jnp.pad(wmo, ((0, 0), (0, 0), (0, HEAD_W - V_DIM))).reshape(D, N_HEADS * HEAD_W)
    L['dims'] = (D, C, QL, KVL)
    return L


def _row(v, width=None):
    v = v.reshape(1, -1)
    if width is not None and v.shape[1] < width:
        v = jnp.pad(v, ((0, 0), (0, width - v.shape[1])))
    return v


def _ffn_fwd(x, h, wg, wu, wd, next_g):
    a, b, u = _ffn_up(h, wg, wu)
    if next_g is None:
        x_out, h_out = _mm("ffn_down", u, wd, 'nn', res=x, scale=0.5), None
    else:
        x_out, h_out = _mm_res_norm("ffn_down_norm", u, wd, x, 0.5, _row(next_g))
    return x_out, h_out, (x, h, a, b, u)


def _ffn_bwd_weights(dx, saved, wd, after):
    x, h, a, b, u = saved
    d_wd = _wgrad("ffn_dwd", u, dx, scale=0.5)
    da, db = _ffn_bwd_up(dx, wd, a, b, after)
    return da, db, (_wgrad("ffn_dwg", da, h), _wgrad("ffn_dwg", db, h), d_wd)


def _ffn_bwd_input(dx, saved, g, wg, wu, da, db, after):
    return _dh_rms_bwd("ffn_dh_rms", [(da, wg), (db, wu)], saved[0], _row(g), dx, after)


def _layer_fwd_rest(x1, h, ffn1, L, S, tabs, next_g):
    D, C, QL, KVL = L['dims']
    rope_cb = (QL + KVL) // HEAD_W
    pc = _mm("proj_c", h, L['w_c'], 'nt')
    pl_ = _mm("proj_l", h, L['w_l'], 'nt')
    pg = _mm("proj_g", h, L['w_g'], 'nt')
    conv_w = jnp.pad(S['conv_w'], ((0, HALO - CONV_WIDTH), (0, 0)))
    z, c = _conv_fwd(pc, conv_w, _row(S['conv_b']), _row(S['conv_ln_g']), _row(S['conv_ln_b']))
    yc = _mm("conv_out", c, L['w_conv_out'], 'nt')
    cqn, ckvn = _lat_norm_fwd(pl_, _row(S['cq_norm']), _row(S['ckv_norm']), QL, KVL)
    q_raw = _mm("up_q", cqn, L['w_q'], 'nt')
    k_raw = _mm("up_kv", ckvn, L['w_k'], 'nt')
    v = _mm("up_kv", ckvn, L['w_v'], 'nt', out_dtype=_MXU_DTYPE)
    gq, gk = _row(S['q_norm'], HEAD_W), _row(S['k_norm'], HEAD_W)
    qn, kn = _qk_prep_fwd(q_raw, k_raw, pl_, rope_cb, *tabs, gq, gk)
    o, lse = _attn_fwd(qn, kn, v)
    ym = _mm("mla_out", o, L['w_mo'], 'nt')
    y = _gate_fwd(pg, S['gate_bias'], yc, ym)
    x2, h2 = _mm_res_norm("mix_out_norm", y, L['w_out'], x1, 1.0, _row(S['ffn2_norm']))
    x3, h3, ffn2 = _ffn_fwd(x2, h2, L['ffn2_w_gate'], L['ffn2_w_up'], L['ffn2_w_down'], next_g)
    saved = dict(ffn1=ffn1, ffn2=ffn2, x1=x1, h=h, pc=pc, pl=pl_, pg=pg, z=z, c=c, yc=yc, cqn=cqn, ckvn=ckvn,
                 q_raw=q_raw, k_raw=k_raw, v=v, qn=qn, kn=kn, o=o, lse=lse, ym=ym, y=y, conv_w=conv_w, gq=gq, gk=gk)
    return x3, h3, saved


def _layer_bwd_mix(dx, L, S, tabs, A, after):
    D, C, QL, KVL = L['dims']
    rope_cb = (QL + KVL) // HEAD_W
    G, g = {}, {}
    da, db, (G['ffn2_w_gate'], G['ffn2_w_up'], G['ffn2_w_down']) = _ffn_bwd_weights(dx, A['ffn2'], L['ffn2_w_down'], after)
    dx, g['ffn2_norm'] = _ffn_bwd_input(dx, A['ffn2'], S['ffn2_norm'], L['ffn2_w_gate'], L['ffn2_w_up'], da, db, after)
    G['w_out'] = _wgrad("mix_dwout", A['y'], dx)
    dy = _mm("mix_dy", dx, L['w_out'], 'nt')
    dyc, dym, dpg, g['gate_bias'] = _gate_bwd(A['pg'], S['gate_bias'], A['yc'], A['ym'], dy)
    G['w_conv_out'] = _wgrad("conv_dwout", dyc, A['c'])
    dc = _mm("conv_dc", dyc, L['w_conv_out'], 'nn')
    dpc, dcw, g['conv_b'], g['conv_ln_g'], g['conv_ln_b'] = _conv_bwd(
        A['pc'], A['z'], dc, A['conv_w'], _row(S['conv_ln_g']), _row(S['conv_ln_b']))
    g['conv_w'] = dcw[:CONV_WIDTH]
    d_wmo = _wgrad("mla_dwout", dym, A['o'])
    G['w_mla_out'] = d_wmo.reshape(D, N_HEADS, HEAD_W)[:, :, :V_DIM].reshape(D, N_HEADS * V_DIM)
    do = _mm("mla_do", dym, L['w_mo'], 'nn', out_dtype=_MXU_DTYPE)
    dqn, dkn, dv = _attn_bwd(A['qn'], A['kn'], A['v'], A['o'], do, A['lse'])
    dq_raw, dk_raw, drope, dgq, dgk = _qk_prep_bwd(A['q_raw'], A['k_raw'], A['pl'], rope_cb, *tabs, A['gq'], A['gk'], dqn, dkn)
    g['q_norm'], g['k_norm'] = dgq[:, :QK_DIM], dgk[:, :QK_DIM]
    G['w_uq'] = _unpad_heads(_wgrad("up_dwq", dq_raw, A['cqn']), QK_DIM).reshape(N_HEADS * QK_DIM, QL)
    d_wk = _unpad_heads(_wgrad("up_dwkv", dk_raw, A['ckvn']), NOPE_DIM)
    d_wv = _unpad_heads(_wgrad("up_dwkv", dv, A['ckvn']), V_DIM)
    G['w_ukv'] = jnp.concatenate([d_wk, d_wv], axis=1).reshape(N_HEADS * (NOPE_DIM + V_DIM), KVL)
    dcq = _mm("up_dcq", dq_raw, L['w_q'], 'nn')
    dckv = _mm("up_dckv_k", dk_raw, L['w_k'], 'nn')
    dckv = _mm("up_dckv_v", dv, L['w_v'], 'nn', res=dckv)
    dpl, g['cq_norm'], g['ckv_norm'] = _lat_norm_bwd(A['pl'], _row(S['cq_norm']), _row(S['ckv_norm']), dcq, dckv, drope, QL, KVL)
    d_wc = _wgrad("proj_dwc", dpc, A['h'])
    d_wl = _wgrad("proj_dwl", dpl, A['h'])
    d_wg = _wgrad("proj_dwg", dpg, A['h'])
    ql = QL + KVL
    G['w_in'] = jnp.concatenate([d_wc, d_wl[:ql], d_wl[ql + NOPE_DIM:ql + NOPE_DIM + ROPE_DIM], d_wg], axis=0)
    dx, g['mix_norm'] = _dh_rms_bwd("proj_dh_rms", [(dpc, L['w_c']), (dpl, L['w_l']), (dpg, L['w_g'])],
                                    A['x1'], _row(S['mix_norm']), dx, after)
    return dx, G, g


def kernel(x, positions, ffn1_norm, ffn1_w_gate, ffn1_w_up, ffn1_w_down, mix_norm, w_in, gate_bias, conv_w, conv_b, conv_ln_g, conv_ln_b, w_conv_out, cq_norm, ckv_norm, w_uq, w_ukv, q_norm, k_norm, w_mla_out, w_out, ffn2_norm, ffn2_w_gate, ffn2_w_up, ffn2_w_down, loss_target, m_ffn1_norm, m_ffn1_w_gate, m_ffn1_w_up, m_ffn1_w_down, m_mix_norm, m_w_in, m_gate_bias, m_conv_w, m_conv_b, m_conv_ln_g, m_conv_ln_b, m_w_conv_out, m_cq_norm, m_ckv_norm, m_w_uq, m_w_ukv, m_q_norm, m_k_norm, m_w_mla_out, m_w_out, m_ffn2_norm, m_ffn2_w_gate, m_ffn2_w_up, m_ffn2_w_down, v_ffn1_norm, v_ffn1_w_gate, v_ffn1_w_up, v_ffn1_w_down, v_mix_norm, v_w_in, v_gate_bias, v_conv_w, v_conv_b, v_conv_ln_g, v_conv_ln_b, v_w_conv_out, v_cq_norm, v_ckv_norm, v_w_uq, v_w_ukv, v_q_norm, v_k_norm, v_w_mla_out, v_w_out, v_ffn2_norm, v_ffn2_w_gate, v_ffn2_w_up, v_ffn2_w_down):
    w = dict(zip(WEIGHTS, (ffn1_norm, ffn1_w_gate, ffn1_w_up, ffn1_w_down, mix_norm, w_in, gate_bias, conv_w, conv_b, conv_ln_g, conv_ln_b, w_conv_out, cq_norm, ckv_norm, w_uq, w_ukv, q_norm, k_norm, w_mla_out, w_out, ffn2_norm, ffn2_w_gate, ffn2_w_up, ffn2_w_down)))
    m = dict(zip(WEIGHTS, (m_ffn1_norm, m_ffn1_w_gate, m_ffn1_w_up, m_ffn1_w_down, m_mix_norm, m_w_in, m_gate_bias, m_conv_w, m_conv_b, m_conv_ln_g, m_conv_ln_b, m_w_conv_out, m_cq_norm, m_ckv_norm, m_w_uq, m_w_ukv, m_q_norm, m_k_norm, m_w_mla_out, m_w_out, m_ffn2_norm, m_ffn2_w_gate, m_ffn2_w_up, m_ffn2_w_down)))
    v = dict(zip(WEIGHTS, (v_ffn1_norm, v_ffn1_w_gate, v_ffn1_w_up, v_ffn1_w_down, v_mix_norm, v_w_in, v_gate_bias, v_conv_w, v_conv_b, v_conv_ln_g, v_conv_ln_b, v_w_conv_out, v_cq_norm, v_ckv_norm, v_w_uq, v_w_ukv, v_q_norm, v_k_norm, v_w_mla_out, v_w_out, v_ffn2_norm, v_ffn2_w_gate, v_ffn2_w_up, v_ffn2_w_down)))
    depth = ffn1_norm.shape[0]
    T, D = x.shape[1], x.shape[2]
    xs = x.reshape(T, D)
    target = loss_target.reshape(T, D)
    tabs = _rope_tables(positions, T)
    me = 4 * lax.axis_index("x") + 2 * lax.axis_index("y") + lax.axis_index("c")

    big = list(BIG)
    groups = [[n for n in big if n.startswith('ffn1_')], [n for n in big if not n.startswith('ffn1_')]]
    packed, wspans = {}, {}
    for l in range(depth):
        for gi, names in enumerate(groups):
            packed[l, gi], wspans[gi] = _pack_rows([(w[n][l].T if BIG[n] else w[n][l]).astype(_WIRE_DTYPE) for n in names])
    small_names = [(l, n) for l in range(depth) for n in SMALL_SHARDED]
    small_packed, small_spans = _pack_flat([w[n][l].T for l, n in small_names])
    small_flight = _exchange_start("small_weights_start", small_packed, _own_slot(small_packed, me), False)
    relay = _relay_start("gather0_start", packed[0, 1], _own_slot(packed[0, 1], me))
    gathered = {(0, 0): _all_gather("gather_weights", packed[0, 0])}
    send_sems, recv_sems, src, land, _ = small_flight
    small_gathered = _exchange_wait("small_weights_wait", send_sems, recv_sems, src, land, gathered[0, 0], False)

    def unpack(l, gi):
        W = {}
        for n, span in zip(groups[gi], wspans[gi]):
            sh = _unpack_rows(gathered[l, gi], span, (N_DEV,))
            W[n] = sh.reshape(N_DEV * sh.shape[1], sh.shape[2])
        return W

    def small_weights(l):
        S = {n: w[n][l] for n in REPLICATED}
        for (ll, n), span in zip(small_names, small_spans):
            if ll == l:
                sh = _unpack_flat(small_gathered, span, (N_DEV,))
                S[n] = sh.reshape(N_DEV * sh.shape[1], sh.shape[2]).T
        return S

    layers, acts, in_flight = [], [], {}
    hcur = xs
    hn = _rms_fwd(hcur, _row(w['ffn1_norm'][0]))
    for l in range(depth):
        S = small_weights(l)
        if l > 0:
            send_sems, recv_sems, src, land, _ = in_flight[l, 0]
            gathered[l, 0] = _exchange_wait(f"gather_wait_{l}_0", send_sems, recv_sems, src, land, hcur, False)
        W = unpack(l, 0)
        x1, h, ffn1 = _ffn_fwd(hcur, hn, W['ffn1_w_gate'], W['ffn1_w_up'], W['ffn1_w_down'], S['mix_norm'])
        if l == 0:
            send_a, recv_a, src, land, _ = relay
            src, land, send_b, recv_b = _relay_forward("gather0_relay", send_a, recv_a, src, land, x1)
            gathered[0, 1] = _relay_wait("gather0_wait", send_a, recv_a, send_b, recv_b, src, land)
            for ll in range(1, depth):
                for gi in range(len(groups)):
                    in_flight[ll, gi] = _exchange_start(f"gather_start_{ll}_{gi}", packed[ll, gi], _own_slot(packed[ll, gi], me), False)
        else:
            send_sems, recv_sems, src, land, _ = in_flight[l, 1]
            gathered[l, 1] = _exchange_wait(f"gather_wait_{l}_1", send_sems, recv_sems, src, land, x1, False)
        W.update(unpack(l, 1))
        layers.append((_layer_weights(W), S))
        hcur, hn, saved = _layer_fwd_rest(x1, h, ffn1, *layers[l], tabs, w['ffn1_norm'][l + 1] if l + 1 < depth else None)
        acts.append(saved)
    dx, loss_part = _loss_grad(hcur, target)

    stage_names = {'mix': [n for n in big if not n.startswith('ffn1_')], 'down': ['ffn1_w_down'],
                   'gateup': ['ffn1_w_gate', 'ffn1_w_up']}
    small_grads, gspans, in_flight = [{} for _ in range(depth)], {}, {}

    def send_stage(l, stage, G):
        blocks = [G[n].reshape(N_DEV, -1, G[n].shape[1]) for n in stage_names[stage]]
        send, gspans[stage] = _pack_rows(blocks, (N_DEV,))
        own = lax.dynamic_index_in_dim(send, me, 0, keepdims=False)
        in_flight[l, stage] = _exchange_start(f"scatter_start_{l}_{stage}", send, _own_slot(own, me), True)
        return in_flight[l, stage][4]

    after = jnp.zeros((_SUBLANES, _LANES), F32)
    for l in reversed(range(depth)):
        L, S = layers[l]
        dx, G, g = _layer_bwd_mix(dx, L, S, tabs, acts[l], after)
        small_grads[l].update(g)
        after = send_stage(l, 'mix', G)
        x_in, h, a, b, u = acts[l]['ffn1']
        after = after + send_stage(l, 'down', {'ffn1_w_down': _wgrad("ffn_dwd", u, dx, scale=0.5)})
        da, db = _ffn_bwd_up(dx, L['ffn1_w_down'], a, b, after)
        after = send_stage(l, 'gateup', {'ffn1_w_gate': _wgrad("ffn_dwg", da, h), 'ffn1_w_up': _wgrad("ffn_dwg", db, h)})
        dx, small_grads[l]['ffn1_norm'] = _ffn_bwd_input(dx, acts[l]['ffn1'], S['ffn1_norm'], L['ffn1_w_gate'],
                                                          L['ffn1_w_up'], da, db, after)
    small_list = [(l, n) for l in range(depth) for n in REPLICATED + SMALL_SHARDED]
    loss_part = loss_part + after[:1, :]
    small_send, sspans = _pack_flat([small_grads[l][n] for l, n in small_list] + [loss_part])
    small_flight = _exchange_start("small_grads_start", small_send, _own_slot(small_send, me), False)

    out4 = {}
    after = small_flight[4]
    for (l, stage), (send_sems, recv_sems, src, land, _) in in_flight.items():
        got = _exchange_wait(f"scatter_wait_{l}_{stage}", send_sems, recv_sems, src, land, after, True)
        reduced = _sum_blocks("sum_grads_direct", got)
        for n, span in zip(stage_names[stage], gspans[stage]):
            form = (lambda t: jnp.swapaxes(t, 1, 2)) if BIG[n] else (lambda t: t)
            out4[n] = _adamw_layer(l, form(w[n]), _unpack_rows(reduced, span), form(m[n]), form(v[n]), out4.get(n), after)
            after = out4[n][1]
    for n in big:
        if BIG[n]:
            out4[n] = [jnp.swapaxes(t, 1, 2) for t in out4[n]]
    send_sems, recv_sems, src, land, _ = small_flight
    small_all = _sum_blocks("sum_small", _exchange_wait("small_grads_wait", send_sems, recv_sems, src, land, after, False))
    loss = _unpack_flat(small_all, sspans[-1])[0, 0]
    for n in WEIGHTS:
        if n in BIG:
            continue
        per_layer = []
        for l in range(depth):
            gs = _unpack_flat(small_all, sspans[small_list.index((l, n))])
            if n in SMALL_SHARDED:
                cols = w[n].shape[-1]
                gs = lax.dynamic_slice_in_dim(gs, me * cols, cols, axis=1)
            per_layer.append(gs.reshape(w[n].shape[1:]))
        g = jnp.stack(per_layer, axis=0)
        out4[n] = (g,) + _adamw(w[n], g, m[n], v[n])
    return (loss, dx.reshape(1, T, D), *[out4[n][k] for k in range(4) for n in WEIGHTS])
```
